```python
import jax, jax.numpy as jnp
from jax import lax
import numpy as np

D_MODEL = 1024
BATCH = 8
SEQ = 8192
DEPTH = 2

CHUNK = 64
D_MIX = D_MODEL
D_RET = D_MIX // 2
D_SB = D_MIX - D_RET
RET_HEADS = 4
RET_HEAD_DIM = D_RET // RET_HEADS
SB_HEADS = 8
SB_HEAD_DIM = D_SB // SB_HEADS
Q_BLOCK = 128
ROPE_BASE = 10000.0
EPS = 1e-6
ADA_SCALE = 0.2
SPLIT_SIZES = [D_RET] * 4 + [D_SB] * 4
D_IN = sum(SPLIT_SIZES)

kernel_name = "hybrid_retention_stickbreaking_block"


def rms_norm(x, g):
    xf = x.astype(jnp.float32)
    return xf * lax.rsqrt(jnp.mean(xf * xf, axis=-1, keepdims=True) + EPS) * g.astype(jnp.float32)


def split_heads(x, n_heads):
    b, s, _ = x.shape
    return x.reshape(b, s, n_heads, -1).transpose(0, 2, 1, 3)


def merge_heads(x):
    b, h, s, d = x.shape
    return x.transpose(0, 2, 1, 3).reshape(b, s, h * d)


def rotary(x, pos):
    half = x.shape[-1] // 2
    inv = ROPE_BASE ** (-jnp.arange(half, dtype=jnp.float32) / half)
    ang = pos[:, None] * inv[None, :]
    cos, sin = jnp.cos(ang), jnp.sin(ang)
    x1, x2 = x[..., :half], x[..., half:]
    return jnp.concatenate([x1 * cos - x2 * sin, x1 * sin + x2 * cos], axis=-1)


def retention(q, k, v):
    b, h, s, d = q.shape
    n = s // CHUNK
    log_gamma = jnp.log1p(-(2.0 ** (-5.0 - jnp.arange(h, dtype=jnp.float32))))
    idx = jnp.arange(CHUNK, dtype=jnp.float32)
    dmask = jnp.exp(jnp.abs(idx[:, None] - idx[None, :])[None] * log_gamma[:, None, None])
    q_dec = jnp.exp((idx + 1.0)[None, :] * log_gamma[:, None])
    k_dec = jnp.exp((CHUNK - 1.0 - idx)[None, :] * log_gamma[:, None])
    chunk_dec = jnp.exp(CHUNK * log_gamma)

    k = k * (d ** -0.5)
    qc = q.reshape(b, h, n, CHUNK, d)
    kc = k.reshape(b, h, n, CHUNK, d)
    vc = v.reshape(b, h, n, CHUNK, v.shape[-1])

    scores = jnp.einsum('bhncd,bhnmd->bhncm', qc, kc) * dmask[None, :, None]
    o_inner = jnp.einsum('bhncm,bhnme->bhnce', scores, vc)

    kv = jnp.einsum('bhnmd,bhnme->nbhde', kc * k_dec[None, :, None, :, None], vc)

    def step(state, kv_i):
        return state * chunk_dec[None, :, None, None] + kv_i, state

    init = jnp.zeros((b, h, d, v.shape[-1]), dtype=kv.dtype)
    _, prev = lax.scan(step, init, kv)
    o_cross = jnp.einsum('bhncd,nbhde->bhnce', qc, prev) * q_dec[None, :, None, :, None]

    o = (o_inner + o_cross).reshape(b, h, s, v.shape[-1])
    mu = jnp.mean(o, axis=-1, keepdims=True)
    var = jnp.mean(jnp.square(o - mu), axis=-1, keepdims=True)
    return (o - mu) * lax.rsqrt(var + EPS)


def stick_breaking(q, k, v):
    b, h, s, d = q.shape
    nb = s // Q_BLOCK
    scale = d ** -0.5
    q_blocks = q.reshape(b, h, nb, Q_BLOCK, d).transpose(2, 0, 1, 3, 4)
    key_pos = jnp.arange(s)

    def block(args):
        qi, bi = args
        t = bi * Q_BLOCK + jnp.arange(Q_BLOCK)
        mask = key_pos[None, :] < t[:, None]
        z = jnp.einsum('bhqd,bhsd->bhqs', qi, k) * scale
        log_beta = jax.nn.log_sigmoid(z)
        log_keep = jnp.where(mask, jax.nn.log_sigmoid(-z), 0.0)
        suffix = lax.cumsum(log_keep, axis=3, reverse=True) - log_keep
        a = jnp.where(mask, jnp.exp(log_beta + suffix), 0.0)
        return jnp.einsum('bhqs,bhse->bhqe', a, v)

    out = lax.map(block, (q_blocks, jnp.arange(nb)))
    return out.transpose(1, 2, 0, 3, 4).reshape(b, h, s, v.shape[-1])


def hybrid_layer(x, c_act, norm_g, w_ada, b_ada, w_in, w_out, pos):
    mod = c_act @ w_ada + b_ada
    shift, scale, gate = jnp.split(mod, 3, axis=-1)
    h = rms_norm(x, norm_g) * (1.0 + scale[:, None, :]) + shift[:, None, :]
    proj = h @ w_in
    split_at = [int(i) for i in np.cumsum(SPLIT_SIZES)[:-1]]
    rq, rk, rv, rg, sq, sk, sv, sg = jnp.split(proj, split_at, axis=-1)

    y_ret = retention(rotary(split_heads(rq, RET_HEADS), pos),
                      rotary(split_heads(rk, RET_HEADS), pos),
                      split_heads(rv, RET_HEADS))
    y_ret = merge_heads(y_ret) * jax.nn.silu(rg)

    y_sb = stick_breaking(split_heads(sq, SB_HEADS), split_heads(sk, SB_HEADS),
                          split_heads(sv, SB_HEADS))
    y_sb = merge_heads(y_sb) * jax.nn.silu(sg)

    y = jnp.concatenate([y_ret, y_sb], axis=-1) @ w_out
    return x + gate[:, None, :] * y


def _fwd_setup_inputs(seed: int = 0) -> dict:
    key = jax.random.key(seed)
    ks = jax.random.split(key, 8)
    f32 = jnp.float32
    x = jax.random.normal(ks[0], (BATCH, SEQ, D_MODEL), f32)
    c = jax.random.normal(ks[1], (BATCH, D_MODEL), f32)
    norm_g = 1.0 + 0.02 * jax.random.normal(ks[2], (DEPTH, D_MODEL), f32)
    w_ada = jax.random.normal(ks[3], (DEPTH, D_MODEL, 3 * D_MODEL), f32) * (ADA_SCALE * D_MODEL ** -0.5)
    b_ada = 0.02 * jax.random.normal(ks[4], (DEPTH, 3 * D_MODEL), f32)
    w_in = jax.random.normal(ks[5], (DEPTH, D_MODEL, D_IN), f32) * (D_MODEL ** -0.5)
    w_out = jax.random.normal(ks[6], (DEPTH, D_MIX, D_MODEL), f32) * (D_MIX ** -0.5)
    final_g = 1.0 + 0.02 * jax.random.normal(ks[7], (D_MODEL,), f32)
    return {"x": x, "c": c, "norm_g": norm_g, "w_ada": w_ada, "b_ada": b_ada,
            "w_in": w_in, "w_out": w_out, "final_g": final_g}


def _fwd_reference(x, c, norm_g, w_ada, b_ada, w_in, w_out, final_g):
    pos = jnp.arange(x.shape[1], dtype=jnp.float32)
    c_act = jax.nn.silu(c.astype(jnp.float32))
    h = x.astype(jnp.float32)
    for layer in range(DEPTH):
        h = hybrid_layer(h, c_act, norm_g[layer], w_ada[layer], b_ada[layer],
                         w_in[layer], w_out[layer], pos)
    return rms_norm(h, final_g).astype(x.dtype)


import jax as _jax
import jax.numpy as _jnp

TWIN_FORMAT = 'train_step'
FWD_PARAMS = ['x', 'c', 'norm_g', 'w_ada', 'b_ada', 'w_in', 'w_out', 'final_g']
TWIN_WEIGHTS = ['norm_g', 'w_ada', 'b_ada', 'w_in', 'w_out', 'final_g']
TWIN_DIFF_INPUT = 'x'
TWIN_INPUTS = ['x', 'c', 'norm_g', 'w_ada', 'b_ada', 'w_in', 'w_out', 'final_g', 'loss_target', 'm_norm_g', 'm_w_ada', 'm_b_ada', 'm_w_in', 'm_w_out', 'm_final_g', 'v_norm_g', 'v_w_ada', 'v_b_ada', 'v_w_in', 'v_w_out', 'v_final_g']
TWIN_OUTPUTS = ['loss', 'grad_x', 'grad_norm_g', 'grad_w_ada', 'grad_b_ada', 'grad_w_in', 'grad_w_out', 'grad_final_g', 'delta_norm_g', 'delta_w_ada', 'delta_b_ada', 'delta_w_in', 'delta_w_out', 'delta_final_g', 'new_m_norm_g', 'new_m_w_ada', 'new_m_b_ada', 'new_m_w_in', 'new_m_w_out', 'new_m_final_g', 'new_v_norm_g', 'new_v_w_ada', 'new_v_b_ada', 'new_v_w_in', 'new_v_w_out', 'new_v_final_g']
TWIN_LEAF_KINDS = {'loss': 'loss', 'grad_x': 'grad_x', 'grad_norm_g': 'grad_w', 'grad_w_ada': 'grad_w', 'grad_b_ada': 'grad_w', 'grad_w_in': 'grad_w', 'grad_w_out': 'grad_w', 'grad_final_g': 'grad_w', 'delta_norm_g': 'delta_w', 'delta_w_ada': 'delta_w', 'delta_b_ada': 'delta_w', 'delta_w_in': 'delta_w', 'delta_w_out': 'delta_w', 'delta_final_g': 'delta_w', 'new_m_norm_g': 'new_m', 'new_m_w_ada': 'new_m', 'new_m_b_ada': 'new_m', 'new_m_w_in': 'new_m', 'new_m_w_out': 'new_m', 'new_m_final_g': 'new_m', 'new_v_norm_g': 'new_v', 'new_v_w_ada': 'new_v', 'new_v_b_ada': 'new_v', 'new_v_w_in': 'new_v', 'new_v_w_out': 'new_v', 'new_v_final_g': 'new_v'}


def _forward(args):
    return _fwd_reference(*[args[k] for k in FWD_PARAMS])


def _output_shape():
    def fwd():
        inp = _fwd_setup_inputs(0)
        return _fwd_reference(*[inp[k] for k in FWD_PARAMS])
    out = _jax.eval_shape(fwd)
    return out.shape, out.dtype

N_MICROBATCH = 1
ADAM_LR = 0.001
ADAM_B1 = 0.9
ADAM_B2 = 0.999
ADAM_EPS = 1e-08
ADAM_WD = 0.01
ADAM_STEP = 10
PER_EXAMPLE_BATCH_AXIS = {'x': 0, 'c': 0, 'loss_target': 0}
SHARED_INPUTS = []
_WEIGHT_DTYPES = {'norm_g': _jnp.float32, 'w_ada': _jnp.float32, 'b_ada': _jnp.float32, 'w_in': _jnp.float32, 'w_out': _jnp.float32, 'final_g': _jnp.float32}
MOMENT_SCALE = {'norm_g': 3.183078e-02, 'w_ada': 4.856624e-02, 'b_ada': 8.108708e-02, 'w_in': 1.501518e-02, 'w_out': 1.575674e-02, 'final_g': 6.393724e+01}


def _to_microbatches(a, axis):
    t = _jnp.moveaxis(a, axis, 0)
    t = t.reshape((N_MICROBATCH, t.shape[0] // N_MICROBATCH) + t.shape[1:])
    return _jnp.moveaxis(t, 1, axis + 1)


def setup_inputs(seed: int = 0) -> dict:
    inp = _fwd_setup_inputs(seed)
    key = _jax.random.fold_in(_jax.random.key(seed), 7919)
    shape, _ = _output_shape()
    out = dict(inp)
    out["loss_target"] = _jax.random.normal(_jax.random.fold_in(key, 0), shape, _jnp.float32)
    for i, name in enumerate(TWIN_WEIGHTS):
        w = inp[name].astype(_jnp.float32)
        if MOMENT_SCALE is None:
            s = _jnp.sqrt(_jnp.mean(_jnp.square(w)) + 1e-30)
        else:
            s = MOMENT_SCALE[name]
        km, kv = _jax.random.split(_jax.random.fold_in(key, i + 1))
        out[name] = w
        out["m_" + name] = s * _jax.random.normal(km, w.shape, _jnp.float32)
        out["v_" + name] = (s * s) * _jax.random.uniform(kv, w.shape, _jnp.float32, 0.5, 1.5)
    if N_MICROBATCH > 1:
        for name, axis in PER_EXAMPLE_BATCH_AXIS.items():
            out[name] = _to_microbatches(out[name], axis)
    return {'x': out['x'], 'c': out['c'], 'norm_g': out['norm_g'], 'w_ada': out['w_ada'], 'b_ada': out['b_ada'], 'w_in': out['w_in'], 'w_out': out['w_out'], 'final_g': out['final_g'], 'loss_target': out['loss_target'], 'm_norm_g': out['m_norm_g'], 'm_w_ada': out['m_w_ada'], 'm_b_ada': out['m_b_ada'], 'm_w_in': out['m_w_in'], 'm_w_out': out['m_w_out'], 'm_final_g': out['m_final_g'], 'v_norm_g': out['v_norm_g'], 'v_w_ada': out['v_w_ada'], 'v_b_ada': out['v_b_ada'], 'v_w_in': out['v_w_in'], 'v_w_out': out['v_w_out'], 'v_final_g': out['v_final_g']}


def _loss(weights, diff, rest, loss_target):
    with _jax.named_scope("forward"):
        args = {**rest, TWIN_DIFF_INPUT: diff, **{k: w.astype(_WEIGHT_DTYPES[k]) for k, w in weights.items()}}
        y = _forward(args)
    with _jax.named_scope("loss_head"):
        err = _jnp.square(y.astype(_jnp.float32) - loss_target)
        return 0.5 * _jnp.sum(_jnp.mean(err, axis=-1)) if err.ndim else 0.5 * err


def _adamw(w, g, m, v):
    m = ADAM_B1 * m + (1.0 - ADAM_B1) * g
    v = ADAM_B2 * v + (1.0 - ADAM_B2) * _jnp.square(g)
    m_hat = m / (1.0 - ADAM_B1 ** ADAM_STEP)
    v_hat = v / (1.0 - ADAM_B2 ** ADAM_STEP)
    delta = -ADAM_LR * (m_hat / (_jnp.sqrt(v_hat) + ADAM_EPS) + ADAM_WD * w)
    return delta, m, v


def reference(x, c, norm_g, w_ada, b_ada, w_in, w_out, final_g, loss_target, m_norm_g, m_w_ada, m_b_ada, m_w_in, m_w_out, m_final_g, v_norm_g, v_w_ada, v_b_ada, v_w_in, v_w_out, v_final_g):
    given = dict(x=x, c=c, norm_g=norm_g, w_ada=w_ada, b_ada=b_ada, w_in=w_in, w_out=w_out, final_g=final_g, loss_target=loss_target, m_norm_g=m_norm_g, m_w_ada=m_w_ada, m_b_ada=m_b_ada, m_w_in=m_w_in, m_w_out=m_w_out, m_final_g=m_final_g, v_norm_g=v_norm_g, v_w_ada=v_w_ada, v_b_ada=v_b_ada, v_w_in=v_w_in, v_w_out=v_w_out, v_final_g=v_final_g)
    weights = {n: given[n] for n in TWIN_WEIGHTS}
    shared = {n: given[n] for n in SHARED_INPUTS}
    per_example = {n: given[n] for n in ['x', 'c']}
    grad_fn = _jax.value_and_grad(_loss, argnums=(0, 1))

    def one_microbatch(ex, loss_target):
        ex = dict(ex)
        diff = ex.pop(TWIN_DIFF_INPUT)
        return grad_fn(weights, diff, {**shared, **ex}, loss_target)

    if N_MICROBATCH == 1:
        loss, (grad_w, grad_x) = one_microbatch(per_example, given["loss_target"])
    else:
        def body(carry, xs):
            loss_sum, grad_sum = carry
            l_k, (gw_k, gx_k) = one_microbatch(xs[0], xs[1])
            with _jax.named_scope("update"):
                return (loss_sum + l_k, _jax.tree.map(_jnp.add, grad_sum, gw_k)), gx_k

        init = (_jnp.zeros((), _jnp.float32), _jax.tree.map(_jnp.zeros_like, weights))
        (loss, grad_w), grad_x = _jax.lax.scan(body, init, (per_example, given["loss_target"]))
    with _jax.named_scope("update"):
        delta_w, new_m, new_v = {}, {}, {}
        for n in TWIN_WEIGHTS:
            delta_w[n], new_m[n], new_v[n] = _adamw(weights[n], grad_w[n], given["m_" + n], given["v_" + n])
    return (loss, grad_x, *[grad_w[n] for n in TWIN_WEIGHTS], *[delta_w[n] for n in TWIN_WEIGHTS],
            *[new_m[n] for n in TWIN_WEIGHTS], *[new_v[n] for n in TWIN_WEIGHTS])
```

```python
import functools

import numpy as np
import jax
import jax.numpy as jnp
from jax import lax
from jax.experimental import pallas as pl
from jax.experimental.pallas import tpu as pltpu

F32, BF16 = jnp.float32, jnp.bfloat16
MESH = pl.DeviceIdType.MESH

D_MODEL = 1024
DEPTH = 2
SHARD_W = 1024
N_SHARD = 4
GROUP_W = 512
LANES = 128
SB_HEAD_DIM = 64
RET_HEAD_DIM = 128
CHUNK = 64
ROPE_BASE = 10000.0
EPS = 1e-6
SQ_SCALE = SB_HEAD_DIM ** -0.5
RK_SCALE = RET_HEAD_DIM ** -0.5
SB_T = 128
RET_T = 256
EXP_ZERO = -104.0
VMEM_LIMIT_BYTES = 56 * 2 ** 20

ADAM_LR, ADAM_B1, ADAM_B2, ADAM_EPS, ADAM_WD, ADAM_STEP = 0.001, 0.9, 0.999, 1e-08, 0.01, 10


def _cp(*sem):
    return pltpu.CompilerParams(dimension_semantics=sem, vmem_limit_bytes=VMEM_LIMIT_BYTES)


def _dot(a, b):
    return lax.dot_general(a, b, (((1,), (0,)), ((), ())), preferred_element_type=F32)


def _dot_nt(a, b):
    return lax.dot_general(a, b, (((1,), (1,)), ((), ())), preferred_element_type=F32)


def _dot_tn(a, b):
    return lax.dot_general(a, b, (((0,), (0,)), ((), ())), preferred_element_type=F32)


def _dot_hilo(a, tri):
    hi = a.astype(BF16)
    lo = (a - hi.astype(F32)).astype(BF16)
    return _dot(hi, tri) + _dot(lo, tri)


def _sigmoid(x):
    return 1.0 / (1.0 + jnp.exp(-x))


def _rowsum(a):
    return jnp.sum(a, axis=1, keepdims=True)


def _rowmean(a):
    return jnp.mean(a, axis=1, keepdims=True)


def inproj_fwd(x, vecs, w3, tm=256):
    S, D = x.shape

    def body(x_ref, v_ref, w_ref, proj_ref, h_ref, sb_ref):
        xv = x_ref[...]
        r = lax.rsqrt(_rowmean(xv * xv) + EPS)
        h = xv * r * v_ref[3:4, :] * (1.0 + v_ref[1:2, :]) + v_ref[0:1, :]
        hb = h.astype(BF16)
        h_ref[...] = hb
        for s in range(N_SHARD):
            p = _dot(hb, w_ref[s])
            proj_ref[:, s * SHARD_W:(s + 1) * SHARD_W] = p
            if s == 2:
                sb_ref[:, 0:GROUP_W] = (p[:, 0:GROUP_W] * SQ_SCALE).astype(BF16)
                sb_ref[:, GROUP_W:SHARD_W] = p[:, GROUP_W:].astype(BF16)
            if s == 3:
                sb_ref[:, SHARD_W:2 * SHARD_W] = p.astype(BF16)

    return pl.pallas_call(
        body, name="inproj_fwd", grid=(S // tm,),
        in_specs=[pl.BlockSpec((tm, D), lambda i: (i, 0)),
                  pl.BlockSpec((8, D), lambda i: (0, 0)),
                  pl.BlockSpec((N_SHARD, D, SHARD_W), lambda i: (0, 0, 0))],
        out_specs=[pl.BlockSpec((tm, 4 * D), lambda i: (i, 0)),
                   pl.BlockSpec((tm, D), lambda i: (i, 0)),
                   pl.BlockSpec((tm, 2 * SHARD_W), lambda i: (i, 0))],
        out_shape=[jax.ShapeDtypeStruct((S, 4 * D), F32),
                   jax.ShapeDtypeStruct((S, D), BF16),
                   jax.ShapeDtypeStruct((S, 2 * SHARD_W), BF16)],
        compiler_params=_cp("arbitrary"),
    )(x, vecs, w3)


def _sb_logits(qh, k2, mask):
    z = _dot_nt(qh, k2)
    sp = jnp.log1p(jnp.exp(-jnp.abs(z)))
    lb = jnp.minimum(z, 0.0) - sp
    lk = lb - z
    if mask is not None:
        lk = jnp.where(mask, lk, 0.0)
    return lb, lk


def _sb_masks():
    T = SB_T
    row = lax.broadcasted_iota(jnp.int32, (T, T), 0)
    col = lax.broadcasted_iota(jnp.int32, (T, T), 1)
    causal = col < row
    later = jnp.where(row > col, 1.0, 0.0).astype(BF16)
    earlier = jnp.where(row < col, 1.0, 0.0).astype(BF16)
    lane = lax.broadcasted_iota(jnp.int32, (1, LANES), 1)
    return causal, later, earlier, lane < SB_HEAD_DIM


def _sb_rows(ref, j):
    return ref[pl.ds(pl.multiple_of(j * SB_T, SB_T), SB_T), :]


def sb_fwd(sb, proj):
    S = sb.shape[0]
    T = SB_T
    nq = S // T

    def body(q_ref, k_ref, v_ref, sg_ref, y_ref, o_ref):
        i = pl.program_id(1)
        causal, later, _, head0 = _sb_masks()
        q2 = q_ref[...]
        outs = []
        for hh in range(2):
            hm = head0 if hh == 0 else jnp.logical_not(head0)
            qh = jnp.where(hm, q2, jnp.zeros_like(q2))

            def tile(j, R, mask):
                lb, lk = _sb_logits(qh, _sb_rows(k_ref, j), mask)
                a = jnp.exp(lb + _dot_hilo(lk, later) + R)
                if mask is not None:
                    a = jnp.where(mask, a, 0.0)
                return _dot(a.astype(BF16), _sb_rows(v_ref, j)), R + _rowsum(lk)

            acc0, R0 = tile(i, jnp.zeros((T, 1), F32), causal)

            def cond(st):
                return jnp.logical_and(st[0] >= 0, st[3] > EXP_ZERO)

            def step(st):
                j, acc, R, _ = st
                c, Rn = tile(j, R, None)
                return j - 1, acc + c, Rn, jnp.max(Rn)

            outs.append(lax.while_loop(cond, step, (i - 1, acc0, R0, jnp.max(R0)))[1])
        o = jnp.where(head0, outs[0], outs[1])
        o_ref[...] = o
        sg = sg_ref[...]
        y_ref[...] = (o * (sg * _sigmoid(sg))).astype(BF16)

    return pl.pallas_call(
        body, name="sb_fwd", grid=(4, nq),
        in_specs=[pl.BlockSpec((T, LANES), lambda p, i: (i, p)),
                  pl.BlockSpec((S, LANES), lambda p, i: (0, 4 + p)),
                  pl.BlockSpec((S, LANES), lambda p, i: (0, 8 + p)),
                  pl.BlockSpec((T, LANES), lambda p, i: (i, 28 + p))],
        out_specs=[pl.BlockSpec((T, LANES), lambda p, i: (i, p)),
                   pl.BlockSpec((T, LANES), lambda p, i: (i, p))],
        out_shape=[jax.ShapeDtypeStruct((S, GROUP_W), BF16),
                   jax.ShapeDtypeStruct((S, GROUP_W), F32)],
        compiler_params=_cp("arbitrary", "arbitrary"),
    )(sb, sb, sb, proj)


def sb_bwd(sb, proj, o, dycat):
    S = sb.shape[0]
    T = SB_T
    nq = S // T

    def body(q_ref, k_ref, v_ref, sg_ref, o_ref, dy_ref, dq_ref, dk_ref, dv_ref, dsg_ref,
             dk_acc, dv_acc, stick):
        i = pl.program_id(1)

        @pl.when(i == 0)
        def _():
            dk_acc[...] = jnp.zeros_like(dk_acc)
            dv_acc[...] = jnp.zeros_like(dv_acc)

        causal, later, earlier, head0 = _sb_masks()
        q2 = q_ref[...]
        sg = sg_ref[...]
        sig = _sigmoid(sg)
        dy = dy_ref[...]
        do = dy * (sg * sig)
        dsg_ref[...] = (dy * o_ref[...] * (sig * (1.0 + sg * (1.0 - sig)))).astype(BF16)
        do_b = do.astype(BF16)
        dqs = []
        for hh in range(2):
            hm = head0 if hh == 0 else jnp.logical_not(head0)
            qh = jnp.where(hm, q2, jnp.zeros_like(q2))
            doh = jnp.where(hm, do_b, jnp.zeros_like(do_b))

            stick[i] = jnp.zeros((T, 1), F32)
            R0 = _rowsum(_sb_logits(qh, _sb_rows(k_ref, i), causal)[1])

            def cond(st):
                return jnp.logical_and(st[0] >= 0, st[2] > EXP_ZERO)

            def step(st):
                j, R, _ = st
                stick[j] = R
                Rn = R + _rowsum(_sb_logits(qh, _sb_rows(k_ref, j), None)[1])
                return j - 1, Rn, jnp.max(Rn)

            j_end = lax.while_loop(cond, step, (i - 1, R0, jnp.max(R0)))[0]

            def tile(j, G0, mask):
                k2 = _sb_rows(k_ref, j)
                v2 = _sb_rows(v_ref, j)
                lb, lk = _sb_logits(qh, k2, mask)
                a = jnp.exp(lb + _dot_hilo(lk, later) + stick[j])
                if mask is not None:
                    a = jnp.where(mask, a, 0.0)
                g = a * _dot_nt(doh, v2)
                G = _dot_hilo(g, earlier) + G0
                dz = g - jnp.exp(lb) * (g + G)
                if mask is not None:
                    dz = jnp.where(mask, dz, 0.0)
                dzb = dz.astype(BF16)
                rows = pl.ds(pl.multiple_of(j * T, T), T)
                dk_acc[rows, :] += _dot_tn(dzb, qh)
                dv_acc[rows, :] += _dot_tn(a.astype(BF16), doh)
                return _dot(dzb, k2), G0 + _rowsum(g)

            def sweep(j, st):
                c, Gn = tile(j, st[1], None)
                return st[0] + c, Gn

            dq, G0 = lax.fori_loop(j_end + 1, i, sweep, (jnp.zeros((T, LANES), F32), jnp.zeros((T, 1), F32)))
            dqs.append(dq + tile(i, G0, causal)[0])
        dq_ref[...] = (jnp.where(head0, dqs[0], dqs[1]) * SQ_SCALE).astype(BF16)

        @pl.when(i == nq - 1)
        def _():
            dk_ref[...] = dk_acc[...].astype(BF16)
            dv_ref[...] = dv_acc[...].astype(BF16)

    tile_spec = lambda c0: pl.BlockSpec((T, LANES), lambda p, i: (i, c0 + p))
    head_spec = lambda c0: pl.BlockSpec((S, LANES), lambda p, i: (0, c0 + p))
    return pl.pallas_call(
        body, name="sb_bwd", grid=(4, nq),
        in_specs=[tile_spec(0), head_spec(4), head_spec(8), tile_spec(28), tile_spec(0), tile_spec(4)],
        out_specs=[tile_spec(0), head_spec(0), head_spec(0), tile_spec(0)],
        out_shape=[jax.ShapeDtypeStruct((S, GROUP_W), BF16)] * 4,
        scratch_shapes=[pltpu.VMEM((S, LANES), F32), pltpu.VMEM((S, LANES), F32),
                        pltpu.VMEM((nq, T, 1), F32)],
        compiler_params=_cp("arbitrary", "arbitrary"),
    )(sb, sb, sb, proj, o, dycat)


def rope_tables(S):
    half = RET_HEAD_DIM // 2
    inv = ROPE_BASE ** (-jnp.arange(half, dtype=F32) / half)
    ang = jnp.arange(S, dtype=F32)[:, None] * inv[None, :]
    cos, sin = jnp.cos(ang), jnp.sin(ang)
    return jnp.concatenate([cos, cos], axis=1), jnp.concatenate([-sin, sin], axis=1)


def ret_log_gamma():
    return jnp.log1p(-(2.0 ** (-5.0 - jnp.arange(4, dtype=F32))))


def _swap_halves(a):
    return pltpu.roll(a, RET_HEAD_DIM // 2, axis=1)


def _ret_decay_mask(lg):
    n = lax.broadcasted_iota(jnp.int32, (RET_T, RET_T), 0)
    m = lax.broadcasted_iota(jnp.int32, (RET_T, RET_T), 1)
    dist = jnp.abs(n - m).astype(F32)
    return jnp.where((m // CHUNK) <= (n // CHUNK), jnp.exp(lg * dist), 0.0)


def _ret_block(lg, rq_ref, rk_ref, rv_ref, cos_ref, sin_ref, dm_ref):
    cosf, sinf = cos_ref[...], sin_ref[...]
    rq, rk = rq_ref[...], rk_ref[...]
    q = rq * cosf + _swap_halves(rq) * sinf
    k = (rk * cosf + _swap_halves(rk) * sinf) * RK_SCALE
    qb, kb, vb = q.astype(BF16), k.astype(BF16), rv_ref[...].astype(BF16)
    sc = _dot_nt(qb, kb) * dm_ref[...]
    nloc = lax.broadcasted_iota(jnp.int32, (RET_T, 1), 0).astype(F32)
    qdec = jnp.exp(lg * (nloc + 1.0))
    kdec = jnp.exp(lg * (RET_T - 1.0 - nloc))
    block_dec = jnp.exp(jnp.full((1, LANES), lg * RET_T, F32))
    return q, k, qb, kb, vb, sc, qdec, kdec, block_dec, cosf, sinf


def _ret_specs(S):
    nb = S // RET_T
    return nb, (lambda c0, rb: pl.BlockSpec((RET_T, LANES), lambda h, b: (rb(b), c0 + h)))


def ret_fwd(proj, cosf, sinf, lgam):
    S = proj.shape[0]
    nb, spec = _ret_specs(S)
    ident = lambda b: b

    def body(lg_ref, rq_ref, rk_ref, rv_ref, rg_ref, cos_ref, sin_ref, y_ref, o_ref, st_out, st_ref, dm_ref):
        lg = lg_ref[pl.program_id(0)]

        @pl.when(pl.program_id(1) == 0)
        def _():
            st_ref[...] = jnp.zeros_like(st_ref)
            dm_ref[...] = _ret_decay_mask(lg)

        q, k, qb, kb, vb, sc, qdec, kdec, block_dec, _, _ = _ret_block(lg, rq_ref, rk_ref, rv_ref, cos_ref, sin_ref, dm_ref)
        st = st_ref[...]
        st_out[0, 0] = st
        o = _dot(sc.astype(BF16), vb) + _dot(qb, st.astype(BF16)) * qdec
        st_ref[...] = st * block_dec + _dot_tn((k * kdec).astype(BF16), vb)
        o_ref[...] = o
        cen = o - _rowmean(o)
        on = cen * lax.rsqrt(_rowmean(cen * cen) + EPS)
        rg = rg_ref[...]
        y_ref[...] = (on * (rg * _sigmoid(rg))).astype(BF16)

    row_tab = pl.BlockSpec((RET_T, LANES), lambda h, b: (b, 0))
    return pl.pallas_call(
        body, name="ret_fwd", grid=(4, nb),
        in_specs=[pl.BlockSpec(memory_space=pltpu.SMEM),
                  spec(0, ident), spec(4, ident), spec(8, ident), spec(12, ident), row_tab, row_tab],
        out_specs=[spec(0, ident), spec(0, ident),
                   pl.BlockSpec((1, 1, LANES, LANES), lambda h, b: (h, b, 0, 0))],
        out_shape=[jax.ShapeDtypeStruct((S, GROUP_W), BF16),
                   jax.ShapeDtypeStruct((S, GROUP_W), F32),
                   jax.ShapeDtypeStruct((4, nb, LANES, LANES), F32)],
        scratch_shapes=[pltpu.VMEM((LANES, LANES), F32), pltpu.VMEM((RET_T, RET_T), F32)],
        compiler_params=_cp("arbitrary", "arbitrary"),
    )(lgam, proj, proj, proj, proj, cosf, sinf)


def ret_bwd(proj, cosf, sinf, lgam, o, states, dycat):
    S = proj.shape[0]
    nb, spec = _ret_specs(S)
    rev = lambda b: nb - 1 - b

    def body(lg_ref, rq_ref, rk_ref, rv_ref, rg_ref, cos_ref, sin_ref, o_ref, st_in, dy_ref,
             drq_ref, drk_ref, drv_ref, drg_ref, ds_ref, dm_ref):
        lg = lg_ref[pl.program_id(0)]

        @pl.when(pl.program_id(1) == 0)
        def _():
            ds_ref[...] = jnp.zeros_like(ds_ref)
            dm_ref[...] = _ret_decay_mask(lg)

        q, k, qb, kb, vb, sc, qdec, kdec, block_dec, cosf_v, sinf_v = _ret_block(
            lg, rq_ref, rk_ref, rv_ref, cos_ref, sin_ref, dm_ref)
        o_v = o_ref[...]
        cen = o_v - _rowmean(o_v)
        rstd = lax.rsqrt(_rowmean(cen * cen) + EPS)
        on = cen * rstd
        rg = rg_ref[...]
        sig = _sigmoid(rg)
        dy = dy_ref[...]
        drg_ref[...] = (dy * on * (sig * (1.0 + rg * (1.0 - sig)))).astype(BF16)
        don = dy * (rg * sig)
        do = rstd * (don - _rowmean(don) - on * _rowmean(don * on))
        dob = do.astype(BF16)
        dsc = (_dot_nt(dob, vb) * dm_ref[...]).astype(BF16)
        st_b = st_in[0, 0].astype(BF16)
        dsn = ds_ref[...]
        dsn_b = dsn.astype(BF16)
        dq = _dot(dsc, kb) + _dot_nt(dob, st_b) * qdec
        dk = _dot_tn(dsc, qb) + _dot_nt(vb, dsn_b) * kdec
        dv = _dot_tn(sc.astype(BF16), dob) + _dot((k * kdec).astype(BF16), dsn_b)
        ds_ref[...] = dsn * block_dec + _dot_tn((q * qdec).astype(BF16), dob)
        dk = dk * RK_SCALE
        drq_ref[...] = (dq * cosf_v + _swap_halves(dq * sinf_v)).astype(BF16)
        drk_ref[...] = (dk * cosf_v + _swap_halves(dk * sinf_v)).astype(BF16)
        drv_ref[...] = dv.astype(BF16)

    row_tab = pl.BlockSpec((RET_T, LANES), lambda h, b: (rev(b), 0))
    return pl.pallas_call(
        body, name="ret_bwd", grid=(4, nb),
        in_specs=[pl.BlockSpec(memory_space=pltpu.SMEM),
                  spec(0, rev), spec(4, rev), spec(8, rev), spec(12, rev), row_tab, row_tab,
                  spec(0, rev), pl.BlockSpec((1, 1, LANES, LANES), lambda h, b: (h, rev(b), 0, 0)),
                  spec(0, rev)],
        out_specs=[spec(0, rev)] * 4,
        out_shape=[jax.ShapeDtypeStruct((S, GROUP_W), BF16)] * 4,
        scratch_shapes=[pltpu.VMEM((LANES, LANES), F32), pltpu.VMEM((RET_T, RET_T), F32)],
        compiler_params=_cp("arbitrary", "arbitrary"),
    )(lgam, proj, proj, proj, proj, cosf, sinf, o, states, dycat)


def outproj_fwd(x, vecs, y_ret, y_sb, w_out, tm=512):
    S, D = x.shape

    def body(x_ref, v_ref, yr_ref, ys_ref, w_ref, y_ref, xo_ref):
        y = _dot(yr_ref[...], w_ref[0:GROUP_W, :]) + _dot(ys_ref[...], w_ref[GROUP_W:, :])
        y_ref[...] = y
        xo_ref[...] = x_ref[...] + v_ref[2:3, :] * y

    row = lambda w: pl.BlockSpec((tm, w), lambda i: (i, 0))
    return pl.pallas_call(
        body, name="outproj_fwd", grid=(S // tm,),
        in_specs=[row(D), pl.BlockSpec((8, D), lambda i: (0, 0)), row(GROUP_W), row(GROUP_W),
                  pl.BlockSpec((D, D), lambda i: (0, 0))],
        out_specs=[row(D), row(D)],
        out_shape=[jax.ShapeDtypeStruct((S, D), F32)] * 2,
        compiler_params=_cp("arbitrary"),
    )(x, vecs, y_ret, y_sb, w_out)


def loss_head(x, final_g, target, tm=512):
    S, D = x.shape

    def body(x_ref, g_ref, t_ref, dx_ref, st_ref):
        @pl.when(pl.program_id(0) == 0)
        def _():
            st_ref[...] = jnp.zeros_like(st_ref)

        xv = x_ref[...]
        g = g_ref[0:1, :]
        r = lax.rsqrt(_rowmean(xv * xv) + EPS)
        xn = xv * r
        err = xn * g - t_ref[...]
        dy = err * (1.0 / D)
        dxn = dy * g
        dx_ref[...] = r * (dxn - xn * _rowmean(dxn * xn))
        st_ref[0:1, :] += jnp.sum(dy * xn, axis=0, keepdims=True)
        st_ref[1:2, :] += jnp.sum(err * err, axis=0, keepdims=True)

    row = pl.BlockSpec((tm, D), lambda i: (i, 0))
    fixed = pl.BlockSpec((8, D), lambda i: (0, 0))
    return pl.pallas_call(
        body, name="loss_head", grid=(S // tm,),
        in_specs=[row, fixed, row], out_specs=[row, fixed],
        out_shape=[jax.ShapeDtypeStruct((S, D), F32), jax.ShapeDtypeStruct((8, D), F32)],
        compiler_params=_cp("arbitrary"),
    )(x, final_g, target)


def outproj_bwd(dx, y, vecs, y_ret, y_sb, w_out, tm=512):
    S, D = dx.shape
    n = S // tm

    def body(dx_ref, y_ref, v_ref, yr_ref, ys_ref, w_ref, dyc_ref, dw_ref, st_ref, acc):
        i = pl.program_id(0)

        @pl.when(i == 0)
        def _():
            st_ref[...] = jnp.zeros_like(st_ref)
            acc[...] = jnp.zeros_like(acc)

        dxv = dx_ref[...]
        st_ref[0:1, :] += jnp.sum(dxv * y_ref[...], axis=0, keepdims=True)
        dyy = (dxv * v_ref[2:3, :]).astype(BF16)
        dyc_ref[...] = _dot_nt(dyy, w_ref[...])
        acc[0:GROUP_W, :] += _dot_tn(yr_ref[...], dyy)
        acc[GROUP_W:, :] += _dot_tn(ys_ref[...], dyy)

        @pl.when(i == n - 1)
        def _():
            dw_ref[...] = acc[...].astype(BF16)

    row = lambda w: pl.BlockSpec((tm, w), lambda i: (i, 0))
    fixed = lambda r: pl.BlockSpec((r, D), lambda i: (0, 0))
    return pl.pallas_call(
        body, name="outproj_bwd", grid=(n,),
        in_specs=[row(D), row(D), fixed(8), row(GROUP_W), row(GROUP_W), fixed(D)],
        out_specs=[row(D), fixed(D), fixed(8)],
        out_shape=[jax.ShapeDtypeStruct((S, D), F32), jax.ShapeDtypeStruct((D, D), BF16),
                   jax.ShapeDtypeStruct((8, D), F32)],
        scratch_shapes=[pltpu.VMEM((D, D), F32)],
        compiler_params=_cp("arbitrary"),
    )(dx, y, vecs, y_ret, y_sb, w_out)


def inproj_bwd_x(pieces, w3, x, vecs, dx_res, tm=256):
    S, D = x.shape

    def body(*refs):
        p_refs, (w_ref, x_ref, v_ref, dr_ref, dx_ref, st_ref) = refs[:8], refs[8:]

        @pl.when(pl.program_id(0) == 0)
        def _():
            st_ref[...] = jnp.zeros_like(st_ref)

        dh = jnp.zeros((tm, D), F32)
        for k, p_ref in enumerate(p_refs):
            c0 = (k % 2) * GROUP_W
            dh = dh + _dot_nt(p_ref[...], w_ref[k // 2, :, c0:c0 + GROUP_W])
        xv = x_ref[...]
        r = lax.rsqrt(_rowmean(xv * xv) + EPS)
        xn = xv * r
        g, scale1 = v_ref[3:4, :], 1.0 + v_ref[1:2, :]
        st_ref[0:1, :] += jnp.sum(dh, axis=0, keepdims=True)
        dh_xn = dh * xn
        st_ref[1:2, :] += jnp.sum(dh_xn, axis=0, keepdims=True) * g
        st_ref[2:3, :] += jnp.sum(dh_xn, axis=0, keepdims=True) * scale1
        dxn = dh * (g * scale1)
        dx_ref[...] = r * (dxn - xn * _rowmean(dxn * xn)) + dr_ref[...]

    row = lambda w: pl.BlockSpec((tm, w), lambda i: (i, 0))
    return pl.pallas_call(
        body, name="inproj_bwd_x", grid=(S // tm,),
        in_specs=[row(GROUP_W)] * 8 + [pl.BlockSpec((N_SHARD, D, SHARD_W), lambda i: (0, 0, 0)),
                                       row(D), pl.BlockSpec((8, D), lambda i: (0, 0)), row(D)],
        out_specs=[row(D), pl.BlockSpec((8, D), lambda i: (0, 0))],
        out_shape=[jax.ShapeDtypeStruct((S, D), F32), jax.ShapeDtypeStruct((8, D), F32)],
        compiler_params=_cp("arbitrary"),
    )(*pieces, w3, x, vecs, dx_res)


def inproj_bwd_w(h, pieces, tm=512):
    S, D = h.shape
    n = S // tm

    def body(*refs):
        h_ref, p_refs, dw_ref, acc = refs[0], refs[1:9], refs[9], refs[10]
        i = pl.program_id(0)

        @pl.when(i == 0)
        def _():
            acc[...] = jnp.zeros_like(acc)

        hv = h_ref[...]
        for k, p_ref in enumerate(p_refs):
            c0 = (k % 2) * GROUP_W
            acc[k // 2, :, c0:c0 + GROUP_W] += _dot_tn(hv, p_ref[...])

        @pl.when(i == n - 1)
        def _():
            dw_ref[...] = acc[...].astype(BF16)

    row = lambda w: pl.BlockSpec((tm, w), lambda i: (i, 0))
    return pl.pallas_call(
        body, name="inproj_bwd_w", grid=(n,),
        in_specs=[row(D)] + [row(GROUP_W)] * 8,
        out_specs=pl.BlockSpec((N_SHARD, D, SHARD_W), lambda i: (0, 0, 0)),
        out_shape=jax.ShapeDtypeStruct((N_SHARD, D, SHARD_W), BF16),
        scratch_shapes=[pltpu.VMEM((N_SHARD, D, SHARD_W), F32)],
        compiler_params=_cp("arbitrary"),
    )(h, *pieces)


def layer_fwd(x, vecs, w3, w_out, tabs):
    cosf, sinf, lgam = tabs
    proj, h, sb = inproj_fwd(x, vecs, w3)
    y_ret, o_ret, states = ret_fwd(proj, cosf, sinf, lgam)
    y_sb, o_sb = sb_fwd(sb, proj)
    y, x_next = outproj_fwd(x, vecs, y_ret, y_sb, w_out)
    return x_next, (x, proj, h, sb, y_ret, o_ret, states, y_sb, o_sb, y)


def layer_bwd(dx, saved, vecs, w3, w_out, tabs):
    cosf, sinf, lgam = tabs
    x, proj, h, sb, y_ret, o_ret, states, y_sb, o_sb, y = saved
    dycat, dw_out, st_o = outproj_bwd(dx, y, vecs, y_ret, y_sb, w_out)
    d_sb = sb_bwd(sb, proj, o_sb, dycat)
    d_ret = ret_bwd(proj, cosf, sinf, lgam, o_ret, states, dycat)
    pieces = list(d_ret) + list(d_sb)
    dx, st_i = inproj_bwd_x(pieces, w3, x, vecs, dx)
    dw_in = inproj_bwd_w(h, pieces)
    dmod = jnp.concatenate([st_i[0:2], st_o[0:1]], axis=0)
    return dx, dw_in, dw_out, dmod, st_i[2:3]


def _place():
    return lax.axis_index("x"), lax.axis_index("y"), lax.axis_index("c")


def _other_chips(mx, my):
    return [(1 - mx, my), (mx, 1 - my), (1 - mx, 1 - my)]


_ANY = pl.BlockSpec(memory_space=pl.ANY)


def allgather8(x, name):
    def body(x_ref, out_ref, send_sems, recv_sems, local_sem):
        mx, my, mc = _place()
        me, sibling = (mx, my, mc), (mx, my, 1 - mc)
        chips = _other_chips(mx, my)

        def slot(px, py, pc):
            return out_ref.at[4 * px + 2 * py + pc]

        def copy(k, block, to, src=None):
            return pltpu.make_async_remote_copy(
                src_ref=slot(*block) if src is None else src, dst_ref=slot(*block),
                send_sem=send_sems.at[k], recv_sem=recv_sems.at[k], device_id=to, device_id_type=MESH)

        mine = pltpu.make_async_copy(x_ref, slot(*me), local_sem)
        mine.start()
        first = [copy(0, me, sibling, src=x_ref)]
        first += [copy(1 + j, me, (*chip, mc), src=x_ref) for j, chip in enumerate(chips)]
        for cp in first:
            cp.start()
        passed = [copy(4 + j, (*chip, mc), sibling) for j, chip in enumerate(chips)]
        for j, chip in enumerate(chips):
            copy(1 + j, (*chip, mc), me).wait_recv()
            passed[j].start()
        copy(0, sibling, me).wait_recv()
        for j, chip in enumerate(chips):
            copy(4 + j, (*chip, 1 - mc), me).wait_recv()
        for cp in first + passed:
            cp.wait_send()
        mine.wait()

    return pl.pallas_call(
        body, name=name, out_shape=jax.ShapeDtypeStruct((8,) + x.shape, x.dtype),
        in_specs=[_ANY], out_specs=_ANY,
        scratch_shapes=[pltpu.SemaphoreType.DMA((7,)), pltpu.SemaphoreType.DMA((7,)), pltpu.SemaphoreType.DMA(())],
    )(x)


SLAB_ROWS = DEPTH * D_MODEL + DEPTH * (D_MODEL // N_SHARD)


def _slab_pieces(gin, gout, s):
    out_rows = D_MODEL // N_SHARD
    p = [(gin[l].at[s], l * D_MODEL, D_MODEL) for l in range(DEPTH)]
    p += [(gout[l].at[s], DEPTH * D_MODEL + l * out_rows, out_rows) for l in range(DEPTH)]
    return p


def grad_exchange(gin, gout):
    def body(g0, g1, o0, o1, recv, send_sems, recv_sems, local_sem):
        mx, my, mc = _place()
        my_chip = 2 * mx + my
        chips = _other_chips(mx, my)
        gin_r, gout_r = (g0, g1), (o0, o1)
        for src, r0, n in _slab_pieces(gin_r, gout_r, my_chip):
            pltpu.make_async_copy(src, recv.at[my_chip, pl.ds(r0, n)], local_sem).start()
        for j, (px, py) in enumerate(chips):
            for src, r0, n in _slab_pieces(gin_r, gout_r, 2 * px + py):
                pltpu.make_async_remote_copy(
                    src_ref=src, dst_ref=recv.at[my_chip, pl.ds(r0, n)],
                    send_sem=send_sems.at[j], recv_sem=recv_sems.at[j],
                    device_id=(px, py, mc), device_id_type=MESH).start()
        for j, (px, py) in enumerate(chips):
            whole = recv.at[2 * px + py]
            both = pltpu.make_async_remote_copy(
                src_ref=whole, dst_ref=whole, send_sem=send_sems.at[j], recv_sem=recv_sems.at[j],
                device_id=(px, py, mc), device_id_type=MESH)
            both.wait_recv()
            both.wait_send()
        pltpu.make_async_copy(recv.at[my_chip], recv.at[my_chip], local_sem).wait()

    return pl.pallas_call(
        body, name="grad_exchange", out_shape=jax.ShapeDtypeStruct((N_SHARD, SLAB_ROWS, SHARD_W), BF16),
        in_specs=[_ANY] * 4, out_specs=_ANY,
        scratch_shapes=[pltpu.SemaphoreType.DMA((3,)), pltpu.SemaphoreType.DMA((3,)), pltpu.SemaphoreType.DMA(())],
    )(gin[0], gin[1], gout[0], gout[1])


def sum_slots(recv, tr=256):
    n, rows, cols = recv.shape

    def body(r_ref, o_ref):
        acc = r_ref[0].astype(F32)
        for k in range(1, n):
            acc = acc + r_ref[k].astype(F32)
        o_ref[...] = acc

    return pl.pallas_call(
        body, name="sum_slots", grid=(rows // tr,),
        in_specs=[pl.BlockSpec((n, tr, cols), lambda i: (0, i, 0))],
        out_specs=pl.BlockSpec((tr, cols), lambda i: (i, 0)),
        out_shape=jax.ShapeDtypeStruct((rows, cols), F32),
        compiler_params=_cp("arbitrary"),
    )(recv)


def swap_sibling(p):
    def body(p_ref, out_ref, send_sem, recv_sem):
        mx, my, mc = _place()
        cp = pltpu.make_async_remote_copy(
            src_ref=p_ref, dst_ref=out_ref, send_sem=send_sem, recv_sem=recv_sem,
            device_id=(mx, my, 1 - mc), device_id_type=MESH)
        cp.start()
        cp.wait()

    return pl.pallas_call(
        body, name="swap_sibling", out_shape=jax.ShapeDtypeStruct(p.shape, p.dtype),
        in_specs=[_ANY], out_specs=_ANY,
        scratch_shapes=[pltpu.SemaphoreType.DMA(()), pltpu.SemaphoreType.DMA(())],
    )(p)


def _adamw(w, g, m, v):
    m = ADAM_B1 * m + (1.0 - ADAM_B1) * g
    v = ADAM_B2 * v + (1.0 - ADAM_B2) * (g * g)
    m_hat = m / (1.0 - ADAM_B1 ** ADAM_STEP)
    v_hat = v / (1.0 - ADAM_B2 ** ADAM_STEP)
    delta = -ADAM_LR * (m_hat / (jnp.sqrt(v_hat) + ADAM_EPS) + ADAM_WD * w)
    return delta, m, v


def adam_slab(p_own, p_sib, w, m, v, row0, name, tr=256):
    L, R, C = w.shape
    nr = R // tr

    def body(a_ref, b_ref, w_ref, m_ref, v_ref, g_out, d_out, m_out, v_out):
        g = a_ref[...] + b_ref[...]
        d, m2, v2 = _adamw(w_ref[0], g, m_ref[0], v_ref[0])
        g_out[0], d_out[0], m_out[0], v_out[0] = g, d, m2, v2

    slab = pl.BlockSpec((tr, C), lambda l, i: (row0 // tr + l * nr + i, 0))
    blk = pl.BlockSpec((1, tr, C), lambda l, i: (l, i, 0))
    return pl.pallas_call(
        body, name=name, grid=(L, nr),
        in_specs=[slab, slab, blk, blk, blk], out_specs=[blk] * 4,
        out_shape=[jax.ShapeDtypeStruct(w.shape, F32)] * 4,
        compiler_params=_cp("arbitrary", "arbitrary"),
    )(p_own, p_sib, w, m, v)


def ada_fwd(c_all, w_ada):
    L, D, W = w_ada.shape

    def body(c_ref, w_ref, o_ref):
        cv = c_ref[...]
        o_ref[0] = jnp.dot(cv * _sigmoid(cv), w_ref[0], precision=lax.Precision.HIGHEST,
                           preferred_element_type=F32)

    return pl.pallas_call(
        body, name="ada_fwd", grid=(L,),
        in_specs=[pl.BlockSpec((8, D), lambda l: (0, 0)), pl.BlockSpec((1, D, W), lambda l: (l, 0, 0))],
        out_specs=pl.BlockSpec((1, 8, W), lambda l: (l, 0, 0)),
        out_shape=jax.ShapeDtypeStruct((L, 8, W), F32),
        compiler_params=_cp("arbitrary"),
    )(c_all, w_ada)


def vecs_build(mod_all, b_ada, norm_g):
    W = mod_all.shape[2]

    def body(m_ref, b_ref, g_ref, o_ref):
        mx, my, mc = _place()
        me = 4 * mx + 2 * my + mc
        rowid = lax.broadcasted_iota(jnp.int32, (2 * 8, 1), 0)
        o_ref[...] = jnp.zeros_like(o_ref)
        for l in range(DEPTH):
            parts = [jnp.sum(jnp.where(rowid == l * 8 + me, m_ref[2 * s + mc], 0.0), axis=0, keepdims=True)
                     for s in range(N_SHARD)]
            mod = jnp.concatenate(parts, axis=1) + b_ref[l:l + 1, :]
            for t in range(3):
                o_ref[l, t:t + 1, :] = mod[:, t * D_MODEL:(t + 1) * D_MODEL]
            o_ref[l, 3:4, :] = g_ref[l:l + 1, :]

    return pl.pallas_call(
        body, name="vecs_build", out_shape=jax.ShapeDtypeStruct((DEPTH, 8, D_MODEL), F32),
    )(mod_all, b_ada, norm_g)


def ada_update(dmods, c_t, w, m, v, tr=256):
    L, D, W = w.shape

    def body(dm_ref, c_ref, w_ref, m_ref, v_ref, g_out, d_out, m_out, v_out):
        mx, my, _ = _place()
        shard = 2 * mx + my
        dm = jnp.zeros((8, W), F32)
        for s in range(N_SHARD):
            dm = dm + jnp.where(shard == s, dm_ref[0, :, s * W:(s + 1) * W], 0.0)
        cv = c_ref[...]
        ca = cv * _sigmoid(cv)
        g = jnp.zeros((tr, W), F32)
        for b in range(8):
            g = g + ca[:, b:b + 1] * dm[b:b + 1, :]
        d, m2, v2 = _adamw(w_ref[0], g, m_ref[0], v_ref[0])
        g_out[0], d_out[0], m_out[0], v_out[0] = g, d, m2, v2

    blk = pl.BlockSpec((1, tr, W), lambda l, i: (l, i, 0))
    return pl.pallas_call(
        body, name="ada_update", grid=(L, D // tr),
        in_specs=[pl.BlockSpec((1, 8, 3 * D), lambda l, i: (l, 0, 0)), pl.BlockSpec((tr, 8), lambda l, i: (i, 0)),
                  blk, blk, blk],
        out_specs=[blk] * 4, out_shape=[jax.ShapeDtypeStruct(w.shape, F32)] * 4,
        compiler_params=_cp("arbitrary", "arbitrary"),
    )(dmods, c_t, w, m, v)


STAT_ROWS = 16


def small_update(stats_all, norm, b_ada, final):
    def body(s_ref, *refs):
        ins, outs = refs[:9], refs[9:]
        tot = s_ref[0]
        for k in range(1, 8):
            tot = tot + s_ref[k]
        g_norm = tot[0:2, :]
        g_final = tot[2:3, :]
        g_b = jnp.concatenate(
            [jnp.concatenate([tot[3 + 3 * l + t:4 + 3 * l + t, :] for t in range(3)], axis=1) for l in range(DEPTH)],
            axis=0)
        for p, g in enumerate((g_norm, g_b, g_final)):
            w_ref, m_ref, v_ref = ins[3 * p:3 * p + 3]
            d, m2, v2 = _adamw(w_ref[...], g, m_ref[...], v_ref[...])
            for o_ref, val in zip(outs[4 * p:4 * p + 4], (g, d, m2, v2)):
                o_ref[...] = val
        loss = (0.5 / D_MODEL) * jnp.sum(tot[9:10, :], axis=1, keepdims=True)
        outs[12][...] = jnp.broadcast_to(loss, (8, LANES))

    shapes = []
    for w, _, _ in (norm, b_ada, final):
        shapes += [jax.ShapeDtypeStruct(w.shape, F32)] * 4
    shapes.append(jax.ShapeDtypeStruct((8, LANES), F32))
    return pl.pallas_call(body, name="small_update", out_shape=shapes)(stats_all, *norm, *b_ada, *final)


def kernel(x, c, norm_g, w_ada, b_ada, w_in, w_out, final_g, loss_target, m_norm_g, m_w_ada, m_b_ada, m_w_in, m_w_out, m_final_g, v_norm_g, v_w_ada, v_b_ada, v_w_in, v_w_out, v_final_g):
    S, D = x.shape[1], x.shape[2]
    mc = lax.axis_index("c")
    out_rows = D // N_SHARD

    def my_half(a, rows):
        return lax.dynamic_slice_in_dim(a, mc * rows, rows, axis=0)

    wblk = jnp.concatenate([my_half(w_in[l], D // 2) for l in range(DEPTH)]
                           + [my_half(w_out[l], out_rows // 2) for l in range(DEPTH)], axis=0).astype(BF16)
    wall = allgather8(wblk, "gather_weights").reshape(N_SHARD, 2, wblk.shape[0], SHARD_W)
    w3 = [wall[:, :, l * (D // 2):(l + 1) * (D // 2)].reshape(N_SHARD, D, SHARD_W) for l in range(DEPTH)]
    wo = [wall[:, :, D + l * (out_rows // 2):D + (l + 1) * (out_rows // 2)].reshape(D, D) for l in range(DEPTH)]

    c_all = allgather8(jnp.broadcast_to(c, (8, D)), "gather_c")[:, 0, :]
    mod_all = allgather8(ada_fwd(c_all, w_ada).reshape(DEPTH * 8, -1), "gather_mod")
    vecs = vecs_build(mod_all, b_ada, norm_g)

    tabs = (*rope_tables(S), ret_log_gamma())
    h = x[0]
    saved = []
    for l in range(DEPTH):
        h, sv = layer_fwd(h, vecs[l], w3[l], wo[l], tabs)
        saved.append(sv)
    dx, st_loss = loss_head(h, jnp.broadcast_to(final_g[None, :], (8, D)), loss_target[0])

    gin, gout, dmod, dnorm = [None] * DEPTH, [None] * DEPTH, [None] * DEPTH, [None] * DEPTH
    for l in reversed(range(DEPTH)):
        dx, gin[l], g_out, dmod[l], dnorm[l] = layer_bwd(dx, saved[l], vecs[l], w3[l], wo[l], tabs)
        gout[l] = g_out.reshape(N_SHARD, out_rows, D)

    recv = grad_exchange(gin, gout)
    p_own = sum_slots(recv)
    p_sib = swap_sibling(p_own)
    res_in = adam_slab(p_own, p_sib, w_in, m_w_in, v_w_in, 0, "adam_w_in")
    res_out = adam_slab(p_own, p_sib, w_out, m_w_out, v_w_out, DEPTH * D, "adam_w_out", tr=128)

    stats = jnp.concatenate(dnorm + [st_loss[0:1]] + dmod + [st_loss[1:2], jnp.zeros((STAT_ROWS - 10, D), F32)], axis=0)
    stats_all = allgather8(stats, "gather_stats")
    dmods = stats_all[:, 3:9, :].reshape(8, DEPTH, 3 * D).transpose(1, 0, 2)
    res_ada = ada_update(dmods, c_all.T, w_ada, m_w_ada, v_w_ada)
    small = small_update(stats_all, (norm_g, m_norm_g, v_norm_g), (b_ada, m_b_ada, v_b_ada),
                         (final_g[None, :], m_final_g[None, :], v_final_g[None, :]))
    res_norm, res_b, res_final = small[0:4], small[4:8], [a[0] for a in small[8:12]]
    loss = small[12][0, 0]

    by_kind = [res_norm, res_ada, res_b, res_in, res_out, res_final]
    outs = [loss, dx[None]]
    for kind in range(4):
        outs += [r[kind] for r in by_kind]
    return tuple(outs)
```

```python
import functools

import numpy as np
import jax
import jax.numpy as jnp
from jax import lax
from jax.experimental import pallas as pl
from jax.experimental.pallas import tpu as pltpu

F32, BF16 = jnp.float32, jnp.bfloat16
MESH = pl.DeviceIdType.MESH

D_MODEL = 1024
DEPTH = 2
SHARD_W = 1024
N_SHARD = 4
GROUP_W = 512
LANES = 128
SB_HEAD_DIM = 64
RET_HEAD_DIM = 128
CHUNK = 64
ROPE_BASE = 10000.0
EPS = 1e-6
SQ_SCALE = SB_HEAD_DIM ** -0.5
RK_SCALE = RET_HEAD_DIM ** -0.5
SB_T = 256
RET_T = 256
EXP_ZERO = -104.0
VMEM_LIMIT_BYTES = 56 * 2 ** 20

ADAM_LR, ADAM_B1, ADAM_B2, ADAM_EPS, ADAM_WD, ADAM_STEP = 0.001, 0.9, 0.999, 1e-08, 0.01, 10


def _cp(*sem):
    return pltpu.CompilerParams(dimension_semantics=sem, vmem_limit_bytes=VMEM_LIMIT_BYTES)


def _dot(a, b):
    return lax.dot_general(a, b, (((1,), (0,)), ((), ())), preferred_element_type=F32)


def _dot_nt(a, b):
    return lax.dot_general(a, b, (((1,), (1,)), ((), ())), preferred_element_type=F32)


def _dot_tn(a, b):
    return lax.dot_general(a, b, (((0,), (0,)), ((), ())), preferred_element_type=F32)


def _dot_hilo(a, tri):
    hi = a.astype(BF16)
    lo = (a - hi.astype(F32)).astype(BF16)
    return _dot(hi, tri) + _dot(lo, tri)


def _sigmoid(x):
    return 1.0 / (1.0 + jnp.exp(-x))


def _rowsum(a):
    return jnp.sum(a, axis=1, keepdims=True)


def _rowmean(a):
    return jnp.mean(a, axis=1, keepdims=True)


def inproj_fwd(x, vecs, w3, tm=256):
    S, D = x.shape

    def body(x_ref, v_ref, w_ref, proj_ref, h_ref, sb_ref):
        xv = x_ref[...]
        r = lax.rsqrt(_rowmean(xv * xv) + EPS)
        h = xv * r * v_ref[3:4, :] * (1.0 + v_ref[1:2, :]) + v_ref[0:1, :]
        hb = h.astype(BF16)
        h_ref[...] = hb
        for s in range(N_SHARD):
            p = _dot(hb, w_ref[s])
            proj_ref[:, s * SHARD_W:(s + 1) * SHARD_W] = p
            if s == 2:
                sb_ref[:, 0:GROUP_W] = (p[:, 0:GROUP_W] * SQ_SCALE).astype(BF16)
                sb_ref[:, GROUP_W:SHARD_W] = p[:, GROUP_W:].astype(BF16)
            if s == 3:
                sb_ref[:, SHARD_W:2 * SHARD_W] = p.astype(BF16)

    return pl.pallas_call(
        body, name="inproj_fwd", grid=(S // tm,),
        in_specs=[pl.BlockSpec((tm, D), lambda i: (i, 0)),
                  pl.BlockSpec((8, D), lambda i: (0, 0)),
                  pl.BlockSpec((N_SHARD, D, SHARD_W), lambda i: (0, 0, 0))],
        out_specs=[pl.BlockSpec((tm, 4 * D), lambda i: (i, 0)),
                   pl.BlockSpec((tm, D), lambda i: (i, 0)),
                   pl.BlockSpec((tm, 2 * SHARD_W), lambda i: (i, 0))],
        out_shape=[jax.ShapeDtypeStruct((S, 4 * D), F32),
                   jax.ShapeDtypeStruct((S, D), BF16),
                   jax.ShapeDtypeStruct((S, 2 * SHARD_W), BF16)],
        compiler_params=_cp("arbitrary"),
    )(x, vecs, w3)


def _sb_logits(qh, k2, mask):
    z = _dot_nt(qh, k2)
    sp = jnp.log(1.0 + jnp.exp(-jnp.abs(z)))
    lb = jnp.minimum(z, 0.0) - sp
    lk = lb - z
    if mask is not None:
        lk = jnp.where(mask, lk, 0.0)
    return lb, lk


def _sb_masks():
    T = SB_T
    row = lax.broadcasted_iota(jnp.int32, (T, T), 0)
    col = lax.broadcasted_iota(jnp.int32, (T, T), 1)
    causal = col < row
    later = jnp.where(row > col, 1.0, 0.0).astype(BF16)
    earlier = jnp.where(row < col, 1.0, 0.0).astype(BF16)
    lane = lax.broadcasted_iota(jnp.int32, (1, LANES), 1)
    return causal, later, earlier, lane < SB_HEAD_DIM


def _sb_rows(ref, j):
    return ref[pl.ds(pl.multiple_of(j * SB_T, SB_T), SB_T), :]


def sb_fwd(sb, proj):
    S = sb.shape[0]
    T = SB_T
    nq = S // T

    def body(q_ref, k_ref, v_ref, sg_ref, y_ref, o_ref):
        i = pl.program_id(1)
        causal, later, _, head0 = _sb_masks()
        q2 = q_ref[...]
        zero = jnp.zeros_like(q2)
        qs = jnp.concatenate([jnp.where(head0, q2, zero), jnp.where(head0, zero, q2)], axis=0)
        causal2 = jnp.concatenate([causal, causal], axis=0)

        def tile(j, R, mask):
            lb, lk = _sb_logits(qs, _sb_rows(k_ref, j), mask)
            a = jnp.exp(lb + _dot_hilo(lk, later) + R)
            if mask is not None:
                a = jnp.where(mask, a, 0.0)
            return _dot(a.astype(BF16), _sb_rows(v_ref, j)), R + _rowsum(lk)

        acc, R = tile(i, jnp.zeros((2 * T, 1), F32), causal2)

        def cond(st):
            return jnp.logical_and(st[0] >= 0, st[3] > EXP_ZERO)

        def step(st):
            c, Rn = tile(st[0], st[2], None)
            return st[0] - 1, st[1] + c, Rn, jnp.max(Rn)

        acc = lax.while_loop(cond, step, (i - 1, acc, R, jnp.max(R)))[1]
        o = jnp.where(head0, acc[:T], acc[T:])
        o_ref[...] = o
        sg = sg_ref[...]
        y_ref[...] = (o * (sg * _sigmoid(sg))).astype(BF16)

    return pl.pallas_call(
        body, name="sb_fwd", grid=(4, nq),
        in_specs=[pl.BlockSpec((T, LANES), lambda p, i: (i, p)),
                  pl.BlockSpec((S, LANES), lambda p, i: (0, 4 + p)),
                  pl.BlockSpec((S, LANES), lambda p, i: (0, 8 + p)),
                  pl.BlockSpec((T, LANES), lambda p, i: (i, 28 + p))],
        out_specs=[pl.BlockSpec((T, LANES), lambda p, i: (i, p)),
                   pl.BlockSpec((T, LANES), lambda p, i: (i, p))],
        out_shape=[jax.ShapeDtypeStruct((S, GROUP_W), BF16),
                   jax.ShapeDtypeStruct((S, GROUP_W), F32)],
        compiler_params=_cp("arbitrary", "arbitrary"),
    )(sb, sb, sb, proj)


def sb_bwd(sb, proj, o, dycat):
    S = sb.shape[0]
    T = SB_T
    nq = S // T

    def body(q_ref, k_ref, v_ref, sg_ref, o_ref, dy_ref, dq_ref, dk_ref, dv_ref, dsg_ref,
             dk_acc, dv_acc, stick):
        i = pl.program_id(1)

        @pl.when(i == 0)
        def _():
            dk_acc[...] = jnp.zeros_like(dk_acc)
            dv_acc[...] = jnp.zeros_like(dv_acc)

        causal, later, earlier, head0 = _sb_masks()
        q2 = q_ref[...]
        sg = sg_ref[...]
        sig = _sigmoid(sg)
        dy = dy_ref[...]
        dsg_ref[...] = (dy * o_ref[...] * (sig * (1.0 + sg * (1.0 - sig)))).astype(BF16)
        do_b = (dy * (sg * sig)).astype(BF16)
        zero = jnp.zeros_like(q2)
        qs = jnp.concatenate([jnp.where(head0, q2, zero), jnp.where(head0, zero, q2)], axis=0)
        dos = jnp.concatenate([jnp.where(head0, do_b, zero), jnp.where(head0, zero, do_b)], axis=0)
        causal2 = jnp.concatenate([causal, causal], axis=0)

        def advance(j, R, mask):
            stick[j] = R
            return R + _rowsum(_sb_logits(qs, _sb_rows(k_ref, j), mask)[1])

        r_zero = jnp.zeros((2 * T, 1), F32)
        R = advance(i, r_zero, causal2)

        def cond(st):
            return jnp.logical_and(st[0] >= 0, st[2] > EXP_ZERO)

        def step(st):
            Rn = advance(st[0], st[1], None)
            return st[0] - 1, Rn, jnp.max(Rn)

        j_end = lax.while_loop(cond, step, (i - 1, R, jnp.max(R)))[0]

        def tile(j, G0, mask):
            k2, v2 = _sb_rows(k_ref, j), _sb_rows(v_ref, j)
            lb, lk = _sb_logits(qs, k2, mask)
            a = jnp.exp(lb + _dot_hilo(lk, later) + stick[j])
            if mask is not None:
                a = jnp.where(mask, a, 0.0)
            g = a * _dot_nt(dos, v2)
            G = _dot_hilo(g, earlier) + G0
            dz = g - jnp.exp(lb) * (g + G)
            if mask is not None:
                dz = jnp.where(mask, dz, 0.0)
            dzb = dz.astype(BF16)
            rows = pl.ds(pl.multiple_of(j * T, T), T)
            dk_acc[rows, :] += _dot_tn(dzb, qs)
            dv_acc[rows, :] += _dot_tn(a.astype(BF16), dos)
            return _dot(dzb, k2), G0 + _rowsum(g)

        def sweep(j, st):
            dq_t, Gn = tile(j, st[1], None)
            return st[0] + dq_t, Gn

        st = lax.fori_loop(j_end + 1, i, sweep, (jnp.zeros((2 * T, LANES), F32), r_zero))
        dq = st[0] + tile(i, st[1], causal2)[0]
        dq_ref[...] = (jnp.where(head0, dq[:T], dq[T:]) * SQ_SCALE).astype(BF16)

        @pl.when(i == nq - 1)
        def _():
            dk_ref[...] = dk_acc[...].astype(BF16)
            dv_ref[...] = dv_acc[...].astype(BF16)

    tile_spec = lambda c0: pl.BlockSpec((T, LANES), lambda p, i: (i, c0 + p))
    head_spec = lambda c0: pl.BlockSpec((S, LANES), lambda p, i: (0, c0 + p))
    return pl.pallas_call(
        body, name="sb_bwd", grid=(4, nq),
        in_specs=[tile_spec(0), head_spec(4), head_spec(8), tile_spec(28), tile_spec(0), tile_spec(4)],
        out_specs=[tile_spec(0), head_spec(0), head_spec(0), tile_spec(0)],
        out_shape=[jax.ShapeDtypeStruct((S, GROUP_W), BF16)] * 4,
        scratch_shapes=[pltpu.VMEM((S, LANES), F32), pltpu.VMEM((S, LANES), F32),
                        pltpu.VMEM((nq, 2 * T, 1), F32)],
        compiler_params=_cp("arbitrary", "arbitrary"),
    )(sb, sb, sb, proj, o, dycat)


def rope_tables(S):
    half = RET_HEAD_DIM // 2
    inv = ROPE_BASE ** (-jnp.arange(half, dtype=F32) / half)
    ang = jnp.arange(S, dtype=F32)[:, None] * inv[None, :]
    cos, sin = jnp.cos(ang), jnp.sin(ang)
    return jnp.concatenate([cos, cos], axis=1), jnp.concatenate([-sin, sin], axis=1)


def ret_log_gamma():
    return jnp.log1p(-(2.0 ** (-5.0 - jnp.arange(4, dtype=F32))))


def _swap_halves(a):
    return pltpu.roll(a, RET_HEAD_DIM // 2, axis=1)


def _ret_decay_mask(lg):
    n = lax.broadcasted_iota(jnp.int32, (RET_T, RET_T), 0)
    m = lax.broadcasted_iota(jnp.int32, (RET_T, RET_T), 1)
    dist = jnp.abs(n - m).astype(F32)
    return jnp.where((m // CHUNK) <= (n // CHUNK), jnp.exp(lg * dist), 0.0)


def _ret_block(lg, rq_ref, rk_ref, rv_ref, cos_ref, sin_ref, dm_ref):
    cosf, sinf = cos_ref[...], sin_ref[...]
    rq, rk = rq_ref[...], rk_ref[...]
    q = rq * cosf + _swap_halves(rq) * sinf
    k = (rk * cosf + _swap_halves(rk) * sinf) * RK_SCALE
    qb, kb, vb = q.astype(BF16), k.astype(BF16), rv_ref[...].astype(BF16)
    sc = _dot_nt(qb, kb) * dm_ref[...]
    nloc = lax.broadcasted_iota(jnp.int32, (RET_T, 1), 0).astype(F32)
    qdec = jnp.exp(lg * (nloc + 1.0))
    kdec = jnp.exp(lg * (RET_T - 1.0 - nloc))
    block_dec = jnp.exp(jnp.full((1, LANES), lg * RET_T, F32))
    return q, k, qb, kb, vb, sc, qdec, kdec, block_dec, cosf, sinf


def _ret_specs(S):
    nb = S // RET_T
    return nb, (lambda c0, rb: pl.BlockSpec((RET_T, LANES), lambda h, b: (rb(b), c0 + h)))


def ret_fwd(proj, cosf, sinf, lgam):
    S = proj.shape[0]
    nb, spec = _ret_specs(S)
    ident = lambda b: b

    def body(lg_ref, rq_ref, rk_ref, rv_ref, rg_ref, cos_ref, sin_ref, y_ref, o_ref, st_out, st_ref, dm_ref):
        lg = lg_ref[pl.program_id(0)]

        @pl.when(pl.program_id(1) == 0)
        def _():
            st_ref[...] = jnp.zeros_like(st_ref)
            dm_ref[...] = _ret_decay_mask(lg)

        q, k, qb, kb, vb, sc, qdec, kdec, block_dec, _, _ = _ret_block(lg, rq_ref, rk_ref, rv_ref, cos_ref, sin_ref, dm_ref)
        st = st_ref[...]
        st_out[0, 0] = st
        o = _dot(sc.astype(BF16), vb) + _dot(qb, st.astype(BF16)) * qdec
        st_ref[...] = st * block_dec + _dot_tn((k * kdec).astype(BF16), vb)
        o_ref[...] = o
        cen = o - _rowmean(o)
        on = cen * lax.rsqrt(_rowmean(cen * cen) + EPS)
        rg = rg_ref[...]
        y_ref[...] = (on * (rg * _sigmoid(rg))).astype(BF16)

    row_tab = pl.BlockSpec((RET_T, LANES), lambda h, b: (b, 0))
    return pl.pallas_call(
        body, name="ret_fwd", grid=(4, nb),
        in_specs=[pl.BlockSpec(memory_space=pltpu.SMEM),
                  spec(0, ident), spec(4, ident), spec(8, ident), spec(12, ident), row_tab, row_tab],
        out_specs=[spec(0, ident), spec(0, ident),
                   pl.BlockSpec((1, 1, LANES, LANES), lambda h, b: (h, b, 0, 0))],
        out_shape=[jax.ShapeDtypeStruct((S, GROUP_W), BF16),
                   jax.ShapeDtypeStruct((S, GROUP_W), F32),
                   jax.ShapeDtypeStruct((4, nb, LANES, LANES), F32)],
        scratch_shapes=[pltpu.VMEM((LANES, LANES), F32), pltpu.VMEM((RET_T, RET_T), F32)],
        compiler_params=_cp("arbitrary", "arbitrary"),
    )(lgam, proj, proj, proj, proj, cosf, sinf)


def ret_bwd(proj, cosf, sinf, lgam, o, states, dycat):
    S = proj.shape[0]
    nb, spec = _ret_specs(S)
    rev = lambda b: nb - 1 - b

    def body(lg_ref, rq_ref, rk_ref, rv_ref, rg_ref, cos_ref, sin_ref, o_ref, st_in, dy_ref,
             drq_ref, drk_ref, drv_ref, drg_ref, ds_ref, dm_ref):
        lg = lg_ref[pl.program_id(0)]

        @pl.when(pl.program_id(1) == 0)
        def _():
            ds_ref[...] = jnp.zeros_like(ds_ref)
            dm_ref[...] = _ret_decay_mask(lg)

        q, k, qb, kb, vb, sc, qdec, kdec, block_dec, cosf_v, sinf_v = _ret_block(
            lg, rq_ref, rk_ref, rv_ref, cos_ref, sin_ref, dm_ref)
        o_v = o_ref[...]
        cen = o_v - _rowmean(o_v)
        rstd = lax.rsqrt(_rowmean(cen * cen) + EPS)
        on = cen * rstd
        rg = rg_ref[...]
        sig = _sigmoid(rg)
        dy = dy_ref[...]
        drg_ref[...] = (dy * on * (sig * (1.0 + rg * (1.0 - sig)))).astype(BF16)
        don = dy * (rg * sig)
        do = rstd * (don - _rowmean(don) - on * _rowmean(don * on))
        dob = do.astype(BF16)
        dsc = (_dot_nt(dob, vb) * dm_ref[...]).astype(BF16)
        st_b = st_in[0, 0].astype(BF16)
        dsn = ds_ref[...]
        dsn_b = dsn.astype(BF16)
        dq = _dot(dsc, kb) + _dot_nt(dob, st_b) * qdec
        dk = _dot_tn(dsc, qb) + _dot_nt(vb, dsn_b) * kdec
        dv = _dot_tn(sc.astype(BF16), dob) + _dot((k * kdec).astype(BF16), dsn_b)
        ds_ref[...] = dsn * block_dec + _dot_tn((q * qdec).astype(BF16), dob)
        dk = dk * RK_SCALE
        drq_ref[...] = (dq * cosf_v + _swap_halves(dq * sinf_v)).astype(BF16)
        drk_ref[...] = (dk * cosf_v + _swap_halves(dk * sinf_v)).astype(BF16)
        drv_ref[...] = dv.astype(BF16)

    row_tab = pl.BlockSpec((RET_T, LANES), lambda h, b: (rev(b), 0))
    return pl.pallas_call(
        body, name="ret_bwd", grid=(4, nb),
        in_specs=[pl.BlockSpec(memory_space=pltpu.SMEM),
                  spec(0, rev), spec(4, rev), spec(8, rev), spec(12, rev), row_tab, row_tab,
                  spec(0, rev), pl.BlockSpec((1, 1, LANES, LANES), lambda h, b: (h, rev(b), 0, 0)),
                  spec(0, rev)],
        out_specs=[spec(0, rev)] * 4,
        out_shape=[jax.ShapeDtypeStruct((S, GROUP_W), BF16)] * 4,
        scratch_shapes=[pltpu.VMEM((LANES, LANES), F32), pltpu.VMEM((RET_T, RET_T), F32)],
        compiler_params=_cp("arbitrary", "arbitrary"),
    )(lgam, proj, proj, proj, proj, cosf, sinf, o, states, dycat)


def outproj_fwd(x, vecs, y_ret, y_sb, w_out, tm=512):
    S, D = x.shape

    def body(x_ref, v_ref, yr_ref, ys_ref, w_ref, y_ref, xo_ref):
        y = _dot(yr_ref[...], w_ref[0:GROUP_W, :]) + _dot(ys_ref[...], w_ref[GROUP_W:, :])
        y_ref[...] = y
        xo_ref[...] = x_ref[...] + v_ref[2:3, :] * y

    row = lambda w: pl.BlockSpec((tm, w), lambda i: (i, 0))
    return pl.pallas_call(
        body, name="outproj_fwd", grid=(S // tm,),
        in_specs=[row(D), pl.BlockSpec((8, D), lambda i: (0, 0)), row(GROUP_W), row(GROUP_W),
                  pl.BlockSpec((D, D), lambda i: (0, 0))],
        out_specs=[row(D), row(D)],
        out_shape=[jax.ShapeDtypeStruct((S, D), F32)] * 2,
        compiler_params=_cp("arbitrary"),
    )(x, vecs, y_ret, y_sb, w_out)


def loss_head(x, final_g, target, tm=512):
    S, D = x.shape

    def body(x_ref, g_ref, t_ref, dx_ref, st_ref):
        @pl.when(pl.program_id(0) == 0)
        def _():
            st_ref[...] = jnp.zeros_like(st_ref)

        xv = x_ref[...]
        g = g_ref[0:1, :]
        r = lax.rsqrt(_rowmean(xv * xv) + EPS)
        xn = xv * r
        err = xn * g - t_ref[...]
        dy = err * (1.0 / D)
        dxn = dy * g
        dx_ref[...] = r * (dxn - xn * _rowmean(dxn * xn))
        st_ref[0:1, :] += jnp.sum(dy * xn, axis=0, keepdims=True)
        st_ref[1:2, :] += jnp.sum(err * err, axis=0, keepdims=True)

    row = pl.BlockSpec((tm, D), lambda i: (i, 0))
    fixed = pl.BlockSpec((8, D), lambda i: (0, 0))
    return pl.pallas_call(
        body, name="loss_head", grid=(S // tm,),
        in_specs=[row, fixed, row], out_specs=[row, fixed],
        out_shape=[jax.ShapeDtypeStruct((S, D), F32), jax.ShapeDtypeStruct((8, D), F32)],
        compiler_params=_cp("arbitrary"),
    )(x, final_g, target)


def outproj_bwd(dx, y, vecs, y_ret, y_sb, w_out, tm=512):
    S, D = dx.shape
    n = S // tm

    def body(dx_ref, y_ref, v_ref, yr_ref, ys_ref, w_ref, dyc_ref, dw_ref, st_ref, acc):
        i = pl.program_id(0)

        @pl.when(i == 0)
        def _():
            st_ref[...] = jnp.zeros_like(st_ref)
            acc[...] = jnp.zeros_like(acc)

        dxv = dx_ref[...]
        st_ref[0:1, :] += jnp.sum(dxv * y_ref[...], axis=0, keepdims=True)
        dyy = (dxv * v_ref[2:3, :]).astype(BF16)
        dyc_ref[...] = _dot_nt(dyy, w_ref[...])
        acc[0:GROUP_W, :] += _dot_tn(yr_ref[...], dyy)
        acc[GROUP_W:, :] += _dot_tn(ys_ref[...], dyy)

        @pl.when(i == n - 1)
        def _():
            dw_ref[...] = acc[...].astype(BF16)

    row = lambda w: pl.BlockSpec((tm, w), lambda i: (i, 0))
    fixed = lambda r: pl.BlockSpec((r, D), lambda i: (0, 0))
    return pl.pallas_call(
        body, name="outproj_bwd", grid=(n,),
        in_specs=[row(D), row(D), fixed(8), row(GROUP_W), row(GROUP_W), fixed(D)],
        out_specs=[row(D), fixed(D), fixed(8)],
        out_shape=[jax.ShapeDtypeStruct((S, D), F32), jax.ShapeDtypeStruct((D, D), BF16),
                   jax.ShapeDtypeStruct((8, D), F32)],
        scratch_shapes=[pltpu.VMEM((D, D), F32)],
        compiler_params=_cp("arbitrary"),
    )(dx, y, vecs, y_ret, y_sb, w_out)


def inproj_bwd_x(pieces, w3, x, vecs, dx_res, tm=256):
    S, D = x.shape

    def body(*refs):
        p_refs, (w_ref, x_ref, v_ref, dr_ref, dx_ref, st_ref) = refs[:8], refs[8:]

        @pl.when(pl.program_id(0) == 0)
        def _():
            st_ref[...] = jnp.zeros_like(st_ref)

        dh = jnp.zeros((tm, D), F32)
        for k, p_ref in enumerate(p_refs):
            c0 = (k % 2) * GROUP_W
            dh = dh + _dot_nt(p_ref[...], w_ref[k // 2, :, c0:c0 + GROUP_W])
        xv = x_ref[...]
        r = lax.rsqrt(_rowmean(xv * xv) + EPS)
        xn = xv * r
        g, scale1 = v_ref[3:4, :], 1.0 + v_ref[1:2, :]
        st_ref[0:1, :] += jnp.sum(dh, axis=0, keepdims=True)
        dh_xn = dh * xn
        st_ref[1:2, :] += jnp.sum(dh_xn, axis=0, keepdims=True) * g
        st_ref[2:3, :] += jnp.sum(dh_xn, axis=0, keepdims=True) * scale1
        dxn = dh * (g * scale1)
        dx_ref[...] = r * (dxn - xn * _rowmean(dxn * xn)) + dr_ref[...]

    row = lambda w: pl.BlockSpec((tm, w), lambda i: (i, 0))
    return pl.pallas_call(
        body, name="inproj_bwd_x", grid=(S // tm,),
        in_specs=[row(GROUP_W)] * 8 + [pl.BlockSpec((N_SHARD, D, SHARD_W), lambda i: (0, 0, 0)),
                                       row(D), pl.BlockSpec((8, D), lambda i: (0, 0)), row(D)],
        out_specs=[row(D), pl.BlockSpec((8, D), lambda i: (0, 0))],
        out_shape=[jax.ShapeDtypeStruct((S, D), F32), jax.ShapeDtypeStruct((8, D), F32)],
        compiler_params=_cp("arbitrary"),
    )(*pieces, w3, x, vecs, dx_res)


def inproj_bwd_w(h, pieces, tm=512):
    S, D = h.shape
    n = S // tm

    def body(*refs):
        h_ref, p_refs, dw_ref, acc = refs[0], refs[1:9], refs[9], refs[10]
        i = pl.program_id(0)

        @pl.when(i == 0)
        def _():
            acc[...] = jnp.zeros_like(acc)

        hv = h_ref[...]
        for k, p_ref in enumerate(p_refs):
            c0 = (k % 2) * GROUP_W
            acc[k // 2, :, c0:c0 + GROUP_W] += _dot_tn(hv, p_ref[...])

        @pl.when(i == n - 1)
        def _():
            dw_ref[...] = acc[...].astype(BF16)

    row = lambda w: pl.BlockSpec((tm, w), lambda i: (i, 0))
    return pl.pallas_call(
        body, name="inproj_bwd_w", grid=(n,),
        in_specs=[row(D)] + [row(GROUP_W)] * 8,
        out_specs=pl.BlockSpec((N_SHARD, D, SHARD_W), lambda i: (0, 0, 0)),
        out_shape=jax.ShapeDtypeStruct((N_SHARD, D, SHARD_W), BF16),
        scratch_shapes=[pltpu.VMEM((N_SHARD, D, SHARD_W), F32)],
        compiler_params=_cp("arbitrary"),
    )(h, *pieces)


def layer_fwd(x, vecs, w3, w_out, tabs):
    cosf, sinf, lgam = tabs
    proj, h, sb = inproj_fwd(x, vecs, w3)
    y_ret, o_ret, states = ret_fwd(proj, cosf, sinf, lgam)
    y_sb, o_sb = sb_fwd(sb, proj)
    y, x_next = outproj_fwd(x, vecs, y_ret, y_sb, w_out)
    return x_next, (x, proj, h, sb, y_ret, o_ret, states, y_sb, o_sb, y)


def layer_bwd(dx, saved, vecs, w3, w_out, tabs):
    cosf, sinf, lgam = tabs
    x, proj, h, sb, y_ret, o_ret, states, y_sb, o_sb, y = saved
    dycat, dw_out, st_o = outproj_bwd(dx, y, vecs, y_ret, y_sb, w_out)
    d_sb = sb_bwd(sb, proj, o_sb, dycat)
    d_ret = ret_bwd(proj, cosf, sinf, lgam, o_ret, states, dycat)
    pieces = list(d_ret) + list(d_sb)
    dx, st_i = inproj_bwd_x(pieces, w3, x, vecs, dx)
    dw_in = inproj_bwd_w(h, pieces)
    dmod = jnp.concatenate([st_i[0:2], st_o[0:1]], axis=0)
    return dx, dw_in, dw_out, dmod, st_i[2:3]


def _place():
    return lax.axis_index("x"), lax.axis_index("y"), lax.axis_index("c")


def _other_chips(mx, my):
    return [(1 - mx, my), (mx, 1 - my), (1 - mx, 1 - my)]


_ANY = pl.BlockSpec(memory_space=pl.ANY)


def allgather8(x, name):
    def body(x_ref, out_ref, send_sems, recv_sems, local_sem):
        mx, my, mc = _place()
        me, sibling = (mx, my, mc), (mx, my, 1 - mc)
        chips = _other_chips(mx, my)

        def slot(px, py, pc):
            return out_ref.at[4 * px + 2 * py + pc]

        def copy(k, block, to, src=None):
            return pltpu.make_async_remote_copy(
                src_ref=slot(*block) if src is None else src, dst_ref=slot(*block),
                send_sem=send_sems.at[k], recv_sem=recv_sems.at[k], device_id=to, device_id_type=MESH)

        mine = pltpu.make_async_copy(x_ref, slot(*me), local_sem)
        mine.start()
        first = [copy(0, me, sibling, src=x_ref)]
        first += [copy(1 + j, me, (*chip, mc), src=x_ref) for j, chip in enumerate(chips)]
        for cp in first:
            cp.start()
        passed = [copy(4 + j, (*chip, mc), sibling) for j, chip in enumerate(chips)]
        for j, chip in enumerate(chips):
            copy(1 + j, (*chip, mc), me).wait_recv()
            passed[j].start()
        copy(0, sibling, me).wait_recv()
        for j, chip in enumerate(chips):
            copy(4 + j, (*chip, 1 - mc), me).wait_recv()
        for cp in first + passed:
            cp.wait_send()
        mine.wait()

    return pl.pallas_call(
        body, name=name, out_shape=jax.ShapeDtypeStruct((8,) + x.shape, x.dtype),
        in_specs=[_ANY], out_specs=_ANY,
        scratch_shapes=[pltpu.SemaphoreType.DMA((7,)), pltpu.SemaphoreType.DMA((7,)), pltpu.SemaphoreType.DMA(())],
    )(x)


SLAB_ROWS = DEPTH * D_MODEL + DEPTH * (D_MODEL // N_SHARD)


def _slab_pieces(gin, gout, s):
    out_rows = D_MODEL // N_SHARD
    p = [(gin[l].at[s], l * D_MODEL, D_MODEL) for l in range(DEPTH)]
    p += [(gout[l].at[s], DEPTH * D_MODEL + l * out_rows, out_rows) for l in range(DEPTH)]
    return p


def grad_exchange(gin, gout):
    def body(g0, g1, o0, o1, recv, send_sems, recv_sems, local_sem):
        mx, my, mc = _place()
        my_chip = 2 * mx + my
        chips = _other_chips(mx, my)
        gin_r, gout_r = (g0, g1), (o0, o1)
        for src, r0, n in _slab_pieces(gin_r, gout_r, my_chip):
            pltpu.make_async_copy(src, recv.at[my_chip, pl.ds(r0, n)], local_sem).start()
        for j, (px, py) in enumerate(chips):
            for src, r0, n in _slab_pieces(gin_r, gout_r, 2 * px + py):
                pltpu.make_async_remote_copy(
                    src_ref=src, dst_ref=recv.at[my_chip, pl.ds(r0, n)],
                    send_sem=send_sems.at[j], recv_sem=recv_sems.at[j],
                    device_id=(px, py, mc), device_id_type=MESH).start()
        for j, (px, py) in enumerate(chips):
            whole = recv.at[2 * px + py]
            both = pltpu.make_async_remote_copy(
                src_ref=whole, dst_ref=whole, send_sem=send_sems.at[j], recv_sem=recv_sems.at[j],
                device_id=(px, py, mc), device_id_type=MESH)
            both.wait_recv()
            both.wait_send()
        pltpu.make_async_copy(recv.at[my_chip], recv.at[my_chip], local_sem).wait()

    return pl.pallas_call(
        body, name="grad_exchange", out_shape=jax.ShapeDtypeStruct((N_SHARD, SLAB_ROWS, SHARD_W), BF16),
        in_specs=[_ANY] * 4, out_specs=_ANY,
        scratch_shapes=[pltpu.SemaphoreType.DMA((3,)), pltpu.SemaphoreType.DMA((3,)), pltpu.SemaphoreType.DMA(())],
    )(gin[0], gin[1], gout[0], gout[1])


def sum_slots(recv, tr=256):
    n, rows, cols = recv.shape

    def body(r_ref, o_ref):
        acc = r_ref[0].astype(F32)
        for k in range(1, n):
            acc = acc + r_ref[k].astype(F32)
        o_ref[...] = acc

    return pl.pallas_call(
        body, name="sum_slots", grid=(rows // tr,),
        in_specs=[pl.BlockSpec((n, tr, cols), lambda i: (0, i, 0))],
        out_specs=pl.BlockSpec((tr, cols), lambda i: (i, 0)),
        out_shape=jax.ShapeDtypeStruct((rows, cols), F32),
        compiler_params=_cp("arbitrary"),
    )(recv)


def swap_sibling(p):
    def body(p_ref, out_ref, send_sem, recv_sem):
        mx, my, mc = _place()
        cp = pltpu.make_async_remote_copy(
            src_ref=p_ref, dst_ref=out_ref, send_sem=send_sem, recv_sem=recv_sem,
            device_id=(mx, my, 1 - mc), device_id_type=MESH)
        cp.start()
        cp.wait()

    return pl.pallas_call(
        body, name="swap_sibling", out_shape=jax.ShapeDtypeStruct(p.shape, p.dtype),
        in_specs=[_ANY], out_specs=_ANY,
        scratch_shapes=[pltpu.SemaphoreType.DMA(()), pltpu.SemaphoreType.DMA(())],
    )(p)


def _adamw(w, g, m, v):
    m = ADAM_B1 * m + (1.0 - ADAM_B1) * g
    v = ADAM_B2 * v + (1.0 - ADAM_B2) * (g * g)
    m_hat = m / (1.0 - ADAM_B1 ** ADAM_STEP)
    v_hat = v / (1.0 - ADAM_B2 ** ADAM_STEP)
    delta = -ADAM_LR * (m_hat / (jnp.sqrt(v_hat) + ADAM_EPS) + ADAM_WD * w)
    return delta, m, v


def adam_slab(p_own, p_sib, w, m, v, row0, name, tr=256):
    L, R, C = w.shape
    nr = R // tr

    def body(a_ref, b_ref, w_ref, m_ref, v_ref, g_out, d_out, m_out, v_out):
        g = a_ref[...] + b_ref[...]
        d, m2, v2 = _adamw(w_ref[0], g, m_ref[0], v_ref[0])
        g_out[0], d_out[0], m_out[0], v_out[0] = g, d, m2, v2

    slab = pl.BlockSpec((tr, C), lambda l, i: (row0 // tr + l * nr + i, 0))
    blk = pl.BlockSpec((1, tr, C), lambda l, i: (l, i, 0))
    return pl.pallas_call(
        body, name=name, grid=(L, nr),
        in_specs=[slab, slab, blk, blk, blk], out_specs=[blk] * 4,
        out_shape=[jax.ShapeDtypeStruct(w.shape, F32)] * 4,
        compiler_params=_cp("arbitrary", "arbitrary"),
    )(p_own, p_sib, w, m, v)


def ada_fwd(c_all, w_ada):
    L, D, W = w_ada.shape

    def body(c_ref, w_ref, o_ref):
        cv = c_ref[...]
        o_ref[0] = jnp.dot(cv * _sigmoid(cv), w_ref[0], precision=lax.Precision.HIGHEST,
                           preferred_element_type=F32)

    return pl.pallas_call(
        body, name="ada_fwd", grid=(L,),
        in_specs=[pl.BlockSpec((8, D), lambda l: (0, 0)), pl.BlockSpec((1, D, W), lambda l: (l, 0, 0))],
        out_specs=pl.BlockSpec((1, 8, W), lambda l: (l, 0, 0)),
        out_shape=jax.ShapeDtypeStruct((L, 8, W), F32),
        compiler_params=_cp("arbitrary"),
    )(c_all, w_ada)


def vecs_build(mod_all, b_ada, norm_g):
    W = mod_all.shape[2]

    def body(m_ref, b_ref, g_ref, o_ref):
        mx, my, mc = _place()
        me = 4 * mx + 2 * my + mc
        rowid = lax.broadcasted_iota(jnp.int32, (2 * 8, 1), 0)
        o_ref[...] = jnp.zeros_like(o_ref)
        for l in range(DEPTH):
            parts = [jnp.sum(jnp.where(rowid == l * 8 + me, m_ref[2 * s + mc], 0.0), axis=0, keepdims=True)
                     for s in range(N_SHARD)]
            mod = jnp.concatenate(parts, axis=1) + b_ref[l:l + 1, :]
            for t in range(3):
                o_ref[l, t:t + 1, :] = mod[:, t * D_MODEL:(t + 1) * D_MODEL]
            o_ref[l, 3:4, :] = g_ref[l:l + 1, :]

    return pl.pallas_call(
        body, name="vecs_build", out_shape=jax.ShapeDtypeStruct((DEPTH, 8, D_MODEL), F32),
    )(mod_all, b_ada, norm_g)


def ada_update(dmods, c_t, w, m, v, tr=256):
    L, D, W = w.shape

    def body(dm_ref, c_ref, w_ref, m_ref, v_ref, g_out, d_out, m_out, v_out):
        mx, my, _ = _place()
        shard = 2 * mx + my
        dm = jnp.zeros((8, W), F32)
        for s in range(N_SHARD):
            dm = dm + jnp.where(shard == s, dm_ref[0, :, s * W:(s + 1) * W], 0.0)
        cv = c_ref[...]
        ca = cv * _sigmoid(cv)
        g = jnp.zeros((tr, W), F32)
        for b in range(8):
            g = g + ca[:, b:b + 1] * dm[b:b + 1, :]
        d, m2, v2 = _adamw(w_ref[0], g, m_ref[0], v_ref[0])
        g_out[0], d_out[0], m_out[0], v_out[0] = g, d, m2, v2

    blk = pl.BlockSpec((1, tr, W), lambda l, i: (l, i, 0))
    return pl.pallas_call(
        body, name="ada_update", grid=(L, D // tr),
        in_specs=[pl.BlockSpec((1, 8, 3 * D), lambda l, i: (l, 0, 0)), pl.BlockSpec((tr, 8), lambda l, i: (i, 0)),
                  blk, blk, blk],
        out_specs=[blk] * 4, out_shape=[jax.ShapeDtypeStruct(w.shape, F32)] * 4,
        compiler_params=_cp("arbitrary", "arbitrary"),
    )(dmods, c_t, w, m, v)


STAT_ROWS = 16


def small_update(stats_all, norm, b_ada, final):
    def body(s_ref, *refs):
        ins, outs = refs[:9], refs[9:]
        tot = s_ref[0]
        for k in range(1, 8):
            tot = tot + s_ref[k]
        g_norm = tot[0:2, :]
        g_final = tot[2:3, :]
        g_b = jnp.concatenate(
            [jnp.concatenate([tot[3 + 3 * l + t:4 + 3 * l + t, :] for t in range(3)], axis=1) for l in range(DEPTH)],
            axis=0)
        for p, g in enumerate((g_norm, g_b, g_final)):
            w_ref, m_ref, v_ref = ins[3 * p:3 * p + 3]
            d, m2, v2 = _adamw(w_ref[...], g, m_ref[...], v_ref[...])
            for o_ref, val in zip(outs[4 * p:4 * p + 4], (g, d, m2, v2)):
                o_ref[...] = val
        loss = (0.5 / D_MODEL) * jnp.sum(tot[9:10, :], axis=1, keepdims=True)
        outs[12][...] = jnp.broadcast_to(loss, (8, LANES))

    shapes = []
    for w, _, _ in (norm, b_ada, final):
        shapes += [jax.ShapeDtypeStruct(w.shape, F32)] * 4
    shapes.append(jax.ShapeDtypeStruct((8, LANES), F32))
    return pl.pallas_call(body, name="small_update", out_shape=shapes)(stats_all, *norm, *b_ada, *final)


def kernel(x, c, norm_g, w_ada, b_ada, w_in, w_out, final_g, loss_target, m_norm_g, m_w_ada, m_b_ada, m_w_in, m_w_out, m_final_g, v_norm_g, v_w_ada, v_b_ada, v_w_in, v_w_out, v_final_g):
    S, D = x.shape[1], x.shape[2]
    mc = lax.axis_index("c")
    out_rows = D // N_SHARD

    def my_half(a, rows):
        return lax.dynamic_slice_in_dim(a, mc * rows, rows, axis=0)

    wblk = jnp.concatenate([my_half(w_in[l], D // 2) for l in range(DEPTH)]
                           + [my_half(w_out[l], out_rows // 2) for l in range(DEPTH)], axis=0).astype(BF16)
    wall = allgather8(wblk, "gather_weights").reshape(N_SHARD, 2, wblk.shape[0], SHARD_W)
    w3 = [wall[:, :, l * (D // 2):(l + 1) * (D // 2)].reshape(N_SHARD, D, SHARD_W) for l in range(DEPTH)]
    wo = [wall[:, :, D + l * (out_rows // 2):D + (l + 1) * (out_rows // 2)].reshape(D, D) for l in range(DEPTH)]

    c_all = allgather8(jnp.broadcast_to(c, (8, D)), "gather_c")[:, 0, :]
    mod_all = allgather8(ada_fwd(c_all, w_ada).reshape(DEPTH * 8, -1), "gather_mod")
    vecs = vecs_build(mod_all, b_ada, norm_g)

    tabs = (*rope_tables(S), ret_log_gamma())
    h = x[0]
    saved = []
    for l in range(DEPTH):
        h, sv = layer_fwd(h, vecs[l], w3[l], wo[l], tabs)
        saved.append(sv)
    dx, st_loss = loss_head(h, jnp.broadcast_to(final_g[None, :], (8, D)), loss_target[0])

    gin, gout, dmod, dnorm = [None] * DEPTH, [None] * DEPTH, [None] * DEPTH, [None] * DEPTH
    for l in reversed(range(DEPTH)):
        dx, gin[l], g_out, dmod[l], dnorm[l] = layer_bwd(dx, saved[l], vecs[l], w3[l], wo[l], tabs)
        gout[l] = g_out.reshape(N_SHARD, out_rows, D)

    recv = grad_exchange(gin, gout)
    p_own = sum_slots(recv)
    p_sib = swap_sibling(p_own)
    res_in = adam_slab(p_own, p_sib, w_in, m_w_in, v_w_in, 0, "adam_w_in")
    res_out = adam_slab(p_own, p_sib, w_out, m_w_out, v_w_out, DEPTH * D, "adam_w_out", tr=128)

    stats = jnp.concatenate(dnorm + [st_loss[0:1]] + dmod + [st_loss[1:2], jnp.zeros((STAT_ROWS - 10, D), F32)], axis=0)
    stats_all = allgather8(stats, "gather_stats")
    dmods = stats_all[:, 3:9, :].reshape(8, DEPTH, 3 * D).transpose(1, 0, 2)
    res_ada = ada_update(dmods, c_all.T, w_ada, m_w_ada, v_w_ada)
    small = small_update(stats_all, (norm_g, m_norm_g, v_norm_g), (b_ada, m_b_ada, v_b_ada),
                         (final_g[None, :], m_final_g[None, :], v_final_g[None, :]))
    res_norm, res_b, res_final = small[0:4], small[4:8], [a[0] for a in small[8:12]]
    loss = small[12][0, 0]

    by_kind = [res_norm, res_ada, res_b, res_in, res_out, res_final]
    outs = [loss, dx[None]]
    for kind in range(4):
        outs += [r[kind] for r in by_kind]
    return tuple(outs)
```

```python
import functools

import numpy as np
import jax
import jax.numpy as jnp
from jax import lax
from jax.experimental import pallas as pl
from jax.experimental.pallas import tpu as pltpu

F32, BF16 = jnp.float32, jnp.bfloat16
MESH = pl.DeviceIdType.MESH

D_MODEL = 1024
DEPTH = 2
SHARD_W = 1024
N_SHARD = 4
GROUP_W = 512
LANES = 128
SB_HEAD_DIM = 64
RET_HEAD_DIM = 128
CHUNK = 64
ROPE_BASE = 10000.0
EPS = 1e-6
SQ_SCALE = SB_HEAD_DIM ** -0.5
RK_SCALE = RET_HEAD_DIM ** -0.5
SB_T = 256
RET_T = 256
EXP_ZERO = -104.0
VMEM_LIMIT_BYTES = 56 * 2 ** 20

ADAM_LR, ADAM_B1, ADAM_B2, ADAM_EPS, ADAM_WD, ADAM_STEP = 0.001, 0.9, 0.999, 1e-08, 0.01, 10


def _cp(*sem):
    return pltpu.CompilerParams(dimension_semantics=sem, vmem_limit_bytes=VMEM_LIMIT_BYTES)


def _dot(a, b):
    return lax.dot_general(a, b, (((1,), (0,)), ((), ())), preferred_element_type=F32)


def _dot_nt(a, b):
    return lax.dot_general(a, b, (((1,), (1,)), ((), ())), preferred_element_type=F32)


def _dot_tn(a, b):
    return lax.dot_general(a, b, (((0,), (0,)), ((), ())), preferred_element_type=F32)


def _dot_hilo(a, tri):
    hi = a.astype(BF16)
    lo = (a - hi.astype(F32)).astype(BF16)
    return _dot(hi, tri) + _dot(lo, tri)


def _sigmoid(x):
    return 1.0 / (1.0 + jnp.exp(-x))


def _rowsum(a):
    return jnp.sum(a, axis=1, keepdims=True)


def _rowmean(a):
    return jnp.mean(a, axis=1, keepdims=True)


def inproj_fwd(x, vecs, w3, tm=256):
    S, D = x.shape

    def body(x_ref, v_ref, w_ref, proj_ref, h_ref, sb_ref):
        xv = x_ref[...]
        r = lax.rsqrt(_rowmean(xv * xv) + EPS)
        h = xv * r * v_ref[3:4, :] * (1.0 + v_ref[1:2, :]) + v_ref[0:1, :]
        hb = h.astype(BF16)
        h_ref[...] = hb
        for s in range(N_SHARD):
            p = _dot(hb, w_ref[s])
            proj_ref[:, s * SHARD_W:(s + 1) * SHARD_W] = p
            if s == 2:
                sb_ref[:, 0:GROUP_W] = (p[:, 0:GROUP_W] * SQ_SCALE).astype(BF16)
                sb_ref[:, GROUP_W:SHARD_W] = p[:, GROUP_W:].astype(BF16)
            if s == 3:
                sb_ref[:, SHARD_W:2 * SHARD_W] = p.astype(BF16)

    return pl.pallas_call(
        body, name="inproj_fwd", grid=(S // tm,),
        in_specs=[pl.BlockSpec((tm, D), lambda i: (i, 0)),
                  pl.BlockSpec((8, D), lambda i: (0, 0)),
                  pl.BlockSpec((N_SHARD, D, SHARD_W), lambda i: (0, 0, 0))],
        out_specs=[pl.BlockSpec((tm, 4 * D), lambda i: (i, 0)),
                   pl.BlockSpec((tm, D), lambda i: (i, 0)),
                   pl.BlockSpec((tm, 2 * SHARD_W), lambda i: (i, 0))],
        out_shape=[jax.ShapeDtypeStruct((S, 4 * D), F32),
                   jax.ShapeDtypeStruct((S, D), BF16),
                   jax.ShapeDtypeStruct((S, 2 * SHARD_W), BF16)],
        compiler_params=_cp("arbitrary"),
    )(x, vecs, w3)


def _sb_logits(qh, k2, mask):
    z = _dot_nt(qh, k2)
    sp = jnp.log(1.0 + jnp.exp(-jnp.abs(z)))
    lb = jnp.minimum(z, 0.0) - sp
    lk = lb - z
    if mask is not None:
        lk = jnp.where(mask, lk, 0.0)
    return lb, lk


def _sb_masks():
    T = SB_T
    row = lax.broadcasted_iota(jnp.int32, (T, T), 0)
    col = lax.broadcasted_iota(jnp.int32, (T, T), 1)
    causal = col < row
    later = jnp.where(row > col, 1.0, 0.0).astype(BF16)
    earlier = jnp.where(row < col, 1.0, 0.0).astype(BF16)
    lane = lax.broadcasted_iota(jnp.int32, (1, LANES), 1)
    return causal, later, earlier, lane < SB_HEAD_DIM


def _sb_rows(ref, j):
    return ref[pl.ds(pl.multiple_of(j * SB_T, SB_T), SB_T), :]


def sb_fwd(sb, proj, gather=None):
    S = sb.shape[0]
    T = SB_T
    nq = S // T
    carried = [] if gather is None else [gather]

    def body(*refs):
        (q_ref, k_ref, v_ref, sg_ref), refs = refs[:4], refs[4:]
        p, i = pl.program_id(0), pl.program_id(1)
        if carried:
            x_ref, y_ref, o_ref, out_ref, send_sems, recv_sems, local_sem = refs
            start, forward, finish = _gather_ops(x_ref, out_ref, send_sems, recv_sems, local_sem)
            pl.when(jnp.logical_and(p == 0, i == 0))(start)
            pl.when(jnp.logical_and(p == 2, i == 0))(forward)
        else:
            y_ref, o_ref = refs
        causal, later, _, head0 = _sb_masks()
        q2 = q_ref[...]
        zero = jnp.zeros_like(q2)
        qs = jnp.concatenate([jnp.where(head0, q2, zero), jnp.where(head0, zero, q2)], axis=0)
        causal2 = jnp.concatenate([causal, causal], axis=0)

        def tile(j, R, mask):
            lb, lk = _sb_logits(qs, _sb_rows(k_ref, j), mask)
            a = jnp.exp(lb + _dot_hilo(lk, later) + R)
            if mask is not None:
                a = jnp.where(mask, a, 0.0)
            return _dot(a.astype(BF16), _sb_rows(v_ref, j)), R + _rowsum(lk)

        acc, R = tile(i, jnp.zeros((2 * T, 1), F32), causal2)

        def cond(st):
            return jnp.logical_and(st[0] >= 0, st[3] > EXP_ZERO)

        def step(st):
            c, Rn = tile(st[0], st[2], None)
            return st[0] - 1, st[1] + c, Rn, jnp.max(Rn)

        acc = lax.while_loop(cond, step, (i - 1, acc, R, jnp.max(R)))[1]
        o = jnp.where(head0, acc[:T], acc[T:])
        o_ref[...] = o
        sg = sg_ref[...]
        y_ref[...] = (o * (sg * _sigmoid(sg))).astype(BF16)
        if carried:
            pl.when(jnp.logical_and(p == 3, i == nq - 1))(finish)

    return pl.pallas_call(
        body, name="sb_fwd", grid=(4, nq),
        in_specs=[pl.BlockSpec((T, LANES), lambda p, i: (i, p)),
                  pl.BlockSpec((S, LANES), lambda p, i: (0, 4 + p)),
                  pl.BlockSpec((S, LANES), lambda p, i: (0, 8 + p)),
                  pl.BlockSpec((T, LANES), lambda p, i: (i, 28 + p))] + [_ANY for _ in carried],
        out_specs=[pl.BlockSpec((T, LANES), lambda p, i: (i, p)),
                   pl.BlockSpec((T, LANES), lambda p, i: (i, p))] + [_ANY for _ in carried],
        out_shape=[jax.ShapeDtypeStruct((S, GROUP_W), BF16),
                   jax.ShapeDtypeStruct((S, GROUP_W), F32)]
        + [jax.ShapeDtypeStruct((8,) + a.shape, a.dtype) for a in carried],
        scratch_shapes=_GATHER_SCRATCH if carried else [],
        compiler_params=_cp("arbitrary", "arbitrary"),
    )(sb, sb, sb, proj, *carried)


def sb_bwd(sb, proj, o, dycat, ship=None):
    S = sb.shape[0]
    T = SB_T
    nq = S // T
    ex = _Exchange(ship)

    def body(*refs):
        (q_ref, k_ref, v_ref, sg_ref, o_ref, dy_ref), refs = refs[:6], refs[6:]
        ship_refs, (dq_ref, dk_ref, dv_ref, dsg_ref), refs = refs[:ex.n_in], refs[ex.n_in:ex.n_in + 4], refs[ex.n_in + 4:]
        recv, (dk_acc, dv_acc, stick), sems = refs[:ex.n_out], refs[ex.n_out:ex.n_out + 3], refs[ex.n_out + 3:]
        start, finish = ex.ops(ship_refs, recv + sems)
        p, i = pl.program_id(0), pl.program_id(1)
        pl.when(jnp.logical_and(p == 0, i == 0))(start)

        @pl.when(i == 0)
        def _():
            dk_acc[...] = jnp.zeros_like(dk_acc)
            dv_acc[...] = jnp.zeros_like(dv_acc)

        causal, later, earlier, head0 = _sb_masks()
        q2 = q_ref[...]
        sg = sg_ref[...]
        sig = _sigmoid(sg)
        dy = dy_ref[...]
        dsg_ref[...] = (dy * o_ref[...] * (sig * (1.0 + sg * (1.0 - sig)))).astype(BF16)
        do_b = (dy * (sg * sig)).astype(BF16)
        zero = jnp.zeros_like(q2)
        qs = jnp.concatenate([jnp.where(head0, q2, zero), jnp.where(head0, zero, q2)], axis=0)
        dos = jnp.concatenate([jnp.where(head0, do_b, zero), jnp.where(head0, zero, do_b)], axis=0)
        causal2 = jnp.concatenate([causal, causal], axis=0)

        def advance(j, R, mask):
            stick[j] = R
            return R + _rowsum(_sb_logits(qs, _sb_rows(k_ref, j), mask)[1])

        r_zero = jnp.zeros((2 * T, 1), F32)
        R = advance(i, r_zero, causal2)

        def cond(st):
            return jnp.logical_and(st[0] >= 0, st[2] > EXP_ZERO)

        def step(st):
            Rn = advance(st[0], st[1], None)
            return st[0] - 1, Rn, jnp.max(Rn)

        j_end = lax.while_loop(cond, step, (i - 1, R, jnp.max(R)))[0]

        def tile(j, G0, mask):
            k2, v2 = _sb_rows(k_ref, j), _sb_rows(v_ref, j)
            lb, lk = _sb_logits(qs, k2, mask)
            a = jnp.exp(lb + _dot_hilo(lk, later) + stick[j])
            if mask is not None:
                a = jnp.where(mask, a, 0.0)
            g = a * _dot_nt(dos, v2)
            G = _dot_hilo(g, earlier) + G0
            dz = g - jnp.exp(lb) * (g + G)
            if mask is not None:
                dz = jnp.where(mask, dz, 0.0)
            dzb = dz.astype(BF16)
            rows = pl.ds(pl.multiple_of(j * T, T), T)
            dk_acc[rows, :] += _dot_tn(dzb, qs)
            dv_acc[rows, :] += _dot_tn(a.astype(BF16), dos)
            return _dot(dzb, k2), G0 + _rowsum(g)

        def sweep(j, st):
            dq_t, Gn = tile(j, st[1], None)
            return st[0] + dq_t, Gn

        st = lax.fori_loop(j_end + 1, i, sweep, (jnp.zeros((2 * T, LANES), F32), r_zero))
        dq = st[0] + tile(i, st[1], causal2)[0]
        dq_ref[...] = (jnp.where(head0, dq[:T], dq[T:]) * SQ_SCALE).astype(BF16)

        @pl.when(i == nq - 1)
        def _():
            dk_ref[...] = dk_acc[...].astype(BF16)
            dv_ref[...] = dv_acc[...].astype(BF16)

        pl.when(jnp.logical_and(p == 3, i == nq - 1))(finish)

    tile_spec = lambda c0: pl.BlockSpec((T, LANES), lambda p, i: (i, c0 + p))
    head_spec = lambda c0: pl.BlockSpec((S, LANES), lambda p, i: (0, c0 + p))
    return pl.pallas_call(
        body, name="sb_bwd", grid=(4, nq),
        in_specs=[tile_spec(0), head_spec(4), head_spec(8), tile_spec(28), tile_spec(0), tile_spec(4)] + ex.in_specs,
        out_specs=[tile_spec(0), head_spec(0), head_spec(0), tile_spec(0)] + ex.out_specs,
        out_shape=[jax.ShapeDtypeStruct((S, GROUP_W), BF16)] * 4 + ex.out_shape,
        scratch_shapes=[pltpu.VMEM((S, LANES), F32), pltpu.VMEM((S, LANES), F32),
                        pltpu.VMEM((nq, 2 * T, 1), F32)] + ex.scratch,
        compiler_params=_cp("arbitrary", "arbitrary"),
    )(sb, sb, sb, proj, o, dycat, *ex.ship)


def rope_tables(S):
    half = RET_HEAD_DIM // 2
    inv = ROPE_BASE ** (-jnp.arange(half, dtype=F32) / half)
    ang = jnp.arange(S, dtype=F32)[:, None] * inv[None, :]
    cos, sin = jnp.cos(ang), jnp.sin(ang)
    return jnp.concatenate([cos, cos], axis=1), jnp.concatenate([-sin, sin], axis=1)


def ret_log_gamma():
    return jnp.log1p(-(2.0 ** (-5.0 - jnp.arange(4, dtype=F32))))


def _swap_halves(a):
    return pltpu.roll(a, RET_HEAD_DIM // 2, axis=1)


def _ret_decay_mask(lg):
    n = lax.broadcasted_iota(jnp.int32, (RET_T, RET_T), 0)
    m = lax.broadcasted_iota(jnp.int32, (RET_T, RET_T), 1)
    dist = jnp.abs(n - m).astype(F32)
    return jnp.where((m // CHUNK) <= (n // CHUNK), jnp.exp(lg * dist), 0.0)


def _ret_block(lg, rq, rk, rv, cosf, sinf, dm):
    q = rq * cosf + _swap_halves(rq) * sinf
    k = (rk * cosf + _swap_halves(rk) * sinf) * RK_SCALE
    qb, kb, vb = q.astype(BF16), k.astype(BF16), rv.astype(BF16)
    sc = _dot_nt(qb, kb) * dm
    nloc = lax.broadcasted_iota(jnp.int32, (RET_T, 1), 0).astype(F32)
    qdec = jnp.exp(lg * (nloc + 1.0))
    kdec = jnp.exp(lg * (RET_T - 1.0 - nloc))
    block_dec = jnp.exp(jnp.full((1, LANES), lg * RET_T, F32))
    return q, k, qb, kb, vb, sc, qdec, kdec, block_dec


def _ret_specs(S, rb):
    group = lambda c0: pl.BlockSpec((RET_T, GROUP_W), lambda b: (rb(b), c0))
    return group, pl.BlockSpec((RET_T, LANES), lambda b: (rb(b), 0))


def _head(ref, h):
    return ref[:, h * LANES:(h + 1) * LANES]


def ret_fwd(proj, cosf, sinf, lgam):
    S = proj.shape[0]
    nb = S // RET_T
    group, row_tab = _ret_specs(S, lambda b: b)

    def body(lg_ref, rq_ref, rk_ref, rv_ref, rg_ref, cos_ref, sin_ref, y_ref, o_ref, st_out, st_ref, dm_ref):
        @pl.when(pl.program_id(0) == 0)
        def _():
            st_ref[...] = jnp.zeros_like(st_ref)
            for h in range(4):
                dm_ref[h] = _ret_decay_mask(lg_ref[h])

        cosf, sinf = cos_ref[...], sin_ref[...]
        for h in range(4):
            lg = lg_ref[h]
            q, k, qb, kb, vb, sc, qdec, kdec, block_dec = _ret_block(
                lg, _head(rq_ref, h), _head(rk_ref, h), _head(rv_ref, h), cosf, sinf, dm_ref[h])
            st = st_ref[h]
            st_out[h, 0] = st
            o = _dot(sc.astype(BF16), vb) + _dot(qb, st.astype(BF16)) * qdec
            st_ref[h] = st * block_dec + _dot_tn((k * kdec).astype(BF16), vb)
            o_ref[:, h * LANES:(h + 1) * LANES] = o
            cen = o - _rowmean(o)
            on = cen * lax.rsqrt(_rowmean(cen * cen) + EPS)
            rg = _head(rg_ref, h)
            y_ref[:, h * LANES:(h + 1) * LANES] = (on * (rg * _sigmoid(rg))).astype(BF16)

    return pl.pallas_call(
        body, name="ret_fwd", grid=(nb,),
        in_specs=[pl.BlockSpec(memory_space=pltpu.SMEM),
                  group(0), group(1), group(2), group(3), row_tab, row_tab],
        out_specs=[group(0), group(0),
                   pl.BlockSpec((4, 1, LANES, LANES), lambda b: (0, b, 0, 0))],
        out_shape=[jax.ShapeDtypeStruct((S, GROUP_W), BF16),
                   jax.ShapeDtypeStruct((S, GROUP_W), F32),
                   jax.ShapeDtypeStruct((4, nb, LANES, LANES), F32)],
        scratch_shapes=[pltpu.VMEM((4, LANES, LANES), F32), pltpu.VMEM((4, RET_T, RET_T), F32)],
        compiler_params=_cp("arbitrary"),
    )(lgam, proj, proj, proj, proj, cosf, sinf)


def ret_bwd(proj, cosf, sinf, lgam, o, states, dycat):
    S = proj.shape[0]
    nb = S // RET_T
    rev = lambda b: nb - 1 - b
    group, row_tab = _ret_specs(S, rev)

    def body(lg_ref, rq_ref, rk_ref, rv_ref, rg_ref, cos_ref, sin_ref, o_ref, st_in, dy_ref,
             drq_ref, drk_ref, drv_ref, drg_ref, ds_ref, dm_ref):
        @pl.when(pl.program_id(0) == 0)
        def _():
            ds_ref[...] = jnp.zeros_like(ds_ref)
            for h in range(4):
                dm_ref[h] = _ret_decay_mask(lg_ref[h])

        cosf, sinf = cos_ref[...], sin_ref[...]
        for h in range(4):
            lanes = slice(h * LANES, (h + 1) * LANES)
            dm = dm_ref[h]
            q, k, qb, kb, vb, sc, qdec, kdec, block_dec = _ret_block(
                lg_ref[h], _head(rq_ref, h), _head(rk_ref, h), _head(rv_ref, h), cosf, sinf, dm)
            o_v = _head(o_ref, h)
            cen = o_v - _rowmean(o_v)
            rstd = lax.rsqrt(_rowmean(cen * cen) + EPS)
            on = cen * rstd
            rg = _head(rg_ref, h)
            sig = _sigmoid(rg)
            dy = _head(dy_ref, h)
            drg_ref[:, lanes] = (dy * on * (sig * (1.0 + rg * (1.0 - sig)))).astype(BF16)
            don = dy * (rg * sig)
            do = rstd * (don - _rowmean(don) - on * _rowmean(don * on))
            dob = do.astype(BF16)
            dsc = (_dot_nt(dob, vb) * dm).astype(BF16)
            st_b = st_in[h, 0].astype(BF16)
            dsn = ds_ref[h]
            dsn_b = dsn.astype(BF16)
            dq = _dot(dsc, kb) + _dot_nt(dob, st_b) * qdec
            dk = _dot_tn(dsc, qb) + _dot_nt(vb, dsn_b) * kdec
            dv = _dot_tn(sc.astype(BF16), dob) + _dot((k * kdec).astype(BF16), dsn_b)
            ds_ref[h] = dsn * block_dec + _dot_tn((q * qdec).astype(BF16), dob)
            dk = dk * RK_SCALE
            drq_ref[:, lanes] = (dq * cosf + _swap_halves(dq * sinf)).astype(BF16)
            drk_ref[:, lanes] = (dk * cosf + _swap_halves(dk * sinf)).astype(BF16)
            drv_ref[:, lanes] = dv.astype(BF16)

    return pl.pallas_call(
        body, name="ret_bwd", grid=(nb,),
        in_specs=[pl.BlockSpec(memory_space=pltpu.SMEM),
                  group(0), group(1), group(2), group(3), row_tab, row_tab,
                  group(0), pl.BlockSpec((4, 1, LANES, LANES), lambda b: (0, rev(b), 0, 0)),
                  group(0)],
        out_specs=[group(0)] * 4,
        out_shape=[jax.ShapeDtypeStruct((S, GROUP_W), BF16)] * 4,
        scratch_shapes=[pltpu.VMEM((4, LANES, LANES), F32), pltpu.VMEM((4, RET_T, RET_T), F32)],
        compiler_params=_cp("arbitrary"),
    )(lgam, proj, proj, proj, proj, cosf, sinf, o, states, dycat)


def outproj_fwd(x, vecs, y_ret, y_sb, w_out, tm=512):
    S, D = x.shape

    def body(x_ref, v_ref, yr_ref, ys_ref, w_ref, y_ref, xo_ref):
        y = _dot(yr_ref[...], w_ref[0:GROUP_W, :]) + _dot(ys_ref[...], w_ref[GROUP_W:, :])
        y_ref[...] = y
        xo_ref[...] = x_ref[...] + v_ref[2:3, :] * y

    row = lambda w: pl.BlockSpec((tm, w), lambda i: (i, 0))
    return pl.pallas_call(
        body, name="outproj_fwd", grid=(S // tm,),
        in_specs=[row(D), pl.BlockSpec((8, D), lambda i: (0, 0)), row(GROUP_W), row(GROUP_W),
                  pl.BlockSpec((D, D), lambda i: (0, 0))],
        out_specs=[row(D), row(D)],
        out_shape=[jax.ShapeDtypeStruct((S, D), F32)] * 2,
        compiler_params=_cp("arbitrary"),
    )(x, vecs, y_ret, y_sb, w_out)


def loss_head(x, final_g, target, tm=512):
    S, D = x.shape

    def body(x_ref, g_ref, t_ref, dx_ref, st_ref):
        @pl.when(pl.program_id(0) == 0)
        def _():
            st_ref[...] = jnp.zeros_like(st_ref)

        xv = x_ref[...]
        g = g_ref[0:1, :]
        r = lax.rsqrt(_rowmean(xv * xv) + EPS)
        xn = xv * r
        err = xn * g - t_ref[...]
        dy = err * (1.0 / D)
        dxn = dy * g
        dx_ref[...] = r * (dxn - xn * _rowmean(dxn * xn))
        st_ref[0:1, :] += jnp.sum(dy * xn, axis=0, keepdims=True)
        st_ref[1:2, :] += jnp.sum(err * err, axis=0, keepdims=True)

    row = pl.BlockSpec((tm, D), lambda i: (i, 0))
    fixed = pl.BlockSpec((8, D), lambda i: (0, 0))
    return pl.pallas_call(
        body, name="loss_head", grid=(S // tm,),
        in_specs=[row, fixed, row], out_specs=[row, fixed],
        out_shape=[jax.ShapeDtypeStruct((S, D), F32), jax.ShapeDtypeStruct((8, D), F32)],
        compiler_params=_cp("arbitrary"),
    )(x, final_g, target)


def outproj_bwd(dx, y, vecs, y_ret, y_sb, w_out, tm=512):
    S, D = dx.shape
    n = S // tm

    def body(dx_ref, y_ref, v_ref, yr_ref, ys_ref, w_ref, dyc_ref, dw_ref, st_ref, acc):
        i = pl.program_id(0)

        @pl.when(i == 0)
        def _():
            st_ref[...] = jnp.zeros_like(st_ref)
            acc[...] = jnp.zeros_like(acc)

        dxv = dx_ref[...]
        st_ref[0:1, :] += jnp.sum(dxv * y_ref[...], axis=0, keepdims=True)
        dyy = (dxv * v_ref[2:3, :]).astype(BF16)
        dyc_ref[...] = _dot_nt(dyy, w_ref[...])
        acc[0:GROUP_W, :] += _dot_tn(yr_ref[...], dyy)
        acc[GROUP_W:, :] += _dot_tn(ys_ref[...], dyy)

        @pl.when(i == n - 1)
        def _():
            dw_ref[...] = acc[...].astype(BF16)

    row = lambda w: pl.BlockSpec((tm, w), lambda i: (i, 0))
    fixed = lambda r: pl.BlockSpec((r, D), lambda i: (0, 0))
    return pl.pallas_call(
        body, name="outproj_bwd", grid=(n,),
        in_specs=[row(D), row(D), fixed(8), row(GROUP_W), row(GROUP_W), fixed(D)],
        out_specs=[row(D), fixed(D), fixed(8)],
        out_shape=[jax.ShapeDtypeStruct((S, D), F32), jax.ShapeDtypeStruct((D, D), BF16),
                   jax.ShapeDtypeStruct((8, D), F32)],
        scratch_shapes=[pltpu.VMEM((D, D), F32)],
        compiler_params=_cp("arbitrary"),
    )(dx, y, vecs, y_ret, y_sb, w_out)


def inproj_bwd_x(pieces, w3, x, vecs, dx_res, ship=None, tm=256):
    S, D = x.shape
    n = S // tm
    ex = _Exchange(ship)

    def body(*refs):
        p_refs, (w_ref, x_ref, v_ref, dr_ref), refs = refs[:8], refs[8:12], refs[12:]
        ship_refs, (dx_ref, st_ref), refs = refs[:ex.n_in], refs[ex.n_in:ex.n_in + 2], refs[ex.n_in + 2:]
        start, finish = ex.ops(ship_refs, refs)

        @pl.when(pl.program_id(0) == 0)
        def _():
            st_ref[...] = jnp.zeros_like(st_ref)
            start()

        dh = jnp.zeros((tm, D), F32)
        for k, p_ref in enumerate(p_refs):
            c0 = (k % 2) * GROUP_W
            dh = dh + _dot_nt(p_ref[...], w_ref[k // 2, :, c0:c0 + GROUP_W])
        xv = x_ref[...]
        r = lax.rsqrt(_rowmean(xv * xv) + EPS)
        xn = xv * r
        g, scale1 = v_ref[3:4, :], 1.0 + v_ref[1:2, :]
        st_ref[0:1, :] += jnp.sum(dh, axis=0, keepdims=True)
        dh_xn = dh * xn
        st_ref[1:2, :] += jnp.sum(dh_xn, axis=0, keepdims=True) * g
        st_ref[2:3, :] += jnp.sum(dh_xn, axis=0, keepdims=True) * scale1
        dxn = dh * (g * scale1)
        dx_ref[...] = r * (dxn - xn * _rowmean(dxn * xn)) + dr_ref[...]
        pl.when(pl.program_id(0) == n - 1)(finish)

    row = lambda w: pl.BlockSpec((tm, w), lambda i: (i, 0))
    return pl.pallas_call(
        body, name="inproj_bwd_x", grid=(n,),
        in_specs=[row(GROUP_W)] * 8 + [pl.BlockSpec((N_SHARD, D, SHARD_W), lambda i: (0, 0, 0)),
                                       row(D), pl.BlockSpec((8, D), lambda i: (0, 0)), row(D)] + ex.in_specs,
        out_specs=[row(D), pl.BlockSpec((8, D), lambda i: (0, 0))] + ex.out_specs,
        out_shape=[jax.ShapeDtypeStruct((S, D), F32), jax.ShapeDtypeStruct((8, D), F32)] + ex.out_shape,
        scratch_shapes=ex.scratch,
        compiler_params=_cp("arbitrary"),
    )(*pieces, w3, x, vecs, dx_res, *ex.ship)


def inproj_bwd_w(h, pieces, tm=512):
    S, D = h.shape
    n = S // tm

    def body(*refs):
        h_ref, p_refs, dw_ref, acc = refs[0], refs[1:9], refs[9], refs[10]
        i = pl.program_id(0)

        @pl.when(i == 0)
        def _():
            acc[...] = jnp.zeros_like(acc)

        hv = h_ref[...]
        for k, p_ref in enumerate(p_refs):
            c0 = (k % 2) * GROUP_W
            acc[k // 2, :, c0:c0 + GROUP_W] += _dot_tn(hv, p_ref[...])

        @pl.when(i == n - 1)
        def _():
            dw_ref[...] = acc[...].astype(BF16)

    row = lambda w: pl.BlockSpec((tm, w), lambda i: (i, 0))
    return pl.pallas_call(
        body, name="inproj_bwd_w", grid=(n,),
        in_specs=[row(D)] + [row(GROUP_W)] * 8,
        out_specs=pl.BlockSpec((N_SHARD, D, SHARD_W), lambda i: (0, 0, 0)),
        out_shape=jax.ShapeDtypeStruct((N_SHARD, D, SHARD_W), BF16),
        scratch_shapes=[pltpu.VMEM((N_SHARD, D, SHARD_W), F32)],
        compiler_params=_cp("arbitrary"),
    )(h, *pieces)


def layer_fwd(x, vecs, w3, w_out, tabs, gather=None):
    cosf, sinf, lgam = tabs
    proj, h, sb = inproj_fwd(x, vecs, w3)
    y_ret, o_ret, states = ret_fwd(proj, cosf, sinf, lgam)
    y_sb, o_sb, *gathered = sb_fwd(sb, proj, gather)
    y, x_next = outproj_fwd(x, vecs, y_ret, y_sb, w_out)
    return x_next, (x, proj, h, sb, y_ret, o_ret, states, y_sb, o_sb, y), (gathered[0] if gathered else None)


def _by_shard(dw_out):
    return dw_out.reshape(N_SHARD, D_MODEL // N_SHARD, D_MODEL)


def layer_bwd(dx, saved, vecs, w3, w_out, tabs, later_grads=None):
    cosf, sinf, lgam = tabs
    x, proj, h, sb, y_ret, o_ret, states, y_sb, o_sb, y = saved
    dycat, dw_out, st_o = outproj_bwd(dx, y, vecs, y_ret, y_sb, w_out)
    dw_out = _by_shard(dw_out)
    ship = None if later_grads is None else (later_grads[0], dw_out, later_grads[1])
    *d_sb, = sb_bwd(sb, proj, o_sb, dycat, ship)
    d_ret = ret_bwd(proj, cosf, sinf, lgam, o_ret, states, dycat)
    pieces = list(d_ret) + d_sb[:4]
    dw_in = inproj_bwd_w(h, pieces)
    dx, st_i, *recv_in = inproj_bwd_x(pieces, w3, x, vecs, dx, None if later_grads is None else (dw_in,))
    dmod = jnp.concatenate([st_i[0:2], st_o[0:1]], axis=0)
    grads = (dw_in, dw_out) if later_grads is None else (recv_in[0], d_sb[4])
    return dx, dmod, st_i[2:3], grads


def _place():
    return lax.axis_index("x"), lax.axis_index("y"), lax.axis_index("c")


def _other_chips(mx, my):
    return [(1 - mx, my), (mx, 1 - my), (1 - mx, 1 - my)]


_ANY = pl.BlockSpec(memory_space=pl.ANY)


_GATHER_SCRATCH = [pltpu.SemaphoreType.DMA((7,)), pltpu.SemaphoreType.DMA((7,)), pltpu.SemaphoreType.DMA(())]


def _gather_ops(x_ref, out_ref, send_sems, recv_sems, local_sem):
    mx, my, mc = _place()
    me, sibling = (mx, my, mc), (mx, my, 1 - mc)
    chips = _other_chips(mx, my)

    def slot(px, py, pc):
        return out_ref.at[4 * px + 2 * py + pc]

    def copy(k, block, to, src=None):
        return pltpu.make_async_remote_copy(
            src_ref=slot(*block) if src is None else src, dst_ref=slot(*block),
            send_sem=send_sems.at[k], recv_sem=recv_sems.at[k], device_id=to, device_id_type=MESH)

    mine = pltpu.make_async_copy(x_ref, slot(*me), local_sem)
    first = [copy(0, me, sibling, src=x_ref)]
    first += [copy(1 + j, me, (*chip, mc), src=x_ref) for j, chip in enumerate(chips)]
    passed = [copy(4 + j, (*chip, mc), sibling) for j, chip in enumerate(chips)]

    def start():
        mine.start()
        for cp in first:
            cp.start()

    def forward():
        for j, chip in enumerate(chips):
            copy(1 + j, (*chip, mc), me).wait_recv()
            passed[j].start()

    def finish():
        copy(0, sibling, me).wait_recv()
        for j, chip in enumerate(chips):
            copy(4 + j, (*chip, 1 - mc), me).wait_recv()
        for cp in first + passed:
            cp.wait_send()
        mine.wait()

    return start, forward, finish


def allgather8(x, name):
    def body(x_ref, out_ref, send_sems, recv_sems, local_sem):
        for step in _gather_ops(x_ref, out_ref, send_sems, recv_sems, local_sem):
            step()

    return pl.pallas_call(
        body, name=name, out_shape=jax.ShapeDtypeStruct((8,) + x.shape, x.dtype),
        in_specs=[_ANY], out_specs=_ANY, scratch_shapes=_GATHER_SCRATCH,
    )(x)


class _Exchange:
    def __init__(self, ship):
        self.ship = list(ship or ())
        self.n_in = len(self.ship)
        self.n_out = 1 if self.ship else 0
        self.rows = [a.shape[1] for a in self.ship]
        self.in_specs = [_ANY] * self.n_in
        self.out_specs = [_ANY] * self.n_out
        self.out_shape = [jax.ShapeDtypeStruct((N_SHARD, sum(self.rows), SHARD_W), BF16)] * self.n_out
        sem = pltpu.SemaphoreType.DMA
        self.scratch = [sem((3,)), sem((3,)), sem(())] * self.n_out

    def ops(self, ship_refs, tail):
        if not self.ship:
            return (lambda: None), (lambda: None)
        recv, send_sems, recv_sems, local_sem = tail
        mx, my, mc = _place()
        my_chip = 2 * mx + my
        chips = _other_chips(mx, my)

        def pieces(s):
            firsts = np.cumsum([0] + self.rows[:-1])
            return [(ref.at[s], int(r0), n) for ref, r0, n in zip(ship_refs, firsts, self.rows)]

        def start():
            for src, r0, n in pieces(my_chip):
                pltpu.make_async_copy(src, recv.at[my_chip, pl.ds(r0, n)], local_sem).start()
            for j, (px, py) in enumerate(chips):
                for src, r0, n in pieces(2 * px + py):
                    pltpu.make_async_remote_copy(
                        src_ref=src, dst_ref=recv.at[my_chip, pl.ds(r0, n)],
                        send_sem=send_sems.at[j], recv_sem=recv_sems.at[j],
                        device_id=(px, py, mc), device_id_type=MESH).start()

        def finish():
            for j, (px, py) in enumerate(chips):
                whole = recv.at[2 * px + py]
                both = pltpu.make_async_remote_copy(
                    src_ref=whole, dst_ref=whole, send_sem=send_sems.at[j], recv_sem=recv_sems.at[j],
                    device_id=(px, py, mc), device_id_type=MESH)
                both.wait_recv()
                both.wait_send()
            pltpu.make_async_copy(recv.at[my_chip], recv.at[my_chip], local_sem).wait()

        return start, finish


def sum_slots(recv_a, recv_b, tr=256):
    n, rows_a, cols = recv_a.shape
    na, nb = rows_a // tr, recv_b.shape[1] // tr

    def body(a_ref, b_ref, o_ref):
        def total(r_ref):
            acc = r_ref[0].astype(F32)
            for k in range(1, n):
                acc = acc + r_ref[k].astype(F32)
            o_ref[...] = acc

        pl.when(pl.program_id(0) < na)(lambda: total(a_ref))
        pl.when(pl.program_id(0) >= na)(lambda: total(b_ref))

    return pl.pallas_call(
        body, name="sum_slots", grid=(na + nb,),
        in_specs=[pl.BlockSpec((n, tr, cols), lambda i: (0, jnp.minimum(i, na - 1), 0)),
                  pl.BlockSpec((n, tr, cols), lambda i: (0, jnp.maximum(i - na, 0), 0))],
        out_specs=pl.BlockSpec((tr, cols), lambda i: (i, 0)),
        out_shape=jax.ShapeDtypeStruct(((na + nb) * tr, cols), F32),
        compiler_params=_cp("arbitrary"),
    )(recv_a, recv_b)


def swap_sibling(p):
    def body(p_ref, out_ref, send_sem, recv_sem):
        mx, my, mc = _place()
        cp = pltpu.make_async_remote_copy(
            src_ref=p_ref, dst_ref=out_ref, send_sem=send_sem, recv_sem=recv_sem,
            device_id=(mx, my, 1 - mc), device_id_type=MESH)
        cp.start()
        cp.wait()

    return pl.pallas_call(
        body, name="swap_sibling", out_shape=jax.ShapeDtypeStruct(p.shape, p.dtype),
        in_specs=[_ANY], out_specs=_ANY,
        scratch_shapes=[pltpu.SemaphoreType.DMA(()), pltpu.SemaphoreType.DMA(())],
    )(p)


def _adamw(w, g, m, v):
    m = ADAM_B1 * m + (1.0 - ADAM_B1) * g
    v = ADAM_B2 * v + (1.0 - ADAM_B2) * (g * g)
    m_hat = m / (1.0 - ADAM_B1 ** ADAM_STEP)
    v_hat = v / (1.0 - ADAM_B2 ** ADAM_STEP)
    delta = -ADAM_LR * (m_hat / (jnp.sqrt(v_hat) + ADAM_EPS) + ADAM_WD * w)
    return delta, m, v


def adam_slab(p_own, p_sib, w, m, v, row0, name, tr=256):
    L, R, C = w.shape
    nr = R // tr

    def body(a_ref, b_ref, w_ref, m_ref, v_ref, g_out, d_out, m_out, v_out):
        g = a_ref[...] + b_ref[...]
        d, m2, v2 = _adamw(w_ref[0], g, m_ref[0], v_ref[0])
        g_out[0], d_out[0], m_out[0], v_out[0] = g, d, m2, v2

    slab = pl.BlockSpec((tr, C), lambda l, i: (row0 // tr + l * nr + i, 0))
    blk = pl.BlockSpec((1, tr, C), lambda l, i: (l, i, 0))
    return pl.pallas_call(
        body, name=name, grid=(L, nr),
        in_specs=[slab, slab, blk, blk, blk], out_specs=[blk] * 4,
        out_shape=[jax.ShapeDtypeStruct(w.shape, F32)] * 4,
        compiler_params=_cp("arbitrary", "arbitrary"),
    )(p_own, p_sib, w, m, v)


def ada_fwd(c_all, w_ada):
    L, D, W = w_ada.shape

    def body(c_ref, w_ref, o_ref):
        cv = c_ref[...]
        o_ref[0] = jnp.dot(cv * _sigmoid(cv), w_ref[0], precision=lax.Precision.HIGHEST,
                           preferred_element_type=F32)

    return pl.pallas_call(
        body, name="ada_fwd", grid=(L,),
        in_specs=[pl.BlockSpec((8, D), lambda l: (0, 0)), pl.BlockSpec((1, D, W), lambda l: (l, 0, 0))],
        out_specs=pl.BlockSpec((1, 8, W), lambda l: (l, 0, 0)),
        out_shape=jax.ShapeDtypeStruct((L, 8, W), F32),
        compiler_params=_cp("arbitrary"),
    )(c_all, w_ada)


def vecs_build(mod_all, b_ada, norm_g):
    W = mod_all.shape[2]

    def body(m_ref, b_ref, g_ref, o_ref):
        mx, my, mc = _place()
        me = 4 * mx + 2 * my + mc
        rowid = lax.broadcasted_iota(jnp.int32, (2 * 8, 1), 0)
        o_ref[...] = jnp.zeros_like(o_ref)
        for l in range(DEPTH):
            parts = [jnp.sum(jnp.where(rowid == l * 8 + me, m_ref[2 * s + mc], 0.0), axis=0, keepdims=True)
                     for s in range(N_SHARD)]
            mod = jnp.concatenate(parts, axis=1) + b_ref[l:l + 1, :]
            for t in range(3):
                o_ref[l, t:t + 1, :] = mod[:, t * D_MODEL:(t + 1) * D_MODEL]
            o_ref[l, 3:4, :] = g_ref[l:l + 1, :]

    return pl.pallas_call(
        body, name="vecs_build", out_shape=jax.ShapeDtypeStruct((DEPTH, 8, D_MODEL), F32),
    )(mod_all, b_ada, norm_g)


def ada_update(dmods, c_t, w, m, v, tr=256):
    L, D, W = w.shape

    def body(dm_ref, c_ref, w_ref, m_ref, v_ref, g_out, d_out, m_out, v_out):
        mx, my, _ = _place()
        shard = 2 * mx + my
        dm = jnp.zeros((8, W), F32)
        for s in range(N_SHARD):
            dm = dm + jnp.where(shard == s, dm_ref[0, :, s * W:(s + 1) * W], 0.0)
        cv = c_ref[...]
        ca = cv * _sigmoid(cv)
        g = jnp.zeros((tr, W), F32)
        for b in range(8):
            g = g + ca[:, b:b + 1] * dm[b:b + 1, :]
        d, m2, v2 = _adamw(w_ref[0], g, m_ref[0], v_ref[0])
        g_out[0], d_out[0], m_out[0], v_out[0] = g, d, m2, v2

    blk = pl.BlockSpec((1, tr, W), lambda l, i: (l, i, 0))
    return pl.pallas_call(
        body, name="ada_update", grid=(L, D // tr),
        in_specs=[pl.BlockSpec((1, 8, 3 * D), lambda l, i: (l, 0, 0)), pl.BlockSpec((tr, 8), lambda l, i: (i, 0)),
                  blk, blk, blk],
        out_specs=[blk] * 4, out_shape=[jax.ShapeDtypeStruct(w.shape, F32)] * 4,
        compiler_params=_cp("arbitrary", "arbitrary"),
    )(dmods, c_t, w, m, v)


STAT_ROWS = 16


def small_update(stats_all, norm, b_ada, final):
    def body(s_ref, *refs):
        ins, outs = refs[:9], refs[9:]
        tot = s_ref[0]
        for k in range(1, 8):
            tot = tot + s_ref[k]
        g_norm = tot[0:2, :]
        g_final = tot[2:3, :]
        g_b = jnp.concatenate(
            [jnp.concatenate([tot[3 + 3 * l + t:4 + 3 * l + t, :] for t in range(3)], axis=1) for l in range(DEPTH)],
            axis=0)
        for p, g in enumerate((g_norm, g_b, g_final)):
            w_ref, m_ref, v_ref = ins[3 * p:3 * p + 3]
            d, m2, v2 = _adamw(w_ref[...], g, m_ref[...], v_ref[...])
            for o_ref, val in zip(outs[4 * p:4 * p + 4], (g, d, m2, v2)):
                o_ref[...] = val
        loss = (0.5 / D_MODEL) * jnp.sum(tot[9:10, :], axis=1, keepdims=True)
        outs[12][...] = jnp.broadcast_to(loss, (8, LANES))

    shapes = []
    for w, _, _ in (norm, b_ada, final):
        shapes += [jax.ShapeDtypeStruct(w.shape, F32)] * 4
    shapes.append(jax.ShapeDtypeStruct((8, LANES), F32))
    return pl.pallas_call(body, name="small_update", out_shape=shapes)(stats_all, *norm, *b_ada, *final)


def kernel(x, c, norm_g, w_ada, b_ada, w_in, w_out, final_g, loss_target, m_norm_g, m_w_ada, m_b_ada, m_w_in, m_w_out, m_final_g, v_norm_g, v_w_ada, v_b_ada, v_w_in, v_w_out, v_final_g):
    S, D = x.shape[1], x.shape[2]
    mc = lax.axis_index("c")
    out_rows = D // N_SHARD

    def my_half(a, rows):
        return lax.dynamic_slice_in_dim(a, mc * rows, rows, axis=0)

    wblk = [jnp.concatenate([my_half(w_in[l], D // 2), my_half(w_out[l], out_rows // 2)], axis=0).astype(BF16)
            for l in range(DEPTH)]

    def unpack(wall):
        wall = wall.reshape(N_SHARD, 2, wall.shape[1], SHARD_W)
        return wall[:, :, :D // 2].reshape(N_SHARD, D, SHARD_W), wall[:, :, D // 2:].reshape(D, D)

    weights = [unpack(allgather8(wblk[0], "gather_weights"))]

    c_all = allgather8(jnp.broadcast_to(c, (8, D)), "gather_c")[:, 0, :]
    mod_all = allgather8(ada_fwd(c_all, w_ada).reshape(DEPTH * 8, -1), "gather_mod")
    vecs = vecs_build(mod_all, b_ada, norm_g)

    tabs = (*rope_tables(S), ret_log_gamma())
    h = x[0]
    saved = []
    for l in range(DEPTH):
        h, sv, wall = layer_fwd(h, vecs[l], *weights[l], tabs, wblk[l + 1] if l + 1 < DEPTH else None)
        saved.append(sv)
        if wall is not None:
            weights.append(unpack(wall))
    dx, st_loss = loss_head(h, jnp.broadcast_to(final_g[None, :], (8, D)), loss_target[0])

    dmod, dnorm, grads = [None] * DEPTH, [None] * DEPTH, None
    for l in reversed(range(DEPTH)):
        dx, dmod[l], dnorm[l], grads = layer_bwd(dx, saved[l], vecs[l], *weights[l], tabs, grads)

    p_own = sum_slots(*grads)
    p_sib = swap_sibling(p_own)
    res_in = adam_slab(p_own, p_sib, w_in, m_w_in, v_w_in, 0, "adam_w_in")
    res_out = adam_slab(p_own, p_sib, w_out, m_w_out, v_w_out, DEPTH * D, "adam_w_out", tr=128)

    stats = jnp.concatenate(dnorm + [st_loss[0:1]] + dmod + [st_loss[1:2], jnp.zeros((STAT_ROWS - 10, D), F32)], axis=0)
    stats_all = allgather8(stats, "gather_stats")
    dmods = stats_all[:, 3:9, :].reshape(8, DEPTH, 3 * D).transpose(1, 0, 2)
    res_ada = ada_update(dmods, c_all.T, w_ada, m_w_ada, v_w_ada)
    small = small_update(stats_all, (norm_g, m_norm_g, v_norm_g), (b_ada, m_b_ada, v_b_ada),
                         (final_g[None, :], m_final_g[None, :], v_final_g[None, :]))
    res_norm, res_b, res_final = small[0:4], small[4:8], [a[0] for a in small[8:12]]
    loss = small[12][0, 0]

    by_kind = [res_norm, res_ada, res_b, res_in, res_out, res_final]
    outs = [loss, dx[None]]
    for kind in range(4):
        outs += [r[kind] for r in by_kind]
    return tuple(outs)
```

```python
import functools

import numpy as np
import jax
import jax.numpy as jnp
from jax import lax
from jax.experimental import pallas as pl
from jax.experimental.pallas import tpu as pltpu

F32, BF16 = jnp.float32, jnp.bfloat16
MESH = pl.DeviceIdType.MESH

D_MODEL = 1024
DEPTH = 2
SHARD_W = 1024
N_SHARD = 4
GROUP_W = 512
LANES = 128
SB_HEAD_DIM = 64
RET_HEAD_DIM = 128
CHUNK = 64
ROPE_BASE = 10000.0
EPS = 1e-6
SQ_SCALE = SB_HEAD_DIM ** -0.5
RK_SCALE = RET_HEAD_DIM ** -0.5
SB_T = 256
RET_T = 256
EXP_ZERO = -104.0
VMEM_LIMIT_BYTES = 56 * 2 ** 20

ADAM_LR, ADAM_B1, ADAM_B2, ADAM_EPS, ADAM_WD, ADAM_STEP = 0.001, 0.9, 0.999, 1e-08, 0.01, 10


def _cp(*sem):
    return pltpu.CompilerParams(dimension_semantics=sem, vmem_limit_bytes=VMEM_LIMIT_BYTES)


def _dot(a, b):
    return lax.dot_general(a, b, (((1,), (0,)), ((), ())), preferred_element_type=F32)


def _dot_nt(a, b):
    return lax.dot_general(a, b, (((1,), (1,)), ((), ())), preferred_element_type=F32)


def _dot_tn(a, b):
    return lax.dot_general(a, b, (((0,), (0,)), ((), ())), preferred_element_type=F32)


def _running_sum(a, tri):
    return _dot(a.astype(BF16), tri)


def _sigmoid(x):
    return 1.0 / (1.0 + jnp.exp(-x))


def _rowsum(a):
    return jnp.sum(a, axis=1, keepdims=True)


def _rowmean(a):
    return jnp.mean(a, axis=1, keepdims=True)


def inproj_fwd(x, vecs, w3, tm=256):
    S, D = x.shape

    def body(x_ref, v_ref, w_ref, proj_ref, h_ref, sb_ref):
        xv = x_ref[...]
        r = lax.rsqrt(_rowmean(xv * xv) + EPS)
        h = xv * r * v_ref[3:4, :] * (1.0 + v_ref[1:2, :]) + v_ref[0:1, :]
        hb = h.astype(BF16)
        h_ref[...] = hb
        for s in range(N_SHARD):
            p = _dot(hb, w_ref[s])
            proj_ref[:, s * SHARD_W:(s + 1) * SHARD_W] = p
            if s == 2:
                sb_ref[:, 0:GROUP_W] = (p[:, 0:GROUP_W] * SQ_SCALE).astype(BF16)
                sb_ref[:, GROUP_W:SHARD_W] = p[:, GROUP_W:].astype(BF16)
            if s == 3:
                sb_ref[:, SHARD_W:2 * SHARD_W] = p.astype(BF16)

    return pl.pallas_call(
        body, name="inproj_fwd", grid=(S // tm,),
        in_specs=[pl.BlockSpec((tm, D), lambda i: (i, 0)),
                  pl.BlockSpec((8, D), lambda i: (0, 0)),
                  pl.BlockSpec((N_SHARD, D, SHARD_W), lambda i: (0, 0, 0))],
        out_specs=[pl.BlockSpec((tm, 4 * D), lambda i: (i, 0)),
                   pl.BlockSpec((tm, D), lambda i: (i, 0)),
                   pl.BlockSpec((tm, 2 * SHARD_W), lambda i: (i, 0))],
        out_shape=[jax.ShapeDtypeStruct((S, 4 * D), F32),
                   jax.ShapeDtypeStruct((S, D), BF16),
                   jax.ShapeDtypeStruct((S, 2 * SHARD_W), BF16)],
        compiler_params=_cp("arbitrary"),
    )(x, vecs, w3)


def _sb_logits(qh, k2, mask):
    z = _dot_nt(qh, k2)
    sp = jnp.log(1.0 + jnp.exp(-jnp.abs(z)))
    lb = jnp.minimum(z, 0.0) - sp
    lk = lb - z
    if mask is not None:
        lk = jnp.where(mask, lk, 0.0)
    return lb, lk


def _sb_masks():
    T = SB_T
    row = lax.broadcasted_iota(jnp.int32, (T, T), 0)
    col = lax.broadcasted_iota(jnp.int32, (T, T), 1)
    causal = col < row
    later = jnp.where(row > col, 1.0, 0.0).astype(BF16)
    earlier = jnp.where(row < col, 1.0, 0.0).astype(BF16)
    lane = lax.broadcasted_iota(jnp.int32, (1, LANES), 1)
    return causal, later, earlier, lane < SB_HEAD_DIM


def _sb_rows(ref, j):
    return ref[pl.ds(pl.multiple_of(j * SB_T, SB_T), SB_T), :]


def sb_fwd(sb, proj, gather=None):
    S = sb.shape[0]
    T = SB_T
    nq = S // T
    carried = [] if gather is None else [gather]

    def body(*refs):
        (q_ref, k_ref, v_ref, sg_ref), refs = refs[:4], refs[4:]
        p, i = pl.program_id(0), pl.program_id(1)
        if carried:
            x_ref, y_ref, o_ref, end_ref, out_ref, send_sems, recv_sems, local_sem = refs
            start, forward, finish = _gather_ops(x_ref, out_ref, send_sems, recv_sems, local_sem)
            pl.when(jnp.logical_and(p == 0, i == 0))(start)
            pl.when(jnp.logical_and(p == 2, i == 0))(forward)
        else:
            y_ref, o_ref, end_ref = refs
        causal, later, _, head0 = _sb_masks()
        q2 = q_ref[...]
        zero = jnp.zeros_like(q2)
        qs = jnp.concatenate([jnp.where(head0, q2, zero), jnp.where(head0, zero, q2)], axis=0)
        causal2 = jnp.concatenate([causal, causal], axis=0)

        def tile(j, R, mask):
            lb, lk = _sb_logits(qs, _sb_rows(k_ref, j), mask)
            a = jnp.exp(lb + _running_sum(lk, later) + R)
            if mask is not None:
                a = jnp.where(mask, a, 0.0)
            return _dot(a.astype(BF16), _sb_rows(v_ref, j)), R + _rowsum(lk)

        acc, R = tile(i, jnp.zeros((2 * T, 1), F32), causal2)

        def cond(st):
            return jnp.logical_and(st[0] >= 0, st[3] > EXP_ZERO)

        def step(st):
            c, Rn = tile(st[0], st[2], None)
            return st[0] - 1, st[1] + c, Rn, jnp.max(Rn)

        j_end, acc, R, _ = lax.while_loop(cond, step, (i - 1, acc, R, jnp.max(R)))
        end_ref[0, 0, 0:2 * T, :] = jnp.broadcast_to(R, (2 * T, LANES))
        end_ref[0, 0, 2 * T:, :] = jnp.full((8, LANES), j_end.astype(F32))
        o = jnp.where(head0, acc[:T], acc[T:])
        o_ref[...] = o
        sg = sg_ref[...]
        y_ref[...] = (o * (sg * _sigmoid(sg))).astype(BF16)
        if carried:
            pl.when(jnp.logical_and(p == 3, i == nq - 1))(finish)

    return pl.pallas_call(
        body, name="sb_fwd", grid=(4, nq),
        in_specs=[pl.BlockSpec((T, LANES), lambda p, i: (i, p)),
                  pl.BlockSpec((S, LANES), lambda p, i: (0, 4 + p)),
                  pl.BlockSpec((S, LANES), lambda p, i: (0, 8 + p)),
                  pl.BlockSpec((T, LANES), lambda p, i: (i, 28 + p))] + [_ANY for _ in carried],
        out_specs=[pl.BlockSpec((T, LANES), lambda p, i: (i, p)),
                   pl.BlockSpec((T, LANES), lambda p, i: (i, p)),
                   pl.BlockSpec((1, 1, 2 * T + 8, LANES), lambda p, i: (p, i, 0, 0))] + [_ANY for _ in carried],
        out_shape=[jax.ShapeDtypeStruct((S, GROUP_W), BF16),
                   jax.ShapeDtypeStruct((S, GROUP_W), F32),
                   jax.ShapeDtypeStruct((4, nq, 2 * T + 8, LANES), F32)]
        + [jax.ShapeDtypeStruct((8,) + a.shape, a.dtype) for a in carried],
        scratch_shapes=_GATHER_SCRATCH if carried else [],
        compiler_params=_cp("arbitrary", "arbitrary"),
    )(sb, sb, sb, proj, *carried)


def sb_bwd(sb, proj, o, sb_end, dycat, ship=None):
    S = sb.shape[0]
    T = SB_T
    nq = S // T
    ex = _Exchange(ship)

    def body(*refs):
        (q_ref, k_ref, v_ref, sg_ref, o_ref, dy_ref, end_ref), refs = refs[:7], refs[7:]
        ship_refs, (dq_ref, dk_ref, dv_ref, dsg_ref), refs = refs[:ex.n_in], refs[ex.n_in:ex.n_in + 4], refs[ex.n_in + 4:]
        recv, (dk_acc, dv_acc), sems = refs[:ex.n_out], refs[ex.n_out:ex.n_out + 2], refs[ex.n_out + 2:]
        start, finish = ex.ops(ship_refs, recv + sems)
        p, i = pl.program_id(0), pl.program_id(1)
        pl.when(jnp.logical_and(p == 0, i == 0))(start)

        @pl.when(i == 0)
        def _():
            dk_acc[...] = jnp.zeros_like(dk_acc)
            dv_acc[...] = jnp.zeros_like(dv_acc)

        causal, later, earlier, head0 = _sb_masks()
        q2 = q_ref[...]
        sg = sg_ref[...]
        sig = _sigmoid(sg)
        dy = dy_ref[...]
        dsg_ref[...] = (dy * o_ref[...] * (sig * (1.0 + sg * (1.0 - sig)))).astype(BF16)
        do_b = (dy * (sg * sig)).astype(BF16)
        zero = jnp.zeros_like(q2)
        qs = jnp.concatenate([jnp.where(head0, q2, zero), jnp.where(head0, zero, q2)], axis=0)
        dos = jnp.concatenate([jnp.where(head0, do_b, zero), jnp.where(head0, zero, do_b)], axis=0)
        causal2 = jnp.concatenate([causal, causal], axis=0)

        end = end_ref[0, 0]
        j_end = jnp.max(end[2 * T:, :]).astype(jnp.int32)

        def tile(j, G0, left, mask):
            k2, v2 = _sb_rows(k_ref, j), _sb_rows(v_ref, j)
            lb, lk = _sb_logits(qs, k2, mask)
            stick = left - _rowsum(lk) if mask is None else jnp.zeros_like(left)
            a = jnp.exp(lb + _running_sum(lk, later) + stick)
            if mask is not None:
                a = jnp.where(mask, a, 0.0)
            g = a * _dot_nt(dos, v2)
            G = _running_sum(g, earlier) + G0
            dz = g - jnp.exp(lb) * (g + G)
            if mask is not None:
                dz = jnp.where(mask, dz, 0.0)
            dzb = dz.astype(BF16)
            rows = pl.ds(pl.multiple_of(j * T, T), T)
            dk_acc[rows, :] += _dot_tn(dzb, qs)
            dv_acc[rows, :] += _dot_tn(a.astype(BF16), dos)
            return _dot(dzb, k2), G0 + _rowsum(g), stick

        def sweep(j, st):
            dq_t, Gn, stick = tile(j, st[1], st[2], None)
            return st[0] + dq_t, Gn, stick

        st = lax.fori_loop(j_end + 1, i, sweep,
                           (jnp.zeros((2 * T, LANES), F32), jnp.zeros((2 * T, 1), F32), end[:2 * T, 0:1]))
        dq = st[0] + tile(i, st[1], st[2], causal2)[0]
        dq_ref[...] = (jnp.where(head0, dq[:T], dq[T:]) * SQ_SCALE).astype(BF16)

        @pl.when(i == nq - 1)
        def _():
            dk_ref[...] = dk_acc[...].astype(BF16)
            dv_ref[...] = dv_acc[...].astype(BF16)

        pl.when(jnp.logical_and(p == 3, i == nq - 1))(finish)

    tile_spec = lambda c0: pl.BlockSpec((T, LANES), lambda p, i: (i, c0 + p))
    head_spec = lambda c0: pl.BlockSpec((S, LANES), lambda p, i: (0, c0 + p))
    return pl.pallas_call(
        body, name="sb_bwd", grid=(4, nq),
        in_specs=[tile_spec(0), head_spec(4), head_spec(8), tile_spec(28), tile_spec(0), tile_spec(4),
                  pl.BlockSpec((1, 1, 2 * T + 8, LANES), lambda p, i: (p, i, 0, 0))] + ex.in_specs,
        out_specs=[tile_spec(0), head_spec(0), head_spec(0), tile_spec(0)] + ex.out_specs,
        out_shape=[jax.ShapeDtypeStruct((S, GROUP_W), BF16)] * 4 + ex.out_shape,
        scratch_shapes=[pltpu.VMEM((S, LANES), F32), pltpu.VMEM((S, LANES), F32)] + ex.scratch,
        compiler_params=_cp("arbitrary", "arbitrary"),
    )(sb, sb, sb, proj, o, dycat, sb_end, *ex.ship)


def rope_tables(S):
    half = RET_HEAD_DIM // 2
    inv = ROPE_BASE ** (-jnp.arange(half, dtype=F32) / half)
    ang = jnp.arange(S, dtype=F32)[:, None] * inv[None, :]
    cos, sin = jnp.cos(ang), jnp.sin(ang)
    return jnp.concatenate([cos, cos], axis=1), jnp.concatenate([-sin, sin], axis=1)


def ret_log_gamma():
    return jnp.log1p(-(2.0 ** (-5.0 - jnp.arange(4, dtype=F32))))


def _swap_halves(a):
    return pltpu.roll(a, RET_HEAD_DIM // 2, axis=1)


def _ret_decay_mask(lg):
    n = lax.broadcasted_iota(jnp.int32, (RET_T, RET_T), 0)
    m = lax.broadcasted_iota(jnp.int32, (RET_T, RET_T), 1)
    dist = jnp.abs(n - m).astype(F32)
    return jnp.where((m // CHUNK) <= (n // CHUNK), jnp.exp(lg * dist), 0.0)


def _ret_block(lg, rq, rk, rv, cosf, sinf, dm):
    q = rq * cosf + _swap_halves(rq) * sinf
    k = (rk * cosf + _swap_halves(rk) * sinf) * RK_SCALE
    qb, kb, vb = q.astype(BF16), k.astype(BF16), rv.astype(BF16)
    sc = _dot_nt(qb, kb) * dm
    nloc = lax.broadcasted_iota(jnp.int32, (RET_T, 1), 0).astype(F32)
    qdec = jnp.exp(lg * (nloc + 1.0))
    kdec = jnp.exp(lg * (RET_T - 1.0 - nloc))
    block_dec = jnp.exp(jnp.full((1, LANES), lg * RET_T, F32))
    return q, k, qb, kb, vb, sc, qdec, kdec, block_dec


def _ret_specs(S, rb):
    group = lambda c0: pl.BlockSpec((RET_T, GROUP_W), lambda b: (rb(b), c0))
    return group, pl.BlockSpec((RET_T, LANES), lambda b: (rb(b), 0))


def _head(ref, h):
    return ref[:, h * LANES:(h + 1) * LANES]


def ret_fwd(proj, cosf, sinf, lgam):
    S = proj.shape[0]
    nb = S // RET_T
    group, row_tab = _ret_specs(S, lambda b: b)

    def body(lg_ref, rq_ref, rk_ref, rv_ref, rg_ref, cos_ref, sin_ref, y_ref, o_ref, st_out, st_ref, dm_ref):
        @pl.when(pl.program_id(0) == 0)
        def _():
            st_ref[...] = jnp.zeros_like(st_ref)
            for h in range(4):
                dm_ref[h] = _ret_decay_mask(lg_ref[h])

        cosf, sinf = cos_ref[...], sin_ref[...]
        for h in range(4):
            lg = lg_ref[h]
            q, k, qb, kb, vb, sc, qdec, kdec, block_dec = _ret_block(
                lg, _head(rq_ref, h), _head(rk_ref, h), _head(rv_ref, h), cosf, sinf, dm_ref[h])
            st = st_ref[h]
            st_out[h, 0] = st
            o = _dot(sc.astype(BF16), vb) + _dot(qb, st.astype(BF16)) * qdec
            st_ref[h] = st * block_dec + _dot_tn((k * kdec).astype(BF16), vb)
            o_ref[:, h * LANES:(h + 1) * LANES] = o
            cen = o - _rowmean(o)
            on = cen * lax.rsqrt(_rowmean(cen * cen) + EPS)
            rg = _head(rg_ref, h)
            y_ref[:, h * LANES:(h + 1) * LANES] = (on * (rg * _sigmoid(rg))).astype(BF16)

    return pl.pallas_call(
        body, name="ret_fwd", grid=(nb,),
        in_specs=[pl.BlockSpec(memory_space=pltpu.SMEM),
                  group(0), group(1), group(2), group(3), row_tab, row_tab],
        out_specs=[group(0), group(0),
                   pl.BlockSpec((4, 1, LANES, LANES), lambda b: (0, b, 0, 0))],
        out_shape=[jax.ShapeDtypeStruct((S, GROUP_W), BF16),
                   jax.ShapeDtypeStruct((S, GROUP_W), F32),
                   jax.ShapeDtypeStruct((4, nb, LANES, LANES), F32)],
        scratch_shapes=[pltpu.VMEM((4, LANES, LANES), F32), pltpu.VMEM((4, RET_T, RET_T), F32)],
        compiler_params=_cp("arbitrary"),
    )(lgam, proj, proj, proj, proj, cosf, sinf)


def ret_bwd(proj, cosf, sinf, lgam, o, states, dycat):
    S = proj.shape[0]
    nb = S // RET_T
    rev = lambda b: nb - 1 - b
    group, row_tab = _ret_specs(S, rev)

    def body(lg_ref, rq_ref, rk_ref, rv_ref, rg_ref, cos_ref, sin_ref, o_ref, st_in, dy_ref,
             drq_ref, drk_ref, drv_ref, drg_ref, ds_ref, dm_ref):
        @pl.when(pl.program_id(0) == 0)
        def _():
            ds_ref[...] = jnp.zeros_like(ds_ref)
            for h in range(4):
                dm_ref[h] = _ret_decay_mask(lg_ref[h])

        cosf, sinf = cos_ref[...], sin_ref[...]
        for h in range(4):
            lanes = slice(h * LANES, (h + 1) * LANES)
            dm = dm_ref[h]
            q, k, qb, kb, vb, sc, qdec, kdec, block_dec = _ret_block(
                lg_ref[h], _head(rq_ref, h), _head(rk_ref, h), _head(rv_ref, h), cosf, sinf, dm)
            o_v = _head(o_ref, h)
            cen = o_v - _rowmean(o_v)
            rstd = lax.rsqrt(_rowmean(cen * cen) + EPS)
            on = cen * rstd
            rg = _head(rg_ref, h)
            sig = _sigmoid(rg)
            dy = _head(dy_ref, h)
            drg_ref[:, lanes] = (dy * on * (sig * (1.0 + rg * (1.0 - sig)))).astype(BF16)
            don = dy * (rg * sig)
            do = rstd * (don - _rowmean(don) - on * _rowmean(don * on))
            dob = do.astype(BF16)
            dsc = (_dot_nt(dob, vb) * dm).astype(BF16)
            st_b = st_in[h, 0].astype(BF16)
            dsn = ds_ref[h]
            dsn_b = dsn.astype(BF16)
            dq = _dot(dsc, kb) + _dot_nt(dob, st_b) * qdec
            dk = _dot_tn(dsc, qb) + _dot_nt(vb, dsn_b) * kdec
            dv = _dot_tn(sc.astype(BF16), dob) + _dot((k * kdec).astype(BF16), dsn_b)
            ds_ref[h] = dsn * block_dec + _dot_tn((q * qdec).astype(BF16), dob)
            dk = dk * RK_SCALE
            drq_ref[:, lanes] = (dq * cosf + _swap_halves(dq * sinf)).astype(BF16)
            drk_ref[:, lanes] = (dk * cosf + _swap_halves(dk * sinf)).astype(BF16)
            drv_ref[:, lanes] = dv.astype(BF16)

    return pl.pallas_call(
        body, name="ret_bwd", grid=(nb,),
        in_specs=[pl.BlockSpec(memory_space=pltpu.SMEM),
                  group(0), group(1), group(2), group(3), row_tab, row_tab,
                  group(0), pl.BlockSpec((4, 1, LANES, LANES), lambda b: (0, rev(b), 0, 0)),
                  group(0)],
        out_specs=[group(0)] * 4,
        out_shape=[jax.ShapeDtypeStruct((S, GROUP_W), BF16)] * 4,
        scratch_shapes=[pltpu.VMEM((4, LANES, LANES), F32), pltpu.VMEM((4, RET_T, RET_T), F32)],
        compiler_params=_cp("arbitrary"),
    )(lgam, proj, proj, proj, proj, cosf, sinf, o, states, dycat)


def outproj_fwd(x, vecs, y_ret, y_sb, w_out, tm=512):
    S, D = x.shape

    def body(x_ref, v_ref, yr_ref, ys_ref, w_ref, y_ref, xo_ref):
        y = _dot(yr_ref[...], w_ref[0:GROUP_W, :]) + _dot(ys_ref[...], w_ref[GROUP_W:, :])
        y_ref[...] = y
        xo_ref[...] = x_ref[...] + v_ref[2:3, :] * y

    row = lambda w: pl.BlockSpec((tm, w), lambda i: (i, 0))
    return pl.pallas_call(
        body, name="outproj_fwd", grid=(S // tm,),
        in_specs=[row(D), pl.BlockSpec((8, D), lambda i: (0, 0)), row(GROUP_W), row(GROUP_W),
                  pl.BlockSpec((D, D), lambda i: (0, 0))],
        out_specs=[row(D), row(D)],
        out_shape=[jax.ShapeDtypeStruct((S, D), F32)] * 2,
        compiler_params=_cp("arbitrary"),
    )(x, vecs, y_ret, y_sb, w_out)


def loss_head(x, final_g, target, tm=512):
    S, D = x.shape

    def body(x_ref, g_ref, t_ref, dx_ref, st_ref):
        @pl.when(pl.program_id(0) == 0)
        def _():
            st_ref[...] = jnp.zeros_like(st_ref)

        xv = x_ref[...]
        g = g_ref[0:1, :]
        r = lax.rsqrt(_rowmean(xv * xv) + EPS)
        xn = xv * r
        err = xn * g - t_ref[...]
        dy = err * (1.0 / D)
        dxn = dy * g
        dx_ref[...] = r * (dxn - xn * _rowmean(dxn * xn))
        st_ref[0:1, :] += jnp.sum(dy * xn, axis=0, keepdims=True)
        st_ref[1:2, :] += jnp.sum(err * err, axis=0, keepdims=True)

    row = pl.BlockSpec((tm, D), lambda i: (i, 0))
    fixed = pl.BlockSpec((8, D), lambda i: (0, 0))
    return pl.pallas_call(
        body, name="loss_head", grid=(S // tm,),
        in_specs=[row, fixed, row], out_specs=[row, fixed],
        out_shape=[jax.ShapeDtypeStruct((S, D), F32), jax.ShapeDtypeStruct((8, D), F32)],
        compiler_params=_cp("arbitrary"),
    )(x, final_g, target)


def outproj_bwd(dx, y, vecs, y_ret, y_sb, w_out, tm=512):
    S, D = dx.shape
    n = S // tm

    def body(dx_ref, y_ref, v_ref, yr_ref, ys_ref, w_ref, dyc_ref, dw_ref, st_ref, acc):
        i = pl.program_id(0)

        @pl.when(i == 0)
        def _():
            st_ref[...] = jnp.zeros_like(st_ref)
            acc[...] = jnp.zeros_like(acc)

        dxv = dx_ref[...]
        st_ref[0:1, :] += jnp.sum(dxv * y_ref[...], axis=0, keepdims=True)
        dyy = (dxv * v_ref[2:3, :]).astype(BF16)
        dyc_ref[...] = _dot_nt(dyy, w_ref[...])
        acc[0:GROUP_W, :] += _dot_tn(yr_ref[...], dyy)
        acc[GROUP_W:, :] += _dot_tn(ys_ref[...], dyy)

        @pl.when(i == n - 1)
        def _():
            dw_ref[...] = acc[...].astype(BF16)

    row = lambda w: pl.BlockSpec((tm, w), lambda i: (i, 0))
    fixed = lambda r: pl.BlockSpec((r, D), lambda i: (0, 0))
    return pl.pallas_call(
        body, name="outproj_bwd", grid=(n,),
        in_specs=[row(D), row(D), fixed(8), row(GROUP_W), row(GROUP_W), fixed(D)],
        out_specs=[row(D), fixed(D), fixed(8)],
        out_shape=[jax.ShapeDtypeStruct((S, D), F32), jax.ShapeDtypeStruct((D, D), BF16),
                   jax.ShapeDtypeStruct((8, D), F32)],
        scratch_shapes=[pltpu.VMEM((D, D), F32)],
        compiler_params=_cp("arbitrary"),
    )(dx, y, vecs, y_ret, y_sb, w_out)


def inproj_bwd_x(pieces, w3, x, vecs, dx_res, ship=None, tm=256):
    S, D = x.shape
    n = S // tm
    ex = _Exchange(ship)

    def body(*refs):
        p_refs, (w_ref, x_ref, v_ref, dr_ref), refs = refs[:8], refs[8:12], refs[12:]
        ship_refs, (dx_ref, st_ref), refs = refs[:ex.n_in], refs[ex.n_in:ex.n_in + 2], refs[ex.n_in + 2:]
        start, finish = ex.ops(ship_refs, refs)

        @pl.when(pl.program_id(0) == 0)
        def _():
            st_ref[...] = jnp.zeros_like(st_ref)
            start()

        dh = jnp.zeros((tm, D), F32)
        for k, p_ref in enumerate(p_refs):
            c0 = (k % 2) * GROUP_W
            dh = dh + _dot_nt(p_ref[...], w_ref[k // 2, :, c0:c0 + GROUP_W])
        xv = x_ref[...]
        r = lax.rsqrt(_rowmean(xv * xv) + EPS)
        xn = xv * r
        g, scale1 = v_ref[3:4, :], 1.0 + v_ref[1:2, :]
        st_ref[0:1, :] += jnp.sum(dh, axis=0, keepdims=True)
        dh_xn = dh * xn
        st_ref[1:2, :] += jnp.sum(dh_xn, axis=0, keepdims=True) * g
        st_ref[2:3, :] += jnp.sum(dh_xn, axis=0, keepdims=True) * scale1
        dxn = dh * (g * scale1)
        dx_ref[...] = r * (dxn - xn * _rowmean(dxn * xn)) + dr_ref[...]
        pl.when(pl.program_id(0) == n - 1)(finish)

    row = lambda w: pl.BlockSpec((tm, w), lambda i: (i, 0))
    return pl.pallas_call(
        body, name="inproj_bwd_x", grid=(n,),
        in_specs=[row(GROUP_W)] * 8 + [pl.BlockSpec((N_SHARD, D, SHARD_W), lambda i: (0, 0, 0)),
                                       row(D), pl.BlockSpec((8, D), lambda i: (0, 0)), row(D)] + ex.in_specs,
        out_specs=[row(D), pl.BlockSpec((8, D), lambda i: (0, 0))] + ex.out_specs,
        out_shape=[jax.ShapeDtypeStruct((S, D), F32), jax.ShapeDtypeStruct((8, D), F32)] + ex.out_shape,
        scratch_shapes=ex.scratch,
        compiler_params=_cp("arbitrary"),
    )(*pieces, w3, x, vecs, dx_res, *ex.ship)


def inproj_bwd_w(h, pieces, tm=512):
    S, D = h.shape
    n = S // tm

    def body(*refs):
        h_ref, p_refs, dw_ref, acc = refs[0], refs[1:9], refs[9], refs[10]
        i = pl.program_id(0)

        @pl.when(i == 0)
        def _():
            acc[...] = jnp.zeros_like(acc)

        hv = h_ref[...]
        for k, p_ref in enumerate(p_refs):
            c0 = (k % 2) * GROUP_W
            acc[k // 2, :, c0:c0 + GROUP_W] += _dot_tn(hv, p_ref[...])

        @pl.when(i == n - 1)
        def _():
            dw_ref[...] = acc[...].astype(BF16)

    row = lambda w: pl.BlockSpec((tm, w), lambda i: (i, 0))
    return pl.pallas_call(
        body, name="inproj_bwd_w", grid=(n,),
        in_specs=[row(D)] + [row(GROUP_W)] * 8,
        out_specs=pl.BlockSpec((N_SHARD, D, SHARD_W), lambda i: (0, 0, 0)),
        out_shape=jax.ShapeDtypeStruct((N_SHARD, D, SHARD_W), BF16),
        scratch_shapes=[pltpu.VMEM((N_SHARD, D, SHARD_W), F32)],
        compiler_params=_cp("arbitrary"),
    )(h, *pieces)


def layer_fwd(x, vecs, w3, w_out, tabs, gather=None):
    cosf, sinf, lgam = tabs
    proj, h, sb = inproj_fwd(x, vecs, w3)
    y_ret, o_ret, states = ret_fwd(proj, cosf, sinf, lgam)
    y_sb, o_sb, sb_end, *gathered = sb_fwd(sb, proj, gather)
    y, x_next = outproj_fwd(x, vecs, y_ret, y_sb, w_out)
    saved = (x, proj, h, sb, y_ret, o_ret, states, y_sb, o_sb, sb_end, y)
    return x_next, saved, (gathered[0] if gathered else None)


def _by_shard(dw_out):
    return dw_out.reshape(N_SHARD, D_MODEL // N_SHARD, D_MODEL)


def layer_bwd(dx, saved, vecs, w3, w_out, tabs, later_grads=None):
    cosf, sinf, lgam = tabs
    x, proj, h, sb, y_ret, o_ret, states, y_sb, o_sb, sb_end, y = saved
    dycat, dw_out, st_o = outproj_bwd(dx, y, vecs, y_ret, y_sb, w_out)
    dw_out = _by_shard(dw_out)
    ship = None if later_grads is None else (later_grads[0], dw_out, later_grads[1])
    *d_sb, = sb_bwd(sb, proj, o_sb, sb_end, dycat, ship)
    d_ret = ret_bwd(proj, cosf, sinf, lgam, o_ret, states, dycat)
    pieces = list(d_ret) + d_sb[:4]
    dw_in = inproj_bwd_w(h, pieces)
    dx, st_i, *recv_in = inproj_bwd_x(pieces, w3, x, vecs, dx, None if later_grads is None else (dw_in,))
    dmod = jnp.concatenate([st_i[0:2], st_o[0:1]], axis=0)
    grads = (dw_in, dw_out) if later_grads is None else (recv_in[0], d_sb[4])
    return dx, dmod, st_i[2:3], grads


def _place():
    return lax.axis_index("x"), lax.axis_index("y"), lax.axis_index("c")


def _other_chips(mx, my):
    return [(1 - mx, my), (mx, 1 - my), (1 - mx, 1 - my)]


_ANY = pl.BlockSpec(memory_space=pl.ANY)


_GATHER_SCRATCH = [pltpu.SemaphoreType.DMA((7,)), pltpu.SemaphoreType.DMA((7,)), pltpu.SemaphoreType.DMA(())]


def _gather_ops(x_ref, out_ref, send_sems, recv_sems, local_sem):
    mx, my, mc = _place()
    me, sibling = (mx, my, mc), (mx, my, 1 - mc)
    chips = _other_chips(mx, my)

    def slot(px, py, pc):
        return out_ref.at[4 * px + 2 * py + pc]

    def copy(k, block, to, src=None):
        return pltpu.make_async_remote_copy(
            src_ref=slot(*block) if src is None else src, dst_ref=slot(*block),
            send_sem=send_sems.at[k], recv_sem=recv_sems.at[k], device_id=to, device_id_type=MESH)

    mine = pltpu.make_async_copy(x_ref, slot(*me), local_sem)
    first = [copy(0, me, sibling, src=x_ref)]
    first += [copy(1 + j, me, (*chip, mc), src=x_ref) for j, chip in enumerate(chips)]
    passed = [copy(4 + j, (*chip, mc), sibling) for j, chip in enumerate(chips)]

    def start():
        mine.start()
        for cp in first:
            cp.start()

    def forward():
        for j, chip in enumerate(chips):
            copy(1 + j, (*chip, mc), me).wait_recv()
            passed[j].start()

    def finish():
        copy(0, sibling, me).wait_recv()
        for j, chip in enumerate(chips):
            copy(4 + j, (*chip, 1 - mc), me).wait_recv()
        for cp in first + passed:
            cp.wait_send()
        mine.wait()

    return start, forward, finish


def allgather8(x, name):
    def body(x_ref, out_ref, send_sems, recv_sems, local_sem):
        for step in _gather_ops(x_ref, out_ref, send_sems, recv_sems, local_sem):
            step()

    return pl.pallas_call(
        body, name=name, out_shape=jax.ShapeDtypeStruct((8,) + x.shape, x.dtype),
        in_specs=[_ANY], out_specs=_ANY, scratch_shapes=_GATHER_SCRATCH,
    )(x)


class _Exchange:
    def __init__(self, ship):
        self.ship = list(ship or ())
        self.n_in = len(self.ship)
        self.n_out = 1 if self.ship else 0
        self.rows = [a.shape[1] for a in self.ship]
        self.in_specs = [_ANY] * self.n_in
        self.out_specs = [_ANY] * self.n_out
        self.out_shape = [jax.ShapeDtypeStruct((N_SHARD, sum(self.rows), SHARD_W), BF16)] * self.n_out
        sem = pltpu.SemaphoreType.DMA
        self.scratch = [sem((3,)), sem((3,)), sem(())] * self.n_out

    def ops(self, ship_refs, tail):
        if not self.ship:
            return (lambda: None), (lambda: None)
        recv, send_sems, recv_sems, local_sem = tail
        mx, my, mc = _place()
        my_chip = 2 * mx + my
        chips = _other_chips(mx, my)

        def pieces(s):
            firsts = np.cumsum([0] + self.rows[:-1])
            return [(ref.at[s], int(r0), n) for ref, r0, n in zip(ship_refs, firsts, self.rows)]

        def start():
            for src, r0, n in pieces(my_chip):
                pltpu.make_async_copy(src, recv.at[my_chip, pl.ds(r0, n)], local_sem).start()
            for j, (px, py) in enumerate(chips):
                for src, r0, n in pieces(2 * px + py):
                    pltpu.make_async_remote_copy(
                        src_ref=src, dst_ref=recv.at[my_chip, pl.ds(r0, n)],
                        send_sem=send_sems.at[j], recv_sem=recv_sems.at[j],
                        device_id=(px, py, mc), device_id_type=MESH).start()

        def finish():
            for j, (px, py) in enumerate(chips):
                whole = recv.at[2 * px + py]
                both = pltpu.make_async_remote_copy(
                    src_ref=whole, dst_ref=whole, send_sem=send_sems.at[j], recv_sem=recv_sems.at[j],
                    device_id=(px, py, mc), device_id_type=MESH)
                both.wait_recv()
                both.wait_send()
            pltpu.make_async_copy(recv.at[my_chip], recv.at[my_chip], local_sem).wait()

        return start, finish


def sum_slots(recv_a, recv_b, tr=256):
    n, rows_a, cols = recv_a.shape
    na, nb = rows_a // tr, recv_b.shape[1] // tr

    def body(a_ref, b_ref, o_ref):
        def total(r_ref):
            acc = r_ref[0].astype(F32)
            for k in range(1, n):
                acc = acc + r_ref[k].astype(F32)
            o_ref[...] = acc

        pl.when(pl.program_id(0) < na)(lambda: total(a_ref))
        pl.when(pl.program_id(0) >= na)(lambda: total(b_ref))

    return pl.pallas_call(
        body, name="sum_slots", grid=(na + nb,),
        in_specs=[pl.BlockSpec((n, tr, cols), lambda i: (0, jnp.minimum(i, na - 1), 0)),
                  pl.BlockSpec((n, tr, cols), lambda i: (0, jnp.maximum(i - na, 0), 0))],
        out_specs=pl.BlockSpec((tr, cols), lambda i: (i, 0)),
        out_shape=jax.ShapeDtypeStruct(((na + nb) * tr, cols), F32),
        compiler_params=_cp("arbitrary"),
    )(recv_a, recv_b)


def swap_sibling(p):
    def body(p_ref, out_ref, send_sem, recv_sem):
        mx, my, mc = _place()
        cp = pltpu.make_async_remote_copy(
            src_ref=p_ref, dst_ref=out_ref, send_sem=send_sem, recv_sem=recv_sem,
            device_id=(mx, my, 1 - mc), device_id_type=MESH)
        cp.start()
        cp.wait()

    return pl.pallas_call(
        body, name="swap_sibling", out_shape=jax.ShapeDtypeStruct(p.shape, p.dtype),
        in_specs=[_ANY], out_specs=_ANY,
        scratch_shapes=[pltpu.SemaphoreType.DMA(()), pltpu.SemaphoreType.DMA(())],
    )(p)


def _adamw(w, g, m, v):
    m = ADAM_B1 * m + (1.0 - ADAM_B1) * g
    v = ADAM_B2 * v + (1.0 - ADAM_B2) * (g * g)
    m_hat = m / (1.0 - ADAM_B1 ** ADAM_STEP)
    v_hat = v / (1.0 - ADAM_B2 ** ADAM_STEP)
    delta = -ADAM_LR * (m_hat / (jnp.sqrt(v_hat) + ADAM_EPS) + ADAM_WD * w)
    return delta, m, v


def adam_slab(p_own, p_sib, w, m, v, row0, name, tr=256):
    L, R, C = w.shape
    nr = R // tr

    def body(a_ref, b_ref, w_ref, m_ref, v_ref, g_out, d_out, m_out, v_out):
        g = a_ref[...] + b_ref[...]
        d, m2, v2 = _adamw(w_ref[0], g, m_ref[0], v_ref[0])
        g_out[0], d_out[0], m_out[0], v_out[0] = g, d, m2, v2

    slab = pl.BlockSpec((tr, C), lambda l, i: (row0 // tr + l * nr + i, 0))
    blk = pl.BlockSpec((1, tr, C), lambda l, i: (l, i, 0))
    return pl.pallas_call(
        body, name=name, grid=(L, nr),
        in_specs=[slab, slab, blk, blk, blk], out_specs=[blk] * 4,
        out_shape=[jax.ShapeDtypeStruct(w.shape, F32)] * 4,
        compiler_params=_cp("arbitrary", "arbitrary"),
    )(p_own, p_sib, w, m, v)


def ada_fwd(c_all, w_ada):
    L, D, W = w_ada.shape

    def body(c_ref, w_ref, o_ref):
        cv = c_ref[...]
        o_ref[0] = jnp.dot(cv * _sigmoid(cv), w_ref[0], precision=lax.Precision.HIGHEST,
                           preferred_element_type=F32)

    return pl.pallas_call(
        body, name="ada_fwd", grid=(L,),
        in_specs=[pl.BlockSpec((8, D), lambda l: (0, 0)), pl.BlockSpec((1, D, W), lambda l: (l, 0, 0))],
        out_specs=pl.BlockSpec((1, 8, W), lambda l: (l, 0, 0)),
        out_shape=jax.ShapeDtypeStruct((L, 8, W), F32),
        compiler_params=_cp("arbitrary"),
    )(c_all, w_ada)


def vecs_build(mod_all, b_ada, norm_g):
    W = mod_all.shape[2]

    def body(m_ref, b_ref, g_ref, o_ref):
        mx, my, mc = _place()
        me = 4 * mx + 2 * my + mc
        rowid = lax.broadcasted_iota(jnp.int32, (2 * 8, 1), 0)
        o_ref[...] = jnp.zeros_like(o_ref)
        for l in range(DEPTH):
            parts = [jnp.sum(jnp.where(rowid == l * 8 + me, m_ref[2 * s + mc], 0.0), axis=0, keepdims=True)
                     for s in range(N_SHARD)]
            mod = jnp.concatenate(parts, axis=1) + b_ref[l:l + 1, :]
            for t in range(3):
                o_ref[l, t:t + 1, :] = mod[:, t * D_MODEL:(t + 1) * D_MODEL]
            o_ref[l, 3:4, :] = g_ref[l:l + 1, :]

    return pl.pallas_call(
        body, name="vecs_build", out_shape=jax.ShapeDtypeStruct((DEPTH, 8, D_MODEL), F32),
    )(mod_all, b_ada, norm_g)


def ada_update(dmods, c_t, w, m, v, tr=256):
    L, D, W = w.shape

    def body(dm_ref, c_ref, w_ref, m_ref, v_ref, g_out, d_out, m_out, v_out):
        mx, my, _ = _place()
        shard = 2 * mx + my
        dm = jnp.zeros((8, W), F32)
        for s in range(N_SHARD):
            dm = dm + jnp.where(shard == s, dm_ref[0, :, s * W:(s + 1) * W], 0.0)
        cv = c_ref[...]
        ca = cv * _sigmoid(cv)
        g = jnp.zeros((tr, W), F32)
        for b in range(8):
            g = g + ca[:, b:b + 1] * dm[b:b + 1, :]
        d, m2, v2 = _adamw(w_ref[0], g, m_ref[0], v_ref[0])
        g_out[0], d_out[0], m_out[0], v_out[0] = g, d, m2, v2

    blk = pl.BlockSpec((1, tr, W), lambda l, i: (l, i, 0))
    return pl.pallas_call(
        body, name="ada_update", grid=(L, D // tr),
        in_specs=[pl.BlockSpec((1, 8, 3 * D), lambda l, i: (l, 0, 0)), pl.BlockSpec((tr, 8), lambda l, i: (i, 0)),
                  blk, blk, blk],
        out_specs=[blk] * 4, out_shape=[jax.ShapeDtypeStruct(w.shape, F32)] * 4,
        compiler_params=_cp("arbitrary", "arbitrary"),
    )(dmods, c_t, w, m, v)


STAT_ROWS = 16


def small_update(stats_all, norm, b_ada, final):
    def body(s_ref, *refs):
        ins, outs = refs[:9], refs[9:]
        tot = s_ref[0]
        for k in range(1, 8):
            tot = tot + s_ref[k]
        g_norm = tot[0:2, :]
        g_final = tot[2:3, :]
        g_b = jnp.concatenate(
            [jnp.concatenate([tot[3 + 3 * l + t:4 + 3 * l + t, :] for t in range(3)], axis=1) for l in range(DEPTH)],
            axis=0)
        for p, g in enumerate((g_norm, g_b, g_final)):
            w_ref, m_ref, v_ref = ins[3 * p:3 * p + 3]
            d, m2, v2 = _adamw(w_ref[...], g, m_ref[...], v_ref[...])
            for o_ref, val in zip(outs[4 * p:4 * p + 4], (g, d, m2, v2)):
                o_ref[...] = val
        loss = (0.5 / D_MODEL) * jnp.sum(tot[9:10, :], axis=1, keepdims=True)
        outs[12][...] = jnp.broadcast_to(loss, (8, LANES))

    shapes = []
    for w, _, _ in (norm, b_ada, final):
        shapes += [jax.ShapeDtypeStruct(w.shape, F32)] * 4
    shapes.append(jax.ShapeDtypeStruct((8, LANES), F32))
    return pl.pallas_call(body, name="small_update", out_shape=shapes)(stats_all, *norm, *b_ada, *final)


def kernel(x, c, norm_g, w_ada, b_ada, w_in, w_out, final_g, loss_target, m_norm_g, m_w_ada, m_b_ada, m_w_in, m_w_out, m_final_g, v_norm_g, v_w_ada, v_b_ada, v_w_in, v_w_out, v_final_g):
    S, D = x.shape[1], x.shape[2]
    mc = lax.axis_index("c")
    out_rows = D // N_SHARD

    def my_half(a, rows):
        return lax.dynamic_slice_in_dim(a, mc * rows, rows, axis=0)

    wblk = [jnp.concatenate([my_half(w_in[l], D // 2), my_half(w_out[l], out_rows // 2)], axis=0).astype(BF16)
            for l in range(DEPTH)]

    def unpack(wall):
        wall = wall.reshape(N_SHARD, 2, wall.shape[1], SHARD_W)
        return wall[:, :, :D // 2].reshape(N_SHARD, D, SHARD_W), wall[:, :, D // 2:].reshape(D, D)

    weights = [unpack(allgather8(wblk[0], "gather_weights"))]

    c_all = allgather8(jnp.broadcast_to(c, (8, D)), "gather_c")[:, 0, :]
    mod_all = allgather8(ada_fwd(c_all, w_ada).reshape(DEPTH * 8, -1), "gather_mod")
    vecs = vecs_build(mod_all, b_ada, norm_g)

    tabs = (*rope_tables(S), ret_log_gamma())
    h = x[0]
    saved = []
    for l in range(DEPTH):
        h, sv, wall = layer_fwd(h, vecs[l], *weights[l], tabs, wblk[l + 1] if l + 1 < DEPTH else None)
        saved.append(sv)
        if wall is not None:
            weights.append(unpack(wall))
    dx, st_loss = loss_head(h, jnp.broadcast_to(final_g[None, :], (8, D)), loss_target[0])

    dmod, dnorm, grads = [None] * DEPTH, [None] * DEPTH, None
    for l in reversed(range(DEPTH)):
        dx, dmod[l], dnorm[l], grads = layer_bwd(dx, saved[l], vecs[l], *weights[l], tabs, grads)

    p_own = sum_slots(*grads)
    p_sib = swap_sibling(p_own)
    res_in = adam_slab(p_own, p_sib, w_in, m_w_in, v_w_in, 0, "adam_w_in")
    res_out = adam_slab(p_own, p_sib, w_out, m_w_out, v_w_out, DEPTH * D, "adam_w_out", tr=128)

    stats = jnp.concatenate(dnorm + [st_loss[0:1]] + dmod + [st_loss[1:2], jnp.zeros((STAT_ROWS - 10, D), F32)], axis=0)
    stats_all = allgather8(stats, "gather_stats")
    dmods = stats_all[:, 3:9, :].reshape(8, DEPTH, 3 * D).transpose(1, 0, 2)
    res_ada = ada_update(dmods, c_all.T, w_ada, m_w_ada, v_w_ada)
    small = small_update(stats_all, (norm_g, m_norm_g, v_norm_g), (b_ada, m_b_ada, v_b_ada),
                         (final_g[None, :], m_final_g[None, :], v_final_g[None, :]))
    res_norm, res_b, res_final = small[0:4], small[4:8], [a[0] for a in small[8:12]]
    loss = small[12][0, 0]

    by_kind = [res_norm, res_ada, res_b, res_in, res_out, res_final]
    outs = [loss, dx[None]]
    for kind in range(4):
        outs += [r[kind] for r in by_kind]
    return tuple(outs)
```

```python
import functools

import numpy as np
import jax
import jax.numpy as jnp
from jax import lax
from jax.experimental import pallas as pl
from jax.experimental.pallas import tpu as pltpu

F32, BF16 = jnp.float32, jnp.bfloat16
MESH = pl.DeviceIdType.MESH

D_MODEL = 1024
DEPTH = 2
SHARD_W = 1024
N_SHARD = 4
GROUP_W = 512
LANES = 128
SB_HEAD_DIM = 64
RET_HEAD_DIM = 128
CHUNK = 64
ROPE_BASE = 10000.0
EPS = 1e-6
SQ_SCALE = SB_HEAD_DIM ** -0.5
RK_SCALE = RET_HEAD_DIM ** -0.5
SB_T = 256
RET_T = 256
EXP_ZERO = -104.0
VMEM_LIMIT_BYTES = 56 * 2 ** 20

ADAM_LR, ADAM_B1, ADAM_B2, ADAM_EPS, ADAM_WD, ADAM_STEP = 0.001, 0.9, 0.999, 1e-08, 0.01, 10


def _cp(*sem):
    return pltpu.CompilerParams(dimension_semantics=sem, vmem_limit_bytes=VMEM_LIMIT_BYTES)


def _dot(a, b):
    return lax.dot_general(a, b, (((1,), (0,)), ((), ())), preferred_element_type=F32)


def _dot_nt(a, b):
    return lax.dot_general(a, b, (((1,), (1,)), ((), ())), preferred_element_type=F32)


def _dot_tn(a, b):
    return lax.dot_general(a, b, (((0,), (0,)), ((), ())), preferred_element_type=F32)


def _running_sum(a, tri):
    return _dot(a.astype(BF16), tri)


def _sigmoid(x):
    return 1.0 / (1.0 + jnp.exp(-x))


def _rowsum(a):
    return jnp.sum(a, axis=1, keepdims=True)


def _rowmean(a):
    return jnp.mean(a, axis=1, keepdims=True)


def inproj_fwd(x, vecs, w3, tm=256):
    S, D = x.shape

    def body(x_ref, v_ref, w_ref, proj_ref, h_ref, sb_ref):
        xv = x_ref[...]
        r = lax.rsqrt(_rowmean(xv * xv) + EPS)
        h = xv * r * v_ref[3:4, :] * (1.0 + v_ref[1:2, :]) + v_ref[0:1, :]
        hb = h.astype(BF16)
        h_ref[...] = hb
        for s in range(N_SHARD):
            p = _dot(hb, w_ref[s])
            proj_ref[:, s * SHARD_W:(s + 1) * SHARD_W] = p
            if s == 2:
                sb_ref[:, 0:GROUP_W] = (p[:, 0:GROUP_W] * SQ_SCALE).astype(BF16)
                sb_ref[:, GROUP_W:SHARD_W] = p[:, GROUP_W:].astype(BF16)
            if s == 3:
                sb_ref[:, SHARD_W:2 * SHARD_W] = p.astype(BF16)

    return pl.pallas_call(
        body, name="inproj_fwd", grid=(S // tm,),
        in_specs=[pl.BlockSpec((tm, D), lambda i: (i, 0)),
                  pl.BlockSpec((8, D), lambda i: (0, 0)),
                  pl.BlockSpec((N_SHARD, D, SHARD_W), lambda i: (0, 0, 0))],
        out_specs=[pl.BlockSpec((tm, 4 * D), lambda i: (i, 0)),
                   pl.BlockSpec((tm, D), lambda i: (i, 0)),
                   pl.BlockSpec((tm, 2 * SHARD_W), lambda i: (i, 0))],
        out_shape=[jax.ShapeDtypeStruct((S, 4 * D), F32),
                   jax.ShapeDtypeStruct((S, D), BF16),
                   jax.ShapeDtypeStruct((S, 2 * SHARD_W), BF16)],
        compiler_params=_cp("arbitrary"),
    )(x, vecs, w3)


def _sb_logits(qh, k2, mask):
    z = _dot_nt(qh, k2)
    sp = jnp.log(1.0 + jnp.exp(-jnp.abs(z)))
    lb = jnp.minimum(z, 0.0) - sp
    lk = lb - z
    if mask is not None:
        lk = jnp.where(mask, lk, 0.0)
    return lb, lk


def _sb_masks(i):
    T = SB_T
    first = jnp.maximum(i - 1, 0)
    row = lax.broadcasted_iota(jnp.int32, (2 * T, 2 * T), 0)
    col = lax.broadcasted_iota(jnp.int32, (2 * T, 2 * T), 1)
    causal = first * T + col < i * T + (row & (T - 1))
    r = lax.broadcasted_iota(jnp.int32, (T, T), 0)
    c = lax.broadcasted_iota(jnp.int32, (T, T), 1)
    later = jnp.where(r > c, 1.0, 0.0).astype(BF16)
    earlier = jnp.where(r < c, 1.0, 0.0).astype(BF16)
    lane = lax.broadcasted_iota(jnp.int32, (1, LANES), 1)
    return first, causal, later, earlier, lane < SB_HEAD_DIM


def _sb_rows(ref, j, tiles=1):
    return ref[pl.ds(pl.multiple_of(j * SB_T, SB_T), tiles * SB_T), :]


def sb_fwd(sb, proj, gather=None):
    S = sb.shape[0]
    T = SB_T
    nq = S // T
    carried = [] if gather is None else [gather]

    def body(*refs):
        (q_ref, k_ref, v_ref, sg_ref), refs = refs[:4], refs[4:]
        p, i = pl.program_id(0), pl.program_id(1)
        if carried:
            x_ref, y_ref, o_ref, end_ref, out_ref, send_sems, recv_sems, local_sem = refs
            start, forward, finish = _gather_ops(x_ref, out_ref, send_sems, recv_sems, local_sem)
            pl.when(jnp.logical_and(p == 0, i == 0))(start)
            pl.when(jnp.logical_and(p == 2, i == 0))(forward)
        else:
            y_ref, o_ref, end_ref = refs
        first, causal, later, _, head0 = _sb_masks(i)
        q2 = q_ref[...]
        zero = jnp.zeros_like(q2)
        qs = jnp.concatenate([jnp.where(head0, q2, zero), jnp.where(head0, zero, q2)], axis=0)

        lb, lk = _sb_logits(qs, _sb_rows(k_ref, first, 2), causal)
        rs_left, rs_right = _rowsum(lk[:, :T]), _rowsum(lk[:, T:])
        suffix = jnp.concatenate([_running_sum(lk[:, :T], later) + rs_right, _running_sum(lk[:, T:], later)], axis=1)
        a = jnp.where(causal, jnp.exp(lb + suffix), 0.0)
        acc = _dot(a.astype(BF16), _sb_rows(v_ref, first, 2))
        R = rs_left + rs_right

        def tile(j, R):
            lb, lk = _sb_logits(qs, _sb_rows(k_ref, j), None)
            a = jnp.exp(lb + _running_sum(lk, later) + R)
            return _dot(a.astype(BF16), _sb_rows(v_ref, j)), R + _rowsum(lk)

        def cond(st):
            return jnp.logical_and(st[0] >= 0, st[3] > EXP_ZERO)

        def step(st):
            c, Rn = tile(st[0], st[2])
            return st[0] - 1, st[1] + c, Rn, jnp.max(Rn)

        j_end, acc, R, _ = lax.while_loop(cond, step, (first - 1, acc, R, jnp.max(R)))
        end_ref[0, 0, 0:2 * T, :] = jnp.broadcast_to(R, (2 * T, LANES))
        end_ref[0, 0, 2 * T:, :] = jnp.full((8, LANES), j_end.astype(F32))
        o = jnp.where(head0, acc[:T], acc[T:])
        o_ref[...] = o
        sg = sg_ref[...]
        y_ref[...] = (o * (sg * _sigmoid(sg))).astype(BF16)
        if carried:
            pl.when(jnp.logical_and(p == 3, i == nq - 1))(finish)

    return pl.pallas_call(
        body, name="sb_fwd", grid=(4, nq),
        in_specs=[pl.BlockSpec((T, LANES), lambda p, i: (i, p)),
                  pl.BlockSpec((S, LANES), lambda p, i: (0, 4 + p)),
                  pl.BlockSpec((S, LANES), lambda p, i: (0, 8 + p)),
                  pl.BlockSpec((T, LANES), lambda p, i: (i, 28 + p))] + [_ANY for _ in carried],
        out_specs=[pl.BlockSpec((T, LANES), lambda p, i: (i, p)),
                   pl.BlockSpec((T, LANES), lambda p, i: (i, p)),
                   pl.BlockSpec((1, 1, 2 * T + 8, LANES), lambda p, i: (p, i, 0, 0))] + [_ANY for _ in carried],
        out_shape=[jax.ShapeDtypeStruct((S, GROUP_W), BF16),
                   jax.ShapeDtypeStruct((S, GROUP_W), F32),
                   jax.ShapeDtypeStruct((4, nq, 2 * T + 8, LANES), F32)]
        + [jax.ShapeDtypeStruct((8,) + a.shape, a.dtype) for a in carried],
        scratch_shapes=_GATHER_SCRATCH if carried else [],
        compiler_params=_cp("arbitrary", "arbitrary"),
    )(sb, sb, sb, proj, *carried)


def sb_bwd(sb, proj, o, sb_end, dycat, ship=None):
    S = sb.shape[0]
    T = SB_T
    nq = S // T
    ex = _Exchange(ship)

    def body(*refs):
        (q_ref, k_ref, v_ref, sg_ref, o_ref, dy_ref, end_ref), refs = refs[:7], refs[7:]
        ship_refs, (dq_ref, dk_ref, dv_ref, dsg_ref), refs = refs[:ex.n_in], refs[ex.n_in:ex.n_in + 4], refs[ex.n_in + 4:]
        recv, (dk_acc, dv_acc), sems = refs[:ex.n_out], refs[ex.n_out:ex.n_out + 2], refs[ex.n_out + 2:]
        start, finish = ex.ops(ship_refs, recv + sems)
        p, i = pl.program_id(0), pl.program_id(1)
        pl.when(jnp.logical_and(p == 0, i == 0))(start)

        @pl.when(i == 0)
        def _():
            dk_acc[...] = jnp.zeros_like(dk_acc)
            dv_acc[...] = jnp.zeros_like(dv_acc)

        first, causal, later, earlier, head0 = _sb_masks(i)
        q2 = q_ref[...]
        sg = sg_ref[...]
        sig = _sigmoid(sg)
        dy = dy_ref[...]
        dsg_ref[...] = (dy * o_ref[...] * (sig * (1.0 + sg * (1.0 - sig)))).astype(BF16)
        do_b = (dy * (sg * sig)).astype(BF16)
        zero = jnp.zeros_like(q2)
        qs = jnp.concatenate([jnp.where(head0, q2, zero), jnp.where(head0, zero, q2)], axis=0)
        dos = jnp.concatenate([jnp.where(head0, do_b, zero), jnp.where(head0, zero, do_b)], axis=0)

        end = end_ref[0, 0]
        j_end = jnp.max(end[2 * T:, :]).astype(jnp.int32)

        def grads(j, tiles, a, lb, g, G, mask):
            dz = g - jnp.exp(lb) * (g + G)
            if mask is not None:
                dz = jnp.where(mask, dz, 0.0)
            dzb = dz.astype(BF16)
            rows = pl.ds(pl.multiple_of(j * T, T), tiles * T)
            dk_acc[rows, :] += _dot_tn(dzb, qs)
            dv_acc[rows, :] += _dot_tn(a.astype(BF16), dos)
            return _dot(dzb, _sb_rows(k_ref, j, tiles))

        def sweep(j, st):
            dq, G0, left = st
            lb, lk = _sb_logits(qs, _sb_rows(k_ref, j), None)
            stick = left - _rowsum(lk)
            a = jnp.exp(lb + _running_sum(lk, later) + stick)
            g = a * _dot_nt(dos, _sb_rows(v_ref, j))
            G = _running_sum(g, earlier) + G0
            return dq + grads(j, 1, a, lb, g, G, None), G0 + _rowsum(g), stick

        dq, G0, _ = lax.fori_loop(j_end + 1, first, sweep,
                                  (jnp.zeros((2 * T, LANES), F32), jnp.zeros((2 * T, 1), F32), end[:2 * T, 0:1]))

        lb, lk = _sb_logits(qs, _sb_rows(k_ref, first, 2), causal)
        suffix = jnp.concatenate([_running_sum(lk[:, :T], later) + _rowsum(lk[:, T:]),
                                  _running_sum(lk[:, T:], later)], axis=1)
        a = jnp.where(causal, jnp.exp(lb + suffix), 0.0)
        g = a * _dot_nt(dos, _sb_rows(v_ref, first, 2))
        G = jnp.concatenate([_running_sum(g[:, :T], earlier) + G0,
                             _running_sum(g[:, T:], earlier) + (G0 + _rowsum(g[:, :T]))], axis=1)
        dq = dq + grads(first, 2, a, lb, g, G, causal)
        dq_ref[...] = (jnp.where(head0, dq[:T], dq[T:]) * SQ_SCALE).astype(BF16)

        @pl.when(i == nq - 1)
        def _():
            dk_ref[...] = dk_acc[...].astype(BF16)
            dv_ref[...] = dv_acc[...].astype(BF16)

        pl.when(jnp.logical_and(p == 3, i == nq - 1))(finish)

    tile_spec = lambda c0: pl.BlockSpec((T, LANES), lambda p, i: (i, c0 + p))
    head_spec = lambda c0: pl.BlockSpec((S, LANES), lambda p, i: (0, c0 + p))
    return pl.pallas_call(
        body, name="sb_bwd", grid=(4, nq),
        in_specs=[tile_spec(0), head_spec(4), head_spec(8), tile_spec(28), tile_spec(0), tile_spec(4),
                  pl.BlockSpec((1, 1, 2 * T + 8, LANES), lambda p, i: (p, i, 0, 0))] + ex.in_specs,
        out_specs=[tile_spec(0), head_spec(0), head_spec(0), tile_spec(0)] + ex.out_specs,
        out_shape=[jax.ShapeDtypeStruct((S, GROUP_W), BF16)] * 4 + ex.out_shape,
        scratch_shapes=[pltpu.VMEM((S, LANES), F32), pltpu.VMEM((S, LANES), F32)] + ex.scratch,
        compiler_params=_cp("arbitrary", "arbitrary"),
    )(sb, sb, sb, proj, o, dycat, sb_end, *ex.ship)


def rope_tables(S):
    half = RET_HEAD_DIM // 2
    inv = ROPE_BASE ** (-jnp.arange(half, dtype=F32) / half)
    ang = jnp.arange(S, dtype=F32)[:, None] * inv[None, :]
    cos, sin = jnp.cos(ang), jnp.sin(ang)
    return jnp.concatenate([cos, cos], axis=1), jnp.concatenate([-sin, sin], axis=1)


def ret_log_gamma():
    return jnp.log1p(-(2.0 ** (-5.0 - jnp.arange(4, dtype=F32))))


def _swap_halves(a):
    return pltpu.roll(a, RET_HEAD_DIM // 2, axis=1)


def _ret_decay_mask(lg):
    n = lax.broadcasted_iota(jnp.int32, (RET_T, RET_T), 0)
    m = lax.broadcasted_iota(jnp.int32, (RET_T, RET_T), 1)
    dist = jnp.abs(n - m).astype(F32)
    return jnp.where((m // CHUNK) <= (n // CHUNK), jnp.exp(lg * dist), 0.0)


def _ret_block(lg, rq, rk, rv, cosf, sinf, dm):
    q = rq * cosf + _swap_halves(rq) * sinf
    k = (rk * cosf + _swap_halves(rk) * sinf) * RK_SCALE
    qb, kb, vb = q.astype(BF16), k.astype(BF16), rv.astype(BF16)
    sc = _dot_nt(qb, kb) * dm
    nloc = lax.broadcasted_iota(jnp.int32, (RET_T, 1), 0).astype(F32)
    qdec = jnp.exp(lg * (nloc + 1.0))
    kdec = jnp.exp(lg * (RET_T - 1.0 - nloc))
    block_dec = jnp.exp(jnp.full((1, LANES), lg * RET_T, F32))
    return q, k, qb, kb, vb, sc, qdec, kdec, block_dec


def _ret_specs(S, rb):
    group = lambda c0: pl.BlockSpec((RET_T, GROUP_W), lambda b: (rb(b), c0))
    return group, pl.BlockSpec((RET_T, LANES), lambda b: (rb(b), 0))


def _head(ref, h):
    return ref[:, h * LANES:(h + 1) * LANES]


def ret_fwd(proj, cosf, sinf, lgam):
    S = proj.shape[0]
    nb = S // RET_T
    group, row_tab = _ret_specs(S, lambda b: b)

    def body(lg_ref, rq_ref, rk_ref, rv_ref, rg_ref, cos_ref, sin_ref, y_ref, o_ref, st_out, st_ref, dm_ref):
        @pl.when(pl.program_id(0) == 0)
        def _():
            st_ref[...] = jnp.zeros_like(st_ref)
            for h in range(4):
                dm_ref[h] = _ret_decay_mask(lg_ref[h])

        cosf, sinf = cos_ref[...], sin_ref[...]
        for h in range(4):
            lg = lg_ref[h]
            q, k, qb, kb, vb, sc, qdec, kdec, block_dec = _ret_block(
                lg, _head(rq_ref, h), _head(rk_ref, h), _head(rv_ref, h), cosf, sinf, dm_ref[h])
            st = st_ref[h]
            st_out[h, 0] = st
            o = _dot(sc.astype(BF16), vb) + _dot(qb, st.astype(BF16)) * qdec
            st_ref[h] = st * block_dec + _dot_tn((k * kdec).astype(BF16), vb)
            o_ref[:, h * LANES:(h + 1) * LANES] = o
            cen = o - _rowmean(o)
            on = cen * lax.rsqrt(_rowmean(cen * cen) + EPS)
            rg = _head(rg_ref, h)
            y_ref[:, h * LANES:(h + 1) * LANES] = (on * (rg * _sigmoid(rg))).astype(BF16)

    return pl.pallas_call(
        body, name="ret_fwd", grid=(nb,),
        in_specs=[pl.BlockSpec(memory_space=pltpu.SMEM),
                  group(0), group(1), group(2), group(3), row_tab, row_tab],
        out_specs=[group(0), group(0),
                   pl.BlockSpec((4, 1, LANES, LANES), lambda b: (0, b, 0, 0))],
        out_shape=[jax.ShapeDtypeStruct((S, GROUP_W), BF16),
                   jax.ShapeDtypeStruct((S, GROUP_W), F32),
                   jax.ShapeDtypeStruct((4, nb, LANES, LANES), F32)],
        scratch_shapes=[pltpu.VMEM((4, LANES, LANES), F32), pltpu.VMEM((4, RET_T, RET_T), F32)],
        compiler_params=_cp("arbitrary"),
    )(lgam, proj, proj, proj, proj, cosf, sinf)


def ret_bwd(proj, cosf, sinf, lgam, o, states, dycat):
    S = proj.shape[0]
    nb = S // RET_T
    rev = lambda b: nb - 1 - b
    group, row_tab = _ret_specs(S, rev)

    def body(lg_ref, rq_ref, rk_ref, rv_ref, rg_ref, cos_ref, sin_ref, o_ref, st_in, dy_ref,
             drq_ref, drk_ref, drv_ref, drg_ref, ds_ref, dm_ref):
        @pl.when(pl.program_id(0) == 0)
        def _():
            ds_ref[...] = jnp.zeros_like(ds_ref)
            for h in range(4):
                dm_ref[h] = _ret_decay_mask(lg_ref[h])

        cosf, sinf = cos_ref[...], sin_ref[...]
        for h in range(4):
            lanes = slice(h * LANES, (h + 1) * LANES)
            dm = dm_ref[h]
            q, k, qb, kb, vb, sc, qdec, kdec, block_dec = _ret_block(
                lg_ref[h], _head(rq_ref, h), _head(rk_ref, h), _head(rv_ref, h), cosf, sinf, dm)
            o_v = _head(o_ref, h)
            cen = o_v - _rowmean(o_v)
            rstd = lax.rsqrt(_rowmean(cen * cen) + EPS)
            on = cen * rstd
            rg = _head(rg_ref, h)
            sig = _sigmoid(rg)
            dy = _head(dy_ref, h)
            drg_ref[:, lanes] = (dy * on * (sig * (1.0 + rg * (1.0 - sig)))).astype(BF16)
            don = dy * (rg * sig)
            do = rstd * (don - _rowmean(don) - on * _rowmean(don * on))
            dob = do.astype(BF16)
            dsc = (_dot_nt(dob, vb) * dm).astype(BF16)
            st_b = st_in[h, 0].astype(BF16)
            dsn = ds_ref[h]
            dsn_b = dsn.astype(BF16)
            dq = _dot(dsc, kb) + _dot_nt(dob, st_b) * qdec
            dk = _dot_tn(dsc, qb) + _dot_nt(vb, dsn_b) * kdec
            dv = _dot_tn(sc.astype(BF16), dob) + _dot((k * kdec).astype(BF16), dsn_b)
            ds_ref[h] = dsn * block_dec + _dot_tn((q * qdec).astype(BF16), dob)
            dk = dk * RK_SCALE
            drq_ref[:, lanes] = (dq * cosf + _swap_halves(dq * sinf)).astype(BF16)
            drk_ref[:, lanes] = (dk * cosf + _swap_halves(dk * sinf)).astype(BF16)
            drv_ref[:, lanes] = dv.astype(BF16)

    return pl.pallas_call(
        body, name="ret_bwd", grid=(nb,),
        in_specs=[pl.BlockSpec(memory_space=pltpu.SMEM),
                  group(0), group(1), group(2), group(3), row_tab, row_tab,
                  group(0), pl.BlockSpec((4, 1, LANES, LANES), lambda b: (0, rev(b), 0, 0)),
                  group(0)],
        out_specs=[group(0)] * 4,
        out_shape=[jax.ShapeDtypeStruct((S, GROUP_W), BF16)] * 4,
        scratch_shapes=[pltpu.VMEM((4, LANES, LANES), F32), pltpu.VMEM((4, RET_T, RET_T), F32)],
        compiler_params=_cp("arbitrary"),
    )(lgam, proj, proj, proj, proj, cosf, sinf, o, states, dycat)


def outproj_fwd(x, vecs, y_ret, y_sb, w_out, tm=512):
    S, D = x.shape

    def body(x_ref, v_ref, yr_ref, ys_ref, w_ref, y_ref, xo_ref):
        y = _dot(yr_ref[...], w_ref[0:GROUP_W, :]) + _dot(ys_ref[...], w_ref[GROUP_W:, :])
        y_ref[...] = y
        xo_ref[...] = x_ref[...] + v_ref[2:3, :] * y

    row = lambda w: pl.BlockSpec((tm, w), lambda i: (i, 0))
    return pl.pallas_call(
        body, name="outproj_fwd", grid=(S // tm,),
        in_specs=[row(D), pl.BlockSpec((8, D), lambda i: (0, 0)), row(GROUP_W), row(GROUP_W),
                  pl.BlockSpec((D, D), lambda i: (0, 0))],
        out_specs=[row(D), row(D)],
        out_shape=[jax.ShapeDtypeStruct((S, D), F32)] * 2,
        compiler_params=_cp("arbitrary"),
    )(x, vecs, y_ret, y_sb, w_out)


def loss_head(x, final_g, target, tm=512):
    S, D = x.shape

    def body(x_ref, g_ref, t_ref, dx_ref, st_ref):
        @pl.when(pl.program_id(0) == 0)
        def _():
            st_ref[...] = jnp.zeros_like(st_ref)

        xv = x_ref[...]
        g = g_ref[0:1, :]
        r = lax.rsqrt(_rowmean(xv * xv) + EPS)
        xn = xv * r
        err = xn * g - t_ref[...]
        dy = err * (1.0 / D)
        dxn = dy * g
        dx_ref[...] = r * (dxn - xn * _rowmean(dxn * xn))
        st_ref[0:1, :] += jnp.sum(dy * xn, axis=0, keepdims=True)
        st_ref[1:2, :] += jnp.sum(err * err, axis=0, keepdims=True)

    row = pl.BlockSpec((tm, D), lambda i: (i, 0))
    fixed = pl.BlockSpec((8, D), lambda i: (0, 0))
    return pl.pallas_call(
        body, name="loss_head", grid=(S // tm,),
        in_specs=[row, fixed, row], out_specs=[row, fixed],
        out_shape=[jax.ShapeDtypeStruct((S, D), F32), jax.ShapeDtypeStruct((8, D), F32)],
        compiler_params=_cp("arbitrary"),
    )(x, final_g, target)


def outproj_bwd(dx, y, vecs, y_ret, y_sb, w_out, tm=512):
    S, D = dx.shape
    n = S // tm

    def body(dx_ref, y_ref, v_ref, yr_ref, ys_ref, w_ref, dyc_ref, dw_ref, st_ref, acc):
        i = pl.program_id(0)

        @pl.when(i == 0)
        def _():
            st_ref[...] = jnp.zeros_like(st_ref)
            acc[...] = jnp.zeros_like(acc)

        dxv = dx_ref[...]
        st_ref[0:1, :] += jnp.sum(dxv * y_ref[...], axis=0, keepdims=True)
        dyy = (dxv * v_ref[2:3, :]).astype(BF16)
        dyc_ref[...] = _dot_nt(dyy, w_ref[...])
        acc[0:GROUP_W, :] += _dot_tn(yr_ref[...], dyy)
        acc[GROUP_W:, :] += _dot_tn(ys_ref[...], dyy)

        @pl.when(i == n - 1)
        def _():
            dw_ref[...] = acc[...].astype(BF16)

    row = lambda w: pl.BlockSpec((tm, w), lambda i: (i, 0))
    fixed = lambda r: pl.BlockSpec((r, D), lambda i: (0, 0))
    return pl.pallas_call(
        body, name="outproj_bwd", grid=(n,),
        in_specs=[row(D), row(D), fixed(8), row(GROUP_W), row(GROUP_W), fixed(D)],
        out_specs=[row(D), fixed(D), fixed(8)],
        out_shape=[jax.ShapeDtypeStruct((S, D), F32), jax.ShapeDtypeStruct((D, D), BF16),
                   jax.ShapeDtypeStruct((8, D), F32)],
        scratch_shapes=[pltpu.VMEM((D, D), F32)],
        compiler_params=_cp("arbitrary"),
    )(dx, y, vecs, y_ret, y_sb, w_out)


def inproj_bwd_x(pieces, w3, x, vecs, dx_res, ship=None, tm=256):
    S, D = x.shape
    n = S // tm
    ex = _Exchange(ship)

    def body(*refs):
        p_refs, (w_ref, x_ref, v_ref, dr_ref), refs = refs[:8], refs[8:12], refs[12:]
        ship_refs, (dx_ref, st_ref), refs = refs[:ex.n_in], refs[ex.n_in:ex.n_in + 2], refs[ex.n_in + 2:]
        start, finish = ex.ops(ship_refs, refs)

        @pl.when(pl.program_id(0) == 0)
        def _():
            st_ref[...] = jnp.zeros_like(st_ref)
            start()

        dh = jnp.zeros((tm, D), F32)
        for k, p_ref in enumerate(p_refs):
            c0 = (k % 2) * GROUP_W
            dh = dh + _dot_nt(p_ref[...], w_ref[k // 2, :, c0:c0 + GROUP_W])
        xv = x_ref[...]
        r = lax.rsqrt(_rowmean(xv * xv) + EPS)
        xn = xv * r
        g, scale1 = v_ref[3:4, :], 1.0 + v_ref[1:2, :]
        st_ref[0:1, :] += jnp.sum(dh, axis=0, keepdims=True)
        dh_xn = dh * xn
        st_ref[1:2, :] += jnp.sum(dh_xn, axis=0, keepdims=True) * g
        st_ref[2:3, :] += jnp.sum(dh_xn, axis=0, keepdims=True) * scale1
        dxn = dh * (g * scale1)
        dx_ref[...] = r * (dxn - xn * _rowmean(dxn * xn)) + dr_ref[...]
        pl.when(pl.program_id(0) == n - 1)(finish)

    row = lambda w: pl.BlockSpec((tm, w), lambda i: (i, 0))
    return pl.pallas_call(
        body, name="inproj_bwd_x", grid=(n,),
        in_specs=[row(GROUP_W)] * 8 + [pl.BlockSpec((N_SHARD, D, SHARD_W), lambda i: (0, 0, 0)),
                                       row(D), pl.BlockSpec((8, D), lambda i: (0, 0)), row(D)] + ex.in_specs,
        out_specs=[row(D), pl.BlockSpec((8, D), lambda i: (0, 0))] + ex.out_specs,
        out_shape=[jax.ShapeDtypeStruct((S, D), F32), jax.ShapeDtypeStruct((8, D), F32)] + ex.out_shape,
        scratch_shapes=ex.scratch,
        compiler_params=_cp("arbitrary"),
    )(*pieces, w3, x, vecs, dx_res, *ex.ship)


def inproj_bwd_w(h, pieces, tm=512):
    S, D = h.shape
    n = S // tm

    def body(*refs):
        h_ref, p_refs, dw_ref, acc = refs[0], refs[1:9], refs[9], refs[10]
        i = pl.program_id(0)

        @pl.when(i == 0)
        def _():
            acc[...] = jnp.zeros_like(acc)

        hv = h_ref[...]
        for k, p_ref in enumerate(p_refs):
            c0 = (k % 2) * GROUP_W
            acc[k // 2, :, c0:c0 + GROUP_W] += _dot_tn(hv, p_ref[...])

        @pl.when(i == n - 1)
        def _():
            dw_ref[...] = acc[...].astype(BF16)

    row = lambda w: pl.BlockSpec((tm, w), lambda i: (i, 0))
    return pl.pallas_call(
        body, name="inproj_bwd_w", grid=(n,),
        in_specs=[row(D)] + [row(GROUP_W)] * 8,
        out_specs=pl.BlockSpec((N_SHARD, D, SHARD_W), lambda i: (0, 0, 0)),
        out_shape=jax.ShapeDtypeStruct((N_SHARD, D, SHARD_W), BF16),
        scratch_shapes=[pltpu.VMEM((N_SHARD, D, SHARD_W), F32)],
        compiler_params=_cp("arbitrary"),
    )(h, *pieces)


def layer_fwd(x, vecs, w3, w_out, tabs, gather=None):
    cosf, sinf, lgam = tabs
    proj, h, sb = inproj_fwd(x, vecs, w3)
    y_ret, o_ret, states = ret_fwd(proj, cosf, sinf, lgam)
    y_sb, o_sb, sb_end, *gathered = sb_fwd(sb, proj, gather)
    y, x_next = outproj_fwd(x, vecs, y_ret, y_sb, w_out)
    saved = (x, proj, h, sb, y_ret, o_ret, states, y_sb, o_sb, sb_end, y)
    return x_next, saved, (gathered[0] if gathered else None)


def _by_shard(dw_out):
    return dw_out.reshape(N_SHARD, D_MODEL // N_SHARD, D_MODEL)


def layer_bwd(dx, saved, vecs, w3, w_out, tabs, later_grads=None):
    cosf, sinf, lgam = tabs
    x, proj, h, sb, y_ret, o_ret, states, y_sb, o_sb, sb_end, y = saved
    dycat, dw_out, st_o = outproj_bwd(dx, y, vecs, y_ret, y_sb, w_out)
    dw_out = _by_shard(dw_out)
    ship = None if later_grads is None else (later_grads[0], dw_out, later_grads[1])
    *d_sb, = sb_bwd(sb, proj, o_sb, sb_end, dycat, ship)
    d_ret = ret_bwd(proj, cosf, sinf, lgam, o_ret, states, dycat)
    pieces = list(d_ret) + d_sb[:4]
    dw_in = inproj_bwd_w(h, pieces)
    dx, st_i, *recv_in = inproj_bwd_x(pieces, w3, x, vecs, dx, None if later_grads is None else (dw_in,))
    dmod = jnp.concatenate([st_i[0:2], st_o[0:1]], axis=0)
    grads = (dw_in, dw_out) if later_grads is None else (recv_in[0], d_sb[4])
    return dx, dmod, st_i[2:3], grads


def _place():
    return lax.axis_index("x"), lax.axis_index("y"), lax.axis_index("c")


def _other_chips(mx, my):
    return [(1 - mx, my), (mx, 1 - my), (1 - mx, 1 - my)]


_ANY = pl.BlockSpec(memory_space=pl.ANY)


_GATHER_SCRATCH = [pltpu.SemaphoreType.DMA((7,)), pltpu.SemaphoreType.DMA((7,)), pltpu.SemaphoreType.DMA(())]


def _gather_ops(x_ref, out_ref, send_sems, recv_sems, local_sem):
    mx, my, mc = _place()
    me, sibling = (mx, my, mc), (mx, my, 1 - mc)
    chips = _other_chips(mx, my)

    def slot(px, py, pc):
        return out_ref.at[4 * px + 2 * py + pc]

    def copy(k, block, to, src=None):
        return pltpu.make_async_remote_copy(
            src_ref=slot(*block) if src is None else src, dst_ref=slot(*block),
            send_sem=send_sems.at[k], recv_sem=recv_sems.at[k], device_id=to, device_id_type=MESH)

    mine = pltpu.make_async_copy(x_ref, slot(*me), local_sem)
    first = [copy(0, me, sibling, src=x_ref)]
    first += [copy(1 + j, me, (*chip, mc), src=x_ref) for j, chip in enumerate(chips)]
    passed = [copy(4 + j, (*chip, mc), sibling) for j, chip in enumerate(chips)]

    def start():
        mine.start()
        for cp in first:
            cp.start()

    def forward():
        for j, chip in enumerate(chips):
            copy(1 + j, (*chip, mc), me).wait_recv()
            passed[j].start()

    def finish():
        copy(0, sibling, me).wait_recv()
        for j, chip in enumerate(chips):
            copy(4 + j, (*chip, 1 - mc), me).wait_recv()
        for cp in first + passed:
            cp.wait_send()
        mine.wait()

    return start, forward, finish


def allgather8(x, name):
    def body(x_ref, out_ref, send_sems, recv_sems, local_sem):
        for step in _gather_ops(x_ref, out_ref, send_sems, recv_sems, local_sem):
            step()

    return pl.pallas_call(
        body, name=name, out_shape=jax.ShapeDtypeStruct((8,) + x.shape, x.dtype),
        in_specs=[_ANY], out_specs=_ANY, scratch_shapes=_GATHER_SCRATCH,
    )(x)


class _Exchange:
    def __init__(self, ship):
        self.ship = list(ship or ())
        self.n_in = len(self.ship)
        self.n_out = 1 if self.ship else 0
        self.rows = [a.shape[1] for a in self.ship]
        self.in_specs = [_ANY] * self.n_in
        self.out_specs = [_ANY] * self.n_out
        self.out_shape = [jax.ShapeDtypeStruct((N_SHARD, sum(self.rows), SHARD_W), BF16)] * self.n_out
        sem = pltpu.SemaphoreType.DMA
        self.scratch = [sem((3,)), sem((3,)), sem(())] * self.n_out

    def ops(self, ship_refs, tail):
        if not self.ship:
            return (lambda: None), (lambda: None)
        recv, send_sems, recv_sems, local_sem = tail
        mx, my, mc = _place()
        my_chip = 2 * mx + my
        chips = _other_chips(mx, my)

        def pieces(s):
            firsts = np.cumsum([0] + self.rows[:-1])
            return [(ref.at[s], int(r0), n) for ref, r0, n in zip(ship_refs, firsts, self.rows)]

        def start():
            for src, r0, n in pieces(my_chip):
                pltpu.make_async_copy(src, recv.at[my_chip, pl.ds(r0, n)], local_sem).start()
            for j, (px, py) in enumerate(chips):
                for src, r0, n in pieces(2 * px + py):
                    pltpu.make_async_remote_copy(
                        src_ref=src, dst_ref=recv.at[my_chip, pl.ds(r0, n)],
                        send_sem=send_sems.at[j], recv_sem=recv_sems.at[j],
                        device_id=(px, py, mc), device_id_type=MESH).start()

        def finish():
            for j, (px, py) in enumerate(chips):
                whole = recv.at[2 * px + py]
                both = pltpu.make_async_remote_copy(
                    src_ref=whole, dst_ref=whole, send_sem=send_sems.at[j], recv_sem=recv_sems.at[j],
                    device_id=(px, py, mc), device_id_type=MESH)
                both.wait_recv()
                both.wait_send()
            pltpu.make_async_copy(recv.at[my_chip], recv.at[my_chip], local_sem).wait()

        return start, finish


def sum_slots(recv_a, recv_b, tr=256):
    n, rows_a, cols = recv_a.shape
    na, nb = rows_a // tr, recv_b.shape[1] // tr

    def body(a_ref, b_ref, o_ref):
        def total(r_ref):
            acc = r_ref[0].astype(F32)
            for k in range(1, n):
                acc = acc + r_ref[k].astype(F32)
            o_ref[...] = acc

        pl.when(pl.program_id(0) < na)(lambda: total(a_ref))
        pl.when(pl.program_id(0) >= na)(lambda: total(b_ref))

    return pl.pallas_call(
        body, name="sum_slots", grid=(na + nb,),
        in_specs=[pl.BlockSpec((n, tr, cols), lambda i: (0, jnp.minimum(i, na - 1), 0)),
                  pl.BlockSpec((n, tr, cols), lambda i: (0, jnp.maximum(i - na, 0), 0))],
        out_specs=pl.BlockSpec((tr, cols), lambda i: (i, 0)),
        out_shape=jax.ShapeDtypeStruct(((na + nb) * tr, cols), F32),
        compiler_params=_cp("arbitrary"),
    )(recv_a, recv_b)


def swap_sibling(p):
    def body(p_ref, out_ref, send_sem, recv_sem):
        mx, my, mc = _place()
        cp = pltpu.make_async_remote_copy(
            src_ref=p_ref, dst_ref=out_ref, send_sem=send_sem, recv_sem=recv_sem,
            device_id=(mx, my, 1 - mc), device_id_type=MESH)
        cp.start()
        cp.wait()

    return pl.pallas_call(
        body, name="swap_sibling", out_shape=jax.ShapeDtypeStruct(p.shape, p.dtype),
        in_specs=[_ANY], out_specs=_ANY,
        scratch_shapes=[pltpu.SemaphoreType.DMA(()), pltpu.SemaphoreType.DMA(())],
    )(p)


def _adamw(w, g, m, v):
    m = ADAM_B1 * m + (1.0 - ADAM_B1) * g
    v = ADAM_B2 * v + (1.0 - ADAM_B2) * (g * g)
    m_hat = m / (1.0 - ADAM_B1 ** ADAM_STEP)
    v_hat = v / (1.0 - ADAM_B2 ** ADAM_STEP)
    delta = -ADAM_LR * (m_hat / (jnp.sqrt(v_hat) + ADAM_EPS) + ADAM_WD * w)
    return delta, m, v


def adam_slab(p_own, p_sib, w, m, v, row0, name, tr=256):
    L, R, C = w.shape
    nr = R // tr

    def body(a_ref, b_ref, w_ref, m_ref, v_ref, g_out, d_out, m_out, v_out):
        g = a_ref[...] + b_ref[...]
        d, m2, v2 = _adamw(w_ref[0], g, m_ref[0], v_ref[0])
        g_out[0], d_out[0], m_out[0], v_out[0] = g, d, m2, v2

    slab = pl.BlockSpec((tr, C), lambda l, i: (row0 // tr + l * nr + i, 0))
    blk = pl.BlockSpec((1, tr, C), lambda l, i: (l, i, 0))
    return pl.pallas_call(
        body, name=name, grid=(L, nr),
        in_specs=[slab, slab, blk, blk, blk], out_specs=[blk] * 4,
        out_shape=[jax.ShapeDtypeStruct(w.shape, F32)] * 4,
        compiler_params=_cp("arbitrary", "arbitrary"),
    )(p_own, p_sib, w, m, v)


def ada_fwd(c_all, w_ada):
    L, D, W = w_ada.shape

    def body(c_ref, w_ref, o_ref):
        cv = c_ref[...]
        o_ref[0] = jnp.dot(cv * _sigmoid(cv), w_ref[0], precision=lax.Precision.HIGHEST,
                           preferred_element_type=F32)

    return pl.pallas_call(
        body, name="ada_fwd", grid=(L,),
        in_specs=[pl.BlockSpec((8, D), lambda l: (0, 0)), pl.BlockSpec((1, D, W), lambda l: (l, 0, 0))],
        out_specs=pl.BlockSpec((1, 8, W), lambda l: (l, 0, 0)),
        out_shape=jax.ShapeDtypeStruct((L, 8, W), F32),
        compiler_params=_cp("arbitrary"),
    )(c_all, w_ada)


def vecs_build(mod_all, b_ada, norm_g):
    W = mod_all.shape[2]

    def body(m_ref, b_ref, g_ref, o_ref):
        mx, my, mc = _place()
        me = 4 * mx + 2 * my + mc
        rowid = lax.broadcasted_iota(jnp.int32, (2 * 8, 1), 0)
        o_ref[...] = jnp.zeros_like(o_ref)
        for l in range(DEPTH):
            parts = [jnp.sum(jnp.where(rowid == l * 8 + me, m_ref[2 * s + mc], 0.0), axis=0, keepdims=True)
                     for s in range(N_SHARD)]
            mod = jnp.concatenate(parts, axis=1) + b_ref[l:l + 1, :]
            for t in range(3):
                o_ref[l, t:t + 1, :] = mod[:, t * D_MODEL:(t + 1) * D_MODEL]
            o_ref[l, 3:4, :] = g_ref[l:l + 1, :]

    return pl.pallas_call(
        body, name="vecs_build", out_shape=jax.ShapeDtypeStruct((DEPTH, 8, D_MODEL), F32),
    )(mod_all, b_ada, norm_g)


def ada_update(dmods, c_t, w, m, v, tr=256):
    L, D, W = w.shape

    def body(dm_ref, c_ref, w_ref, m_ref, v_ref, g_out, d_out, m_out, v_out):
        mx, my, _ = _place()
        shard = 2 * mx + my
        dm = jnp.zeros((8, W), F32)
        for s in range(N_SHARD):
            dm = dm + jnp.where(shard == s, dm_ref[0, :, s * W:(s + 1) * W], 0.0)
        cv = c_ref[...]
        ca = cv * _sigmoid(cv)
        g = jnp.zeros((tr, W), F32)
        for b in range(8):
            g = g + ca[:, b:b + 1] * dm[b:b + 1, :]
        d, m2, v2 = _adamw(w_ref[0], g, m_ref[0], v_ref[0])
        g_out[0], d_out[0], m_out[0], v_out[0] = g, d, m2, v2

    blk = pl.BlockSpec((1, tr, W), lambda l, i: (l, i, 0))
    return pl.pallas_call(
        body, name="ada_update", grid=(L, D // tr),
        in_specs=[pl.BlockSpec((1, 8, 3 * D), lambda l, i: (l, 0, 0)), pl.BlockSpec((tr, 8), lambda l, i: (i, 0)),
                  blk, blk, blk],
        out_specs=[blk] * 4, out_shape=[jax.ShapeDtypeStruct(w.shape, F32)] * 4,
        compiler_params=_cp("arbitrary", "arbitrary"),
    )(dmods, c_t, w, m, v)


STAT_ROWS = 16


def small_update(stats_all, norm, b_ada, final):
    def body(s_ref, *refs):
        ins, outs = refs[:9], refs[9:]
        tot = s_ref[0]
        for k in range(1, 8):
            tot = tot + s_ref[k]
        g_norm = tot[0:2, :]
        g_final = tot[2:3, :]
        g_b = jnp.concatenate(
            [jnp.concatenate([tot[3 + 3 * l + t:4 + 3 * l + t, :] for t in range(3)], axis=1) for l in range(DEPTH)],
            axis=0)
        for p, g in enumerate((g_norm, g_b, g_final)):
            w_ref, m_ref, v_ref = ins[3 * p:3 * p + 3]
            d, m2, v2 = _adamw(w_ref[...], g, m_ref[...], v_ref[...])
            for o_ref, val in zip(outs[4 * p:4 * p + 4], (g, d, m2, v2)):
                o_ref[...] = val
        loss = (0.5 / D_MODEL) * jnp.sum(tot[9:10, :], axis=1, keepdims=True)
        outs[12][...] = jnp.broadcast_to(loss, (8, LANES))

    shapes = []
    for w, _, _ in (norm, b_ada, final):
        shapes += [jax.ShapeDtypeStruct(w.shape, F32)] * 4
    shapes.append(jax.ShapeDtypeStruct((8, LANES), F32))
    return pl.pallas_call(body, name="small_update", out_shape=shapes)(stats_all, *norm, *b_ada, *final)


def kernel(x, c, norm_g, w_ada, b_ada, w_in, w_out, final_g, loss_target, m_norm_g, m_w_ada, m_b_ada, m_w_in, m_w_out, m_final_g, v_norm_g, v_w_ada, v_b_ada, v_w_in, v_w_out, v_final_g):
    S, D = x.shape[1], x.shape[2]
    mc = lax.axis_index("c")
    out_rows = D // N_SHARD

    def my_half(a, rows):
        return lax.dynamic_slice_in_dim(a, mc * rows, rows, axis=0)

    wblk = [jnp.concatenate([my_half(w_in[l], D // 2), my_half(w_out[l], out_rows // 2)], axis=0).astype(BF16)
            for l in range(DEPTH)]

    def unpack(wall):
        wall = wall.reshape(N_SHARD, 2, wall.shape[1], SHARD_W)
        return wall[:, :, :D // 2].reshape(N_SHARD, D, SHARD_W), wall[:, :, D // 2:].reshape(D, D)

    weights = [unpack(allgather8(wblk[0], "gather_weights"))]

    c_all = allgather8(jnp.broadcast_to(c, (8, D)), "gather_c")[:, 0, :]
    mod_all = allgather8(ada_fwd(c_all, w_ada).reshape(DEPTH * 8, -1), "gather_mod")
    vecs = vecs_build(mod_all, b_ada, norm_g)

    tabs = (*rope_tables(S), ret_log_gamma())
    h = x[0]
    saved = []
    for l in range(DEPTH):
        h, sv, wall = layer_fwd(h, vecs[l], *weights[l], tabs, wblk[l + 1] if l + 1 < DEPTH else None)
        saved.append(sv)
        if wall is not None:
            weights.append(unpack(wall))
    dx, st_loss = loss_head(h, jnp.broadcast_to(final_g[None, :], (8, D)), loss_target[0])

    dmod, dnorm, grads = [None] * DEPTH, [None] * DEPTH, None
    for l in reversed(range(DEPTH)):
        dx, dmod[l], dnorm[l], grads = layer_bwd(dx, saved[l], vecs[l], *weights[l], tabs, grads)

    p_own = sum_slots(*grads)
    p_sib = swap_sibling(p_own)
    res_in = adam_slab(p_own, p_sib, w_in, m_w_in, v_w_in, 0, "adam_w_in")
    res_out = adam_slab(p_own, p_sib, w_out, m_w_out, v_w_out, DEPTH * D, "adam_w_out", tr=128)

    stats = jnp.concatenate(dnorm + [st_loss[0:1]] + dmod + [st_loss[1:2], jnp.zeros((STAT_ROWS - 10, D), F32)], axis=0)
    stats_all = allgather8(stats, "gather_stats")
    dmods = stats_all[:, 3:9, :].reshape(8, DEPTH, 3 * D).transpose(1, 0, 2)
    res_ada = ada_update(dmods, c_all.T, w_ada, m_w_ada, v_w_ada)
    small = small_update(stats_all, (norm_g, m_norm_g, v_norm_g), (b_ada, m_b_ada, v_b_ada),
                         (final_g[None, :], m_final_g[None, :], v_final_g[None, :]))
    res_norm, res_b, res_final = small[0:4], small[4:8], [a[0] for a in small[8:12]]
    loss = small[12][0, 0]

    by_kind = [res_norm, res_ada, res_b, res_in, res_out, res_final]
    outs = [loss, dx[None]]
    for kind in range(4):
        outs += [r[kind] for r in by_kind]
    return tuple(outs)
```

```python
import functools

import numpy as np
import jax
import jax.numpy as jnp
from jax import lax
from jax.experimental import pallas as pl
from jax.experimental.pallas import tpu as pltpu

F32, BF16 = jnp.float32, jnp.bfloat16
MESH = pl.DeviceIdType.MESH

D_MODEL = 1024
DEPTH = 2
SHARD_W = 1024
N_SHARD = 4
GROUP_W = 512
LANES = 128
SB_HEAD_DIM = 64
RET_HEAD_DIM = 128
CHUNK = 64
ROPE_BASE = 10000.0
EPS = 1e-6
SQ_SCALE = SB_HEAD_DIM ** -0.5
RK_SCALE = RET_HEAD_DIM ** -0.5
SB_T = 256
RET_T = 256
EXP_ZERO = -104.0
VMEM_LIMIT_BYTES = 56 * 2 ** 20

ADAM_LR, ADAM_B1, ADAM_B2, ADAM_EPS, ADAM_WD, ADAM_STEP = 0.001, 0.9, 0.999, 1e-08, 0.01, 10


def _cp(*sem):
    return pltpu.CompilerParams(dimension_semantics=sem, vmem_limit_bytes=VMEM_LIMIT_BYTES)


def _dot(a, b):
    return lax.dot_general(a, b, (((1,), (0,)), ((), ())), preferred_element_type=F32)


def _dot_nt(a, b):
    return lax.dot_general(a, b, (((1,), (1,)), ((), ())), preferred_element_type=F32)


def _dot_tn(a, b):
    return lax.dot_general(a, b, (((0,), (0,)), ((), ())), preferred_element_type=F32)


def _running_sum(a, tri):
    return _dot(a.astype(BF16), tri)


def _sigmoid(x):
    return 1.0 / (1.0 + jnp.exp(-x))


def _rowsum(a):
    return jnp.sum(a, axis=1, keepdims=True)


def _rowmean(a):
    return jnp.mean(a, axis=1, keepdims=True)


def inproj_fwd(x, vecs, w3, tm=256):
    S, D = x.shape

    def body(x_ref, v_ref, w_ref, proj_ref, h_ref, sb_ref):
        xv = x_ref[...]
        r = lax.rsqrt(_rowmean(xv * xv) + EPS)
        h = xv * r * v_ref[3:4, :] * (1.0 + v_ref[1:2, :]) + v_ref[0:1, :]
        hb = h.astype(BF16)
        h_ref[...] = hb
        for s in range(N_SHARD):
            p = _dot(hb, w_ref[s])
            proj_ref[:, s * SHARD_W:(s + 1) * SHARD_W] = p
            if s == 2:
                sb_ref[:, 0:GROUP_W] = (p[:, 0:GROUP_W] * SQ_SCALE).astype(BF16)
                sb_ref[:, GROUP_W:SHARD_W] = p[:, GROUP_W:].astype(BF16)
            if s == 3:
                sb_ref[:, SHARD_W:2 * SHARD_W] = p.astype(BF16)

    return pl.pallas_call(
        body, name="inproj_fwd", grid=(S // tm,),
        in_specs=[pl.BlockSpec((tm, D), lambda i: (i, 0)),
                  pl.BlockSpec((8, D), lambda i: (0, 0)),
                  pl.BlockSpec((N_SHARD, D, SHARD_W), lambda i: (0, 0, 0))],
        out_specs=[pl.BlockSpec((tm, 4 * D), lambda i: (i, 0)),
                   pl.BlockSpec((tm, D), lambda i: (i, 0)),
                   pl.BlockSpec((tm, 2 * SHARD_W), lambda i: (i, 0))],
        out_shape=[jax.ShapeDtypeStruct((S, 4 * D), F32),
                   jax.ShapeDtypeStruct((S, D), BF16),
                   jax.ShapeDtypeStruct((S, 2 * SHARD_W), BF16)],
        compiler_params=_cp("arbitrary"),
    )(x, vecs, w3)


def _sb_logits(qh, k2, mask):
    z = _dot_nt(qh, k2)
    sp = jnp.log(1.0 + jnp.exp(-jnp.abs(z)))
    lb = jnp.minimum(z, 0.0) - sp
    lk = lb - z
    if mask is not None:
        lk = jnp.where(mask, lk, 0.0)
    return lb, lk


def _sb_masks(i):
    T = SB_T
    first = jnp.maximum(i - 1, 0)
    row = lax.broadcasted_iota(jnp.int32, (2 * T, 2 * T), 0)
    col = lax.broadcasted_iota(jnp.int32, (2 * T, 2 * T), 1)
    causal = first * T + col < i * T + (row & (T - 1))
    r = lax.broadcasted_iota(jnp.int32, (T, T), 0)
    c = lax.broadcasted_iota(jnp.int32, (T, T), 1)
    later = jnp.where(r > c, 1.0, 0.0).astype(BF16)
    earlier = jnp.where(r < c, 1.0, 0.0).astype(BF16)
    lane = lax.broadcasted_iota(jnp.int32, (1, LANES), 1)
    return first, causal, later, earlier, lane < SB_HEAD_DIM


def _sb_rows(ref, j, tiles=1):
    return ref[pl.ds(pl.multiple_of(j * SB_T, SB_T), tiles * SB_T), :]


def sb_fwd(sb, proj, gather=None):
    S = sb.shape[0]
    T = SB_T
    nq = S // T
    carried = [] if gather is None else [gather]

    def body(*refs):
        (q_ref, k_ref, v_ref, sg_ref), refs = refs[:4], refs[4:]
        p, i = pl.program_id(0), pl.program_id(1)
        if carried:
            x_ref, y_ref, o_ref, end_ref, out_ref, send_sems, recv_sems, local_sem = refs
            start, forward, finish = _gather_ops(x_ref, out_ref, send_sems, recv_sems, local_sem)
            pl.when(jnp.logical_and(p == 0, i == 0))(start)
            pl.when(jnp.logical_and(p == 2, i == 0))(forward)
        else:
            y_ref, o_ref, end_ref = refs
        first, causal, later, _, head0 = _sb_masks(i)
        q2 = q_ref[...]
        zero = jnp.zeros_like(q2)
        qs = jnp.concatenate([jnp.where(head0, q2, zero), jnp.where(head0, zero, q2)], axis=0)

        lb, lk = _sb_logits(qs, _sb_rows(k_ref, first, 2), causal)
        rs_left, rs_right = _rowsum(lk[:, :T]), _rowsum(lk[:, T:])
        suffix = jnp.concatenate([_running_sum(lk[:, :T], later) + rs_right, _running_sum(lk[:, T:], later)], axis=1)
        a = jnp.where(causal, jnp.exp(lb + suffix), 0.0)
        acc = _dot(a.astype(BF16), _sb_rows(v_ref, first, 2))
        R = rs_left + rs_right

        def tile(j, R):
            lb, lk = _sb_logits(qs, _sb_rows(k_ref, j), None)
            a = jnp.exp(lb + _running_sum(lk, later) + R)
            return _dot(a.astype(BF16), _sb_rows(v_ref, j)), R + _rowsum(lk)

        def cond(st):
            return jnp.logical_and(st[0] >= 0, st[3] > EXP_ZERO)

        def step(st):
            c, Rn = tile(st[0], st[2])
            return st[0] - 1, st[1] + c, Rn, jnp.max(Rn)

        j_end, acc, R, _ = lax.while_loop(cond, step, (first - 1, acc, R, jnp.max(R)))
        end_ref[0, 0, 0:2 * T, :] = jnp.broadcast_to(R, (2 * T, LANES))
        end_ref[0, 0, 2 * T:, :] = jnp.full((8, LANES), j_end.astype(F32))
        o = jnp.where(head0, acc[:T], acc[T:])
        o_ref[...] = o
        sg = sg_ref[...]
        y_ref[...] = (o * (sg * _sigmoid(sg))).astype(BF16)
        if carried:
            pl.when(jnp.logical_and(p == 3, i == nq - 1))(finish)

    return pl.pallas_call(
        body, name="sb_fwd", grid=(4, nq),
        in_specs=[pl.BlockSpec((T, LANES), lambda p, i: (i, p)),
                  pl.BlockSpec((S, LANES), lambda p, i: (0, 4 + p)),
                  pl.BlockSpec((S, LANES), lambda p, i: (0, 8 + p)),
                  pl.BlockSpec((T, LANES), lambda p, i: (i, 28 + p))] + [_ANY for _ in carried],
        out_specs=[pl.BlockSpec((T, LANES), lambda p, i: (i, p)),
                   pl.BlockSpec((T, LANES), lambda p, i: (i, p)),
                   pl.BlockSpec((1, 1, 2 * T + 8, LANES), lambda p, i: (p, i, 0, 0))] + [_ANY for _ in carried],
        out_shape=[jax.ShapeDtypeStruct((S, GROUP_W), BF16),
                   jax.ShapeDtypeStruct((S, GROUP_W), F32),
                   jax.ShapeDtypeStruct((4, nq, 2 * T + 8, LANES), F32)]
        + [jax.ShapeDtypeStruct((8,) + a.shape, a.dtype) for a in carried],
        scratch_shapes=_GATHER_SCRATCH if carried else [],
        compiler_params=_cp("arbitrary", "arbitrary"),
    )(sb, sb, sb, proj, *carried)


def sb_bwd(sb, proj, o, sb_end, dycat, ship=None):
    S = sb.shape[0]
    T = SB_T
    nq = S // T
    ex = _Exchange(ship)

    def body(*refs):
        (q_ref, k_ref, v_ref, sg_ref, o_ref, dy_ref, end_ref), refs = refs[:7], refs[7:]
        ship_refs, (dq_ref, dk_ref, dv_ref, dsg_ref), refs = refs[:ex.n_in], refs[ex.n_in:ex.n_in + 4], refs[ex.n_in + 4:]
        recv, (dk_acc, dv_acc), sems = refs[:ex.n_out], refs[ex.n_out:ex.n_out + 2], refs[ex.n_out + 2:]
        start, finish = ex.ops(ship_refs, recv + sems)
        p, i = pl.program_id(0), pl.program_id(1)
        pl.when(jnp.logical_and(p == 0, i == 0))(start)

        @pl.when(i == 0)
        def _():
            dk_acc[...] = jnp.zeros_like(dk_acc)
            dv_acc[...] = jnp.zeros_like(dv_acc)

        first, causal, later, earlier, head0 = _sb_masks(i)
        q2 = q_ref[...]
        sg = sg_ref[...]
        sig = _sigmoid(sg)
        dy = dy_ref[...]
        dsg_ref[...] = (dy * o_ref[...] * (sig * (1.0 + sg * (1.0 - sig)))).astype(BF16)
        do_b = (dy * (sg * sig)).astype(BF16)
        zero = jnp.zeros_like(q2)
        qs = jnp.concatenate([jnp.where(head0, q2, zero), jnp.where(head0, zero, q2)], axis=0)
        dos = jnp.concatenate([jnp.where(head0, do_b, zero), jnp.where(head0, zero, do_b)], axis=0)

        end = end_ref[0, 0]
        j_end = jnp.max(end[2 * T:, :]).astype(jnp.int32)

        def grads(j, tiles, a, lb, g, G, mask):
            dz = g - jnp.exp(lb) * (g + G)
            if mask is not None:
                dz = jnp.where(mask, dz, 0.0)
            dzb = dz.astype(BF16)
            rows = pl.ds(pl.multiple_of(j * T, T), tiles * T)
            dk_acc[rows, :] += _dot_tn(dzb, qs)
            dv_acc[rows, :] += _dot_tn(a.astype(BF16), dos)
            return _dot(dzb, _sb_rows(k_ref, j, tiles))

        def sweep(j, st):
            dq, G0, left = st
            lb, lk = _sb_logits(qs, _sb_rows(k_ref, j), None)
            stick = left - _rowsum(lk)
            a = jnp.exp(lb + _running_sum(lk, later) + stick)
            g = a * _dot_nt(dos, _sb_rows(v_ref, j))
            G = _running_sum(g, earlier) + G0
            return dq + grads(j, 1, a, lb, g, G, None), G0 + _rowsum(g), stick

        dq, G0, _ = lax.fori_loop(j_end + 1, first, sweep,
                                  (jnp.zeros((2 * T, LANES), F32), jnp.zeros((2 * T, 1), F32), end[:2 * T, 0:1]))

        lb, lk = _sb_logits(qs, _sb_rows(k_ref, first, 2), causal)
        suffix = jnp.concatenate([_running_sum(lk[:, :T], later) + _rowsum(lk[:, T:]),
                                  _running_sum(lk[:, T:], later)], axis=1)
        a = jnp.where(causal, jnp.exp(lb + suffix), 0.0)
        g = a * _dot_nt(dos, _sb_rows(v_ref, first, 2))
        G = jnp.concatenate([_running_sum(g[:, :T], earlier) + G0,
                             _running_sum(g[:, T:], earlier) + (G0 + _rowsum(g[:, :T]))], axis=1)
        dq = dq + grads(first, 2, a, lb, g, G, causal)
        dq_ref[...] = (jnp.where(head0, dq[:T], dq[T:]) * SQ_SCALE).astype(BF16)

        @pl.when(i == nq - 1)
        def _():
            dk_ref[...] = dk_acc[...].astype(BF16)
            dv_ref[...] = dv_acc[...].astype(BF16)

        pl.when(jnp.logical_and(p == 3, i == nq - 1))(finish)

    tile_spec = lambda c0: pl.BlockSpec((T, LANES), lambda p, i: (i, c0 + p))
    head_spec = lambda c0: pl.BlockSpec((S, LANES), lambda p, i: (0, c0 + p))
    return pl.pallas_call(
        body, name="sb_bwd", grid=(4, nq),
        in_specs=[tile_spec(0), head_spec(4), head_spec(8), tile_spec(28), tile_spec(0), tile_spec(4),
                  pl.BlockSpec((1, 1, 2 * T + 8, LANES), lambda p, i: (p, i, 0, 0))] + ex.in_specs,
        out_specs=[tile_spec(0), head_spec(0), head_spec(0), tile_spec(0)] + ex.out_specs,
        out_shape=[jax.ShapeDtypeStruct((S, GROUP_W), BF16)] * 4 + ex.out_shape,
        scratch_shapes=[pltpu.VMEM((S, LANES), F32), pltpu.VMEM((S, LANES), F32)] + ex.scratch,
        compiler_params=_cp("arbitrary", "arbitrary"),
    )(sb, sb, sb, proj, o, dycat, sb_end, *ex.ship)


def rope_tables(S):
    half = RET_HEAD_DIM // 2
    inv = ROPE_BASE ** (-jnp.arange(half, dtype=F32) / half)
    ang = jnp.arange(S, dtype=F32)[:, None] * inv[None, :]
    cos, sin = jnp.cos(ang), jnp.sin(ang)
    return jnp.concatenate([cos, cos], axis=1), jnp.concatenate([-sin, sin], axis=1)


def ret_log_gamma():
    return jnp.log1p(-(2.0 ** (-5.0 - jnp.arange(4, dtype=F32))))


def _swap_halves(a):
    return pltpu.roll(a, RET_HEAD_DIM // 2, axis=1)


def _ret_decay_mask(lg):
    n = lax.broadcasted_iota(jnp.int32, (RET_T, RET_T), 0)
    m = lax.broadcasted_iota(jnp.int32, (RET_T, RET_T), 1)
    dist = jnp.abs(n - m).astype(F32)
    return jnp.where((m // CHUNK) <= (n // CHUNK), jnp.exp(lg * dist), 0.0)


def _ret_block(lg, rq, rk, rv, cosf, sinf, dm):
    q = rq * cosf + _swap_halves(rq) * sinf
    k = (rk * cosf + _swap_halves(rk) * sinf) * RK_SCALE
    qb, kb, vb = q.astype(BF16), k.astype(BF16), rv.astype(BF16)
    sc = _dot_nt(qb, kb) * dm
    nloc = lax.broadcasted_iota(jnp.int32, (RET_T, 1), 0).astype(F32)
    qdec = jnp.exp(lg * (nloc + 1.0))
    kdec = jnp.exp(lg * (RET_T - 1.0 - nloc))
    block_dec = jnp.exp(jnp.full((1, LANES), lg * RET_T, F32))
    return q, k, qb, kb, vb, sc, qdec, kdec, block_dec


def _ret_specs(S, rb):
    group = lambda c0: pl.BlockSpec((RET_T, GROUP_W), lambda b: (rb(b), c0))
    return group, pl.BlockSpec((RET_T, LANES), lambda b: (rb(b), 0))


def _head(ref, h):
    return ref[:, h * LANES:(h + 1) * LANES]


def ret_fwd(proj, cosf, sinf, lgam):
    S = proj.shape[0]
    nb = S // RET_T
    group, row_tab = _ret_specs(S, lambda b: b)

    def body(lg_ref, rq_ref, rk_ref, rv_ref, rg_ref, cos_ref, sin_ref, y_ref, o_ref, st_out, st_ref, dm_ref):
        @pl.when(pl.program_id(0) == 0)
        def _():
            st_ref[...] = jnp.zeros_like(st_ref)
            for h in range(4):
                dm_ref[h] = _ret_decay_mask(lg_ref[h])

        cosf, sinf = cos_ref[...], sin_ref[...]
        H = range(4)
        lanes = [slice(h * LANES, (h + 1) * LANES) for h in H]
        blk = [_ret_block(lg_ref[h], _head(rq_ref, h), _head(rk_ref, h), _head(rv_ref, h), cosf, sinf, dm_ref[h])
               for h in H]
        q, k, qb, kb, vb, sc, qdec, kdec, block_dec = zip(*blk)
        st = [st_ref[h] for h in H]
        for h in H:
            st_out[h, 0] = st[h]
        o = [_dot(sc[h].astype(BF16), vb[h]) + _dot(qb[h], st[h].astype(BF16)) * qdec[h] for h in H]
        for h in H:
            st_ref[h] = st[h] * block_dec[h] + _dot_tn((k[h] * kdec[h]).astype(BF16), vb[h])
        for h in H:
            o_ref[:, lanes[h]] = o[h]
        cen = [o[h] - _rowmean(o[h]) for h in H]
        on = [cen[h] * lax.rsqrt(_rowmean(cen[h] * cen[h]) + EPS) for h in H]
        rg = [_head(rg_ref, h) for h in H]
        for h in H:
            y_ref[:, lanes[h]] = (on[h] * (rg[h] * _sigmoid(rg[h]))).astype(BF16)

    return pl.pallas_call(
        body, name="ret_fwd", grid=(nb,),
        in_specs=[pl.BlockSpec(memory_space=pltpu.SMEM),
                  group(0), group(1), group(2), group(3), row_tab, row_tab],
        out_specs=[group(0), group(0),
                   pl.BlockSpec((4, 1, LANES, LANES), lambda b: (0, b, 0, 0))],
        out_shape=[jax.ShapeDtypeStruct((S, GROUP_W), BF16),
                   jax.ShapeDtypeStruct((S, GROUP_W), F32),
                   jax.ShapeDtypeStruct((4, nb, LANES, LANES), F32)],
        scratch_shapes=[pltpu.VMEM((4, LANES, LANES), F32), pltpu.VMEM((4, RET_T, RET_T), F32)],
        compiler_params=_cp("arbitrary"),
    )(lgam, proj, proj, proj, proj, cosf, sinf)


def ret_bwd(proj, cosf, sinf, lgam, o, states, dycat):
    S = proj.shape[0]
    nb = S // RET_T
    rev = lambda b: nb - 1 - b
    group, row_tab = _ret_specs(S, rev)

    def body(lg_ref, rq_ref, rk_ref, rv_ref, rg_ref, cos_ref, sin_ref, o_ref, st_in, dy_ref,
             drq_ref, drk_ref, drv_ref, drg_ref, ds_ref, dm_ref):
        @pl.when(pl.program_id(0) == 0)
        def _():
            ds_ref[...] = jnp.zeros_like(ds_ref)
            for h in range(4):
                dm_ref[h] = _ret_decay_mask(lg_ref[h])

        cosf, sinf = cos_ref[...], sin_ref[...]
        H = range(4)
        lanes = [slice(h * LANES, (h + 1) * LANES) for h in H]
        dms = [dm_ref[h] for h in H]
        blk = [_ret_block(lg_ref[h], _head(rq_ref, h), _head(rk_ref, h), _head(rv_ref, h), cosf, sinf, dms[h])
               for h in H]
        q, k, qb, kb, vb, sc, qdec, kdec, block_dec = zip(*blk)
        o_v = [_head(o_ref, h) for h in H]
        cen = [o_v[h] - _rowmean(o_v[h]) for h in H]
        rstd = [lax.rsqrt(_rowmean(cen[h] * cen[h]) + EPS) for h in H]
        on = [cen[h] * rstd[h] for h in H]
        rg = [_head(rg_ref, h) for h in H]
        sig = [_sigmoid(rg[h]) for h in H]
        dy = [_head(dy_ref, h) for h in H]
        for h in H:
            drg_ref[:, lanes[h]] = (dy[h] * on[h] * (sig[h] * (1.0 + rg[h] * (1.0 - sig[h])))).astype(BF16)
        don = [dy[h] * (rg[h] * sig[h]) for h in H]
        do = [rstd[h] * (don[h] - _rowmean(don[h]) - on[h] * _rowmean(don[h] * on[h])) for h in H]
        dob = [do[h].astype(BF16) for h in H]
        dsc = [(_dot_nt(dob[h], vb[h]) * dms[h]).astype(BF16) for h in H]
        st_b = [st_in[h, 0].astype(BF16) for h in H]
        dsn = [ds_ref[h] for h in H]
        dsn_b = [dsn[h].astype(BF16) for h in H]
        dq = [_dot(dsc[h], kb[h]) + _dot_nt(dob[h], st_b[h]) * qdec[h] for h in H]
        dk = [(_dot_tn(dsc[h], qb[h]) + _dot_nt(vb[h], dsn_b[h]) * kdec[h]) * RK_SCALE for h in H]
        dv = [_dot_tn(sc[h].astype(BF16), dob[h]) + _dot((k[h] * kdec[h]).astype(BF16), dsn_b[h]) for h in H]
        for h in H:
            ds_ref[h] = dsn[h] * block_dec[h] + _dot_tn((q[h] * qdec[h]).astype(BF16), dob[h])
        for h in H:
            drq_ref[:, lanes[h]] = (dq[h] * cosf + _swap_halves(dq[h] * sinf)).astype(BF16)
            drk_ref[:, lanes[h]] = (dk[h] * cosf + _swap_halves(dk[h] * sinf)).astype(BF16)
            drv_ref[:, lanes[h]] = dv[h].astype(BF16)

    return pl.pallas_call(
        body, name="ret_bwd", grid=(nb,),
        in_specs=[pl.BlockSpec(memory_space=pltpu.SMEM),
                  group(0), group(1), group(2), group(3), row_tab, row_tab,
                  group(0), pl.BlockSpec((4, 1, LANES, LANES), lambda b: (0, rev(b), 0, 0)),
                  group(0)],
        out_specs=[group(0)] * 4,
        out_shape=[jax.ShapeDtypeStruct((S, GROUP_W), BF16)] * 4,
        scratch_shapes=[pltpu.VMEM((4, LANES, LANES), F32), pltpu.VMEM((4, RET_T, RET_T), F32)],
        compiler_params=_cp("arbitrary"),
    )(lgam, proj, proj, proj, proj, cosf, sinf, o, states, dycat)


def outproj_fwd(x, vecs, y_ret, y_sb, w_out, tm=512):
    S, D = x.shape

    def body(x_ref, v_ref, yr_ref, ys_ref, w_ref, y_ref, xo_ref):
        y = _dot(yr_ref[...], w_ref[0:GROUP_W, :]) + _dot(ys_ref[...], w_ref[GROUP_W:, :])
        y_ref[...] = y
        xo_ref[...] = x_ref[...] + v_ref[2:3, :] * y

    row = lambda w: pl.BlockSpec((tm, w), lambda i: (i, 0))
    return pl.pallas_call(
        body, name="outproj_fwd", grid=(S // tm,),
        in_specs=[row(D), pl.BlockSpec((8, D), lambda i: (0, 0)), row(GROUP_W), row(GROUP_W),
                  pl.BlockSpec((D, D), lambda i: (0, 0))],
        out_specs=[row(D), row(D)],
        out_shape=[jax.ShapeDtypeStruct((S, D), F32)] * 2,
        compiler_params=_cp("arbitrary"),
    )(x, vecs, y_ret, y_sb, w_out)


def loss_head(x, final_g, target, tm=512):
    S, D = x.shape

    def body(x_ref, g_ref, t_ref, dx_ref, st_ref):
        @pl.when(pl.program_id(0) == 0)
        def _():
            st_ref[...] = jnp.zeros_like(st_ref)

        xv = x_ref[...]
        g = g_ref[0:1, :]
        r = lax.rsqrt(_rowmean(xv * xv) + EPS)
        xn = xv * r
        err = xn * g - t_ref[...]
        dy = err * (1.0 / D)
        dxn = dy * g
        dx_ref[...] = r * (dxn - xn * _rowmean(dxn * xn))
        st_ref[0:1, :] += jnp.sum(dy * xn, axis=0, keepdims=True)
        st_ref[1:2, :] += jnp.sum(err * err, axis=0, keepdims=True)

    row = pl.BlockSpec((tm, D), lambda i: (i, 0))
    fixed = pl.BlockSpec((8, D), lambda i: (0, 0))
    return pl.pallas_call(
        body, name="loss_head", grid=(S // tm,),
        in_specs=[row, fixed, row], out_specs=[row, fixed],
        out_shape=[jax.ShapeDtypeStruct((S, D), F32), jax.ShapeDtypeStruct((8, D), F32)],
        compiler_params=_cp("arbitrary"),
    )(x, final_g, target)


def outproj_bwd(dx, y, vecs, y_ret, y_sb, w_out, tm=512):
    S, D = dx.shape
    n = S // tm

    def body(dx_ref, y_ref, v_ref, yr_ref, ys_ref, w_ref, dyc_ref, dw_ref, st_ref, acc):
        i = pl.program_id(0)

        @pl.when(i == 0)
        def _():
            st_ref[...] = jnp.zeros_like(st_ref)
            acc[...] = jnp.zeros_like(acc)

        dxv = dx_ref[...]
        st_ref[0:1, :] += jnp.sum(dxv * y_ref[...], axis=0, keepdims=True)
        dyy = (dxv * v_ref[2:3, :]).astype(BF16)
        dyc_ref[...] = _dot_nt(dyy, w_ref[...])
        acc[0:GROUP_W, :] += _dot_tn(yr_ref[...], dyy)
        acc[GROUP_W:, :] += _dot_tn(ys_ref[...], dyy)

        @pl.when(i == n - 1)
        def _():
            dw_ref[...] = acc[...].astype(BF16)

    row = lambda w: pl.BlockSpec((tm, w), lambda i: (i, 0))
    fixed = lambda r: pl.BlockSpec((r, D), lambda i: (0, 0))
    return pl.pallas_call(
        body, name="outproj_bwd", grid=(n,),
        in_specs=[row(D), row(D), fixed(8), row(GROUP_W), row(GROUP_W), fixed(D)],
        out_specs=[row(D), fixed(D), fixed(8)],
        out_shape=[jax.ShapeDtypeStruct((S, D), F32), jax.ShapeDtypeStruct((D, D), BF16),
                   jax.ShapeDtypeStruct((8, D), F32)],
        scratch_shapes=[pltpu.VMEM((D, D), F32)],
        compiler_params=_cp("arbitrary"),
    )(dx, y, vecs, y_ret, y_sb, w_out)


def inproj_bwd_x(pieces, w3, x, vecs, dx_res, ship=None, tm=256):
    S, D = x.shape
    n = S // tm
    ex = _Exchange(ship)

    def body(*refs):
        p_refs, (w_ref, x_ref, v_ref, dr_ref), refs = refs[:8], refs[8:12], refs[12:]
        ship_refs, (dx_ref, st_ref), refs = refs[:ex.n_in], refs[ex.n_in:ex.n_in + 2], refs[ex.n_in + 2:]
        start, finish = ex.ops(ship_refs, refs)

        @pl.when(pl.program_id(0) == 0)
        def _():
            st_ref[...] = jnp.zeros_like(st_ref)
            start()

        dh = jnp.zeros((tm, D), F32)
        for k, p_ref in enumerate(p_refs):
            c0 = (k % 2) * GROUP_W
            dh = dh + _dot_nt(p_ref[...], w_ref[k // 2, :, c0:c0 + GROUP_W])
        xv = x_ref[...]
        r = lax.rsqrt(_rowmean(xv * xv) + EPS)
        xn = xv * r
        g, scale1 = v_ref[3:4, :], 1.0 + v_ref[1:2, :]
        st_ref[0:1, :] += jnp.sum(dh, axis=0, keepdims=True)
        dh_xn = dh * xn
        st_ref[1:2, :] += jnp.sum(dh_xn, axis=0, keepdims=True) * g
        st_ref[2:3, :] += jnp.sum(dh_xn, axis=0, keepdims=True) * scale1
        dxn = dh * (g * scale1)
        dx_ref[...] = r * (dxn - xn * _rowmean(dxn * xn)) + dr_ref[...]
        pl.when(pl.program_id(0) == n - 1)(finish)

    row = lambda w: pl.BlockSpec((tm, w), lambda i: (i, 0))
    return pl.pallas_call(
        body, name="inproj_bwd_x", grid=(n,),
        in_specs=[row(GROUP_W)] * 8 + [pl.BlockSpec((N_SHARD, D, SHARD_W), lambda i: (0, 0, 0)),
                                       row(D), pl.BlockSpec((8, D), lambda i: (0, 0)), row(D)] + ex.in_specs,
        out_specs=[row(D), pl.BlockSpec((8, D), lambda i: (0, 0))] + ex.out_specs,
        out_shape=[jax.ShapeDtypeStruct((S, D), F32), jax.ShapeDtypeStruct((8, D), F32)] + ex.out_shape,
        scratch_shapes=ex.scratch,
        compiler_params=_cp("arbitrary"),
    )(*pieces, w3, x, vecs, dx_res, *ex.ship)


def inproj_bwd_w(h, pieces, tm=512):
    S, D = h.shape
    n = S // tm

    def body(*refs):
        h_ref, p_refs, dw_ref, acc = refs[0], refs[1:9], refs[9], refs[10]
        i = pl.program_id(0)

        @pl.when(i == 0)
        def _():
            acc[...] = jnp.zeros_like(acc)

        hv = h_ref[...]
        for k, p_ref in enumerate(p_refs):
            c0 = (k % 2) * GROUP_W
            acc[k // 2, :, c0:c0 + GROUP_W] += _dot_tn(hv, p_ref[...])

        @pl.when(i == n - 1)
        def _():
            dw_ref[...] = acc[...].astype(BF16)

    row = lambda w: pl.BlockSpec((tm, w), lambda i: (i, 0))
    return pl.pallas_call(
        body, name="inproj_bwd_w", grid=(n,),
        in_specs=[row(D)] + [row(GROUP_W)] * 8,
        out_specs=pl.BlockSpec((N_SHARD, D, SHARD_W), lambda i: (0, 0, 0)),
        out_shape=jax.ShapeDtypeStruct((N_SHARD, D, SHARD_W), BF16),
        scratch_shapes=[pltpu.VMEM((N_SHARD, D, SHARD_W), F32)],
        compiler_params=_cp("arbitrary"),
    )(h, *pieces)


def layer_fwd(x, vecs, w3, w_out, tabs, gather=None):
    cosf, sinf, lgam = tabs
    proj, h, sb = inproj_fwd(x, vecs, w3)
    y_ret, o_ret, states = ret_fwd(proj, cosf, sinf, lgam)
    y_sb, o_sb, sb_end, *gathered = sb_fwd(sb, proj, gather)
    y, x_next = outproj_fwd(x, vecs, y_ret, y_sb, w_out)
    saved = (x, proj, h, sb, y_ret, o_ret, states, y_sb, o_sb, sb_end, y)
    return x_next, saved, (gathered[0] if gathered else None)


def _by_shard(dw_out):
    return dw_out.reshape(N_SHARD, D_MODEL // N_SHARD, D_MODEL)


def layer_bwd(dx, saved, vecs, w3, w_out, tabs, later_grads=None):
    cosf, sinf, lgam = tabs
    x, proj, h, sb, y_ret, o_ret, states, y_sb, o_sb, sb_end, y = saved
    dycat, dw_out, st_o = outproj_bwd(dx, y, vecs, y_ret, y_sb, w_out)
    dw_out = _by_shard(dw_out)
    ship = None if later_grads is None else (later_grads[0], dw_out, later_grads[1])
    *d_sb, = sb_bwd(sb, proj, o_sb, sb_end, dycat, ship)
    d_ret = ret_bwd(proj, cosf, sinf, lgam, o_ret, states, dycat)
    pieces = list(d_ret) + d_sb[:4]
    dw_in = inproj_bwd_w(h, pieces)
    dx, st_i, *recv_in = inproj_bwd_x(pieces, w3, x, vecs, dx, None if later_grads is None else (dw_in,))
    dmod = jnp.concatenate([st_i[0:2], st_o[0:1]], axis=0)
    grads = (dw_in, dw_out) if later_grads is None else (recv_in[0], d_sb[4])
    return dx, dmod, st_i[2:3], grads


def _place():
    return lax.axis_index("x"), lax.axis_index("y"), lax.axis_index("c")


def _other_chips(mx, my):
    return [(1 - mx, my), (mx, 1 - my), (1 - mx, 1 - my)]


_ANY = pl.BlockSpec(memory_space=pl.ANY)


_GATHER_SCRATCH = [pltpu.SemaphoreType.DMA((7,)), pltpu.SemaphoreType.DMA((7,)), pltpu.SemaphoreType.DMA(())]


def _gather_ops(x_ref, out_ref, send_sems, recv_sems, local_sem):
    mx, my, mc = _place()
    me, sibling = (mx, my, mc), (mx, my, 1 - mc)
    chips = _other_chips(mx, my)

    def slot(px, py, pc):
        return out_ref.at[4 * px + 2 * py + pc]

    def copy(k, block, to, src=None):
        return pltpu.make_async_remote_copy(
            src_ref=slot(*block) if src is None else src, dst_ref=slot(*block),
            send_sem=send_sems.at[k], recv_sem=recv_sems.at[k], device_id=to, device_id_type=MESH)

    mine = pltpu.make_async_copy(x_ref, slot(*me), local_sem)
    first = [copy(0, me, sibling, src=x_ref)]
    first += [copy(1 + j, me, (*chip, mc), src=x_ref) for j, chip in enumerate(chips)]
    passed = [copy(4 + j, (*chip, mc), sibling) for j, chip in enumerate(chips)]

    def start():
        mine.start()
        for cp in first:
            cp.start()

    def forward():
        for j, chip in enumerate(chips):
            copy(1 + j, (*chip, mc), me).wait_recv()
            passed[j].start()

    def finish():
        copy(0, sibling, me).wait_recv()
        for j, chip in enumerate(chips):
            copy(4 + j, (*chip, 1 - mc), me).wait_recv()
        for cp in first + passed:
            cp.wait_send()
        mine.wait()

    return start, forward, finish


def allgather8(x, name):
    def body(x_ref, out_ref, send_sems, recv_sems, local_sem):
        for step in _gather_ops(x_ref, out_ref, send_sems, recv_sems, local_sem):
            step()

    return pl.pallas_call(
        body, name=name, out_shape=jax.ShapeDtypeStruct((8,) + x.shape, x.dtype),
        in_specs=[_ANY], out_specs=_ANY, scratch_shapes=_GATHER_SCRATCH,
    )(x)


class _Exchange:
    def __init__(self, ship):
        self.ship = list(ship or ())
        self.n_in = len(self.ship)
        self.n_out = 1 if self.ship else 0
        self.rows = [a.shape[1] for a in self.ship]
        self.in_specs = [_ANY] * self.n_in
        self.out_specs = [_ANY] * self.n_out
        self.out_shape = [jax.ShapeDtypeStruct((N_SHARD, sum(self.rows), SHARD_W), BF16)] * self.n_out
        sem = pltpu.SemaphoreType.DMA
        self.scratch = [sem((3,)), sem((3,)), sem(())] * self.n_out

    def ops(self, ship_refs, tail):
        if not self.ship:
            return (lambda: None), (lambda: None)
        recv, send_sems, recv_sems, local_sem = tail
        mx, my, mc = _place()
        my_chip = 2 * mx + my
        chips = _other_chips(mx, my)

        def pieces(s):
            firsts = np.cumsum([0] + self.rows[:-1])
            return [(ref.at[s], int(r0), n) for ref, r0, n in zip(ship_refs, firsts, self.rows)]

        def start():
            for src, r0, n in pieces(my_chip):
                pltpu.make_async_copy(src, recv.at[my_chip, pl.ds(r0, n)], local_sem).start()
            for j, (px, py) in enumerate(chips):
                for src, r0, n in pieces(2 * px + py):
                    pltpu.make_async_remote_copy(
                        src_ref=src, dst_ref=recv.at[my_chip, pl.ds(r0, n)],
                        send_sem=send_sems.at[j], recv_sem=recv_sems.at[j],
                        device_id=(px, py, mc), device_id_type=MESH).start()

        def finish():
            for j, (px, py) in enumerate(chips):
                whole = recv.at[2 * px + py]
                both = pltpu.make_async_remote_copy(
                    src_ref=whole, dst_ref=whole, send_sem=send_sems.at[j], recv_sem=recv_sems.at[j],
                    device_id=(px, py, mc), device_id_type=MESH)
                both.wait_recv()
                both.wait_send()
            pltpu.make_async_copy(recv.at[my_chip], recv.at[my_chip], local_sem).wait()

        return start, finish


def sum_slots(recv_a, recv_b, tr=256):
    n, rows_a, cols = recv_a.shape
    na, nb = rows_a // tr, recv_b.shape[1] // tr

    def body(a_ref, b_ref, o_ref):
        def total(r_ref):
            acc = r_ref[0].astype(F32)
            for k in range(1, n):
                acc = acc + r_ref[k].astype(F32)
            o_ref[...] = acc

        pl.when(pl.program_id(0) < na)(lambda: total(a_ref))
        pl.when(pl.program_id(0) >= na)(lambda: total(b_ref))

    return pl.pallas_call(
        body, name="sum_slots", grid=(na + nb,),
        in_specs=[pl.BlockSpec((n, tr, cols), lambda i: (0, jnp.minimum(i, na - 1), 0)),
                  pl.BlockSpec((n, tr, cols), lambda i: (0, jnp.maximum(i - na, 0), 0))],
        out_specs=pl.BlockSpec((tr, cols), lambda i: (i, 0)),
        out_shape=jax.ShapeDtypeStruct(((na + nb) * tr, cols), F32),
        compiler_params=_cp("arbitrary"),
    )(recv_a, recv_b)


def swap_sibling(p):
    def body(p_ref, out_ref, send_sem, recv_sem):
        mx, my, mc = _place()
        cp = pltpu.make_async_remote_copy(
            src_ref=p_ref, dst_ref=out_ref, send_sem=send_sem, recv_sem=recv_sem,
            device_id=(mx, my, 1 - mc), device_id_type=MESH)
        cp.start()
        cp.wait()

    return pl.pallas_call(
        body, name="swap_sibling", out_shape=jax.ShapeDtypeStruct(p.shape, p.dtype),
        in_specs=[_ANY], out_specs=_ANY,
        scratch_shapes=[pltpu.SemaphoreType.DMA(()), pltpu.SemaphoreType.DMA(())],
    )(p)


def _adamw(w, g, m, v):
    m = ADAM_B1 * m + (1.0 - ADAM_B1) * g
    v = ADAM_B2 * v + (1.0 - ADAM_B2) * (g * g)
    m_hat = m / (1.0 - ADAM_B1 ** ADAM_STEP)
    v_hat = v / (1.0 - ADAM_B2 ** ADAM_STEP)
    delta = -ADAM_LR * (m_hat / (jnp.sqrt(v_hat) + ADAM_EPS) + ADAM_WD * w)
    return delta, m, v


def adam_slab(p_own, p_sib, w, m, v, row0, name, tr=256):
    L, R, C = w.shape
    nr = R // tr

    def body(a_ref, b_ref, w_ref, m_ref, v_ref, g_out, d_out, m_out, v_out):
        g = a_ref[...] + b_ref[...]
        d, m2, v2 = _adamw(w_ref[0], g, m_ref[0], v_ref[0])
        g_out[0], d_out[0], m_out[0], v_out[0] = g, d, m2, v2

    slab = pl.BlockSpec((tr, C), lambda l, i: (row0 // tr + l * nr + i, 0))
    blk = pl.BlockSpec((1, tr, C), lambda l, i: (l, i, 0))
    return pl.pallas_call(
        body, name=name, grid=(L, nr),
        in_specs=[slab, slab, blk, blk, blk], out_specs=[blk] * 4,
        out_shape=[jax.ShapeDtypeStruct(w.shape, F32)] * 4,
        compiler_params=_cp("arbitrary", "arbitrary"),
    )(p_own, p_sib, w, m, v)


def ada_fwd(c_all, w_ada):
    L, D, W = w_ada.shape

    def body(c_ref, w_ref, o_ref):
        cv = c_ref[...]
        o_ref[0] = jnp.dot(cv * _sigmoid(cv), w_ref[0], precision=lax.Precision.HIGHEST,
                           preferred_element_type=F32)

    return pl.pallas_call(
        body, name="ada_fwd", grid=(L,),
        in_specs=[pl.BlockSpec((8, D), lambda l: (0, 0)), pl.BlockSpec((1, D, W), lambda l: (l, 0, 0))],
        out_specs=pl.BlockSpec((1, 8, W), lambda l: (l, 0, 0)),
        out_shape=jax.ShapeDtypeStruct((L, 8, W), F32),
        compiler_params=_cp("arbitrary"),
    )(c_all, w_ada)


def vecs_build(mod_all, b_ada, norm_g):
    W = mod_all.shape[2]

    def body(m_ref, b_ref, g_ref, o_ref):
        mx, my, mc = _place()
        me = 4 * mx + 2 * my + mc
        rowid = lax.broadcasted_iota(jnp.int32, (2 * 8, 1), 0)
        o_ref[...] = jnp.zeros_like(o_ref)
        for l in range(DEPTH):
            parts = [jnp.sum(jnp.where(rowid == l * 8 + me, m_ref[2 * s + mc], 0.0), axis=0, keepdims=True)
                     for s in range(N_SHARD)]
            mod = jnp.concatenate(parts, axis=1) + b_ref[l:l + 1, :]
            for t in range(3):
                o_ref[l, t:t + 1, :] = mod[:, t * D_MODEL:(t + 1) * D_MODEL]
            o_ref[l, 3:4, :] = g_ref[l:l + 1, :]

    return pl.pallas_call(
        body, name="vecs_build", out_shape=jax.ShapeDtypeStruct((DEPTH, 8, D_MODEL), F32),
    )(mod_all, b_ada, norm_g)


def ada_update(dmods, c_t, w, m, v, tr=256):
    L, D, W = w.shape

    def body(dm_ref, c_ref, w_ref, m_ref, v_ref, g_out, d_out, m_out, v_out):
        mx, my, _ = _place()
        shard = 2 * mx + my
        dm = jnp.zeros((8, W), F32)
        for s in range(N_SHARD):
            dm = dm + jnp.where(shard == s, dm_ref[0, :, s * W:(s + 1) * W], 0.0)
        cv = c_ref[...]
        ca = cv * _sigmoid(cv)
        g = jnp.zeros((tr, W), F32)
        for b in range(8):
            g = g + ca[:, b:b + 1] * dm[b:b + 1, :]
        d, m2, v2 = _adamw(w_ref[0], g, m_ref[0], v_ref[0])
        g_out[0], d_out[0], m_out[0], v_out[0] = g, d, m2, v2

    blk = pl.BlockSpec((1, tr, W), lambda l, i: (l, i, 0))
    return pl.pallas_call(
        body, name="ada_update", grid=(L, D // tr),
        in_specs=[pl.BlockSpec((1, 8, 3 * D), lambda l, i: (l, 0, 0)), pl.BlockSpec((tr, 8), lambda l, i: (i, 0)),
                  blk, blk, blk],
        out_specs=[blk] * 4, out_shape=[jax.ShapeDtypeStruct(w.shape, F32)] * 4,
        compiler_params=_cp("arbitrary", "arbitrary"),
    )(dmods, c_t, w, m, v)


STAT_ROWS = 16


def small_update(stats_all, norm, b_ada, final):
    def body(s_ref, *refs):
        ins, outs = refs[:9], refs[9:]
        tot = s_ref[0]
        for k in range(1, 8):
            tot = tot + s_ref[k]
        g_norm = tot[0:2, :]
        g_final = tot[2:3, :]
        g_b = jnp.concatenate(
            [jnp.concatenate([tot[3 + 3 * l + t:4 + 3 * l + t, :] for t in range(3)], axis=1) for l in range(DEPTH)],
            axis=0)
        for p, g in enumerate((g_norm, g_b, g_final)):
            w_ref, m_ref, v_ref = ins[3 * p:3 * p + 3]
            d, m2, v2 = _adamw(w_ref[...], g, m_ref[...], v_ref[...])
            for o_ref, val in zip(outs[4 * p:4 * p + 4], (g, d, m2, v2)):
                o_ref[...] = val
        loss = (0.5 / D_MODEL) * jnp.sum(tot[9:10, :], axis=1, keepdims=True)
        outs[12][...] = jnp.broadcast_to(loss, (8, LANES))

    shapes = []
    for w, _, _ in (norm, b_ada, final):
        shapes += [jax.ShapeDtypeStruct(w.shape, F32)] * 4
    shapes.append(jax.ShapeDtypeStruct((8, LANES), F32))
    return pl.pallas_call(body, name="small_update", out_shape=shapes)(stats_all, *norm, *b_ada, *final)


def kernel(x, c, norm_g, w_ada, b_ada, w_in, w_out, final_g, loss_target, m_norm_g, m_w_ada, m_b_ada, m_w_in, m_w_out, m_final_g, v_norm_g, v_w_ada, v_b_ada, v_w_in, v_w_out, v_final_g):
    S, D = x.shape[1], x.shape[2]
    mc = lax.axis_index("c")
    out_rows = D // N_SHARD

    def my_half(a, rows):
        return lax.dynamic_slice_in_dim(a, mc * rows, rows, axis=0)

    wblk = [jnp.concatenate([my_half(w_in[l], D // 2), my_half(w_out[l], out_rows // 2)], axis=0).astype(BF16)
            for l in range(DEPTH)]

    def unpack(wall):
        wall = wall.reshape(N_SHARD, 2, wall.shape[1], SHARD_W)
        return wall[:, :, :D // 2].reshape(N_SHARD, D, SHARD_W), wall[:, :, D // 2:].reshape(D, D)

    weights = [unpack(allgather8(wblk[0], "gather_weights"))]

    c_all = allgather8(jnp.broadcast_to(c, (8, D)), "gather_c")[:, 0, :]
    mod_all = allgather8(ada_fwd(c_all, w_ada).reshape(DEPTH * 8, -1), "gather_mod")
    vecs = vecs_build(mod_all, b_ada, norm_g)

    tabs = (*rope_tables(S), ret_log_gamma())
    h = x[0]
    saved = []
    for l in range(DEPTH):
        h, sv, wall = layer_fwd(h, vecs[l], *weights[l], tabs, wblk[l + 1] if l + 1 < DEPTH else None)
        saved.append(sv)
        if wall is not None:
            weights.append(unpack(wall))
    dx, st_loss = loss_head(h, jnp.broadcast_to(final_g[None, :], (8, D)), loss_target[0])

    dmod, dnorm, grads = [None] * DEPTH, [None] * DEPTH, None
    for l in reversed(range(DEPTH)):
        dx, dmod[l], dnorm[l], grads = layer_bwd(dx, saved[l], vecs[l], *weights[l], tabs, grads)

    p_own = sum_slots(*grads)
    p_sib = swap_sibling(p_own)
    res_in = adam_slab(p_own, p_sib, w_in, m_w_in, v_w_in, 0, "adam_w_in")
    res_out = adam_slab(p_own, p_sib, w_out, m_w_out, v_w_out, DEPTH * D, "adam_w_out", tr=128)

    stats = jnp.concatenate(dnorm + [st_loss[0:1]] + dmod + [st_loss[1:2], jnp.zeros((STAT_ROWS - 10, D), F32)], axis=0)
    stats_all = allgather8(stats, "gather_stats")
    dmods = stats_all[:, 3:9, :].reshape(8, DEPTH, 3 * D).transpose(1, 0, 2)
    res_ada = ada_update(dmods, c_all.T, w_ada, m_w_ada, v_w_ada)
    small = small_update(stats_all, (norm_g, m_norm_g, v_norm_g), (b_ada, m_b_ada, v_b_ada),
                         (final_g[None, :], m_final_g[None, :], v_final_g[None, :]))
    res_norm, res_b, res_final = small[0:4], small[4:8], [a[0] for a in small[8:12]]
    loss = small[12][0, 0]

    by_kind = [res_norm, res_ada, res_b, res_in, res_out, res_final]
    outs = [loss, dx[None]]
    for kind in range(4):
        outs += [r[kind] for r in by_kind]
    return tuple(outs)
```

```python
import functools

import numpy as np
import jax
import jax.numpy as jnp
from jax import lax
from jax.experimental import pallas as pl
from jax.experimental.pallas import tpu as pltpu

F32, BF16 = jnp.float32, jnp.bfloat16
MESH = pl.DeviceIdType.MESH

D_MODEL = 1024
DEPTH = 2
SHARD_W = 1024
N_SHARD = 4
GROUP_W = 512
LANES = 128
SB_HEAD_DIM = 64
RET_HEAD_DIM = 128
CHUNK = 64
ROPE_BASE = 10000.0
EPS = 1e-6
SQ_SCALE = SB_HEAD_DIM ** -0.5
RK_SCALE = RET_HEAD_DIM ** -0.5
SB_T = 256
RET_T = 256
EXP_ZERO = -104.0
VMEM_LIMIT_BYTES = 56 * 2 ** 20

ADAM_LR, ADAM_B1, ADAM_B2, ADAM_EPS, ADAM_WD, ADAM_STEP = 0.001, 0.9, 0.999, 1e-08, 0.01, 10


def _cp(*sem):
    return pltpu.CompilerParams(dimension_semantics=sem, vmem_limit_bytes=VMEM_LIMIT_BYTES)


def _dot(a, b):
    return lax.dot_general(a, b, (((1,), (0,)), ((), ())), preferred_element_type=F32)


def _dot_nt(a, b):
    return lax.dot_general(a, b, (((1,), (1,)), ((), ())), preferred_element_type=F32)


def _dot_tn(a, b):
    return lax.dot_general(a, b, (((0,), (0,)), ((), ())), preferred_element_type=F32)


def _running_sum(a, tri):
    return _dot(a.astype(BF16), tri)


def _sigmoid(x):
    return 1.0 / (1.0 + jnp.exp(-x))


def _rowsum(a):
    return jnp.sum(a, axis=1, keepdims=True)


def _rowmean(a):
    return jnp.mean(a, axis=1, keepdims=True)


def inproj_fwd(x, vecs, w3, tm=512):
    S, D = x.shape

    def body(x_ref, v_ref, w_ref, ret_ref, sg_ref, h_ref, sb_ref):
        xv = x_ref[...]
        r = lax.rsqrt(_rowmean(xv * xv) + EPS)
        h = xv * r * v_ref[3:4, :] * (1.0 + v_ref[1:2, :]) + v_ref[0:1, :]
        hb = h.astype(BF16)
        h_ref[...] = hb
        for s in range(N_SHARD):
            p = _dot(hb, w_ref[s])
            if s < 2:
                ret_ref[:, s * SHARD_W:(s + 1) * SHARD_W] = p
            if s == 2:
                sb_ref[:, 0:GROUP_W] = (p[:, 0:GROUP_W] * SQ_SCALE).astype(BF16)
                sb_ref[:, GROUP_W:SHARD_W] = p[:, GROUP_W:].astype(BF16)
            if s == 3:
                sb_ref[:, SHARD_W:SHARD_W + GROUP_W] = p[:, 0:GROUP_W].astype(BF16)
                sg_ref[...] = p[:, GROUP_W:]

    row = lambda w: pl.BlockSpec((tm, w), lambda i: (i, 0))
    return pl.pallas_call(
        body, name="inproj_fwd", grid=(S // tm,),
        in_specs=[row(D), pl.BlockSpec((8, D), lambda i: (0, 0)),
                  pl.BlockSpec((N_SHARD, D, SHARD_W), lambda i: (0, 0, 0))],
        out_specs=[row(2 * SHARD_W), row(GROUP_W), row(D), row(3 * GROUP_W)],
        out_shape=[jax.ShapeDtypeStruct((S, 2 * SHARD_W), F32), jax.ShapeDtypeStruct((S, GROUP_W), F32),
                   jax.ShapeDtypeStruct((S, D), BF16), jax.ShapeDtypeStruct((S, 3 * GROUP_W), BF16)],
        compiler_params=_cp("arbitrary"),
    )(x, vecs, w3)


def _sb_logits(qh, k2, keep):
    z = _dot_nt(qh, k2)
    sp = jnp.log(1.0 + jnp.exp(-jnp.abs(z)))
    lb = jnp.minimum(z, 0.0) - sp
    lk = lb - z
    if keep is not None:
        lk = lk * keep
    return lb, lk


def sb_keep_masks():
    T = SB_T
    row = lax.broadcasted_iota(jnp.int32, (2, 2 * T, 2 * T), 1) & (T - 1)
    col = lax.broadcasted_iota(jnp.int32, (2, 2 * T, 2 * T), 2)
    diag_first = lax.broadcasted_iota(jnp.int32, (2, 2 * T, 2 * T), 0) * T
    return (col < diag_first + row).astype(F32)


def _sb_masks(i):
    T = SB_T
    r = lax.broadcasted_iota(jnp.int32, (T, T), 0)
    c = lax.broadcasted_iota(jnp.int32, (T, T), 1)
    later = jnp.where(r > c, 1.0, 0.0).astype(BF16)
    earlier = jnp.where(r < c, 1.0, 0.0).astype(BF16)
    lane = lax.broadcasted_iota(jnp.int32, (1, LANES), 1)
    return jnp.maximum(i - 1, 0), later, earlier, lane < SB_HEAD_DIM


def _sb_rows(ref, j, tiles=1):
    return ref[pl.ds(pl.multiple_of(j * SB_T, SB_T), tiles * SB_T), :]


def _sb_keep_spec():
    return pl.BlockSpec((1, 2 * SB_T, 2 * SB_T), lambda p, i: (jnp.minimum(i, 1), 0, 0))


def sb_fwd(sb, sg, keep, gather=None):
    S = sb.shape[0]
    T = SB_T
    nq = S // T
    carried = [] if gather is None else [gather]

    def body(*refs):
        (q_ref, k_ref, v_ref, sg_ref, keep_ref), refs = refs[:5], refs[5:]
        p, i = pl.program_id(0), pl.program_id(1)
        if carried:
            x_ref, y_ref, o_ref, end_ref, out_ref, send_sems, recv_sems, local_sem = refs
            start, forward, finish = _gather_ops(x_ref, out_ref, send_sems, recv_sems, local_sem)
            pl.when(jnp.logical_and(p == 0, i == 0))(start)
            pl.when(jnp.logical_and(p == 2, i == 0))(forward)
        else:
            y_ref, o_ref, end_ref = refs
        first, later, _, head0 = _sb_masks(i)
        keep = keep_ref[0]
        q2 = q_ref[...]
        zero = jnp.zeros_like(q2)
        qs = jnp.concatenate([jnp.where(head0, q2, zero), jnp.where(head0, zero, q2)], axis=0)

        lb, lk = _sb_logits(qs, _sb_rows(k_ref, first, 2), keep)
        rs_left, rs_right = _rowsum(lk[:, :T]), _rowsum(lk[:, T:])
        suffix = jnp.concatenate([_running_sum(lk[:, :T], later) + rs_right, _running_sum(lk[:, T:], later)], axis=1)
        a = jnp.exp(lb + suffix) * keep
        acc = _dot(a.astype(BF16), _sb_rows(v_ref, first, 2))
        R = rs_left + rs_right

        def tile(j, R):
            lb, lk = _sb_logits(qs, _sb_rows(k_ref, j), None)
            a = jnp.exp(lb + _running_sum(lk, later) + R)
            return _dot(a.astype(BF16), _sb_rows(v_ref, j)), R + _rowsum(lk)

        def cond(st):
            return jnp.logical_and(st[0] >= 0, st[3] > EXP_ZERO)

        def step(st):
            c, Rn = tile(st[0], st[2])
            return st[0] - 1, st[1] + c, Rn, jnp.max(Rn)

        j_end, acc, R, _ = lax.while_loop(cond, step, (first - 1, acc, R, jnp.max(R)))
        end_ref[0, 0, 0:2 * T, :] = jnp.broadcast_to(R, (2 * T, 8))
        end_ref[0, 0, 2 * T:, :] = jnp.full((8, 8), j_end.astype(F32))
        o = jnp.where(head0, acc[:T], acc[T:])
        o_ref[...] = o
        sg = sg_ref[...]
        y_ref[...] = (o * (sg * _sigmoid(sg))).astype(BF16)
        if carried:
            pl.when(jnp.logical_and(p == 3, i == nq - 1))(finish)

    return pl.pallas_call(
        body, name="sb_fwd", grid=(4, nq),
        in_specs=[pl.BlockSpec((T, LANES), lambda p, i: (i, p)),
                  pl.BlockSpec((S, LANES), lambda p, i: (0, 4 + p)),
                  pl.BlockSpec((S, LANES), lambda p, i: (0, 8 + p)),
                  pl.BlockSpec((T, LANES), lambda p, i: (i, p)), _sb_keep_spec()] + [_ANY for _ in carried],
        out_specs=[pl.BlockSpec((T, LANES), lambda p, i: (i, p)),
                   pl.BlockSpec((T, LANES), lambda p, i: (i, p)),
                   pl.BlockSpec((1, 1, 2 * T + 8, 8), lambda p, i: (p, i, 0, 0))] + [_ANY for _ in carried],
        out_shape=[jax.ShapeDtypeStruct((S, GROUP_W), BF16),
                   jax.ShapeDtypeStruct((S, GROUP_W), F32),
                   jax.ShapeDtypeStruct((4, nq, 2 * T + 8, 8), F32)]
        + [jax.ShapeDtypeStruct((8,) + a.shape, a.dtype) for a in carried],
        scratch_shapes=_GATHER_SCRATCH if carried else [],
        compiler_params=_cp("arbitrary", "arbitrary"),
    )(sb, sb, sb, sg, keep, *carried)


def sb_bwd(sb, sg, o, sb_end, keep, dycat, ship=None):
    S = sb.shape[0]
    T = SB_T
    nq = S // T
    ex = _Exchange(ship)

    def body(*refs):
        (q_ref, k_ref, v_ref, sg_ref, o_ref, dy_ref, end_ref, keep_ref), refs = refs[:8], refs[8:]
        ship_refs, (dq_ref, dk_ref, dv_ref, dsg_ref), refs = refs[:ex.n_in], refs[ex.n_in:ex.n_in + 4], refs[ex.n_in + 4:]
        recv, (dk_acc, dv_acc), sems = refs[:ex.n_out], refs[ex.n_out:ex.n_out + 2], refs[ex.n_out + 2:]
        start, finish = ex.ops(ship_refs, recv + sems)
        p, i = pl.program_id(0), pl.program_id(1)
        pl.when(jnp.logical_and(p == 0, i == 0))(start)

        @pl.when(i == 0)
        def _():
            dk_acc[...] = jnp.zeros_like(dk_acc)
            dv_acc[...] = jnp.zeros_like(dv_acc)

        first, later, earlier, head0 = _sb_masks(i)
        keep = keep_ref[0]
        q2 = q_ref[...]
        sg = sg_ref[...]
        sig = _sigmoid(sg)
        dy = dy_ref[...]
        dsg_ref[...] = (dy * o_ref[...] * (sig * (1.0 + sg * (1.0 - sig)))).astype(BF16)
        do_b = (dy * (sg * sig)).astype(BF16)
        zero = jnp.zeros_like(q2)
        qs = jnp.concatenate([jnp.where(head0, q2, zero), jnp.where(head0, zero, q2)], axis=0)
        dos = jnp.concatenate([jnp.where(head0, do_b, zero), jnp.where(head0, zero, do_b)], axis=0)

        end = end_ref[0, 0]
        j_end = jnp.max(end[2 * T:, :]).astype(jnp.int32)

        def grads(j, tiles, a, lb, g, G, keep):
            dz = g - jnp.exp(lb) * (g + G)
            if keep is not None:
                dz = dz * keep
            dzb = dz.astype(BF16)
            rows = pl.ds(pl.multiple_of(j * T, T), tiles * T)
            dk_acc[rows, :] += _dot_tn(dzb, qs)
            dv_acc[rows, :] += _dot_tn(a.astype(BF16), dos)
            return _dot(dzb, _sb_rows(k_ref, j, tiles))

        def sweep(j, st):
            dq, G0, left = st
            lb, lk = _sb_logits(qs, _sb_rows(k_ref, j), None)
            stick = left - _rowsum(lk)
            a = jnp.exp(lb + _running_sum(lk, later) + stick)
            g = a * _dot_nt(dos, _sb_rows(v_ref, j))
            G = _running_sum(g, earlier) + G0
            return dq + grads(j, 1, a, lb, g, G, None), G0 + _rowsum(g), stick

        dq, G0, _ = lax.fori_loop(j_end + 1, first, sweep,
                                  (jnp.zeros((2 * T, LANES), F32), jnp.zeros((2 * T, 1), F32), end[:2 * T, 0:1]))

        lb, lk = _sb_logits(qs, _sb_rows(k_ref, first, 2), keep)
        suffix = jnp.concatenate([_running_sum(lk[:, :T], later) + _rowsum(lk[:, T:]),
                                  _running_sum(lk[:, T:], later)], axis=1)
        a = jnp.exp(lb + suffix) * keep
        g = a * _dot_nt(dos, _sb_rows(v_ref, first, 2))
        G = jnp.concatenate([_running_sum(g[:, :T], earlier) + G0,
                             _running_sum(g[:, T:], earlier) + (G0 + _rowsum(g[:, :T]))], axis=1)
        dq = dq + grads(first, 2, a, lb, g, G, keep)
        dq_ref[...] = (jnp.where(head0, dq[:T], dq[T:]) * SQ_SCALE).astype(BF16)

        @pl.when(i == nq - 1)
        def _():
            dk_ref[...] = dk_acc[...].astype(BF16)
            dv_ref[...] = dv_acc[...].astype(BF16)

        pl.when(jnp.logical_and(p == 3, i == nq - 1))(finish)

    tile_spec = lambda c0: pl.BlockSpec((T, LANES), lambda p, i: (i, c0 + p))
    head_spec = lambda c0: pl.BlockSpec((S, LANES), lambda p, i: (0, c0 + p))
    return pl.pallas_call(
        body, name="sb_bwd", grid=(4, nq),
        in_specs=[tile_spec(0), head_spec(4), head_spec(8), tile_spec(0), tile_spec(0), tile_spec(4),
                  pl.BlockSpec((1, 1, 2 * T + 8, 8), lambda p, i: (p, i, 0, 0)), _sb_keep_spec()] + ex.in_specs,
        out_specs=[tile_spec(0), head_spec(0), head_spec(0), tile_spec(0)] + ex.out_specs,
        out_shape=[jax.ShapeDtypeStruct((S, GROUP_W), BF16)] * 4 + ex.out_shape,
        scratch_shapes=[pltpu.VMEM((S, LANES), F32), pltpu.VMEM((S, LANES), F32)] + ex.scratch,
        compiler_params=_cp("arbitrary", "arbitrary"),
    )(sb, sb, sb, sg, o, dycat, sb_end, keep, *ex.ship)


def rope_tables(S):
    half = RET_HEAD_DIM // 2
    inv = ROPE_BASE ** (-jnp.arange(half, dtype=F32) / half)
    ang = jnp.arange(S, dtype=F32)[:, None] * inv[None, :]
    cos, sin = jnp.cos(ang), jnp.sin(ang)
    return jnp.concatenate([cos, cos], axis=1), jnp.concatenate([-sin, sin], axis=1)


def ret_log_gamma():
    return jnp.log1p(-(2.0 ** (-5.0 - jnp.arange(4, dtype=F32))))


def _swap_halves(a):
    return pltpu.roll(a, RET_HEAD_DIM // 2, axis=1)


def _ret_decay_mask(lg):
    n = lax.broadcasted_iota(jnp.int32, (RET_T, RET_T), 0)
    m = lax.broadcasted_iota(jnp.int32, (RET_T, RET_T), 1)
    dist = jnp.abs(n - m).astype(F32)
    return jnp.where((m // CHUNK) <= (n // CHUNK), jnp.exp(lg * dist), 0.0)


def _ret_block(lg, rq, rk, rv, cosf, sinf, dm):
    q = rq * cosf + _swap_halves(rq) * sinf
    k = (rk * cosf + _swap_halves(rk) * sinf) * RK_SCALE
    qb, kb, vb = q.astype(BF16), k.astype(BF16), rv.astype(BF16)
    sc = _dot_nt(qb, kb) * dm
    nloc = lax.broadcasted_iota(jnp.int32, (RET_T, 1), 0).astype(F32)
    qdec = jnp.exp(lg * (nloc + 1.0))
    kdec = jnp.exp(lg * (RET_T - 1.0 - nloc))
    block_dec = jnp.exp(jnp.full((1, LANES), lg * RET_T, F32))
    return q, k, qb, kb, vb, sc, qdec, kdec, block_dec


def _ret_specs(S, rb):
    group = lambda c0: pl.BlockSpec((RET_T, GROUP_W), lambda b: (rb(b), c0))
    return group, pl.BlockSpec((RET_T, LANES), lambda b: (rb(b), 0))


def _head(ref, h):
    return ref[:, h * LANES:(h + 1) * LANES]


def ret_fwd(proj, cosf, sinf, lgam):
    S = proj.shape[0]
    nb = S // RET_T
    group, row_tab = _ret_specs(S, lambda b: b)

    def body(lg_ref, rq_ref, rk_ref, rv_ref, rg_ref, cos_ref, sin_ref, y_ref, o_ref, st_out, st_ref, dm_ref):
        @pl.when(pl.program_id(0) == 0)
        def _():
            st_ref[...] = jnp.zeros_like(st_ref)
            for h in range(4):
                dm_ref[h] = _ret_decay_mask(lg_ref[h])

        cosf, sinf = cos_ref[...], sin_ref[...]
        H = range(4)
        lanes = [slice(h * LANES, (h + 1) * LANES) for h in H]
        blk = [_ret_block(lg_ref[h], _head(rq_ref, h), _head(rk_ref, h), _head(rv_ref, h), cosf, sinf, dm_ref[h])
               for h in H]
        q, k, qb, kb, vb, sc, qdec, kdec, block_dec = zip(*blk)
        st = [st_ref[h] for h in H]
        for h in H:
            st_out[h, 0] = st[h]
        o = [_dot(sc[h].astype(BF16), vb[h]) + _dot(qb[h], st[h].astype(BF16)) * qdec[h] for h in H]
        for h in H:
            st_ref[h] = st[h] * block_dec[h] + _dot_tn((k[h] * kdec[h]).astype(BF16), vb[h])
        for h in H:
            o_ref[:, lanes[h]] = o[h]
        cen = [o[h] - _rowmean(o[h]) for h in H]
        on = [cen[h] * lax.rsqrt(_rowmean(cen[h] * cen[h]) + EPS) for h in H]
        rg = [_head(rg_ref, h) for h in H]
        for h in H:
            y_ref[:, lanes[h]] = (on[h] * (rg[h] * _sigmoid(rg[h]))).astype(BF16)

    return pl.pallas_call(
        body, name="ret_fwd", grid=(nb,),
        in_specs=[pl.BlockSpec(memory_space=pltpu.SMEM),
                  group(0), group(1), group(2), group(3), row_tab, row_tab],
        out_specs=[group(0), group(0),
                   pl.BlockSpec((4, 1, LANES, LANES), lambda b: (0, b, 0, 0))],
        out_shape=[jax.ShapeDtypeStruct((S, GROUP_W), BF16),
                   jax.ShapeDtypeStruct((S, GROUP_W), F32),
                   jax.ShapeDtypeStruct((4, nb, LANES, LANES), F32)],
        scratch_shapes=[pltpu.VMEM((4, LANES, LANES), F32), pltpu.VMEM((4, RET_T, RET_T), F32)],
        compiler_params=_cp("arbitrary"),
    )(lgam, proj, proj, proj, proj, cosf, sinf)


def ret_bwd(proj, cosf, sinf, lgam, o, states, dycat):
    S = proj.shape[0]
    nb = S // RET_T
    rev = lambda b: nb - 1 - b
    group, row_tab = _ret_specs(S, rev)

    def body(lg_ref, rq_ref, rk_ref, rv_ref, rg_ref, cos_ref, sin_ref, o_ref, st_in, dy_ref,
             drq_ref, drk_ref, drv_ref, drg_ref, ds_ref, dm_ref):
        @pl.when(pl.program_id(0) == 0)
        def _():
            ds_ref[...] = jnp.zeros_like(ds_ref)
            for h in range(4):
                dm_ref[h] = _ret_decay_mask(lg_ref[h])

        cosf, sinf = cos_ref[...], sin_ref[...]
        H = range(4)
        lanes = [slice(h * LANES, (h + 1) * LANES) for h in H]
        dms = [dm_ref[h] for h in H]
        blk = [_ret_block(lg_ref[h], _head(rq_ref, h), _head(rk_ref, h), _head(rv_ref, h), cosf, sinf, dms[h])
               for h in H]
        q, k, qb, kb, vb, sc, qdec, kdec, block_dec = zip(*blk)
        o_v = [_head(o_ref, h) for h in H]
        cen = [o_v[h] - _rowmean(o_v[h]) for h in H]
        rstd = [lax.rsqrt(_rowmean(cen[h] * cen[h]) + EPS) for h in H]
        on = [cen[h] * rstd[h] for h in H]
        rg = [_head(rg_ref, h) for h in H]
        sig = [_sigmoid(rg[h]) for h in H]
        dy = [_head(dy_ref, h) for h in H]
        for h in H:
            drg_ref[:, lanes[h]] = (dy[h] * on[h] * (sig[h] * (1.0 + rg[h] * (1.0 - sig[h])))).astype(BF16)
        don = [dy[h] * (rg[h] * sig[h]) for h in H]
        do = [rstd[h] * (don[h] - _rowmean(don[h]) - on[h] * _rowmean(don[h] * on[h])) for h in H]
        dob = [do[h].astype(BF16) for h in H]
        dsc = [(_dot_nt(dob[h], vb[h]) * dms[h]).astype(BF16) for h in H]
        st_b = [st_in[h, 0].astype(BF16) for h in H]
        dsn = [ds_ref[h] for h in H]
        dsn_b = [dsn[h].astype(BF16) for h in H]
        dq = [_dot(dsc[h], kb[h]) + _dot_nt(dob[h], st_b[h]) * qdec[h] for h in H]
        dk = [(_dot_tn(dsc[h], qb[h]) + _dot_nt(vb[h], dsn_b[h]) * kdec[h]) * RK_SCALE for h in H]
        dv = [_dot_tn(sc[h].astype(BF16), dob[h]) + _dot((k[h] * kdec[h]).astype(BF16), dsn_b[h]) for h in H]
        for h in H:
            ds_ref[h] = dsn[h] * block_dec[h] + _dot_tn((q[h] * qdec[h]).astype(BF16), dob[h])
        for h in H:
            drq_ref[:, lanes[h]] = (dq[h] * cosf + _swap_halves(dq[h] * sinf)).astype(BF16)
            drk_ref[:, lanes[h]] = (dk[h] * cosf + _swap_halves(dk[h] * sinf)).astype(BF16)
            drv_ref[:, lanes[h]] = dv[h].astype(BF16)

    return pl.pallas_call(
        body, name="ret_bwd", grid=(nb,),
        in_specs=[pl.BlockSpec(memory_space=pltpu.SMEM),
                  group(0), group(1), group(2), group(3), row_tab, row_tab,
                  group(0), pl.BlockSpec((4, 1, LANES, LANES), lambda b: (0, rev(b), 0, 0)),
                  group(0)],
        out_specs=[group(0)] * 4,
        out_shape=[jax.ShapeDtypeStruct((S, GROUP_W), BF16)] * 4,
        scratch_shapes=[pltpu.VMEM((4, LANES, LANES), F32), pltpu.VMEM((4, RET_T, RET_T), F32)],
        compiler_params=_cp("arbitrary"),
    )(lgam, proj, proj, proj, proj, cosf, sinf, o, states, dycat)


def outproj_fwd(x, vecs, y_ret, y_sb, w_out, tm=512):
    S, D = x.shape

    def body(x_ref, v_ref, yr_ref, ys_ref, w_ref, y_ref, xo_ref):
        y = _dot(yr_ref[...], w_ref[0:GROUP_W, :]) + _dot(ys_ref[...], w_ref[GROUP_W:, :])
        y_ref[...] = y.astype(BF16)
        xo_ref[...] = x_ref[...] + v_ref[2:3, :] * y

    row = lambda w: pl.BlockSpec((tm, w), lambda i: (i, 0))
    return pl.pallas_call(
        body, name="outproj_fwd", grid=(S // tm,),
        in_specs=[row(D), pl.BlockSpec((8, D), lambda i: (0, 0)), row(GROUP_W), row(GROUP_W),
                  pl.BlockSpec((D, D), lambda i: (0, 0))],
        out_specs=[row(D), row(D)],
        out_shape=[jax.ShapeDtypeStruct((S, D), BF16), jax.ShapeDtypeStruct((S, D), F32)],
        compiler_params=_cp("arbitrary"),
    )(x, vecs, y_ret, y_sb, w_out)


def loss_head(x, final_g, target, tm=512):
    S, D = x.shape

    def body(x_ref, g_ref, t_ref, dx_ref, st_ref):
        @pl.when(pl.program_id(0) == 0)
        def _():
            st_ref[...] = jnp.zeros_like(st_ref)

        xv = x_ref[...]
        g = g_ref[0:1, :]
        r = lax.rsqrt(_rowmean(xv * xv) + EPS)
        xn = xv * r
        err = xn * g - t_ref[...]
        dy = err * (1.0 / D)
        dxn = dy * g
        dx_ref[...] = r * (dxn - xn * _rowmean(dxn * xn))
        st_ref[0:1, :] += jnp.sum(dy * xn, axis=0, keepdims=True)
        st_ref[1:2, :] += jnp.sum(err * err, axis=0, keepdims=True)

    row = pl.BlockSpec((tm, D), lambda i: (i, 0))
    fixed = pl.BlockSpec((8, D), lambda i: (0, 0))
    return pl.pallas_call(
        body, name="loss_head", grid=(S // tm,),
        in_specs=[row, fixed, row], out_specs=[row, fixed],
        out_shape=[jax.ShapeDtypeStruct((S, D), F32), jax.ShapeDtypeStruct((8, D), F32)],
        compiler_params=_cp("arbitrary"),
    )(x, final_g, target)


def outproj_bwd(dx, y, vecs, y_ret, y_sb, w_out, tm=512):
    S, D = dx.shape
    n = S // tm

    def body(dx_ref, y_ref, v_ref, yr_ref, ys_ref, w_ref, dyc_ref, dw_ref, st_ref, acc):
        i = pl.program_id(0)

        @pl.when(i == 0)
        def _():
            st_ref[...] = jnp.zeros_like(st_ref)
            acc[...] = jnp.zeros_like(acc)

        dxv = dx_ref[...]
        st_ref[0:1, :] += jnp.sum(dxv * y_ref[...].astype(F32), axis=0, keepdims=True)
        dyy = (dxv * v_ref[2:3, :]).astype(BF16)
        dyc_ref[...] = _dot_nt(dyy, w_ref[...])
        acc[0:GROUP_W, :] += _dot_tn(yr_ref[...], dyy)
        acc[GROUP_W:, :] += _dot_tn(ys_ref[...], dyy)

        @pl.when(i == n - 1)
        def _():
            dw_ref[...] = acc[...].astype(BF16)

    row = lambda w: pl.BlockSpec((tm, w), lambda i: (i, 0))
    fixed = lambda r: pl.BlockSpec((r, D), lambda i: (0, 0))
    return pl.pallas_call(
        body, name="outproj_bwd", grid=(n,),
        in_specs=[row(D), row(D), fixed(8), row(GROUP_W), row(GROUP_W), fixed(D)],
        out_specs=[row(D), fixed(D), fixed(8)],
        out_shape=[jax.ShapeDtypeStruct((S, D), F32), jax.ShapeDtypeStruct((D, D), BF16),
                   jax.ShapeDtypeStruct((8, D), F32)],
        scratch_shapes=[pltpu.VMEM((D, D), F32)],
        compiler_params=_cp("arbitrary"),
    )(dx, y, vecs, y_ret, y_sb, w_out)


def inproj_bwd_x(pieces, w3, x, vecs, dx_res, ship=None, tm=512):
    S, D = x.shape
    n = S // tm
    ex = _Exchange(ship)

    def body(*refs):
        p_refs, (w_ref, x_ref, v_ref, dr_ref), refs = refs[:8], refs[8:12], refs[12:]
        ship_refs, (dx_ref, st_ref), refs = refs[:ex.n_in], refs[ex.n_in:ex.n_in + 2], refs[ex.n_in + 2:]
        start, finish = ex.ops(ship_refs, refs)

        @pl.when(pl.program_id(0) == 0)
        def _():
            st_ref[...] = jnp.zeros_like(st_ref)
            start()

        dh = jnp.zeros((tm, D), F32)
        for k, p_ref in enumerate(p_refs):
            c0 = (k % 2) * GROUP_W
            dh = dh + _dot_nt(p_ref[...], w_ref[k // 2, :, c0:c0 + GROUP_W])
        xv = x_ref[...]
        r = lax.rsqrt(_rowmean(xv * xv) + EPS)
        xn = xv * r
        g, scale1 = v_ref[3:4, :], 1.0 + v_ref[1:2, :]
        st_ref[0:1, :] += jnp.sum(dh, axis=0, keepdims=True)
        dh_xn = dh * xn
        st_ref[1:2, :] += jnp.sum(dh_xn, axis=0, keepdims=True) * g
        st_ref[2:3, :] += jnp.sum(dh_xn, axis=0, keepdims=True) * scale1
        dxn = dh * (g * scale1)
        dx_ref[...] = r * (dxn - xn * _rowmean(dxn * xn)) + dr_ref[...]
        pl.when(pl.program_id(0) == n - 1)(finish)

    row = lambda w: pl.BlockSpec((tm, w), lambda i: (i, 0))
    return pl.pallas_call(
        body, name="inproj_bwd_x", grid=(n,),
        in_specs=[row(GROUP_W)] * 8 + [pl.BlockSpec((N_SHARD, D, SHARD_W), lambda i: (0, 0, 0)),
                                       row(D), pl.BlockSpec((8, D), lambda i: (0, 0)), row(D)] + ex.in_specs,
        out_specs=[row(D), pl.BlockSpec((8, D), lambda i: (0, 0))] + ex.out_specs,
        out_shape=[jax.ShapeDtypeStruct((S, D), F32), jax.ShapeDtypeStruct((8, D), F32)] + ex.out_shape,
        scratch_shapes=ex.scratch,
        compiler_params=_cp("arbitrary"),
    )(*pieces, w3, x, vecs, dx_res, *ex.ship)


def inproj_bwd_w(h, pieces, tm=512):
    S, D = h.shape
    n = S // tm

    def body(*refs):
        h_ref, p_refs, dw_ref, acc = refs[0], refs[1:9], refs[9], refs[10]
        i = pl.program_id(0)

        @pl.when(i == 0)
        def _():
            acc[...] = jnp.zeros_like(acc)

        hv = h_ref[...]
        for k, p_ref in enumerate(p_refs):
            c0 = (k % 2) * GROUP_W
            acc[k // 2, :, c0:c0 + GROUP_W] += _dot_tn(hv, p_ref[...])

        @pl.when(i == n - 1)
        def _():
            dw_ref[...] = acc[...].astype(BF16)

    row = lambda w: pl.BlockSpec((tm, w), lambda i: (i, 0))
    return pl.pallas_call(
        body, name="inproj_bwd_w", grid=(n,),
        in_specs=[row(D)] + [row(GROUP_W)] * 8,
        out_specs=pl.BlockSpec((N_SHARD, D, SHARD_W), lambda i: (0, 0, 0)),
        out_shape=jax.ShapeDtypeStruct((N_SHARD, D, SHARD_W), BF16),
        scratch_shapes=[pltpu.VMEM((N_SHARD, D, SHARD_W), F32)],
        compiler_params=_cp("arbitrary"),
    )(h, *pieces)


def layer_fwd(x, vecs, w3, w_out, tabs, gather=None):
    cosf, sinf, lgam, keep = tabs
    ret, sg, h, sb = inproj_fwd(x, vecs, w3)
    y_ret, o_ret, states = ret_fwd(ret, cosf, sinf, lgam)
    y_sb, o_sb, sb_end, *gathered = sb_fwd(sb, sg, keep, gather)
    if callable(w_out):
        w_out = w_out(gathered[0])
    y, x_next = outproj_fwd(x, vecs, y_ret, y_sb, w_out)
    saved = (x, ret, sg, h, sb, y_ret, o_ret, states, y_sb, o_sb, sb_end, y)
    return x_next, saved, (gathered[0] if gathered else None)


def _by_shard(dw_out):
    return dw_out.reshape(N_SHARD, D_MODEL // N_SHARD, D_MODEL)


def layer_bwd(dx, saved, vecs, w3, w_out, tabs, later_grads=None):
    cosf, sinf, lgam, keep = tabs
    x, ret, sg, h, sb, y_ret, o_ret, states, y_sb, o_sb, sb_end, y = saved
    dycat, dw_out, st_o = outproj_bwd(dx, y, vecs, y_ret, y_sb, w_out)
    dw_out = _by_shard(dw_out)
    ship = None if later_grads is None else (later_grads[0], dw_out, later_grads[1])
    *d_sb, = sb_bwd(sb, sg, o_sb, sb_end, keep, dycat, ship)
    d_ret = ret_bwd(ret, cosf, sinf, lgam, o_ret, states, dycat)
    pieces = list(d_ret) + d_sb[:4]
    dw_in = inproj_bwd_w(h, pieces)
    dx, st_i, *recv_in = inproj_bwd_x(pieces, w3, x, vecs, dx, None if later_grads is None else (dw_in,))
    dmod = jnp.concatenate([st_i[0:2], st_o[0:1]], axis=0)
    grads = (dw_in, dw_out) if later_grads is None else (recv_in[0], d_sb[4])
    return dx, dmod, st_i[2:3], grads


def _place():
    return lax.axis_index("x"), lax.axis_index("y"), lax.axis_index("c")


def _other_chips(mx, my):
    return [(1 - mx, my), (mx, 1 - my), (1 - mx, 1 - my)]


_ANY = pl.BlockSpec(memory_space=pl.ANY)


_GATHER_SCRATCH = [pltpu.SemaphoreType.DMA((7,)), pltpu.SemaphoreType.DMA((7,)), pltpu.SemaphoreType.DMA(())]


def _gather_ops(x_ref, out_ref, send_sems, recv_sems, local_sem):
    mx, my, mc = _place()
    me, sibling = (mx, my, mc), (mx, my, 1 - mc)
    chips = _other_chips(mx, my)

    def slot(px, py, pc):
        return out_ref.at[4 * px + 2 * py + pc]

    def copy(k, block, to, src=None):
        return pltpu.make_async_remote_copy(
            src_ref=slot(*block) if src is None else src, dst_ref=slot(*block),
            send_sem=send_sems.at[k], recv_sem=recv_sems.at[k], device_id=to, device_id_type=MESH)

    mine = pltpu.make_async_copy(x_ref, slot(*me), local_sem)
    first = [copy(0, me, sibling, src=x_ref)]
    first += [copy(1 + j, me, (*chip, mc), src=x_ref) for j, chip in enumerate(chips)]
    passed = [copy(4 + j, (*chip, mc), sibling) for j, chip in enumerate(chips)]

    def start():
        mine.start()
        for cp in first:
            cp.start()

    def forward():
        for j, chip in enumerate(chips):
            copy(1 + j, (*chip, mc), me).wait_recv()
            passed[j].start()

    def finish():
        copy(0, sibling, me).wait_recv()
        for j, chip in enumerate(chips):
            copy(4 + j, (*chip, 1 - mc), me).wait_recv()
        for cp in first + passed:
            cp.wait_send()
        mine.wait()

    return start, forward, finish


def allgather8(x, name):
    def body(x_ref, out_ref, send_sems, recv_sems, local_sem):
        for step in _gather_ops(x_ref, out_ref, send_sems, recv_sems, local_sem):
            step()

    return pl.pallas_call(
        body, name=name, out_shape=jax.ShapeDtypeStruct((8,) + x.shape, x.dtype),
        in_specs=[_ANY], out_specs=_ANY, scratch_shapes=_GATHER_SCRATCH,
    )(x)


class _Exchange:
    def __init__(self, ship):
        self.ship = list(ship or ())
        self.n_in = len(self.ship)
        self.n_out = 1 if self.ship else 0
        self.rows = [a.shape[1] for a in self.ship]
        self.in_specs = [_ANY] * self.n_in
        self.out_specs = [_ANY] * self.n_out
        self.out_shape = [jax.ShapeDtypeStruct((N_SHARD, sum(self.rows), SHARD_W), BF16)] * self.n_out
        sem = pltpu.SemaphoreType.DMA
        self.scratch = [sem((3,)), sem((3,)), sem(())] * self.n_out

    def ops(self, ship_refs, tail):
        if not self.ship:
            return (lambda: None), (lambda: None)
        recv, send_sems, recv_sems, local_sem = tail
        mx, my, mc = _place()
        my_chip = 2 * mx + my
        chips = _other_chips(mx, my)

        def pieces(s):
            firsts = np.cumsum([0] + self.rows[:-1])
            return [(ref.at[s], int(r0), n) for ref, r0, n in zip(ship_refs, firsts, self.rows)]

        def start():
            for src, r0, n in pieces(my_chip):
                pltpu.make_async_copy(src, recv.at[my_chip, pl.ds(r0, n)], local_sem).start()
            for j, (px, py) in enumerate(chips):
                for src, r0, n in pieces(2 * px + py):
                    pltpu.make_async_remote_copy(
                        src_ref=src, dst_ref=recv.at[my_chip, pl.ds(r0, n)],
                        send_sem=send_sems.at[j], recv_sem=recv_sems.at[j],
                        device_id=(px, py, mc), device_id_type=MESH).start()

        def finish():
            for j, (px, py) in enumerate(chips):
                whole = recv.at[2 * px + py]
                both = pltpu.make_async_remote_copy(
                    src_ref=whole, dst_ref=whole, send_sem=send_sems.at[j], recv_sem=recv_sems.at[j],
                    device_id=(px, py, mc), device_id_type=MESH)
                both.wait_recv()
                both.wait_send()
            pltpu.make_async_copy(recv.at[my_chip], recv.at[my_chip], local_sem).wait()

        return start, finish


def sum_slots(recv_a, recv_b, tr=256):
    n, rows_a, cols = recv_a.shape
    na, nb = rows_a // tr, recv_b.shape[1] // tr

    def body(a_ref, b_ref, o_ref):
        def total(r_ref):
            acc = r_ref[0].astype(F32)
            for k in range(1, n):
                acc = acc + r_ref[k].astype(F32)
            o_ref[...] = acc

        pl.when(pl.program_id(0) < na)(lambda: total(a_ref))
        pl.when(pl.program_id(0) >= na)(lambda: total(b_ref))

    return pl.pallas_call(
        body, name="sum_slots", grid=(na + nb,),
        in_specs=[pl.BlockSpec((n, tr, cols), lambda i: (0, jnp.minimum(i, na - 1), 0)),
                  pl.BlockSpec((n, tr, cols), lambda i: (0, jnp.maximum(i - na, 0), 0))],
        out_specs=pl.BlockSpec((tr, cols), lambda i: (i, 0)),
        out_shape=jax.ShapeDtypeStruct(((na + nb) * tr, cols), F32),
        compiler_params=_cp("arbitrary"),
    )(recv_a, recv_b)


def swap_sibling(p):
    def body(p_ref, out_ref, send_sem, recv_sem):
        mx, my, mc = _place()
        cp = pltpu.make_async_remote_copy(
            src_ref=p_ref, dst_ref=out_ref, send_sem=send_sem, recv_sem=recv_sem,
            device_id=(mx, my, 1 - mc), device_id_type=MESH)
        cp.start()
        cp.wait()

    return pl.pallas_call(
        body, name="swap_sibling", out_shape=jax.ShapeDtypeStruct(p.shape, p.dtype),
        in_specs=[_ANY], out_specs=_ANY,
        scratch_shapes=[pltpu.SemaphoreType.DMA(()), pltpu.SemaphoreType.DMA(())],
    )(p)


def _adamw(w, g, m, v):
    m = ADAM_B1 * m + (1.0 - ADAM_B1) * g
    v = ADAM_B2 * v + (1.0 - ADAM_B2) * (g * g)
    m_hat = m / (1.0 - ADAM_B1 ** ADAM_STEP)
    v_hat = v / (1.0 - ADAM_B2 ** ADAM_STEP)
    delta = -ADAM_LR * (m_hat / (jnp.sqrt(v_hat) + ADAM_EPS) + ADAM_WD * w)
    return delta, m, v


def adam_slab(p_own, p_sib, w, m, v, row0, name, tr=256):
    L, R, C = w.shape
    nr = R // tr

    def body(a_ref, b_ref, w_ref, m_ref, v_ref, g_out, d_out, m_out, v_out):
        g = a_ref[...] + b_ref[...]
        d, m2, v2 = _adamw(w_ref[0], g, m_ref[0], v_ref[0])
        g_out[0], d_out[0], m_out[0], v_out[0] = g, d, m2, v2

    slab = pl.BlockSpec((tr, C), lambda l, i: (row0 // tr + l * nr + i, 0))
    blk = pl.BlockSpec((1, tr, C), lambda l, i: (l, i, 0))
    return pl.pallas_call(
        body, name=name, grid=(L, nr),
        in_specs=[slab, slab, blk, blk, blk], out_specs=[blk] * 4,
        out_shape=[jax.ShapeDtypeStruct(w.shape, F32)] * 4,
        compiler_params=_cp("arbitrary", "arbitrary"),
    )(p_own, p_sib, w, m, v)


def ada_fwd(c_all, w_ada):
    L, D, W = w_ada.shape

    def body(c_ref, w_ref, o_ref):
        cv = c_ref[...]
        o_ref[0] = jnp.dot(cv * _sigmoid(cv), w_ref[0], precision=lax.Precision.HIGHEST,
                           preferred_element_type=F32)

    return pl.pallas_call(
        body, name="ada_fwd", grid=(L,),
        in_specs=[pl.BlockSpec((8, D), lambda l: (0, 0)), pl.BlockSpec((1, D, W), lambda l: (l, 0, 0))],
        out_specs=pl.BlockSpec((1, 8, W), lambda l: (l, 0, 0)),
        out_shape=jax.ShapeDtypeStruct((L, 8, W), F32),
        compiler_params=_cp("arbitrary"),
    )(c_all, w_ada)


def vecs_build(mod_all, b_ada, norm_g):
    W = mod_all.shape[2]

    def body(m_ref, b_ref, g_ref, o_ref):
        mx, my, mc = _place()
        me = 4 * mx + 2 * my + mc
        rowid = lax.broadcasted_iota(jnp.int32, (2 * 8, 1), 0)
        o_ref[...] = jnp.zeros_like(o_ref)
        for l in range(DEPTH):
            parts = [jnp.sum(jnp.where(rowid == l * 8 + me, m_ref[2 * s + mc], 0.0), axis=0, keepdims=True)
                     for s in range(N_SHARD)]
            mod = jnp.concatenate(parts, axis=1) + b_ref[l:l + 1, :]
            for t in range(3):
                o_ref[l, t:t + 1, :] = mod[:, t * D_MODEL:(t + 1) * D_MODEL]
            o_ref[l, 3:4, :] = g_ref[l:l + 1, :]

    return pl.pallas_call(
        body, name="vecs_build", out_shape=jax.ShapeDtypeStruct((DEPTH, 8, D_MODEL), F32),
    )(mod_all, b_ada, norm_g)


def ada_update(dmods, c_t, w, m, v, tr=256):
    L, D, W = w.shape

    def body(dm_ref, c_ref, w_ref, m_ref, v_ref, g_out, d_out, m_out, v_out):
        mx, my, _ = _place()
        shard = 2 * mx + my
        dm = jnp.zeros((8, W), F32)
        for s in range(N_SHARD):
            dm = dm + jnp.where(shard == s, dm_ref[0, :, s * W:(s + 1) * W], 0.0)
        cv = c_ref[...]
        ca = cv * _sigmoid(cv)
        g = jnp.zeros((tr, W), F32)
        for b in range(8):
            g = g + ca[:, b:b + 1] * dm[b:b + 1, :]
        d, m2, v2 = _adamw(w_ref[0], g, m_ref[0], v_ref[0])
        g_out[0], d_out[0], m_out[0], v_out[0] = g, d, m2, v2

    blk = pl.BlockSpec((1, tr, W), lambda l, i: (l, i, 0))
    return pl.pallas_call(
        body, name="ada_update", grid=(L, D // tr),
        in_specs=[pl.BlockSpec((1, 8, 3 * D), lambda l, i: (l, 0, 0)), pl.BlockSpec((tr, 8), lambda l, i: (i, 0)),
                  blk, blk, blk],
        out_specs=[blk] * 4, out_shape=[jax.ShapeDtypeStruct(w.shape, F32)] * 4,
        compiler_params=_cp("arbitrary", "arbitrary"),
    )(dmods, c_t, w, m, v)


STAT_ROWS = 16


def small_update(stats_all, norm, b_ada, final):
    def body(s_ref, *refs):
        ins, outs = refs[:9], refs[9:]
        tot = s_ref[0]
        for k in range(1, 8):
            tot = tot + s_ref[k]
        g_norm = tot[0:2, :]
        g_final = tot[2:3, :]
        g_b = jnp.concatenate(
            [jnp.concatenate([tot[3 + 3 * l + t:4 + 3 * l + t, :] for t in range(3)], axis=1) for l in range(DEPTH)],
            axis=0)
        for p, g in enumerate((g_norm, g_b, g_final)):
            w_ref, m_ref, v_ref = ins[3 * p:3 * p + 3]
            d, m2, v2 = _adamw(w_ref[...], g, m_ref[...], v_ref[...])
            for o_ref, val in zip(outs[4 * p:4 * p + 4], (g, d, m2, v2)):
                o_ref[...] = val
        loss = (0.5 / D_MODEL) * jnp.sum(tot[9:10, :], axis=1, keepdims=True)
        outs[12][...] = jnp.broadcast_to(loss, (8, LANES))

    shapes = []
    for w, _, _ in (norm, b_ada, final):
        shapes += [jax.ShapeDtypeStruct(w.shape, F32)] * 4
    shapes.append(jax.ShapeDtypeStruct((8, LANES), F32))
    return pl.pallas_call(body, name="small_update", out_shape=shapes)(stats_all, *norm, *b_ada, *final)


def kernel(x, c, norm_g, w_ada, b_ada, w_in, w_out, final_g, loss_target, m_norm_g, m_w_ada, m_b_ada, m_w_in, m_w_out, m_final_g, v_norm_g, v_w_ada, v_b_ada, v_w_in, v_w_out, v_final_g):
    S, D = x.shape[1], x.shape[2]
    mc = lax.axis_index("c")
    out_rows = D // N_SHARD

    def my_half(a, rows):
        return lax.dynamic_slice_in_dim(a, mc * rows, rows, axis=0)

    assert DEPTH == 2
    win = [my_half(w_in[l], D // 2).astype(BF16) for l in range(DEPTH)]
    wout = [my_half(w_out[l], out_rows // 2).astype(BF16) for l in range(DEPTH)]
    w3_first = allgather8(win[0], "gather_weights").reshape(N_SHARD, D, SHARD_W)
    rest = jnp.concatenate([wout[0], win[1], wout[1]], axis=0)

    def unpack(wall):
        wall = wall.reshape(N_SHARD, 2, rest.shape[0], SHARD_W)
        a, b = out_rows // 2, out_rows // 2 + D // 2
        return wall[:, :, :a].reshape(D, D), (wall[:, :, a:b].reshape(N_SHARD, D, SHARD_W), wall[:, :, b:].reshape(D, D))

    c_all = allgather8(jnp.broadcast_to(c, (8, D)), "gather_c")[:, 0, :]
    mod_all = allgather8(ada_fwd(c_all, w_ada).reshape(DEPTH * 8, -1), "gather_mod")
    vecs = vecs_build(mod_all, b_ada, norm_g)

    tabs = (*rope_tables(S), ret_log_gamma(), sb_keep_masks())
    saved = [None] * DEPTH
    h, saved[0], wall = layer_fwd(x[0], vecs[0], w3_first, lambda g: unpack(g)[0], tabs, rest)
    weights = [(w3_first, unpack(wall)[0]), unpack(wall)[1]]
    h, saved[1], _ = layer_fwd(h, vecs[1], *weights[1], tabs)
    dx, st_loss = loss_head(h, jnp.broadcast_to(final_g[None, :], (8, D)), loss_target[0])

    dmod, dnorm, grads = [None] * DEPTH, [None] * DEPTH, None
    for l in reversed(range(DEPTH)):
        dx, dmod[l], dnorm[l], grads = layer_bwd(dx, saved[l], vecs[l], *weights[l], tabs, grads)

    p_own = sum_slots(*grads)
    p_sib = swap_sibling(p_own)
    res_in = adam_slab(p_own, p_sib, w_in, m_w_in, v_w_in, 0, "adam_w_in")
    res_out = adam_slab(p_own, p_sib, w_out, m_w_out, v_w_out, DEPTH * D, "adam_w_out", tr=128)

    stats = jnp.concatenate(dnorm + [st_loss[0:1]] + dmod + [st_loss[1:2], jnp.zeros((STAT_ROWS - 10, D), F32)], axis=0)
    stats_all = allgather8(stats, "gather_stats")
    dmods = stats_all[:, 3:9, :].reshape(8, DEPTH, 3 * D).transpose(1, 0, 2)
    res_ada = ada_update(dmods, c_all.T, w_ada, m_w_ada, v_w_ada)
    small = small_update(stats_all, (norm_g, m_norm_g, v_norm_g), (b_ada, m_b_ada, v_b_ada),
                         (final_g[None, :], m_final_g[None, :], v_final_g[None, :]))
    res_norm, res_b, res_final = small[0:4], small[4:8], [a[0] for a in small[8:12]]
    loss = small[12][0, 0]

    by_kind = [res_norm, res_ada, res_b, res_in, res_out, res_final]
    outs = [loss, dx[None]]
    for kind in range(4):
        outs += [r[kind] for r in by_kind]
    return tuple(outs)
```

```python
import functools

import numpy as np
import jax
import jax.numpy as jnp
from jax import lax
from jax.experimental import pallas as pl
from jax.experimental.pallas import tpu as pltpu

F32, BF16 = jnp.float32, jnp.bfloat16
MESH = pl.DeviceIdType.MESH

D_MODEL = 1024
DEPTH = 2
SHARD_W = 1024
N_SHARD = 4
GROUP_W = 512
LANES = 128
SB_HEAD_DIM = 64
RET_HEAD_DIM = 128
CHUNK = 64
ROPE_BASE = 10000.0
EPS = 1e-6
SQ_SCALE = SB_HEAD_DIM ** -0.5
RK_SCALE = RET_HEAD_DIM ** -0.5
SB_T = 256
RET_T = 256
EXP_ZERO = -104.0
VMEM_LIMIT_BYTES = 56 * 2 ** 20

ADAM_LR, ADAM_B1, ADAM_B2, ADAM_EPS, ADAM_WD, ADAM_STEP = 0.001, 0.9, 0.999, 1e-08, 0.01, 10


def _cp(*sem):
    return pltpu.CompilerParams(dimension_semantics=sem, vmem_limit_bytes=VMEM_LIMIT_BYTES)


def _dot(a, b):
    return lax.dot_general(a, b, (((1,), (0,)), ((), ())), preferred_element_type=F32)


def _dot_nt(a, b):
    return lax.dot_general(a, b, (((1,), (1,)), ((), ())), preferred_element_type=F32)


def _dot_tn(a, b):
    return lax.dot_general(a, b, (((0,), (0,)), ((), ())), preferred_element_type=F32)


def _running_sum(a, tri):
    return _dot(a.astype(BF16), tri)


def _sigmoid(x):
    return 1.0 / (1.0 + jnp.exp(-x))


def _rowsum(a):
    return jnp.sum(a, axis=1, keepdims=True)


def _rowmean(a):
    return jnp.mean(a, axis=1, keepdims=True)


def inproj_fwd(x, vecs, w3, tm=512):
    S, D = x.shape

    def body(x_ref, v_ref, w_ref, ret_ref, sg_ref, h_ref, sb_ref):
        xv = x_ref[...]
        r = lax.rsqrt(_rowmean(xv * xv) + EPS)
        h = xv * r * v_ref[3:4, :] * (1.0 + v_ref[1:2, :]) + v_ref[0:1, :]
        hb = h.astype(BF16)
        h_ref[...] = hb
        for s in range(N_SHARD):
            p = _dot(hb, w_ref[s])
            if s < 2:
                ret_ref[:, s * SHARD_W:(s + 1) * SHARD_W] = p
            if s == 2:
                sb_ref[:, 0:GROUP_W] = (p[:, 0:GROUP_W] * SQ_SCALE).astype(BF16)
                sb_ref[:, GROUP_W:SHARD_W] = p[:, GROUP_W:].astype(BF16)
            if s == 3:
                sb_ref[:, SHARD_W:SHARD_W + GROUP_W] = p[:, 0:GROUP_W].astype(BF16)
                sg_ref[...] = p[:, GROUP_W:]

    row = lambda w: pl.BlockSpec((tm, w), lambda i: (i, 0))
    return pl.pallas_call(
        body, name="inproj_fwd", grid=(S // tm,),
        in_specs=[row(D), pl.BlockSpec((8, D), lambda i: (0, 0)),
                  pl.BlockSpec((N_SHARD, D, SHARD_W), lambda i: (0, 0, 0))],
        out_specs=[row(2 * SHARD_W), row(GROUP_W), row(D), row(3 * GROUP_W)],
        out_shape=[jax.ShapeDtypeStruct((S, 2 * SHARD_W), F32), jax.ShapeDtypeStruct((S, GROUP_W), F32),
                   jax.ShapeDtypeStruct((S, D), BF16), jax.ShapeDtypeStruct((S, 3 * GROUP_W), BF16)],
        compiler_params=_cp("arbitrary"),
    )(x, vecs, w3)


def _sb_logits(qh, k2, keep):
    z = _dot_nt(qh, k2)
    sp = jnp.log(1.0 + jnp.exp(-jnp.abs(z)))
    lb = jnp.minimum(z, 0.0) - sp
    lk = lb - z
    if keep is not None:
        lk = jnp.where(keep, lk, 0.0)
    return lb, lk


def _sb_masks(i):
    T = SB_T
    first = jnp.maximum(i - 1, 0)
    row = lax.broadcasted_iota(jnp.int32, (2 * T, 2 * T), 0)
    col = lax.broadcasted_iota(jnp.int32, (2 * T, 2 * T), 1)
    keep = first * T + col < i * T + (row & (T - 1))
    r = lax.broadcasted_iota(jnp.int32, (T, T), 0)
    c = lax.broadcasted_iota(jnp.int32, (T, T), 1)
    later = jnp.where(r > c, 1.0, 0.0).astype(BF16)
    earlier = jnp.where(r < c, 1.0, 0.0).astype(BF16)
    lane = lax.broadcasted_iota(jnp.int32, (1, LANES), 1)
    return first, keep, later, earlier, lane < SB_HEAD_DIM


def _sb_rows(ref, j, tiles=1):
    return ref[pl.ds(pl.multiple_of(j * SB_T, SB_T), tiles * SB_T), :]


def sb_fwd(sb, sg, gather=None):
    S = sb.shape[0]
    T = SB_T
    nq = S // T
    carried = [] if gather is None else [gather]

    def body(*refs):
        (q_ref, k_ref, v_ref, sg_ref), refs = refs[:4], refs[4:]
        p, i = pl.program_id(0), pl.program_id(1)
        if carried:
            x_ref, y_ref, o_ref, end_ref, out_ref, send_sems, recv_sems, local_sem = refs
            start, forward, finish = _gather_ops(x_ref, out_ref, send_sems, recv_sems, local_sem)
            pl.when(jnp.logical_and(p == 0, i == 0))(start)
            pl.when(jnp.logical_and(p == 2, i == 0))(forward)
        else:
            y_ref, o_ref, end_ref = refs
        first, keep, later, _, head0 = _sb_masks(i)
        q2 = q_ref[...]
        zero = jnp.zeros_like(q2)
        qs = jnp.concatenate([jnp.where(head0, q2, zero), jnp.where(head0, zero, q2)], axis=0)

        lb, lk = _sb_logits(qs, _sb_rows(k_ref, first, 2), keep)
        rs_left, rs_right = _rowsum(lk[:, :T]), _rowsum(lk[:, T:])
        suffix = jnp.concatenate([_running_sum(lk[:, :T], later) + rs_right, _running_sum(lk[:, T:], later)], axis=1)
        a = jnp.where(keep, jnp.exp(lb + suffix), 0.0)
        acc = _dot(a.astype(BF16), _sb_rows(v_ref, first, 2))
        R = rs_left + rs_right

        def tile(j, R):
            lb, lk = _sb_logits(qs, _sb_rows(k_ref, j), None)
            a = jnp.exp(lb + _running_sum(lk, later) + R)
            return _dot(a.astype(BF16), _sb_rows(v_ref, j)), R + _rowsum(lk)

        def cond(st):
            return jnp.logical_and(st[0] >= 0, st[3] > EXP_ZERO)

        def step(st):
            c, Rn = tile(st[0], st[2])
            return st[0] - 1, st[1] + c, Rn, jnp.max(Rn)

        j_end, acc, R, _ = lax.while_loop(cond, step, (first - 1, acc, R, jnp.max(R)))
        end_ref[0, 0, 0:2 * T, :] = jnp.broadcast_to(R, (2 * T, 8))
        end_ref[0, 0, 2 * T:, :] = jnp.full((8, 8), j_end.astype(F32))
        o = jnp.where(head0, acc[:T], acc[T:])
        o_ref[...] = o
        sg = sg_ref[...]
        y_ref[...] = (o * (sg * _sigmoid(sg))).astype(BF16)
        if carried:
            pl.when(jnp.logical_and(p == 3, i == nq - 1))(finish)

    return pl.pallas_call(
        body, name="sb_fwd", grid=(4, nq),
        in_specs=[pl.BlockSpec((T, LANES), lambda p, i: (i, p)),
                  pl.BlockSpec((S, LANES), lambda p, i: (0, 4 + p)),
                  pl.BlockSpec((S, LANES), lambda p, i: (0, 8 + p)),
                  pl.BlockSpec((T, LANES), lambda p, i: (i, p))] + [_ANY for _ in carried],
        out_specs=[pl.BlockSpec((T, LANES), lambda p, i: (i, p)),
                   pl.BlockSpec((T, LANES), lambda p, i: (i, p)),
                   pl.BlockSpec((1, 1, 2 * T + 8, 8), lambda p, i: (p, i, 0, 0))] + [_ANY for _ in carried],
        out_shape=[jax.ShapeDtypeStruct((S, GROUP_W), BF16),
                   jax.ShapeDtypeStruct((S, GROUP_W), F32),
                   jax.ShapeDtypeStruct((4, nq, 2 * T + 8, 8), F32)]
        + [jax.ShapeDtypeStruct((8,) + a.shape, a.dtype) for a in carried],
        scratch_shapes=_GATHER_SCRATCH if carried else [],
        compiler_params=_cp("arbitrary", "arbitrary"),
    )(sb, sb, sb, sg, *carried)


def sb_bwd(sb, sg, o, sb_end, dycat, ship=None):
    S = sb.shape[0]
    T = SB_T
    nq = S // T
    ex = _Exchange(ship)

    def body(*refs):
        (q_ref, k_ref, v_ref, sg_ref, o_ref, dy_ref, end_ref), refs = refs[:7], refs[7:]
        ship_refs, (dq_ref, dk_ref, dv_ref, dsg_ref), refs = refs[:ex.n_in], refs[ex.n_in:ex.n_in + 4], refs[ex.n_in + 4:]
        recv, (dk_acc, dv_acc), sems = refs[:ex.n_out], refs[ex.n_out:ex.n_out + 2], refs[ex.n_out + 2:]
        start, finish = ex.ops(ship_refs, recv + sems)
        p, i = pl.program_id(0), pl.program_id(1)
        pl.when(jnp.logical_and(p == 0, i == 0))(start)

        @pl.when(i == 0)
        def _():
            dk_acc[...] = jnp.zeros_like(dk_acc)
            dv_acc[...] = jnp.zeros_like(dv_acc)

        first, keep, later, earlier, head0 = _sb_masks(i)
        q2 = q_ref[...]
        sg = sg_ref[...]
        sig = _sigmoid(sg)
        dy = dy_ref[...]
        dsg_ref[...] = (dy * o_ref[...] * (sig * (1.0 + sg * (1.0 - sig)))).astype(BF16)
        do_b = (dy * (sg * sig)).astype(BF16)
        zero = jnp.zeros_like(q2)
        qs = jnp.concatenate([jnp.where(head0, q2, zero), jnp.where(head0, zero, q2)], axis=0)
        dos = jnp.concatenate([jnp.where(head0, do_b, zero), jnp.where(head0, zero, do_b)], axis=0)

        end = end_ref[0, 0]
        j_end = jnp.max(end[2 * T:, :]).astype(jnp.int32)

        def grads(j, tiles, a, lb, g, G, keep):
            dz = g - jnp.exp(lb) * (g + G)
            if keep is not None:
                dz = jnp.where(keep, dz, 0.0)
            dzb = dz.astype(BF16)
            rows = pl.ds(pl.multiple_of(j * T, T), tiles * T)
            dk_acc[rows, :] += _dot_tn(dzb, qs)
            dv_acc[rows, :] += _dot_tn(a.astype(BF16), dos)
            return _dot(dzb, _sb_rows(k_ref, j, tiles))

        def sweep(j, st):
            dq, G0, left = st
            lb, lk = _sb_logits(qs, _sb_rows(k_ref, j), None)
            stick = left - _rowsum(lk)
            a = jnp.exp(lb + _running_sum(lk, later) + stick)
            g = a * _dot_nt(dos, _sb_rows(v_ref, j))
            G = _running_sum(g, earlier) + G0
            return dq + grads(j, 1, a, lb, g, G, None), G0 + _rowsum(g), stick

        dq, G0, _ = lax.fori_loop(j_end + 1, first, sweep,
                                  (jnp.zeros((2 * T, LANES), F32), jnp.zeros((2 * T, 1), F32), end[:2 * T, 0:1]))

        lb, lk = _sb_logits(qs, _sb_rows(k_ref, first, 2), keep)
        suffix = jnp.concatenate([_running_sum(lk[:, :T], later) + _rowsum(lk[:, T:]),
                                  _running_sum(lk[:, T:], later)], axis=1)
        a = jnp.where(keep, jnp.exp(lb + suffix), 0.0)
        g = a * _dot_nt(dos, _sb_rows(v_ref, first, 2))
        G = jnp.concatenate([_running_sum(g[:, :T], earlier) + G0,
                             _running_sum(g[:, T:], earlier) + (G0 + _rowsum(g[:, :T]))], axis=1)
        dq = dq + grads(first, 2, a, lb, g, G, keep)
        dq_ref[...] = (jnp.where(head0, dq[:T], dq[T:]) * SQ_SCALE).astype(BF16)

        @pl.when(i == nq - 1)
        def _():
            dk_ref[...] = dk_acc[...].astype(BF16)
            dv_ref[...] = dv_acc[...].astype(BF16)

        pl.when(jnp.logical_and(p == 3, i == nq - 1))(finish)

    tile_spec = lambda c0: pl.BlockSpec((T, LANES), lambda p, i: (i, c0 + p))
    head_spec = lambda c0: pl.BlockSpec((S, LANES), lambda p, i: (0, c0 + p))
    return pl.pallas_call(
        body, name="sb_bwd", grid=(4, nq),
        in_specs=[tile_spec(0), head_spec(4), head_spec(8), tile_spec(0), tile_spec(0), tile_spec(4),
                  pl.BlockSpec((1, 1, 2 * T + 8, 8), lambda p, i: (p, i, 0, 0))] + ex.in_specs,
        out_specs=[tile_spec(0), head_spec(0), head_spec(0), tile_spec(0)] + ex.out_specs,
        out_shape=[jax.ShapeDtypeStruct((S, GROUP_W), BF16)] * 4 + ex.out_shape,
        scratch_shapes=[pltpu.VMEM((S, LANES), F32), pltpu.VMEM((S, LANES), F32)] + ex.scratch,
        compiler_params=_cp("arbitrary", "arbitrary"),
    )(sb, sb, sb, sg, o, dycat, sb_end, *ex.ship)


def rope_tables(S):
    half = RET_HEAD_DIM // 2
    inv = ROPE_BASE ** (-jnp.arange(half, dtype=F32) / half)
    ang = jnp.arange(S, dtype=F32)[:, None] * inv[None, :]
    cos, sin = jnp.cos(ang), jnp.sin(ang)
    return jnp.concatenate([cos, cos], axis=1), jnp.concatenate([-sin, sin], axis=1)


def ret_log_gamma():
    return jnp.log1p(-(2.0 ** (-5.0 - jnp.arange(4, dtype=F32))))


def _swap_halves(a):
    return pltpu.roll(a, RET_HEAD_DIM // 2, axis=1)


def _ret_decay_mask(lg):
    n = lax.broadcasted_iota(jnp.int32, (RET_T, RET_T), 0)
    m = lax.broadcasted_iota(jnp.int32, (RET_T, RET_T), 1)
    dist = jnp.abs(n - m).astype(F32)
    return jnp.where((m // CHUNK) <= (n // CHUNK), jnp.exp(lg * dist), 0.0)


def _ret_block(lg, rq, rk, rv, cosf, sinf, dm):
    q = rq * cosf + _swap_halves(rq) * sinf
    k = (rk * cosf + _swap_halves(rk) * sinf) * RK_SCALE
    qb, kb, vb = q.astype(BF16), k.astype(BF16), rv.astype(BF16)
    sc = _dot_nt(qb, kb) * dm
    nloc = lax.broadcasted_iota(jnp.int32, (RET_T, 1), 0).astype(F32)
    qdec = jnp.exp(lg * (nloc + 1.0))
    kdec = jnp.exp(lg * (RET_T - 1.0 - nloc))
    block_dec = jnp.exp(jnp.full((1, LANES), lg * RET_T, F32))
    return q, k, qb, kb, vb, sc, qdec, kdec, block_dec


def _ret_specs(S, rb):
    group = lambda c0: pl.BlockSpec((RET_T, GROUP_W), lambda b: (rb(b), c0))
    return group, pl.BlockSpec((RET_T, LANES), lambda b: (rb(b), 0))


def _head(ref, h):
    return ref[:, h * LANES:(h + 1) * LANES]


def ret_fwd(proj, cosf, sinf, lgam):
    S = proj.shape[0]
    nb = S // RET_T
    group, row_tab = _ret_specs(S, lambda b: b)

    def body(lg_ref, rq_ref, rk_ref, rv_ref, rg_ref, cos_ref, sin_ref, y_ref, o_ref, st_out, st_ref, dm_ref):
        @pl.when(pl.program_id(0) == 0)
        def _():
            st_ref[...] = jnp.zeros_like(st_ref)
            for h in range(4):
                dm_ref[h] = _ret_decay_mask(lg_ref[h])

        cosf, sinf = cos_ref[...], sin_ref[...]
        H = range(4)
        lanes = [slice(h * LANES, (h + 1) * LANES) for h in H]
        blk = [_ret_block(lg_ref[h], _head(rq_ref, h), _head(rk_ref, h), _head(rv_ref, h), cosf, sinf, dm_ref[h])
               for h in H]
        q, k, qb, kb, vb, sc, qdec, kdec, block_dec = zip(*blk)
        st = [st_ref[h] for h in H]
        for h in H:
            st_out[h, 0] = st[h]
        o = [_dot(sc[h].astype(BF16), vb[h]) + _dot(qb[h], st[h].astype(BF16)) * qdec[h] for h in H]
        for h in H:
            st_ref[h] = st[h] * block_dec[h] + _dot_tn((k[h] * kdec[h]).astype(BF16), vb[h])
        for h in H:
            o_ref[:, lanes[h]] = o[h]
        cen = [o[h] - _rowmean(o[h]) for h in H]
        on = [cen[h] * lax.rsqrt(_rowmean(cen[h] * cen[h]) + EPS) for h in H]
        rg = [_head(rg_ref, h) for h in H]
        for h in H:
            y_ref[:, lanes[h]] = (on[h] * (rg[h] * _sigmoid(rg[h]))).astype(BF16)

    return pl.pallas_call(
        body, name="ret_fwd", grid=(nb,),
        in_specs=[pl.BlockSpec(memory_space=pltpu.SMEM),
                  group(0), group(1), group(2), group(3), row_tab, row_tab],
        out_specs=[group(0), group(0),
                   pl.BlockSpec((4, 1, LANES, LANES), lambda b: (0, b, 0, 0))],
        out_shape=[jax.ShapeDtypeStruct((S, GROUP_W), BF16),
                   jax.ShapeDtypeStruct((S, GROUP_W), F32),
                   jax.ShapeDtypeStruct((4, nb, LANES, LANES), F32)],
        scratch_shapes=[pltpu.VMEM((4, LANES, LANES), F32), pltpu.VMEM((4, RET_T, RET_T), F32)],
        compiler_params=_cp("arbitrary"),
    )(lgam, proj, proj, proj, proj, cosf, sinf)


def ret_bwd(proj, cosf, sinf, lgam, o, states, dycat):
    S = proj.shape[0]
    nb = S // RET_T
    rev = lambda b: nb - 1 - b
    group, row_tab = _ret_specs(S, rev)

    def body(lg_ref, rq_ref, rk_ref, rv_ref, rg_ref, cos_ref, sin_ref, o_ref, st_in, dy_ref,
             drq_ref, drk_ref, drv_ref, drg_ref, ds_ref, dm_ref):
        @pl.when(pl.program_id(0) == 0)
        def _():
            ds_ref[...] = jnp.zeros_like(ds_ref)
            for h in range(4):
                dm_ref[h] = _ret_decay_mask(lg_ref[h])

        cosf, sinf = cos_ref[...], sin_ref[...]
        H = range(4)
        lanes = [slice(h * LANES, (h + 1) * LANES) for h in H]
        dms = [dm_ref[h] for h in H]
        blk = [_ret_block(lg_ref[h], _head(rq_ref, h), _head(rk_ref, h), _head(rv_ref, h), cosf, sinf, dms[h])
               for h in H]
        q, k, qb, kb, vb, sc, qdec, kdec, block_dec = zip(*blk)
        o_v = [_head(o_ref, h) for h in H]
        cen = [o_v[h] - _rowmean(o_v[h]) for h in H]
        rstd = [lax.rsqrt(_rowmean(cen[h] * cen[h]) + EPS) for h in H]
        on = [cen[h] * rstd[h] for h in H]
        rg = [_head(rg_ref, h) for h in H]
        sig = [_sigmoid(rg[h]) for h in H]
        dy = [_head(dy_ref, h) for h in H]
        for h in H:
            drg_ref[:, lanes[h]] = (dy[h] * on[h] * (sig[h] * (1.0 + rg[h] * (1.0 - sig[h])))).astype(BF16)
        don = [dy[h] * (rg[h] * sig[h]) for h in H]
        do = [rstd[h] * (don[h] - _rowmean(don[h]) - on[h] * _rowmean(don[h] * on[h])) for h in H]
        dob = [do[h].astype(BF16) for h in H]
        dsc = [(_dot_nt(dob[h], vb[h]) * dms[h]).astype(BF16) for h in H]
        st_b = [st_in[h, 0].astype(BF16) for h in H]
        dsn = [ds_ref[h] for h in H]
        dsn_b = [dsn[h].astype(BF16) for h in H]
        dq = [_dot(dsc[h], kb[h]) + _dot_nt(dob[h], st_b[h]) * qdec[h] for h in H]
        dk = [(_dot_tn(dsc[h], qb[h]) + _dot_nt(vb[h], dsn_b[h]) * kdec[h]) * RK_SCALE for h in H]
        dv = [_dot_tn(sc[h].astype(BF16), dob[h]) + _dot((k[h] * kdec[h]).astype(BF16), dsn_b[h]) for h in H]
        for h in H:
            ds_ref[h] = dsn[h] * block_dec[h] + _dot_tn((q[h] * qdec[h]).astype(BF16), dob[h])
        for h in H:
            drq_ref[:, lanes[h]] = (dq[h] * cosf + _swap_halves(dq[h] * sinf)).astype(BF16)
            drk_ref[:, lanes[h]] = (dk[h] * cosf + _swap_halves(dk[h] * sinf)).astype(BF16)
            drv_ref[:, lanes[h]] = dv[h].astype(BF16)

    return pl.pallas_call(
        body, name="ret_bwd", grid=(nb,),
        in_specs=[pl.BlockSpec(memory_space=pltpu.SMEM),
                  group(0), group(1), group(2), group(3), row_tab, row_tab,
                  group(0), pl.BlockSpec((4, 1, LANES, LANES), lambda b: (0, rev(b), 0, 0)),
                  group(0)],
        out_specs=[group(0)] * 4,
        out_shape=[jax.ShapeDtypeStruct((S, GROUP_W), BF16)] * 4,
        scratch_shapes=[pltpu.VMEM((4, LANES, LANES), F32), pltpu.VMEM((4, RET_T, RET_T), F32)],
        compiler_params=_cp("arbitrary"),
    )(lgam, proj, proj, proj, proj, cosf, sinf, o, states, dycat)


def outproj_fwd(x, vecs, y_ret, y_sb, w_out, head=None, tm=1024):
    S, D = x.shape
    tm = min(tm, S)
    last = list(head or ())

    def body(x_ref, v_ref, yr_ref, ys_ref, w_ref, *refs):
        y = _dot(yr_ref[...], w_ref[0:GROUP_W, :]) + _dot(ys_ref[...], w_ref[GROUP_W:, :])
        xv = x_ref[...] + v_ref[2:3, :] * y
        if not last:
            y_ref, xo_ref = refs
            y_ref[...] = y.astype(BF16)
            xo_ref[...] = xv
            return
        g_ref, t_ref, y_ref, dx_ref, st_ref = refs
        y_ref[...] = y.astype(BF16)

        @pl.when(pl.program_id(0) == 0)
        def _():
            st_ref[...] = jnp.zeros_like(st_ref)

        g = g_ref[0:1, :]
        r = lax.rsqrt(_rowmean(xv * xv) + EPS)
        xn = xv * r
        err = xn * g - t_ref[...]
        dy = err * (1.0 / D)
        dxn = dy * g
        dx_ref[...] = r * (dxn - xn * _rowmean(dxn * xn))
        st_ref[0:1, :] += jnp.sum(dy * xn, axis=0, keepdims=True)
        st_ref[1:2, :] += jnp.sum(err * err, axis=0, keepdims=True)

    row = lambda w: pl.BlockSpec((tm, w), lambda i: (i, 0))
    fixed = pl.BlockSpec((8, D), lambda i: (0, 0))
    return pl.pallas_call(
        body, name="outproj_fwd", grid=(S // tm,),
        in_specs=[row(D), fixed, row(GROUP_W), row(GROUP_W), pl.BlockSpec((D, D), lambda i: (0, 0))]
        + ([fixed, row(D)] if last else []),
        out_specs=[row(D), row(D)] + ([fixed] if last else []),
        out_shape=[jax.ShapeDtypeStruct((S, D), BF16), jax.ShapeDtypeStruct((S, D), F32)]
        + ([jax.ShapeDtypeStruct((8, D), F32)] if last else []),
        compiler_params=_cp("arbitrary"),
    )(x, vecs, y_ret, y_sb, w_out, *last)


def outproj_bwd(dx, y, vecs, y_ret, y_sb, w_out, tm=1024):
    S, D = dx.shape
    tm = min(tm, S)
    n = S // tm

    def body(dx_ref, y_ref, v_ref, yr_ref, ys_ref, w_ref, dyc_ref, dw_ref, st_ref, acc):
        i = pl.program_id(0)

        @pl.when(i == 0)
        def _():
            st_ref[...] = jnp.zeros_like(st_ref)
            acc[...] = jnp.zeros_like(acc)

        dxv = dx_ref[...]
        st_ref[0:1, :] += jnp.sum(dxv * y_ref[...].astype(F32), axis=0, keepdims=True)
        dyy = (dxv * v_ref[2:3, :]).astype(BF16)
        dyc_ref[...] = _dot_nt(dyy, w_ref[...])
        acc[0:GROUP_W, :] += _dot_tn(yr_ref[...], dyy)
        acc[GROUP_W:, :] += _dot_tn(ys_ref[...], dyy)

        @pl.when(i == n - 1)
        def _():
            dw_ref[...] = acc[...].astype(BF16)

    row = lambda w: pl.BlockSpec((tm, w), lambda i: (i, 0))
    fixed = lambda r: pl.BlockSpec((r, D), lambda i: (0, 0))
    return pl.pallas_call(
        body, name="outproj_bwd", grid=(n,),
        in_specs=[row(D), row(D), fixed(8), row(GROUP_W), row(GROUP_W), fixed(D)],
        out_specs=[row(D), fixed(D), fixed(8)],
        out_shape=[jax.ShapeDtypeStruct((S, D), F32), jax.ShapeDtypeStruct((D, D), BF16),
                   jax.ShapeDtypeStruct((8, D), F32)],
        scratch_shapes=[pltpu.VMEM((D, D), F32)],
        compiler_params=_cp("arbitrary"),
    )(dx, y, vecs, y_ret, y_sb, w_out)


def inproj_bwd_x(pieces, w3, x, vecs, dx_res, ship=None, tm=512):
    S, D = x.shape
    n = S // tm
    ex = _Exchange(ship)

    def body(*refs):
        p_refs, (w_ref, x_ref, v_ref, dr_ref), refs = refs[:8], refs[8:12], refs[12:]
        ship_refs, (dx_ref, st_ref), refs = refs[:ex.n_in], refs[ex.n_in:ex.n_in + 2], refs[ex.n_in + 2:]
        start, finish = ex.ops(ship_refs, refs)

        @pl.when(pl.program_id(0) == 0)
        def _():
            st_ref[...] = jnp.zeros_like(st_ref)
            start()

        dh = jnp.zeros((tm, D), F32)
        for k, p_ref in enumerate(p_refs):
            c0 = (k % 2) * GROUP_W
            dh = dh + _dot_nt(p_ref[...], w_ref[k // 2, :, c0:c0 + GROUP_W])
        xv = x_ref[...]
        r = lax.rsqrt(_rowmean(xv * xv) + EPS)
        xn = xv * r
        g, scale1 = v_ref[3:4, :], 1.0 + v_ref[1:2, :]
        st_ref[0:1, :] += jnp.sum(dh, axis=0, keepdims=True)
        dh_xn = dh * xn
        st_ref[1:2, :] += jnp.sum(dh_xn, axis=0, keepdims=True) * g
        st_ref[2:3, :] += jnp.sum(dh_xn, axis=0, keepdims=True) * scale1
        dxn = dh * (g * scale1)
        dx_ref[...] = r * (dxn - xn * _rowmean(dxn * xn)) + dr_ref[...]
        pl.when(pl.program_id(0) == n - 1)(finish)

    row = lambda w: pl.BlockSpec((tm, w), lambda i: (i, 0))
    return pl.pallas_call(
        body, name="inproj_bwd_x", grid=(n,),
        in_specs=[row(GROUP_W)] * 8 + [pl.BlockSpec((N_SHARD, D, SHARD_W), lambda i: (0, 0, 0)),
                                       row(D), pl.BlockSpec((8, D), lambda i: (0, 0)), row(D)] + ex.in_specs,
        out_specs=[row(D), pl.BlockSpec((8, D), lambda i: (0, 0))] + ex.out_specs,
        out_shape=[jax.ShapeDtypeStruct((S, D), F32), jax.ShapeDtypeStruct((8, D), F32)] + ex.out_shape,
        scratch_shapes=ex.scratch,
        compiler_params=_cp("arbitrary"),
    )(*pieces, w3, x, vecs, dx_res, *ex.ship)


def inproj_bwd_w(h, pieces, tm=1024):
    S, D = h.shape
    tm = min(tm, S)
    n = S // tm

    def body(*refs):
        h_ref, p_refs, dw_ref, acc = refs[0], refs[1:9], refs[9], refs[10]
        i = pl.program_id(0)

        @pl.when(i == 0)
        def _():
            acc[...] = jnp.zeros_like(acc)

        hv = h_ref[...]
        for k, p_ref in enumerate(p_refs):
            c0 = (k % 2) * GROUP_W
            acc[k // 2, :, c0:c0 + GROUP_W] += _dot_tn(hv, p_ref[...])

        @pl.when(i == n - 1)
        def _():
            dw_ref[...] = acc[...].astype(BF16)

    row = lambda w: pl.BlockSpec((tm, w), lambda i: (i, 0))
    return pl.pallas_call(
        body, name="inproj_bwd_w", grid=(n,),
        in_specs=[row(D)] + [row(GROUP_W)] * 8,
        out_specs=pl.BlockSpec((N_SHARD, D, SHARD_W), lambda i: (0, 0, 0), pipeline_mode=pl.Buffered(1)),
        out_shape=jax.ShapeDtypeStruct((N_SHARD, D, SHARD_W), BF16),
        scratch_shapes=[pltpu.VMEM((N_SHARD, D, SHARD_W), F32)],
        compiler_params=_cp("arbitrary"),
    )(h, *pieces)


def layer_fwd(x, vecs, w3, w_out, tabs, gather=None, head=None):
    cosf, sinf, lgam = tabs
    ret, sg, h, sb = inproj_fwd(x, vecs, w3)
    y_ret, o_ret, states = ret_fwd(ret, cosf, sinf, lgam)
    y_sb, o_sb, sb_end, *gathered = sb_fwd(sb, sg, gather)
    if callable(w_out):
        w_out = w_out(gathered[0])
    y, *x_next = outproj_fwd(x, vecs, y_ret, y_sb, w_out, head)
    saved = (x, ret, sg, h, sb, y_ret, o_ret, states, y_sb, o_sb, sb_end, y)
    return (x_next[0] if head is None else x_next), saved, (gathered[0] if gathered else None)


def _by_shard(dw_out):
    return dw_out.reshape(N_SHARD, D_MODEL // N_SHARD, D_MODEL)


def layer_bwd(dx, saved, vecs, w3, w_out, tabs, later_grads=None):
    cosf, sinf, lgam = tabs
    x, ret, sg, h, sb, y_ret, o_ret, states, y_sb, o_sb, sb_end, y = saved
    dycat, dw_out, st_o = outproj_bwd(dx, y, vecs, y_ret, y_sb, w_out)
    dw_out = _by_shard(dw_out)
    ship = None if later_grads is None else (later_grads[0], dw_out, later_grads[1])
    *d_sb, = sb_bwd(sb, sg, o_sb, sb_end, dycat, ship)
    d_ret = ret_bwd(ret, cosf, sinf, lgam, o_ret, states, dycat)
    pieces = list(d_ret) + d_sb[:4]
    dw_in = inproj_bwd_w(h, pieces)
    dx, st_i, *recv_in = inproj_bwd_x(pieces, w3, x, vecs, dx, None if later_grads is None else (dw_in,))
    dmod = jnp.concatenate([st_i[0:2], st_o[0:1]], axis=0)
    grads = (dw_in, dw_out) if later_grads is None else (recv_in[0], d_sb[4])
    return dx, dmod, st_i[2:3], grads


def _place():
    return lax.axis_index("x"), lax.axis_index("y"), lax.axis_index("c")


def _other_chips(mx, my):
    return [(1 - mx, my), (mx, 1 - my), (1 - mx, 1 - my)]


_ANY = pl.BlockSpec(memory_space=pl.ANY)


_GATHER_SCRATCH = [pltpu.SemaphoreType.DMA((7,)), pltpu.SemaphoreType.DMA((7,)), pltpu.SemaphoreType.DMA(())]


def _gather_ops(x_ref, out_ref, send_sems, recv_sems, local_sem):
    mx, my, mc = _place()
    me, sibling = (mx, my, mc), (mx, my, 1 - mc)
    chips = _other_chips(mx, my)

    def slot(px, py, pc):
        return out_ref.at[4 * px + 2 * py + pc]

    def copy(k, block, to, src=None):
        return pltpu.make_async_remote_copy(
            src_ref=slot(*block) if src is None else src, dst_ref=slot(*block),
            send_sem=send_sems.at[k], recv_sem=recv_sems.at[k], device_id=to, device_id_type=MESH)

    mine = pltpu.make_async_copy(x_ref, slot(*me), local_sem)
    first = [copy(0, me, sibling, src=x_ref)]
    first += [copy(1 + j, me, (*chip, mc), src=x_ref) for j, chip in enumerate(chips)]
    passed = [copy(4 + j, (*chip, mc), sibling) for j, chip in enumerate(chips)]

    def start():
        mine.start()
        for cp in first:
            cp.start()

    def forward():
        for j, chip in enumerate(chips):
            copy(1 + j, (*chip, mc), me).wait_recv()
            passed[j].start()

    def finish():
        copy(0, sibling, me).wait_recv()
        for j, chip in enumerate(chips):
            copy(4 + j, (*chip, 1 - mc), me).wait_recv()
        for cp in first + passed:
            cp.wait_send()
        mine.wait()

    return start, forward, finish


def allgather8(x, name):
    def body(x_ref, out_ref, send_sems, recv_sems, local_sem):
        for step in _gather_ops(x_ref, out_ref, send_sems, recv_sems, local_sem):
            step()

    return pl.pallas_call(
        body, name=name, out_shape=jax.ShapeDtypeStruct((8,) + x.shape, x.dtype),
        in_specs=[_ANY], out_specs=_ANY, scratch_shapes=_GATHER_SCRATCH,
    )(x)


class _Exchange:
    def __init__(self, ship):
        self.ship = list(ship or ())
        self.n_in = len(self.ship)
        self.n_out = 1 if self.ship else 0
        self.rows = [a.shape[1] for a in self.ship]
        self.in_specs = [_ANY] * self.n_in
        self.out_specs = [_ANY] * self.n_out
        self.out_shape = [jax.ShapeDtypeStruct((N_SHARD, sum(self.rows), SHARD_W), BF16)] * self.n_out
        sem = pltpu.SemaphoreType.DMA
        self.scratch = [sem((3,)), sem((3,)), sem(())] * self.n_out

    def ops(self, ship_refs, tail):
        if not self.ship:
            return (lambda: None), (lambda: None)
        recv, send_sems, recv_sems, local_sem = tail
        mx, my, mc = _place()
        my_chip = 2 * mx + my
        chips = _other_chips(mx, my)

        def pieces(s):
            firsts = np.cumsum([0] + self.rows[:-1])
            return [(ref.at[s], int(r0), n) for ref, r0, n in zip(ship_refs, firsts, self.rows)]

        def start():
            for src, r0, n in pieces(my_chip):
                pltpu.make_async_copy(src, recv.at[my_chip, pl.ds(r0, n)], local_sem).start()
            for j, (px, py) in enumerate(chips):
                for src, r0, n in pieces(2 * px + py):
                    pltpu.make_async_remote_copy(
                        src_ref=src, dst_ref=recv.at[my_chip, pl.ds(r0, n)],
                        send_sem=send_sems.at[j], recv_sem=recv_sems.at[j],
                        device_id=(px, py, mc), device_id_type=MESH).start()

        def finish():
            for j, (px, py) in enumerate(chips):
                whole = recv.at[2 * px + py]
                both = pltpu.make_async_remote_copy(
                    src_ref=whole, dst_ref=whole, send_sem=send_sems.at[j], recv_sem=recv_sems.at[j],
                    device_id=(px, py, mc), device_id_type=MESH)
                both.wait_recv()
                both.wait_send()
            pltpu.make_async_copy(recv.at[my_chip], recv.at[my_chip], local_sem).wait()

        return start, finish


def sum_slots(recv_a, recv_b, tr=256):
    n, rows_a, cols = recv_a.shape
    na, nb = rows_a // tr, recv_b.shape[1] // tr

    def body(a_ref, b_ref, o_ref):
        def total(r_ref):
            acc = r_ref[0].astype(F32)
            for k in range(1, n):
                acc = acc + r_ref[k].astype(F32)
            o_ref[...] = acc

        pl.when(pl.program_id(0) < na)(lambda: total(a_ref))
        pl.when(pl.program_id(0) >= na)(lambda: total(b_ref))

    return pl.pallas_call(
        body, name="sum_slots", grid=(na + nb,),
        in_specs=[pl.BlockSpec((n, tr, cols), lambda i: (0, jnp.minimum(i, na - 1), 0)),
                  pl.BlockSpec((n, tr, cols), lambda i: (0, jnp.maximum(i - na, 0), 0))],
        out_specs=pl.BlockSpec((tr, cols), lambda i: (i, 0)),
        out_shape=jax.ShapeDtypeStruct(((na + nb) * tr, cols), F32),
        compiler_params=_cp("arbitrary"),
    )(recv_a, recv_b)


def swap_sibling(p):
    def body(p_ref, out_ref, send_sem, recv_sem):
        mx, my, mc = _place()
        cp = pltpu.make_async_remote_copy(
            src_ref=p_ref, dst_ref=out_ref, send_sem=send_sem, recv_sem=recv_sem,
            device_id=(mx, my, 1 - mc), device_id_type=MESH)
        cp.start()
        cp.wait()

    return pl.pallas_call(
        body, name="swap_sibling", out_shape=jax.ShapeDtypeStruct(p.shape, p.dtype),
        in_specs=[_ANY], out_specs=_ANY,
        scratch_shapes=[pltpu.SemaphoreType.DMA(()), pltpu.SemaphoreType.DMA(())],
    )(p)


def _adamw(w, g, m, v):
    m = ADAM_B1 * m + (1.0 - ADAM_B1) * g
    v = ADAM_B2 * v + (1.0 - ADAM_B2) * (g * g)
    m_hat = m / (1.0 - ADAM_B1 ** ADAM_STEP)
    v_hat = v / (1.0 - ADAM_B2 ** ADAM_STEP)
    delta = -ADAM_LR * (m_hat / (jnp.sqrt(v_hat) + ADAM_EPS) + ADAM_WD * w)
    return delta, m, v


def adam_slab(p_own, p_sib, w, m, v, row0, name, tr=256):
    L, R, C = w.shape
    nr = R // tr

    def body(a_ref, b_ref, w_ref, m_ref, v_ref, g_out, d_out, m_out, v_out):
        g = a_ref[...] + b_ref[...]
        d, m2, v2 = _adamw(w_ref[0], g, m_ref[0], v_ref[0])
        g_out[0], d_out[0], m_out[0], v_out[0] = g, d, m2, v2

    slab = pl.BlockSpec((tr, C), lambda l, i: (row0 // tr + l * nr + i, 0))
    blk = pl.BlockSpec((1, tr, C), lambda l, i: (l, i, 0))
    return pl.pallas_call(
        body, name=name, grid=(L, nr),
        in_specs=[slab, slab, blk, blk, blk], out_specs=[blk] * 4,
        out_shape=[jax.ShapeDtypeStruct(w.shape, F32)] * 4,
        compiler_params=_cp("arbitrary", "arbitrary"),
    )(p_own, p_sib, w, m, v)


def ada_fwd(c_all, w_ada):
    L, D, W = w_ada.shape

    def body(c_ref, w_ref, o_ref):
        cv = c_ref[...]
        o_ref[0] = jnp.dot(cv * _sigmoid(cv), w_ref[0], precision=lax.Precision.HIGHEST,
                           preferred_element_type=F32)

    return pl.pallas_call(
        body, name="ada_fwd", grid=(L,),
        in_specs=[pl.BlockSpec((8, D), lambda l: (0, 0)), pl.BlockSpec((1, D, W), lambda l: (l, 0, 0))],
        out_specs=pl.BlockSpec((1, 8, W), lambda l: (l, 0, 0)),
        out_shape=jax.ShapeDtypeStruct((L, 8, W), F32),
        compiler_params=_cp("arbitrary"),
    )(c_all, w_ada)


def vecs_build(mod_all, b_ada, norm_g):
    W = mod_all.shape[2]

    def body(m_ref, b_ref, g_ref, o_ref):
        mx, my, mc = _place()
        me = 4 * mx + 2 * my + mc
        rowid = lax.broadcasted_iota(jnp.int32, (2 * 8, 1), 0)
        o_ref[...] = jnp.zeros_like(o_ref)
        for l in range(DEPTH):
            parts = [jnp.sum(jnp.where(rowid == l * 8 + me, m_ref[2 * s + mc], 0.0), axis=0, keepdims=True)
                     for s in range(N_SHARD)]
            mod = jnp.concatenate(parts, axis=1) + b_ref[l:l + 1, :]
            for t in range(3):
                o_ref[l, t:t + 1, :] = mod[:, t * D_MODEL:(t + 1) * D_MODEL]
            o_ref[l, 3:4, :] = g_ref[l:l + 1, :]

    return pl.pallas_call(
        body, name="vecs_build", out_shape=jax.ShapeDtypeStruct((DEPTH, 8, D_MODEL), F32),
    )(mod_all, b_ada, norm_g)


def ada_update(dmods, c_t, w, m, v, tr=256):
    L, D, W = w.shape

    def body(dm_ref, c_ref, w_ref, m_ref, v_ref, g_out, d_out, m_out, v_out):
        mx, my, _ = _place()
        shard = 2 * mx + my
        dm = jnp.zeros((8, W), F32)
        for s in range(N_SHARD):
            dm = dm + jnp.where(shard == s, dm_ref[0, :, s * W:(s + 1) * W], 0.0)
        cv = c_ref[...]
        ca = cv * _sigmoid(cv)
        g = jnp.zeros((tr, W), F32)
        for b in range(8):
            g = g + ca[:, b:b + 1] * dm[b:b + 1, :]
        d, m2, v2 = _adamw(w_ref[0], g, m_ref[0], v_ref[0])
        g_out[0], d_out[0], m_out[0], v_out[0] = g, d, m2, v2

    blk = pl.BlockSpec((1, tr, W), lambda l, i: (l, i, 0))
    return pl.pallas_call(
        body, name="ada_update", grid=(L, D // tr),
        in_specs=[pl.BlockSpec((1, 8, 3 * D), lambda l, i: (l, 0, 0)), pl.BlockSpec((tr, 8), lambda l, i: (i, 0)),
                  blk, blk, blk],
        out_specs=[blk] * 4, out_shape=[jax.ShapeDtypeStruct(w.shape, F32)] * 4,
        compiler_params=_cp("arbitrary", "arbitrary"),
    )(dmods, c_t, w, m, v)


STAT_ROWS = 16


def small_update(stats_all, norm, b_ada, final):
    def body(s_ref, *refs):
        ins, outs = refs[:9], refs[9:]
        tot = s_ref[0]
        for k in range(1, 8):
            tot = tot + s_ref[k]
        g_norm = tot[0:2, :]
        g_final = tot[2:3, :]
        g_b = jnp.concatenate(
            [jnp.concatenate([tot[3 + 3 * l + t:4 + 3 * l + t, :] for t in range(3)], axis=1) for l in range(DEPTH)],
            axis=0)
        for p, g in enumerate((g_norm, g_b, g_final)):
            w_ref, m_ref, v_ref = ins[3 * p:3 * p + 3]
            d, m2, v2 = _adamw(w_ref[...], g, m_ref[...], v_ref[...])
            for o_ref, val in zip(outs[4 * p:4 * p + 4], (g, d, m2, v2)):
                o_ref[...] = val
        loss = (0.5 / D_MODEL) * jnp.sum(tot[9:10, :], axis=1, keepdims=True)
        outs[12][...] = jnp.broadcast_to(loss, (8, LANES))

    shapes = []
    for w, _, _ in (norm, b_ada, final):
        shapes += [jax.ShapeDtypeStruct(w.shape, F32)] * 4
    shapes.append(jax.ShapeDtypeStruct((8, LANES), F32))
    return pl.pallas_call(body, name="small_update", out_shape=shapes)(stats_all, *norm, *b_ada, *final)


def kernel(x, c, norm_g, w_ada, b_ada, w_in, w_out, final_g, loss_target, m_norm_g, m_w_ada, m_b_ada, m_w_in, m_w_out, m_final_g, v_norm_g, v_w_ada, v_b_ada, v_w_in, v_w_out, v_final_g):
    S, D = x.shape[1], x.shape[2]
    mc = lax.axis_index("c")
    out_rows = D // N_SHARD

    def my_half(a, rows):
        return lax.dynamic_slice_in_dim(a, mc * rows, rows, axis=0)

    assert DEPTH == 2
    win = [my_half(w_in[l], D // 2).astype(BF16) for l in range(DEPTH)]
    wout = [my_half(w_out[l], out_rows // 2).astype(BF16) for l in range(DEPTH)]
    w3_first = allgather8(win[0], "gather_weights").reshape(N_SHARD, D, SHARD_W)
    rest = jnp.concatenate([wout[0], win[1], wout[1]], axis=0)

    def unpack(wall):
        wall = wall.reshape(N_SHARD, 2, rest.shape[0], SHARD_W)
        a, b = out_rows // 2, out_rows // 2 + D // 2
        return wall[:, :, :a].reshape(D, D), (wall[:, :, a:b].reshape(N_SHARD, D, SHARD_W), wall[:, :, b:].reshape(D, D))

    c_all = allgather8(jnp.broadcast_to(c, (8, D)), "gather_c")[:, 0, :]
    mod_all = allgather8(ada_fwd(c_all, w_ada).reshape(DEPTH * 8, -1), "gather_mod")
    vecs = vecs_build(mod_all, b_ada, norm_g)

    tabs = (*rope_tables(S), ret_log_gamma())
    saved = [None] * DEPTH
    h, saved[0], wall = layer_fwd(x[0], vecs[0], w3_first, lambda g: unpack(g)[0], tabs, rest)
    weights = [(w3_first, unpack(wall)[0]), unpack(wall)[1]]
    head = (jnp.broadcast_to(final_g[None, :], (8, D)), loss_target[0])
    (dx, st_loss), saved[1], _ = layer_fwd(h, vecs[1], *weights[1], tabs, head=head)

    dmod, dnorm, grads = [None] * DEPTH, [None] * DEPTH, None
    for l in reversed(range(DEPTH)):
        dx, dmod[l], dnorm[l], grads = layer_bwd(dx, saved[l], vecs[l], *weights[l], tabs, grads)

    p_own = sum_slots(*grads)
    p_sib = swap_sibling(p_own)
    res_in = adam_slab(p_own, p_sib, w_in, m_w_in, v_w_in, 0, "adam_w_in")
    res_out = adam_slab(p_own, p_sib, w_out, m_w_out, v_w_out, DEPTH * D, "adam_w_out", tr=128)

    stats = jnp.concatenate(dnorm + [st_loss[0:1]] + dmod + [st_loss[1:2], jnp.zeros((STAT_ROWS - 10, D), F32)], axis=0)
    stats_all = allgather8(stats, "gather_stats")
    dmods = stats_all[:, 3:9, :].reshape(8, DEPTH, 3 * D).transpose(1, 0, 2)
    res_ada = ada_update(dmods, c_all.T, w_ada, m_w_ada, v_w_ada)
    small = small_update(stats_all, (norm_g, m_norm_g, v_norm_g), (b_ada, m_b_ada, v_b_ada),
                         (final_g[None, :], m_final_g[None, :], v_final_g[None, :]))
    res_norm, res_b, res_final = small[0:4], small[4:8], [a[0] for a in small[8:12]]
    loss = small[12][0, 0]

    by_kind = [res_norm, res_ada, res_b, res_in, res_out, res_final]
    outs = [loss, dx[None]]
    for kind in range(4):
        outs += [r[kind] for r in by_kind]
    return tuple(outs)
```

```python
import functools

import numpy as np
import jax
import jax.numpy as jnp
from jax import lax
from jax.experimental import pallas as pl
from jax.experimental.pallas import tpu as pltpu

F32, BF16 = jnp.float32, jnp.bfloat16
MESH = pl.DeviceIdType.MESH

D_MODEL = 1024
DEPTH = 2
SHARD_W = 1024
N_SHARD = 4
GROUP_W = 512
LANES = 128
SB_HEAD_DIM = 64
RET_HEAD_DIM = 128
CHUNK = 64
ROPE_BASE = 10000.0
EPS = 1e-6
SQ_SCALE = SB_HEAD_DIM ** -0.5
RK_SCALE = RET_HEAD_DIM ** -0.5
SB_T = 256
RET_T = 256
EXP_ZERO = -104.0
VMEM_LIMIT_BYTES = 56 * 2 ** 20

ADAM_LR, ADAM_B1, ADAM_B2, ADAM_EPS, ADAM_WD, ADAM_STEP = 0.001, 0.9, 0.999, 1e-08, 0.01, 10


def _cp(*sem):
    return pltpu.CompilerParams(dimension_semantics=sem, vmem_limit_bytes=VMEM_LIMIT_BYTES)


def _dot(a, b):
    return lax.dot_general(a, b, (((1,), (0,)), ((), ())), preferred_element_type=F32)


def _dot_nt(a, b):
    return lax.dot_general(a, b, (((1,), (1,)), ((), ())), preferred_element_type=F32)


def _dot_tn(a, b):
    return lax.dot_general(a, b, (((0,), (0,)), ((), ())), preferred_element_type=F32)


def _running_sum(a, tri):
    return _dot(a.astype(BF16), tri)


def _sigmoid(x):
    return 1.0 / (1.0 + jnp.exp(-x))


def _rowsum(a):
    return jnp.sum(a, axis=1, keepdims=True)


def _rowmean(a):
    return jnp.mean(a, axis=1, keepdims=True)


def inproj_fwd(x, vecs, w3, tm=512):
    S, D = x.shape

    def body(x_ref, v_ref, w_ref, ret_ref, sg_ref, h_ref, sb_ref):
        xv = x_ref[...]
        r = lax.rsqrt(_rowmean(xv * xv) + EPS)
        h = xv * r * v_ref[3:4, :] * (1.0 + v_ref[1:2, :]) + v_ref[0:1, :]
        hb = h.astype(BF16)
        h_ref[...] = hb
        for s in range(N_SHARD):
            p = _dot(hb, w_ref[s])
            if s < 2:
                ret_ref[:, s * SHARD_W:(s + 1) * SHARD_W] = p
            if s == 2:
                sb_ref[:, 0:GROUP_W] = (p[:, 0:GROUP_W] * SQ_SCALE).astype(BF16)
                sb_ref[:, GROUP_W:SHARD_W] = p[:, GROUP_W:].astype(BF16)
            if s == 3:
                sb_ref[:, SHARD_W:SHARD_W + GROUP_W] = p[:, 0:GROUP_W].astype(BF16)
                sg_ref[...] = p[:, GROUP_W:]

    row = lambda w: pl.BlockSpec((tm, w), lambda i: (i, 0))
    return pl.pallas_call(
        body, name="inproj_fwd", grid=(S // tm,),
        in_specs=[row(D), pl.BlockSpec((8, D), lambda i: (0, 0)),
                  pl.BlockSpec((N_SHARD, D, SHARD_W), lambda i: (0, 0, 0))],
        out_specs=[row(2 * SHARD_W), row(GROUP_W), row(D), row(3 * GROUP_W)],
        out_shape=[jax.ShapeDtypeStruct((S, 2 * SHARD_W), F32), jax.ShapeDtypeStruct((S, GROUP_W), F32),
                   jax.ShapeDtypeStruct((S, D), BF16), jax.ShapeDtypeStruct((S, 3 * GROUP_W), BF16)],
        compiler_params=_cp("arbitrary"),
    )(x, vecs, w3)


def _sb_logits(qh, k2, keep):
    z = _dot_nt(qh, k2)
    sp = jnp.log(1.0 + jnp.exp(-jnp.abs(z)))
    lb = jnp.minimum(z, 0.0) - sp
    lk = lb - z
    if keep is not None:
        lk = jnp.where(keep, lk, 0.0)
    return lb, lk


SB_NB = 3


class _sb_chains:
    def __init__(self, i, q2, do_b=None):
        t = self.t = SB_T // 2
        self.C = range(2)
        r = lax.broadcasted_iota(jnp.int32, (t, t), 0)
        c = lax.broadcasted_iota(jnp.int32, (t, t), 1)
        self.later = jnp.where(r > c, 1.0, 0.0).astype(BF16)
        self.earlier = jnp.where(r < c, 1.0, 0.0).astype(BF16)
        self.head0 = lax.broadcasted_iota(jnp.int32, (1, LANES), 1) < SB_HEAD_DIM
        row = lax.broadcasted_iota(jnp.int32, (2 * t, SB_NB * t), 0) & (t - 1)
        col = lax.broadcasted_iota(jnp.int32, (2 * t, SB_NB * t), 1)
        qt = [2 * i + cc for cc in self.C]
        self.first = [jnp.maximum(qt[cc] - (SB_NB - 1), 0) for cc in self.C]
        self.keep = [self.first[cc] * t + col < qt[cc] * t + row for cc in self.C]
        self.qs = [self._stack(q2[cc * t:(cc + 1) * t]) for cc in self.C]
        if do_b is not None:
            self.dos = [self._stack(do_b[cc * t:(cc + 1) * t]) for cc in self.C]

    def _stack(self, a):
        zero = jnp.zeros_like(a)
        return jnp.concatenate([jnp.where(self.head0, a, zero), jnp.where(self.head0, zero, a)], axis=0)

    def rows(self, ref, j, n):
        return ref[pl.ds(pl.multiple_of(j * self.t, self.t), n * self.t), :]

    def _parts(self, a):
        return [a[:, k * self.t:(k + 1) * self.t] for k in range(SB_NB)]

    def suffix(self, lk):
        parts = self._parts(lk)
        rs = [_rowsum(p) for p in parts]
        out, right = [], None
        for k in reversed(range(SB_NB)):
            cs = _running_sum(parts[k], self.later)
            out.append(cs if right is None else cs + right)
            right = rs[k] if right is None else right + rs[k]
        return jnp.concatenate(out[::-1], axis=1), right

    def prefix(self, g, G0):
        out, left = [], G0
        for p in self._parts(g):
            out.append(_running_sum(p, self.earlier) + left)
            left = left + _rowsum(p)
        return jnp.concatenate(out, axis=1)


def sb_fwd(sb, sg, gather=None):
    S = sb.shape[0]
    T = SB_T
    nq = S // T
    carried = [] if gather is None else [gather]

    def body(*refs):
        (q_ref, k_ref, v_ref, sg_ref), refs = refs[:4], refs[4:]
        p, i = pl.program_id(0), pl.program_id(1)
        if carried:
            x_ref, y_ref, o_ref, end_ref, out_ref, send_sems, recv_sems, local_sem = refs
            start, forward, finish = _gather_ops(x_ref, out_ref, send_sems, recv_sems, local_sem)
            pl.when(jnp.logical_and(p == 0, i == 0))(start)
            pl.when(jnp.logical_and(p == 2, i == 0))(forward)
        else:
            y_ref, o_ref, end_ref = refs
        ch = _sb_chains(i, q_ref[...])
        later, head0 = ch.later, ch.head0
        lbk = [_sb_logits(ch.qs[c], ch.rows(k_ref, ch.first[c], SB_NB), ch.keep[c]) for c in ch.C]
        suffix, R = zip(*[ch.suffix(lbk[c][1]) for c in ch.C])
        aa = [jnp.where(ch.keep[c], jnp.exp(lbk[c][0] + suffix[c]), 0.0) for c in ch.C]
        acc = [_dot(aa[c].astype(BF16), ch.rows(v_ref, ch.first[c], SB_NB)) for c in ch.C]

        outs = []
        for c in ch.C:
            def tile(j, Rc, c=c):
                lb, lk = _sb_logits(ch.qs[c], ch.rows(k_ref, j, 1), None)
                a = jnp.exp(lb + _running_sum(lk, later) + Rc)
                return _dot(a.astype(BF16), ch.rows(v_ref, j, 1)), Rc + _rowsum(lk)

            def cond(st):
                return jnp.logical_and(st[0] >= 0, st[3] > EXP_ZERO)

            def step(st, tile=tile):
                cx, Rn = tile(st[0], st[2])
                return st[0] - 1, st[1] + cx, Rn, jnp.max(Rn)

            j_end, ac, Rc, _ = lax.while_loop(cond, step, (ch.first[c] - 1, acc[c], R[c], jnp.max(R[c])))
            base = c * (T + 8)
            end_ref[0, 0, base:base + T, :] = jnp.broadcast_to(Rc, (T, 8))
            end_ref[0, 0, base + T:base + T + 8, :] = jnp.full((8, 8), j_end.astype(F32))
            outs.append(jnp.where(head0, ac[:ch.t], ac[ch.t:]))
        o = jnp.concatenate(outs, axis=0)
        o_ref[...] = o
        sg = sg_ref[...]
        y_ref[...] = (o * (sg * _sigmoid(sg))).astype(BF16)
        if carried:
            pl.when(jnp.logical_and(p == 3, i == nq - 1))(finish)

    return pl.pallas_call(
        body, name="sb_fwd", grid=(4, nq),
        in_specs=[pl.BlockSpec((T, LANES), lambda p, i: (i, p)),
                  pl.BlockSpec((S, LANES), lambda p, i: (0, 4 + p)),
                  pl.BlockSpec((S, LANES), lambda p, i: (0, 8 + p)),
                  pl.BlockSpec((T, LANES), lambda p, i: (i, p))] + [_ANY for _ in carried],
        out_specs=[pl.BlockSpec((T, LANES), lambda p, i: (i, p)),
                   pl.BlockSpec((T, LANES), lambda p, i: (i, p)),
                   pl.BlockSpec((1, 1, 2 * (T + 8), 8), lambda p, i: (p, i, 0, 0))] + [_ANY for _ in carried],
        out_shape=[jax.ShapeDtypeStruct((S, GROUP_W), BF16),
                   jax.ShapeDtypeStruct((S, GROUP_W), F32),
                   jax.ShapeDtypeStruct((4, nq, 2 * (T + 8), 8), F32)]
        + [jax.ShapeDtypeStruct((8,) + a.shape, a.dtype) for a in carried],
        scratch_shapes=_GATHER_SCRATCH if carried else [],
        compiler_params=_cp("arbitrary", "arbitrary"),
    )(sb, sb, sb, sg, *carried)


def sb_bwd(sb, sg, o, sb_end, dycat, ship=None):
    S = sb.shape[0]
    T = SB_T
    nq = S // T
    ex = _Exchange(ship)

    def body(*refs):
        (q_ref, k_ref, v_ref, sg_ref, o_ref, dy_ref, end_ref), refs = refs[:7], refs[7:]
        ship_refs, (dq_ref, dk_ref, dv_ref, dsg_ref), refs = refs[:ex.n_in], refs[ex.n_in:ex.n_in + 4], refs[ex.n_in + 4:]
        recv, (dk_acc, dv_acc), sems = refs[:ex.n_out], refs[ex.n_out:ex.n_out + 2], refs[ex.n_out + 2:]
        start, finish = ex.ops(ship_refs, recv + sems)
        p, i = pl.program_id(0), pl.program_id(1)
        pl.when(jnp.logical_and(p == 0, i == 0))(start)

        @pl.when(i == 0)
        def _():
            dk_acc[...] = jnp.zeros_like(dk_acc)
            dv_acc[...] = jnp.zeros_like(dv_acc)

        sg = sg_ref[...]
        sig = _sigmoid(sg)
        dy = dy_ref[...]
        dsg_ref[...] = (dy * o_ref[...] * (sig * (1.0 + sg * (1.0 - sig)))).astype(BF16)
        do_b = (dy * (sg * sig)).astype(BF16)
        ch = _sb_chains(i, q_ref[...], do_b)
        later, earlier, head0, t = ch.later, ch.earlier, ch.head0, ch.t
        end = end_ref[0, 0]

        def grads(c, j, n, a, lb, g, G, keep):
            dz = g - jnp.exp(lb) * (g + G)
            if keep is not None:
                dz = jnp.where(keep, dz, 0.0)
            dzb = dz.astype(BF16)
            rows = pl.ds(pl.multiple_of(j * t, t), n * t)
            dk_acc[rows, :] += _dot_tn(dzb, ch.qs[c])
            dv_acc[rows, :] += _dot_tn(a.astype(BF16), ch.dos[c])
            return _dot(dzb, ch.rows(k_ref, j, n))

        dq, G0 = [], []
        for c in ch.C:
            base = c * (T + 8)
            j_end = jnp.max(end[base + T:base + T + 8, :]).astype(jnp.int32)

            def sweep(j, st, c=c):
                dqc, G0c, left = st
                lb, lk = _sb_logits(ch.qs[c], ch.rows(k_ref, j, 1), None)
                stick = left - _rowsum(lk)
                a = jnp.exp(lb + _running_sum(lk, later) + stick)
                g = a * _dot_nt(ch.dos[c], ch.rows(v_ref, j, 1))
                G = _running_sum(g, earlier) + G0c
                return dqc + grads(c, j, 1, a, lb, g, G, None), G0c + _rowsum(g), stick

            st = lax.fori_loop(j_end + 1, ch.first[c], sweep,
                               (jnp.zeros((T, LANES), F32), jnp.zeros((T, 1), F32), end[base:base + T, 0:1]))
            dq.append(st[0])
            G0.append(st[1])

        lbk = [_sb_logits(ch.qs[c], ch.rows(k_ref, ch.first[c], SB_NB), ch.keep[c]) for c in ch.C]
        suffix = [ch.suffix(lbk[c][1])[0] for c in ch.C]
        aa = [jnp.where(ch.keep[c], jnp.exp(lbk[c][0] + suffix[c]), 0.0) for c in ch.C]
        g = [aa[c] * _dot_nt(ch.dos[c], ch.rows(v_ref, ch.first[c], SB_NB)) for c in ch.C]
        G = [ch.prefix(g[c], G0[c]) for c in ch.C]
        for c in ch.C:
            dqc = dq[c] + grads(c, ch.first[c], SB_NB, aa[c], lbk[c][0], g[c], G[c], ch.keep[c])
            dq_ref[c * t:(c + 1) * t, :] = (jnp.where(head0, dqc[:t], dqc[t:]) * SQ_SCALE).astype(BF16)

        @pl.when(i == nq - 1)
        def _():
            dk_ref[...] = dk_acc[...].astype(BF16)
            dv_ref[...] = dv_acc[...].astype(BF16)

        pl.when(jnp.logical_and(p == 3, i == nq - 1))(finish)

    tile_spec = lambda c0: pl.BlockSpec((T, LANES), lambda p, i: (i, c0 + p))
    head_spec = lambda c0: pl.BlockSpec((S, LANES), lambda p, i: (0, c0 + p))
    return pl.pallas_call(
        body, name="sb_bwd", grid=(4, nq),
        in_specs=[tile_spec(0), head_spec(4), head_spec(8), tile_spec(0), tile_spec(0), tile_spec(4),
                  pl.BlockSpec((1, 1, 2 * (T + 8), 8), lambda p, i: (p, i, 0, 0))] + ex.in_specs,
        out_specs=[tile_spec(0), head_spec(0), head_spec(0), tile_spec(0)] + ex.out_specs,
        out_shape=[jax.ShapeDtypeStruct((S, GROUP_W), BF16)] * 4 + ex.out_shape,
        scratch_shapes=[pltpu.VMEM((S, LANES), F32), pltpu.VMEM((S, LANES), F32)] + ex.scratch,
        compiler_params=_cp("arbitrary", "arbitrary"),
    )(sb, sb, sb, sg, o, dycat, sb_end, *ex.ship)


def rope_tables(S):
    half = RET_HEAD_DIM // 2
    inv = ROPE_BASE ** (-jnp.arange(half, dtype=F32) / half)
    ang = jnp.arange(S, dtype=F32)[:, None] * inv[None, :]
    cos, sin = jnp.cos(ang), jnp.sin(ang)
    return jnp.concatenate([cos, cos], axis=1), jnp.concatenate([-sin, sin], axis=1)


def ret_log_gamma():
    return jnp.log1p(-(2.0 ** (-5.0 - jnp.arange(4, dtype=F32))))


def _swap_halves(a):
    return pltpu.roll(a, RET_HEAD_DIM // 2, axis=1)


def _ret_decay_mask(lg):
    n = lax.broadcasted_iota(jnp.int32, (RET_T, RET_T), 0)
    m = lax.broadcasted_iota(jnp.int32, (RET_T, RET_T), 1)
    dist = jnp.abs(n - m).astype(F32)
    return jnp.where((m // CHUNK) <= (n // CHUNK), jnp.exp(lg * dist), 0.0)


def _ret_block(lg, rq, rk, rv, cosf, sinf, dm):
    q = rq * cosf + _swap_halves(rq) * sinf
    k = (rk * cosf + _swap_halves(rk) * sinf) * RK_SCALE
    qb, kb, vb = q.astype(BF16), k.astype(BF16), rv.astype(BF16)
    sc = _dot_nt(qb, kb) * dm
    nloc = lax.broadcasted_iota(jnp.int32, (RET_T, 1), 0).astype(F32)
    qdec = jnp.exp(lg * (nloc + 1.0))
    kdec = jnp.exp(lg * (RET_T - 1.0 - nloc))
    block_dec = jnp.exp(jnp.full((1, LANES), lg * RET_T, F32))
    return q, k, qb, kb, vb, sc, qdec, kdec, block_dec


def _ret_specs(S, rb):
    group = lambda c0: pl.BlockSpec((RET_T, GROUP_W), lambda b: (rb(b), c0))
    return group, pl.BlockSpec((RET_T, LANES), lambda b: (rb(b), 0))


def _head(ref, h):
    return ref[:, h * LANES:(h + 1) * LANES]


def ret_fwd(proj, cosf, sinf, lgam):
    S = proj.shape[0]
    nb = S // RET_T
    group, row_tab = _ret_specs(S, lambda b: b)

    def body(lg_ref, rq_ref, rk_ref, rv_ref, rg_ref, cos_ref, sin_ref, y_ref, o_ref, st_out, st_ref, dm_ref):
        @pl.when(pl.program_id(0) == 0)
        def _():
            st_ref[...] = jnp.zeros_like(st_ref)
            for h in range(4):
                dm_ref[h] = _ret_decay_mask(lg_ref[h])

        cosf, sinf = cos_ref[...], sin_ref[...]
        H = range(4)
        lanes = [slice(h * LANES, (h + 1) * LANES) for h in H]
        blk = [_ret_block(lg_ref[h], _head(rq_ref, h), _head(rk_ref, h), _head(rv_ref, h), cosf, sinf, dm_ref[h])
               for h in H]
        q, k, qb, kb, vb, sc, qdec, kdec, block_dec = zip(*blk)
        st = [st_ref[h] for h in H]
        for h in H:
            st_out[h, 0] = st[h]
        o = [_dot(sc[h].astype(BF16), vb[h]) + _dot(qb[h], st[h].astype(BF16)) * qdec[h] for h in H]
        for h in H:
            st_ref[h] = st[h] * block_dec[h] + _dot_tn((k[h] * kdec[h]).astype(BF16), vb[h])
        for h in H:
            o_ref[:, lanes[h]] = o[h]
        cen = [o[h] - _rowmean(o[h]) for h in H]
        on = [cen[h] * lax.rsqrt(_rowmean(cen[h] * cen[h]) + EPS) for h in H]
        rg = [_head(rg_ref, h) for h in H]
        for h in H:
            y_ref[:, lanes[h]] = (on[h] * (rg[h] * _sigmoid(rg[h]))).astype(BF16)

    return pl.pallas_call(
        body, name="ret_fwd", grid=(nb,),
        in_specs=[pl.BlockSpec(memory_space=pltpu.SMEM),
                  group(0), group(1), group(2), group(3), row_tab, row_tab],
        out_specs=[group(0), group(0),
                   pl.BlockSpec((4, 1, LANES, LANES), lambda b: (0, b, 0, 0))],
        out_shape=[jax.ShapeDtypeStruct((S, GROUP_W), BF16),
                   jax.ShapeDtypeStruct((S, GROUP_W), F32),
                   jax.ShapeDtypeStruct((4, nb, LANES, LANES), F32)],
        scratch_shapes=[pltpu.VMEM((4, LANES, LANES), F32), pltpu.VMEM((4, RET_T, RET_T), F32)],
        compiler_params=_cp("arbitrary"),
    )(lgam, proj, proj, proj, proj, cosf, sinf)


def ret_bwd(proj, cosf, sinf, lgam, o, states, dycat):
    S = proj.shape[0]
    nb = S // RET_T
    rev = lambda b: nb - 1 - b
    group, row_tab = _ret_specs(S, rev)

    def body(lg_ref, rq_ref, rk_ref, rv_ref, rg_ref, cos_ref, sin_ref, o_ref, st_in, dy_ref,
             drq_ref, drk_ref, drv_ref, drg_ref, ds_ref, dm_ref):
        @pl.when(pl.program_id(0) == 0)
        def _():
            ds_ref[...] = jnp.zeros_like(ds_ref)
            for h in range(4):
                dm_ref[h] = _ret_decay_mask(lg_ref[h])

        cosf, sinf = cos_ref[...], sin_ref[...]
        H = range(4)
        lanes = [slice(h * LANES, (h + 1) * LANES) for h in H]
        dms = [dm_ref[h] for h in H]
        blk = [_ret_block(lg_ref[h], _head(rq_ref, h), _head(rk_ref, h), _head(rv_ref, h), cosf, sinf, dms[h])
               for h in H]
        q, k, qb, kb, vb, sc, qdec, kdec, block_dec = zip(*blk)
        o_v = [_head(o_ref, h) for h in H]
        cen = [o_v[h] - _rowmean(o_v[h]) for h in H]
        rstd = [lax.rsqrt(_rowmean(cen[h] * cen[h]) + EPS) for h in H]
        on = [cen[h] * rstd[h] for h in H]
        rg = [_head(rg_ref, h) for h in H]
        sig = [_sigmoid(rg[h]) for h in H]
        dy = [_head(dy_ref, h) for h in H]
        for h in H:
            drg_ref[:, lanes[h]] = (dy[h] * on[h] * (sig[h] * (1.0 + rg[h] * (1.0 - sig[h])))).astype(BF16)
        don = [dy[h] * (rg[h] * sig[h]) for h in H]
        do = [rstd[h] * (don[h] - _rowmean(don[h]) - on[h] * _rowmean(don[h] * on[h])) for h in H]
        dob = [do[h].astype(BF16) for h in H]
        dsc = [(_dot_nt(dob[h], vb[h]) * dms[h]).astype(BF16) for h in H]
        st_b = [st_in[h, 0].astype(BF16) for h in H]
        dsn = [ds_ref[h] for h in H]
        dsn_b = [dsn[h].astype(BF16) for h in H]
        dq = [_dot(dsc[h], kb[h]) + _dot_nt(dob[h], st_b[h]) * qdec[h] for h in H]
        dk = [(_dot_tn(dsc[h], qb[h]) + _dot_nt(vb[h], dsn_b[h]) * kdec[h]) * RK_SCALE for h in H]
        dv = [_dot_tn(sc[h].astype(BF16), dob[h]) + _dot((k[h] * kdec[h]).astype(BF16), dsn_b[h]) for h in H]
        for h in H:
            ds_ref[h] = dsn[h] * block_dec[h] + _dot_tn((q[h] * qdec[h]).astype(BF16), dob[h])
        for h in H:
            drq_ref[:, lanes[h]] = (dq[h] * cosf + _swap_halves(dq[h] * sinf)).astype(BF16)
            drk_ref[:, lanes[h]] = (dk[h] * cosf + _swap_halves(dk[h] * sinf)).astype(BF16)
            drv_ref[:, lanes[h]] = dv[h].astype(BF16)

    return pl.pallas_call(
        body, name="ret_bwd", grid=(nb,),
        in_specs=[pl.BlockSpec(memory_space=pltpu.SMEM),
                  group(0), group(1), group(2), group(3), row_tab, row_tab,
                  group(0), pl.BlockSpec((4, 1, LANES, LANES), lambda b: (0, rev(b), 0, 0)),
                  group(0)],
        out_specs=[group(0)] * 4,
        out_shape=[jax.ShapeDtypeStruct((S, GROUP_W), BF16)] * 4,
        scratch_shapes=[pltpu.VMEM((4, LANES, LANES), F32), pltpu.VMEM((4, RET_T, RET_T), F32)],
        compiler_params=_cp("arbitrary"),
    )(lgam, proj, proj, proj, proj, cosf, sinf, o, states, dycat)


def outproj_fwd(x, vecs, y_ret, y_sb, w_out, head=None, tm=1024):
    S, D = x.shape
    tm = min(tm, S)
    last = list(head or ())

    def body(x_ref, v_ref, yr_ref, ys_ref, w_ref, *refs):
        y = _dot(yr_ref[...], w_ref[0:GROUP_W, :]) + _dot(ys_ref[...], w_ref[GROUP_W:, :])
        xv = x_ref[...] + v_ref[2:3, :] * y
        if not last:
            y_ref, xo_ref = refs
            y_ref[...] = y.astype(BF16)
            xo_ref[...] = xv
            return
        g_ref, t_ref, y_ref, dx_ref, st_ref = refs
        y_ref[...] = y.astype(BF16)

        @pl.when(pl.program_id(0) == 0)
        def _():
            st_ref[...] = jnp.zeros_like(st_ref)

        g = g_ref[0:1, :]
        r = lax.rsqrt(_rowmean(xv * xv) + EPS)
        xn = xv * r
        err = xn * g - t_ref[...]
        dy = err * (1.0 / D)
        dxn = dy * g
        dx_ref[...] = r * (dxn - xn * _rowmean(dxn * xn))
        st_ref[0:1, :] += jnp.sum(dy * xn, axis=0, keepdims=True)
        st_ref[1:2, :] += jnp.sum(err * err, axis=0, keepdims=True)

    row = lambda w: pl.BlockSpec((tm, w), lambda i: (i, 0))
    fixed = pl.BlockSpec((8, D), lambda i: (0, 0))
    return pl.pallas_call(
        body, name="outproj_fwd", grid=(S // tm,),
        in_specs=[row(D), fixed, row(GROUP_W), row(GROUP_W), pl.BlockSpec((D, D), lambda i: (0, 0))]
        + ([fixed, row(D)] if last else []),
        out_specs=[row(D), row(D)] + ([fixed] if last else []),
        out_shape=[jax.ShapeDtypeStruct((S, D), BF16), jax.ShapeDtypeStruct((S, D), F32)]
        + ([jax.ShapeDtypeStruct((8, D), F32)] if last else []),
        compiler_params=_cp("arbitrary"),
    )(x, vecs, y_ret, y_sb, w_out, *last)


def outproj_bwd(dx, y, vecs, y_ret, y_sb, w_out, tm=1024):
    S, D = dx.shape
    tm = min(tm, S)
    n = S // tm

    def body(dx_ref, y_ref, v_ref, yr_ref, ys_ref, w_ref, dyc_ref, dw_ref, st_ref, acc):
        i = pl.program_id(0)

        @pl.when(i == 0)
        def _():
            st_ref[...] = jnp.zeros_like(st_ref)
            acc[...] = jnp.zeros_like(acc)

        dxv = dx_ref[...]
        st_ref[0:1, :] += jnp.sum(dxv * y_ref[...].astype(F32), axis=0, keepdims=True)
        dyy = (dxv * v_ref[2:3, :]).astype(BF16)
        dyc_ref[...] = _dot_nt(dyy, w_ref[...])
        acc[0:GROUP_W, :] += _dot_tn(yr_ref[...], dyy)
        acc[GROUP_W:, :] += _dot_tn(ys_ref[...], dyy)

        @pl.when(i == n - 1)
        def _():
            dw_ref[...] = acc[...].astype(BF16)

    row = lambda w: pl.BlockSpec((tm, w), lambda i: (i, 0))
    fixed = lambda r: pl.BlockSpec((r, D), lambda i: (0, 0))
    return pl.pallas_call(
        body, name="outproj_bwd", grid=(n,),
        in_specs=[row(D), row(D), fixed(8), row(GROUP_W), row(GROUP_W), fixed(D)],
        out_specs=[row(D), fixed(D), fixed(8)],
        out_shape=[jax.ShapeDtypeStruct((S, D), F32), jax.ShapeDtypeStruct((D, D), BF16),
                   jax.ShapeDtypeStruct((8, D), F32)],
        scratch_shapes=[pltpu.VMEM((D, D), F32)],
        compiler_params=_cp("arbitrary"),
    )(dx, y, vecs, y_ret, y_sb, w_out)


def inproj_bwd_x(pieces, w3, x, vecs, dx_res, ship=None, tm=512):
    S, D = x.shape
    n = S // tm
    ex = _Exchange(ship)

    def body(*refs):
        p_refs, (w_ref, x_ref, v_ref, dr_ref), refs = refs[:8], refs[8:12], refs[12:]
        ship_refs, (dx_ref, st_ref), refs = refs[:ex.n_in], refs[ex.n_in:ex.n_in + 2], refs[ex.n_in + 2:]
        start, finish = ex.ops(ship_refs, refs)

        @pl.when(pl.program_id(0) == 0)
        def _():
            st_ref[...] = jnp.zeros_like(st_ref)
            start()

        dh = jnp.zeros((tm, D), F32)
        for k, p_ref in enumerate(p_refs):
            c0 = (k % 2) * GROUP_W
            dh = dh + _dot_nt(p_ref[...], w_ref[k // 2, :, c0:c0 + GROUP_W])
        xv = x_ref[...]
        r = lax.rsqrt(_rowmean(xv * xv) + EPS)
        xn = xv * r
        g, scale1 = v_ref[3:4, :], 1.0 + v_ref[1:2, :]
        st_ref[0:1, :] += jnp.sum(dh, axis=0, keepdims=True)
        dh_xn = dh * xn
        st_ref[1:2, :] += jnp.sum(dh_xn, axis=0, keepdims=True) * g
        st_ref[2:3, :] += jnp.sum(dh_xn, axis=0, keepdims=True) * scale1
        dxn = dh * (g * scale1)
        dx_ref[...] = r * (dxn - xn * _rowmean(dxn * xn)) + dr_ref[...]
        pl.when(pl.program_id(0) == n - 1)(finish)

    row = lambda w: pl.BlockSpec((tm, w), lambda i: (i, 0))
    return pl.pallas_call(
        body, name="inproj_bwd_x", grid=(n,),
        in_specs=[row(GROUP_W)] * 8 + [pl.BlockSpec((N_SHARD, D, SHARD_W), lambda i: (0, 0, 0)),
                                       row(D), pl.BlockSpec((8, D), lambda i: (0, 0)), row(D)] + ex.in_specs,
        out_specs=[row(D), pl.BlockSpec((8, D), lambda i: (0, 0))] + ex.out_specs,
        out_shape=[jax.ShapeDtypeStruct((S, D), F32), jax.ShapeDtypeStruct((8, D), F32)] + ex.out_shape,
        scratch_shapes=ex.scratch,
        compiler_params=_cp("arbitrary"),
    )(*pieces, w3, x, vecs, dx_res, *ex.ship)


def inproj_bwd_w(h, pieces, tm=1024):
    S, D = h.shape
    tm = min(tm, S)
    n = S // tm

    def body(*refs):
        h_ref, p_refs, dw_ref, acc = refs[0], refs[1:9], refs[9], refs[10]
        i = pl.program_id(0)

        @pl.when(i == 0)
        def _():
            acc[...] = jnp.zeros_like(acc)

        hv = h_ref[...]
        for k, p_ref in enumerate(p_refs):
            c0 = (k % 2) * GROUP_W
            acc[k // 2, :, c0:c0 + GROUP_W] += _dot_tn(hv, p_ref[...])

        @pl.when(i == n - 1)
        def _():
            dw_ref[...] = acc[...].astype(BF16)

    row = lambda w: pl.BlockSpec((tm, w), lambda i: (i, 0))
    return pl.pallas_call(
        body, name="inproj_bwd_w", grid=(n,),
        in_specs=[row(D)] + [row(GROUP_W)] * 8,
        out_specs=pl.BlockSpec((N_SHARD, D, SHARD_W), lambda i: (0, 0, 0), pipeline_mode=pl.Buffered(1)),
        out_shape=jax.ShapeDtypeStruct((N_SHARD, D, SHARD_W), BF16),
        scratch_shapes=[pltpu.VMEM((N_SHARD, D, SHARD_W), F32)],
        compiler_params=_cp("arbitrary"),
    )(h, *pieces)


def layer_fwd(x, vecs, w3, w_out, tabs, gather=None, head=None):
    cosf, sinf, lgam = tabs
    ret, sg, h, sb = inproj_fwd(x, vecs, w3)
    y_ret, o_ret, states = ret_fwd(ret, cosf, sinf, lgam)
    y_sb, o_sb, sb_end, *gathered = sb_fwd(sb, sg, gather)
    if callable(w_out):
        w_out = w_out(gathered[0])
    y, *x_next = outproj_fwd(x, vecs, y_ret, y_sb, w_out, head)
    saved = (x, ret, sg, h, sb, y_ret, o_ret, states, y_sb, o_sb, sb_end, y)
    return (x_next[0] if head is None else x_next), saved, (gathered[0] if gathered else None)


def _by_shard(dw_out):
    return dw_out.reshape(N_SHARD, D_MODEL // N_SHARD, D_MODEL)


def layer_bwd(dx, saved, vecs, w3, w_out, tabs, later_grads=None):
    cosf, sinf, lgam = tabs
    x, ret, sg, h, sb, y_ret, o_ret, states, y_sb, o_sb, sb_end, y = saved
    dycat, dw_out, st_o = outproj_bwd(dx, y, vecs, y_ret, y_sb, w_out)
    dw_out = _by_shard(dw_out)
    ship = None if later_grads is None else (later_grads[0], dw_out, later_grads[1])
    *d_sb, = sb_bwd(sb, sg, o_sb, sb_end, dycat, ship)
    d_ret = ret_bwd(ret, cosf, sinf, lgam, o_ret, states, dycat)
    pieces = list(d_ret) + d_sb[:4]
    dw_in = inproj_bwd_w(h, pieces)
    dx, st_i, *recv_in = inproj_bwd_x(pieces, w3, x, vecs, dx, None if later_grads is None else (dw_in,))
    dmod = jnp.concatenate([st_i[0:2], st_o[0:1]], axis=0)
    grads = (dw_in, dw_out) if later_grads is None else (recv_in[0], d_sb[4])
    return dx, dmod, st_i[2:3], grads


def _place():
    return lax.axis_index("x"), lax.axis_index("y"), lax.axis_index("c")


def _other_chips(mx, my):
    return [(1 - mx, my), (mx, 1 - my), (1 - mx, 1 - my)]


_ANY = pl.BlockSpec(memory_space=pl.ANY)


_GATHER_SCRATCH = [pltpu.SemaphoreType.DMA((7,)), pltpu.SemaphoreType.DMA((7,)), pltpu.SemaphoreType.DMA(())]


def _gather_ops(x_ref, out_ref, send_sems, recv_sems, local_sem):
    mx, my, mc = _place()
    me, sibling = (mx, my, mc), (mx, my, 1 - mc)
    chips = _other_chips(mx, my)

    def slot(px, py, pc):
        return out_ref.at[4 * px + 2 * py + pc]

    def copy(k, block, to, src=None):
        return pltpu.make_async_remote_copy(
            src_ref=slot(*block) if src is None else src, dst_ref=slot(*block),
            send_sem=send_sems.at[k], recv_sem=recv_sems.at[k], device_id=to, device_id_type=MESH)

    mine = pltpu.make_async_copy(x_ref, slot(*me), local_sem)
    first = [copy(0, me, sibling, src=x_ref)]
    first += [copy(1 + j, me, (*chip, mc), src=x_ref) for j, chip in enumerate(chips)]
    passed = [copy(4 + j, (*chip, mc), sibling) for j, chip in enumerate(chips)]

    def start():
        mine.start()
        for cp in first:
            cp.start()

    def forward():
        for j, chip in enumerate(chips):
            copy(1 + j, (*chip, mc), me).wait_recv()
            passed[j].start()

    def finish():
        copy(0, sibling, me).wait_recv()
        for j, chip in enumerate(chips):
            copy(4 + j, (*chip, 1 - mc), me).wait_recv()
        for cp in first + passed:
            cp.wait_send()
        mine.wait()

    return start, forward, finish


def allgather8(x, name):
    def body(x_ref, out_ref, send_sems, recv_sems, local_sem):
        for step in _gather_ops(x_ref, out_ref, send_sems, recv_sems, local_sem):
            step()

    return pl.pallas_call(
        body, name=name, out_shape=jax.ShapeDtypeStruct((8,) + x.shape, x.dtype),
        in_specs=[_ANY], out_specs=_ANY, scratch_shapes=_GATHER_SCRATCH,
    )(x)


class _Exchange:
    def __init__(self, ship):
        self.ship = list(ship or ())
        self.n_in = len(self.ship)
        self.n_out = 1 if self.ship else 0
        self.rows = [a.shape[1] for a in self.ship]
        self.in_specs = [_ANY] * self.n_in
        self.out_specs = [_ANY] * self.n_out
        self.out_shape = [jax.ShapeDtypeStruct((N_SHARD, sum(self.rows), SHARD_W), BF16)] * self.n_out
        sem = pltpu.SemaphoreType.DMA
        self.scratch = [sem((3,)), sem((3,)), sem(())] * self.n_out

    def ops(self, ship_refs, tail):
        if not self.ship:
            return (lambda: None), (lambda: None)
        recv, send_sems, recv_sems, local_sem = tail
        mx, my, mc = _place()
        my_chip = 2 * mx + my
        chips = _other_chips(mx, my)

        def pieces(s):
            firsts = np.cumsum([0] + self.rows[:-1])
            return [(ref.at[s], int(r0), n) for ref, r0, n in zip(ship_refs, firsts, self.rows)]

        def start():
            for src, r0, n in pieces(my_chip):
                pltpu.make_async_copy(src, recv.at[my_chip, pl.ds(r0, n)], local_sem).start()
            for j, (px, py) in enumerate(chips):
                for src, r0, n in pieces(2 * px + py):
                    pltpu.make_async_remote_copy(
                        src_ref=src, dst_ref=recv.at[my_chip, pl.ds(r0, n)],
                        send_sem=send_sems.at[j], recv_sem=recv_sems.at[j],
                        device_id=(px, py, mc), device_id_type=MESH).start()

        def finish():
            for j, (px, py) in enumerate(chips):
                whole = recv.at[2 * px + py]
                both = pltpu.make_async_remote_copy(
                    src_ref=whole, dst_ref=whole, send_sem=send_sems.at[j], recv_sem=recv_sems.at[j],
                    device_id=(px, py, mc), device_id_type=MESH)
                both.wait_recv()
                both.wait_send()
            pltpu.make_async_copy(recv.at[my_chip], recv.at[my_chip], local_sem).wait()

        return start, finish


def sum_slots(recv_a, recv_b, tr=256):
    n, rows_a, cols = recv_a.shape
    na, nb = rows_a // tr, recv_b.shape[1] // tr

    def body(a_ref, b_ref, o_ref):
        def total(r_ref):
            acc = r_ref[0].astype(F32)
            for k in range(1, n):
                acc = acc + r_ref[k].astype(F32)
            o_ref[...] = acc

        pl.when(pl.program_id(0) < na)(lambda: total(a_ref))
        pl.when(pl.program_id(0) >= na)(lambda: total(b_ref))

    return pl.pallas_call(
        body, name="sum_slots", grid=(na + nb,),
        in_specs=[pl.BlockSpec((n, tr, cols), lambda i: (0, jnp.minimum(i, na - 1), 0)),
                  pl.BlockSpec((n, tr, cols), lambda i: (0, jnp.maximum(i - na, 0), 0))],
        out_specs=pl.BlockSpec((tr, cols), lambda i: (i, 0)),
        out_shape=jax.ShapeDtypeStruct(((na + nb) * tr, cols), F32),
        compiler_params=_cp("arbitrary"),
    )(recv_a, recv_b)


def swap_sibling(p):
    def body(p_ref, out_ref, send_sem, recv_sem):
        mx, my, mc = _place()
        cp = pltpu.make_async_remote_copy(
            src_ref=p_ref, dst_ref=out_ref, send_sem=send_sem, recv_sem=recv_sem,
            device_id=(mx, my, 1 - mc), device_id_type=MESH)
        cp.start()
        cp.wait()

    return pl.pallas_call(
        body, name="swap_sibling", out_shape=jax.ShapeDtypeStruct(p.shape, p.dtype),
        in_specs=[_ANY], out_specs=_ANY,
        scratch_shapes=[pltpu.SemaphoreType.DMA(()), pltpu.SemaphoreType.DMA(())],
    )(p)


def _adamw(w, g, m, v):
    m = ADAM_B1 * m + (1.0 - ADAM_B1) * g
    v = ADAM_B2 * v + (1.0 - ADAM_B2) * (g * g)
    m_hat = m / (1.0 - ADAM_B1 ** ADAM_STEP)
    v_hat = v / (1.0 - ADAM_B2 ** ADAM_STEP)
    delta = -ADAM_LR * (m_hat / (jnp.sqrt(v_hat) + ADAM_EPS) + ADAM_WD * w)
    return delta, m, v


def adam_slab(p_own, p_sib, w, m, v, row0, name, tr=256):
    L, R, C = w.shape
    nr = R // tr

    def body(a_ref, b_ref, w_ref, m_ref, v_ref, g_out, d_out, m_out, v_out):
        g = a_ref[...] + b_ref[...]
        d, m2, v2 = _adamw(w_ref[0], g, m_ref[0], v_ref[0])
        g_out[0], d_out[0], m_out[0], v_out[0] = g, d, m2, v2

    slab = pl.BlockSpec((tr, C), lambda l, i: (row0 // tr + l * nr + i, 0))
    blk = pl.BlockSpec((1, tr, C), lambda l, i: (l, i, 0))
    return pl.pallas_call(
        body, name=name, grid=(L, nr),
        in_specs=[slab, slab, blk, blk, blk], out_specs=[blk] * 4,
        out_shape=[jax.ShapeDtypeStruct(w.shape, F32)] * 4,
        compiler_params=_cp("arbitrary", "arbitrary"),
    )(p_own, p_sib, w, m, v)


def ada_fwd(c_all, w_ada):
    L, D, W = w_ada.shape

    def body(c_ref, w_ref, o_ref):
        cv = c_ref[...]
        o_ref[0] = jnp.dot(cv * _sigmoid(cv), w_ref[0], precision=lax.Precision.HIGHEST,
                           preferred_element_type=F32)

    return pl.pallas_call(
        body, name="ada_fwd", grid=(L,),
        in_specs=[pl.BlockSpec((8, D), lambda l: (0, 0)), pl.BlockSpec((1, D, W), lambda l: (l, 0, 0))],
        out_specs=pl.BlockSpec((1, 8, W), lambda l: (l, 0, 0)),
        out_shape=jax.ShapeDtypeStruct((L, 8, W), F32),
        compiler_params=_cp("arbitrary"),
    )(c_all, w_ada)


def vecs_build(mod_all, b_ada, norm_g):
    W = mod_all.shape[2]

    def body(m_ref, b_ref, g_ref, o_ref):
        mx, my, mc = _place()
        me = 4 * mx + 2 * my + mc
        rowid = lax.broadcasted_iota(jnp.int32, (2 * 8, 1), 0)
        o_ref[...] = jnp.zeros_like(o_ref)
        for l in range(DEPTH):
            parts = [jnp.sum(jnp.where(rowid == l * 8 + me, m_ref[2 * s + mc], 0.0), axis=0, keepdims=True)
                     for s in range(N_SHARD)]
            mod = jnp.concatenate(parts, axis=1) + b_ref[l:l + 1, :]
            for t in range(3):
                o_ref[l, t:t + 1, :] = mod[:, t * D_MODEL:(t + 1) * D_MODEL]
            o_ref[l, 3:4, :] = g_ref[l:l + 1, :]

    return pl.pallas_call(
        body, name="vecs_build", out_shape=jax.ShapeDtypeStruct((DEPTH, 8, D_MODEL), F32),
    )(mod_all, b_ada, norm_g)


def ada_update(dmods, c_t, w, m, v, tr=256):
    L, D, W = w.shape

    def body(dm_ref, c_ref, w_ref, m_ref, v_ref, g_out, d_out, m_out, v_out):
        mx, my, _ = _place()
        shard = 2 * mx + my
        dm = jnp.zeros((8, W), F32)
        for s in range(N_SHARD):
            dm = dm + jnp.where(shard == s, dm_ref[0, :, s * W:(s + 1) * W], 0.0)
        cv = c_ref[...]
        ca = cv * _sigmoid(cv)
        g = jnp.zeros((tr, W), F32)
        for b in range(8):
            g = g + ca[:, b:b + 1] * dm[b:b + 1, :]
        d, m2, v2 = _adamw(w_ref[0], g, m_ref[0], v_ref[0])
        g_out[0], d_out[0], m_out[0], v_out[0] = g, d, m2, v2

    blk = pl.BlockSpec((1, tr, W), lambda l, i: (l, i, 0))
    return pl.pallas_call(
        body, name="ada_update", grid=(L, D // tr),
        in_specs=[pl.BlockSpec((1, 8, 3 * D), lambda l, i: (l, 0, 0)), pl.BlockSpec((tr, 8), lambda l, i: (i, 0)),
                  blk, blk, blk],
        out_specs=[blk] * 4, out_shape=[jax.ShapeDtypeStruct(w.shape, F32)] * 4,
        compiler_params=_cp("arbitrary", "arbitrary"),
    )(dmods, c_t, w, m, v)


STAT_ROWS = 16


def small_update(stats_all, norm, b_ada, final):
    def body(s_ref, *refs):
        ins, outs = refs[:9], refs[9:]
        tot = s_ref[0]
        for k in range(1, 8):
            tot = tot + s_ref[k]
        g_norm = tot[0:2, :]
        g_final = tot[2:3, :]
        g_b = jnp.concatenate(
            [jnp.concatenate([tot[3 + 3 * l + t:4 + 3 * l + t, :] for t in range(3)], axis=1) for l in range(DEPTH)],
            axis=0)
        for p, g in enumerate((g_norm, g_b, g_final)):
            w_ref, m_ref, v_ref = ins[3 * p:3 * p + 3]
            d, m2, v2 = _adamw(w_ref[...], g, m_ref[...], v_ref[...])
            for o_ref, val in zip(outs[4 * p:4 * p + 4], (g, d, m2, v2)):
                o_ref[...] = val
        loss = (0.5 / D_MODEL) * jnp.sum(tot[9:10, :], axis=1, keepdims=True)
        outs[12][...] = jnp.broadcast_to(loss, (8, LANES))

    shapes = []
    for w, _, _ in (norm, b_ada, final):
        shapes += [jax.ShapeDtypeStruct(w.shape, F32)] * 4
    shapes.append(jax.ShapeDtypeStruct((8, LANES), F32))
    return pl.pallas_call(body, name="small_update", out_shape=shapes)(stats_all, *norm, *b_ada, *final)


def kernel(x, c, norm_g, w_ada, b_ada, w_in, w_out, final_g, loss_target, m_norm_g, m_w_ada, m_b_ada, m_w_in, m_w_out, m_final_g, v_norm_g, v_w_ada, v_b_ada, v_w_in, v_w_out, v_final_g):
    S, D = x.shape[1], x.shape[2]
    mc = lax.axis_index("c")
    out_rows = D // N_SHARD

    def my_half(a, rows):
        return lax.dynamic_slice_in_dim(a, mc * rows, rows, axis=0)

    assert DEPTH == 2
    win = [my_half(w_in[l], D // 2).astype(BF16) for l in range(DEPTH)]
    wout = [my_half(w_out[l], out_rows // 2).astype(BF16) for l in range(DEPTH)]
    w3_first = allgather8(win[0], "gather_weights").reshape(N_SHARD, D, SHARD_W)
    rest = jnp.concatenate([wout[0], win[1], wout[1]], axis=0)

    def unpack(wall):
        wall = wall.reshape(N_SHARD, 2, rest.shape[0], SHARD_W)
        a, b = out_rows // 2, out_rows // 2 + D // 2
        return wall[:, :, :a].reshape(D, D), (wall[:, :, a:b].reshape(N_SHARD, D, SHARD_W), wall[:, :, b:].reshape(D, D))

    c_all = allgather8(jnp.broadcast_to(c, (8, D)), "gather_c")[:, 0, :]
    mod_all = allgather8(ada_fwd(c_all, w_ada).reshape(DEPTH * 8, -1), "gather_mod")
    vecs = vecs_build(mod_all, b_ada, norm_g)

    tabs = (*rope_tables(S), ret_log_gamma())
    saved = [None] * DEPTH
    h, saved[0], wall = layer_fwd(x[0], vecs[0], w3_first, lambda g: unpack(g)[0], tabs, rest)
    weights = [(w3_first, unpack(wall)[0]), unpack(wall)[1]]
    head = (jnp.broadcast_to(final_g[None, :], (8, D)), loss_target[0])
    (dx, st_loss), saved[1], _ = layer_fwd(h, vecs[1], *weights[1], tabs, head=head)

    dmod, dnorm, grads = [None] * DEPTH, [None] * DEPTH, None
    for l in reversed(range(DEPTH)):
        dx, dmod[l], dnorm[l], grads = layer_bwd(dx, saved[l], vecs[l], *weights[l], tabs, grads)

    p_own = sum_slots(*grads)
    p_sib = swap_sibling(p_own)
    res_in = adam_slab(p_own, p_sib, w_in, m_w_in, v_w_in, 0, "adam_w_in")
    res_out = adam_slab(p_own, p_sib, w_out, m_w_out, v_w_out, DEPTH * D, "adam_w_out", tr=128)

    stats = jnp.concatenate(dnorm + [st_loss[0:1]] + dmod + [st_loss[1:2], jnp.zeros((STAT_ROWS - 10, D), F32)], axis=0)
    stats_all = allgather8(stats, "gather_stats")
    dmods = stats_all[:, 3:9, :].reshape(8, DEPTH, 3 * D).transpose(1, 0, 2)
    res_ada = ada_update(dmods, c_all.T, w_ada, m_w_ada, v_w_ada)
    small = small_update(stats_all, (norm_g, m_norm_g, v_norm_g), (b_ada, m_b_ada, v_b_ada),
                         (final_g[None, :], m_final_g[None, :], v_final_g[None, :]))
    res_norm, res_b, res_final = small[0:4], small[4:8], [a[0] for a in small[8:12]]
    loss = small[12][0, 0]

    by_kind = [res_norm, res_ada, res_b, res_in, res_out, res_final]
    outs = [loss, dx[None]]
    for kind in range(4):
        outs += [r[kind] for r in by_kind]
    return tuple(outs)
```

```python
import functools

import numpy as np
import jax
import jax.numpy as jnp
from jax import lax
from jax.experimental import pallas as pl
from jax.experimental.pallas import tpu as pltpu

F32, BF16 = jnp.float32, jnp.bfloat16
MESH = pl.DeviceIdType.MESH

D_MODEL = 1024
DEPTH = 2
SHARD_W = 1024
N_SHARD = 4
GROUP_W = 512
LANES = 128
SB_HEAD_DIM = 64
RET_HEAD_DIM = 128
CHUNK = 64
ROPE_BASE = 10000.0
EPS = 1e-6
SQ_SCALE = SB_HEAD_DIM ** -0.5
RK_SCALE = RET_HEAD_DIM ** -0.5
SB_T = 512
SB_CHAINS = 8
SB_NB = 4
RET_T = 256
EXP_ZERO = -104.0
VMEM_LIMIT_BYTES = 56 * 2 ** 20

ADAM_LR, ADAM_B1, ADAM_B2, ADAM_EPS, ADAM_WD, ADAM_STEP = 0.001, 0.9, 0.999, 1e-08, 0.01, 10


def _cp(*sem):
    return pltpu.CompilerParams(dimension_semantics=sem, vmem_limit_bytes=VMEM_LIMIT_BYTES)


def _dot(a, b):
    return lax.dot_general(a, b, (((1,), (0,)), ((), ())), preferred_element_type=F32)


def _dot_nt(a, b):
    return lax.dot_general(a, b, (((1,), (1,)), ((), ())), preferred_element_type=F32)


def _dot_tn(a, b):
    return lax.dot_general(a, b, (((0,), (0,)), ((), ())), preferred_element_type=F32)


def _running_sum(a, tri):
    return _dot(a.astype(BF16), tri)


def _sigmoid(x):
    return 1.0 / (1.0 + jnp.exp(-x))


def _rowsum(a):
    return jnp.sum(a, axis=1, keepdims=True)


def _rowmean(a):
    return jnp.mean(a, axis=1, keepdims=True)


def inproj_fwd(x, vecs, w3, tm=512):
    S, D = x.shape

    def body(x_ref, v_ref, w_ref, ret_ref, sg_ref, h_ref, sb_ref):
        xv = x_ref[...]
        r = lax.rsqrt(_rowmean(xv * xv) + EPS)
        h = xv * r * v_ref[3:4, :] * (1.0 + v_ref[1:2, :]) + v_ref[0:1, :]
        hb = h.astype(BF16)
        h_ref[...] = hb
        for s in range(N_SHARD):
            p = _dot(hb, w_ref[s])
            if s < 2:
                ret_ref[:, s * SHARD_W:(s + 1) * SHARD_W] = p
            if s == 2:
                sb_ref[:, 0:GROUP_W] = (p[:, 0:GROUP_W] * SQ_SCALE).astype(BF16)
                sb_ref[:, GROUP_W:SHARD_W] = p[:, GROUP_W:].astype(BF16)
            if s == 3:
                sb_ref[:, SHARD_W:SHARD_W + GROUP_W] = p[:, 0:GROUP_W].astype(BF16)
                sg_ref[...] = p[:, GROUP_W:]

    row = lambda w: pl.BlockSpec((tm, w), lambda i: (i, 0))
    return pl.pallas_call(
        body, name="inproj_fwd", grid=(S // tm,),
        in_specs=[row(D), pl.BlockSpec((8, D), lambda i: (0, 0)),
                  pl.BlockSpec((N_SHARD, D, SHARD_W), lambda i: (0, 0, 0))],
        out_specs=[row(2 * SHARD_W), row(GROUP_W), row(D), row(3 * GROUP_W)],
        out_shape=[jax.ShapeDtypeStruct((S, 2 * SHARD_W), F32), jax.ShapeDtypeStruct((S, GROUP_W), F32),
                   jax.ShapeDtypeStruct((S, D), BF16), jax.ShapeDtypeStruct((S, 3 * GROUP_W), BF16)],
        compiler_params=_cp("arbitrary"),
    )(x, vecs, w3)


def _sb_logits(qh, k2, keep):
    z = _dot_nt(qh, k2)
    sp = jnp.log(1.0 + jnp.exp(-jnp.abs(z)))
    lb = jnp.minimum(z, 0.0) - sp
    lk = lb - z
    if keep is not None:
        lk = jnp.where(keep, lk, 0.0)
    return lb, lk


class _sb_chains:
    def __init__(self, i, q2, do_b=None):
        t = self.t = SB_T // SB_CHAINS
        self.C = range(SB_CHAINS)
        r = lax.broadcasted_iota(jnp.int32, (SB_NB * t, SB_NB * t), 0)
        c = lax.broadcasted_iota(jnp.int32, (SB_NB * t, SB_NB * t), 1)
        self.later_all = jnp.where(r > c, 1.0, 0.0).astype(BF16)
        self.earlier_all = jnp.where(r < c, 1.0, 0.0).astype(BF16)
        self.later, self.earlier = self.later_all[:t, :t], self.earlier_all[:t, :t]
        self.head0 = lax.broadcasted_iota(jnp.int32, (1, LANES), 1) < SB_HEAD_DIM
        row = lax.broadcasted_iota(jnp.int32, (2 * t, SB_NB * t), 0) & (t - 1)
        col = lax.broadcasted_iota(jnp.int32, (2 * t, SB_NB * t), 1)
        qt = [SB_CHAINS * i + cc for cc in self.C]
        self.first = [jnp.maximum(qt[cc] - (SB_NB - 1), 0) for cc in self.C]
        self.keep = [self.first[cc] * t + col < qt[cc] * t + row for cc in self.C]
        self.qs = [self._stack(q2[cc * t:(cc + 1) * t]) for cc in self.C]
        if do_b is not None:
            self.dos = [self._stack(do_b[cc * t:(cc + 1) * t]) for cc in self.C]

    def _stack(self, a):
        zero = jnp.zeros_like(a)
        return jnp.concatenate([jnp.where(self.head0, a, zero), jnp.where(self.head0, zero, a)], axis=0)

    def rows(self, ref, j, n):
        return ref[pl.ds(pl.multiple_of(j * self.t, self.t), n * self.t), :]

    def suffix(self, lk):
        return _running_sum(lk, self.later_all), _rowsum(lk)

    def prefix(self, g, G0):
        return _running_sum(g, self.earlier_all) + G0


def sb_fwd(sb, sg, gather=None):
    S = sb.shape[0]
    T = SB_T
    nq = S // T
    carried = [] if gather is None else [gather]

    def body(*refs):
        (q_ref, k_ref, v_ref, sg_ref), refs = refs[:4], refs[4:]
        p, i = pl.program_id(0), pl.program_id(1)
        if carried:
            x_ref, y_ref, o_ref, end_ref, out_ref, send_sems, recv_sems, local_sem = refs
            start, forward, finish = _gather_ops(x_ref, out_ref, send_sems, recv_sems, local_sem)
            pl.when(jnp.logical_and(p == 0, i == 0))(start)
            pl.when(jnp.logical_and(p == 2, i == 0))(forward)
        else:
            y_ref, o_ref, end_ref = refs
        ch = _sb_chains(i, q_ref[...])
        later, head0 = ch.later, ch.head0
        lbk = [_sb_logits(ch.qs[c], ch.rows(k_ref, ch.first[c], SB_NB), ch.keep[c]) for c in ch.C]
        suffix, R = zip(*[ch.suffix(lbk[c][1]) for c in ch.C])
        aa = [jnp.where(ch.keep[c], jnp.exp(lbk[c][0] + suffix[c]), 0.0) for c in ch.C]
        acc = [_dot(aa[c].astype(BF16), ch.rows(v_ref, ch.first[c], SB_NB)) for c in ch.C]

        outs = []
        for c in ch.C:
            def tile(j, Rc, c=c):
                lb, lk = _sb_logits(ch.qs[c], ch.rows(k_ref, j, 1), None)
                a = jnp.exp(lb + _running_sum(lk, later) + Rc)
                return _dot(a.astype(BF16), ch.rows(v_ref, j, 1)), Rc + _rowsum(lk)

            def cond(st):
                return jnp.logical_and(st[0] >= 0, st[3] > EXP_ZERO)

            def step(st, tile=tile):
                cx, Rn = tile(st[0], st[2])
                return st[0] - 1, st[1] + cx, Rn, jnp.max(Rn)

            j_end, ac, Rc, _ = lax.while_loop(cond, step, (ch.first[c] - 1, acc[c], R[c], jnp.max(R[c])))
            base = c * (2 * ch.t + 8)
            end_ref[0, 0, base:base + 2 * ch.t, :] = jnp.broadcast_to(Rc, (2 * ch.t, 8))
            end_ref[0, 0, base + 2 * ch.t:base + 2 * ch.t + 8, :] = jnp.full((8, 8), j_end.astype(F32))
            outs.append(jnp.where(head0, ac[:ch.t], ac[ch.t:]))
        o = jnp.concatenate(outs, axis=0)
        o_ref[...] = o
        sg = sg_ref[...]
        y_ref[...] = (o * (sg * _sigmoid(sg))).astype(BF16)
        if carried:
            pl.when(jnp.logical_and(p == 3, i == nq - 1))(finish)

    return pl.pallas_call(
        body, name="sb_fwd", grid=(4, nq),
        in_specs=[pl.BlockSpec((T, LANES), lambda p, i: (i, p)),
                  pl.BlockSpec((S, LANES), lambda p, i: (0, 4 + p)),
                  pl.BlockSpec((S, LANES), lambda p, i: (0, 8 + p)),
                  pl.BlockSpec((T, LANES), lambda p, i: (i, p))] + [_ANY for _ in carried],
        out_specs=[pl.BlockSpec((T, LANES), lambda p, i: (i, p)),
                   pl.BlockSpec((T, LANES), lambda p, i: (i, p)),
                   pl.BlockSpec((1, 1, SB_CHAINS * (2 * T // SB_CHAINS + 8), 8), lambda p, i: (p, i, 0, 0))] + [_ANY for _ in carried],
        out_shape=[jax.ShapeDtypeStruct((S, GROUP_W), BF16),
                   jax.ShapeDtypeStruct((S, GROUP_W), F32),
                   jax.ShapeDtypeStruct((4, nq, SB_CHAINS * (2 * T // SB_CHAINS + 8), 8), F32)]
        + [jax.ShapeDtypeStruct((8,) + a.shape, a.dtype) for a in carried],
        scratch_shapes=_GATHER_SCRATCH if carried else [],
        compiler_params=_cp("arbitrary", "arbitrary"),
    )(sb, sb, sb, sg, *carried)


def sb_bwd(sb, sg, o, sb_end, dycat, ship=None):
    S = sb.shape[0]
    T = SB_T
    nq = S // T
    ex = _Exchange(ship)

    def body(*refs):
        (q_ref, k_ref, v_ref, sg_ref, o_ref, dy_ref, end_ref), refs = refs[:7], refs[7:]
        ship_refs, (dq_ref, dk_ref, dv_ref, dsg_ref), refs = refs[:ex.n_in], refs[ex.n_in:ex.n_in + 4], refs[ex.n_in + 4:]
        recv, (dk_acc, dv_acc), sems = refs[:ex.n_out], refs[ex.n_out:ex.n_out + 2], refs[ex.n_out + 2:]
        start, finish = ex.ops(ship_refs, recv + sems)
        p, i = pl.program_id(0), pl.program_id(1)
        pl.when(jnp.logical_and(p == 0, i == 0))(start)

        @pl.when(i == 0)
        def _():
            dk_acc[...] = jnp.zeros_like(dk_acc)
            dv_acc[...] = jnp.zeros_like(dv_acc)

        sg = sg_ref[...]
        sig = _sigmoid(sg)
        dy = dy_ref[...]
        dsg_ref[...] = (dy * o_ref[...] * (sig * (1.0 + sg * (1.0 - sig)))).astype(BF16)
        do_b = (dy * (sg * sig)).astype(BF16)
        ch = _sb_chains(i, q_ref[...], do_b)
        later, earlier, head0, t = ch.later, ch.earlier, ch.head0, ch.t
        end = end_ref[0, 0]

        def grads(c, j, n, a, lb, g, G, keep):
            dz = g - jnp.exp(lb) * (g + G)
            if keep is not None:
                dz = jnp.where(keep, dz, 0.0)
            dzb = dz.astype(BF16)
            rows = pl.ds(pl.multiple_of(j * t, t), n * t)
            dk_acc[rows, :] += _dot_tn(dzb, ch.qs[c])
            dv_acc[rows, :] += _dot_tn(a.astype(BF16), ch.dos[c])
            return _dot(dzb, ch.rows(k_ref, j, n))

        dq, G0 = [], []
        for c in ch.C:
            base = c * (2 * t + 8)
            j_end = jnp.max(end[base + 2 * t:base + 2 * t + 8, :]).astype(jnp.int32)

            def sweep(j, st, c=c):
                dqc, G0c, left = st
                lb, lk = _sb_logits(ch.qs[c], ch.rows(k_ref, j, 1), None)
                stick = left - _rowsum(lk)
                a = jnp.exp(lb + _running_sum(lk, later) + stick)
                g = a * _dot_nt(ch.dos[c], ch.rows(v_ref, j, 1))
                G = _running_sum(g, earlier) + G0c
                return dqc + grads(c, j, 1, a, lb, g, G, None), G0c + _rowsum(g), stick

            st = lax.fori_loop(j_end + 1, ch.first[c], sweep,
                               (jnp.zeros((2 * t, LANES), F32), jnp.zeros((2 * t, 1), F32), end[base:base + 2 * t, 0:1]))
            dq.append(st[0])
            G0.append(st[1])

        lbk = [_sb_logits(ch.qs[c], ch.rows(k_ref, ch.first[c], SB_NB), ch.keep[c]) for c in ch.C]
        suffix = [ch.suffix(lbk[c][1])[0] for c in ch.C]
        aa = [jnp.where(ch.keep[c], jnp.exp(lbk[c][0] + suffix[c]), 0.0) for c in ch.C]
        g = [aa[c] * _dot_nt(ch.dos[c], ch.rows(v_ref, ch.first[c], SB_NB)) for c in ch.C]
        G = [ch.prefix(g[c], G0[c]) for c in ch.C]
        for c in ch.C:
            dqc = dq[c] + grads(c, ch.first[c], SB_NB, aa[c], lbk[c][0], g[c], G[c], ch.keep[c])
            dq_ref[c * t:(c + 1) * t, :] = (jnp.where(head0, dqc[:t], dqc[t:]) * SQ_SCALE).astype(BF16)

        @pl.when(i == nq - 1)
        def _():
            dk_ref[...] = dk_acc[...].astype(BF16)
            dv_ref[...] = dv_acc[...].astype(BF16)

        pl.when(jnp.logical_and(p == 3, i == nq - 1))(finish)

    tile_spec = lambda c0: pl.BlockSpec((T, LANES), lambda p, i: (i, c0 + p))
    head_spec = lambda c0: pl.BlockSpec((S, LANES), lambda p, i: (0, c0 + p))
    return pl.pallas_call(
        body, name="sb_bwd", grid=(4, nq),
        in_specs=[tile_spec(0), head_spec(4), head_spec(8), tile_spec(0), tile_spec(0), tile_spec(4),
                  pl.BlockSpec((1, 1, SB_CHAINS * (2 * T // SB_CHAINS + 8), 8), lambda p, i: (p, i, 0, 0))] + ex.in_specs,
        out_specs=[tile_spec(0), head_spec(0), head_spec(0), tile_spec(0)] + ex.out_specs,
        out_shape=[jax.ShapeDtypeStruct((S, GROUP_W), BF16)] * 4 + ex.out_shape,
        scratch_shapes=[pltpu.VMEM((S, LANES), F32), pltpu.VMEM((S, LANES), F32)] + ex.scratch,
        compiler_params=_cp("arbitrary", "arbitrary"),
    )(sb, sb, sb, sg, o, dycat, sb_end, *ex.ship)


def rope_tables(S):
    half = RET_HEAD_DIM // 2
    inv = ROPE_BASE ** (-jnp.arange(half, dtype=F32) / half)
    ang = jnp.arange(S, dtype=F32)[:, None] * inv[None, :]
    cos, sin = jnp.cos(ang), jnp.sin(ang)
    return jnp.concatenate([cos, cos], axis=1), jnp.concatenate([-sin, sin], axis=1)


def ret_log_gamma():
    return jnp.log1p(-(2.0 ** (-5.0 - jnp.arange(4, dtype=F32))))


def _swap_halves(a):
    return pltpu.roll(a, RET_HEAD_DIM // 2, axis=1)


def _ret_decay_mask(lg):
    n = lax.broadcasted_iota(jnp.int32, (RET_T, RET_T), 0)
    m = lax.broadcasted_iota(jnp.int32, (RET_T, RET_T), 1)
    dist = jnp.abs(n - m).astype(F32)
    return jnp.where((m // CHUNK) <= (n // CHUNK), jnp.exp(lg * dist), 0.0)


def _ret_block(lg, rq, rk, rv, cosf, sinf, dm):
    q = rq * cosf + _swap_halves(rq) * sinf
    k = (rk * cosf + _swap_halves(rk) * sinf) * RK_SCALE
    qb, kb, vb = q.astype(BF16), k.astype(BF16), rv.astype(BF16)
    sc = _dot_nt(qb, kb) * dm
    nloc = lax.broadcasted_iota(jnp.int32, (RET_T, 1), 0).astype(F32)
    qdec = jnp.exp(lg * (nloc + 1.0))
    kdec = jnp.exp(lg * (RET_T - 1.0 - nloc))
    block_dec = jnp.exp(jnp.full((1, LANES), lg * RET_T, F32))
    return q, k, qb, kb, vb, sc, qdec, kdec, block_dec


def _ret_specs(S, rb):
    group = lambda c0: pl.BlockSpec((RET_T, GROUP_W), lambda b: (rb(b), c0))
    return group, pl.BlockSpec((RET_T, LANES), lambda b: (rb(b), 0))


def _head(ref, h):
    return ref[:, h * LANES:(h + 1) * LANES]


def ret_fwd(proj, cosf, sinf, lgam):
    S = proj.shape[0]
    nb = S // RET_T
    group, row_tab = _ret_specs(S, lambda b: b)

    def body(lg_ref, rq_ref, rk_ref, rv_ref, rg_ref, cos_ref, sin_ref, y_ref, o_ref, st_out, st_ref, dm_ref):
        @pl.when(pl.program_id(0) == 0)
        def _():
            st_ref[...] = jnp.zeros_like(st_ref)
            for h in range(4):
                dm_ref[h] = _ret_decay_mask(lg_ref[h])

        cosf, sinf = cos_ref[...], sin_ref[...]
        H = range(4)
        lanes = [slice(h * LANES, (h + 1) * LANES) for h in H]
        blk = [_ret_block(lg_ref[h], _head(rq_ref, h), _head(rk_ref, h), _head(rv_ref, h), cosf, sinf, dm_ref[h])
               for h in H]
        q, k, qb, kb, vb, sc, qdec, kdec, block_dec = zip(*blk)
        st = [st_ref[h] for h in H]
        for h in H:
            st_out[h, 0] = st[h]
        o = [_dot(sc[h].astype(BF16), vb[h]) + _dot(qb[h], st[h].astype(BF16)) * qdec[h] for h in H]
        for h in H:
            st_ref[h] = st[h] * block_dec[h] + _dot_tn((k[h] * kdec[h]).astype(BF16), vb[h])
        for h in H:
            o_ref[:, lanes[h]] = o[h]
        cen = [o[h] - _rowmean(o[h]) for h in H]
        on = [cen[h] * lax.rsqrt(_rowmean(cen[h] * cen[h]) + EPS) for h in H]
        rg = [_head(rg_ref, h) for h in H]
        for h in H:
            y_ref[:, lanes[h]] = (on[h] * (rg[h] * _sigmoid(rg[h]))).astype(BF16)

    return pl.pallas_call(
        body, name="ret_fwd", grid=(nb,),
        in_specs=[pl.BlockSpec(memory_space=pltpu.SMEM),
                  group(0), group(1), group(2), group(3), row_tab, row_tab],
        out_specs=[group(0), group(0),
                   pl.BlockSpec((4, 1, LANES, LANES), lambda b: (0, b, 0, 0))],
        out_shape=[jax.ShapeDtypeStruct((S, GROUP_W), BF16),
                   jax.ShapeDtypeStruct((S, GROUP_W), F32),
                   jax.ShapeDtypeStruct((4, nb, LANES, LANES), F32)],
        scratch_shapes=[pltpu.VMEM((4, LANES, LANES), F32), pltpu.VMEM((4, RET_T, RET_T), F32)],
        compiler_params=_cp("arbitrary"),
    )(lgam, proj, proj, proj, proj, cosf, sinf)


def ret_bwd(proj, cosf, sinf, lgam, o, states, dycat):
    S = proj.shape[0]
    nb = S // RET_T
    rev = lambda b: nb - 1 - b
    group, row_tab = _ret_specs(S, rev)

    def body(lg_ref, rq_ref, rk_ref, rv_ref, rg_ref, cos_ref, sin_ref, o_ref, st_in, dy_ref,
             drq_ref, drk_ref, drv_ref, drg_ref, ds_ref, dm_ref):
        @pl.when(pl.program_id(0) == 0)
        def _():
            ds_ref[...] = jnp.zeros_like(ds_ref)
            for h in range(4):
                dm_ref[h] = _ret_decay_mask(lg_ref[h])

        cosf, sinf = cos_ref[...], sin_ref[...]
        H = range(4)
        lanes = [slice(h * LANES, (h + 1) * LANES) for h in H]
        dms = [dm_ref[h] for h in H]
        blk = [_ret_block(lg_ref[h], _head(rq_ref, h), _head(rk_ref, h), _head(rv_ref, h), cosf, sinf, dms[h])
               for h in H]
        q, k, qb, kb, vb, sc, qdec, kdec, block_dec = zip(*blk)
        o_v = [_head(o_ref, h) for h in H]
        cen = [o_v[h] - _rowmean(o_v[h]) for h in H]
        rstd = [lax.rsqrt(_rowmean(cen[h] * cen[h]) + EPS) for h in H]
        on = [cen[h] * rstd[h] for h in H]
        rg = [_head(rg_ref, h) for h in H]
        sig = [_sigmoid(rg[h]) for h in H]
        dy = [_head(dy_ref, h) for h in H]
        for h in H:
            drg_ref[:, lanes[h]] = (dy[h] * on[h] * (sig[h] * (1.0 + rg[h] * (1.0 - sig[h])))).astype(BF16)
        don = [dy[h] * (rg[h] * sig[h]) for h in H]
        do = [rstd[h] * (don[h] - _rowmean(don[h]) - on[h] * _rowmean(don[h] * on[h])) for h in H]
        dob = [do[h].astype(BF16) for h in H]
        dsc = [(_dot_nt(dob[h], vb[h]) * dms[h]).astype(BF16) for h in H]
        st_b = [st_in[h, 0].astype(BF16) for h in H]
        dsn = [ds_ref[h] for h in H]
        dsn_b = [dsn[h].astype(BF16) for h in H]
        dq = [_dot(dsc[h], kb[h]) + _dot_nt(dob[h], st_b[h]) * qdec[h] for h in H]
        dk = [(_dot_tn(dsc[h], qb[h]) + _dot_nt(vb[h], dsn_b[h]) * kdec[h]) * RK_SCALE for h in H]
        dv = [_dot_tn(sc[h].astype(BF16), dob[h]) + _dot((k[h] * kdec[h]).astype(BF16), dsn_b[h]) for h in H]
        for h in H:
            ds_ref[h] = dsn[h] * block_dec[h] + _dot_tn((q[h] * qdec[h]).astype(BF16), dob[h])
        for h in H:
            drq_ref[:, lanes[h]] = (dq[h] * cosf + _swap_halves(dq[h] * sinf)).astype(BF16)
            drk_ref[:, lanes[h]] = (dk[h] * cosf + _swap_halves(dk[h] * sinf)).astype(BF16)
            drv_ref[:, lanes[h]] = dv[h].astype(BF16)

    return pl.pallas_call(
        body, name="ret_bwd", grid=(nb,),
        in_specs=[pl.BlockSpec(memory_space=pltpu.SMEM),
                  group(0), group(1), group(2), group(3), row_tab, row_tab,
                  group(0), pl.BlockSpec((4, 1, LANES, LANES), lambda b: (0, rev(b), 0, 0)),
                  group(0)],
        out_specs=[group(0)] * 4,
        out_shape=[jax.ShapeDtypeStruct((S, GROUP_W), BF16)] * 4,
        scratch_shapes=[pltpu.VMEM((4, LANES, LANES), F32), pltpu.VMEM((4, RET_T, RET_T), F32)],
        compiler_params=_cp("arbitrary"),
    )(lgam, proj, proj, proj, proj, cosf, sinf, o, states, dycat)


def outproj_fwd(x, vecs, y_ret, y_sb, w_out, head=None, tm=1024):
    S, D = x.shape
    tm = min(tm, S)
    last = list(head or ())

    def body(x_ref, v_ref, yr_ref, ys_ref, w_ref, *refs):
        y = _dot(yr_ref[...], w_ref[0:GROUP_W, :]) + _dot(ys_ref[...], w_ref[GROUP_W:, :])
        xv = x_ref[...] + v_ref[2:3, :] * y
        if not last:
            y_ref, xo_ref = refs
            y_ref[...] = y.astype(BF16)
            xo_ref[...] = xv
            return
        g_ref, t_ref, y_ref, dx_ref, st_ref = refs
        y_ref[...] = y.astype(BF16)

        @pl.when(pl.program_id(0) == 0)
        def _():
            st_ref[...] = jnp.zeros_like(st_ref)

        g = g_ref[0:1, :]
        r = lax.rsqrt(_rowmean(xv * xv) + EPS)
        xn = xv * r
        err = xn * g - t_ref[...]
        dy = err * (1.0 / D)
        dxn = dy * g
        dx_ref[...] = r * (dxn - xn * _rowmean(dxn * xn))
        st_ref[0:1, :] += jnp.sum(dy * xn, axis=0, keepdims=True)
        st_ref[1:2, :] += jnp.sum(err * err, axis=0, keepdims=True)

    row = lambda w: pl.BlockSpec((tm, w), lambda i: (i, 0))
    fixed = pl.BlockSpec((8, D), lambda i: (0, 0))
    return pl.pallas_call(
        body, name="outproj_fwd", grid=(S // tm,),
        in_specs=[row(D), fixed, row(GROUP_W), row(GROUP_W), pl.BlockSpec((D, D), lambda i: (0, 0))]
        + ([fixed, row(D)] if last else []),
        out_specs=[row(D), row(D)] + ([fixed] if last else []),
        out_shape=[jax.ShapeDtypeStruct((S, D), BF16), jax.ShapeDtypeStruct((S, D), F32)]
        + ([jax.ShapeDtypeStruct((8, D), F32)] if last else []),
        compiler_params=_cp("arbitrary"),
    )(x, vecs, y_ret, y_sb, w_out, *last)


def outproj_bwd(dx, y, vecs, y_ret, y_sb, w_out, tm=1024):
    S, D = dx.shape
    tm = min(tm, S)
    n = S // tm

    def body(dx_ref, y_ref, v_ref, yr_ref, ys_ref, w_ref, dyc_ref, dw_ref, st_ref, acc):
        i = pl.program_id(0)

        @pl.when(i == 0)
        def _():
            st_ref[...] = jnp.zeros_like(st_ref)
            acc[...] = jnp.zeros_like(acc)

        dxv = dx_ref[...]
        st_ref[0:1, :] += jnp.sum(dxv * y_ref[...].astype(F32), axis=0, keepdims=True)
        dyy = (dxv * v_ref[2:3, :]).astype(BF16)
        dyc_ref[...] = _dot_nt(dyy, w_ref[...])
        acc[0:GROUP_W, :] += _dot_tn(yr_ref[...], dyy)
        acc[GROUP_W:, :] += _dot_tn(ys_ref[...], dyy)

        @pl.when(i == n - 1)
        def _():
            dw_ref[...] = acc[...].astype(BF16)

    row = lambda w: pl.BlockSpec((tm, w), lambda i: (i, 0))
    fixed = lambda r: pl.BlockSpec((r, D), lambda i: (0, 0))
    return pl.pallas_call(
        body, name="outproj_bwd", grid=(n,),
        in_specs=[row(D), row(D), fixed(8), row(GROUP_W), row(GROUP_W), fixed(D)],
        out_specs=[row(D), fixed(D), fixed(8)],
        out_shape=[jax.ShapeDtypeStruct((S, D), F32), jax.ShapeDtypeStruct((D, D), BF16),
                   jax.ShapeDtypeStruct((8, D), F32)],
        scratch_shapes=[pltpu.VMEM((D, D), F32)],
        compiler_params=_cp("arbitrary"),
    )(dx, y, vecs, y_ret, y_sb, w_out)


def inproj_bwd_x(pieces, w3, x, vecs, dx_res, ship=None, tm=512):
    S, D = x.shape
    n = S // tm
    ex = _Exchange(ship)

    def body(*refs):
        p_refs, (w_ref, x_ref, v_ref, dr_ref), refs = refs[:8], refs[8:12], refs[12:]
        ship_refs, (dx_ref, st_ref), refs = refs[:ex.n_in], refs[ex.n_in:ex.n_in + 2], refs[ex.n_in + 2:]
        start, finish = ex.ops(ship_refs, refs)

        @pl.when(pl.program_id(0) == 0)
        def _():
            st_ref[...] = jnp.zeros_like(st_ref)
            start()

        dh = jnp.zeros((tm, D), F32)
        for k, p_ref in enumerate(p_refs):
            c0 = (k % 2) * GROUP_W
            dh = dh + _dot_nt(p_ref[...], w_ref[k // 2, :, c0:c0 + GROUP_W])
        xv = x_ref[...]
        r = lax.rsqrt(_rowmean(xv * xv) + EPS)
        xn = xv * r
        g, scale1 = v_ref[3:4, :], 1.0 + v_ref[1:2, :]
        st_ref[0:1, :] += jnp.sum(dh, axis=0, keepdims=True)
        dh_xn = dh * xn
        st_ref[1:2, :] += jnp.sum(dh_xn, axis=0, keepdims=True) * g
        st_ref[2:3, :] += jnp.sum(dh_xn, axis=0, keepdims=True) * scale1
        dxn = dh * (g * scale1)
        dx_ref[...] = r * (dxn - xn * _rowmean(dxn * xn)) + dr_ref[...]
        pl.when(pl.program_id(0) == n - 1)(finish)

    row = lambda w: pl.BlockSpec((tm, w), lambda i: (i, 0))
    return pl.pallas_call(
        body, name="inproj_bwd_x", grid=(n,),
        in_specs=[row(GROUP_W)] * 8 + [pl.BlockSpec((N_SHARD, D, SHARD_W), lambda i: (0, 0, 0)),
                                       row(D), pl.BlockSpec((8, D), lambda i: (0, 0)), row(D)] + ex.in_specs,
        out_specs=[row(D), pl.BlockSpec((8, D), lambda i: (0, 0))] + ex.out_specs,
        out_shape=[jax.ShapeDtypeStruct((S, D), F32), jax.ShapeDtypeStruct((8, D), F32)] + ex.out_shape,
        scratch_shapes=ex.scratch,
        compiler_params=_cp("arbitrary"),
    )(*pieces, w3, x, vecs, dx_res, *ex.ship)


def inproj_bwd_w(h, pieces, tm=1024):
    S, D = h.shape
    tm = min(tm, S)
    n = S // tm

    def body(*refs):
        h_ref, p_refs, dw_ref, acc = refs[0], refs[1:9], refs[9], refs[10]
        i = pl.program_id(0)

        @pl.when(i == 0)
        def _():
            acc[...] = jnp.zeros_like(acc)

        hv = h_ref[...]
        for k, p_ref in enumerate(p_refs):
            c0 = (k % 2) * GROUP_W
            acc[k // 2, :, c0:c0 + GROUP_W] += _dot_tn(hv, p_ref[...])

        @pl.when(i == n - 1)
        def _():
            dw_ref[...] = acc[...].astype(BF16)

    row = lambda w: pl.BlockSpec((tm, w), lambda i: (i, 0))
    return pl.pallas_call(
        body, name="inproj_bwd_w", grid=(n,),
        in_specs=[row(D)] + [row(GROUP_W)] * 8,
        out_specs=pl.BlockSpec((N_SHARD, D, SHARD_W), lambda i: (0, 0, 0), pipeline_mode=pl.Buffered(1)),
        out_shape=jax.ShapeDtypeStruct((N_SHARD, D, SHARD_W), BF16),
        scratch_shapes=[pltpu.VMEM((N_SHARD, D, SHARD_W), F32)],
        compiler_params=_cp("arbitrary"),
    )(h, *pieces)


def layer_fwd(x, vecs, w3, w_out, tabs, gather=None, head=None):
    cosf, sinf, lgam = tabs
    ret, sg, h, sb = inproj_fwd(x, vecs, w3)
    y_ret, o_ret, states = ret_fwd(ret, cosf, sinf, lgam)
    y_sb, o_sb, sb_end, *gathered = sb_fwd(sb, sg, gather)
    if callable(w_out):
        w_out = w_out(gathered[0])
    y, *x_next = outproj_fwd(x, vecs, y_ret, y_sb, w_out, head)
    saved = (x, ret, sg, h, sb, y_ret, o_ret, states, y_sb, o_sb, sb_end, y)
    return (x_next[0] if head is None else x_next), saved, (gathered[0] if gathered else None)


def _by_shard(dw_out):
    return dw_out.reshape(N_SHARD, D_MODEL // N_SHARD, D_MODEL)


def layer_bwd(dx, saved, vecs, w3, w_out, tabs, later_grads=None):
    cosf, sinf, lgam = tabs
    x, ret, sg, h, sb, y_ret, o_ret, states, y_sb, o_sb, sb_end, y = saved
    dycat, dw_out, st_o = outproj_bwd(dx, y, vecs, y_ret, y_sb, w_out)
    dw_out = _by_shard(dw_out)
    ship = None if later_grads is None else (later_grads[0], dw_out, later_grads[1])
    *d_sb, = sb_bwd(sb, sg, o_sb, sb_end, dycat, ship)
    d_ret = ret_bwd(ret, cosf, sinf, lgam, o_ret, states, dycat)
    pieces = list(d_ret) + d_sb[:4]
    dw_in = inproj_bwd_w(h, pieces)
    dx, st_i, *recv_in = inproj_bwd_x(pieces, w3, x, vecs, dx, None if later_grads is None else (dw_in,))
    dmod = jnp.concatenate([st_i[0:2], st_o[0:1]], axis=0)
    grads = (dw_in, dw_out) if later_grads is None else (recv_in[0], d_sb[4])
    return dx, dmod, st_i[2:3], grads


def _place():
    return lax.axis_index("x"), lax.axis_index("y"), lax.axis_index("c")


def _other_chips(mx, my):
    return [(1 - mx, my), (mx, 1 - my), (1 - mx, 1 - my)]


_ANY = pl.BlockSpec(memory_space=pl.ANY)


_GATHER_SCRATCH = [pltpu.SemaphoreType.DMA((7,)), pltpu.SemaphoreType.DMA((7,)), pltpu.SemaphoreType.DMA(())]


def _gather_ops(x_ref, out_ref, send_sems, recv_sems, local_sem):
    mx, my, mc = _place()
    me, sibling = (mx, my, mc), (mx, my, 1 - mc)
    chips = _other_chips(mx, my)

    def slot(px, py, pc):
        return out_ref.at[4 * px + 2 * py + pc]

    def copy(k, block, to, src=None):
        return pltpu.make_async_remote_copy(
            src_ref=slot(*block) if src is None else src, dst_ref=slot(*block),
            send_sem=send_sems.at[k], recv_sem=recv_sems.at[k], device_id=to, device_id_type=MESH)

    mine = pltpu.make_async_copy(x_ref, slot(*me), local_sem)
    first = [copy(0, me, sibling, src=x_ref)]
    first += [copy(1 + j, me, (*chip, mc), src=x_ref) for j, chip in enumerate(chips)]
    passed = [copy(4 + j, (*chip, mc), sibling) for j, chip in enumerate(chips)]

    def start():
        mine.start()
        for cp in first:
            cp.start()

    def forward():
        for j, chip in enumerate(chips):
            copy(1 + j, (*chip, mc), me).wait_recv()
            passed[j].start()

    def finish():
        copy(0, sibling, me).wait_recv()
        for j, chip in enumerate(chips):
            copy(4 + j, (*chip, 1 - mc), me).wait_recv()
        for cp in first + passed:
            cp.wait_send()
        mine.wait()

    return start, forward, finish


def allgather8(x, name):
    def body(x_ref, out_ref, send_sems, recv_sems, local_sem):
        for step in _gather_ops(x_ref, out_ref, send_sems, recv_sems, local_sem):
            step()

    return pl.pallas_call(
        body, name=name, out_shape=jax.ShapeDtypeStruct((8,) + x.shape, x.dtype),
        in_specs=[_ANY], out_specs=_ANY, scratch_shapes=_GATHER_SCRATCH,
    )(x)


class _Exchange:
    def __init__(self, ship):
        self.ship = list(ship or ())
        self.n_in = len(self.ship)
        self.n_out = 1 if self.ship else 0
        self.rows = [a.shape[1] for a in self.ship]
        self.in_specs = [_ANY] * self.n_in
        self.out_specs = [_ANY] * self.n_out
        self.out_shape = [jax.ShapeDtypeStruct((N_SHARD, sum(self.rows), SHARD_W), BF16)] * self.n_out
        sem = pltpu.SemaphoreType.DMA
        self.scratch = [sem((3,)), sem((3,)), sem(())] * self.n_out

    def ops(self, ship_refs, tail):
        if not self.ship:
            return (lambda: None), (lambda: None)
        recv, send_sems, recv_sems, local_sem = tail
        mx, my, mc = _place()
        my_chip = 2 * mx + my
        chips = _other_chips(mx, my)

        def pieces(s):
            firsts = np.cumsum([0] + self.rows[:-1])
            return [(ref.at[s], int(r0), n) for ref, r0, n in zip(ship_refs, firsts, self.rows)]

        def start():
            for src, r0, n in pieces(my_chip):
                pltpu.make_async_copy(src, recv.at[my_chip, pl.ds(r0, n)], local_sem).start()
            for j, (px, py) in enumerate(chips):
                for src, r0, n in pieces(2 * px + py):
                    pltpu.make_async_remote_copy(
                        src_ref=src, dst_ref=recv.at[my_chip, pl.ds(r0, n)],
                        send_sem=send_sems.at[j], recv_sem=recv_sems.at[j],
                        device_id=(px, py, mc), device_id_type=MESH).start()

        def finish():
            for j, (px, py) in enumerate(chips):
                whole = recv.at[2 * px + py]
                both = pltpu.make_async_remote_copy(
                    src_ref=whole, dst_ref=whole, send_sem=send_sems.at[j], recv_sem=recv_sems.at[j],
                    device_id=(px, py, mc), device_id_type=MESH)
                both.wait_recv()
                both.wait_send()
            pltpu.make_async_copy(recv.at[my_chip], recv.at[my_chip], local_sem).wait()

        return start, finish


def sum_slots(recv_a, recv_b, tr=256):
    n, rows_a, cols = recv_a.shape
    na, nb = rows_a // tr, recv_b.shape[1] // tr

    def body(a_ref, b_ref, o_ref):
        def total(r_ref):
            acc = r_ref[0].astype(F32)
            for k in range(1, n):
                acc = acc + r_ref[k].astype(F32)
            o_ref[...] = acc

        pl.when(pl.program_id(0) < na)(lambda: total(a_ref))
        pl.when(pl.program_id(0) >= na)(lambda: total(b_ref))

    return pl.pallas_call(
        body, name="sum_slots", grid=(na + nb,),
        in_specs=[pl.BlockSpec((n, tr, cols), lambda i: (0, jnp.minimum(i, na - 1), 0)),
                  pl.BlockSpec((n, tr, cols), lambda i: (0, jnp.maximum(i - na, 0), 0))],
        out_specs=pl.BlockSpec((tr, cols), lambda i: (i, 0)),
        out_shape=jax.ShapeDtypeStruct(((na + nb) * tr, cols), F32),
        compiler_params=_cp("arbitrary"),
    )(recv_a, recv_b)


def swap_sibling(p):
    def body(p_ref, out_ref, send_sem, recv_sem):
        mx, my, mc = _place()
        cp = pltpu.make_async_remote_copy(
            src_ref=p_ref, dst_ref=out_ref, send_sem=send_sem, recv_sem=recv_sem,
            device_id=(mx, my, 1 - mc), device_id_type=MESH)
        cp.start()
        cp.wait()

    return pl.pallas_call(
        body, name="swap_sibling", out_shape=jax.ShapeDtypeStruct(p.shape, p.dtype),
        in_specs=[_ANY], out_specs=_ANY,
        scratch_shapes=[pltpu.SemaphoreType.DMA(()), pltpu.SemaphoreType.DMA(())],
    )(p)


def _adamw(w, g, m, v):
    m = ADAM_B1 * m + (1.0 - ADAM_B1) * g
    v = ADAM_B2 * v + (1.0 - ADAM_B2) * (g * g)
    m_hat = m / (1.0 - ADAM_B1 ** ADAM_STEP)
    v_hat = v / (1.0 - ADAM_B2 ** ADAM_STEP)
    delta = -ADAM_LR * (m_hat / (jnp.sqrt(v_hat) + ADAM_EPS) + ADAM_WD * w)
    return delta, m, v


def adam_slab(p_own, p_sib, w, m, v, row0, name, tr=256):
    L, R, C = w.shape
    nr = R // tr

    def body(a_ref, b_ref, w_ref, m_ref, v_ref, g_out, d_out, m_out, v_out):
        g = a_ref[...] + b_ref[...]
        d, m2, v2 = _adamw(w_ref[0], g, m_ref[0], v_ref[0])
        g_out[0], d_out[0], m_out[0], v_out[0] = g, d, m2, v2

    slab = pl.BlockSpec((tr, C), lambda l, i: (row0 // tr + l * nr + i, 0))
    blk = pl.BlockSpec((1, tr, C), lambda l, i: (l, i, 0))
    return pl.pallas_call(
        body, name=name, grid=(L, nr),
        in_specs=[slab, slab, blk, blk, blk], out_specs=[blk] * 4,
        out_shape=[jax.ShapeDtypeStruct(w.shape, F32)] * 4,
        compiler_params=_cp("arbitrary", "arbitrary"),
    )(p_own, p_sib, w, m, v)


def ada_fwd(c_all, w_ada):
    L, D, W = w_ada.shape

    def body(c_ref, w_ref, o_ref):
        cv = c_ref[...]
        o_ref[0] = jnp.dot(cv * _sigmoid(cv), w_ref[0], precision=lax.Precision.HIGHEST,
                           preferred_element_type=F32)

    return pl.pallas_call(
        body, name="ada_fwd", grid=(L,),
        in_specs=[pl.BlockSpec((8, D), lambda l: (0, 0)), pl.BlockSpec((1, D, W), lambda l: (l, 0, 0))],
        out_specs=pl.BlockSpec((1, 8, W), lambda l: (l, 0, 0)),
        out_shape=jax.ShapeDtypeStruct((L, 8, W), F32),
        compiler_params=_cp("arbitrary"),
    )(c_all, w_ada)


def vecs_build(mod_all, b_ada, norm_g):
    W = mod_all.shape[2]

    def body(m_ref, b_ref, g_ref, o_ref):
        mx, my, mc = _place()
        me = 4 * mx + 2 * my + mc
        rowid = lax.broadcasted_iota(jnp.int32, (2 * 8, 1), 0)
        o_ref[...] = jnp.zeros_like(o_ref)
        for l in range(DEPTH):
            parts = [jnp.sum(jnp.where(rowid == l * 8 + me, m_ref[2 * s + mc], 0.0), axis=0, keepdims=True)
                     for s in range(N_SHARD)]
            mod = jnp.concatenate(parts, axis=1) + b_ref[l:l + 1, :]
            for t in range(3):
                o_ref[l, t:t + 1, :] = mod[:, t * D_MODEL:(t + 1) * D_MODEL]
            o_ref[l, 3:4, :] = g_ref[l:l + 1, :]

    return pl.pallas_call(
        body, name="vecs_build", out_shape=jax.ShapeDtypeStruct((DEPTH, 8, D_MODEL), F32),
    )(mod_all, b_ada, norm_g)


def ada_update(dmods, c_t, w, m, v, tr=256):
    L, D, W = w.shape

    def body(dm_ref, c_ref, w_ref, m_ref, v_ref, g_out, d_out, m_out, v_out):
        mx, my, _ = _place()
        shard = 2 * mx + my
        dm = jnp.zeros((8, W), F32)
        for s in range(N_SHARD):
            dm = dm + jnp.where(shard == s, dm_ref[0, :, s * W:(s + 1) * W], 0.0)
        cv = c_ref[...]
        ca = cv * _sigmoid(cv)
        g = jnp.zeros((tr, W), F32)
        for b in range(8):
            g = g + ca[:, b:b + 1] * dm[b:b + 1, :]
        d, m2, v2 = _adamw(w_ref[0], g, m_ref[0], v_ref[0])
        g_out[0], d_out[0], m_out[0], v_out[0] = g, d, m2, v2

    blk = pl.BlockSpec((1, tr, W), lambda l, i: (l, i, 0))
    return pl.pallas_call(
        body, name="ada_update", grid=(L, D // tr),
        in_specs=[pl.BlockSpec((1, 8, 3 * D), lambda l, i: (l, 0, 0)), pl.BlockSpec((tr, 8), lambda l, i: (i, 0)),
                  blk, blk, blk],
        out_specs=[blk] * 4, out_shape=[jax.ShapeDtypeStruct(w.shape, F32)] * 4,
        compiler_params=_cp("arbitrary", "arbitrary"),
    )(dmods, c_t, w, m, v)


STAT_ROWS = 16


def small_update(stats_all, norm, b_ada, final):
    def body(s_ref, *refs):
        ins, outs = refs[:9], refs[9:]
        tot = s_ref[0]
        for k in range(1, 8):
            tot = tot + s_ref[k]
        g_norm = tot[0:2, :]
        g_final = tot[2:3, :]
        g_b = jnp.concatenate(
            [jnp.concatenate([tot[3 + 3 * l + t:4 + 3 * l + t, :] for t in range(3)], axis=1) for l in range(DEPTH)],
            axis=0)
        for p, g in enumerate((g_norm, g_b, g_final)):
            w_ref, m_ref, v_ref = ins[3 * p:3 * p + 3]
            d, m2, v2 = _adamw(w_ref[...], g, m_ref[...], v_ref[...])
            for o_ref, val in zip(outs[4 * p:4 * p + 4], (g, d, m2, v2)):
                o_ref[...] = val
        loss = (0.5 / D_MODEL) * jnp.sum(tot[9:10, :], axis=1, keepdims=True)
        outs[12][...] = jnp.broadcast_to(loss, (8, LANES))

    shapes = []
    for w, _, _ in (norm, b_ada, final):
        shapes += [jax.ShapeDtypeStruct(w.shape, F32)] * 4
    shapes.append(jax.ShapeDtypeStruct((8, LANES), F32))
    return pl.pallas_call(body, name="small_update", out_shape=shapes)(stats_all, *norm, *b_ada, *final)


def kernel(x, c, norm_g, w_ada, b_ada, w_in, w_out, final_g, loss_target, m_norm_g, m_w_ada, m_b_ada, m_w_in, m_w_out, m_final_g, v_norm_g, v_w_ada, v_b_ada, v_w_in, v_w_out, v_final_g):
    S, D = x.shape[1], x.shape[2]
    mc = lax.axis_index("c")
    out_rows = D // N_SHARD

    def my_half(a, rows):
        return lax.dynamic_slice_in_dim(a, mc * rows, rows, axis=0)

    assert DEPTH == 2
    win = [my_half(w_in[l], D // 2).astype(BF16) for l in range(DEPTH)]
    wout = [my_half(w_out[l], out_rows // 2).astype(BF16) for l in range(DEPTH)]
    w3_first = allgather8(win[0], "gather_weights").reshape(N_SHARD, D, SHARD_W)
    rest = jnp.concatenate([wout[0], win[1], wout[1]], axis=0)

    def unpack(wall):
        wall = wall.reshape(N_SHARD, 2, rest.shape[0], SHARD_W)
        a, b = out_rows // 2, out_rows // 2 + D // 2
        return wall[:, :, :a].reshape(D, D), (wall[:, :, a:b].reshape(N_SHARD, D, SHARD_W), wall[:, :, b:].reshape(D, D))

    c_all = allgather8(jnp.broadcast_to(c, (8, D)), "gather_c")[:, 0, :]
    mod_all = allgather8(ada_fwd(c_all, w_ada).reshape(DEPTH * 8, -1), "gather_mod")
    vecs = vecs_build(mod_all, b_ada, norm_g)

    tabs = (*rope_tables(S), ret_log_gamma())
    saved = [None] * DEPTH
    h, saved[0], wall = layer_fwd(x[0], vecs[0], w3_first, lambda g: unpack(g)[0], tabs, rest)
    weights = [(w3_first, unpack(wall)[0]), unpack(wall)[1]]
    head = (jnp.broadcast_to(final_g[None, :], (8, D)), loss_target[0])
    (dx, st_loss), saved[1], _ = layer_fwd(h, vecs[1], *weights[1], tabs, head=head)

    dmod, dnorm, grads = [None] * DEPTH, [None] * DEPTH, None
    for l in reversed(range(DEPTH)):
        dx, dmod[l], dnorm[l], grads = layer_bwd(dx, saved[l], vecs[l], *weights[l], tabs, grads)

    p_own = sum_slots(*grads)
    p_sib = swap_sibling(p_own)
    res_in = adam_slab(p_own, p_sib, w_in, m_w_in, v_w_in, 0, "adam_w_in")
    res_out = adam_slab(p_own, p_sib, w_out, m_w_out, v_w_out, DEPTH * D, "adam_w_out", tr=128)

    stats = jnp.concatenate(dnorm + [st_loss[0:1]] + dmod + [st_loss[1:2], jnp.zeros((STAT_ROWS - 10, D), F32)], axis=0)
    stats_all = allgather8(stats, "gather_stats")
    dmods = stats_all[:, 3:9, :].reshape(8, DEPTH, 3 * D).transpose(1, 0, 2)
    res_ada = ada_update(dmods, c_all.T, w_ada, m_w_ada, v_w_ada)
    small = small_update(stats_all, (norm_g, m_norm_g, v_norm_g), (b_ada, m_b_ada, v_b_ada),
                         (final_g[None, :], m_final_g[None, :], v_final_g[None, :]))
    res_norm, res_b, res_final = small[0:4], small[4:8], [a[0] for a in small[8:12]]
    loss = small[12][0, 0]

    by_kind = [res_norm, res_ada, res_b, res_in, res_out, res_final]
    outs = [loss, dx[None]]
    for kind in range(4):
        outs += [r[kind] for r in by_kind]
    return tuple(outs)
```

```python
import functools

import numpy as np
import jax
import jax.numpy as jnp
from jax import lax
from jax.experimental import pallas as pl
from jax.experimental.pallas import tpu as pltpu

F32, BF16 = jnp.float32, jnp.bfloat16
MESH = pl.DeviceIdType.MESH

D_MODEL = 1024
DEPTH = 2
SHARD_W = 1024
N_SHARD = 4
GROUP_W = 512
LANES = 128
SB_HEAD_DIM = 64
RET_HEAD_DIM = 128
CHUNK = 64
ROPE_BASE = 10000.0
EPS = 1e-6
SQ_SCALE = SB_HEAD_DIM ** -0.5
RK_SCALE = RET_HEAD_DIM ** -0.5
SB_T = 1024
SB_CHAINS = 16
SB_NB = 4
RET_T = 256
EXP_ZERO = -104.0
VMEM_LIMIT_BYTES = 56 * 2 ** 20

ADAM_LR, ADAM_B1, ADAM_B2, ADAM_EPS, ADAM_WD, ADAM_STEP = 0.001, 0.9, 0.999, 1e-08, 0.01, 10


def _cp(*sem):
    return pltpu.CompilerParams(dimension_semantics=sem, vmem_limit_bytes=VMEM_LIMIT_BYTES)


def _dot(a, b):
    return lax.dot_general(a, b, (((1,), (0,)), ((), ())), preferred_element_type=F32)


def _dot_nt(a, b):
    return lax.dot_general(a, b, (((1,), (1,)), ((), ())), preferred_element_type=F32)


def _dot_tn(a, b):
    return lax.dot_general(a, b, (((0,), (0,)), ((), ())), preferred_element_type=F32)


def _running_sum(a, tri):
    return _dot(a.astype(BF16), tri)


def _sigmoid(x):
    return 1.0 / (1.0 + jnp.exp(-x))


def _rowsum(a):
    return jnp.sum(a, axis=1, keepdims=True)


def _rowmean(a):
    return jnp.mean(a, axis=1, keepdims=True)


def inproj_fwd(x, vecs, w3, tm=512):
    S, D = x.shape

    def body(x_ref, v_ref, w_ref, ret_ref, sg_ref, h_ref, sb_ref):
        xv = x_ref[...]
        r = lax.rsqrt(_rowmean(xv * xv) + EPS)
        h = xv * r * v_ref[3:4, :] * (1.0 + v_ref[1:2, :]) + v_ref[0:1, :]
        hb = h.astype(BF16)
        h_ref[...] = hb
        for s in range(N_SHARD):
            p = _dot(hb, w_ref[s])
            if s < 2:
                ret_ref[:, s * SHARD_W:(s + 1) * SHARD_W] = p
            if s == 2:
                sb_ref[:, 0:GROUP_W] = (p[:, 0:GROUP_W] * SQ_SCALE).astype(BF16)
                sb_ref[:, GROUP_W:SHARD_W] = p[:, GROUP_W:].astype(BF16)
            if s == 3:
                sb_ref[:, SHARD_W:SHARD_W + GROUP_W] = p[:, 0:GROUP_W].astype(BF16)
                sg_ref[...] = p[:, GROUP_W:]

    row = lambda w: pl.BlockSpec((tm, w), lambda i: (i, 0))
    return pl.pallas_call(
        body, name="inproj_fwd", grid=(S // tm,),
        in_specs=[row(D), pl.BlockSpec((8, D), lambda i: (0, 0)),
                  pl.BlockSpec((N_SHARD, D, SHARD_W), lambda i: (0, 0, 0))],
        out_specs=[row(2 * SHARD_W), row(GROUP_W), row(D), row(3 * GROUP_W)],
        out_shape=[jax.ShapeDtypeStruct((S, 2 * SHARD_W), F32), jax.ShapeDtypeStruct((S, GROUP_W), F32),
                   jax.ShapeDtypeStruct((S, D), BF16), jax.ShapeDtypeStruct((S, 3 * GROUP_W), BF16)],
        compiler_params=_cp("arbitrary"),
    )(x, vecs, w3)


def _sb_logits(qh, k2, keep):
    z = _dot_nt(qh, k2)
    sp = jnp.log(1.0 + jnp.exp(-jnp.abs(z)))
    lb = jnp.minimum(z, 0.0) - sp
    lk = lb - z
    if keep is not None:
        lk = jnp.where(keep, lk, 0.0)
    return lb, lk


class _sb_chains:
    def __init__(self, i, q2, do_b=None):
        t = self.t = SB_T // SB_CHAINS
        self.C = range(SB_CHAINS)
        r = lax.broadcasted_iota(jnp.int32, (SB_NB * t, SB_NB * t), 0)
        c = lax.broadcasted_iota(jnp.int32, (SB_NB * t, SB_NB * t), 1)
        self.later_all = jnp.where(r > c, 1.0, 0.0).astype(BF16)
        self.earlier_all = jnp.where(r < c, 1.0, 0.0).astype(BF16)
        self.later, self.earlier = self.later_all[:t, :t], self.earlier_all[:t, :t]
        self.head0 = lax.broadcasted_iota(jnp.int32, (1, LANES), 1) < SB_HEAD_DIM
        row = lax.broadcasted_iota(jnp.int32, (2 * t, SB_NB * t), 0) & (t - 1)
        col = lax.broadcasted_iota(jnp.int32, (2 * t, SB_NB * t), 1)
        qt = [SB_CHAINS * i + cc for cc in self.C]
        self.first = [jnp.maximum(qt[cc] - (SB_NB - 1), 0) for cc in self.C]
        self.keep = [self.first[cc] * t + col < qt[cc] * t + row for cc in self.C]
        self.qs = [self._stack(q2[cc * t:(cc + 1) * t]) for cc in self.C]
        if do_b is not None:
            self.dos = [self._stack(do_b[cc * t:(cc + 1) * t]) for cc in self.C]

    def _stack(self, a):
        zero = jnp.zeros_like(a)
        return jnp.concatenate([jnp.where(self.head0, a, zero), jnp.where(self.head0, zero, a)], axis=0)

    def rows(self, ref, j, n):
        return ref[pl.ds(pl.multiple_of(j * self.t, self.t), n * self.t), :]

    def suffix(self, lk):
        return _running_sum(lk, self.later_all), _rowsum(lk)

    def prefix(self, g, G0):
        return _running_sum(g, self.earlier_all) + G0


def sb_fwd(sb, sg, gather=None):
    S = sb.shape[0]
    T = SB_T
    nq = S // T
    carried = [] if gather is None else [gather]

    def body(*refs):
        (q_ref, k_ref, v_ref, sg_ref), refs = refs[:4], refs[4:]
        p, i = pl.program_id(0), pl.program_id(1)
        if carried:
            x_ref, y_ref, o_ref, end_ref, out_ref, send_sems, recv_sems, local_sem = refs
            start, forward, finish = _gather_ops(x_ref, out_ref, send_sems, recv_sems, local_sem)
            pl.when(jnp.logical_and(p == 0, i == 0))(start)
            pl.when(jnp.logical_and(p == 3, i == 0))(forward)
        else:
            y_ref, o_ref, end_ref = refs
        ch = _sb_chains(i, q_ref[...])
        later, head0 = ch.later, ch.head0
        lbk = [_sb_logits(ch.qs[c], ch.rows(k_ref, ch.first[c], SB_NB), ch.keep[c]) for c in ch.C]
        suffix, R = zip(*[ch.suffix(lbk[c][1]) for c in ch.C])
        aa = [jnp.where(ch.keep[c], jnp.exp(lbk[c][0] + suffix[c]), 0.0) for c in ch.C]
        acc = [_dot(aa[c].astype(BF16), ch.rows(v_ref, ch.first[c], SB_NB)) for c in ch.C]

        outs = []
        for c in ch.C:
            def tile(j, Rc, c=c):
                lb, lk = _sb_logits(ch.qs[c], ch.rows(k_ref, j, 1), None)
                a = jnp.exp(lb + _running_sum(lk, later) + Rc)
                return _dot(a.astype(BF16), ch.rows(v_ref, j, 1)), Rc + _rowsum(lk)

            def cond(st):
                return jnp.logical_and(st[0] >= 0, st[3] > EXP_ZERO)

            def step(st, tile=tile):
                cx, Rn = tile(st[0], st[2])
                return st[0] - 1, st[1] + cx, Rn, jnp.max(Rn)

            j_end, ac, Rc, _ = lax.while_loop(cond, step, (ch.first[c] - 1, acc[c], R[c], jnp.max(R[c])))
            base = c * (2 * ch.t + 8)
            end_ref[0, 0, base:base + 2 * ch.t, :] = jnp.broadcast_to(Rc, (2 * ch.t, 8))
            end_ref[0, 0, base + 2 * ch.t:base + 2 * ch.t + 8, :] = jnp.full((8, 8), j_end.astype(F32))
            outs.append(jnp.where(head0, ac[:ch.t], ac[ch.t:]))
        o = jnp.concatenate(outs, axis=0)
        o_ref[...] = o
        sg = sg_ref[...]
        y_ref[...] = (o * (sg * _sigmoid(sg))).astype(BF16)
        if carried:
            pl.when(jnp.logical_and(p == 3, i == nq - 1))(finish)

    return pl.pallas_call(
        body, name="sb_fwd", grid=(4, nq),
        in_specs=[pl.BlockSpec((T, LANES), lambda p, i: (i, p)),
                  pl.BlockSpec((S, LANES), lambda p, i: (0, 4 + p)),
                  pl.BlockSpec((S, LANES), lambda p, i: (0, 8 + p)),
                  pl.BlockSpec((T, LANES), lambda p, i: (i, p))] + [_ANY for _ in carried],
        out_specs=[pl.BlockSpec((T, LANES), lambda p, i: (i, p)),
                   pl.BlockSpec((T, LANES), lambda p, i: (i, p)),
                   pl.BlockSpec((1, 1, SB_CHAINS * (2 * T // SB_CHAINS + 8), 8), lambda p, i: (p, i, 0, 0))] + [_ANY for _ in carried],
        out_shape=[jax.ShapeDtypeStruct((S, GROUP_W), BF16),
                   jax.ShapeDtypeStruct((S, GROUP_W), F32),
                   jax.ShapeDtypeStruct((4, nq, SB_CHAINS * (2 * T // SB_CHAINS + 8), 8), F32)]
        + [jax.ShapeDtypeStruct((8,) + a.shape, a.dtype) for a in carried],
        scratch_shapes=_GATHER_SCRATCH if carried else [],
        compiler_params=_cp("arbitrary", "arbitrary"),
    )(sb, sb, sb, sg, *carried)


def sb_bwd(sb, sg, o, sb_end, dycat, ship=None):
    S = sb.shape[0]
    T = SB_T
    nq = S // T
    ex = _Exchange(ship)

    def body(*refs):
        (q_ref, k_ref, v_ref, sg_ref, o_ref, dy_ref, end_ref), refs = refs[:7], refs[7:]
        ship_refs, (dq_ref, dk_ref, dv_ref, dsg_ref), refs = refs[:ex.n_in], refs[ex.n_in:ex.n_in + 4], refs[ex.n_in + 4:]
        recv, (dk_acc, dv_acc), sems = refs[:ex.n_out], refs[ex.n_out:ex.n_out + 2], refs[ex.n_out + 2:]
        start, finish = ex.ops(ship_refs, recv + sems)
        p, i = pl.program_id(0), pl.program_id(1)
        pl.when(jnp.logical_and(p == 0, i == 0))(start)

        @pl.when(i == 0)
        def _():
            dk_acc[...] = jnp.zeros_like(dk_acc)
            dv_acc[...] = jnp.zeros_like(dv_acc)

        sg = sg_ref[...]
        sig = _sigmoid(sg)
        dy = dy_ref[...]
        dsg_ref[...] = (dy * o_ref[...] * (sig * (1.0 + sg * (1.0 - sig)))).astype(BF16)
        do_b = (dy * (sg * sig)).astype(BF16)
        ch = _sb_chains(i, q_ref[...], do_b)
        later, earlier, head0, t = ch.later, ch.earlier, ch.head0, ch.t
        end = end_ref[0, 0]

        def grads(c, j, n, a, lb, g, G, keep):
            dz = g - jnp.exp(lb) * (g + G)
            if keep is not None:
                dz = jnp.where(keep, dz, 0.0)
            dzb = dz.astype(BF16)
            rows = pl.ds(pl.multiple_of(j * t, t), n * t)
            dk_acc[rows, :] += _dot_tn(dzb, ch.qs[c])
            dv_acc[rows, :] += _dot_tn(a.astype(BF16), ch.dos[c])
            return _dot(dzb, ch.rows(k_ref, j, n))

        dq, G0 = [], []
        for c in ch.C:
            base = c * (2 * t + 8)
            j_end = jnp.max(end[base + 2 * t:base + 2 * t + 8, :]).astype(jnp.int32)

            def sweep(j, st, c=c):
                dqc, G0c, left = st
                lb, lk = _sb_logits(ch.qs[c], ch.rows(k_ref, j, 1), None)
                stick = left - _rowsum(lk)
                a = jnp.exp(lb + _running_sum(lk, later) + stick)
                g = a * _dot_nt(ch.dos[c], ch.rows(v_ref, j, 1))
                G = _running_sum(g, earlier) + G0c
                return dqc + grads(c, j, 1, a, lb, g, G, None), G0c + _rowsum(g), stick

            st = lax.fori_loop(j_end + 1, ch.first[c], sweep,
                               (jnp.zeros((2 * t, LANES), F32), jnp.zeros((2 * t, 1), F32), end[base:base + 2 * t, 0:1]))
            dq.append(st[0])
            G0.append(st[1])

        lbk = [_sb_logits(ch.qs[c], ch.rows(k_ref, ch.first[c], SB_NB), ch.keep[c]) for c in ch.C]
        suffix = [ch.suffix(lbk[c][1])[0] for c in ch.C]
        aa = [jnp.where(ch.keep[c], jnp.exp(lbk[c][0] + suffix[c]), 0.0) for c in ch.C]
        g = [aa[c] * _dot_nt(ch.dos[c], ch.rows(v_ref, ch.first[c], SB_NB)) for c in ch.C]
        G = [ch.prefix(g[c], G0[c]) for c in ch.C]
        for c in ch.C:
            dqc = dq[c] + grads(c, ch.first[c], SB_NB, aa[c], lbk[c][0], g[c], G[c], ch.keep[c])
            dq_ref[c * t:(c + 1) * t, :] = (jnp.where(head0, dqc[:t], dqc[t:]) * SQ_SCALE).astype(BF16)

        @pl.when(i == nq - 1)
        def _():
            dk_ref[...] = dk_acc[...].astype(BF16)
            dv_ref[...] = dv_acc[...].astype(BF16)

        pl.when(jnp.logical_and(p == 3, i == nq - 1))(finish)

    tile_spec = lambda c0: pl.BlockSpec((T, LANES), lambda p, i: (i, c0 + p))
    head_spec = lambda c0: pl.BlockSpec((S, LANES), lambda p, i: (0, c0 + p))
    return pl.pallas_call(
        body, name="sb_bwd", grid=(4, nq),
        in_specs=[tile_spec(0), head_spec(4), head_spec(8), tile_spec(0), tile_spec(0), tile_spec(4),
                  pl.BlockSpec((1, 1, SB_CHAINS * (2 * T // SB_CHAINS + 8), 8), lambda p, i: (p, i, 0, 0))] + ex.in_specs,
        out_specs=[tile_spec(0), head_spec(0), head_spec(0), tile_spec(0)] + ex.out_specs,
        out_shape=[jax.ShapeDtypeStruct((S, GROUP_W), BF16)] * 4 + ex.out_shape,
        scratch_shapes=[pltpu.VMEM((S, LANES), F32), pltpu.VMEM((S, LANES), F32)] + ex.scratch,
        compiler_params=_cp("arbitrary", "arbitrary"),
    )(sb, sb, sb, sg, o, dycat, sb_end, *ex.ship)


def rope_tables(S):
    half = RET_HEAD_DIM // 2
    inv = ROPE_BASE ** (-jnp.arange(half, dtype=F32) / half)
    ang = jnp.arange(S, dtype=F32)[:, None] * inv[None, :]
    cos, sin = jnp.cos(ang), jnp.sin(ang)
    return jnp.concatenate([cos, cos], axis=1), jnp.concatenate([-sin, sin], axis=1)


def ret_log_gamma():
    return jnp.log1p(-(2.0 ** (-5.0 - jnp.arange(4, dtype=F32))))


def _swap_halves(a):
    return pltpu.roll(a, RET_HEAD_DIM // 2, axis=1)


def _ret_decay_mask(lg):
    n = lax.broadcasted_iota(jnp.int32, (RET_T, RET_T), 0)
    m = lax.broadcasted_iota(jnp.int32, (RET_T, RET_T), 1)
    dist = jnp.abs(n - m).astype(F32)
    return jnp.where((m // CHUNK) <= (n // CHUNK), jnp.exp(lg * dist), 0.0)


def _ret_block(lg, rq, rk, rv, cosf, sinf, dm):
    q = rq * cosf + _swap_halves(rq) * sinf
    k = (rk * cosf + _swap_halves(rk) * sinf) * RK_SCALE
    qb, kb, vb = q.astype(BF16), k.astype(BF16), rv.astype(BF16)
    sc = _dot_nt(qb, kb) * dm
    nloc = lax.broadcasted_iota(jnp.int32, (RET_T, 1), 0).astype(F32)
    qdec = jnp.exp(lg * (nloc + 1.0))
    kdec = jnp.exp(lg * (RET_T - 1.0 - nloc))
    block_dec = jnp.exp(jnp.full((1, LANES), lg * RET_T, F32))
    return q, k, qb, kb, vb, sc, qdec, kdec, block_dec


def _ret_specs(S, rb):
    group = lambda c0: pl.BlockSpec((RET_T, GROUP_W), lambda b: (rb(b), c0))
    return group, pl.BlockSpec((RET_T, LANES), lambda b: (rb(b), 0))


def _head(ref, h):
    return ref[:, h * LANES:(h + 1) * LANES]


def ret_fwd(proj, cosf, sinf, lgam):
    S = proj.shape[0]
    nb = S // RET_T
    group, row_tab = _ret_specs(S, lambda b: b)

    def body(lg_ref, rq_ref, rk_ref, rv_ref, rg_ref, cos_ref, sin_ref, y_ref, o_ref, st_out, st_ref, dm_ref):
        @pl.when(pl.program_id(0) == 0)
        def _():
            st_ref[...] = jnp.zeros_like(st_ref)
            for h in range(4):
                dm_ref[h] = _ret_decay_mask(lg_ref[h])

        cosf, sinf = cos_ref[...], sin_ref[...]
        H = range(4)
        lanes = [slice(h * LANES, (h + 1) * LANES) for h in H]
        blk = [_ret_block(lg_ref[h], _head(rq_ref, h), _head(rk_ref, h), _head(rv_ref, h), cosf, sinf, dm_ref[h])
               for h in H]
        q, k, qb, kb, vb, sc, qdec, kdec, block_dec = zip(*blk)
        st = [st_ref[h] for h in H]
        for h in H:
            st_out[h, 0] = st[h]
        o = [_dot(sc[h].astype(BF16), vb[h]) + _dot(qb[h], st[h].astype(BF16)) * qdec[h] for h in H]
        for h in H:
            st_ref[h] = st[h] * block_dec[h] + _dot_tn((k[h] * kdec[h]).astype(BF16), vb[h])
        for h in H:
            o_ref[:, lanes[h]] = o[h]
        cen = [o[h] - _rowmean(o[h]) for h in H]
        on = [cen[h] * lax.rsqrt(_rowmean(cen[h] * cen[h]) + EPS) for h in H]
        rg = [_head(rg_ref, h) for h in H]
        for h in H:
            y_ref[:, lanes[h]] = (on[h] * (rg[h] * _sigmoid(rg[h]))).astype(BF16)

    return pl.pallas_call(
        body, name="ret_fwd", grid=(nb,),
        in_specs=[pl.BlockSpec(memory_space=pltpu.SMEM),
                  group(0), group(1), group(2), group(3), row_tab, row_tab],
        out_specs=[group(0), group(0),
                   pl.BlockSpec((4, 1, LANES, LANES), lambda b: (0, b, 0, 0))],
        out_shape=[jax.ShapeDtypeStruct((S, GROUP_W), BF16),
                   jax.ShapeDtypeStruct((S, GROUP_W), F32),
                   jax.ShapeDtypeStruct((4, nb, LANES, LANES), F32)],
        scratch_shapes=[pltpu.VMEM((4, LANES, LANES), F32), pltpu.VMEM((4, RET_T, RET_T), F32)],
        compiler_params=_cp("arbitrary"),
    )(lgam, proj, proj, proj, proj, cosf, sinf)


def ret_bwd(proj, cosf, sinf, lgam, o, states, dycat):
    S = proj.shape[0]
    nb = S // RET_T
    rev = lambda b: nb - 1 - b
    group, row_tab = _ret_specs(S, rev)

    def body(lg_ref, rq_ref, rk_ref, rv_ref, rg_ref, cos_ref, sin_ref, o_ref, st_in, dy_ref,
             drq_ref, drk_ref, drv_ref, drg_ref, ds_ref, dm_ref):
        @pl.when(pl.program_id(0) == 0)
        def _():
            ds_ref[...] = jnp.zeros_like(ds_ref)
            for h in range(4):
                dm_ref[h] = _ret_decay_mask(lg_ref[h])

        cosf, sinf = cos_ref[...], sin_ref[...]
        H = range(4)
        lanes = [slice(h * LANES, (h + 1) * LANES) for h in H]
        dms = [dm_ref[h] for h in H]
        blk = [_ret_block(lg_ref[h], _head(rq_ref, h), _head(rk_ref, h), _head(rv_ref, h), cosf, sinf, dms[h])
               for h in H]
        q, k, qb, kb, vb, sc, qdec, kdec, block_dec = zip(*blk)
        o_v = [_head(o_ref, h) for h in H]
        cen = [o_v[h] - _rowmean(o_v[h]) for h in H]
        rstd = [lax.rsqrt(_rowmean(cen[h] * cen[h]) + EPS) for h in H]
        on = [cen[h] * rstd[h] for h in H]
        rg = [_head(rg_ref, h) for h in H]
        sig = [_sigmoid(rg[h]) for h in H]
        dy = [_head(dy_ref, h) for h in H]
        for h in H:
            drg_ref[:, lanes[h]] = (dy[h] * on[h] * (sig[h] * (1.0 + rg[h] * (1.0 - sig[h])))).astype(BF16)
        don = [dy[h] * (rg[h] * sig[h]) for h in H]
        do = [rstd[h] * (don[h] - _rowmean(don[h]) - on[h] * _rowmean(don[h] * on[h])) for h in H]
        dob = [do[h].astype(BF16) for h in H]
        dsc = [(_dot_nt(dob[h], vb[h]) * dms[h]).astype(BF16) for h in H]
        st_b = [st_in[h, 0].astype(BF16) for h in H]
        dsn = [ds_ref[h] for h in H]
        dsn_b = [dsn[h].astype(BF16) for h in H]
        dq = [_dot(dsc[h], kb[h]) + _dot_nt(dob[h], st_b[h]) * qdec[h] for h in H]
        dk = [(_dot_tn(dsc[h], qb[h]) + _dot_nt(vb[h], dsn_b[h]) * kdec[h]) * RK_SCALE for h in H]
        dv = [_dot_tn(sc[h].astype(BF16), dob[h]) + _dot((k[h] * kdec[h]).astype(BF16), dsn_b[h]) for h in H]
        for h in H:
            ds_ref[h] = dsn[h] * block_dec[h] + _dot_tn((q[h] * qdec[h]).astype(BF16), dob[h])
        for h in H:
            drq_ref[:, lanes[h]] = (dq[h] * cosf + _swap_halves(dq[h] * sinf)).astype(BF16)
            drk_ref[:, lanes[h]] = (dk[h] * cosf + _swap_halves(dk[h] * sinf)).astype(BF16)
            drv_ref[:, lanes[h]] = dv[h].astype(BF16)

    return pl.pallas_call(
        body, name="ret_bwd", grid=(nb,),
        in_specs=[pl.BlockSpec(memory_space=pltpu.SMEM),
                  group(0), group(1), group(2), group(3), row_tab, row_tab,
                  group(0), pl.BlockSpec((4, 1, LANES, LANES), lambda b: (0, rev(b), 0, 0)),
                  group(0)],
        out_specs=[group(0)] * 4,
        out_shape=[jax.ShapeDtypeStruct((S, GROUP_W), BF16)] * 4,
        scratch_shapes=[pltpu.VMEM((4, LANES, LANES), F32), pltpu.VMEM((4, RET_T, RET_T), F32)],
        compiler_params=_cp("arbitrary"),
    )(lgam, proj, proj, proj, proj, cosf, sinf, o, states, dycat)


def outproj_fwd(x, vecs, y_ret, y_sb, w_out, head=None, tm=1024):
    S, D = x.shape
    tm = min(tm, S)
    last = list(head or ())

    def body(x_ref, v_ref, yr_ref, ys_ref, w_ref, *refs):
        y = _dot(yr_ref[...], w_ref[0:GROUP_W, :]) + _dot(ys_ref[...], w_ref[GROUP_W:, :])
        xv = x_ref[...] + v_ref[2:3, :] * y
        if not last:
            y_ref, xo_ref = refs
            y_ref[...] = y.astype(BF16)
            xo_ref[...] = xv
            return
        g_ref, t_ref, y_ref, dx_ref, st_ref = refs
        y_ref[...] = y.astype(BF16)

        @pl.when(pl.program_id(0) == 0)
        def _():
            st_ref[...] = jnp.zeros_like(st_ref)

        g = g_ref[0:1, :]
        r = lax.rsqrt(_rowmean(xv * xv) + EPS)
        xn = xv * r
        err = xn * g - t_ref[...]
        dy = err * (1.0 / D)
        dxn = dy * g
        dx_ref[...] = r * (dxn - xn * _rowmean(dxn * xn))
        st_ref[0:1, :] += jnp.sum(dy * xn, axis=0, keepdims=True)
        st_ref[1:2, :] += jnp.sum(err * err, axis=0, keepdims=True)

    row = lambda w: pl.BlockSpec((tm, w), lambda i: (i, 0))
    fixed = pl.BlockSpec((8, D), lambda i: (0, 0))
    return pl.pallas_call(
        body, name="outproj_fwd", grid=(S // tm,),
        in_specs=[row(D), fixed, row(GROUP_W), row(GROUP_W), pl.BlockSpec((D, D), lambda i: (0, 0))]
        + ([fixed, row(D)] if last else []),
        out_specs=[row(D), row(D)] + ([fixed] if last else []),
        out_shape=[jax.ShapeDtypeStruct((S, D), BF16), jax.ShapeDtypeStruct((S, D), F32)]
        + ([jax.ShapeDtypeStruct((8, D), F32)] if last else []),
        compiler_params=_cp("arbitrary"),
    )(x, vecs, y_ret, y_sb, w_out, *last)


def outproj_bwd(dx, y, vecs, y_ret, y_sb, w_out, tm=1024):
    S, D = dx.shape
    tm = min(tm, S)
    n = S // tm

    def body(dx_ref, y_ref, v_ref, yr_ref, ys_ref, w_ref, dyc_ref, dw_ref, st_ref, acc):
        i = pl.program_id(0)

        @pl.when(i == 0)
        def _():
            st_ref[...] = jnp.zeros_like(st_ref)
            acc[...] = jnp.zeros_like(acc)

        dxv = dx_ref[...]
        st_ref[0:1, :] += jnp.sum(dxv * y_ref[...].astype(F32), axis=0, keepdims=True)
        dyy = (dxv * v_ref[2:3, :]).astype(BF16)
        dyc_ref[...] = _dot_nt(dyy, w_ref[...])
        acc[0:GROUP_W, :] += _dot_tn(yr_ref[...], dyy)
        acc[GROUP_W:, :] += _dot_tn(ys_ref[...], dyy)

        @pl.when(i == n - 1)
        def _():
            dw_ref[...] = acc[...].astype(BF16)

    row = lambda w: pl.BlockSpec((tm, w), lambda i: (i, 0))
    fixed = lambda r: pl.BlockSpec((r, D), lambda i: (0, 0))
    return pl.pallas_call(
        body, name="outproj_bwd", grid=(n,),
        in_specs=[row(D), row(D), fixed(8), row(GROUP_W), row(GROUP_W), fixed(D)],
        out_specs=[row(D), fixed(D), fixed(8)],
        out_shape=[jax.ShapeDtypeStruct((S, D), F32), jax.ShapeDtypeStruct((D, D), BF16),
                   jax.ShapeDtypeStruct((8, D), F32)],
        scratch_shapes=[pltpu.VMEM((D, D), F32)],
        compiler_params=_cp("arbitrary"),
    )(dx, y, vecs, y_ret, y_sb, w_out)


def inproj_bwd_x(pieces, w3, x, vecs, dx_res, ship=None, tm=512):
    S, D = x.shape
    n = S // tm
    ex = _Exchange(ship)

    def body(*refs):
        p_refs, (w_ref, x_ref, v_ref, dr_ref), refs = refs[:8], refs[8:12], refs[12:]
        ship_refs, (dx_ref, st_ref), refs = refs[:ex.n_in], refs[ex.n_in:ex.n_in + 2], refs[ex.n_in + 2:]
        start, finish = ex.ops(ship_refs, refs)

        @pl.when(pl.program_id(0) == 0)
        def _():
            st_ref[...] = jnp.zeros_like(st_ref)
            start()

        dh = jnp.zeros((tm, D), F32)
        for k, p_ref in enumerate(p_refs):
            c0 = (k % 2) * GROUP_W
            dh = dh + _dot_nt(p_ref[...], w_ref[k // 2, :, c0:c0 + GROUP_W])
        xv = x_ref[...]
        r = lax.rsqrt(_rowmean(xv * xv) + EPS)
        xn = xv * r
        g, scale1 = v_ref[3:4, :], 1.0 + v_ref[1:2, :]
        st_ref[0:1, :] += jnp.sum(dh, axis=0, keepdims=True)
        dh_xn = dh * xn
        st_ref[1:2, :] += jnp.sum(dh_xn, axis=0, keepdims=True) * g
        st_ref[2:3, :] += jnp.sum(dh_xn, axis=0, keepdims=True) * scale1
        dxn = dh * (g * scale1)
        dx_ref[...] = r * (dxn - xn * _rowmean(dxn * xn)) + dr_ref[...]
        pl.when(pl.program_id(0) == n - 1)(finish)

    row = lambda w: pl.BlockSpec((tm, w), lambda i: (i, 0))
    return pl.pallas_call(
        body, name="inproj_bwd_x", grid=(n,),
        in_specs=[row(GROUP_W)] * 8 + [pl.BlockSpec((N_SHARD, D, SHARD_W), lambda i: (0, 0, 0)),
                                       row(D), pl.BlockSpec((8, D), lambda i: (0, 0)), row(D)] + ex.in_specs,
        out_specs=[row(D), pl.BlockSpec((8, D), lambda i: (0, 0))] + ex.out_specs,
        out_shape=[jax.ShapeDtypeStruct((S, D), F32), jax.ShapeDtypeStruct((8, D), F32)] + ex.out_shape,
        scratch_shapes=ex.scratch,
        compiler_params=_cp("arbitrary"),
    )(*pieces, w3, x, vecs, dx_res, *ex.ship)


def inproj_bwd_w(h, pieces, tm=1024):
    S, D = h.shape
    tm = min(tm, S)
    n = S // tm

    def body(*refs):
        h_ref, p_refs, dw_ref, acc = refs[0], refs[1:9], refs[9], refs[10]
        i = pl.program_id(0)

        @pl.when(i == 0)
        def _():
            acc[...] = jnp.zeros_like(acc)

        hv = h_ref[...]
        for k, p_ref in enumerate(p_refs):
            c0 = (k % 2) * GROUP_W
            acc[k // 2, :, c0:c0 + GROUP_W] += _dot_tn(hv, p_ref[...])

        @pl.when(i == n - 1)
        def _():
            dw_ref[...] = acc[...].astype(BF16)

    row = lambda w: pl.BlockSpec((tm, w), lambda i: (i, 0))
    return pl.pallas_call(
        body, name="inproj_bwd_w", grid=(n,),
        in_specs=[row(D)] + [row(GROUP_W)] * 8,
        out_specs=pl.BlockSpec((N_SHARD, D, SHARD_W), lambda i: (0, 0, 0), pipeline_mode=pl.Buffered(1)),
        out_shape=jax.ShapeDtypeStruct((N_SHARD, D, SHARD_W), BF16),
        scratch_shapes=[pltpu.VMEM((N_SHARD, D, SHARD_W), F32)],
        compiler_params=_cp("arbitrary"),
    )(h, *pieces)


def layer_fwd(x, vecs, w3, w_out, tabs, gather=None, head=None):
    cosf, sinf, lgam = tabs
    ret, sg, h, sb = inproj_fwd(x, vecs, w3)
    y_ret, o_ret, states = ret_fwd(ret, cosf, sinf, lgam)
    y_sb, o_sb, sb_end, *gathered = sb_fwd(sb, sg, gather)
    if callable(w_out):
        w_out = w_out(gathered[0])
    y, *x_next = outproj_fwd(x, vecs, y_ret, y_sb, w_out, head)
    saved = (x, ret, sg, h, sb, y_ret, o_ret, states, y_sb, o_sb, sb_end, y)
    return (x_next[0] if head is None else x_next), saved, (gathered[0] if gathered else None)


def _by_shard(dw_out):
    return dw_out.reshape(N_SHARD, D_MODEL // N_SHARD, D_MODEL)


def layer_bwd(dx, saved, vecs, w3, w_out, tabs, later_grads=None):
    cosf, sinf, lgam = tabs
    x, ret, sg, h, sb, y_ret, o_ret, states, y_sb, o_sb, sb_end, y = saved
    dycat, dw_out, st_o = outproj_bwd(dx, y, vecs, y_ret, y_sb, w_out)
    dw_out = _by_shard(dw_out)
    ship = None if later_grads is None else (later_grads[0], dw_out, later_grads[1])
    *d_sb, = sb_bwd(sb, sg, o_sb, sb_end, dycat, ship)
    d_ret = ret_bwd(ret, cosf, sinf, lgam, o_ret, states, dycat)
    pieces = list(d_ret) + d_sb[:4]
    dw_in = inproj_bwd_w(h, pieces)
    dx, st_i, *recv_in = inproj_bwd_x(pieces, w3, x, vecs, dx, None if later_grads is None else (dw_in,))
    dmod = jnp.concatenate([st_i[0:2], st_o[0:1]], axis=0)
    grads = (dw_in, dw_out) if later_grads is None else (recv_in[0], d_sb[4])
    return dx, dmod, st_i[2:3], grads


def _place():
    return lax.axis_index("x"), lax.axis_index("y"), lax.axis_index("c")


def _other_chips(mx, my):
    return [(1 - mx, my), (mx, 1 - my), (1 - mx, 1 - my)]


_ANY = pl.BlockSpec(memory_space=pl.ANY)


_GATHER_SCRATCH = [pltpu.SemaphoreType.DMA((7,)), pltpu.SemaphoreType.DMA((7,)), pltpu.SemaphoreType.DMA(())]


def _gather_ops(x_ref, out_ref, send_sems, recv_sems, local_sem):
    mx, my, mc = _place()
    me, sibling = (mx, my, mc), (mx, my, 1 - mc)
    chips = _other_chips(mx, my)

    def slot(px, py, pc):
        return out_ref.at[4 * px + 2 * py + pc]

    def copy(k, block, to, src=None):
        return pltpu.make_async_remote_copy(
            src_ref=slot(*block) if src is None else src, dst_ref=slot(*block),
            send_sem=send_sems.at[k], recv_sem=recv_sems.at[k], device_id=to, device_id_type=MESH)

    mine = pltpu.make_async_copy(x_ref, slot(*me), local_sem)
    first = [copy(0, me, sibling, src=x_ref)]
    first += [copy(1 + j, me, (*chip, mc), src=x_ref) for j, chip in enumerate(chips)]
    passed = [copy(4 + j, (*chip, mc), sibling) for j, chip in enumerate(chips)]

    def start():
        mine.start()
        for cp in first:
            cp.start()

    def forward():
        for j, chip in enumerate(chips):
            copy(1 + j, (*chip, mc), me).wait_recv()
            passed[j].start()

    def finish():
        copy(0, sibling, me).wait_recv()
        for j, chip in enumerate(chips):
            copy(4 + j, (*chip, 1 - mc), me).wait_recv()
        for cp in first + passed:
            cp.wait_send()
        mine.wait()

    return start, forward, finish


def allgather8(x, name):
    def body(x_ref, out_ref, send_sems, recv_sems, local_sem):
        for step in _gather_ops(x_ref, out_ref, send_sems, recv_sems, local_sem):
            step()

    return pl.pallas_call(
        body, name=name, out_shape=jax.ShapeDtypeStruct((8,) + x.shape, x.dtype),
        in_specs=[_ANY], out_specs=_ANY, scratch_shapes=_GATHER_SCRATCH,
    )(x)


class _Exchange:
    def __init__(self, ship):
        self.ship = list(ship or ())
        self.n_in = len(self.ship)
        self.n_out = 1 if self.ship else 0
        self.rows = [a.shape[1] for a in self.ship]
        self.in_specs = [_ANY] * self.n_in
        self.out_specs = [_ANY] * self.n_out
        self.out_shape = [jax.ShapeDtypeStruct((N_SHARD, sum(self.rows), SHARD_W), BF16)] * self.n_out
        sem = pltpu.SemaphoreType.DMA
        self.scratch = [sem((3,)), sem((3,)), sem(())] * self.n_out

    def ops(self, ship_refs, tail):
        if not self.ship:
            return (lambda: None), (lambda: None)
        recv, send_sems, recv_sems, local_sem = tail
        mx, my, mc = _place()
        my_chip = 2 * mx + my
        chips = _other_chips(mx, my)

        def pieces(s):
            firsts = np.cumsum([0] + self.rows[:-1])
            return [(ref.at[s], int(r0), n) for ref, r0, n in zip(ship_refs, firsts, self.rows)]

        def start():
            for src, r0, n in pieces(my_chip):
                pltpu.make_async_copy(src, recv.at[my_chip, pl.ds(r0, n)], local_sem).start()
            for j, (px, py) in enumerate(chips):
                for src, r0, n in pieces(2 * px + py):
                    pltpu.make_async_remote_copy(
                        src_ref=src, dst_ref=recv.at[my_chip, pl.ds(r0, n)],
                        send_sem=send_sems.at[j], recv_sem=recv_sems.at[j],
                        device_id=(px, py, mc), device_id_type=MESH).start()

        def finish():
            for j, (px, py) in enumerate(chips):
                whole = recv.at[2 * px + py]
                both = pltpu.make_async_remote_copy(
                    src_ref=whole, dst_ref=whole, send_sem=send_sems.at[j], recv_sem=recv_sems.at[j],
                    device_id=(px, py, mc), device_id_type=MESH)
                both.wait_recv()
                both.wait_send()
            pltpu.make_async_copy(recv.at[my_chip], recv.at[my_chip], local_sem).wait()

        return start, finish


def sum_slots(recv_a, recv_b, tr=256):
    n, rows_a, cols = recv_a.shape
    na, nb = rows_a // tr, recv_b.shape[1] // tr

    def body(a_ref, b_ref, o_ref):
        def total(r_ref):
            acc = r_ref[0].astype(F32)
            for k in range(1, n):
                acc = acc + r_ref[k].astype(F32)
            o_ref[...] = acc

        pl.when(pl.program_id(0) < na)(lambda: total(a_ref))
        pl.when(pl.program_id(0) >= na)(lambda: total(b_ref))

    return pl.pallas_call(
        body, name="sum_slots", grid=(na + nb,),
        in_specs=[pl.BlockSpec((n, tr, cols), lambda i: (0, jnp.minimum(i, na - 1), 0)),
                  pl.BlockSpec((n, tr, cols), lambda i: (0, jnp.maximum(i - na, 0), 0))],
        out_specs=pl.BlockSpec((tr, cols), lambda i: (i, 0)),
        out_shape=jax.ShapeDtypeStruct(((na + nb) * tr, cols), F32),
        compiler_params=_cp("arbitrary"),
    )(recv_a, recv_b)


def swap_sibling(p):
    def body(p_ref, out_ref, send_sem, recv_sem):
        mx, my, mc = _place()
        cp = pltpu.make_async_remote_copy(
            src_ref=p_ref, dst_ref=out_ref, send_sem=send_sem, recv_sem=recv_sem,
            device_id=(mx, my, 1 - mc), device_id_type=MESH)
        cp.start()
        cp.wait()

    return pl.pallas_call(
        body, name="swap_sibling", out_shape=jax.ShapeDtypeStruct(p.shape, p.dtype),
        in_specs=[_ANY], out_specs=_ANY,
        scratch_shapes=[pltpu.SemaphoreType.DMA(()), pltpu.SemaphoreType.DMA(())],
    )(p)


def _adamw(w, g, m, v):
    m = ADAM_B1 * m + (1.0 - ADAM_B1) * g
    v = ADAM_B2 * v + (1.0 - ADAM_B2) * (g * g)
    m_hat = m / (1.0 - ADAM_B1 ** ADAM_STEP)
    v_hat = v / (1.0 - ADAM_B2 ** ADAM_STEP)
    delta = -ADAM_LR * (m_hat / (jnp.sqrt(v_hat) + ADAM_EPS) + ADAM_WD * w)
    return delta, m, v


def adam_slab(p_own, p_sib, w, m, v, row0, name, tr=256):
    L, R, C = w.shape
    nr = R // tr

    def body(a_ref, b_ref, w_ref, m_ref, v_ref, g_out, d_out, m_out, v_out):
        g = a_ref[...] + b_ref[...]
        d, m2, v2 = _adamw(w_ref[0], g, m_ref[0], v_ref[0])
        g_out[0], d_out[0], m_out[0], v_out[0] = g, d, m2, v2

    slab = pl.BlockSpec((tr, C), lambda l, i: (row0 // tr + l * nr + i, 0))
    blk = pl.BlockSpec((1, tr, C), lambda l, i: (l, i, 0))
    return pl.pallas_call(
        body, name=name, grid=(L, nr),
        in_specs=[slab, slab, blk, blk, blk], out_specs=[blk] * 4,
        out_shape=[jax.ShapeDtypeStruct(w.shape, F32)] * 4,
        compiler_params=_cp("arbitrary", "arbitrary"),
    )(p_own, p_sib, w, m, v)


def ada_fwd(c_all, w_ada):
    L, D, W = w_ada.shape

    def body(c_ref, w_ref, o_ref):
        cv = c_ref[...]
        o_ref[0] = jnp.dot(cv * _sigmoid(cv), w_ref[0], precision=lax.Precision.HIGHEST,
                           preferred_element_type=F32)

    return pl.pallas_call(
        body, name="ada_fwd", grid=(L,),
        in_specs=[pl.BlockSpec((8, D), lambda l: (0, 0)), pl.BlockSpec((1, D, W), lambda l: (l, 0, 0))],
        out_specs=pl.BlockSpec((1, 8, W), lambda l: (l, 0, 0)),
        out_shape=jax.ShapeDtypeStruct((L, 8, W), F32),
        compiler_params=_cp("arbitrary"),
    )(c_all, w_ada)


def vecs_build(mod_all, b_ada, norm_g):
    W = mod_all.shape[2]

    def body(m_ref, b_ref, g_ref, o_ref):
        mx, my, mc = _place()
        me = 4 * mx + 2 * my + mc
        rowid = lax.broadcasted_iota(jnp.int32, (2 * 8, 1), 0)
        o_ref[...] = jnp.zeros_like(o_ref)
        for l in range(DEPTH):
            parts = [jnp.sum(jnp.where(rowid == l * 8 + me, m_ref[2 * s + mc], 0.0), axis=0, keepdims=True)
                     for s in range(N_SHARD)]
            mod = jnp.concatenate(parts, axis=1) + b_ref[l:l + 1, :]
            for t in range(3):
                o_ref[l, t:t + 1, :] = mod[:, t * D_MODEL:(t + 1) * D_MODEL]
            o_ref[l, 3:4, :] = g_ref[l:l + 1, :]

    return pl.pallas_call(
        body, name="vecs_build", out_shape=jax.ShapeDtypeStruct((DEPTH, 8, D_MODEL), F32),
    )(mod_all, b_ada, norm_g)


def ada_update(dmods, c_t, w, m, v, tr=256):
    L, D, W = w.shape

    def body(dm_ref, c_ref, w_ref, m_ref, v_ref, g_out, d_out, m_out, v_out):
        mx, my, _ = _place()
        shard = 2 * mx + my
        dm = jnp.zeros((8, W), F32)
        for s in range(N_SHARD):
            dm = dm + jnp.where(shard == s, dm_ref[0, :, s * W:(s + 1) * W], 0.0)
        cv = c_ref[...]
        ca = cv * _sigmoid(cv)
        g = jnp.zeros((tr, W), F32)
        for b in range(8):
            g = g + ca[:, b:b + 1] * dm[b:b + 1, :]
        d, m2, v2 = _adamw(w_ref[0], g, m_ref[0], v_ref[0])
        g_out[0], d_out[0], m_out[0], v_out[0] = g, d, m2, v2

    blk = pl.BlockSpec((1, tr, W), lambda l, i: (l, i, 0))
    return pl.pallas_call(
        body, name="ada_update", grid=(L, D // tr),
        in_specs=[pl.BlockSpec((1, 8, 3 * D), lambda l, i: (l, 0, 0)), pl.BlockSpec((tr, 8), lambda l, i: (i, 0)),
                  blk, blk, blk],
        out_specs=[blk] * 4, out_shape=[jax.ShapeDtypeStruct(w.shape, F32)] * 4,
        compiler_params=_cp("arbitrary", "arbitrary"),
    )(dmods, c_t, w, m, v)


STAT_ROWS = 16


def small_update(stats_all, norm, b_ada, final):
    def body(s_ref, *refs):
        ins, outs = refs[:9], refs[9:]
        tot = s_ref[0]
        for k in range(1, 8):
            tot = tot + s_ref[k]
        g_norm = tot[0:2, :]
        g_final = tot[2:3, :]
        g_b = jnp.concatenate(
            [jnp.concatenate([tot[3 + 3 * l + t:4 + 3 * l + t, :] for t in range(3)], axis=1) for l in range(DEPTH)],
            axis=0)
        for p, g in enumerate((g_norm, g_b, g_final)):
            w_ref, m_ref, v_ref = ins[3 * p:3 * p + 3]
            d, m2, v2 = _adamw(w_ref[...], g, m_ref[...], v_ref[...])
            for o_ref, val in zip(outs[4 * p:4 * p + 4], (g, d, m2, v2)):
                o_ref[...] = val
        loss = (0.5 / D_MODEL) * jnp.sum(tot[9:10, :], axis=1, keepdims=True)
        outs[12][...] = jnp.broadcast_to(loss, (8, LANES))

    shapes = []
    for w, _, _ in (norm, b_ada, final):
        shapes += [jax.ShapeDtypeStruct(w.shape, F32)] * 4
    shapes.append(jax.ShapeDtypeStruct((8, LANES), F32))
    return pl.pallas_call(body, name="small_update", out_shape=shapes)(stats_all, *norm, *b_ada, *final)


def kernel(x, c, norm_g, w_ada, b_ada, w_in, w_out, final_g, loss_target, m_norm_g, m_w_ada, m_b_ada, m_w_in, m_w_out, m_final_g, v_norm_g, v_w_ada, v_b_ada, v_w_in, v_w_out, v_final_g):
    S, D = x.shape[1], x.shape[2]
    mc = lax.axis_index("c")
    out_rows = D // N_SHARD

    def my_half(a, rows):
        return lax.dynamic_slice_in_dim(a, mc * rows, rows, axis=0)

    assert DEPTH == 2
    win = [my_half(w_in[l], D // 2).astype(BF16) for l in range(DEPTH)]
    wout = [my_half(w_out[l], out_rows // 2).astype(BF16) for l in range(DEPTH)]
    w3_first = allgather8(win[0], "gather_weights").reshape(N_SHARD, D, SHARD_W)
    rest = jnp.concatenate([wout[0], win[1], wout[1]], axis=0)

    def unpack(wall):
        wall = wall.reshape(N_SHARD, 2, rest.shape[0], SHARD_W)
        a, b = out_rows // 2, out_rows // 2 + D // 2
        return wall[:, :, :a].reshape(D, D), (wall[:, :, a:b].reshape(N_SHARD, D, SHARD_W), wall[:, :, b:].reshape(D, D))

    c_all = allgather8(jnp.broadcast_to(c, (8, D)), "gather_c")[:, 0, :]
    mod_all = allgather8(ada_fwd(c_all, w_ada).reshape(DEPTH * 8, -1), "gather_mod")
    vecs = vecs_build(mod_all, b_ada, norm_g)

    tabs = (*rope_tables(S), ret_log_gamma())
    saved = [None] * DEPTH
    h, saved[0], wall = layer_fwd(x[0], vecs[0], w3_first, lambda g: unpack(g)[0], tabs, rest)
    weights = [(w3_first, unpack(wall)[0]), unpack(wall)[1]]
    head = (jnp.broadcast_to(final_g[None, :], (8, D)), loss_target[0])
    (dx, st_loss), saved[1], _ = layer_fwd(h, vecs[1], *weights[1], tabs, head=head)

    dmod, dnorm, grads = [None] * DEPTH, [None] * DEPTH, None
    for l in reversed(range(DEPTH)):
        dx, dmod[l], dnorm[l], grads = layer_bwd(dx, saved[l], vecs[l], *weights[l], tabs, grads)

    p_own = sum_slots(*grads)
    p_sib = swap_sibling(p_own)
    res_in = adam_slab(p_own, p_sib, w_in, m_w_in, v_w_in, 0, "adam_w_in")
    res_out = adam_slab(p_own, p_sib, w_out, m_w_out, v_w_out, DEPTH * D, "adam_w_out", tr=128)

    stats = jnp.concatenate(dnorm + [st_loss[0:1]] + dmod + [st_loss[1:2], jnp.zeros((STAT_ROWS - 10, D), F32)], axis=0)
    stats_all = allgather8(stats, "gather_stats")
    dmods = stats_all[:, 3:9, :].reshape(8, DEPTH, 3 * D).transpose(1, 0, 2)
    res_ada = ada_update(dmods, c_all.T, w_ada, m_w_ada, v_w_ada)
    small = small_update(stats_all, (norm_g, m_norm_g, v_norm_g), (b_ada, m_b_ada, v_b_ada),
                         (final_g[None, :], m_final_g[None, :], v_final_g[None, :]))
    res_norm, res_b, res_final = small[0:4], small[4:8], [a[0] for a in small[8:12]]
    loss = small[12][0, 0]

    by_kind = [res_norm, res_ada, res_b, res_in, res_out, res_final]
    outs = [loss, dx[None]]
    for kind in range(4):
        outs += [r[kind] for r in by_kind]
    return tuple(outs)
```

```python
import functools

import numpy as np
import jax
import jax.numpy as jnp
from jax import lax
from jax.experimental import pallas as pl
from jax.experimental.pallas import tpu as pltpu

F32, BF16 = jnp.float32, jnp.bfloat16
MESH = pl.DeviceIdType.MESH

D_MODEL = 1024
DEPTH = 2
SHARD_W = 1024
N_SHARD = 4
GROUP_W = 512
LANES = 128
SB_HEAD_DIM = 64
RET_HEAD_DIM = 128
CHUNK = 64
ROPE_BASE = 10000.0
EPS = 1e-6
SQ_SCALE = SB_HEAD_DIM ** -0.5
RK_SCALE = RET_HEAD_DIM ** -0.5
SB_T = 512
SB_CHAINS = 8
SB_NB = 4
RET_T = 256
EXP_ZERO = -104.0
VMEM_LIMIT_BYTES = 56 * 2 ** 20

ADAM_LR, ADAM_B1, ADAM_B2, ADAM_EPS, ADAM_WD, ADAM_STEP = 0.001, 0.9, 0.999, 1e-08, 0.01, 10


def _cp(*sem):
    return pltpu.CompilerParams(dimension_semantics=sem, vmem_limit_bytes=VMEM_LIMIT_BYTES)


def _dot(a, b):
    return lax.dot_general(a, b, (((1,), (0,)), ((), ())), preferred_element_type=F32)


def _dot_nt(a, b):
    return lax.dot_general(a, b, (((1,), (1,)), ((), ())), preferred_element_type=F32)


def _dot_tn(a, b):
    return lax.dot_general(a, b, (((0,), (0,)), ((), ())), preferred_element_type=F32)


def _running_sum(a, tri):
    return _dot(a.astype(BF16), tri)


def _sigmoid(x):
    return 1.0 / (1.0 + jnp.exp(-x))


def _rowsum(a):
    return jnp.sum(a, axis=1, keepdims=True)


def _rowmean(a):
    return jnp.mean(a, axis=1, keepdims=True)


def inproj_fwd(x, vecs, w3, tm=512):
    S, D = x.shape

    def body(x_ref, v_ref, w_ref, ret_ref, sg_ref, h_ref, sb_ref):
        xv = x_ref[...]
        r = lax.rsqrt(_rowmean(xv * xv) + EPS)
        h = xv * r * v_ref[3:4, :] * (1.0 + v_ref[1:2, :]) + v_ref[0:1, :]
        hb = h.astype(BF16)
        h_ref[...] = hb
        for s in range(N_SHARD):
            p = _dot(hb, w_ref[s])
            if s < 2:
                ret_ref[:, s * SHARD_W:(s + 1) * SHARD_W] = p
            if s == 2:
                sb_ref[:, 0:GROUP_W] = (p[:, 0:GROUP_W] * SQ_SCALE).astype(BF16)
                sb_ref[:, GROUP_W:SHARD_W] = p[:, GROUP_W:].astype(BF16)
            if s == 3:
                sb_ref[:, SHARD_W:SHARD_W + GROUP_W] = p[:, 0:GROUP_W].astype(BF16)
                sg_ref[...] = p[:, GROUP_W:]

    row = lambda w: pl.BlockSpec((tm, w), lambda i: (i, 0))
    return pl.pallas_call(
        body, name="inproj_fwd", grid=(S // tm,),
        in_specs=[row(D), pl.BlockSpec((8, D), lambda i: (0, 0)),
                  pl.BlockSpec((N_SHARD, D, SHARD_W), lambda i: (0, 0, 0))],
        out_specs=[row(2 * SHARD_W), row(GROUP_W), row(D), row(3 * GROUP_W)],
        out_shape=[jax.ShapeDtypeStruct((S, 2 * SHARD_W), F32), jax.ShapeDtypeStruct((S, GROUP_W), F32),
                   jax.ShapeDtypeStruct((S, D), BF16), jax.ShapeDtypeStruct((S, 3 * GROUP_W), BF16)],
        compiler_params=_cp("arbitrary"),
    )(x, vecs, w3)


def _sb_logits(qh, k2, keep):
    z = _dot_nt(qh, k2)
    sp = jnp.log(1.0 + jnp.exp(-jnp.abs(z)))
    lb = jnp.minimum(z, 0.0) - sp
    lk = lb - z
    if keep is not None:
        lk = jnp.where(keep, lk, 0.0)
    return lb, lk


class _sb_chains:
    def __init__(self, i, q2, do_b=None):
        t = self.t = SB_T // SB_CHAINS
        self.C = range(SB_CHAINS)
        r = lax.broadcasted_iota(jnp.int32, (SB_NB * t, SB_NB * t), 0)
        c = lax.broadcasted_iota(jnp.int32, (SB_NB * t, SB_NB * t), 1)
        self.later_all = jnp.where(r > c, 1.0, 0.0).astype(BF16)
        self.earlier_all = jnp.where(r < c, 1.0, 0.0).astype(BF16)
        self.later, self.earlier = self.later_all[:t, :t], self.earlier_all[:t, :t]
        self.head0 = lax.broadcasted_iota(jnp.int32, (1, LANES), 1) < SB_HEAD_DIM
        row = lax.broadcasted_iota(jnp.int32, (2 * t, SB_NB * t), 0) & (t - 1)
        col = lax.broadcasted_iota(jnp.int32, (2 * t, SB_NB * t), 1)
        qt = [SB_CHAINS * i + cc for cc in self.C]
        self.first = [jnp.maximum(qt[cc] - (SB_NB - 1), 0) for cc in self.C]
        self.keep = [self.first[cc] * t + col < qt[cc] * t + row for cc in self.C]
        self.qs = [self._stack(q2[cc * t:(cc + 1) * t]) for cc in self.C]
        if do_b is not None:
            self.dos = [self._stack(do_b[cc * t:(cc + 1) * t]) for cc in self.C]

    def _stack(self, a):
        zero = jnp.zeros_like(a)
        return jnp.concatenate([jnp.where(self.head0, a, zero), jnp.where(self.head0, zero, a)], axis=0)

    def rows(self, ref, j, n):
        return ref[pl.ds(pl.multiple_of(j * self.t, self.t), n * self.t), :]

    def suffix(self, lk):
        return _running_sum(lk, self.later_all), _rowsum(lk)

    def prefix(self, g, G0):
        return _running_sum(g, self.earlier_all) + G0


def sb_fwd(sb, sg, gather=None):
    S = sb.shape[0]
    T = SB_T
    nq = S // T
    carried = [] if gather is None else [gather]

    def body(*refs):
        (q_ref, k_ref, v_ref, sg_ref), refs = refs[:4], refs[4:]
        p, i = pl.program_id(0), pl.program_id(1)
        if carried:
            x_ref, y_ref, o_ref, end_ref, out_ref, send_sems, recv_sems, local_sem = refs
            start, forward, finish = _gather_ops(x_ref, out_ref, send_sems, recv_sems, local_sem)
            pl.when(jnp.logical_and(p == 0, i == 0))(start)
            pl.when(jnp.logical_and(p == 3, i == 0))(forward)
        else:
            y_ref, o_ref, end_ref = refs
        ch = _sb_chains(i, q_ref[...])
        later, head0 = ch.later, ch.head0
        lbk = [_sb_logits(ch.qs[c], ch.rows(k_ref, ch.first[c], SB_NB), ch.keep[c]) for c in ch.C]
        suffix, R = zip(*[ch.suffix(lbk[c][1]) for c in ch.C])
        aa = [jnp.where(ch.keep[c], jnp.exp(lbk[c][0] + suffix[c]), 0.0) for c in ch.C]
        acc = [_dot(aa[c].astype(BF16), ch.rows(v_ref, ch.first[c], SB_NB)) for c in ch.C]

        outs = []
        for c in ch.C:
            def tile(j, Rc, c=c):
                lb, lk = _sb_logits(ch.qs[c], ch.rows(k_ref, j, 1), None)
                a = jnp.exp(lb + _running_sum(lk, later) + Rc)
                return _dot(a.astype(BF16), ch.rows(v_ref, j, 1)), Rc + _rowsum(lk)

            def cond(st):
                return jnp.logical_and(st[0] >= 0, st[3] > EXP_ZERO)

            def step(st, tile=tile):
                cx, Rn = tile(st[0], st[2])
                return st[0] - 1, st[1] + cx, Rn, jnp.max(Rn)

            j_end, ac, Rc, _ = lax.while_loop(cond, step, (ch.first[c] - 1, acc[c], R[c], jnp.max(R[c])))
            base = c * (2 * ch.t + 8)
            end_ref[0, 0, base:base + 2 * ch.t, :] = jnp.broadcast_to(Rc, (2 * ch.t, 8))
            end_ref[0, 0, base + 2 * ch.t:base + 2 * ch.t + 8, :] = jnp.full((8, 8), j_end.astype(F32))
            outs.append(jnp.where(head0, ac[:ch.t], ac[ch.t:]))
        o = jnp.concatenate(outs, axis=0)
        o_ref[...] = o
        sg = sg_ref[...]
        y_ref[...] = (o * (sg * _sigmoid(sg))).astype(BF16)
        if carried:
            pl.when(jnp.logical_and(p == 3, i == nq - 1))(finish)

    return pl.pallas_call(
        body, name="sb_fwd", grid=(4, nq),
        in_specs=[pl.BlockSpec((T, LANES), lambda p, i: (i, p)),
                  pl.BlockSpec((S, LANES), lambda p, i: (0, 4 + p)),
                  pl.BlockSpec((S, LANES), lambda p, i: (0, 8 + p)),
                  pl.BlockSpec((T, LANES), lambda p, i: (i, p))] + [_ANY for _ in carried],
        out_specs=[pl.BlockSpec((T, LANES), lambda p, i: (i, p)),
                   pl.BlockSpec((T, LANES), lambda p, i: (i, p)),
                   pl.BlockSpec((1, 1, SB_CHAINS * (2 * T // SB_CHAINS + 8), 8), lambda p, i: (p, i, 0, 0))] + [_ANY for _ in carried],
        out_shape=[jax.ShapeDtypeStruct((S, GROUP_W), BF16),
                   jax.ShapeDtypeStruct((S, GROUP_W), F32),
                   jax.ShapeDtypeStruct((4, nq, SB_CHAINS * (2 * T // SB_CHAINS + 8), 8), F32)]
        + [jax.ShapeDtypeStruct((8,) + a.shape, a.dtype) for a in carried],
        scratch_shapes=_GATHER_SCRATCH if carried else [],
        compiler_params=_cp("arbitrary", "arbitrary"),
    )(sb, sb, sb, sg, *carried)


def sb_bwd(sb, sg, o, sb_end, dycat, ship=None):
    S = sb.shape[0]
    T = SB_T
    nq = S // T
    ex = _Exchange(ship)

    def body(*refs):
        (q_ref, k_ref, v_ref, sg_ref, o_ref, dy_ref, end_ref), refs = refs[:7], refs[7:]
        ship_refs, (dq_ref, dk_ref, dv_ref, dsg_ref), refs = refs[:ex.n_in], refs[ex.n_in:ex.n_in + 4], refs[ex.n_in + 4:]
        recv, (dk_acc, dv_acc), sems = refs[:ex.n_out], refs[ex.n_out:ex.n_out + 2], refs[ex.n_out + 2:]
        start, finish = ex.ops(ship_refs, recv + sems)
        p, i = pl.program_id(0), pl.program_id(1)
        pl.when(jnp.logical_and(p == 0, i == 0))(start)

        @pl.when(i == 0)
        def _():
            dk_acc[...] = jnp.zeros_like(dk_acc)
            dv_acc[...] = jnp.zeros_like(dv_acc)

        sg = sg_ref[...]
        sig = _sigmoid(sg)
        dy = dy_ref[...]
        dsg_ref[...] = (dy * o_ref[...] * (sig * (1.0 + sg * (1.0 - sig)))).astype(BF16)
        do_b = (dy * (sg * sig)).astype(BF16)
        ch = _sb_chains(i, q_ref[...], do_b)
        later, earlier, head0, t = ch.later, ch.earlier, ch.head0, ch.t
        end = end_ref[0, 0]

        def grads(c, j, n, a, lb, g, G, keep):
            dz = g - jnp.exp(lb) * (g + G)
            if keep is not None:
                dz = jnp.where(keep, dz, 0.0)
            dzb = dz.astype(BF16)
            rows = pl.ds(pl.multiple_of(j * t, t), n * t)
            dk_acc[rows, :] += _dot_tn(dzb, ch.qs[c])
            dv_acc[rows, :] += _dot_tn(a.astype(BF16), ch.dos[c])
            return _dot(dzb, ch.rows(k_ref, j, n))

        dq, G0 = [], []
        for c in ch.C:
            base = c * (2 * t + 8)
            j_end = jnp.max(end[base + 2 * t:base + 2 * t + 8, :]).astype(jnp.int32)

            def sweep(j, st, c=c):
                dqc, G0c, left = st
                lb, lk = _sb_logits(ch.qs[c], ch.rows(k_ref, j, 1), None)
                stick = left - _rowsum(lk)
                a = jnp.exp(lb + _running_sum(lk, later) + stick)
                g = a * _dot_nt(ch.dos[c], ch.rows(v_ref, j, 1))
                G = _running_sum(g, earlier) + G0c
                return dqc + grads(c, j, 1, a, lb, g, G, None), G0c + _rowsum(g), stick

            st = lax.fori_loop(j_end + 1, ch.first[c], sweep,
                               (jnp.zeros((2 * t, LANES), F32), jnp.zeros((2 * t, 1), F32), end[base:base + 2 * t, 0:1]))
            dq.append(st[0])
            G0.append(st[1])

        lbk = [_sb_logits(ch.qs[c], ch.rows(k_ref, ch.first[c], SB_NB), ch.keep[c]) for c in ch.C]
        suffix = [ch.suffix(lbk[c][1])[0] for c in ch.C]
        aa = [jnp.where(ch.keep[c], jnp.exp(lbk[c][0] + suffix[c]), 0.0) for c in ch.C]
        g = [aa[c] * _dot_nt(ch.dos[c], ch.rows(v_ref, ch.first[c], SB_NB)) for c in ch.C]
        G = [ch.prefix(g[c], G0[c]) for c in ch.C]
        for c in ch.C:
            dqc = dq[c] + grads(c, ch.first[c], SB_NB, aa[c], lbk[c][0], g[c], G[c], ch.keep[c])
            dq_ref[c * t:(c + 1) * t, :] = (jnp.where(head0, dqc[:t], dqc[t:]) * SQ_SCALE).astype(BF16)

        @pl.when(i == nq - 1)
        def _():
            dk_ref[...] = dk_acc[...].astype(BF16)
            dv_ref[...] = dv_acc[...].astype(BF16)

        pl.when(jnp.logical_and(p == 3, i == nq - 1))(finish)

    tile_spec = lambda c0: pl.BlockSpec((T, LANES), lambda p, i: (i, c0 + p))
    head_spec = lambda c0: pl.BlockSpec((S, LANES), lambda p, i: (0, c0 + p))
    return pl.pallas_call(
        body, name="sb_bwd", grid=(4, nq),
        in_specs=[tile_spec(0), head_spec(4), head_spec(8), tile_spec(0), tile_spec(0), tile_spec(4),
                  pl.BlockSpec((1, 1, SB_CHAINS * (2 * T // SB_CHAINS + 8), 8), lambda p, i: (p, i, 0, 0))] + ex.in_specs,
        out_specs=[tile_spec(0), head_spec(0), head_spec(0), tile_spec(0)] + ex.out_specs,
        out_shape=[jax.ShapeDtypeStruct((S, GROUP_W), BF16)] * 4 + ex.out_shape,
        scratch_shapes=[pltpu.VMEM((S, LANES), F32), pltpu.VMEM((S, LANES), F32)] + ex.scratch,
        compiler_params=_cp("arbitrary", "arbitrary"),
    )(sb, sb, sb, sg, o, dycat, sb_end, *ex.ship)


def rope_tables(S):
    half = RET_HEAD_DIM // 2
    inv = ROPE_BASE ** (-jnp.arange(half, dtype=F32) / half)
    ang = jnp.arange(S, dtype=F32)[:, None] * inv[None, :]
    cos, sin = jnp.cos(ang), jnp.sin(ang)
    return jnp.concatenate([cos, cos], axis=1), jnp.concatenate([-sin, sin], axis=1)


def ret_log_gamma():
    return jnp.log1p(-(2.0 ** (-5.0 - jnp.arange(4, dtype=F32))))


def _swap_halves(a):
    return pltpu.roll(a, RET_HEAD_DIM // 2, axis=1)


def _ret_decay_mask(lg):
    n = lax.broadcasted_iota(jnp.int32, (RET_T, RET_T), 0)
    m = lax.broadcasted_iota(jnp.int32, (RET_T, RET_T), 1)
    dist = jnp.abs(n - m).astype(F32)
    return jnp.where((m // CHUNK) <= (n // CHUNK), jnp.exp(lg * dist), 0.0)


def _ret_block(lg, rq, rk, rv, cosf, sinf, dm):
    q = rq * cosf + _swap_halves(rq) * sinf
    k = (rk * cosf + _swap_halves(rk) * sinf) * RK_SCALE
    qb, kb, vb = q.astype(BF16), k.astype(BF16), rv.astype(BF16)
    sc = _dot_nt(qb, kb) * dm
    nloc = lax.broadcasted_iota(jnp.int32, (RET_T, 1), 0).astype(F32)
    qdec = jnp.exp(lg * (nloc + 1.0))
    kdec = jnp.exp(lg * (RET_T - 1.0 - nloc))
    block_dec = jnp.exp(jnp.full((1, LANES), lg * RET_T, F32))
    return q, k, qb, kb, vb, sc, qdec, kdec, block_dec


RET_RB = 2


def _ret_specs(S, rb):
    group = lambda c0: pl.BlockSpec((RET_RB * RET_T, GROUP_W), lambda s: (rb(s), c0))
    return group, pl.BlockSpec((RET_RB * RET_T, LANES), lambda s: (rb(s), 0))


def _ret_chains():
    chains = [(h, b) for b in range(RET_RB) for h in range(4)]
    rows = lambda c: (slice(c[1] * RET_T, (c[1] + 1) * RET_T), slice(c[0] * LANES, (c[0] + 1) * LANES))
    tab = lambda ref, c: ref[c[1] * RET_T:(c[1] + 1) * RET_T, :]
    return chains, rows, tab


def _ret_blocks(chains, rows, lg_ref, rq_ref, rk_ref, rv_ref, cosf, sinf, dm_ref):
    blk = {c: _ret_block(lg_ref[c[0]], rq_ref[rows(c)], rk_ref[rows(c)], rv_ref[rows(c)],
                         cosf[c], sinf[c], dm_ref[c[0]]) for c in chains}
    return ({c: blk[c][n] for c in chains} for n in range(9))


def ret_fwd(proj, cosf, sinf, lgam):
    S = proj.shape[0]
    nb = S // RET_T
    group, row_tab = _ret_specs(S, lambda s: s)

    def body(lg_ref, rq_ref, rk_ref, rv_ref, rg_ref, cos_ref, sin_ref, y_ref, o_ref, st_out, st_ref, dm_ref):
        @pl.when(pl.program_id(0) == 0)
        def _():
            st_ref[...] = jnp.zeros_like(st_ref)
            for h in range(4):
                dm_ref[h] = _ret_decay_mask(lg_ref[h])

        chains, rows, tab = _ret_chains()
        cosf, sinf = {c: tab(cos_ref, c) for c in chains}, {c: tab(sin_ref, c) for c in chains}
        q, k, qb, kb, vb, sc, qdec, kdec, block_dec = _ret_blocks(
            chains, rows, lg_ref, rq_ref, rk_ref, rv_ref, cosf, sinf, dm_ref)
        kv = {c: _dot_tn((k[c] * kdec[c]).astype(BF16), vb[c]) for c in chains}
        st = {(h, 0): st_ref[h] for h in range(4)}
        for b in range(RET_RB):
            for h in range(4):
                st[(h, b + 1)] = st[(h, b)] * block_dec[(h, b)] + kv[(h, b)]
        for h, b in chains:
            st_out[h, b] = st[(h, b)]
        for h in range(4):
            st_ref[h] = st[(h, RET_RB)]
        o = {c: _dot(sc[c].astype(BF16), vb[c]) + _dot(qb[c], st[c].astype(BF16)) * qdec[c] for c in chains}
        for c in chains:
            o_ref[rows(c)] = o[c]
        cen = {c: o[c] - _rowmean(o[c]) for c in chains}
        on = {c: cen[c] * lax.rsqrt(_rowmean(cen[c] * cen[c]) + EPS) for c in chains}
        rg = {c: rg_ref[rows(c)] for c in chains}
        for c in chains:
            y_ref[rows(c)] = (on[c] * (rg[c] * _sigmoid(rg[c]))).astype(BF16)

    return pl.pallas_call(
        body, name="ret_fwd", grid=(nb // RET_RB,),
        in_specs=[pl.BlockSpec(memory_space=pltpu.SMEM),
                  group(0), group(1), group(2), group(3), row_tab, row_tab],
        out_specs=[group(0), group(0),
                   pl.BlockSpec((4, RET_RB, LANES, LANES), lambda s: (0, s, 0, 0))],
        out_shape=[jax.ShapeDtypeStruct((S, GROUP_W), BF16),
                   jax.ShapeDtypeStruct((S, GROUP_W), F32),
                   jax.ShapeDtypeStruct((4, nb, LANES, LANES), F32)],
        scratch_shapes=[pltpu.VMEM((4, LANES, LANES), F32), pltpu.VMEM((4, RET_T, RET_T), F32)],
        compiler_params=_cp("arbitrary"),
    )(lgam, proj, proj, proj, proj, cosf, sinf)


def ret_bwd(proj, cosf, sinf, lgam, o, states, dycat):
    S = proj.shape[0]
    nsteps = S // RET_T // RET_RB
    rev = lambda s: nsteps - 1 - s
    group, row_tab = _ret_specs(S, rev)

    def body(lg_ref, rq_ref, rk_ref, rv_ref, rg_ref, cos_ref, sin_ref, o_ref, st_in, dy_ref,
             drq_ref, drk_ref, drv_ref, drg_ref, ds_ref, dm_ref):
        @pl.when(pl.program_id(0) == 0)
        def _():
            ds_ref[...] = jnp.zeros_like(ds_ref)
            for h in range(4):
                dm_ref[h] = _ret_decay_mask(lg_ref[h])

        chains, rows, tab = _ret_chains()
        cosf, sinf = {c: tab(cos_ref, c) for c in chains}, {c: tab(sin_ref, c) for c in chains}
        dms = {c: dm_ref[c[0]] for c in chains}
        q, k, qb, kb, vb, sc, qdec, kdec, block_dec = _ret_blocks(
            chains, rows, lg_ref, rq_ref, rk_ref, rv_ref, cosf, sinf, dm_ref)
        o_v = {c: o_ref[rows(c)] for c in chains}
        cen = {c: o_v[c] - _rowmean(o_v[c]) for c in chains}
        rstd = {c: lax.rsqrt(_rowmean(cen[c] * cen[c]) + EPS) for c in chains}
        on = {c: cen[c] * rstd[c] for c in chains}
        rg = {c: rg_ref[rows(c)] for c in chains}
        sig = {c: _sigmoid(rg[c]) for c in chains}
        dy = {c: dy_ref[rows(c)] for c in chains}
        for c in chains:
            drg_ref[rows(c)] = (dy[c] * on[c] * (sig[c] * (1.0 + rg[c] * (1.0 - sig[c])))).astype(BF16)
        don = {c: dy[c] * (rg[c] * sig[c]) for c in chains}
        do = {c: rstd[c] * (don[c] - _rowmean(don[c]) - on[c] * _rowmean(don[c] * on[c])) for c in chains}
        dob = {c: do[c].astype(BF16) for c in chains}
        dsc = {c: (_dot_nt(dob[c], vb[c]) * dms[c]).astype(BF16) for c in chains}
        st_b = {c: st_in[c[0], c[1]].astype(BF16) for c in chains}
        dst = {c: _dot_tn((q[c] * qdec[c]).astype(BF16), dob[c]) for c in chains}
        dsn = {(h, RET_RB): ds_ref[h] for h in range(4)}
        for b in reversed(range(RET_RB)):
            for h in range(4):
                dsn[(h, b)] = dsn[(h, b + 1)] * block_dec[(h, b)] + dst[(h, b)]
        for h in range(4):
            ds_ref[h] = dsn[(h, 0)]
        dsn_b = {c: dsn[(c[0], c[1] + 1)].astype(BF16) for c in chains}
        dq = {c: _dot(dsc[c], kb[c]) + _dot_nt(dob[c], st_b[c]) * qdec[c] for c in chains}
        dk = {c: (_dot_tn(dsc[c], qb[c]) + _dot_nt(vb[c], dsn_b[c]) * kdec[c]) * RK_SCALE for c in chains}
        dv = {c: _dot_tn(sc[c].astype(BF16), dob[c]) + _dot((k[c] * kdec[c]).astype(BF16), dsn_b[c])
              for c in chains}
        for c in chains:
            drq_ref[rows(c)] = (dq[c] * cosf[c] + _swap_halves(dq[c] * sinf[c])).astype(BF16)
            drk_ref[rows(c)] = (dk[c] * cosf[c] + _swap_halves(dk[c] * sinf[c])).astype(BF16)
            drv_ref[rows(c)] = dv[c].astype(BF16)

    return pl.pallas_call(
        body, name="ret_bwd", grid=(nsteps,),
        in_specs=[pl.BlockSpec(memory_space=pltpu.SMEM),
                  group(0), group(1), group(2), group(3), row_tab, row_tab,
                  group(0), pl.BlockSpec((4, RET_RB, LANES, LANES), lambda s: (0, rev(s), 0, 0)),
                  group(0)],
        out_specs=[group(0)] * 4,
        out_shape=[jax.ShapeDtypeStruct((S, GROUP_W), BF16)] * 4,
        scratch_shapes=[pltpu.VMEM((4, LANES, LANES), F32), pltpu.VMEM((4, RET_T, RET_T), F32)],
        compiler_params=_cp("arbitrary"),
    )(lgam, proj, proj, proj, proj, cosf, sinf, o, states, dycat)


def outproj_fwd(x, vecs, y_ret, y_sb, w_out, head=None, tm=1024):
    S, D = x.shape
    tm = min(tm, S)
    last = list(head or ())

    def body(x_ref, v_ref, yr_ref, ys_ref, w_ref, *refs):
        y = _dot(yr_ref[...], w_ref[0:GROUP_W, :]) + _dot(ys_ref[...], w_ref[GROUP_W:, :])
        xv = x_ref[...] + v_ref[2:3, :] * y
        if not last:
            y_ref, xo_ref = refs
            y_ref[...] = y.astype(BF16)
            xo_ref[...] = xv
            return
        g_ref, t_ref, y_ref, dx_ref, st_ref = refs
        y_ref[...] = y.astype(BF16)

        @pl.when(pl.program_id(0) == 0)
        def _():
            st_ref[...] = jnp.zeros_like(st_ref)

        g = g_ref[0:1, :]
        r = lax.rsqrt(_rowmean(xv * xv) + EPS)
        xn = xv * r
        err = xn * g - t_ref[...]
        dy = err * (1.0 / D)
        dxn = dy * g
        dx_ref[...] = r * (dxn - xn * _rowmean(dxn * xn))
        st_ref[0:1, :] += jnp.sum(dy * xn, axis=0, keepdims=True)
        st_ref[1:2, :] += jnp.sum(err * err, axis=0, keepdims=True)

    row = lambda w: pl.BlockSpec((tm, w), lambda i: (i, 0))
    fixed = pl.BlockSpec((8, D), lambda i: (0, 0))
    return pl.pallas_call(
        body, name="outproj_fwd", grid=(S // tm,),
        in_specs=[row(D), fixed, row(GROUP_W), row(GROUP_W), pl.BlockSpec((D, D), lambda i: (0, 0))]
        + ([fixed, row(D)] if last else []),
        out_specs=[row(D), row(D)] + ([fixed] if last else []),
        out_shape=[jax.ShapeDtypeStruct((S, D), BF16), jax.ShapeDtypeStruct((S, D), F32)]
        + ([jax.ShapeDtypeStruct((8, D), F32)] if last else []),
        compiler_params=_cp("arbitrary"),
    )(x, vecs, y_ret, y_sb, w_out, *last)


def outproj_bwd(dx, y, vecs, y_ret, y_sb, w_out, tm=1024):
    S, D = dx.shape
    tm = min(tm, S)
    n = S // tm

    def body(dx_ref, y_ref, v_ref, yr_ref, ys_ref, w_ref, dyc_ref, dw_ref, st_ref, acc):
        i = pl.program_id(0)

        @pl.when(i == 0)
        def _():
            st_ref[...] = jnp.zeros_like(st_ref)
            acc[...] = jnp.zeros_like(acc)

        dxv = dx_ref[...]
        st_ref[0:1, :] += jnp.sum(dxv * y_ref[...].astype(F32), axis=0, keepdims=True)
        dyy = (dxv * v_ref[2:3, :]).astype(BF16)
        dyc_ref[...] = _dot_nt(dyy, w_ref[...])
        acc[0:GROUP_W, :] += _dot_tn(yr_ref[...], dyy)
        acc[GROUP_W:, :] += _dot_tn(ys_ref[...], dyy)

        @pl.when(i == n - 1)
        def _():
            dw_ref[...] = acc[...].astype(BF16)

    row = lambda w: pl.BlockSpec((tm, w), lambda i: (i, 0))
    fixed = lambda r: pl.BlockSpec((r, D), lambda i: (0, 0))
    return pl.pallas_call(
        body, name="outproj_bwd", grid=(n,),
        in_specs=[row(D), row(D), fixed(8), row(GROUP_W), row(GROUP_W), fixed(D)],
        out_specs=[row(D), fixed(D), fixed(8)],
        out_shape=[jax.ShapeDtypeStruct((S, D), F32), jax.ShapeDtypeStruct((D, D), BF16),
                   jax.ShapeDtypeStruct((8, D), F32)],
        scratch_shapes=[pltpu.VMEM((D, D), F32)],
        compiler_params=_cp("arbitrary"),
    )(dx, y, vecs, y_ret, y_sb, w_out)


def inproj_bwd_x(pieces, w3, x, vecs, dx_res, ship=None, tm=512):
    S, D = x.shape
    n = S // tm
    ex = _Exchange(ship)

    def body(*refs):
        p_refs, (w_ref, x_ref, v_ref, dr_ref), refs = refs[:8], refs[8:12], refs[12:]
        ship_refs, (dx_ref, st_ref), refs = refs[:ex.n_in], refs[ex.n_in:ex.n_in + 2], refs[ex.n_in + 2:]
        start, finish = ex.ops(ship_refs, refs)

        @pl.when(pl.program_id(0) == 0)
        def _():
            st_ref[...] = jnp.zeros_like(st_ref)
            start()

        dh = jnp.zeros((tm, D), F32)
        for k, p_ref in enumerate(p_refs):
            c0 = (k % 2) * GROUP_W
            dh = dh + _dot_nt(p_ref[...], w_ref[k // 2, :, c0:c0 + GROUP_W])
        xv = x_ref[...]
        r = lax.rsqrt(_rowmean(xv * xv) + EPS)
        xn = xv * r
        g, scale1 = v_ref[3:4, :], 1.0 + v_ref[1:2, :]
        st_ref[0:1, :] += jnp.sum(dh, axis=0, keepdims=True)
        dh_xn = dh * xn
        st_ref[1:2, :] += jnp.sum(dh_xn, axis=0, keepdims=True) * g
        st_ref[2:3, :] += jnp.sum(dh_xn, axis=0, keepdims=True) * scale1
        dxn = dh * (g * scale1)
        dx_ref[...] = r * (dxn - xn * _rowmean(dxn * xn)) + dr_ref[...]
        pl.when(pl.program_id(0) == n - 1)(finish)

    row = lambda w: pl.BlockSpec((tm, w), lambda i: (i, 0))
    return pl.pallas_call(
        body, name="inproj_bwd_x", grid=(n,),
        in_specs=[row(GROUP_W)] * 8 + [pl.BlockSpec((N_SHARD, D, SHARD_W), lambda i: (0, 0, 0)),
                                       row(D), pl.BlockSpec((8, D), lambda i: (0, 0)), row(D)] + ex.in_specs,
        out_specs=[row(D), pl.BlockSpec((8, D), lambda i: (0, 0))] + ex.out_specs,
        out_shape=[jax.ShapeDtypeStruct((S, D), F32), jax.ShapeDtypeStruct((8, D), F32)] + ex.out_shape,
        scratch_shapes=ex.scratch,
        compiler_params=_cp("arbitrary"),
    )(*pieces, w3, x, vecs, dx_res, *ex.ship)


def inproj_bwd_w(h, pieces, tm=1024):
    S, D = h.shape
    tm = min(tm, S)
    n = S // tm

    def body(*refs):
        h_ref, p_refs, dw_ref, acc = refs[0], refs[1:9], refs[9], refs[10]
        i = pl.program_id(0)

        @pl.when(i == 0)
        def _():
            acc[...] = jnp.zeros_like(acc)

        hv = h_ref[...]
        for k, p_ref in enumerate(p_refs):
            c0 = (k % 2) * GROUP_W
            acc[k // 2, :, c0:c0 + GROUP_W] += _dot_tn(hv, p_ref[...])

        @pl.when(i == n - 1)
        def _():
            dw_ref[...] = acc[...].astype(BF16)

    row = lambda w: pl.BlockSpec((tm, w), lambda i: (i, 0))
    return pl.pallas_call(
        body, name="inproj_bwd_w", grid=(n,),
        in_specs=[row(D)] + [row(GROUP_W)] * 8,
        out_specs=pl.BlockSpec((N_SHARD, D, SHARD_W), lambda i: (0, 0, 0), pipeline_mode=pl.Buffered(1)),
        out_shape=jax.ShapeDtypeStruct((N_SHARD, D, SHARD_W), BF16),
        scratch_shapes=[pltpu.VMEM((N_SHARD, D, SHARD_W), F32)],
        compiler_params=_cp("arbitrary"),
    )(h, *pieces)


def layer_fwd(x, vecs, w3, w_out, tabs, gather=None, head=None):
    cosf, sinf, lgam = tabs
    ret, sg, h, sb = inproj_fwd(x, vecs, w3)
    y_ret, o_ret, states = ret_fwd(ret, cosf, sinf, lgam)
    y_sb, o_sb, sb_end, *gathered = sb_fwd(sb, sg, gather)
    if callable(w_out):
        w_out = w_out(gathered[0])
    y, *x_next = outproj_fwd(x, vecs, y_ret, y_sb, w_out, head)
    saved = (x, ret, sg, h, sb, y_ret, o_ret, states, y_sb, o_sb, sb_end, y)
    return (x_next[0] if head is None else x_next), saved, (gathered[0] if gathered else None)


def _by_shard(dw_out):
    return dw_out.reshape(N_SHARD, D_MODEL // N_SHARD, D_MODEL)


def layer_bwd(dx, saved, vecs, w3, w_out, tabs, later_grads=None):
    cosf, sinf, lgam = tabs
    x, ret, sg, h, sb, y_ret, o_ret, states, y_sb, o_sb, sb_end, y = saved
    dycat, dw_out, st_o = outproj_bwd(dx, y, vecs, y_ret, y_sb, w_out)
    dw_out = _by_shard(dw_out)
    ship = None if later_grads is None else (later_grads[0], dw_out, later_grads[1])
    *d_sb, = sb_bwd(sb, sg, o_sb, sb_end, dycat, ship)
    d_ret = ret_bwd(ret, cosf, sinf, lgam, o_ret, states, dycat)
    pieces = list(d_ret) + d_sb[:4]
    dw_in = inproj_bwd_w(h, pieces)
    dx, st_i, *recv_in = inproj_bwd_x(pieces, w3, x, vecs, dx, None if later_grads is None else (dw_in,))
    dmod = jnp.concatenate([st_i[0:2], st_o[0:1]], axis=0)
    grads = (dw_in, dw_out) if later_grads is None else (recv_in[0], d_sb[4])
    return dx, dmod, st_i[2:3], grads


def _place():
    return lax.axis_index("x"), lax.axis_index("y"), lax.axis_index("c")


def _other_chips(mx, my):
    return [(1 - mx, my), (mx, 1 - my), (1 - mx, 1 - my)]


_ANY = pl.BlockSpec(memory_space=pl.ANY)


_GATHER_SCRATCH = [pltpu.SemaphoreType.DMA((7,)), pltpu.SemaphoreType.DMA((7,)), pltpu.SemaphoreType.DMA(())]


def _gather_ops(x_ref, out_ref, send_sems, recv_sems, local_sem):
    mx, my, mc = _place()
    me, sibling = (mx, my, mc), (mx, my, 1 - mc)
    chips = _other_chips(mx, my)

    def slot(px, py, pc):
        return out_ref.at[4 * px + 2 * py + pc]

    def copy(k, block, to, src=None):
        return pltpu.make_async_remote_copy(
            src_ref=slot(*block) if src is None else src, dst_ref=slot(*block),
            send_sem=send_sems.at[k], recv_sem=recv_sems.at[k], device_id=to, device_id_type=MESH)

    mine = pltpu.make_async_copy(x_ref, slot(*me), local_sem)
    first = [copy(0, me, sibling, src=x_ref)]
    first += [copy(1 + j, me, (*chip, mc), src=x_ref) for j, chip in enumerate(chips)]
    passed = [copy(4 + j, (*chip, mc), sibling) for j, chip in enumerate(chips)]

    def start():
        mine.start()
        for cp in first:
            cp.start()

    def forward():
        for j, chip in enumerate(chips):
            copy(1 + j, (*chip, mc), me).wait_recv()
            passed[j].start()

    def finish():
        copy(0, sibling, me).wait_recv()
        for j, chip in enumerate(chips):
            copy(4 + j, (*chip, 1 - mc), me).wait_recv()
        for cp in first + passed:
            cp.wait_send()
        mine.wait()

    return start, forward, finish


def allgather8(x, name):
    def body(x_ref, out_ref, send_sems, recv_sems, local_sem):
        for step in _gather_ops(x_ref, out_ref, send_sems, recv_sems, local_sem):
            step()

    return pl.pallas_call(
        body, name=name, out_shape=jax.ShapeDtypeStruct((8,) + x.shape, x.dtype),
        in_specs=[_ANY], out_specs=_ANY, scratch_shapes=_GATHER_SCRATCH,
    )(x)


class _Exchange:
    def __init__(self, ship):
        self.ship = list(ship or ())
        self.n_in = len(self.ship)
        self.n_out = 1 if self.ship else 0
        self.rows = [a.shape[1] for a in self.ship]
        self.in_specs = [_ANY] * self.n_in
        self.out_specs = [_ANY] * self.n_out
        self.out_shape = [jax.ShapeDtypeStruct((N_SHARD, sum(self.rows), SHARD_W), BF16)] * self.n_out
        sem = pltpu.SemaphoreType.DMA
        self.scratch = [sem((3,)), sem((3,)), sem(())] * self.n_out

    def ops(self, ship_refs, tail):
        if not self.ship:
            return (lambda: None), (lambda: None)
        recv, send_sems, recv_sems, local_sem = tail
        mx, my, mc = _place()
        my_chip = 2 * mx + my
        chips = _other_chips(mx, my)

        def pieces(s):
            firsts = np.cumsum([0] + self.rows[:-1])
            return [(ref.at[s], int(r0), n) for ref, r0, n in zip(ship_refs, firsts, self.rows)]

        def start():
            for src, r0, n in pieces(my_chip):
                pltpu.make_async_copy(src, recv.at[my_chip, pl.ds(r0, n)], local_sem).start()
            for j, (px, py) in enumerate(chips):
                for src, r0, n in pieces(2 * px + py):
                    pltpu.make_async_remote_copy(
                        src_ref=src, dst_ref=recv.at[my_chip, pl.ds(r0, n)],
                        send_sem=send_sems.at[j], recv_sem=recv_sems.at[j],
                        device_id=(px, py, mc), device_id_type=MESH).start()

        def finish():
            for j, (px, py) in enumerate(chips):
                whole = recv.at[2 * px + py]
                both = pltpu.make_async_remote_copy(
                    src_ref=whole, dst_ref=whole, send_sem=send_sems.at[j], recv_sem=recv_sems.at[j],
                    device_id=(px, py, mc), device_id_type=MESH)
                both.wait_recv()
                both.wait_send()
            pltpu.make_async_copy(recv.at[my_chip], recv.at[my_chip], local_sem).wait()

        return start, finish


def sum_slots(recv_a, recv_b, tr=256):
    n, rows_a, cols = recv_a.shape
    na, nb = rows_a // tr, recv_b.shape[1] // tr

    def body(a_ref, b_ref, o_ref):
        def total(r_ref):
            acc = r_ref[0].astype(F32)
            for k in range(1, n):
                acc = acc + r_ref[k].astype(F32)
            o_ref[...] = acc

        pl.when(pl.program_id(0) < na)(lambda: total(a_ref))
        pl.when(pl.program_id(0) >= na)(lambda: total(b_ref))

    return pl.pallas_call(
        body, name="sum_slots", grid=(na + nb,),
        in_specs=[pl.BlockSpec((n, tr, cols), lambda i: (0, jnp.minimum(i, na - 1), 0)),
                  pl.BlockSpec((n, tr, cols), lambda i: (0, jnp.maximum(i - na, 0), 0))],
        out_specs=pl.BlockSpec((tr, cols), lambda i: (i, 0)),
        out_shape=jax.ShapeDtypeStruct(((na + nb) * tr, cols), F32),
        compiler_params=_cp("arbitrary"),
    )(recv_a, recv_b)


def swap_sibling(p):
    def body(p_ref, out_ref, send_sem, recv_sem):
        mx, my, mc = _place()
        cp = pltpu.make_async_remote_copy(
            src_ref=p_ref, dst_ref=out_ref, send_sem=send_sem, recv_sem=recv_sem,
            device_id=(mx, my, 1 - mc), device_id_type=MESH)
        cp.start()
        cp.wait()

    return pl.pallas_call(
        body, name="swap_sibling", out_shape=jax.ShapeDtypeStruct(p.shape, p.dtype),
        in_specs=[_ANY], out_specs=_ANY,
        scratch_shapes=[pltpu.SemaphoreType.DMA(()), pltpu.SemaphoreType.DMA(())],
    )(p)


def _adamw(w, g, m, v):
    m = ADAM_B1 * m + (1.0 - ADAM_B1) * g
    v = ADAM_B2 * v + (1.0 - ADAM_B2) * (g * g)
    m_hat = m / (1.0 - ADAM_B1 ** ADAM_STEP)
    v_hat = v / (1.0 - ADAM_B2 ** ADAM_STEP)
    delta = -ADAM_LR * (m_hat / (jnp.sqrt(v_hat) + ADAM_EPS) + ADAM_WD * w)
    return delta, m, v


def adam_slab(p_own, p_sib, w, m, v, row0, name, tr=256):
    L, R, C = w.shape
    nr = R // tr

    def body(a_ref, b_ref, w_ref, m_ref, v_ref, g_out, d_out, m_out, v_out):
        g = a_ref[...] + b_ref[...]
        d, m2, v2 = _adamw(w_ref[0], g, m_ref[0], v_ref[0])
        g_out[0], d_out[0], m_out[0], v_out[0] = g, d, m2, v2

    slab = pl.BlockSpec((tr, C), lambda l, i: (row0 // tr + l * nr + i, 0))
    blk = pl.BlockSpec((1, tr, C), lambda l, i: (l, i, 0))
    return pl.pallas_call(
        body, name=name, grid=(L, nr),
        in_specs=[slab, slab, blk, blk, blk], out_specs=[blk] * 4,
        out_shape=[jax.ShapeDtypeStruct(w.shape, F32)] * 4,
        compiler_params=_cp("arbitrary", "arbitrary"),
    )(p_own, p_sib, w, m, v)


def prologue(c8, w_ada, b_ada, norm_g, win_first):
    L, D, W = w_ada.shape

    def body(c_ref, w_ref, b_ref, g_ref, win_ref, vecs_ref, call_ref, wall_ref, mod_ref, mall_ref, *sems):
        w_start, w_forward, w_finish = _gather_ops(win_ref, wall_ref, *sems[0:3])
        w_start()
        for step in _gather_ops(c_ref, call_ref, *sems[3:6]):
            step()
        cv = call_ref[:, 0, :]
        ca = cv * _sigmoid(cv)
        for l in range(L):
            mod_ref[l * 8:(l + 1) * 8, :] = jnp.dot(ca, w_ref[l], precision=lax.Precision.HIGHEST,
                                                    preferred_element_type=F32)
        for step in _gather_ops(mod_ref, mall_ref, *sems[6:9]):
            step()
        mx, my, mc = _place()
        me = 4 * mx + 2 * my + mc
        rowid = lax.broadcasted_iota(jnp.int32, (L * 8, 1), 0)
        vecs_ref[...] = jnp.zeros_like(vecs_ref)
        for l in range(L):
            parts = [jnp.sum(jnp.where(rowid == l * 8 + me, mall_ref[2 * s + mc], 0.0), axis=0, keepdims=True)
                     for s in range(N_SHARD)]
            mod = jnp.concatenate(parts, axis=1) + b_ref[l:l + 1, :]
            for t in range(3):
                vecs_ref[l, t:t + 1, :] = mod[:, t * D:(t + 1) * D]
            vecs_ref[l, 3:4, :] = g_ref[l:l + 1, :]
        w_forward()
        w_finish()

    vmem = pl.BlockSpec(memory_space=pltpu.VMEM)
    return pl.pallas_call(
        body, name="prologue",
        in_specs=[vmem, vmem, vmem, vmem, _ANY], out_specs=[vmem, vmem, _ANY],
        out_shape=[jax.ShapeDtypeStruct((L, 8, D), F32), jax.ShapeDtypeStruct((8, 8, D), F32),
                   jax.ShapeDtypeStruct((8,) + win_first.shape, win_first.dtype)],
        scratch_shapes=[pltpu.VMEM((L * 8, W), F32), pltpu.VMEM((8, L * 8, W), F32)] + _GATHER_SCRATCH * 3,
        compiler_params=pltpu.CompilerParams(vmem_limit_bytes=VMEM_LIMIT_BYTES),
    )(c8, w_ada, b_ada, norm_g, win_first)


def ada_update(dmods, c_t, w, m, v, tr=256):
    L, D, W = w.shape

    def body(dm_ref, c_ref, w_ref, m_ref, v_ref, g_out, d_out, m_out, v_out):
        mx, my, _ = _place()
        shard = 2 * mx + my
        dm = jnp.zeros((8, W), F32)
        for s in range(N_SHARD):
            dm = dm + jnp.where(shard == s, dm_ref[0, :, s * W:(s + 1) * W], 0.0)
        cv = c_ref[...]
        ca = cv * _sigmoid(cv)
        g = jnp.zeros((tr, W), F32)
        for b in range(8):
            g = g + ca[:, b:b + 1] * dm[b:b + 1, :]
        d, m2, v2 = _adamw(w_ref[0], g, m_ref[0], v_ref[0])
        g_out[0], d_out[0], m_out[0], v_out[0] = g, d, m2, v2

    blk = pl.BlockSpec((1, tr, W), lambda l, i: (l, i, 0))
    return pl.pallas_call(
        body, name="ada_update", grid=(L, D // tr),
        in_specs=[pl.BlockSpec((1, 8, 3 * D), lambda l, i: (l, 0, 0)), pl.BlockSpec((tr, 8), lambda l, i: (i, 0)),
                  blk, blk, blk],
        out_specs=[blk] * 4, out_shape=[jax.ShapeDtypeStruct(w.shape, F32)] * 4,
        compiler_params=_cp("arbitrary", "arbitrary"),
    )(dmods, c_t, w, m, v)


STAT_ROWS = 16


def small_update(stats_all, norm, b_ada, final):
    def body(s_ref, *refs):
        ins, outs = refs[:9], refs[9:]
        tot = s_ref[0]
        for k in range(1, 8):
            tot = tot + s_ref[k]
        g_norm = tot[0:2, :]
        g_final = tot[2:3, :]
        g_b = jnp.concatenate(
            [jnp.concatenate([tot[3 + 3 * l + t:4 + 3 * l + t, :] for t in range(3)], axis=1) for l in range(DEPTH)],
            axis=0)
        for p, g in enumerate((g_norm, g_b, g_final)):
            w_ref, m_ref, v_ref = ins[3 * p:3 * p + 3]
            d, m2, v2 = _adamw(w_ref[...], g, m_ref[...], v_ref[...])
            for o_ref, val in zip(outs[4 * p:4 * p + 4], (g, d, m2, v2)):
                o_ref[...] = val
        loss = (0.5 / D_MODEL) * jnp.sum(tot[9:10, :], axis=1, keepdims=True)
        outs[12][...] = jnp.broadcast_to(loss, (8, LANES))

    shapes = []
    for w, _, _ in (norm, b_ada, final):
        shapes += [jax.ShapeDtypeStruct(w.shape, F32)] * 4
    shapes.append(jax.ShapeDtypeStruct((8, LANES), F32))
    return pl.pallas_call(body, name="small_update", out_shape=shapes)(stats_all, *norm, *b_ada, *final)


def kernel(x, c, norm_g, w_ada, b_ada, w_in, w_out, final_g, loss_target, m_norm_g, m_w_ada, m_b_ada, m_w_in, m_w_out, m_final_g, v_norm_g, v_w_ada, v_b_ada, v_w_in, v_w_out, v_final_g):
    S, D = x.shape[1], x.shape[2]
    mc = lax.axis_index("c")
    out_rows = D // N_SHARD

    def my_half(a, rows):
        return lax.dynamic_slice_in_dim(a, mc * rows, rows, axis=0)

    assert DEPTH == 2
    win = [my_half(w_in[l], D // 2).astype(BF16) for l in range(DEPTH)]
    wout = [my_half(w_out[l], out_rows // 2).astype(BF16) for l in range(DEPTH)]
    rest = jnp.concatenate([wout[0], win[1], wout[1]], axis=0)

    def unpack(wall):
        wall = wall.reshape(N_SHARD, 2, rest.shape[0], SHARD_W)
        a, b = out_rows // 2, out_rows // 2 + D // 2
        return wall[:, :, :a].reshape(D, D), (wall[:, :, a:b].reshape(N_SHARD, D, SHARD_W), wall[:, :, b:].reshape(D, D))

    vecs, c_all, w3_first = prologue(jnp.broadcast_to(c, (8, D)), w_ada, b_ada, norm_g, win[0])
    c_all, w3_first = c_all[:, 0, :], w3_first.reshape(N_SHARD, D, SHARD_W)

    tabs = (*rope_tables(S), ret_log_gamma())
    saved = [None] * DEPTH
    h, saved[0], wall = layer_fwd(x[0], vecs[0], w3_first, lambda g: unpack(g)[0], tabs, rest)
    weights = [(w3_first, unpack(wall)[0]), unpack(wall)[1]]
    head = (jnp.broadcast_to(final_g[None, :], (8, D)), loss_target[0])
    (dx, st_loss), saved[1], _ = layer_fwd(h, vecs[1], *weights[1], tabs, head=head)

    dmod, dnorm, grads = [None] * DEPTH, [None] * DEPTH, None
    for l in reversed(range(DEPTH)):
        dx, dmod[l], dnorm[l], grads = layer_bwd(dx, saved[l], vecs[l], *weights[l], tabs, grads)

    p_own = sum_slots(*grads)
    p_sib = swap_sibling(p_own)
    res_in = adam_slab(p_own, p_sib, w_in, m_w_in, v_w_in, 0, "adam_w_in")
    res_out = adam_slab(p_own, p_sib, w_out, m_w_out, v_w_out, DEPTH * D, "adam_w_out", tr=128)

    stats = jnp.concatenate(dnorm + [st_loss[0:1]] + dmod + [st_loss[1:2], jnp.zeros((STAT_ROWS - 10, D), F32)], axis=0)
    stats_all = allgather8(stats, "gather_stats")
    dmods = stats_all[:, 3:9, :].reshape(8, DEPTH, 3 * D).transpose(1, 0, 2)
    res_ada = ada_update(dmods, c_all.T, w_ada, m_w_ada, v_w_ada)
    small = small_update(stats_all, (norm_g, m_norm_g, v_norm_g), (b_ada, m_b_ada, v_b_ada),
                         (final_g[None, :], m_final_g[None, :], v_final_g[None, :]))
    res_norm, res_b, res_final = small[0:4], small[4:8], [a[0] for a in small[8:12]]
    loss = small[12][0, 0]

    by_kind = [res_norm, res_ada, res_b, res_in, res_out, res_final]
    outs = [loss, dx[None]]
    for kind in range(4):
        outs += [r[kind] for r in by_kind]
    return tuple(outs)
```

```python
import functools

import numpy as np
import jax
import jax.numpy as jnp
from jax import lax
from jax.experimental import pallas as pl
from jax.experimental.pallas import tpu as pltpu

F32, BF16 = jnp.float32, jnp.bfloat16
MESH = pl.DeviceIdType.MESH

D_MODEL = 1024
DEPTH = 2
SHARD_W = 1024
N_SHARD = 4
GROUP_W = 512
LANES = 128
SB_HEAD_DIM = 64
RET_HEAD_DIM = 128
CHUNK = 64
ROPE_BASE = 10000.0
EPS = 1e-6
SQ_SCALE = SB_HEAD_DIM ** -0.5
RK_SCALE = RET_HEAD_DIM ** -0.5
SB_T = 512
SB_CHAINS = 8
SB_NB = 4
RET_T = 256
EXP_ZERO = -104.0
VMEM_LIMIT_BYTES = 56 * 2 ** 20

ADAM_LR, ADAM_B1, ADAM_B2, ADAM_EPS, ADAM_WD, ADAM_STEP = 0.001, 0.9, 0.999, 1e-08, 0.01, 10


def _cp(*sem):
    return pltpu.CompilerParams(dimension_semantics=sem, vmem_limit_bytes=VMEM_LIMIT_BYTES)


def _dot(a, b):
    return lax.dot_general(a, b, (((1,), (0,)), ((), ())), preferred_element_type=F32)


def _dot_nt(a, b):
    return lax.dot_general(a, b, (((1,), (1,)), ((), ())), preferred_element_type=F32)


def _dot_tn(a, b):
    return lax.dot_general(a, b, (((0,), (0,)), ((), ())), preferred_element_type=F32)


def _running_sum(a, tri):
    return _dot(a.astype(BF16), tri)


def _sigmoid(x):
    return 1.0 / (1.0 + jnp.exp(-x))


def _rowsum(a):
    return jnp.sum(a, axis=1, keepdims=True)


def _rowmean(a):
    return jnp.mean(a, axis=1, keepdims=True)


def inproj_fwd(x, vecs, w3, tm=512):
    S, D = x.shape

    def body(x_ref, v_ref, w_ref, ret_ref, sg_ref, h_ref, sb_ref):
        xv = x_ref[...]
        r = lax.rsqrt(_rowmean(xv * xv) + EPS)
        h = xv * r * v_ref[3:4, :] * (1.0 + v_ref[1:2, :]) + v_ref[0:1, :]
        hb = h.astype(BF16)
        h_ref[...] = hb
        for s in range(N_SHARD):
            p = _dot(hb, w_ref[s])
            if s < 2:
                ret_ref[:, s * SHARD_W:(s + 1) * SHARD_W] = p
            if s == 2:
                sb_ref[:, 0:GROUP_W] = (p[:, 0:GROUP_W] * SQ_SCALE).astype(BF16)
                sb_ref[:, GROUP_W:SHARD_W] = p[:, GROUP_W:].astype(BF16)
            if s == 3:
                sb_ref[:, SHARD_W:SHARD_W + GROUP_W] = p[:, 0:GROUP_W].astype(BF16)
                sg_ref[...] = p[:, GROUP_W:]

    row = lambda w: pl.BlockSpec((tm, w), lambda i: (i, 0))
    return pl.pallas_call(
        body, name="inproj_fwd", grid=(S // tm,),
        in_specs=[row(D), pl.BlockSpec((8, D), lambda i: (0, 0)),
                  pl.BlockSpec((N_SHARD, D, SHARD_W), lambda i: (0, 0, 0))],
        out_specs=[row(2 * SHARD_W), row(GROUP_W), row(D), row(3 * GROUP_W)],
        out_shape=[jax.ShapeDtypeStruct((S, 2 * SHARD_W), F32), jax.ShapeDtypeStruct((S, GROUP_W), F32),
                   jax.ShapeDtypeStruct((S, D), BF16), jax.ShapeDtypeStruct((S, 3 * GROUP_W), BF16)],
        compiler_params=_cp("arbitrary"),
    )(x, vecs, w3)


def _sb_logits(qh, k2, keep):
    z = _dot_nt(qh, k2)
    sp = jnp.log(1.0 + jnp.exp(-jnp.abs(z)))
    lb = jnp.minimum(z, 0.0) - sp
    lk = lb - z
    if keep is not None:
        lk = jnp.where(keep, lk, 0.0)
    return lb, lk


class _sb_chains:
    def __init__(self, i, q2, do_b=None):
        t = self.t = SB_T // SB_CHAINS
        self.C = range(SB_CHAINS)
        r = lax.broadcasted_iota(jnp.int32, (SB_NB * t, SB_NB * t), 0)
        c = lax.broadcasted_iota(jnp.int32, (SB_NB * t, SB_NB * t), 1)
        self.later_all = jnp.where(r > c, 1.0, 0.0).astype(BF16)
        self.earlier_all = jnp.where(r < c, 1.0, 0.0).astype(BF16)
        self.later, self.earlier = self.later_all[:t, :t], self.earlier_all[:t, :t]
        self.head0 = lax.broadcasted_iota(jnp.int32, (1, LANES), 1) < SB_HEAD_DIM
        row = lax.broadcasted_iota(jnp.int32, (2 * t, SB_NB * t), 0) & (t - 1)
        col = lax.broadcasted_iota(jnp.int32, (2 * t, SB_NB * t), 1)
        qt = [SB_CHAINS * i + cc for cc in self.C]
        self.first = [jnp.maximum(qt[cc] - (SB_NB - 1), 0) for cc in self.C]
        self.keep = [self.first[cc] * t + col < qt[cc] * t + row for cc in self.C]
        self.qs = [self._stack(q2[cc * t:(cc + 1) * t]) for cc in self.C]
        if do_b is not None:
            self.dos = [self._stack(do_b[cc * t:(cc + 1) * t]) for cc in self.C]

    def _stack(self, a):
        zero = jnp.zeros_like(a)
        return jnp.concatenate([jnp.where(self.head0, a, zero), jnp.where(self.head0, zero, a)], axis=0)

    def rows(self, ref, j, n):
        return ref[pl.ds(pl.multiple_of(j * self.t, self.t), n * self.t), :]

    def suffix(self, lk):
        return _running_sum(lk, self.later_all), _rowsum(lk)

    def prefix(self, g, G0):
        return _running_sum(g, self.earlier_all) + G0


def sb_fwd(sb, sg, gather=None):
    S = sb.shape[0]
    T = SB_T
    nq = S // T
    carried = [] if gather is None else [gather]

    def body(*refs):
        (q_ref, k_ref, v_ref, sg_ref), refs = refs[:4], refs[4:]
        p, i = pl.program_id(0), pl.program_id(1)
        if carried:
            x_ref, y_ref, o_ref, end_ref, out_ref, send_sems, recv_sems, local_sem = refs
            start, forward, finish = _gather_ops(x_ref, out_ref, send_sems, recv_sems, local_sem)
            pl.when(jnp.logical_and(p == 0, i == 0))(start)
            pl.when(jnp.logical_and(p == 3, i == 0))(forward)
        else:
            y_ref, o_ref, end_ref = refs
        ch = _sb_chains(i, q_ref[...])
        later, head0 = ch.later, ch.head0
        lbk = [_sb_logits(ch.qs[c], ch.rows(k_ref, ch.first[c], SB_NB), ch.keep[c]) for c in ch.C]
        suffix, R = zip(*[ch.suffix(lbk[c][1]) for c in ch.C])
        aa = [jnp.where(ch.keep[c], jnp.exp(lbk[c][0] + suffix[c]), 0.0) for c in ch.C]
        acc = [_dot(aa[c].astype(BF16), ch.rows(v_ref, ch.first[c], SB_NB)) for c in ch.C]

        nc = len(ch.C)

        def alive(n, Rs):
            m = None
            for c in ch.C:
                rc = jnp.where(ch.first[c] - n > 0, Rs[c], EXP_ZERO)
                m = rc if m is None else jnp.maximum(m, rc)
            return jnp.max(m)

        def cond(st):
            return st[-1] > EXP_ZERO

        def step(st):
            n, accs, Rs = st[0], list(st[1:1 + nc]), list(st[1 + nc:1 + 2 * nc])
            for c in ch.C:
                j = ch.first[c] - 1 - n
                jc = jnp.maximum(j, 0)
                lb, lk = _sb_logits(ch.qs[c], ch.rows(k_ref, jc, 1), None)
                a = jnp.exp(lb + _running_sum(lk, later) + Rs[c])
                cx = _dot(a.astype(BF16), ch.rows(v_ref, jc, 1))
                accs[c] = jnp.where(j >= 0, accs[c] + cx, accs[c])
                Rs[c] = jnp.where(j >= 0, Rs[c] + _rowsum(lk), Rs[c])
            return (n + 1, *accs, *Rs, alive(n + 1, Rs))

        st = lax.while_loop(cond, step, (jnp.int32(0), *acc, *R, alive(0, R)))
        n_end, acc, R = st[0], st[1:1 + nc], st[1 + nc:1 + 2 * nc]
        outs = []
        for c in ch.C:
            base = c * (2 * ch.t + 8)
            end_ref[0, 0, base:base + 2 * ch.t, :] = jnp.broadcast_to(R[c], (2 * ch.t, 8))
            end_ref[0, 0, base + 2 * ch.t:base + 2 * ch.t + 8, :] = jnp.full((8, 8), n_end.astype(F32))
            outs.append(jnp.where(head0, acc[c][:ch.t], acc[c][ch.t:]))
        o = jnp.concatenate(outs, axis=0)
        o_ref[...] = o
        sg = sg_ref[...]
        y_ref[...] = (o * (sg * _sigmoid(sg))).astype(BF16)
        if carried:
            pl.when(jnp.logical_and(p == 3, i == nq - 1))(finish)

    return pl.pallas_call(
        body, name="sb_fwd", grid=(4, nq),
        in_specs=[pl.BlockSpec((T, LANES), lambda p, i: (i, p)),
                  pl.BlockSpec((S, LANES), lambda p, i: (0, 4 + p)),
                  pl.BlockSpec((S, LANES), lambda p, i: (0, 8 + p)),
                  pl.BlockSpec((T, LANES), lambda p, i: (i, p))] + [_ANY for _ in carried],
        out_specs=[pl.BlockSpec((T, LANES), lambda p, i: (i, p)),
                   pl.BlockSpec((T, LANES), lambda p, i: (i, p)),
                   pl.BlockSpec((1, 1, SB_CHAINS * (2 * T // SB_CHAINS + 8), 8), lambda p, i: (p, i, 0, 0))] + [_ANY for _ in carried],
        out_shape=[jax.ShapeDtypeStruct((S, GROUP_W), BF16),
                   jax.ShapeDtypeStruct((S, GROUP_W), F32),
                   jax.ShapeDtypeStruct((4, nq, SB_CHAINS * (2 * T // SB_CHAINS + 8), 8), F32)]
        + [jax.ShapeDtypeStruct((8,) + a.shape, a.dtype) for a in carried],
        scratch_shapes=_GATHER_SCRATCH if carried else [],
        compiler_params=_cp("arbitrary", "arbitrary"),
    )(sb, sb, sb, sg, *carried)


def sb_bwd(sb, sg, o, sb_end, dycat, ship=None):
    S = sb.shape[0]
    T = SB_T
    nq = S // T
    ex = _Exchange(ship)

    def body(*refs):
        (q_ref, k_ref, v_ref, sg_ref, o_ref, dy_ref, end_ref), refs = refs[:7], refs[7:]
        ship_refs, (dq_ref, dk_ref, dv_ref, dsg_ref), refs = refs[:ex.n_in], refs[ex.n_in:ex.n_in + 4], refs[ex.n_in + 4:]
        recv, (dk_acc, dv_acc), sems = refs[:ex.n_out], refs[ex.n_out:ex.n_out + 2], refs[ex.n_out + 2:]
        start, finish = ex.ops(ship_refs, recv + sems)
        p, i = pl.program_id(0), pl.program_id(1)
        pl.when(jnp.logical_and(p == 0, i == 0))(start)

        @pl.when(i == 0)
        def _():
            dk_acc[...] = jnp.zeros_like(dk_acc)
            dv_acc[...] = jnp.zeros_like(dv_acc)

        sg = sg_ref[...]
        sig = _sigmoid(sg)
        dy = dy_ref[...]
        dsg_ref[...] = (dy * o_ref[...] * (sig * (1.0 + sg * (1.0 - sig)))).astype(BF16)
        do_b = (dy * (sg * sig)).astype(BF16)
        ch = _sb_chains(i, q_ref[...], do_b)
        later, earlier, head0, t = ch.later, ch.earlier, ch.head0, ch.t
        end = end_ref[0, 0]

        def grads(c, j, n, a, lb, g, G, keep):
            dz = g - jnp.exp(lb) * (g + G)
            if keep is not None:
                dz = jnp.where(keep, dz, 0.0)
            dzb = dz.astype(BF16)
            rows = pl.ds(pl.multiple_of(j * t, t), n * t)
            dk_acc[rows, :] += _dot_tn(dzb, ch.qs[c])
            dv_acc[rows, :] += _dot_tn(a.astype(BF16), ch.dos[c])
            return _dot(dzb, ch.rows(k_ref, j, n))

        nc = len(ch.C)
        n_end = jnp.max(end[2 * t:2 * t + 8, :]).astype(jnp.int32)

        def sweep(m, st):
            dqs, G0s, lefts = list(st[:nc]), list(st[nc:2 * nc]), list(st[2 * nc:])
            for c in ch.C:
                j = ch.first[c] - n_end + m
                jc = jnp.maximum(j, 0)
                lb, lk = _sb_logits(ch.qs[c], ch.rows(k_ref, jc, 1), None)
                stick = lefts[c] - _rowsum(lk)
                a = jnp.where(j >= 0, jnp.exp(lb + _running_sum(lk, later) + stick), 0.0)
                g = a * _dot_nt(ch.dos[c], ch.rows(v_ref, jc, 1))
                G = _running_sum(g, earlier) + G0s[c]
                dqs[c] = dqs[c] + grads(c, jc, 1, a, lb, jnp.where(j >= 0, g, 0.0), jnp.where(j >= 0, G, 0.0), None)
                G0s[c] = G0s[c] + _rowsum(g)
                lefts[c] = jnp.where(j >= 0, stick, lefts[c])
            return (*dqs, *G0s, *lefts)

        lefts = [end[c * (2 * t + 8):c * (2 * t + 8) + 2 * t, 0:1] for c in ch.C]
        st = lax.fori_loop(0, n_end, sweep, (*[jnp.zeros((2 * t, LANES), F32)] * nc,
                                             *[jnp.zeros((2 * t, 1), F32)] * nc, *lefts))
        dq, G0 = st[:nc], st[nc:2 * nc]

        lbk = [_sb_logits(ch.qs[c], ch.rows(k_ref, ch.first[c], SB_NB), ch.keep[c]) for c in ch.C]
        suffix = [ch.suffix(lbk[c][1])[0] for c in ch.C]
        aa = [jnp.where(ch.keep[c], jnp.exp(lbk[c][0] + suffix[c]), 0.0) for c in ch.C]
        g = [aa[c] * _dot_nt(ch.dos[c], ch.rows(v_ref, ch.first[c], SB_NB)) for c in ch.C]
        G = [ch.prefix(g[c], G0[c]) for c in ch.C]
        for c in ch.C:
            dqc = dq[c] + grads(c, ch.first[c], SB_NB, aa[c], lbk[c][0], g[c], G[c], ch.keep[c])
            dq_ref[c * t:(c + 1) * t, :] = (jnp.where(head0, dqc[:t], dqc[t:]) * SQ_SCALE).astype(BF16)

        @pl.when(i == nq - 1)
        def _():
            dk_ref[...] = dk_acc[...].astype(BF16)
            dv_ref[...] = dv_acc[...].astype(BF16)

        pl.when(jnp.logical_and(p == 3, i == nq - 1))(finish)

    tile_spec = lambda c0: pl.BlockSpec((T, LANES), lambda p, i: (i, c0 + p))
    head_spec = lambda c0: pl.BlockSpec((S, LANES), lambda p, i: (0, c0 + p))
    return pl.pallas_call(
        body, name="sb_bwd", grid=(4, nq),
        in_specs=[tile_spec(0), head_spec(4), head_spec(8), tile_spec(0), tile_spec(0), tile_spec(4),
                  pl.BlockSpec((1, 1, SB_CHAINS * (2 * T // SB_CHAINS + 8), 8), lambda p, i: (p, i, 0, 0))] + ex.in_specs,
        out_specs=[tile_spec(0), head_spec(0), head_spec(0), tile_spec(0)] + ex.out_specs,
        out_shape=[jax.ShapeDtypeStruct((S, GROUP_W), BF16)] * 4 + ex.out_shape,
        scratch_shapes=[pltpu.VMEM((S, LANES), F32), pltpu.VMEM((S, LANES), F32)] + ex.scratch,
        compiler_params=_cp("arbitrary", "arbitrary"),
    )(sb, sb, sb, sg, o, dycat, sb_end, *ex.ship)


def rope_tables(S):
    half = RET_HEAD_DIM // 2
    inv = ROPE_BASE ** (-jnp.arange(half, dtype=F32) / half)
    ang = jnp.arange(S, dtype=F32)[:, None] * inv[None, :]
    cos, sin = jnp.cos(ang), jnp.sin(ang)
    return jnp.concatenate([cos, cos], axis=1), jnp.concatenate([-sin, sin], axis=1)


def ret_log_gamma():
    return jnp.log1p(-(2.0 ** (-5.0 - jnp.arange(4, dtype=F32))))


def _swap_halves(a):
    return pltpu.roll(a, RET_HEAD_DIM // 2, axis=1)


def _ret_decay_mask(lg):
    n = lax.broadcasted_iota(jnp.int32, (RET_T, RET_T), 0)
    m = lax.broadcasted_iota(jnp.int32, (RET_T, RET_T), 1)
    dist = jnp.abs(n - m).astype(F32)
    return jnp.where((m // CHUNK) <= (n // CHUNK), jnp.exp(lg * dist), 0.0)


def _ret_block(lg, rq, rk, rv, cosf, sinf, dm):
    q = rq * cosf + _swap_halves(rq) * sinf
    k = (rk * cosf + _swap_halves(rk) * sinf) * RK_SCALE
    qb, kb, vb = q.astype(BF16), k.astype(BF16), rv.astype(BF16)
    sc = _dot_nt(qb, kb) * dm
    nloc = lax.broadcasted_iota(jnp.int32, (RET_T, 1), 0).astype(F32)
    qdec = jnp.exp(lg * (nloc + 1.0))
    kdec = jnp.exp(lg * (RET_T - 1.0 - nloc))
    block_dec = jnp.exp(jnp.full((1, LANES), lg * RET_T, F32))
    return q, k, qb, kb, vb, sc, qdec, kdec, block_dec


RET_RB = 2


def _ret_specs(S, rb):
    group = lambda c0: pl.BlockSpec((RET_RB * RET_T, GROUP_W), lambda s: (rb(s), c0))
    return group, pl.BlockSpec((RET_RB * RET_T, LANES), lambda s: (rb(s), 0))


def _ret_chains():
    chains = [(h, b) for b in range(RET_RB) for h in range(4)]
    rows = lambda c: (slice(c[1] * RET_T, (c[1] + 1) * RET_T), slice(c[0] * LANES, (c[0] + 1) * LANES))
    tab = lambda ref, c: ref[c[1] * RET_T:(c[1] + 1) * RET_T, :]
    return chains, rows, tab


def _ret_blocks(chains, rows, lg_ref, rq_ref, rk_ref, rv_ref, cosf, sinf, dm_ref):
    blk = {c: _ret_block(lg_ref[c[0]], rq_ref[rows(c)], rk_ref[rows(c)], rv_ref[rows(c)],
                         cosf[c], sinf[c], dm_ref[c[0]]) for c in chains}
    return ({c: blk[c][n] for c in chains} for n in range(9))


def ret_fwd(proj, cosf, sinf, lgam):
    S = proj.shape[0]
    nb = S // RET_T
    group, row_tab = _ret_specs(S, lambda s: s)

    def body(lg_ref, rq_ref, rk_ref, rv_ref, rg_ref, cos_ref, sin_ref, y_ref, o_ref, st_out, st_ref, dm_ref):
        @pl.when(pl.program_id(0) == 0)
        def _():
            st_ref[...] = jnp.zeros_like(st_ref)
            for h in range(4):
                dm_ref[h] = _ret_decay_mask(lg_ref[h])

        chains, rows, tab = _ret_chains()
        cosf, sinf = {c: tab(cos_ref, c) for c in chains}, {c: tab(sin_ref, c) for c in chains}
        q, k, qb, kb, vb, sc, qdec, kdec, block_dec = _ret_blocks(
            chains, rows, lg_ref, rq_ref, rk_ref, rv_ref, cosf, sinf, dm_ref)
        kv = {c: _dot_tn((k[c] * kdec[c]).astype(BF16), vb[c]) for c in chains}
        st = {(h, 0): st_ref[h] for h in range(4)}
        for b in range(RET_RB):
            for h in range(4):
                st[(h, b + 1)] = st[(h, b)] * block_dec[(h, b)] + kv[(h, b)]
        for h, b in chains:
            st_out[h, b] = st[(h, b)]
        for h in range(4):
            st_ref[h] = st[(h, RET_RB)]
        o = {c: _dot(sc[c].astype(BF16), vb[c]) + _dot(qb[c], st[c].astype(BF16)) * qdec[c] for c in chains}
        for c in chains:
            o_ref[rows(c)] = o[c]
        cen = {c: o[c] - _rowmean(o[c]) for c in chains}
        on = {c: cen[c] * lax.rsqrt(_rowmean(cen[c] * cen[c]) + EPS) for c in chains}
        rg = {c: rg_ref[rows(c)] for c in chains}
        for c in chains:
            y_ref[rows(c)] = (on[c] * (rg[c] * _sigmoid(rg[c]))).astype(BF16)

    return pl.pallas_call(
        body, name="ret_fwd", grid=(nb // RET_RB,),
        in_specs=[pl.BlockSpec(memory_space=pltpu.SMEM),
                  group(0), group(1), group(2), group(3), row_tab, row_tab],
        out_specs=[group(0), group(0),
                   pl.BlockSpec((4, RET_RB, LANES, LANES), lambda s: (0, s, 0, 0))],
        out_shape=[jax.ShapeDtypeStruct((S, GROUP_W), BF16),
                   jax.ShapeDtypeStruct((S, GROUP_W), F32),
                   jax.ShapeDtypeStruct((4, nb, LANES, LANES), F32)],
        scratch_shapes=[pltpu.VMEM((4, LANES, LANES), F32), pltpu.VMEM((4, RET_T, RET_T), F32)],
        compiler_params=_cp("arbitrary"),
    )(lgam, proj, proj, proj, proj, cosf, sinf)


def ret_bwd(proj, cosf, sinf, lgam, o, states, dycat):
    S = proj.shape[0]
    nsteps = S // RET_T // RET_RB
    rev = lambda s: nsteps - 1 - s
    group, row_tab = _ret_specs(S, rev)

    def body(lg_ref, rq_ref, rk_ref, rv_ref, rg_ref, cos_ref, sin_ref, o_ref, st_in, dy_ref,
             drq_ref, drk_ref, drv_ref, drg_ref, ds_ref, dm_ref):
        @pl.when(pl.program_id(0) == 0)
        def _():
            ds_ref[...] = jnp.zeros_like(ds_ref)
            for h in range(4):
                dm_ref[h] = _ret_decay_mask(lg_ref[h])

        chains, rows, tab = _ret_chains()
        cosf, sinf = {c: tab(cos_ref, c) for c in chains}, {c: tab(sin_ref, c) for c in chains}
        dms = {c: dm_ref[c[0]] for c in chains}
        q, k, qb, kb, vb, sc, qdec, kdec, block_dec = _ret_blocks(
            chains, rows, lg_ref, rq_ref, rk_ref, rv_ref, cosf, sinf, dm_ref)
        o_v = {c: o_ref[rows(c)] for c in chains}
        cen = {c: o_v[c] - _rowmean(o_v[c]) for c in chains}
        rstd = {c: lax.rsqrt(_rowmean(cen[c] * cen[c]) + EPS) for c in chains}
        on = {c: cen[c] * rstd[c] for c in chains}
        rg = {c: rg_ref[rows(c)] for c in chains}
        sig = {c: _sigmoid(rg[c]) for c in chains}
        dy = {c: dy_ref[rows(c)] for c in chains}
        for c in chains:
            drg_ref[rows(c)] = (dy[c] * on[c] * (sig[c] * (1.0 + rg[c] * (1.0 - sig[c])))).astype(BF16)
        don = {c: dy[c] * (rg[c] * sig[c]) for c in chains}
        do = {c: rstd[c] * (don[c] - _rowmean(don[c]) - on[c] * _rowmean(don[c] * on[c])) for c in chains}
        dob = {c: do[c].astype(BF16) for c in chains}
        dsc = {c: (_dot_nt(dob[c], vb[c]) * dms[c]).astype(BF16) for c in chains}
        st_b = {c: st_in[c[0], c[1]].astype(BF16) for c in chains}
        dst = {c: _dot_tn((q[c] * qdec[c]).astype(BF16), dob[c]) for c in chains}
        dsn = {(h, RET_RB): ds_ref[h] for h in range(4)}
        for b in reversed(range(RET_RB)):
            for h in range(4):
                dsn[(h, b)] = dsn[(h, b + 1)] * block_dec[(h, b)] + dst[(h, b)]
        for h in range(4):
            ds_ref[h] = dsn[(h, 0)]
        dsn_b = {c: dsn[(c[0], c[1] + 1)].astype(BF16) for c in chains}
        dq = {c: _dot(dsc[c], kb[c]) + _dot_nt(dob[c], st_b[c]) * qdec[c] for c in chains}
        dk = {c: (_dot_tn(dsc[c], qb[c]) + _dot_nt(vb[c], dsn_b[c]) * kdec[c]) * RK_SCALE for c in chains}
        dv = {c: _dot_tn(sc[c].astype(BF16), dob[c]) + _dot((k[c] * kdec[c]).astype(BF16), dsn_b[c])
              for c in chains}
        for c in chains:
            drq_ref[rows(c)] = (dq[c] * cosf[c] + _swap_halves(dq[c] * sinf[c])).astype(BF16)
            drk_ref[rows(c)] = (dk[c] * cosf[c] + _swap_halves(dk[c] * sinf[c])).astype(BF16)
            drv_ref[rows(c)] = dv[c].astype(BF16)

    return pl.pallas_call(
        body, name="ret_bwd", grid=(nsteps,),
        in_specs=[pl.BlockSpec(memory_space=pltpu.SMEM),
                  group(0), group(1), group(2), group(3), row_tab, row_tab,
                  group(0), pl.BlockSpec((4, RET_RB, LANES, LANES), lambda s: (0, rev(s), 0, 0)),
                  group(0)],
        out_specs=[group(0)] * 4,
        out_shape=[jax.ShapeDtypeStruct((S, GROUP_W), BF16)] * 4,
        scratch_shapes=[pltpu.VMEM((4, LANES, LANES), F32), pltpu.VMEM((4, RET_T, RET_T), F32)],
        compiler_params=_cp("arbitrary"),
    )(lgam, proj, proj, proj, proj, cosf, sinf, o, states, dycat)


def outproj_fwd(x, vecs, y_ret, y_sb, w_out, head=None, tm=1024):
    S, D = x.shape
    tm = min(tm, S)
    last = list(head or ())

    def body(x_ref, v_ref, yr_ref, ys_ref, w_ref, *refs):
        y = _dot(yr_ref[...], w_ref[0:GROUP_W, :]) + _dot(ys_ref[...], w_ref[GROUP_W:, :])
        xv = x_ref[...] + v_ref[2:3, :] * y
        if not last:
            y_ref, xo_ref = refs
            y_ref[...] = y.astype(BF16)
            xo_ref[...] = xv
            return
        g_ref, t_ref, y_ref, dx_ref, st_ref = refs
        y_ref[...] = y.astype(BF16)

        @pl.when(pl.program_id(0) == 0)
        def _():
            st_ref[...] = jnp.zeros_like(st_ref)

        g = g_ref[0:1, :]
        r = lax.rsqrt(_rowmean(xv * xv) + EPS)
        xn = xv * r
        err = xn * g - t_ref[...]
        dy = err * (1.0 / D)
        dxn = dy * g
        dx_ref[...] = r * (dxn - xn * _rowmean(dxn * xn))
        st_ref[0:1, :] += jnp.sum(dy * xn, axis=0, keepdims=True)
        st_ref[1:2, :] += jnp.sum(err * err, axis=0, keepdims=True)

    row = lambda w: pl.BlockSpec((tm, w), lambda i: (i, 0))
    fixed = pl.BlockSpec((8, D), lambda i: (0, 0))
    return pl.pallas_call(
        body, name="outproj_fwd", grid=(S // tm,),
        in_specs=[row(D), fixed, row(GROUP_W), row(GROUP_W), pl.BlockSpec((D, D), lambda i: (0, 0))]
        + ([fixed, row(D)] if last else []),
        out_specs=[row(D), row(D)] + ([fixed] if last else []),
        out_shape=[jax.ShapeDtypeStruct((S, D), BF16), jax.ShapeDtypeStruct((S, D), F32)]
        + ([jax.ShapeDtypeStruct((8, D), F32)] if last else []),
        compiler_params=_cp("arbitrary"),
    )(x, vecs, y_ret, y_sb, w_out, *last)


def outproj_bwd(dx, y, vecs, y_ret, y_sb, w_out, tm=1024):
    S, D = dx.shape
    tm = min(tm, S)
    n = S // tm

    def body(dx_ref, y_ref, v_ref, yr_ref, ys_ref, w_ref, dyc_ref, dw_ref, st_ref, acc):
        i = pl.program_id(0)

        @pl.when(i == 0)
        def _():
            st_ref[...] = jnp.zeros_like(st_ref)
            acc[...] = jnp.zeros_like(acc)

        dxv = dx_ref[...]
        st_ref[0:1, :] += jnp.sum(dxv * y_ref[...].astype(F32), axis=0, keepdims=True)
        dyy = (dxv * v_ref[2:3, :]).astype(BF16)
        dyc_ref[...] = _dot_nt(dyy, w_ref[...])
        acc[0:GROUP_W, :] += _dot_tn(yr_ref[...], dyy)
        acc[GROUP_W:, :] += _dot_tn(ys_ref[...], dyy)

        @pl.when(i == n - 1)
        def _():
            dw_ref[...] = acc[...].astype(BF16)

    row = lambda w: pl.BlockSpec((tm, w), lambda i: (i, 0))
    fixed = lambda r: pl.BlockSpec((r, D), lambda i: (0, 0))
    return pl.pallas_call(
        body, name="outproj_bwd", grid=(n,),
        in_specs=[row(D), row(D), fixed(8), row(GROUP_W), row(GROUP_W), fixed(D)],
        out_specs=[row(D), fixed(D), fixed(8)],
        out_shape=[jax.ShapeDtypeStruct((S, D), F32), jax.ShapeDtypeStruct((D, D), BF16),
                   jax.ShapeDtypeStruct((8, D), F32)],
        scratch_shapes=[pltpu.VMEM((D, D), F32)],
        compiler_params=_cp("arbitrary"),
    )(dx, y, vecs, y_ret, y_sb, w_out)


def inproj_bwd_x(pieces, w3, x, vecs, dx_res, ship=None, tm=512):
    S, D = x.shape
    n = S // tm
    ex = _Exchange(ship)

    def body(*refs):
        p_refs, (w_ref, x_ref, v_ref, dr_ref), refs = refs[:8], refs[8:12], refs[12:]
        ship_refs, (dx_ref, st_ref), refs = refs[:ex.n_in], refs[ex.n_in:ex.n_in + 2], refs[ex.n_in + 2:]
        start, finish = ex.ops(ship_refs, refs)

        @pl.when(pl.program_id(0) == 0)
        def _():
            st_ref[...] = jnp.zeros_like(st_ref)
            start()

        dh = jnp.zeros((tm, D), F32)
        for k, p_ref in enumerate(p_refs):
            c0 = (k % 2) * GROUP_W
            dh = dh + _dot_nt(p_ref[...], w_ref[k // 2, :, c0:c0 + GROUP_W])
        xv = x_ref[...]
        r = lax.rsqrt(_rowmean(xv * xv) + EPS)
        xn = xv * r
        g, scale1 = v_ref[3:4, :], 1.0 + v_ref[1:2, :]
        st_ref[0:1, :] += jnp.sum(dh, axis=0, keepdims=True)
        dh_xn = dh * xn
        st_ref[1:2, :] += jnp.sum(dh_xn, axis=0, keepdims=True) * g
        st_ref[2:3, :] += jnp.sum(dh_xn, axis=0, keepdims=True) * scale1
        dxn = dh * (g * scale1)
        dx_ref[...] = r * (dxn - xn * _rowmean(dxn * xn)) + dr_ref[...]
        pl.when(pl.program_id(0) == n - 1)(finish)

    row = lambda w: pl.BlockSpec((tm, w), lambda i: (i, 0))
    return pl.pallas_call(
        body, name="inproj_bwd_x", grid=(n,),
        in_specs=[row(GROUP_W)] * 8 + [pl.BlockSpec((N_SHARD, D, SHARD_W), lambda i: (0, 0, 0)),
                                       row(D), pl.BlockSpec((8, D), lambda i: (0, 0)), row(D)] + ex.in_specs,
        out_specs=[row(D), pl.BlockSpec((8, D), lambda i: (0, 0))] + ex.out_specs,
        out_shape=[jax.ShapeDtypeStruct((S, D), F32), jax.ShapeDtypeStruct((8, D), F32)] + ex.out_shape,
        scratch_shapes=ex.scratch,
        compiler_params=_cp("arbitrary"),
    )(*pieces, w3, x, vecs, dx_res, *ex.ship)


def inproj_bwd_w(h, pieces, tm=1024):
    S, D = h.shape
    tm = min(tm, S)
    n = S // tm

    def body(*refs):
        h_ref, p_refs, dw_ref, acc = refs[0], refs[1:9], refs[9], refs[10]
        i = pl.program_id(0)

        @pl.when(i == 0)
        def _():
            acc[...] = jnp.zeros_like(acc)

        hv = h_ref[...]
        for k, p_ref in enumerate(p_refs):
            c0 = (k % 2) * GROUP_W
            acc[k // 2, :, c0:c0 + GROUP_W] += _dot_tn(hv, p_ref[...])

        @pl.when(i == n - 1)
        def _():
            dw_ref[...] = acc[...].astype(BF16)

    row = lambda w: pl.BlockSpec((tm, w), lambda i: (i, 0))
    return pl.pallas_call(
        body, name="inproj_bwd_w", grid=(n,),
        in_specs=[row(D)] + [row(GROUP_W)] * 8,
        out_specs=pl.BlockSpec((N_SHARD, D, SHARD_W), lambda i: (0, 0, 0), pipeline_mode=pl.Buffered(1)),
        out_shape=jax.ShapeDtypeStruct((N_SHARD, D, SHARD_W), BF16),
        scratch_shapes=[pltpu.VMEM((N_SHARD, D, SHARD_W), F32)],
        compiler_params=_cp("arbitrary"),
    )(h, *pieces)


def layer_fwd(x, vecs, w3, w_out, tabs, gather=None, head=None):
    cosf, sinf, lgam = tabs
    ret, sg, h, sb = inproj_fwd(x, vecs, w3)
    y_ret, o_ret, states = ret_fwd(ret, cosf, sinf, lgam)
    y_sb, o_sb, sb_end, *gathered = sb_fwd(sb, sg, gather)
    if callable(w_out):
        w_out = w_out(gathered[0])
    y, *x_next = outproj_fwd(x, vecs, y_ret, y_sb, w_out, head)
    saved = (x, ret, sg, h, sb, y_ret, o_ret, states, y_sb, o_sb, sb_end, y)
    return (x_next[0] if head is None else x_next), saved, (gathered[0] if gathered else None)


def _by_shard(dw_out):
    return dw_out.reshape(N_SHARD, D_MODEL // N_SHARD, D_MODEL)


def layer_bwd(dx, saved, vecs, w3, w_out, tabs, later_grads=None):
    cosf, sinf, lgam = tabs
    x, ret, sg, h, sb, y_ret, o_ret, states, y_sb, o_sb, sb_end, y = saved
    dycat, dw_out, st_o = outproj_bwd(dx, y, vecs, y_ret, y_sb, w_out)
    dw_out = _by_shard(dw_out)
    ship = None if later_grads is None else (later_grads[0], dw_out, later_grads[1])
    *d_sb, = sb_bwd(sb, sg, o_sb, sb_end, dycat, ship)
    d_ret = ret_bwd(ret, cosf, sinf, lgam, o_ret, states, dycat)
    pieces = list(d_ret) + d_sb[:4]
    dw_in = inproj_bwd_w(h, pieces)
    dx, st_i, *recv_in = inproj_bwd_x(pieces, w3, x, vecs, dx, None if later_grads is None else (dw_in,))
    dmod = jnp.concatenate([st_i[0:2], st_o[0:1]], axis=0)
    grads = (dw_in, dw_out) if later_grads is None else (recv_in[0], d_sb[4])
    return dx, dmod, st_i[2:3], grads


def _place():
    return lax.axis_index("x"), lax.axis_index("y"), lax.axis_index("c")


def _other_chips(mx, my):
    return [(1 - mx, my), (mx, 1 - my), (1 - mx, 1 - my)]


_ANY = pl.BlockSpec(memory_space=pl.ANY)


_GATHER_SCRATCH = [pltpu.SemaphoreType.DMA((7,)), pltpu.SemaphoreType.DMA((7,)), pltpu.SemaphoreType.DMA(())]


def _gather_ops(x_ref, out_ref, send_sems, recv_sems, local_sem):
    mx, my, mc = _place()
    me, sibling = (mx, my, mc), (mx, my, 1 - mc)
    chips = _other_chips(mx, my)

    def slot(px, py, pc):
        return out_ref.at[4 * px + 2 * py + pc]

    def copy(k, block, to, src=None):
        return pltpu.make_async_remote_copy(
            src_ref=slot(*block) if src is None else src, dst_ref=slot(*block),
            send_sem=send_sems.at[k], recv_sem=recv_sems.at[k], device_id=to, device_id_type=MESH)

    mine = pltpu.make_async_copy(x_ref, slot(*me), local_sem)
    first = [copy(0, me, sibling, src=x_ref)]
    first += [copy(1 + j, me, (*chip, mc), src=x_ref) for j, chip in enumerate(chips)]
    passed = [copy(4 + j, (*chip, mc), sibling) for j, chip in enumerate(chips)]

    def start():
        mine.start()
        for cp in first:
            cp.start()

    def forward():
        for j, chip in enumerate(chips):
            copy(1 + j, (*chip, mc), me).wait_recv()
            passed[j].start()

    def finish():
        copy(0, sibling, me).wait_recv()
        for j, chip in enumerate(chips):
            copy(4 + j, (*chip, 1 - mc), me).wait_recv()
        for cp in first + passed:
            cp.wait_send()
        mine.wait()

    return start, forward, finish


def allgather8(x, name):
    def body(x_ref, out_ref, send_sems, recv_sems, local_sem):
        for step in _gather_ops(x_ref, out_ref, send_sems, recv_sems, local_sem):
            step()

    return pl.pallas_call(
        body, name=name, out_shape=jax.ShapeDtypeStruct((8,) + x.shape, x.dtype),
        in_specs=[_ANY], out_specs=_ANY, scratch_shapes=_GATHER_SCRATCH,
    )(x)


class _Exchange:
    def __init__(self, ship):
        self.ship = list(ship or ())
        self.n_in = len(self.ship)
        self.n_out = 1 if self.ship else 0
        self.rows = [a.shape[1] for a in self.ship]
        self.in_specs = [_ANY] * self.n_in
        self.out_specs = [_ANY] * self.n_out
        self.out_shape = [jax.ShapeDtypeStruct((N_SHARD, sum(self.rows), SHARD_W), BF16)] * self.n_out
        sem = pltpu.SemaphoreType.DMA
        self.scratch = [sem((3,)), sem((3,)), sem(())] * self.n_out

    def ops(self, ship_refs, tail):
        if not self.ship:
            return (lambda: None), (lambda: None)
        recv, send_sems, recv_sems, local_sem = tail
        mx, my, mc = _place()
        my_chip = 2 * mx + my
        chips = _other_chips(mx, my)

        def pieces(s):
            firsts = np.cumsum([0] + self.rows[:-1])
            return [(ref.at[s], int(r0), n) for ref, r0, n in zip(ship_refs, firsts, self.rows)]

        def start():
            for src, r0, n in pieces(my_chip):
                pltpu.make_async_copy(src, recv.at[my_chip, pl.ds(r0, n)], local_sem).start()
            for j, (px, py) in enumerate(chips):
                for src, r0, n in pieces(2 * px + py):
                    pltpu.make_async_remote_copy(
                        src_ref=src, dst_ref=recv.at[my_chip, pl.ds(r0, n)],
                        send_sem=send_sems.at[j], recv_sem=recv_sems.at[j],
                        device_id=(px, py, mc), device_id_type=MESH).start()

        def finish():
            for j, (px, py) in enumerate(chips):
                whole = recv.at[2 * px + py]
                both = pltpu.make_async_remote_copy(
                    src_ref=whole, dst_ref=whole, send_sem=send_sems.at[j], recv_sem=recv_sems.at[j],
                    device_id=(px, py, mc), device_id_type=MESH)
                both.wait_recv()
                both.wait_send()
            pltpu.make_async_copy(recv.at[my_chip], recv.at[my_chip], local_sem).wait()

        return start, finish


def sum_slots(recv_a, recv_b, tr=256):
    n, rows_a, cols = recv_a.shape
    na, nb = rows_a // tr, recv_b.shape[1] // tr

    def body(a_ref, b_ref, o_ref):
        def total(r_ref):
            acc = r_ref[0].astype(F32)
            for k in range(1, n):
                acc = acc + r_ref[k].astype(F32)
            o_ref[...] = acc

        pl.when(pl.program_id(0) < na)(lambda: total(a_ref))
        pl.when(pl.program_id(0) >= na)(lambda: total(b_ref))

    return pl.pallas_call(
        body, name="sum_slots", grid=(na + nb,),
        in_specs=[pl.BlockSpec((n, tr, cols), lambda i: (0, jnp.minimum(i, na - 1), 0)),
                  pl.BlockSpec((n, tr, cols), lambda i: (0, jnp.maximum(i - na, 0), 0))],
        out_specs=pl.BlockSpec((tr, cols), lambda i: (i, 0)),
        out_shape=jax.ShapeDtypeStruct(((na + nb) * tr, cols), F32),
        compiler_params=_cp("arbitrary"),
    )(recv_a, recv_b)


def swap_sibling(p):
    def body(p_ref, out_ref, send_sem, recv_sem):
        mx, my, mc = _place()
        cp = pltpu.make_async_remote_copy(
            src_ref=p_ref, dst_ref=out_ref, send_sem=send_sem, recv_sem=recv_sem,
            device_id=(mx, my, 1 - mc), device_id_type=MESH)
        cp.start()
        cp.wait()

    return pl.pallas_call(
        body, name="swap_sibling", out_shape=jax.ShapeDtypeStruct(p.shape, p.dtype),
        in_specs=[_ANY], out_specs=_ANY,
        scratch_shapes=[pltpu.SemaphoreType.DMA(()), pltpu.SemaphoreType.DMA(())],
    )(p)


def _adamw(w, g, m, v):
    m = ADAM_B1 * m + (1.0 - ADAM_B1) * g
    v = ADAM_B2 * v + (1.0 - ADAM_B2) * (g * g)
    m_hat = m / (1.0 - ADAM_B1 ** ADAM_STEP)
    v_hat = v / (1.0 - ADAM_B2 ** ADAM_STEP)
    delta = -ADAM_LR * (m_hat / (jnp.sqrt(v_hat) + ADAM_EPS) + ADAM_WD * w)
    return delta, m, v


def adam_slab(p_own, p_sib, w, m, v, row0, name, tr=256):
    L, R, C = w.shape
    nr = R // tr

    def body(a_ref, b_ref, w_ref, m_ref, v_ref, g_out, d_out, m_out, v_out):
        g = a_ref[...] + b_ref[...]
        d, m2, v2 = _adamw(w_ref[0], g, m_ref[0], v_ref[0])
        g_out[0], d_out[0], m_out[0], v_out[0] = g, d, m2, v2

    slab = pl.BlockSpec((tr, C), lambda l, i: (row0 // tr + l * nr + i, 0))
    blk = pl.BlockSpec((1, tr, C), lambda l, i: (l, i, 0))
    return pl.pallas_call(
        body, name=name, grid=(L, nr),
        in_specs=[slab, slab, blk, blk, blk], out_specs=[blk] * 4,
        out_shape=[jax.ShapeDtypeStruct(w.shape, F32)] * 4,
        compiler_params=_cp("arbitrary", "arbitrary"),
    )(p_own, p_sib, w, m, v)


def prologue(c8, w_ada, b_ada, norm_g, win_first):
    L, D, W = w_ada.shape

    def body(c_ref, w_ref, b_ref, g_ref, win_ref, vecs_ref, call_ref, wall_ref, mod_ref, mall_ref, *sems):
        w_start, w_forward, w_finish = _gather_ops(win_ref, wall_ref, *sems[0:3])
        w_start()
        for step in _gather_ops(c_ref, call_ref, *sems[3:6]):
            step()
        cv = call_ref[:, 0, :]
        ca = cv * _sigmoid(cv)
        for l in range(L):
            mod_ref[l * 8:(l + 1) * 8, :] = jnp.dot(ca, w_ref[l], precision=lax.Precision.HIGHEST,
                                                    preferred_element_type=F32)
        for step in _gather_ops(mod_ref, mall_ref, *sems[6:9]):
            step()
        mx, my, mc = _place()
        me = 4 * mx + 2 * my + mc
        rowid = lax.broadcasted_iota(jnp.int32, (L * 8, 1), 0)
        vecs_ref[...] = jnp.zeros_like(vecs_ref)
        for l in range(L):
            parts = [jnp.sum(jnp.where(rowid == l * 8 + me, mall_ref[2 * s + mc], 0.0), axis=0, keepdims=True)
                     for s in range(N_SHARD)]
            mod = jnp.concatenate(parts, axis=1) + b_ref[l:l + 1, :]
            for t in range(3):
                vecs_ref[l, t:t + 1, :] = mod[:, t * D:(t + 1) * D]
            vecs_ref[l, 3:4, :] = g_ref[l:l + 1, :]
        w_forward()
        w_finish()

    vmem = pl.BlockSpec(memory_space=pltpu.VMEM)
    return pl.pallas_call(
        body, name="prologue",
        in_specs=[vmem, vmem, vmem, vmem, _ANY], out_specs=[vmem, vmem, _ANY],
        out_shape=[jax.ShapeDtypeStruct((L, 8, D), F32), jax.ShapeDtypeStruct((8, 8, D), F32),
                   jax.ShapeDtypeStruct((8,) + win_first.shape, win_first.dtype)],
        scratch_shapes=[pltpu.VMEM((L * 8, W), F32), pltpu.VMEM((8, L * 8, W), F32)] + _GATHER_SCRATCH * 3,
        compiler_params=pltpu.CompilerParams(vmem_limit_bytes=VMEM_LIMIT_BYTES),
    )(c8, w_ada, b_ada, norm_g, win_first)


def ada_update(dmods, c_t, w, m, v, tr=256):
    L, D, W = w.shape

    def body(dm_ref, c_ref, w_ref, m_ref, v_ref, g_out, d_out, m_out, v_out):
        mx, my, _ = _place()
        shard = 2 * mx + my
        dm = jnp.zeros((8, W), F32)
        for s in range(N_SHARD):
            dm = dm + jnp.where(shard == s, dm_ref[0, :, s * W:(s + 1) * W], 0.0)
        cv = c_ref[...]
        ca = cv * _sigmoid(cv)
        g = jnp.zeros((tr, W), F32)
        for b in range(8):
            g = g + ca[:, b:b + 1] * dm[b:b + 1, :]
        d, m2, v2 = _adamw(w_ref[0], g, m_ref[0], v_ref[0])
        g_out[0], d_out[0], m_out[0], v_out[0] = g, d, m2, v2

    blk = pl.BlockSpec((1, tr, W), lambda l, i: (l, i, 0))
    return pl.pallas_call(
        body, name="ada_update", grid=(L, D // tr),
        in_specs=[pl.BlockSpec((1, 8, 3 * D), lambda l, i: (l, 0, 0)), pl.BlockSpec((tr, 8), lambda l, i: (i, 0)),
                  blk, blk, blk],
        out_specs=[blk] * 4, out_shape=[jax.ShapeDtypeStruct(w.shape, F32)] * 4,
        compiler_params=_cp("arbitrary", "arbitrary"),
    )(dmods, c_t, w, m, v)


STAT_ROWS = 16


def small_update(stats_all, norm, b_ada, final):
    def body(s_ref, *refs):
        ins, outs = refs[:9], refs[9:]
        tot = s_ref[0]
        for k in range(1, 8):
            tot = tot + s_ref[k]
        g_norm = tot[0:2, :]
        g_final = tot[2:3, :]
        g_b = jnp.concatenate(
            [jnp.concatenate([tot[3 + 3 * l + t:4 + 3 * l + t, :] for t in range(3)], axis=1) for l in range(DEPTH)],
            axis=0)
        for p, g in enumerate((g_norm, g_b, g_final)):
            w_ref, m_ref, v_ref = ins[3 * p:3 * p + 3]
            d, m2, v2 = _adamw(w_ref[...], g, m_ref[...], v_ref[...])
            for o_ref, val in zip(outs[4 * p:4 * p + 4], (g, d, m2, v2)):
                o_ref[...] = val
        loss = (0.5 / D_MODEL) * jnp.sum(tot[9:10, :], axis=1, keepdims=True)
        outs[12][...] = jnp.broadcast_to(loss, (8, LANES))

    shapes = []
    for w, _, _ in (norm, b_ada, final):
        shapes += [jax.ShapeDtypeStruct(w.shape, F32)] * 4
    shapes.append(jax.ShapeDtypeStruct((8, LANES), F32))
    return pl.pallas_call(body, name="small_update", out_shape=shapes)(stats_all, *norm, *b_ada, *final)


def kernel(x, c, norm_g, w_ada, b_ada, w_in, w_out, final_g, loss_target, m_norm_g, m_w_ada, m_b_ada, m_w_in, m_w_out, m_final_g, v_norm_g, v_w_ada, v_b_ada, v_w_in, v_w_out, v_final_g):
    S, D = x.shape[1], x.shape[2]
    mc = lax.axis_index("c")
    out_rows = D // N_SHARD

    def my_half(a, rows):
        return lax.dynamic_slice_in_dim(a, mc * rows, rows, axis=0)

    assert DEPTH == 2
    win = [my_half(w_in[l], D // 2).astype(BF16) for l in range(DEPTH)]
    wout = [my_half(w_out[l], out_rows // 2).astype(BF16) for l in range(DEPTH)]
    rest = jnp.concatenate([wout[0], win[1], wout[1]], axis=0)

    def unpack(wall):
        wall = wall.reshape(N_SHARD, 2, rest.shape[0], SHARD_W)
        a, b = out_rows // 2, out_rows // 2 + D // 2
        return wall[:, :, :a].reshape(D, D), (wall[:, :, a:b].reshape(N_SHARD, D, SHARD_W), wall[:, :, b:].reshape(D, D))

    vecs, c_all, w3_first = prologue(jnp.broadcast_to(c, (8, D)), w_ada, b_ada, norm_g, win[0])
    c_all, w3_first = c_all[:, 0, :], w3_first.reshape(N_SHARD, D, SHARD_W)

    tabs = (*rope_tables(S), ret_log_gamma())
    saved = [None] * DEPTH
    h, saved[0], wall = layer_fwd(x[0], vecs[0], w3_first, lambda g: unpack(g)[0], tabs, rest)
    weights = [(w3_first, unpack(wall)[0]), unpack(wall)[1]]
    head = (jnp.broadcast_to(final_g[None, :], (8, D)), loss_target[0])
    (dx, st_loss), saved[1], _ = layer_fwd(h, vecs[1], *weights[1], tabs, head=head)

    dmod, dnorm, grads = [None] * DEPTH, [None] * DEPTH, None
    for l in reversed(range(DEPTH)):
        dx, dmod[l], dnorm[l], grads = layer_bwd(dx, saved[l], vecs[l], *weights[l], tabs, grads)

    p_own = sum_slots(*grads)
    p_sib = swap_sibling(p_own)
    res_in = adam_slab(p_own, p_sib, w_in, m_w_in, v_w_in, 0, "adam_w_in")
    res_out = adam_slab(p_own, p_sib, w_out, m_w_out, v_w_out, DEPTH * D, "adam_w_out", tr=128)

    stats = jnp.concatenate(dnorm + [st_loss[0:1]] + dmod + [st_loss[1:2], jnp.zeros((STAT_ROWS - 10, D), F32)], axis=0)
    stats_all = allgather8(stats, "gather_stats")
    dmods = stats_all[:, 3:9, :].reshape(8, DEPTH, 3 * D).transpose(1, 0, 2)
    res_ada = ada_update(dmods, c_all.T, w_ada, m_w_ada, v_w_ada)
    small = small_update(stats_all, (norm_g, m_norm_g, v_norm_g), (b_ada, m_b_ada, v_b_ada),
                         (final_g[None, :], m_final_g[None, :], v_final_g[None, :]))
    res_norm, res_b, res_final = small[0:4], small[4:8], [a[0] for a in small[8:12]]
    loss = small[12][0, 0]

    by_kind = [res_norm, res_ada, res_b, res_in, res_out, res_final]
    outs = [loss, dx[None]]
    for kind in range(4):
        outs += [r[kind] for r in by_kind]
    return tuple(outs)
```

```python
import functools

import numpy as np
import jax
import jax.numpy as jnp
from jax import lax
from jax.experimental import pallas as pl
from jax.experimental.pallas import tpu as pltpu

F32, BF16 = jnp.float32, jnp.bfloat16
MESH = pl.DeviceIdType.MESH

D_MODEL = 1024
DEPTH = 2
SHARD_W = 1024
N_SHARD = 4
GROUP_W = 512
LANES = 128
SB_HEAD_DIM = 64
RET_HEAD_DIM = 128
CHUNK = 64
ROPE_BASE = 10000.0
EPS = 1e-6
SQ_SCALE = SB_HEAD_DIM ** -0.5
RK_SCALE = RET_HEAD_DIM ** -0.5
SB_T = 1024
SB_CHAINS = 16
SB_NB = 4
RET_T = 256
EXP_ZERO = -104.0
VMEM_LIMIT_BYTES = 56 * 2 ** 20

ADAM_LR, ADAM_B1, ADAM_B2, ADAM_EPS, ADAM_WD, ADAM_STEP = 0.001, 0.9, 0.999, 1e-08, 0.01, 10


def _cp(*sem):
    return pltpu.CompilerParams(dimension_semantics=sem, vmem_limit_bytes=VMEM_LIMIT_BYTES)


def _dot(a, b):
    return lax.dot_general(a, b, (((1,), (0,)), ((), ())), preferred_element_type=F32)


def _dot_nt(a, b):
    return lax.dot_general(a, b, (((1,), (1,)), ((), ())), preferred_element_type=F32)


def _dot_tn(a, b):
    return lax.dot_general(a, b, (((0,), (0,)), ((), ())), preferred_element_type=F32)


def _running_sum(a, tri):
    return _dot(a.astype(BF16), tri)


def _sigmoid(x):
    return 1.0 / (1.0 + jnp.exp(-x))


def _rowsum(a):
    return jnp.sum(a, axis=1, keepdims=True)


def _rowmean(a):
    return jnp.mean(a, axis=1, keepdims=True)


def inproj_fwd(x, vecs, w3, tm=512):
    S, D = x.shape

    def body(x_ref, v_ref, w_ref, ret_ref, sg_ref, h_ref, sb_ref):
        xv = x_ref[...]
        r = lax.rsqrt(_rowmean(xv * xv) + EPS)
        h = xv * r * v_ref[3:4, :] * (1.0 + v_ref[1:2, :]) + v_ref[0:1, :]
        hb = h.astype(BF16)
        h_ref[...] = hb
        for s in range(N_SHARD):
            p = _dot(hb, w_ref[s])
            if s < 2:
                ret_ref[:, s * SHARD_W:(s + 1) * SHARD_W] = p
            if s == 2:
                sb_ref[:, 0:GROUP_W] = (p[:, 0:GROUP_W] * SQ_SCALE).astype(BF16)
                sb_ref[:, GROUP_W:SHARD_W] = p[:, GROUP_W:].astype(BF16)
            if s == 3:
                sb_ref[:, SHARD_W:SHARD_W + GROUP_W] = p[:, 0:GROUP_W].astype(BF16)
                sg_ref[...] = p[:, GROUP_W:]

    row = lambda w: pl.BlockSpec((tm, w), lambda i: (i, 0))
    return pl.pallas_call(
        body, name="inproj_fwd", grid=(S // tm,),
        in_specs=[row(D), pl.BlockSpec((8, D), lambda i: (0, 0)),
                  pl.BlockSpec((N_SHARD, D, SHARD_W), lambda i: (0, 0, 0))],
        out_specs=[row(2 * SHARD_W), row(GROUP_W), row(D), row(3 * GROUP_W)],
        out_shape=[jax.ShapeDtypeStruct((S, 2 * SHARD_W), F32), jax.ShapeDtypeStruct((S, GROUP_W), F32),
                   jax.ShapeDtypeStruct((S, D), BF16), jax.ShapeDtypeStruct((S, 3 * GROUP_W), BF16)],
        compiler_params=_cp("arbitrary"),
    )(x, vecs, w3)


def _sb_logits(qh, k2, keep):
    z = _dot_nt(qh, k2)
    sp = jnp.log(1.0 + jnp.exp(-jnp.abs(z)))
    lb = jnp.minimum(z, 0.0) - sp
    lk = lb - z
    if keep is not None:
        lk = jnp.where(keep, lk, 0.0)
    return lb, lk


class _sb_chains:
    def __init__(self, i, q2, do_b=None):
        t = self.t = SB_T // SB_CHAINS
        self.C = range(SB_CHAINS)
        r = lax.broadcasted_iota(jnp.int32, (SB_NB * t, SB_NB * t), 0)
        c = lax.broadcasted_iota(jnp.int32, (SB_NB * t, SB_NB * t), 1)
        self.later_all = jnp.where(r > c, 1.0, 0.0).astype(BF16)
        self.earlier_all = jnp.where(r < c, 1.0, 0.0).astype(BF16)
        self.later, self.earlier = self.later_all[:t, :t], self.earlier_all[:t, :t]
        self.head0 = lax.broadcasted_iota(jnp.int32, (1, LANES), 1) < SB_HEAD_DIM
        row = lax.broadcasted_iota(jnp.int32, (2 * t, SB_NB * t), 0) & (t - 1)
        col = lax.broadcasted_iota(jnp.int32, (2 * t, SB_NB * t), 1)
        qt = [SB_CHAINS * i + cc for cc in self.C]
        self.first = [jnp.maximum(qt[cc] - (SB_NB - 1), 0) for cc in self.C]
        self.keep = [self.first[cc] * t + col < qt[cc] * t + row for cc in self.C]
        self.qs = [self._stack(q2[cc * t:(cc + 1) * t]) for cc in self.C]
        if do_b is not None:
            self.dos = [self._stack(do_b[cc * t:(cc + 1) * t]) for cc in self.C]

    def _stack(self, a):
        zero = jnp.zeros_like(a)
        return jnp.concatenate([jnp.where(self.head0, a, zero), jnp.where(self.head0, zero, a)], axis=0)

    def rows(self, ref, j, n):
        return ref[pl.ds(pl.multiple_of(j * self.t, self.t), n * self.t), :]

    def suffix(self, lk):
        return _running_sum(lk, self.later_all), _rowsum(lk)

    def prefix(self, g, G0):
        return _running_sum(g, self.earlier_all) + G0


def sb_fwd(sb, sg, gather=None):
    S = sb.shape[0]
    T = SB_T
    nq = S // T
    carried = [] if gather is None else [gather]

    def body(*refs):
        (q_ref, k_ref, v_ref, sg_ref), refs = refs[:4], refs[4:]
        p, i = pl.program_id(0), pl.program_id(1)
        if carried:
            x_ref, y_ref, o_ref, end_ref, out_ref, send_sems, recv_sems, local_sem = refs
            start, forward, finish = _gather_ops(x_ref, out_ref, send_sems, recv_sems, local_sem)
            pl.when(jnp.logical_and(p == 0, i == 0))(start)
            pl.when(jnp.logical_and(p == 3, i == 0))(forward)
        else:
            y_ref, o_ref, end_ref = refs
        ch = _sb_chains(i, q_ref[...])
        later, head0 = ch.later, ch.head0
        lbk = [_sb_logits(ch.qs[c], ch.rows(k_ref, ch.first[c], SB_NB), ch.keep[c]) for c in ch.C]
        suffix, R = zip(*[ch.suffix(lbk[c][1]) for c in ch.C])
        aa = [jnp.where(ch.keep[c], jnp.exp(lbk[c][0] + suffix[c]), 0.0) for c in ch.C]
        acc = [_dot(aa[c].astype(BF16), ch.rows(v_ref, ch.first[c], SB_NB)) for c in ch.C]

        nc = len(ch.C)

        def alive(n, Rs):
            m = None
            for c in ch.C:
                rc = jnp.where(ch.first[c] - n > 0, Rs[c], EXP_ZERO)
                m = rc if m is None else jnp.maximum(m, rc)
            return jnp.max(m)

        def cond(st):
            return st[-1] > EXP_ZERO

        def step(st):
            n, accs, Rs = st[0], list(st[1:1 + nc]), list(st[1 + nc:1 + 2 * nc])
            for c in ch.C:
                j = ch.first[c] - 1 - n
                jc = jnp.maximum(j, 0)
                lb, lk = _sb_logits(ch.qs[c], ch.rows(k_ref, jc, 1), None)
                a = jnp.exp(lb + _running_sum(lk, later) + Rs[c])
                cx = _dot(a.astype(BF16), ch.rows(v_ref, jc, 1))
                accs[c] = jnp.where(j >= 0, accs[c] + cx, accs[c])
                Rs[c] = jnp.where(j >= 0, Rs[c] + _rowsum(lk), Rs[c])
            return (n + 1, *accs, *Rs, alive(n + 1, Rs))

        st = lax.while_loop(cond, step, (jnp.int32(0), *acc, *R, alive(0, R)))
        n_end, acc, R = st[0], st[1:1 + nc], st[1 + nc:1 + 2 * nc]
        outs = []
        for c in ch.C:
            base = c * (2 * ch.t + 8)
            end_ref[0, 0, base:base + 2 * ch.t, :] = jnp.broadcast_to(R[c], (2 * ch.t, 8))
            end_ref[0, 0, base + 2 * ch.t:base + 2 * ch.t + 8, :] = jnp.full((8, 8), n_end.astype(F32))
            outs.append(jnp.where(head0, acc[c][:ch.t], acc[c][ch.t:]))
        o = jnp.concatenate(outs, axis=0)
        o_ref[...] = o
        sg = sg_ref[...]
        y_ref[...] = (o * (sg * _sigmoid(sg))).astype(BF16)
        if carried:
            pl.when(jnp.logical_and(p == 3, i == nq - 1))(finish)

    return pl.pallas_call(
        body, name="sb_fwd", grid=(4, nq),
        in_specs=[pl.BlockSpec((T, LANES), lambda p, i: (i, p)),
                  pl.BlockSpec((S, LANES), lambda p, i: (0, 4 + p)),
                  pl.BlockSpec((S, LANES), lambda p, i: (0, 8 + p)),
                  pl.BlockSpec((T, LANES), lambda p, i: (i, p))] + [_ANY for _ in carried],
        out_specs=[pl.BlockSpec((T, LANES), lambda p, i: (i, p)),
                   pl.BlockSpec((T, LANES), lambda p, i: (i, p)),
                   pl.BlockSpec((1, 1, SB_CHAINS * (2 * T // SB_CHAINS + 8), 8), lambda p, i: (p, i, 0, 0))] + [_ANY for _ in carried],
        out_shape=[jax.ShapeDtypeStruct((S, GROUP_W), BF16),
                   jax.ShapeDtypeStruct((S, GROUP_W), F32),
                   jax.ShapeDtypeStruct((4, nq, SB_CHAINS * (2 * T // SB_CHAINS + 8), 8), F32)]
        + [jax.ShapeDtypeStruct((8,) + a.shape, a.dtype) for a in carried],
        scratch_shapes=_GATHER_SCRATCH if carried else [],
        compiler_params=_cp("arbitrary", "arbitrary"),
    )(sb, sb, sb, sg, *carried)


def sb_bwd(sb, sg, o, sb_end, dycat, ship=None):
    S = sb.shape[0]
    T = SB_T
    nq = S // T
    ex = _Exchange(ship)

    def body(*refs):
        (q_ref, k_ref, v_ref, sg_ref, o_ref, dy_ref, end_ref), refs = refs[:7], refs[7:]
        ship_refs, (dq_ref, dk_ref, dv_ref, dsg_ref), refs = refs[:ex.n_in], refs[ex.n_in:ex.n_in + 4], refs[ex.n_in + 4:]
        recv, (dk_acc, dv_acc), sems = refs[:ex.n_out], refs[ex.n_out:ex.n_out + 2], refs[ex.n_out + 2:]
        start, finish = ex.ops(ship_refs, recv + sems)
        p, i = pl.program_id(0), pl.program_id(1)
        pl.when(jnp.logical_and(p == 0, i == 0))(start)

        @pl.when(i == 0)
        def _():
            dk_acc[...] = jnp.zeros_like(dk_acc)
            dv_acc[...] = jnp.zeros_like(dv_acc)

        sg = sg_ref[...]
        sig = _sigmoid(sg)
        dy = dy_ref[...]
        dsg_ref[...] = (dy * o_ref[...] * (sig * (1.0 + sg * (1.0 - sig)))).astype(BF16)
        do_b = (dy * (sg * sig)).astype(BF16)
        ch = _sb_chains(i, q_ref[...], do_b)
        later, earlier, head0, t = ch.later, ch.earlier, ch.head0, ch.t
        end = end_ref[0, 0]

        def grads(c, j, n, a, lb, g, G, keep):
            dz = g - jnp.exp(lb) * (g + G)
            if keep is not None:
                dz = jnp.where(keep, dz, 0.0)
            dzb = dz.astype(BF16)
            rows = pl.ds(pl.multiple_of(j * t, t), n * t)
            dk_acc[rows, :] += _dot_tn(dzb, ch.qs[c])
            dv_acc[rows, :] += _dot_tn(a.astype(BF16), ch.dos[c])
            return _dot(dzb, ch.rows(k_ref, j, n))

        nc = len(ch.C)
        n_end = jnp.max(end[2 * t:2 * t + 8, :]).astype(jnp.int32)

        def sweep(m, st):
            dqs, G0s, lefts = list(st[:nc]), list(st[nc:2 * nc]), list(st[2 * nc:])
            for c in ch.C:
                j = ch.first[c] - n_end + m
                jc = jnp.maximum(j, 0)
                lb, lk = _sb_logits(ch.qs[c], ch.rows(k_ref, jc, 1), None)
                stick = lefts[c] - _rowsum(lk)
                a = jnp.where(j >= 0, jnp.exp(lb + _running_sum(lk, later) + stick), 0.0)
                g = a * _dot_nt(ch.dos[c], ch.rows(v_ref, jc, 1))
                G = _running_sum(g, earlier) + G0s[c]
                dqs[c] = dqs[c] + grads(c, jc, 1, a, lb, jnp.where(j >= 0, g, 0.0), jnp.where(j >= 0, G, 0.0), None)
                G0s[c] = G0s[c] + _rowsum(g)
                lefts[c] = jnp.where(j >= 0, stick, lefts[c])
            return (*dqs, *G0s, *lefts)

        lefts = [end[c * (2 * t + 8):c * (2 * t + 8) + 2 * t, 0:1] for c in ch.C]
        st = lax.fori_loop(0, n_end, sweep, (*[jnp.zeros((2 * t, LANES), F32)] * nc,
                                             *[jnp.zeros((2 * t, 1), F32)] * nc, *lefts))
        dq, G0 = st[:nc], st[nc:2 * nc]

        lbk = [_sb_logits(ch.qs[c], ch.rows(k_ref, ch.first[c], SB_NB), ch.keep[c]) for c in ch.C]
        suffix = [ch.suffix(lbk[c][1])[0] for c in ch.C]
        aa = [jnp.where(ch.keep[c], jnp.exp(lbk[c][0] + suffix[c]), 0.0) for c in ch.C]
        g = [aa[c] * _dot_nt(ch.dos[c], ch.rows(v_ref, ch.first[c], SB_NB)) for c in ch.C]
        G = [ch.prefix(g[c], G0[c]) for c in ch.C]
        for c in ch.C:
            dqc = dq[c] + grads(c, ch.first[c], SB_NB, aa[c], lbk[c][0], g[c], G[c], ch.keep[c])
            dq_ref[c * t:(c + 1) * t, :] = (jnp.where(head0, dqc[:t], dqc[t:]) * SQ_SCALE).astype(BF16)

        @pl.when(i == nq - 1)
        def _():
            dk_ref[...] = dk_acc[...].astype(BF16)
            dv_ref[...] = dv_acc[...].astype(BF16)

        pl.when(jnp.logical_and(p == 3, i == nq - 1))(finish)

    tile_spec = lambda c0: pl.BlockSpec((T, LANES), lambda p, i: (i, c0 + p))
    head_spec = lambda c0: pl.BlockSpec((S, LANES), lambda p, i: (0, c0 + p))
    return pl.pallas_call(
        body, name="sb_bwd", grid=(4, nq),
        in_specs=[tile_spec(0), head_spec(4), head_spec(8), tile_spec(0), tile_spec(0), tile_spec(4),
                  pl.BlockSpec((1, 1, SB_CHAINS * (2 * T // SB_CHAINS + 8), 8), lambda p, i: (p, i, 0, 0))] + ex.in_specs,
        out_specs=[tile_spec(0), head_spec(0), head_spec(0), tile_spec(0)] + ex.out_specs,
        out_shape=[jax.ShapeDtypeStruct((S, GROUP_W), BF16)] * 4 + ex.out_shape,
        scratch_shapes=[pltpu.VMEM((S, LANES), F32), pltpu.VMEM((S, LANES), F32)] + ex.scratch,
        compiler_params=_cp("arbitrary", "arbitrary"),
    )(sb, sb, sb, sg, o, dycat, sb_end, *ex.ship)


def rope_tables(S):
    half = RET_HEAD_DIM // 2
    inv = ROPE_BASE ** (-jnp.arange(half, dtype=F32) / half)
    ang = jnp.arange(S, dtype=F32)[:, None] * inv[None, :]
    cos, sin = jnp.cos(ang), jnp.sin(ang)
    return jnp.concatenate([cos, cos], axis=1), jnp.concatenate([-sin, sin], axis=1)


def ret_log_gamma():
    return jnp.log1p(-(2.0 ** (-5.0 - jnp.arange(4, dtype=F32))))


def _swap_halves(a):
    return pltpu.roll(a, RET_HEAD_DIM // 2, axis=1)


def _ret_decay_mask(lg):
    n = lax.broadcasted_iota(jnp.int32, (RET_T, RET_T), 0)
    m = lax.broadcasted_iota(jnp.int32, (RET_T, RET_T), 1)
    dist = jnp.abs(n - m).astype(F32)
    return jnp.where((m // CHUNK) <= (n // CHUNK), jnp.exp(lg * dist), 0.0)


def _ret_block(lg, rq, rk, rv, cosf, sinf, dm):
    q = rq * cosf + _swap_halves(rq) * sinf
    k = (rk * cosf + _swap_halves(rk) * sinf) * RK_SCALE
    qb, kb, vb = q.astype(BF16), k.astype(BF16), rv.astype(BF16)
    sc = _dot_nt(qb, kb) * dm
    nloc = lax.broadcasted_iota(jnp.int32, (RET_T, 1), 0).astype(F32)
    qdec = jnp.exp(lg * (nloc + 1.0))
    kdec = jnp.exp(lg * (RET_T - 1.0 - nloc))
    block_dec = jnp.exp(jnp.full((1, LANES), lg * RET_T, F32))
    return q, k, qb, kb, vb, sc, qdec, kdec, block_dec


RET_RB = 2


def _ret_specs(S, rb):
    group = lambda c0: pl.BlockSpec((RET_RB * RET_T, GROUP_W), lambda s: (rb(s), c0))
    return group, pl.BlockSpec((RET_RB * RET_T, LANES), lambda s: (rb(s), 0))


def _ret_chains():
    chains = [(h, b) for b in range(RET_RB) for h in range(4)]
    rows = lambda c: (slice(c[1] * RET_T, (c[1] + 1) * RET_T), slice(c[0] * LANES, (c[0] + 1) * LANES))
    tab = lambda ref, c: ref[c[1] * RET_T:(c[1] + 1) * RET_T, :]
    return chains, rows, tab


def _ret_blocks(chains, rows, lg_ref, rq_ref, rk_ref, rv_ref, cosf, sinf, dm_ref):
    blk = {c: _ret_block(lg_ref[c[0]], rq_ref[rows(c)], rk_ref[rows(c)], rv_ref[rows(c)],
                         cosf[c], sinf[c], dm_ref[c[0]]) for c in chains}
    return ({c: blk[c][n] for c in chains} for n in range(9))


def ret_fwd(proj, cosf, sinf, lgam):
    S = proj.shape[0]
    nb = S // RET_T
    group, row_tab = _ret_specs(S, lambda s: s)

    def body(lg_ref, rq_ref, rk_ref, rv_ref, rg_ref, cos_ref, sin_ref, y_ref, o_ref, st_out, st_ref, dm_ref):
        @pl.when(pl.program_id(0) == 0)
        def _():
            st_ref[...] = jnp.zeros_like(st_ref)
            for h in range(4):
                dm_ref[h] = _ret_decay_mask(lg_ref[h])

        chains, rows, tab = _ret_chains()
        cosf, sinf = {c: tab(cos_ref, c) for c in chains}, {c: tab(sin_ref, c) for c in chains}
        q, k, qb, kb, vb, sc, qdec, kdec, block_dec = _ret_blocks(
            chains, rows, lg_ref, rq_ref, rk_ref, rv_ref, cosf, sinf, dm_ref)
        kv = {c: _dot_tn((k[c] * kdec[c]).astype(BF16), vb[c]) for c in chains}
        st = {(h, 0): st_ref[h] for h in range(4)}
        for b in range(RET_RB):
            for h in range(4):
                st[(h, b + 1)] = st[(h, b)] * block_dec[(h, b)] + kv[(h, b)]
        for h, b in chains:
            st_out[h, b] = st[(h, b)]
        for h in range(4):
            st_ref[h] = st[(h, RET_RB)]
        o = {c: _dot(sc[c].astype(BF16), vb[c]) + _dot(qb[c], st[c].astype(BF16)) * qdec[c] for c in chains}
        for c in chains:
            o_ref[rows(c)] = o[c]
        cen = {c: o[c] - _rowmean(o[c]) for c in chains}
        on = {c: cen[c] * lax.rsqrt(_rowmean(cen[c] * cen[c]) + EPS) for c in chains}
        rg = {c: rg_ref[rows(c)] for c in chains}
        for c in chains:
            y_ref[rows(c)] = (on[c] * (rg[c] * _sigmoid(rg[c]))).astype(BF16)

    return pl.pallas_call(
        body, name="ret_fwd", grid=(nb // RET_RB,),
        in_specs=[pl.BlockSpec(memory_space=pltpu.SMEM),
                  group(0), group(1), group(2), group(3), row_tab, row_tab],
        out_specs=[group(0), group(0),
                   pl.BlockSpec((4, RET_RB, LANES, LANES), lambda s: (0, s, 0, 0))],
        out_shape=[jax.ShapeDtypeStruct((S, GROUP_W), BF16),
                   jax.ShapeDtypeStruct((S, GROUP_W), F32),
                   jax.ShapeDtypeStruct((4, nb, LANES, LANES), F32)],
        scratch_shapes=[pltpu.VMEM((4, LANES, LANES), F32), pltpu.VMEM((4, RET_T, RET_T), F32)],
        compiler_params=_cp("arbitrary"),
    )(lgam, proj, proj, proj, proj, cosf, sinf)


def ret_bwd(proj, cosf, sinf, lgam, o, states, dycat):
    S = proj.shape[0]
    nsteps = S // RET_T // RET_RB
    rev = lambda s: nsteps - 1 - s
    group, row_tab = _ret_specs(S, rev)

    def body(lg_ref, rq_ref, rk_ref, rv_ref, rg_ref, cos_ref, sin_ref, o_ref, st_in, dy_ref,
             drq_ref, drk_ref, drv_ref, drg_ref, ds_ref, dm_ref):
        @pl.when(pl.program_id(0) == 0)
        def _():
            ds_ref[...] = jnp.zeros_like(ds_ref)
            for h in range(4):
                dm_ref[h] = _ret_decay_mask(lg_ref[h])

        chains, rows, tab = _ret_chains()
        cosf, sinf = {c: tab(cos_ref, c) for c in chains}, {c: tab(sin_ref, c) for c in chains}
        dms = {c: dm_ref[c[0]] for c in chains}
        q, k, qb, kb, vb, sc, qdec, kdec, block_dec = _ret_blocks(
            chains, rows, lg_ref, rq_ref, rk_ref, rv_ref, cosf, sinf, dm_ref)
        o_v = {c: o_ref[rows(c)] for c in chains}
        cen = {c: o_v[c] - _rowmean(o_v[c]) for c in chains}
        rstd = {c: lax.rsqrt(_rowmean(cen[c] * cen[c]) + EPS) for c in chains}
        on = {c: cen[c] * rstd[c] for c in chains}
        rg = {c: rg_ref[rows(c)] for c in chains}
        sig = {c: _sigmoid(rg[c]) for c in chains}
        dy = {c: dy_ref[rows(c)] for c in chains}
        for c in chains:
            drg_ref[rows(c)] = (dy[c] * on[c] * (sig[c] * (1.0 + rg[c] * (1.0 - sig[c])))).astype(BF16)
        don = {c: dy[c] * (rg[c] * sig[c]) for c in chains}
        do = {c: rstd[c] * (don[c] - _rowmean(don[c]) - on[c] * _rowmean(don[c] * on[c])) for c in chains}
        dob = {c: do[c].astype(BF16) for c in chains}
        dsc = {c: (_dot_nt(dob[c], vb[c]) * dms[c]).astype(BF16) for c in chains}
        st_b = {c: st_in[c[0], c[1]].astype(BF16) for c in chains}
        dst = {c: _dot_tn((q[c] * qdec[c]).astype(BF16), dob[c]) for c in chains}
        dsn = {(h, RET_RB): ds_ref[h] for h in range(4)}
        for b in reversed(range(RET_RB)):
            for h in range(4):
                dsn[(h, b)] = dsn[(h, b + 1)] * block_dec[(h, b)] + dst[(h, b)]
        for h in range(4):
            ds_ref[h] = dsn[(h, 0)]
        dsn_b = {c: dsn[(c[0], c[1] + 1)].astype(BF16) for c in chains}
        dq = {c: _dot(dsc[c], kb[c]) + _dot_nt(dob[c], st_b[c]) * qdec[c] for c in chains}
        dk = {c: (_dot_tn(dsc[c], qb[c]) + _dot_nt(vb[c], dsn_b[c]) * kdec[c]) * RK_SCALE for c in chains}
        dv = {c: _dot_tn(sc[c].astype(BF16), dob[c]) + _dot((k[c] * kdec[c]).astype(BF16), dsn_b[c])
              for c in chains}
        for c in chains:
            drq_ref[rows(c)] = (dq[c] * cosf[c] + _swap_halves(dq[c] * sinf[c])).astype(BF16)
            drk_ref[rows(c)] = (dk[c] * cosf[c] + _swap_halves(dk[c] * sinf[c])).astype(BF16)
            drv_ref[rows(c)] = dv[c].astype(BF16)

    return pl.pallas_call(
        body, name="ret_bwd", grid=(nsteps,),
        in_specs=[pl.BlockSpec(memory_space=pltpu.SMEM),
                  group(0), group(1), group(2), group(3), row_tab, row_tab,
                  group(0), pl.BlockSpec((4, RET_RB, LANES, LANES), lambda s: (0, rev(s), 0, 0)),
                  group(0)],
        out_specs=[group(0)] * 4,
        out_shape=[jax.ShapeDtypeStruct((S, GROUP_W), BF16)] * 4,
        scratch_shapes=[pltpu.VMEM((4, LANES, LANES), F32), pltpu.VMEM((4, RET_T, RET_T), F32)],
        compiler_params=_cp("arbitrary"),
    )(lgam, proj, proj, proj, proj, cosf, sinf, o, states, dycat)


def outproj_fwd(x, vecs, y_ret, y_sb, w_out, head=None, tm=1024):
    S, D = x.shape
    tm = min(tm, S)
    last = list(head or ())

    def body(x_ref, v_ref, yr_ref, ys_ref, w_ref, *refs):
        y = _dot(yr_ref[...], w_ref[0:GROUP_W, :]) + _dot(ys_ref[...], w_ref[GROUP_W:, :])
        xv = x_ref[...] + v_ref[2:3, :] * y
        if not last:
            y_ref, xo_ref = refs
            y_ref[...] = y.astype(BF16)
            xo_ref[...] = xv
            return
        g_ref, t_ref, y_ref, dx_ref, st_ref = refs
        y_ref[...] = y.astype(BF16)

        @pl.when(pl.program_id(0) == 0)
        def _():
            st_ref[...] = jnp.zeros_like(st_ref)

        g = g_ref[0:1, :]
        r = lax.rsqrt(_rowmean(xv * xv) + EPS)
        xn = xv * r
        err = xn * g - t_ref[...]
        dy = err * (1.0 / D)
        dxn = dy * g
        dx_ref[...] = r * (dxn - xn * _rowmean(dxn * xn))
        st_ref[0:1, :] += jnp.sum(dy * xn, axis=0, keepdims=True)
        st_ref[1:2, :] += jnp.sum(err * err, axis=0, keepdims=True)

    row = lambda w: pl.BlockSpec((tm, w), lambda i: (i, 0))
    fixed = pl.BlockSpec((8, D), lambda i: (0, 0))
    return pl.pallas_call(
        body, name="outproj_fwd", grid=(S // tm,),
        in_specs=[row(D), fixed, row(GROUP_W), row(GROUP_W), pl.BlockSpec((D, D), lambda i: (0, 0))]
        + ([fixed, row(D)] if last else []),
        out_specs=[row(D), row(D)] + ([fixed] if last else []),
        out_shape=[jax.ShapeDtypeStruct((S, D), BF16), jax.ShapeDtypeStruct((S, D), F32)]
        + ([jax.ShapeDtypeStruct((8, D), F32)] if last else []),
        compiler_params=_cp("arbitrary"),
    )(x, vecs, y_ret, y_sb, w_out, *last)


def outproj_bwd(dx, y, vecs, y_ret, y_sb, w_out, tm=1024):
    S, D = dx.shape
    tm = min(tm, S)
    n = S // tm

    def body(dx_ref, y_ref, v_ref, yr_ref, ys_ref, w_ref, dyc_ref, dw_ref, st_ref, acc):
        i = pl.program_id(0)

        @pl.when(i == 0)
        def _():
            st_ref[...] = jnp.zeros_like(st_ref)
            acc[...] = jnp.zeros_like(acc)

        dxv = dx_ref[...]
        st_ref[0:1, :] += jnp.sum(dxv * y_ref[...].astype(F32), axis=0, keepdims=True)
        dyy = (dxv * v_ref[2:3, :]).astype(BF16)
        dyc_ref[...] = _dot_nt(dyy, w_ref[...])
        acc[0:GROUP_W, :] += _dot_tn(yr_ref[...], dyy)
        acc[GROUP_W:, :] += _dot_tn(ys_ref[...], dyy)

        @pl.when(i == n - 1)
        def _():
            dw_ref[...] = acc[...].astype(BF16)

    row = lambda w: pl.BlockSpec((tm, w), lambda i: (i, 0))
    fixed = lambda r: pl.BlockSpec((r, D), lambda i: (0, 0))
    return pl.pallas_call(
        body, name="outproj_bwd", grid=(n,),
        in_specs=[row(D), row(D), fixed(8), row(GROUP_W), row(GROUP_W), fixed(D)],
        out_specs=[row(D), fixed(D), fixed(8)],
        out_shape=[jax.ShapeDtypeStruct((S, D), F32), jax.ShapeDtypeStruct((D, D), BF16),
                   jax.ShapeDtypeStruct((8, D), F32)],
        scratch_shapes=[pltpu.VMEM((D, D), F32)],
        compiler_params=_cp("arbitrary"),
    )(dx, y, vecs, y_ret, y_sb, w_out)


def inproj_bwd_x(pieces, w3, x, vecs, dx_res, ship=None, tm=512):
    S, D = x.shape
    n = S // tm
    ex = _Exchange(ship)

    def body(*refs):
        p_refs, (w_ref, x_ref, v_ref, dr_ref), refs = refs[:8], refs[8:12], refs[12:]
        ship_refs, (dx_ref, st_ref), refs = refs[:ex.n_in], refs[ex.n_in:ex.n_in + 2], refs[ex.n_in + 2:]
        start, finish = ex.ops(ship_refs, refs)

        @pl.when(pl.program_id(0) == 0)
        def _():
            st_ref[...] = jnp.zeros_like(st_ref)
            start()

        dh = jnp.zeros((tm, D), F32)
        for k, p_ref in enumerate(p_refs):
            c0 = (k % 2) * GROUP_W
            dh = dh + _dot_nt(p_ref[...], w_ref[k // 2, :, c0:c0 + GROUP_W])
        xv = x_ref[...]
        r = lax.rsqrt(_rowmean(xv * xv) + EPS)
        xn = xv * r
        g, scale1 = v_ref[3:4, :], 1.0 + v_ref[1:2, :]
        st_ref[0:1, :] += jnp.sum(dh, axis=0, keepdims=True)
        dh_xn = dh * xn
        st_ref[1:2, :] += jnp.sum(dh_xn, axis=0, keepdims=True) * g
        st_ref[2:3, :] += jnp.sum(dh_xn, axis=0, keepdims=True) * scale1
        dxn = dh * (g * scale1)
        dx_ref[...] = r * (dxn - xn * _rowmean(dxn * xn)) + dr_ref[...]
        pl.when(pl.program_id(0) == n - 1)(finish)

    row = lambda w: pl.BlockSpec((tm, w), lambda i: (i, 0))
    return pl.pallas_call(
        body, name="inproj_bwd_x", grid=(n,),
        in_specs=[row(GROUP_W)] * 8 + [pl.BlockSpec((N_SHARD, D, SHARD_W), lambda i: (0, 0, 0)),
                                       row(D), pl.BlockSpec((8, D), lambda i: (0, 0)), row(D)] + ex.in_specs,
        out_specs=[row(D), pl.BlockSpec((8, D), lambda i: (0, 0))] + ex.out_specs,
        out_shape=[jax.ShapeDtypeStruct((S, D), F32), jax.ShapeDtypeStruct((8, D), F32)] + ex.out_shape,
        scratch_shapes=ex.scratch,
        compiler_params=_cp("arbitrary"),
    )(*pieces, w3, x, vecs, dx_res, *ex.ship)


def inproj_bwd_w(h, pieces, tm=1024):
    S, D = h.shape
    tm = min(tm, S)
    n = S // tm

    def body(*refs):
        h_ref, p_refs, dw_ref, acc = refs[0], refs[1:9], refs[9], refs[10]
        i = pl.program_id(0)

        @pl.when(i == 0)
        def _():
            acc[...] = jnp.zeros_like(acc)

        hv = h_ref[...]
        for k, p_ref in enumerate(p_refs):
            c0 = (k % 2) * GROUP_W
            acc[k // 2, :, c0:c0 + GROUP_W] += _dot_tn(hv, p_ref[...])

        @pl.when(i == n - 1)
        def _():
            dw_ref[...] = acc[...].astype(BF16)

    row = lambda w: pl.BlockSpec((tm, w), lambda i: (i, 0))
    return pl.pallas_call(
        body, name="inproj_bwd_w", grid=(n,),
        in_specs=[row(D)] + [row(GROUP_W)] * 8,
        out_specs=pl.BlockSpec((N_SHARD, D, SHARD_W), lambda i: (0, 0, 0), pipeline_mode=pl.Buffered(1)),
        out_shape=jax.ShapeDtypeStruct((N_SHARD, D, SHARD_W), BF16),
        scratch_shapes=[pltpu.VMEM((N_SHARD, D, SHARD_W), F32)],
        compiler_params=_cp("arbitrary"),
    )(h, *pieces)


def layer_fwd(x, vecs, w3, w_out, tabs, gather=None, head=None):
    cosf, sinf, lgam = tabs
    ret, sg, h, sb = inproj_fwd(x, vecs, w3)
    y_ret, o_ret, states = ret_fwd(ret, cosf, sinf, lgam)
    y_sb, o_sb, sb_end, *gathered = sb_fwd(sb, sg, gather)
    if callable(w_out):
        w_out = w_out(gathered[0])
    y, *x_next = outproj_fwd(x, vecs, y_ret, y_sb, w_out, head)
    saved = (x, ret, sg, h, sb, y_ret, o_ret, states, y_sb, o_sb, sb_end, y)
    return (x_next[0] if head is None else x_next), saved, (gathered[0] if gathered else None)


def _by_shard(dw_out):
    return dw_out.reshape(N_SHARD, D_MODEL // N_SHARD, D_MODEL)


def layer_bwd(dx, saved, vecs, w3, w_out, tabs, later_grads=None):
    cosf, sinf, lgam = tabs
    x, ret, sg, h, sb, y_ret, o_ret, states, y_sb, o_sb, sb_end, y = saved
    dycat, dw_out, st_o = outproj_bwd(dx, y, vecs, y_ret, y_sb, w_out)
    dw_out = _by_shard(dw_out)
    ship = None if later_grads is None else (later_grads[0], dw_out, later_grads[1])
    *d_sb, = sb_bwd(sb, sg, o_sb, sb_end, dycat, ship)
    d_ret = ret_bwd(ret, cosf, sinf, lgam, o_ret, states, dycat)
    pieces = list(d_ret) + d_sb[:4]
    dw_in = inproj_bwd_w(h, pieces)
    dx, st_i, *recv_in = inproj_bwd_x(pieces, w3, x, vecs, dx, None if later_grads is None else (dw_in,))
    dmod = jnp.concatenate([st_i[0:2], st_o[0:1]], axis=0)
    grads = (dw_in, dw_out) if later_grads is None else (recv_in[0], d_sb[4])
    return dx, dmod, st_i[2:3], grads


def _place():
    return lax.axis_index("x"), lax.axis_index("y"), lax.axis_index("c")


def _other_chips(mx, my):
    return [(1 - mx, my), (mx, 1 - my), (1 - mx, 1 - my)]


_ANY = pl.BlockSpec(memory_space=pl.ANY)


_GATHER_SCRATCH = [pltpu.SemaphoreType.DMA((7,)), pltpu.SemaphoreType.DMA((7,)), pltpu.SemaphoreType.DMA(())]


def _gather_ops(x_ref, out_ref, send_sems, recv_sems, local_sem):
    mx, my, mc = _place()
    me, sibling = (mx, my, mc), (mx, my, 1 - mc)
    chips = _other_chips(mx, my)

    def slot(px, py, pc):
        return out_ref.at[4 * px + 2 * py + pc]

    def copy(k, block, to, src=None):
        return pltpu.make_async_remote_copy(
            src_ref=slot(*block) if src is None else src, dst_ref=slot(*block),
            send_sem=send_sems.at[k], recv_sem=recv_sems.at[k], device_id=to, device_id_type=MESH)

    mine = pltpu.make_async_copy(x_ref, slot(*me), local_sem)
    first = [copy(0, me, sibling, src=x_ref)]
    first += [copy(1 + j, me, (*chip, mc), src=x_ref) for j, chip in enumerate(chips)]
    passed = [copy(4 + j, (*chip, mc), sibling) for j, chip in enumerate(chips)]

    def start():
        mine.start()
        for cp in first:
            cp.start()

    def forward():
        for j, chip in enumerate(chips):
            copy(1 + j, (*chip, mc), me).wait_recv()
            passed[j].start()

    def finish():
        copy(0, sibling, me).wait_recv()
        for j, chip in enumerate(chips):
            copy(4 + j, (*chip, 1 - mc), me).wait_recv()
        for cp in first + passed:
            cp.wait_send()
        mine.wait()

    return start, forward, finish


def allgather8(x, name):
    def body(x_ref, out_ref, send_sems, recv_sems, local_sem):
        for step in _gather_ops(x_ref, out_ref, send_sems, recv_sems, local_sem):
            step()

    return pl.pallas_call(
        body, name=name, out_shape=jax.ShapeDtypeStruct((8,) + x.shape, x.dtype),
        in_specs=[_ANY], out_specs=_ANY, scratch_shapes=_GATHER_SCRATCH,
    )(x)


class _Exchange:
    def __init__(self, ship):
        self.ship = list(ship or ())
        self.n_in = len(self.ship)
        self.n_out = 1 if self.ship else 0
        self.rows = [a.shape[1] for a in self.ship]
        self.in_specs = [_ANY] * self.n_in
        self.out_specs = [_ANY] * self.n_out
        self.out_shape = [jax.ShapeDtypeStruct((N_SHARD, sum(self.rows), SHARD_W), BF16)] * self.n_out
        sem = pltpu.SemaphoreType.DMA
        self.scratch = [sem((3,)), sem((3,)), sem(())] * self.n_out

    def ops(self, ship_refs, tail):
        if not self.ship:
            return (lambda: None), (lambda: None)
        recv, send_sems, recv_sems, local_sem = tail
        mx, my, mc = _place()
        my_chip = 2 * mx + my
        chips = _other_chips(mx, my)

        def pieces(s):
            firsts = np.cumsum([0] + self.rows[:-1])
            return [(ref.at[s], int(r0), n) for ref, r0, n in zip(ship_refs, firsts, self.rows)]

        def start():
            for src, r0, n in pieces(my_chip):
                pltpu.make_async_copy(src, recv.at[my_chip, pl.ds(r0, n)], local_sem).start()
            for j, (px, py) in enumerate(chips):
                for src, r0, n in pieces(2 * px + py):
                    pltpu.make_async_remote_copy(
                        src_ref=src, dst_ref=recv.at[my_chip, pl.ds(r0, n)],
                        send_sem=send_sems.at[j], recv_sem=recv_sems.at[j],
                        device_id=(px, py, mc), device_id_type=MESH).start()

        def finish():
            for j, (px, py) in enumerate(chips):
                whole = recv.at[2 * px + py]
                both = pltpu.make_async_remote_copy(
                    src_ref=whole, dst_ref=whole, send_sem=send_sems.at[j], recv_sem=recv_sems.at[j],
                    device_id=(px, py, mc), device_id_type=MESH)
                both.wait_recv()
                both.wait_send()
            pltpu.make_async_copy(recv.at[my_chip], recv.at[my_chip], local_sem).wait()

        return start, finish


def sum_slots(recv_a, recv_b, tr=256):
    n, rows_a, cols = recv_a.shape
    na, nb = rows_a // tr, recv_b.shape[1] // tr

    def body(a_ref, b_ref, o_ref):
        def total(r_ref):
            acc = r_ref[0].astype(F32)
            for k in range(1, n):
                acc = acc + r_ref[k].astype(F32)
            o_ref[...] = acc

        pl.when(pl.program_id(0) < na)(lambda: total(a_ref))
        pl.when(pl.program_id(0) >= na)(lambda: total(b_ref))

    return pl.pallas_call(
        body, name="sum_slots", grid=(na + nb,),
        in_specs=[pl.BlockSpec((n, tr, cols), lambda i: (0, jnp.minimum(i, na - 1), 0)),
                  pl.BlockSpec((n, tr, cols), lambda i: (0, jnp.maximum(i - na, 0), 0))],
        out_specs=pl.BlockSpec((tr, cols), lambda i: (i, 0)),
        out_shape=jax.ShapeDtypeStruct(((na + nb) * tr, cols), F32),
        compiler_params=_cp("arbitrary"),
    )(recv_a, recv_b)


def swap_sibling(p):
    def body(p_ref, out_ref, send_sem, recv_sem):
        mx, my, mc = _place()
        cp = pltpu.make_async_remote_copy(
            src_ref=p_ref, dst_ref=out_ref, send_sem=send_sem, recv_sem=recv_sem,
            device_id=(mx, my, 1 - mc), device_id_type=MESH)
        cp.start()
        cp.wait()

    return pl.pallas_call(
        body, name="swap_sibling", out_shape=jax.ShapeDtypeStruct(p.shape, p.dtype),
        in_specs=[_ANY], out_specs=_ANY,
        scratch_shapes=[pltpu.SemaphoreType.DMA(()), pltpu.SemaphoreType.DMA(())],
    )(p)


def _adamw(w, g, m, v):
    m = ADAM_B1 * m + (1.0 - ADAM_B1) * g
    v = ADAM_B2 * v + (1.0 - ADAM_B2) * (g * g)
    m_hat = m / (1.0 - ADAM_B1 ** ADAM_STEP)
    v_hat = v / (1.0 - ADAM_B2 ** ADAM_STEP)
    delta = -ADAM_LR * (m_hat / (jnp.sqrt(v_hat) + ADAM_EPS) + ADAM_WD * w)
    return delta, m, v


def adam_slab(p_own, p_sib, w, m, v, row0, name, tr=256):
    L, R, C = w.shape
    nr = R // tr

    def body(a_ref, b_ref, w_ref, m_ref, v_ref, g_out, d_out, m_out, v_out):
        g = a_ref[...] + b_ref[...]
        d, m2, v2 = _adamw(w_ref[0], g, m_ref[0], v_ref[0])
        g_out[0], d_out[0], m_out[0], v_out[0] = g, d, m2, v2

    slab = pl.BlockSpec((tr, C), lambda l, i: (row0 // tr + l * nr + i, 0))
    blk = pl.BlockSpec((1, tr, C), lambda l, i: (l, i, 0))
    return pl.pallas_call(
        body, name=name, grid=(L, nr),
        in_specs=[slab, slab, blk, blk, blk], out_specs=[blk] * 4,
        out_shape=[jax.ShapeDtypeStruct(w.shape, F32)] * 4,
        compiler_params=_cp("arbitrary", "arbitrary"),
    )(p_own, p_sib, w, m, v)


def prologue(c8, w_ada, b_ada, norm_g, win_first):
    L, D, W = w_ada.shape

    def body(c_ref, w_ref, b_ref, g_ref, win_ref, vecs_ref, call_ref, wall_ref, mod_ref, mall_ref, *sems):
        w_start, w_forward, w_finish = _gather_ops(win_ref, wall_ref, *sems[0:3])
        w_start()
        for step in _gather_ops(c_ref, call_ref, *sems[3:6]):
            step()
        cv = call_ref[:, 0, :]
        ca = cv * _sigmoid(cv)
        for l in range(L):
            mod_ref[l * 8:(l + 1) * 8, :] = jnp.dot(ca, w_ref[l], precision=lax.Precision.HIGHEST,
                                                    preferred_element_type=F32)
        for step in _gather_ops(mod_ref, mall_ref, *sems[6:9]):
            step()
        mx, my, mc = _place()
        me = 4 * mx + 2 * my + mc
        rowid = lax.broadcasted_iota(jnp.int32, (L * 8, 1), 0)
        vecs_ref[...] = jnp.zeros_like(vecs_ref)
        for l in range(L):
            parts = [jnp.sum(jnp.where(rowid == l * 8 + me, mall_ref[2 * s + mc], 0.0), axis=0, keepdims=True)
                     for s in range(N_SHARD)]
            mod = jnp.concatenate(parts, axis=1) + b_ref[l:l + 1, :]
            for t in range(3):
                vecs_ref[l, t:t + 1, :] = mod[:, t * D:(t + 1) * D]
            vecs_ref[l, 3:4, :] = g_ref[l:l + 1, :]
        w_forward()
        w_finish()

    vmem = pl.BlockSpec(memory_space=pltpu.VMEM)
    return pl.pallas_call(
        body, name="prologue",
        in_specs=[vmem, vmem, vmem, vmem, _ANY], out_specs=[vmem, vmem, _ANY],
        out_shape=[jax.ShapeDtypeStruct((L, 8, D), F32), jax.ShapeDtypeStruct((8, 8, D), F32),
                   jax.ShapeDtypeStruct((8,) + win_first.shape, win_first.dtype)],
        scratch_shapes=[pltpu.VMEM((L * 8, W), F32), pltpu.VMEM((8, L * 8, W), F32)] + _GATHER_SCRATCH * 3,
        compiler_params=pltpu.CompilerParams(vmem_limit_bytes=VMEM_LIMIT_BYTES),
    )(c8, w_ada, b_ada, norm_g, win_first)


def ada_update(dmods, c_t, w, m, v, tr=256):
    L, D, W = w.shape

    def body(dm_ref, c_ref, w_ref, m_ref, v_ref, g_out, d_out, m_out, v_out):
        mx, my, _ = _place()
        shard = 2 * mx + my
        dm = jnp.zeros((8, W), F32)
        for s in range(N_SHARD):
            dm = dm + jnp.where(shard == s, dm_ref[0, :, s * W:(s + 1) * W], 0.0)
        cv = c_ref[...]
        ca = cv * _sigmoid(cv)
        g = jnp.zeros((tr, W), F32)
        for b in range(8):
            g = g + ca[:, b:b + 1] * dm[b:b + 1, :]
        d, m2, v2 = _adamw(w_ref[0], g, m_ref[0], v_ref[0])
        g_out[0], d_out[0], m_out[0], v_out[0] = g, d, m2, v2

    blk = pl.BlockSpec((1, tr, W), lambda l, i: (l, i, 0))
    return pl.pallas_call(
        body, name="ada_update", grid=(L, D // tr),
        in_specs=[pl.BlockSpec((1, 8, 3 * D), lambda l, i: (l, 0, 0)), pl.BlockSpec((tr, 8), lambda l, i: (i, 0)),
                  blk, blk, blk],
        out_specs=[blk] * 4, out_shape=[jax.ShapeDtypeStruct(w.shape, F32)] * 4,
        compiler_params=_cp("arbitrary", "arbitrary"),
    )(dmods, c_t, w, m, v)


STAT_ROWS = 16


def small_update(stats_all, norm, b_ada, final):
    def body(s_ref, *refs):
        ins, outs = refs[:9], refs[9:]
        tot = s_ref[0]
        for k in range(1, 8):
            tot = tot + s_ref[k]
        g_norm = tot[0:2, :]
        g_final = tot[2:3, :]
        g_b = jnp.concatenate(
            [jnp.concatenate([tot[3 + 3 * l + t:4 + 3 * l + t, :] for t in range(3)], axis=1) for l in range(DEPTH)],
            axis=0)
        for p, g in enumerate((g_norm, g_b, g_final)):
            w_ref, m_ref, v_ref = ins[3 * p:3 * p + 3]
            d, m2, v2 = _adamw(w_ref[...], g, m_ref[...], v_ref[...])
            for o_ref, val in zip(outs[4 * p:4 * p + 4], (g, d, m2, v2)):
                o_ref[...] = val
        loss = (0.5 / D_MODEL) * jnp.sum(tot[9:10, :], axis=1, keepdims=True)
        outs[12][...] = jnp.broadcast_to(loss, (8, LANES))

    shapes = []
    for w, _, _ in (norm, b_ada, final):
        shapes += [jax.ShapeDtypeStruct(w.shape, F32)] * 4
    shapes.append(jax.ShapeDtypeStruct((8, LANES), F32))
    return pl.pallas_call(body, name="small_update", out_shape=shapes)(stats_all, *norm, *b_ada, *final)


def kernel(x, c, norm_g, w_ada, b_ada, w_in, w_out, final_g, loss_target, m_norm_g, m_w_ada, m_b_ada, m_w_in, m_w_out, m_final_g, v_norm_g, v_w_ada, v_b_ada, v_w_in, v_w_out, v_final_g):
    S, D = x.shape[1], x.shape[2]
    mc = lax.axis_index("c")
    out_rows = D // N_SHARD

    def my_half(a, rows):
        return lax.dynamic_slice_in_dim(a, mc * rows, rows, axis=0)

    assert DEPTH == 2
    win = [my_half(w_in[l], D // 2).astype(BF16) for l in range(DEPTH)]
    wout = [my_half(w_out[l], out_rows // 2).astype(BF16) for l in range(DEPTH)]
    rest = jnp.concatenate([wout[0], win[1], wout[1]], axis=0)

    def unpack(wall):
        wall = wall.reshape(N_SHARD, 2, rest.shape[0], SHARD_W)
        a, b = out_rows // 2, out_rows // 2 + D // 2
        return wall[:, :, :a].reshape(D, D), (wall[:, :, a:b].reshape(N_SHARD, D, SHARD_W), wall[:, :, b:].reshape(D, D))

    vecs, c_all, w3_first = prologue(jnp.broadcast_to(c, (8, D)), w_ada, b_ada, norm_g, win[0])
    c_all, w3_first = c_all[:, 0, :], w3_first.reshape(N_SHARD, D, SHARD_W)

    tabs = (*rope_tables(S), ret_log_gamma())
    saved = [None] * DEPTH
    h, saved[0], wall = layer_fwd(x[0], vecs[0], w3_first, lambda g: unpack(g)[0], tabs, rest)
    weights = [(w3_first, unpack(wall)[0]), unpack(wall)[1]]
    head = (jnp.broadcast_to(final_g[None, :], (8, D)), loss_target[0])
    (dx, st_loss), saved[1], _ = layer_fwd(h, vecs[1], *weights[1], tabs, head=head)

    dmod, dnorm, grads = [None] * DEPTH, [None] * DEPTH, None
    for l in reversed(range(DEPTH)):
        dx, dmod[l], dnorm[l], grads = layer_bwd(dx, saved[l], vecs[l], *weights[l], tabs, grads)

    p_own = sum_slots(*grads)
    p_sib = swap_sibling(p_own)
    res_in = adam_slab(p_own, p_sib, w_in, m_w_in, v_w_in, 0, "adam_w_in")
    res_out = adam_slab(p_own, p_sib, w_out, m_w_out, v_w_out, DEPTH * D, "adam_w_out", tr=128)

    stats = jnp.concatenate(dnorm + [st_loss[0:1]] + dmod + [st_loss[1:2], jnp.zeros((STAT_ROWS - 10, D), F32)], axis=0)
    stats_all = allgather8(stats, "gather_stats")
    dmods = stats_all[:, 3:9, :].reshape(8, DEPTH, 3 * D).transpose(1, 0, 2)
    res_ada = ada_update(dmods, c_all.T, w_ada, m_w_ada, v_w_ada)
    small = small_update(stats_all, (norm_g, m_norm_g, v_norm_g), (b_ada, m_b_ada, v_b_ada),
                         (final_g[None, :], m_final_g[None, :], v_final_g[None, :]))
    res_norm, res_b, res_final = small[0:4], small[4:8], [a[0] for a in small[8:12]]
    loss = small[12][0, 0]

    by_kind = [res_norm, res_ada, res_b, res_in, res_out, res_final]
    outs = [loss, dx[None]]
    for kind in range(4):
        outs += [r[kind] for r in by_kind]
    return tuple(outs)
```

```python
import functools

import numpy as np
import jax
import jax.numpy as jnp
from jax import lax
from jax.experimental import pallas as pl
from jax.experimental.pallas import tpu as pltpu

F32, BF16 = jnp.float32, jnp.bfloat16
MESH = pl.DeviceIdType.MESH

D_MODEL = 1024
DEPTH = 2
SHARD_W = 1024
N_SHARD = 4
GROUP_W = 512
LANES = 128
SB_HEAD_DIM = 64
RET_HEAD_DIM = 128
CHUNK = 64
ROPE_BASE = 10000.0
EPS = 1e-6
SQ_SCALE = SB_HEAD_DIM ** -0.5
RK_SCALE = RET_HEAD_DIM ** -0.5
SB_T = 1024
SB_CHAINS = 16
SB_NB = 4
RET_T = 256
EXP_ZERO = -104.0
VMEM_LIMIT_BYTES = 56 * 2 ** 20

ADAM_LR, ADAM_B1, ADAM_B2, ADAM_EPS, ADAM_WD, ADAM_STEP = 0.001, 0.9, 0.999, 1e-08, 0.01, 10


def _cp(*sem):
    return pltpu.CompilerParams(dimension_semantics=sem, vmem_limit_bytes=VMEM_LIMIT_BYTES)


def _dot(a, b):
    return lax.dot_general(a, b, (((1,), (0,)), ((), ())), preferred_element_type=F32)


def _dot_nt(a, b):
    return lax.dot_general(a, b, (((1,), (1,)), ((), ())), preferred_element_type=F32)


def _dot_tn(a, b):
    return lax.dot_general(a, b, (((0,), (0,)), ((), ())), preferred_element_type=F32)


def _running_sum(a, tri):
    return _dot(a.astype(BF16), tri)


def _sigmoid(x):
    return 1.0 / (1.0 + jnp.exp(-x))


def _rowsum(a):
    return jnp.sum(a, axis=1, keepdims=True)


def _rowmean(a):
    return jnp.mean(a, axis=1, keepdims=True)


def inproj_fwd(x, vecs, w3, tm=512):
    S, D = x.shape

    def body(x_ref, v_ref, w_ref, ret_ref, sg_ref, h_ref, sb_ref):
        xv = x_ref[...]
        r = lax.rsqrt(_rowmean(xv * xv) + EPS)
        h = xv * r * v_ref[3:4, :] * (1.0 + v_ref[1:2, :]) + v_ref[0:1, :]
        hb = h.astype(BF16)
        h_ref[...] = hb
        for s in range(N_SHARD):
            p = _dot(hb, w_ref[s])
            if s < 2:
                ret_ref[:, s * SHARD_W:(s + 1) * SHARD_W] = p
            if s == 2:
                sb_ref[:, 0:GROUP_W] = (p[:, 0:GROUP_W] * SQ_SCALE).astype(BF16)
                sb_ref[:, GROUP_W:SHARD_W] = p[:, GROUP_W:].astype(BF16)
            if s == 3:
                sb_ref[:, SHARD_W:SHARD_W + GROUP_W] = p[:, 0:GROUP_W].astype(BF16)
                sg_ref[...] = p[:, GROUP_W:]

    row = lambda w: pl.BlockSpec((tm, w), lambda i: (i, 0))
    return pl.pallas_call(
        body, name="inproj_fwd", grid=(S // tm,),
        in_specs=[row(D), pl.BlockSpec((8, D), lambda i: (0, 0)),
                  pl.BlockSpec((N_SHARD, D, SHARD_W), lambda i: (0, 0, 0))],
        out_specs=[row(2 * SHARD_W), row(GROUP_W), row(D), row(3 * GROUP_W)],
        out_shape=[jax.ShapeDtypeStruct((S, 2 * SHARD_W), F32), jax.ShapeDtypeStruct((S, GROUP_W), F32),
                   jax.ShapeDtypeStruct((S, D), BF16), jax.ShapeDtypeStruct((S, 3 * GROUP_W), BF16)],
        compiler_params=_cp("arbitrary"),
    )(x, vecs, w3)


def _sb_logits(qh, k2, keep):
    z = _dot_nt(qh, k2)
    sp = jnp.log(1.0 + jnp.exp(-jnp.abs(z)))
    lb = jnp.minimum(z, 0.0) - sp
    lk = lb - z
    if keep is not None:
        lk = jnp.where(keep, lk, 0.0)
    return lb, lk


class _sb_chains:
    def __init__(self, i, q2, do_b=None):
        t = self.t = SB_T // SB_CHAINS
        self.C = range(SB_CHAINS)
        r = lax.broadcasted_iota(jnp.int32, (SB_NB * t, SB_NB * t), 0)
        c = lax.broadcasted_iota(jnp.int32, (SB_NB * t, SB_NB * t), 1)
        self.later_all = jnp.where(r > c, 1.0, 0.0).astype(BF16)
        self.earlier_all = jnp.where(r < c, 1.0, 0.0).astype(BF16)
        self.later, self.earlier = self.later_all[:t, :t], self.earlier_all[:t, :t]
        self.head0 = lax.broadcasted_iota(jnp.int32, (1, LANES), 1) < SB_HEAD_DIM
        row = lax.broadcasted_iota(jnp.int32, (2 * t, SB_NB * t), 0) & (t - 1)
        col = lax.broadcasted_iota(jnp.int32, (2 * t, SB_NB * t), 1)
        qt = [SB_CHAINS * i + cc for cc in self.C]
        self.first = [jnp.maximum(qt[cc] - (SB_NB - 1), 0) for cc in self.C]
        self.keep = [self.first[cc] * t + col < qt[cc] * t + row for cc in self.C]
        self.qs = [self._stack(q2[cc * t:(cc + 1) * t]) for cc in self.C]
        if do_b is not None:
            self.dos = [self._stack(do_b[cc * t:(cc + 1) * t]) for cc in self.C]

    def _stack(self, a):
        zero = jnp.zeros_like(a)
        return jnp.concatenate([jnp.where(self.head0, a, zero), jnp.where(self.head0, zero, a)], axis=0)

    def rows(self, ref, j, n):
        return ref[pl.ds(pl.multiple_of(j * self.t, self.t), n * self.t), :]

    def suffix(self, lk):
        return _running_sum(lk, self.later_all), _rowsum(lk)

    def prefix(self, g, G0):
        return _running_sum(g, self.earlier_all) + G0


def sb_fwd(sb, sg, gather=None):
    S = sb.shape[0]
    T = SB_T
    nq = S // T
    carried = [] if gather is None else [gather]

    def body(*refs):
        (q_ref, k_ref, v_ref, sg_ref), refs = refs[:4], refs[4:]
        p, i = pl.program_id(0), pl.program_id(1)
        if carried:
            x_ref, y_ref, o_ref, end_ref, out_ref, send_sems, recv_sems, local_sem = refs
            start, forward, finish = _gather_ops(x_ref, out_ref, send_sems, recv_sems, local_sem)
            pl.when(jnp.logical_and(p == 0, i == 0))(start)
            pl.when(jnp.logical_and(p == 3, i == 0))(forward)
        else:
            y_ref, o_ref, end_ref = refs
        ch = _sb_chains(i, q_ref[...])
        later, head0 = ch.later, ch.head0
        lbk = [_sb_logits(ch.qs[c], ch.rows(k_ref, ch.first[c], SB_NB), ch.keep[c]) for c in ch.C]
        suffix, R = zip(*[ch.suffix(lbk[c][1]) for c in ch.C])
        aa = [jnp.where(ch.keep[c], jnp.exp(lbk[c][0] + suffix[c]), 0.0) for c in ch.C]
        acc = [_dot(aa[c].astype(BF16), ch.rows(v_ref, ch.first[c], SB_NB)) for c in ch.C]

        nc = len(ch.C)

        def alive(n, Rs):
            m = None
            for c in ch.C:
                rc = jnp.where(ch.first[c] - n > 0, Rs[c], EXP_ZERO)
                m = rc if m is None else jnp.maximum(m, rc)
            return jnp.max(m)

        def cond(st):
            return st[-1] > EXP_ZERO

        def step(st):
            n, accs, Rs = st[0], list(st[1:1 + nc]), list(st[1 + nc:1 + 2 * nc])
            for c in ch.C:
                j = ch.first[c] - 1 - n
                jc = jnp.maximum(j, 0)
                lb, lk = _sb_logits(ch.qs[c], ch.rows(k_ref, jc, 1), None)
                a = jnp.exp(lb + _running_sum(lk, later) + Rs[c])
                cx = _dot(a.astype(BF16), ch.rows(v_ref, jc, 1))
                accs[c] = jnp.where(j >= 0, accs[c] + cx, accs[c])
                Rs[c] = jnp.where(j >= 0, Rs[c] + _rowsum(lk), Rs[c])
            return (n + 1, *accs, *Rs, alive(n + 1, Rs))

        st = lax.while_loop(cond, step, (jnp.int32(0), *acc, *R, alive(0, R)))
        n_end, acc, R = st[0], st[1:1 + nc], st[1 + nc:1 + 2 * nc]
        outs = []
        for c in ch.C:
            base = c * (2 * ch.t + 8)
            end_ref[0, 0, base:base + 2 * ch.t, :] = jnp.broadcast_to(R[c], (2 * ch.t, 8))
            end_ref[0, 0, base + 2 * ch.t:base + 2 * ch.t + 8, :] = jnp.full((8, 8), n_end.astype(F32))
            outs.append(jnp.where(head0, acc[c][:ch.t], acc[c][ch.t:]))
        o = jnp.concatenate(outs, axis=0)
        o_ref[...] = o
        sg = sg_ref[...]
        y_ref[...] = (o * (sg * _sigmoid(sg))).astype(BF16)
        if carried:
            pl.when(jnp.logical_and(p == 3, i == nq - 1))(finish)

    return pl.pallas_call(
        body, name="sb_fwd", grid=(4, nq),
        in_specs=[pl.BlockSpec((T, LANES), lambda p, i: (i, p)),
                  pl.BlockSpec((S, LANES), lambda p, i: (0, 4 + p)),
                  pl.BlockSpec((S, LANES), lambda p, i: (0, 8 + p)),
                  pl.BlockSpec((T, LANES), lambda p, i: (i, p))] + [_ANY for _ in carried],
        out_specs=[pl.BlockSpec((T, LANES), lambda p, i: (i, p)),
                   pl.BlockSpec((T, LANES), lambda p, i: (i, p)),
                   pl.BlockSpec((1, 1, SB_CHAINS * (2 * T // SB_CHAINS + 8), 8), lambda p, i: (p, i, 0, 0))] + [_ANY for _ in carried],
        out_shape=[jax.ShapeDtypeStruct((S, GROUP_W), BF16),
                   jax.ShapeDtypeStruct((S, GROUP_W), F32),
                   jax.ShapeDtypeStruct((4, nq, SB_CHAINS * (2 * T // SB_CHAINS + 8), 8), F32)]
        + [jax.ShapeDtypeStruct((8,) + a.shape, a.dtype) for a in carried],
        scratch_shapes=_GATHER_SCRATCH if carried else [],
        compiler_params=_cp("arbitrary", "arbitrary"),
    )(sb, sb, sb, sg, *carried)


def sb_bwd(sb, sg, o, sb_end, dycat, ship=None):
    S = sb.shape[0]
    T = SB_T
    nq = S // T
    ex = _Exchange(ship)

    def body(*refs):
        (q_ref, k_ref, v_ref, sg_ref, o_ref, dy_ref, end_ref), refs = refs[:7], refs[7:]
        ship_refs, (dq_ref, dk_ref, dv_ref, dsg_ref), refs = refs[:ex.n_in], refs[ex.n_in:ex.n_in + 4], refs[ex.n_in + 4:]
        recv, (dk_acc, dv_acc), sems = refs[:ex.n_out], refs[ex.n_out:ex.n_out + 2], refs[ex.n_out + 2:]
        start, finish = ex.ops(ship_refs, recv + sems)
        p, i = pl.program_id(0), pl.program_id(1)
        pl.when(jnp.logical_and(p == 0, i == 0))(start)

        @pl.when(i == 0)
        def _():
            dk_acc[...] = jnp.zeros_like(dk_acc)
            dv_acc[...] = jnp.zeros_like(dv_acc)

        sg = sg_ref[...]
        sig = _sigmoid(sg)
        dy = dy_ref[...]
        dsg_ref[...] = (dy * o_ref[...] * (sig * (1.0 + sg * (1.0 - sig)))).astype(BF16)
        do_b = (dy * (sg * sig)).astype(BF16)
        ch = _sb_chains(i, q_ref[...], do_b)
        later, earlier, head0, t = ch.later, ch.earlier, ch.head0, ch.t
        end = end_ref[0, 0]

        def grads(c, j, n, a, lb, g, G, keep):
            dz = g - jnp.exp(lb) * (g + G)
            if keep is not None:
                dz = jnp.where(keep, dz, 0.0)
            dzb = dz.astype(BF16)
            rows = pl.ds(pl.multiple_of(j * t, t), n * t)
            dk_acc[rows, :] += _dot_tn(dzb, ch.qs[c])
            dv_acc[rows, :] += _dot_tn(a.astype(BF16), ch.dos[c])
            return _dot(dzb, ch.rows(k_ref, j, n))

        nc = len(ch.C)
        n_end = jnp.max(end[2 * t:2 * t + 8, :]).astype(jnp.int32)

        def sweep(m, st):
            dqs, G0s, lefts = list(st[:nc]), list(st[nc:2 * nc]), list(st[2 * nc:])
            for c in ch.C:
                j = ch.first[c] - n_end + m
                jc = jnp.maximum(j, 0)
                lb, lk = _sb_logits(ch.qs[c], ch.rows(k_ref, jc, 1), None)
                stick = lefts[c] - _rowsum(lk)
                a = jnp.where(j >= 0, jnp.exp(lb + _running_sum(lk, later) + stick), 0.0)
                g = a * _dot_nt(ch.dos[c], ch.rows(v_ref, jc, 1))
                G = _running_sum(g, earlier) + G0s[c]
                dqs[c] = dqs[c] + grads(c, jc, 1, a, lb, jnp.where(j >= 0, g, 0.0), jnp.where(j >= 0, G, 0.0), None)
                G0s[c] = G0s[c] + _rowsum(g)
                lefts[c] = jnp.where(j >= 0, stick, lefts[c])
            return (*dqs, *G0s, *lefts)

        lefts = [end[c * (2 * t + 8):c * (2 * t + 8) + 2 * t, 0:1] for c in ch.C]
        st = lax.fori_loop(0, n_end, sweep, (*[jnp.zeros((2 * t, LANES), F32)] * nc,
                                             *[jnp.zeros((2 * t, 1), F32)] * nc, *lefts))
        dq, G0 = st[:nc], st[nc:2 * nc]

        lbk = [_sb_logits(ch.qs[c], ch.rows(k_ref, ch.first[c], SB_NB), ch.keep[c]) for c in ch.C]
        suffix = [ch.suffix(lbk[c][1])[0] for c in ch.C]
        aa = [jnp.where(ch.keep[c], jnp.exp(lbk[c][0] + suffix[c]), 0.0) for c in ch.C]
        g = [aa[c] * _dot_nt(ch.dos[c], ch.rows(v_ref, ch.first[c], SB_NB)) for c in ch.C]
        G = [ch.prefix(g[c], G0[c]) for c in ch.C]
        for c in ch.C:
            dqc = dq[c] + grads(c, ch.first[c], SB_NB, aa[c], lbk[c][0], g[c], G[c], ch.keep[c])
            dq_ref[c * t:(c + 1) * t, :] = (jnp.where(head0, dqc[:t], dqc[t:]) * SQ_SCALE).astype(BF16)

        @pl.when(i == nq - 1)
        def _():
            dk_ref[...] = dk_acc[...].astype(BF16)
            dv_ref[...] = dv_acc[...].astype(BF16)

        pl.when(jnp.logical_and(p == 3, i == nq - 1))(finish)

    tile_spec = lambda c0: pl.BlockSpec((T, LANES), lambda p, i: (i, c0 + p))
    head_spec = lambda c0: pl.BlockSpec((S, LANES), lambda p, i: (0, c0 + p))
    return pl.pallas_call(
        body, name="sb_bwd", grid=(4, nq),
        in_specs=[tile_spec(0), head_spec(4), head_spec(8), tile_spec(0), tile_spec(0), tile_spec(4),
                  pl.BlockSpec((1, 1, SB_CHAINS * (2 * T // SB_CHAINS + 8), 8), lambda p, i: (p, i, 0, 0))] + ex.in_specs,
        out_specs=[tile_spec(0), head_spec(0), head_spec(0), tile_spec(0)] + ex.out_specs,
        out_shape=[jax.ShapeDtypeStruct((S, GROUP_W), BF16)] * 4 + ex.out_shape,
        scratch_shapes=[pltpu.VMEM((S, LANES), F32), pltpu.VMEM((S, LANES), F32)] + ex.scratch,
        compiler_params=_cp("arbitrary", "arbitrary"),
    )(sb, sb, sb, sg, o, dycat, sb_end, *ex.ship)


def rope_tables(S):
    half = RET_HEAD_DIM // 2
    inv = ROPE_BASE ** (-jnp.arange(half, dtype=F32) / half)
    ang = jnp.arange(S, dtype=F32)[:, None] * inv[None, :]
    cos, sin = jnp.cos(ang), jnp.sin(ang)
    return jnp.concatenate([cos, cos], axis=1), jnp.concatenate([-sin, sin], axis=1)


def ret_log_gamma():
    return jnp.log1p(-(2.0 ** (-5.0 - jnp.arange(4, dtype=F32))))


def _swap_halves(a):
    return pltpu.roll(a, RET_HEAD_DIM // 2, axis=1)


def _ret_decay_mask(lg):
    n = lax.broadcasted_iota(jnp.int32, (RET_T, RET_T), 0)
    m = lax.broadcasted_iota(jnp.int32, (RET_T, RET_T), 1)
    dist = jnp.abs(n - m).astype(F32)
    return jnp.where((m // CHUNK) <= (n // CHUNK), jnp.exp(lg * dist), 0.0)


def _ret_block(lg, rq, rk, rv, cosf, sinf, dm):
    q = rq * cosf + _swap_halves(rq) * sinf
    k = (rk * cosf + _swap_halves(rk) * sinf) * RK_SCALE
    qb, kb, vb = q.astype(BF16), k.astype(BF16), rv.astype(BF16)
    sc = _dot_nt(qb, kb) * dm
    nloc = lax.broadcasted_iota(jnp.int32, (RET_T, 1), 0).astype(F32)
    qdec = jnp.exp(lg * (nloc + 1.0))
    kdec = jnp.exp(lg * (RET_T - 1.0 - nloc))
    block_dec = jnp.exp(jnp.full((1, LANES), lg * RET_T, F32))
    return q, k, qb, kb, vb, sc, qdec, kdec, block_dec


RET_RB = 2


def _ret_specs(S, rb):
    group = lambda c0: pl.BlockSpec((RET_RB * RET_T, GROUP_W), lambda s: (rb(s), c0))
    return group, pl.BlockSpec((RET_RB * RET_T, LANES), lambda s: (rb(s), 0))


def _ret_chains():
    chains = [(h, b) for b in range(RET_RB) for h in range(4)]
    rows = lambda c: (slice(c[1] * RET_T, (c[1] + 1) * RET_T), slice(c[0] * LANES, (c[0] + 1) * LANES))
    tab = lambda ref, c: ref[c[1] * RET_T:(c[1] + 1) * RET_T, :]
    return chains, rows, tab


def _ret_blocks(chains, rows, lg_ref, rq_ref, rk_ref, rv_ref, cosf, sinf, dm_ref):
    blk = {c: _ret_block(lg_ref[c[0]], rq_ref[rows(c)], rk_ref[rows(c)], rv_ref[rows(c)],
                         cosf[c], sinf[c], dm_ref[c[0]]) for c in chains}
    return ({c: blk[c][n] for c in chains} for n in range(9))


def ret_fwd(proj, cosf, sinf, lgam):
    S = proj.shape[0]
    nb = S // RET_T
    group, row_tab = _ret_specs(S, lambda s: s)

    def body(lg_ref, rq_ref, rk_ref, rv_ref, rg_ref, cos_ref, sin_ref, y_ref, o_ref, st_out, st_ref, dm_ref):
        @pl.when(pl.program_id(0) == 0)
        def _():
            st_ref[...] = jnp.zeros_like(st_ref)
            for h in range(4):
                dm_ref[h] = _ret_decay_mask(lg_ref[h])

        chains, rows, tab = _ret_chains()
        cosf, sinf = {c: tab(cos_ref, c) for c in chains}, {c: tab(sin_ref, c) for c in chains}
        q, k, qb, kb, vb, sc, qdec, kdec, block_dec = _ret_blocks(
            chains, rows, lg_ref, rq_ref, rk_ref, rv_ref, cosf, sinf, dm_ref)
        kv = {c: _dot_tn((k[c] * kdec[c]).astype(BF16), vb[c]) for c in chains}
        st = {(h, 0): st_ref[h] for h in range(4)}
        for b in range(RET_RB):
            for h in range(4):
                st[(h, b + 1)] = st[(h, b)] * block_dec[(h, b)] + kv[(h, b)]
        for h, b in chains:
            st_out[h, b] = st[(h, b)]
        for h in range(4):
            st_ref[h] = st[(h, RET_RB)]
        o = {c: _dot(sc[c].astype(BF16), vb[c]) + _dot(qb[c], st[c].astype(BF16)) * qdec[c] for c in chains}
        for c in chains:
            o_ref[rows(c)] = o[c]
        cen = {c: o[c] - _rowmean(o[c]) for c in chains}
        on = {c: cen[c] * lax.rsqrt(_rowmean(cen[c] * cen[c]) + EPS) for c in chains}
        rg = {c: rg_ref[rows(c)] for c in chains}
        for c in chains:
            y_ref[rows(c)] = (on[c] * (rg[c] * _sigmoid(rg[c]))).astype(BF16)

    return pl.pallas_call(
        body, name="ret_fwd", grid=(nb // RET_RB,),
        in_specs=[pl.BlockSpec(memory_space=pltpu.SMEM),
                  group(0), group(1), group(2), group(3), row_tab, row_tab],
        out_specs=[group(0), group(0),
                   pl.BlockSpec((4, RET_RB, LANES, LANES), lambda s: (0, s, 0, 0))],
        out_shape=[jax.ShapeDtypeStruct((S, GROUP_W), BF16),
                   jax.ShapeDtypeStruct((S, GROUP_W), F32),
                   jax.ShapeDtypeStruct((4, nb, LANES, LANES), F32)],
        scratch_shapes=[pltpu.VMEM((4, LANES, LANES), F32), pltpu.VMEM((4, RET_T, RET_T), F32)],
        compiler_params=_cp("arbitrary"),
    )(lgam, proj, proj, proj, proj, cosf, sinf)


def ret_bwd(proj, cosf, sinf, lgam, o, states, dycat):
    S = proj.shape[0]
    nsteps = S // RET_T // RET_RB
    rev = lambda s: nsteps - 1 - s
    group, row_tab = _ret_specs(S, rev)

    def body(lg_ref, rq_ref, rk_ref, rv_ref, rg_ref, cos_ref, sin_ref, o_ref, st_in, dy_ref,
             drq_ref, drk_ref, drv_ref, drg_ref, ds_ref, dm_ref):
        @pl.when(pl.program_id(0) == 0)
        def _():
            ds_ref[...] = jnp.zeros_like(ds_ref)
            for h in range(4):
                dm_ref[h] = _ret_decay_mask(lg_ref[h])

        chains, rows, tab = _ret_chains()
        cosf, sinf = {c: tab(cos_ref, c) for c in chains}, {c: tab(sin_ref, c) for c in chains}
        dms = {c: dm_ref[c[0]] for c in chains}
        q, k, qb, kb, vb, sc, qdec, kdec, block_dec = _ret_blocks(
            chains, rows, lg_ref, rq_ref, rk_ref, rv_ref, cosf, sinf, dm_ref)
        o_v = {c: o_ref[rows(c)] for c in chains}
        cen = {c: o_v[c] - _rowmean(o_v[c]) for c in chains}
        rstd = {c: lax.rsqrt(_rowmean(cen[c] * cen[c]) + EPS) for c in chains}
        on = {c: cen[c] * rstd[c] for c in chains}
        rg = {c: rg_ref[rows(c)] for c in chains}
        sig = {c: _sigmoid(rg[c]) for c in chains}
        dy = {c: dy_ref[rows(c)] for c in chains}
        for c in chains:
            drg_ref[rows(c)] = (dy[c] * on[c] * (sig[c] * (1.0 + rg[c] * (1.0 - sig[c])))).astype(BF16)
        don = {c: dy[c] * (rg[c] * sig[c]) for c in chains}
        do = {c: rstd[c] * (don[c] - _rowmean(don[c]) - on[c] * _rowmean(don[c] * on[c])) for c in chains}
        dob = {c: do[c].astype(BF16) for c in chains}
        dsc = {c: (_dot_nt(dob[c], vb[c]) * dms[c]).astype(BF16) for c in chains}
        st_b = {c: st_in[c[0], c[1]].astype(BF16) for c in chains}
        dst = {c: _dot_tn((q[c] * qdec[c]).astype(BF16), dob[c]) for c in chains}
        dsn = {(h, RET_RB): ds_ref[h] for h in range(4)}
        for b in reversed(range(RET_RB)):
            for h in range(4):
                dsn[(h, b)] = dsn[(h, b + 1)] * block_dec[(h, b)] + dst[(h, b)]
        for h in range(4):
            ds_ref[h] = dsn[(h, 0)]
        dsn_b = {c: dsn[(c[0], c[1] + 1)].astype(BF16) for c in chains}
        dq = {c: _dot(dsc[c], kb[c]) + _dot_nt(dob[c], st_b[c]) * qdec[c] for c in chains}
        dk = {c: (_dot_tn(dsc[c], qb[c]) + _dot_nt(vb[c], dsn_b[c]) * kdec[c]) * RK_SCALE for c in chains}
        dv = {c: _dot_tn(sc[c].astype(BF16), dob[c]) + _dot((k[c] * kdec[c]).astype(BF16), dsn_b[c])
              for c in chains}
        for c in chains:
            drq_ref[rows(c)] = (dq[c] * cosf[c] + _swap_halves(dq[c] * sinf[c])).astype(BF16)
            drk_ref[rows(c)] = (dk[c] * cosf[c] + _swap_halves(dk[c] * sinf[c])).astype(BF16)
            drv_ref[rows(c)] = dv[c].astype(BF16)

    return pl.pallas_call(
        body, name="ret_bwd", grid=(nsteps,),
        in_specs=[pl.BlockSpec(memory_space=pltpu.SMEM),
                  group(0), group(1), group(2), group(3), row_tab, row_tab,
                  group(0), pl.BlockSpec((4, RET_RB, LANES, LANES), lambda s: (0, rev(s), 0, 0)),
                  group(0)],
        out_specs=[group(0)] * 4,
        out_shape=[jax.ShapeDtypeStruct((S, GROUP_W), BF16)] * 4,
        scratch_shapes=[pltpu.VMEM((4, LANES, LANES), F32), pltpu.VMEM((4, RET_T, RET_T), F32)],
        compiler_params=_cp("arbitrary"),
    )(lgam, proj, proj, proj, proj, cosf, sinf, o, states, dycat)


def outproj_fwd(x, vecs, y_ret, y_sb, w_out, head=None, tm=1024):
    S, D = x.shape
    tm = min(tm, S)
    last = list(head or ())

    def body(x_ref, v_ref, yr_ref, ys_ref, w_ref, *refs):
        y = _dot(yr_ref[...], w_ref[0:GROUP_W, :]) + _dot(ys_ref[...], w_ref[GROUP_W:, :])
        xv = x_ref[...] + v_ref[2:3, :] * y
        if not last:
            y_ref, xo_ref = refs
            y_ref[...] = y.astype(BF16)
            xo_ref[...] = xv
            return
        g_ref, t_ref, y_ref, dx_ref, st_ref = refs
        y_ref[...] = y.astype(BF16)

        @pl.when(pl.program_id(0) == 0)
        def _():
            st_ref[...] = jnp.zeros_like(st_ref)

        g = g_ref[0:1, :]
        r = lax.rsqrt(_rowmean(xv * xv) + EPS)
        xn = xv * r
        err = xn * g - t_ref[...]
        dy = err * (1.0 / D)
        dxn = dy * g
        dx_ref[...] = r * (dxn - xn * _rowmean(dxn * xn))
        st_ref[0:1, :] += jnp.sum(dy * xn, axis=0, keepdims=True)
        st_ref[1:2, :] += jnp.sum(err * err, axis=0, keepdims=True)

    row = lambda w: pl.BlockSpec((tm, w), lambda i: (i, 0))
    fixed = pl.BlockSpec((8, D), lambda i: (0, 0))
    return pl.pallas_call(
        body, name="outproj_fwd", grid=(S // tm,),
        in_specs=[row(D), fixed, row(GROUP_W), row(GROUP_W), pl.BlockSpec((D, D), lambda i: (0, 0))]
        + ([fixed, row(D)] if last else []),
        out_specs=[row(D), row(D)] + ([fixed] if last else []),
        out_shape=[jax.ShapeDtypeStruct((S, D), BF16), jax.ShapeDtypeStruct((S, D), F32)]
        + ([jax.ShapeDtypeStruct((8, D), F32)] if last else []),
        compiler_params=_cp("arbitrary"),
    )(x, vecs, y_ret, y_sb, w_out, *last)


def outproj_bwd(dx, y, vecs, y_ret, y_sb, w_out, tm=1024):
    S, D = dx.shape
    tm = min(tm, S)
    n = S // tm

    def body(dx_ref, y_ref, v_ref, yr_ref, ys_ref, w_ref, dyc_ref, dw_ref, st_ref, acc):
        i = pl.program_id(0)

        @pl.when(i == 0)
        def _():
            st_ref[...] = jnp.zeros_like(st_ref)
            acc[...] = jnp.zeros_like(acc)

        dxv = dx_ref[...]
        st_ref[0:1, :] += jnp.sum(dxv * y_ref[...].astype(F32), axis=0, keepdims=True)
        dyy = (dxv * v_ref[2:3, :]).astype(BF16)
        dyc_ref[...] = _dot_nt(dyy, w_ref[...])
        acc[0:GROUP_W, :] += _dot_tn(yr_ref[...], dyy)
        acc[GROUP_W:, :] += _dot_tn(ys_ref[...], dyy)

        @pl.when(i == n - 1)
        def _():
            dw_ref[...] = acc[...].astype(BF16)

    row = lambda w: pl.BlockSpec((tm, w), lambda i: (i, 0))
    fixed = lambda r: pl.BlockSpec((r, D), lambda i: (0, 0))
    return pl.pallas_call(
        body, name="outproj_bwd", grid=(n,),
        in_specs=[row(D), row(D), fixed(8), row(GROUP_W), row(GROUP_W), fixed(D)],
        out_specs=[row(D), fixed(D), fixed(8)],
        out_shape=[jax.ShapeDtypeStruct((S, D), F32), jax.ShapeDtypeStruct((D, D), BF16),
                   jax.ShapeDtypeStruct((8, D), F32)],
        scratch_shapes=[pltpu.VMEM((D, D), F32)],
        compiler_params=_cp("arbitrary"),
    )(dx, y, vecs, y_ret, y_sb, w_out)


def inproj_bwd_x(pieces, w3, x, vecs, dx_res, ship=None, tm=512):
    S, D = x.shape
    n = S // tm
    ex = _Exchange(ship)

    def body(*refs):
        p_refs, (w_ref, x_ref, v_ref, dr_ref), refs = refs[:8], refs[8:12], refs[12:]
        ship_refs, (dx_ref, st_ref), refs = refs[:ex.n_in], refs[ex.n_in:ex.n_in + 2], refs[ex.n_in + 2:]
        start, finish = ex.ops(ship_refs, refs)

        @pl.when(pl.program_id(0) == 0)
        def _():
            st_ref[...] = jnp.zeros_like(st_ref)
            start()

        dh = jnp.zeros((tm, D), F32)
        for k, p_ref in enumerate(p_refs):
            c0 = (k % 2) * GROUP_W
            dh = dh + _dot_nt(p_ref[...], w_ref[k // 2, :, c0:c0 + GROUP_W])
        xv = x_ref[...]
        r = lax.rsqrt(_rowmean(xv * xv) + EPS)
        xn = xv * r
        g, scale1 = v_ref[3:4, :], 1.0 + v_ref[1:2, :]
        st_ref[0:1, :] += jnp.sum(dh, axis=0, keepdims=True)
        dh_xn = dh * xn
        st_ref[1:2, :] += jnp.sum(dh_xn, axis=0, keepdims=True) * g
        st_ref[2:3, :] += jnp.sum(dh_xn, axis=0, keepdims=True) * scale1
        dxn = dh * (g * scale1)
        dx_ref[...] = r * (dxn - xn * _rowmean(dxn * xn)) + dr_ref[...]
        pl.when(pl.program_id(0) == n - 1)(finish)

    row = lambda w: pl.BlockSpec((tm, w), lambda i: (i, 0))
    return pl.pallas_call(
        body, name="inproj_bwd_x", grid=(n,),
        in_specs=[row(GROUP_W)] * 8 + [pl.BlockSpec((N_SHARD, D, SHARD_W), lambda i: (0, 0, 0)),
                                       row(D), pl.BlockSpec((8, D), lambda i: (0, 0)), row(D)] + ex.in_specs,
        out_specs=[row(D), pl.BlockSpec((8, D), lambda i: (0, 0))] + ex.out_specs,
        out_shape=[jax.ShapeDtypeStruct((S, D), F32), jax.ShapeDtypeStruct((8, D), F32)] + ex.out_shape,
        scratch_shapes=ex.scratch,
        compiler_params=_cp("arbitrary"),
    )(*pieces, w3, x, vecs, dx_res, *ex.ship)


def inproj_bwd_w(h, pieces, tm=1024):
    S, D = h.shape
    tm = min(tm, S)
    n = S // tm

    def body(*refs):
        h_ref, p_refs, dw_ref, acc = refs[0], refs[1:9], refs[9], refs[10]
        i = pl.program_id(0)

        @pl.when(i == 0)
        def _():
            acc[...] = jnp.zeros_like(acc)

        hv = h_ref[...]
        for k, p_ref in enumerate(p_refs):
            c0 = (k % 2) * GROUP_W
            acc[k // 2, :, c0:c0 + GROUP_W] += _dot_tn(hv, p_ref[...])

        @pl.when(i == n - 1)
        def _():
            dw_ref[...] = acc[...].astype(BF16)

    row = lambda w: pl.BlockSpec((tm, w), lambda i: (i, 0))
    return pl.pallas_call(
        body, name="inproj_bwd_w", grid=(n,),
        in_specs=[row(D)] + [row(GROUP_W)] * 8,
        out_specs=pl.BlockSpec((N_SHARD, D, SHARD_W), lambda i: (0, 0, 0), pipeline_mode=pl.Buffered(1)),
        out_shape=jax.ShapeDtypeStruct((N_SHARD, D, SHARD_W), BF16),
        scratch_shapes=[pltpu.VMEM((N_SHARD, D, SHARD_W), F32)],
        compiler_params=_cp("arbitrary"),
    )(h, *pieces)


def layer_fwd(x, vecs, w3, w_out, tabs, gather=None, head=None):
    cosf, sinf, lgam = tabs
    ret, sg, h, sb = inproj_fwd(x, vecs, w3)
    y_ret, o_ret, states = ret_fwd(ret, cosf, sinf, lgam)
    y_sb, o_sb, sb_end, *gathered = sb_fwd(sb, sg, gather)
    if callable(w_out):
        w_out = w_out(gathered[0])
    y, *x_next = outproj_fwd(x, vecs, y_ret, y_sb, w_out, head)
    saved = (x, ret, sg, h, sb, y_ret, o_ret, states, y_sb, o_sb, sb_end, y)
    return (x_next[0] if head is None else x_next), saved, (gathered[0] if gathered else None)


def _by_shard(dw_out):
    return dw_out.reshape(N_SHARD, D_MODEL // N_SHARD, D_MODEL)


def layer_bwd(dx, saved, vecs, w3, w_out, tabs, later_grads=None):
    cosf, sinf, lgam = tabs
    x, ret, sg, h, sb, y_ret, o_ret, states, y_sb, o_sb, sb_end, y = saved
    dycat, dw_out, st_o = outproj_bwd(dx, y, vecs, y_ret, y_sb, w_out)
    dw_out = _by_shard(dw_out)
    ship = None if later_grads is None else (later_grads[0], dw_out, later_grads[1])
    *d_sb, = sb_bwd(sb, sg, o_sb, sb_end, dycat, ship)
    d_ret = ret_bwd(ret, cosf, sinf, lgam, o_ret, states, dycat)
    pieces = list(d_ret) + d_sb[:4]
    dw_in = inproj_bwd_w(h, pieces)
    dx, st_i, *recv_in = inproj_bwd_x(pieces, w3, x, vecs, dx, None if later_grads is None else (dw_in,))
    dmod = jnp.concatenate([st_i[0:2], st_o[0:1]], axis=0)
    grads = (dw_in, dw_out) if later_grads is None else (recv_in[0], d_sb[4])
    return dx, dmod, st_i[2:3], grads


def _place():
    return lax.axis_index("x"), lax.axis_index("y"), lax.axis_index("c")


def _other_chips(mx, my):
    return [(1 - mx, my), (mx, 1 - my), (1 - mx, 1 - my)]


_ANY = pl.BlockSpec(memory_space=pl.ANY)


_GATHER_SCRATCH = [pltpu.SemaphoreType.DMA((7,)), pltpu.SemaphoreType.DMA((7,)), pltpu.SemaphoreType.DMA(())]


def _gather_ops(x_ref, out_ref, send_sems, recv_sems, local_sem):
    mx, my, mc = _place()
    me, sibling = (mx, my, mc), (mx, my, 1 - mc)
    chips = _other_chips(mx, my)

    def slot(px, py, pc):
        return out_ref.at[4 * px + 2 * py + pc]

    def copy(k, block, to, src=None):
        return pltpu.make_async_remote_copy(
            src_ref=slot(*block) if src is None else src, dst_ref=slot(*block),
            send_sem=send_sems.at[k], recv_sem=recv_sems.at[k], device_id=to, device_id_type=MESH)

    mine = pltpu.make_async_copy(x_ref, slot(*me), local_sem)
    first = [copy(0, me, sibling, src=x_ref)]
    first += [copy(1 + j, me, (*chip, mc), src=x_ref) for j, chip in enumerate(chips)]
    passed = [copy(4 + j, (*chip, mc), sibling) for j, chip in enumerate(chips)]

    def start():
        mine.start()
        for cp in first:
            cp.start()

    def forward():
        for j, chip in enumerate(chips):
            copy(1 + j, (*chip, mc), me).wait_recv()
            passed[j].start()

    def finish():
        copy(0, sibling, me).wait_recv()
        for j, chip in enumerate(chips):
            copy(4 + j, (*chip, 1 - mc), me).wait_recv()
        for cp in first + passed:
            cp.wait_send()
        mine.wait()

    return start, forward, finish


def allgather8(x, name):
    def body(x_ref, out_ref, send_sems, recv_sems, local_sem):
        for step in _gather_ops(x_ref, out_ref, send_sems, recv_sems, local_sem):
            step()

    return pl.pallas_call(
        body, name=name, out_shape=jax.ShapeDtypeStruct((8,) + x.shape, x.dtype),
        in_specs=[_ANY], out_specs=_ANY, scratch_shapes=_GATHER_SCRATCH,
    )(x)


class _Exchange:
    def __init__(self, ship):
        self.ship = list(ship or ())
        self.n_in = len(self.ship)
        self.n_out = 1 if self.ship else 0
        self.rows = [a.shape[1] for a in self.ship]
        self.in_specs = [_ANY] * self.n_in
        self.out_specs = [_ANY] * self.n_out
        self.out_shape = [jax.ShapeDtypeStruct((N_SHARD, sum(self.rows), SHARD_W), BF16)] * self.n_out
        sem = pltpu.SemaphoreType.DMA
        self.scratch = [sem((3,)), sem((3,)), sem(())] * self.n_out

    def ops(self, ship_refs, tail):
        if not self.ship:
            return (lambda: None), (lambda: None)
        recv, send_sems, recv_sems, local_sem = tail
        mx, my, mc = _place()
        my_chip = 2 * mx + my
        chips = _other_chips(mx, my)

        def pieces(s):
            firsts = np.cumsum([0] + self.rows[:-1])
            return [(ref.at[s], int(r0), n) for ref, r0, n in zip(ship_refs, firsts, self.rows)]

        def start():
            for src, r0, n in pieces(my_chip):
                pltpu.make_async_copy(src, recv.at[my_chip, pl.ds(r0, n)], local_sem).start()
            for j, (px, py) in enumerate(chips):
                for src, r0, n in pieces(2 * px + py):
                    pltpu.make_async_remote_copy(
                        src_ref=src, dst_ref=recv.at[my_chip, pl.ds(r0, n)],
                        send_sem=send_sems.at[j], recv_sem=recv_sems.at[j],
                        device_id=(px, py, mc), device_id_type=MESH).start()

        def finish():
            for j, (px, py) in enumerate(chips):
                whole = recv.at[2 * px + py]
                both = pltpu.make_async_remote_copy(
                    src_ref=whole, dst_ref=whole, send_sem=send_sems.at[j], recv_sem=recv_sems.at[j],
                    device_id=(px, py, mc), device_id_type=MESH)
                both.wait_recv()
                both.wait_send()
            pltpu.make_async_copy(recv.at[my_chip], recv.at[my_chip], local_sem).wait()

        return start, finish


def sum_and_swap(recv_a, recv_b, tr=256):
    n, rows_a, cols = recv_a.shape
    na, nb = rows_a // tr, recv_b.shape[1] // tr
    nt = na + nb

    def body(a_ref, b_ref, own_ref, sib_ref, slots, send_sems, recv_sem):
        i = pl.program_id(0)
        mx, my, mc = _place()
        slot = i % 2

        def push(k, tile):
            return pltpu.make_async_remote_copy(
                src_ref=slots.at[k], dst_ref=sib_ref.at[pl.ds(pl.multiple_of(tile * tr, tr), tr)],
                send_sem=send_sems.at[k], recv_sem=recv_sem, device_id=(mx, my, 1 - mc), device_id_type=MESH)

        pl.when(i >= 2)(lambda: push(slot, i - 2).wait_send())

        def total(r_ref):
            acc = r_ref[0].astype(F32)
            for k in range(1, n):
                acc = acc + r_ref[k].astype(F32)
            own_ref[...] = acc
            slots[slot] = acc

        pl.when(i < na)(lambda: total(a_ref))
        pl.when(i >= na)(lambda: total(b_ref))
        push(slot, i).start()

        @pl.when(i == nt - 1)
        def _():
            push(1 - slot, i - 1).wait_send()
            push(slot, i).wait_send()
            pltpu.make_async_remote_copy(src_ref=sib_ref, dst_ref=sib_ref, send_sem=send_sems.at[0], recv_sem=recv_sem,
                                         device_id=(mx, my, 1 - mc), device_id_type=MESH).wait_recv()

    return pl.pallas_call(
        body, name="sum_and_swap", grid=(nt,),
        in_specs=[pl.BlockSpec((n, tr, cols), lambda i: (0, jnp.minimum(i, na - 1), 0)),
                  pl.BlockSpec((n, tr, cols), lambda i: (0, jnp.maximum(i - na, 0), 0))],
        out_specs=[pl.BlockSpec((tr, cols), lambda i: (i, 0)), _ANY],
        out_shape=[jax.ShapeDtypeStruct((nt * tr, cols), F32)] * 2,
        scratch_shapes=[pltpu.VMEM((2, tr, cols), F32), pltpu.SemaphoreType.DMA((2,)), pltpu.SemaphoreType.DMA(())],
        compiler_params=_cp("arbitrary"),
    )(recv_a, recv_b)


def _adamw(w, g, m, v):
    m = ADAM_B1 * m + (1.0 - ADAM_B1) * g
    v = ADAM_B2 * v + (1.0 - ADAM_B2) * (g * g)
    m_hat = m / (1.0 - ADAM_B1 ** ADAM_STEP)
    v_hat = v / (1.0 - ADAM_B2 ** ADAM_STEP)
    delta = -ADAM_LR * (m_hat / (jnp.sqrt(v_hat) + ADAM_EPS) + ADAM_WD * w)
    return delta, m, v


def adam_slab(p_own, p_sib, w, m, v, row0, name, tr=256):
    L, R, C = w.shape
    nr = R // tr

    def body(a_ref, b_ref, w_ref, m_ref, v_ref, g_out, d_out, m_out, v_out):
        g = a_ref[...] + b_ref[...]
        d, m2, v2 = _adamw(w_ref[0], g, m_ref[0], v_ref[0])
        g_out[0], d_out[0], m_out[0], v_out[0] = g, d, m2, v2

    slab = pl.BlockSpec((tr, C), lambda l, i: (row0 // tr + l * nr + i, 0))
    blk = pl.BlockSpec((1, tr, C), lambda l, i: (l, i, 0))
    return pl.pallas_call(
        body, name=name, grid=(L, nr),
        in_specs=[slab, slab, blk, blk, blk], out_specs=[blk] * 4,
        out_shape=[jax.ShapeDtypeStruct(w.shape, F32)] * 4,
        compiler_params=_cp("arbitrary", "arbitrary"),
    )(p_own, p_sib, w, m, v)


def prologue(c8, w_ada, b_ada, norm_g, win_first):
    L, D, W = w_ada.shape

    def body(c_ref, w_ref, b_ref, g_ref, win_ref, vecs_ref, call_ref, wall_ref, mod_ref, mall_ref, *sems):
        w_start, w_forward, w_finish = _gather_ops(win_ref, wall_ref, *sems[0:3])
        w_start()
        for step in _gather_ops(c_ref, call_ref, *sems[3:6]):
            step()
        cv = call_ref[:, 0, :]
        ca = cv * _sigmoid(cv)
        for l in range(L):
            mod_ref[l * 8:(l + 1) * 8, :] = jnp.dot(ca, w_ref[l], precision=lax.Precision.HIGHEST,
                                                    preferred_element_type=F32)
        for step in _gather_ops(mod_ref, mall_ref, *sems[6:9]):
            step()
        mx, my, mc = _place()
        me = 4 * mx + 2 * my + mc
        rowid = lax.broadcasted_iota(jnp.int32, (L * 8, 1), 0)
        vecs_ref[...] = jnp.zeros_like(vecs_ref)
        for l in range(L):
            parts = [jnp.sum(jnp.where(rowid == l * 8 + me, mall_ref[2 * s + mc], 0.0), axis=0, keepdims=True)
                     for s in range(N_SHARD)]
            mod = jnp.concatenate(parts, axis=1) + b_ref[l:l + 1, :]
            for t in range(3):
                vecs_ref[l, t:t + 1, :] = mod[:, t * D:(t + 1) * D]
            vecs_ref[l, 3:4, :] = g_ref[l:l + 1, :]
        w_forward()
        w_finish()

    vmem = pl.BlockSpec(memory_space=pltpu.VMEM)
    return pl.pallas_call(
        body, name="prologue",
        in_specs=[vmem, vmem, vmem, vmem, _ANY], out_specs=[vmem, vmem, _ANY],
        out_shape=[jax.ShapeDtypeStruct((L, 8, D), F32), jax.ShapeDtypeStruct((8, 8, D), F32),
                   jax.ShapeDtypeStruct((8,) + win_first.shape, win_first.dtype)],
        scratch_shapes=[pltpu.VMEM((L * 8, W), F32), pltpu.VMEM((8, L * 8, W), F32)] + _GATHER_SCRATCH * 3,
        compiler_params=pltpu.CompilerParams(vmem_limit_bytes=VMEM_LIMIT_BYTES),
    )(c8, w_ada, b_ada, norm_g, win_first)


def ada_update(dmods, c_t, w, m, v, tr=256):
    L, D, W = w.shape

    def body(dm_ref, c_ref, w_ref, m_ref, v_ref, g_out, d_out, m_out, v_out):
        mx, my, _ = _place()
        shard = 2 * mx + my
        dm = jnp.zeros((8, W), F32)
        for s in range(N_SHARD):
            dm = dm + jnp.where(shard == s, dm_ref[0, :, s * W:(s + 1) * W], 0.0)
        cv = c_ref[...]
        ca = cv * _sigmoid(cv)
        g = jnp.zeros((tr, W), F32)
        for b in range(8):
            g = g + ca[:, b:b + 1] * dm[b:b + 1, :]
        d, m2, v2 = _adamw(w_ref[0], g, m_ref[0], v_ref[0])
        g_out[0], d_out[0], m_out[0], v_out[0] = g, d, m2, v2

    blk = pl.BlockSpec((1, tr, W), lambda l, i: (l, i, 0))
    return pl.pallas_call(
        body, name="ada_update", grid=(L, D // tr),
        in_specs=[pl.BlockSpec((1, 8, 3 * D), lambda l, i: (l, 0, 0)), pl.BlockSpec((tr, 8), lambda l, i: (i, 0)),
                  blk, blk, blk],
        out_specs=[blk] * 4, out_shape=[jax.ShapeDtypeStruct(w.shape, F32)] * 4,
        compiler_params=_cp("arbitrary", "arbitrary"),
    )(dmods, c_t, w, m, v)


STAT_ROWS = 16


def small_update(stats_all, norm, b_ada, final):
    def body(s_ref, *refs):
        ins, outs = refs[:9], refs[9:]
        tot = s_ref[0]
        for k in range(1, 8):
            tot = tot + s_ref[k]
        g_norm = tot[0:2, :]
        g_final = tot[2:3, :]
        g_b = jnp.concatenate(
            [jnp.concatenate([tot[3 + 3 * l + t:4 + 3 * l + t, :] for t in range(3)], axis=1) for l in range(DEPTH)],
            axis=0)
        for p, g in enumerate((g_norm, g_b, g_final)):
            w_ref, m_ref, v_ref = ins[3 * p:3 * p + 3]
            d, m2, v2 = _adamw(w_ref[...], g, m_ref[...], v_ref[...])
            for o_ref, val in zip(outs[4 * p:4 * p + 4], (g, d, m2, v2)):
                o_ref[...] = val
        loss = (0.5 / D_MODEL) * jnp.sum(tot[9:10, :], axis=1, keepdims=True)
        outs[12][...] = jnp.broadcast_to(loss, (8, LANES))

    shapes = []
    for w, _, _ in (norm, b_ada, final):
        shapes += [jax.ShapeDtypeStruct(w.shape, F32)] * 4
    shapes.append(jax.ShapeDtypeStruct((8, LANES), F32))
    return pl.pallas_call(body, name="small_update", out_shape=shapes)(stats_all, *norm, *b_ada, *final)


def kernel(x, c, norm_g, w_ada, b_ada, w_in, w_out, final_g, loss_target, m_norm_g, m_w_ada, m_b_ada, m_w_in, m_w_out, m_final_g, v_norm_g, v_w_ada, v_b_ada, v_w_in, v_w_out, v_final_g):
    S, D = x.shape[1], x.shape[2]
    mc = lax.axis_index("c")
    out_rows = D // N_SHARD

    def my_half(a, rows):
        return lax.dynamic_slice_in_dim(a, mc * rows, rows, axis=0)

    assert DEPTH == 2
    win = [my_half(w_in[l], D // 2).astype(BF16) for l in range(DEPTH)]
    wout = [my_half(w_out[l], out_rows // 2).astype(BF16) for l in range(DEPTH)]
    rest = jnp.concatenate([wout[0], win[1], wout[1]], axis=0)

    def unpack(wall):
        wall = wall.reshape(N_SHARD, 2, rest.shape[0], SHARD_W)
        a, b = out_rows // 2, out_rows // 2 + D // 2
        return wall[:, :, :a].reshape(D, D), (wall[:, :, a:b].reshape(N_SHARD, D, SHARD_W), wall[:, :, b:].reshape(D, D))

    vecs, c_all, w3_first = prologue(jnp.broadcast_to(c, (8, D)), w_ada, b_ada, norm_g, win[0])
    c_all, w3_first = c_all[:, 0, :], w3_first.reshape(N_SHARD, D, SHARD_W)

    tabs = (*rope_tables(S), ret_log_gamma())
    saved = [None] * DEPTH
    h, saved[0], wall = layer_fwd(x[0], vecs[0], w3_first, lambda g: unpack(g)[0], tabs, rest)
    weights = [(w3_first, unpack(wall)[0]), unpack(wall)[1]]
    head = (jnp.broadcast_to(final_g[None, :], (8, D)), loss_target[0])
    (dx, st_loss), saved[1], _ = layer_fwd(h, vecs[1], *weights[1], tabs, head=head)

    dmod, dnorm, grads = [None] * DEPTH, [None] * DEPTH, None
    for l in reversed(range(DEPTH)):
        dx, dmod[l], dnorm[l], grads = layer_bwd(dx, saved[l], vecs[l], *weights[l], tabs, grads)

    p_own, p_sib = sum_and_swap(*grads)
    res_in = adam_slab(p_own, p_sib, w_in, m_w_in, v_w_in, 0, "adam_w_in")
    res_out = adam_slab(p_own, p_sib, w_out, m_w_out, v_w_out, DEPTH * D, "adam_w_out", tr=128)

    stats = jnp.concatenate(dnorm + [st_loss[0:1]] + dmod + [st_loss[1:2], jnp.zeros((STAT_ROWS - 10, D), F32)], axis=0)
    stats_all = allgather8(stats, "gather_stats")
    dmods = stats_all[:, 3:9, :].reshape(8, DEPTH, 3 * D).transpose(1, 0, 2)
    res_ada = ada_update(dmods, c_all.T, w_ada, m_w_ada, v_w_ada)
    small = small_update(stats_all, (norm_g, m_norm_g, v_norm_g), (b_ada, m_b_ada, v_b_ada),
                         (final_g[None, :], m_final_g[None, :], v_final_g[None, :]))
    res_norm, res_b, res_final = small[0:4], small[4:8], [a[0] for a in small[8:12]]
    loss = small[12][0, 0]

    by_kind = [res_norm, res_ada, res_b, res_in, res_out, res_final]
    outs = [loss, dx[None]]
    for kind in range(4):
        outs += [r[kind] for r in by_kind]
    return tuple(outs)
```

```python
import numpy as np
import jax
import jax.numpy as jnp
from jax import lax
from jax.experimental import pallas as pl
from jax.experimental.pallas import tpu as pltpu

F32, BF16 = jnp.float32, jnp.bfloat16
MESH = pl.DeviceIdType.MESH

D_MODEL = 1024
DEPTH = 2
SHARD_W = 1024
N_SHARD = 4
GROUP_W = 512
LANES = 128
SB_HEAD_DIM = 64
RET_HEAD_DIM = 128
CHUNK = 64
ROPE_BASE = 10000.0
EPS = 1e-6
SQ_SCALE = SB_HEAD_DIM ** -0.5
RK_SCALE = RET_HEAD_DIM ** -0.5
SB_T = 1024
SB_CHAINS = 16
SB_NB = 4
RET_T = 256
EXP_ZERO = -104.0
VMEM_LIMIT_BYTES = 56 * 2 ** 20

ADAM_LR, ADAM_B1, ADAM_B2, ADAM_EPS, ADAM_WD, ADAM_STEP = 0.001, 0.9, 0.999, 1e-08, 0.01, 10


def _cp(*sem):
    return pltpu.CompilerParams(dimension_semantics=sem, vmem_limit_bytes=VMEM_LIMIT_BYTES)


def _dot(a, b):
    return lax.dot_general(a, b, (((1,), (0,)), ((), ())), preferred_element_type=F32)


def _dot_nt(a, b):
    return lax.dot_general(a, b, (((1,), (1,)), ((), ())), preferred_element_type=F32)


def _dot_tn(a, b):
    return lax.dot_general(a, b, (((0,), (0,)), ((), ())), preferred_element_type=F32)


def _running_sum(a, tri):
    return _dot(a.astype(BF16), tri)


def _sigmoid(x):
    return 1.0 / (1.0 + jnp.exp(-x))


def _rowsum(a):
    return jnp.sum(a, axis=1, keepdims=True)


def _rowmean(a):
    return jnp.mean(a, axis=1, keepdims=True)


def inproj_fwd(x, vecs, w3, tm=512):
    S, D = x.shape

    def body(x_ref, v_ref, w_ref, ret_ref, sg_ref, h_ref, sb_ref):
        xv = x_ref[...]
        r = lax.rsqrt(_rowmean(xv * xv) + EPS)
        h = xv * r * v_ref[3:4, :] * (1.0 + v_ref[1:2, :]) + v_ref[0:1, :]
        hb = h.astype(BF16)
        h_ref[...] = hb
        for s in range(N_SHARD):
            p = _dot(hb, w_ref[s])
            if s < 2:
                ret_ref[:, s * SHARD_W:(s + 1) * SHARD_W] = p
            if s == 2:
                sb_ref[:, 0:GROUP_W] = (p[:, 0:GROUP_W] * SQ_SCALE).astype(BF16)
                sb_ref[:, GROUP_W:SHARD_W] = p[:, GROUP_W:].astype(BF16)
            if s == 3:
                sb_ref[:, SHARD_W:SHARD_W + GROUP_W] = p[:, 0:GROUP_W].astype(BF16)
                sg_ref[...] = p[:, GROUP_W:]

    row = lambda w: pl.BlockSpec((tm, w), lambda i: (i, 0))
    return pl.pallas_call(
        body, name="inproj_fwd", grid=(S // tm,),
        in_specs=[row(D), pl.BlockSpec((8, D), lambda i: (0, 0)),
                  pl.BlockSpec((N_SHARD, D, SHARD_W), lambda i: (0, 0, 0))],
        out_specs=[row(2 * SHARD_W), row(GROUP_W), row(D), row(3 * GROUP_W)],
        out_shape=[jax.ShapeDtypeStruct((S, 2 * SHARD_W), F32), jax.ShapeDtypeStruct((S, GROUP_W), F32),
                   jax.ShapeDtypeStruct((S, D), BF16), jax.ShapeDtypeStruct((S, 3 * GROUP_W), BF16)],
        compiler_params=_cp("arbitrary"),
    )(x, vecs, w3)


def _sb_logits(qh, k2, keep):
    z = _dot_nt(qh, k2)
    sp = jnp.log(1.0 + jnp.exp(-jnp.abs(z)))
    lb = jnp.minimum(z, 0.0) - sp
    lk = lb - z
    if keep is not None:
        lk = jnp.where(keep, lk, 0.0)
    return lb, lk


class _sb_chains:
    def __init__(self, i, q2, do_b=None):
        t = self.t = SB_T // SB_CHAINS
        self.C = range(SB_CHAINS)
        r = lax.broadcasted_iota(jnp.int32, (SB_NB * t, SB_NB * t), 0)
        c = lax.broadcasted_iota(jnp.int32, (SB_NB * t, SB_NB * t), 1)
        self.later_all = jnp.where(r > c, 1.0, 0.0).astype(BF16)
        self.earlier_all = jnp.where(r < c, 1.0, 0.0).astype(BF16)
        self.later, self.earlier = self.later_all[:t, :t], self.earlier_all[:t, :t]
        self.head0 = lax.broadcasted_iota(jnp.int32, (1, LANES), 1) < SB_HEAD_DIM
        row = lax.broadcasted_iota(jnp.int32, (2 * t, SB_NB * t), 0) & (t - 1)
        col = lax.broadcasted_iota(jnp.int32, (2 * t, SB_NB * t), 1)
        qt = [SB_CHAINS * i + cc for cc in self.C]
        self.first = [jnp.maximum(qt[cc] - (SB_NB - 1), 0) for cc in self.C]
        self.keep = [self.first[cc] * t + col < qt[cc] * t + row for cc in self.C]
        self.qs = [self._stack(q2[cc * t:(cc + 1) * t]) for cc in self.C]
        if do_b is not None:
            self.dos = [self._stack(do_b[cc * t:(cc + 1) * t]) for cc in self.C]

    def _stack(self, a):
        zero = jnp.zeros_like(a)
        return jnp.concatenate([jnp.where(self.head0, a, zero), jnp.where(self.head0, zero, a)], axis=0)

    def rows(self, ref, j, n):
        return ref[pl.ds(pl.multiple_of(j * self.t, self.t), n * self.t), :]

    def suffix(self, lk):
        return _running_sum(lk, self.later_all), _rowsum(lk)

    def prefix(self, g, G0):
        return _running_sum(g, self.earlier_all) + G0


def sb_fwd(sb, sg, gather=None):
    S = sb.shape[0]
    T = SB_T
    nq = S // T
    carried = list(gather or ())
    ng = len(carried)

    def body(*refs):
        (q_ref, k_ref, v_ref, sg_ref), refs = refs[:4], refs[4:]
        x_refs, (y_ref, o_ref, end_ref), out_refs, sems = refs[:ng], refs[ng:ng + 3], refs[ng + 3:2 * ng + 3], refs[2 * ng + 3:]
        p, i = pl.program_id(0), pl.program_id(1)
        gathers = [_gather_ops(x_refs[g], out_refs[g], *sems[3 * g:3 * g + 3]) for g in range(ng)]
        for start, forward, _ in gathers:
            pl.when(jnp.logical_and(p == 0, i == 0))(start)
            pl.when(jnp.logical_and(p == 3, i == 0))(forward)
        ch = _sb_chains(i, q_ref[...])
        later, head0 = ch.later, ch.head0
        lbk = [_sb_logits(ch.qs[c], ch.rows(k_ref, ch.first[c], SB_NB), ch.keep[c]) for c in ch.C]
        suffix, R = zip(*[ch.suffix(lbk[c][1]) for c in ch.C])
        aa = [jnp.where(ch.keep[c], jnp.exp(lbk[c][0] + suffix[c]), 0.0) for c in ch.C]
        acc = [_dot(aa[c].astype(BF16), ch.rows(v_ref, ch.first[c], SB_NB)) for c in ch.C]

        nc = len(ch.C)

        def alive(n, Rs):
            m = None
            for c in ch.C:
                rc = jnp.where(ch.first[c] - n > 0, Rs[c], EXP_ZERO)
                m = rc if m is None else jnp.maximum(m, rc)
            return jnp.max(m)

        def cond(st):
            return st[-1] > EXP_ZERO

        def step(st):
            n, accs, Rs = st[0], list(st[1:1 + nc]), list(st[1 + nc:1 + 2 * nc])
            for c in ch.C:
                j = ch.first[c] - 1 - n
                jc = jnp.maximum(j, 0)
                lb, lk = _sb_logits(ch.qs[c], ch.rows(k_ref, jc, 1), None)
                a = jnp.exp(lb + _running_sum(lk, later) + Rs[c])
                cx = _dot(a.astype(BF16), ch.rows(v_ref, jc, 1))
                accs[c] = jnp.where(j >= 0, accs[c] + cx, accs[c])
                Rs[c] = jnp.where(j >= 0, Rs[c] + _rowsum(lk), Rs[c])
            return (n + 1, *accs, *Rs, alive(n + 1, Rs))

        st = lax.while_loop(cond, step, (jnp.int32(0), *acc, *R, alive(0, R)))
        n_end, acc, R = st[0], st[1:1 + nc], st[1 + nc:1 + 2 * nc]
        outs = []
        for c in ch.C:
            base = c * (2 * ch.t + 8)
            end_ref[0, 0, base:base + 2 * ch.t, :] = jnp.broadcast_to(R[c], (2 * ch.t, 8))
            end_ref[0, 0, base + 2 * ch.t:base + 2 * ch.t + 8, :] = jnp.full((8, 8), n_end.astype(F32))
            outs.append(jnp.where(head0, acc[c][:ch.t], acc[c][ch.t:]))
        o = jnp.concatenate(outs, axis=0)
        o_ref[...] = o
        sg = sg_ref[...]
        y_ref[...] = (o * (sg * _sigmoid(sg))).astype(BF16)
        for _, _, finish in gathers:
            pl.when(jnp.logical_and(p == 3, i == nq - 1))(finish)

    return pl.pallas_call(
        body, name="sb_fwd", grid=(4, nq),
        in_specs=[pl.BlockSpec((T, LANES), lambda p, i: (i, p)),
                  pl.BlockSpec((S, LANES), lambda p, i: (0, 4 + p)),
                  pl.BlockSpec((S, LANES), lambda p, i: (0, 8 + p)),
                  pl.BlockSpec((T, LANES), lambda p, i: (i, p))] + [_ANY for _ in carried],
        out_specs=[pl.BlockSpec((T, LANES), lambda p, i: (i, p)),
                   pl.BlockSpec((T, LANES), lambda p, i: (i, p)),
                   pl.BlockSpec((1, 1, SB_CHAINS * (2 * T // SB_CHAINS + 8), 8), lambda p, i: (p, i, 0, 0))] + [_ANY for _ in carried],
        out_shape=[jax.ShapeDtypeStruct((S, GROUP_W), BF16),
                   jax.ShapeDtypeStruct((S, GROUP_W), F32),
                   jax.ShapeDtypeStruct((4, nq, SB_CHAINS * (2 * T // SB_CHAINS + 8), 8), F32)]
        + [jax.ShapeDtypeStruct((8,) + a.shape, a.dtype) for a in carried],
        scratch_shapes=_GATHER_SCRATCH * ng,
        compiler_params=_cp("arbitrary", "arbitrary"),
    )(sb, sb, sb, sg, *carried)


def sb_bwd(sb, sg, o, sb_end, dycat, ship=None):
    S = sb.shape[0]
    T = SB_T
    nq = S // T
    ex = _Exchange(ship)

    def body(*refs):
        (q_ref, k_ref, v_ref, sg_ref, o_ref, dy_ref, end_ref), refs = refs[:7], refs[7:]
        ship_refs, (dq_ref, dk_ref, dv_ref, dsg_ref), refs = refs[:ex.n_in], refs[ex.n_in:ex.n_in + 4], refs[ex.n_in + 4:]
        recv, (dk_acc, dv_acc), sems = refs[:ex.n_out], refs[ex.n_out:ex.n_out + 2], refs[ex.n_out + 2:]
        start, finish = ex.ops(ship_refs, recv + sems)
        p, i = pl.program_id(0), pl.program_id(1)
        pl.when(jnp.logical_and(p == 0, i == 0))(start)

        @pl.when(i == 0)
        def _():
            dk_acc[...] = jnp.zeros_like(dk_acc)
            dv_acc[...] = jnp.zeros_like(dv_acc)

        sg = sg_ref[...]
        sig = _sigmoid(sg)
        dy = dy_ref[...]
        dsg_ref[...] = (dy * o_ref[...] * (sig * (1.0 + sg * (1.0 - sig)))).astype(BF16)
        do_b = (dy * (sg * sig)).astype(BF16)
        ch = _sb_chains(i, q_ref[...], do_b)
        later, earlier, head0, t = ch.later, ch.earlier, ch.head0, ch.t
        end = end_ref[0, 0]

        def grads(c, j, n, a, lb, g, G, keep):
            dz = g - jnp.exp(lb) * (g + G)
            if keep is not None:
                dz = jnp.where(keep, dz, 0.0)
            dzb = dz.astype(BF16)
            rows = pl.ds(pl.multiple_of(j * t, t), n * t)
            dk_acc[rows, :] += _dot_tn(dzb, ch.qs[c])
            dv_acc[rows, :] += _dot_tn(a.astype(BF16), ch.dos[c])
            return _dot(dzb, ch.rows(k_ref, j, n))

        nc = len(ch.C)
        n_end = jnp.max(end[2 * t:2 * t + 8, :]).astype(jnp.int32)

        def sweep(m, st):
            dqs, G0s, lefts = list(st[:nc]), list(st[nc:2 * nc]), list(st[2 * nc:])
            for c in ch.C:
                j = ch.first[c] - n_end + m
                jc = jnp.maximum(j, 0)
                lb, lk = _sb_logits(ch.qs[c], ch.rows(k_ref, jc, 1), None)
                stick = lefts[c] - _rowsum(lk)
                a = jnp.where(j >= 0, jnp.exp(lb + _running_sum(lk, later) + stick), 0.0)
                g = a * _dot_nt(ch.dos[c], ch.rows(v_ref, jc, 1))
                G = _running_sum(g, earlier) + G0s[c]
                dqs[c] = dqs[c] + grads(c, jc, 1, a, lb, jnp.where(j >= 0, g, 0.0), jnp.where(j >= 0, G, 0.0), None)
                G0s[c] = G0s[c] + _rowsum(g)
                lefts[c] = jnp.where(j >= 0, stick, lefts[c])
            return (*dqs, *G0s, *lefts)

        lefts = [end[c * (2 * t + 8):c * (2 * t + 8) + 2 * t, 0:1] for c in ch.C]
        st = lax.fori_loop(0, n_end, sweep, (*[jnp.zeros((2 * t, LANES), F32)] * nc,
                                             *[jnp.zeros((2 * t, 1), F32)] * nc, *lefts))
        dq, G0 = st[:nc], st[nc:2 * nc]

        lbk = [_sb_logits(ch.qs[c], ch.rows(k_ref, ch.first[c], SB_NB), ch.keep[c]) for c in ch.C]
        suffix = [ch.suffix(lbk[c][1])[0] for c in ch.C]
        aa = [jnp.where(ch.keep[c], jnp.exp(lbk[c][0] + suffix[c]), 0.0) for c in ch.C]
        g = [aa[c] * _dot_nt(ch.dos[c], ch.rows(v_ref, ch.first[c], SB_NB)) for c in ch.C]
        G = [ch.prefix(g[c], G0[c]) for c in ch.C]
        for c in ch.C:
            dqc = dq[c] + grads(c, ch.first[c], SB_NB, aa[c], lbk[c][0], g[c], G[c], ch.keep[c])
            dq_ref[c * t:(c + 1) * t, :] = (jnp.where(head0, dqc[:t], dqc[t:]) * SQ_SCALE).astype(BF16)

        @pl.when(i == nq - 1)
        def _():
            dk_ref[...] = dk_acc[...].astype(BF16)
            dv_ref[...] = dv_acc[...].astype(BF16)

        pl.when(jnp.logical_and(p == 3, i == nq - 1))(finish)

    tile_spec = lambda c0: pl.BlockSpec((T, LANES), lambda p, i: (i, c0 + p))
    head_spec = lambda c0: pl.BlockSpec((S, LANES), lambda p, i: (0, c0 + p))
    return pl.pallas_call(
        body, name="sb_bwd", grid=(4, nq),
        in_specs=[tile_spec(0), head_spec(4), head_spec(8), tile_spec(0), tile_spec(0), tile_spec(4),
                  pl.BlockSpec((1, 1, SB_CHAINS * (2 * T // SB_CHAINS + 8), 8), lambda p, i: (p, i, 0, 0))] + ex.in_specs,
        out_specs=[tile_spec(0), head_spec(0), head_spec(0), tile_spec(0)] + ex.out_specs,
        out_shape=[jax.ShapeDtypeStruct((S, GROUP_W), BF16)] * 4 + ex.out_shape,
        scratch_shapes=[pltpu.VMEM((S, LANES), F32), pltpu.VMEM((S, LANES), F32)] + ex.scratch,
        compiler_params=_cp("arbitrary", "arbitrary"),
    )(sb, sb, sb, sg, o, dycat, sb_end, *ex.ship)


def rope_tables(S):
    half = RET_HEAD_DIM // 2
    inv = ROPE_BASE ** (-jnp.arange(half, dtype=F32) / half)
    ang = jnp.arange(S, dtype=F32)[:, None] * inv[None, :]
    cos, sin = jnp.cos(ang), jnp.sin(ang)
    return jnp.concatenate([cos, cos], axis=1), jnp.concatenate([-sin, sin], axis=1)


def ret_log_gamma():
    return jnp.log1p(-(2.0 ** (-5.0 - jnp.arange(4, dtype=F32))))


def _swap_halves(a):
    return pltpu.roll(a, RET_HEAD_DIM // 2, axis=1)


def _ret_decay_mask(lg):
    n = lax.broadcasted_iota(jnp.int32, (RET_T, RET_T), 0)
    m = lax.broadcasted_iota(jnp.int32, (RET_T, RET_T), 1)
    dist = jnp.abs(n - m).astype(F32)
    return jnp.where((m // CHUNK) <= (n // CHUNK), jnp.exp(lg * dist), 0.0)


def _ret_block(lg, rq, rk, rv, cosf, sinf, dm):
    q = rq * cosf + _swap_halves(rq) * sinf
    k = (rk * cosf + _swap_halves(rk) * sinf) * RK_SCALE
    qb, kb, vb = q.astype(BF16), k.astype(BF16), rv.astype(BF16)
    sc = _dot_nt(qb, kb) * dm
    nloc = lax.broadcasted_iota(jnp.int32, (RET_T, 1), 0).astype(F32)
    qdec = jnp.exp(lg * (nloc + 1.0))
    kdec = jnp.exp(lg * (RET_T - 1.0 - nloc))
    block_dec = jnp.exp(jnp.full((1, LANES), lg * RET_T, F32))
    return q, k, qb, kb, vb, sc, qdec, kdec, block_dec


RET_RB = 2


def _ret_specs(S, rb):
    group = lambda c0: pl.BlockSpec((RET_RB * RET_T, GROUP_W), lambda s: (rb(s), c0))
    return group, pl.BlockSpec((RET_RB * RET_T, LANES), lambda s: (rb(s), 0))


def _ret_chains():
    chains = [(h, b) for b in range(RET_RB) for h in range(4)]
    rows = lambda c: (slice(c[1] * RET_T, (c[1] + 1) * RET_T), slice(c[0] * LANES, (c[0] + 1) * LANES))
    tab = lambda ref, c: ref[c[1] * RET_T:(c[1] + 1) * RET_T, :]
    return chains, rows, tab


def _ret_blocks(chains, rows, lg_ref, rq_ref, rk_ref, rv_ref, cosf, sinf, dm_ref):
    blk = {c: _ret_block(lg_ref[c[0]], rq_ref[rows(c)], rk_ref[rows(c)], rv_ref[rows(c)],
                         cosf[c], sinf[c], dm_ref[c[0]]) for c in chains}
    return ({c: blk[c][n] for c in chains} for n in range(9))


def ret_fwd(proj, cosf, sinf, lgam):
    S = proj.shape[0]
    nb = S // RET_T
    group, row_tab = _ret_specs(S, lambda s: s)

    def body(lg_ref, rq_ref, rk_ref, rv_ref, rg_ref, cos_ref, sin_ref, y_ref, o_ref, st_out, st_ref, dm_ref):
        @pl.when(pl.program_id(0) == 0)
        def _():
            st_ref[...] = jnp.zeros_like(st_ref)
            for h in range(4):
                dm_ref[h] = _ret_decay_mask(lg_ref[h])

        chains, rows, tab = _ret_chains()
        cosf, sinf = {c: tab(cos_ref, c) for c in chains}, {c: tab(sin_ref, c) for c in chains}
        q, k, qb, kb, vb, sc, qdec, kdec, block_dec = _ret_blocks(
            chains, rows, lg_ref, rq_ref, rk_ref, rv_ref, cosf, sinf, dm_ref)
        kv = {c: _dot_tn((k[c] * kdec[c]).astype(BF16), vb[c]) for c in chains}
        st = {(h, 0): st_ref[h] for h in range(4)}
        for b in range(RET_RB):
            for h in range(4):
                st[(h, b + 1)] = st[(h, b)] * block_dec[(h, b)] + kv[(h, b)]
        for h, b in chains:
            st_out[h, b] = st[(h, b)]
        for h in range(4):
            st_ref[h] = st[(h, RET_RB)]
        o = {c: _dot(sc[c].astype(BF16), vb[c]) + _dot(qb[c], st[c].astype(BF16)) * qdec[c] for c in chains}
        for c in chains:
            o_ref[rows(c)] = o[c]
        cen = {c: o[c] - _rowmean(o[c]) for c in chains}
        on = {c: cen[c] * lax.rsqrt(_rowmean(cen[c] * cen[c]) + EPS) for c in chains}
        rg = {c: rg_ref[rows(c)] for c in chains}
        for c in chains:
            y_ref[rows(c)] = (on[c] * (rg[c] * _sigmoid(rg[c]))).astype(BF16)

    return pl.pallas_call(
        body, name="ret_fwd", grid=(nb // RET_RB,),
        in_specs=[pl.BlockSpec(memory_space=pltpu.SMEM),
                  group(0), group(1), group(2), group(3), row_tab, row_tab],
        out_specs=[group(0), group(0),
                   pl.BlockSpec((4, RET_RB, LANES, LANES), lambda s: (0, s, 0, 0))],
        out_shape=[jax.ShapeDtypeStruct((S, GROUP_W), BF16),
                   jax.ShapeDtypeStruct((S, GROUP_W), F32),
                   jax.ShapeDtypeStruct((4, nb, LANES, LANES), F32)],
        scratch_shapes=[pltpu.VMEM((4, LANES, LANES), F32), pltpu.VMEM((4, RET_T, RET_T), F32)],
        compiler_params=_cp("arbitrary"),
    )(lgam, proj, proj, proj, proj, cosf, sinf)


def ret_bwd(proj, cosf, sinf, lgam, o, states, dycat):
    S = proj.shape[0]
    nsteps = S // RET_T // RET_RB
    rev = lambda s: nsteps - 1 - s
    group, row_tab = _ret_specs(S, rev)

    def body(lg_ref, rq_ref, rk_ref, rv_ref, rg_ref, cos_ref, sin_ref, o_ref, st_in, dy_ref,
             drq_ref, drk_ref, drv_ref, drg_ref, ds_ref, dm_ref):
        @pl.when(pl.program_id(0) == 0)
        def _():
            ds_ref[...] = jnp.zeros_like(ds_ref)
            for h in range(4):
                dm_ref[h] = _ret_decay_mask(lg_ref[h])

        chains, rows, tab = _ret_chains()
        cosf, sinf = {c: tab(cos_ref, c) for c in chains}, {c: tab(sin_ref, c) for c in chains}
        dms = {c: dm_ref[c[0]] for c in chains}
        q, k, qb, kb, vb, sc, qdec, kdec, block_dec = _ret_blocks(
            chains, rows, lg_ref, rq_ref, rk_ref, rv_ref, cosf, sinf, dm_ref)
        o_v = {c: o_ref[rows(c)] for c in chains}
        cen = {c: o_v[c] - _rowmean(o_v[c]) for c in chains}
        rstd = {c: lax.rsqrt(_rowmean(cen[c] * cen[c]) + EPS) for c in chains}
        on = {c: cen[c] * rstd[c] for c in chains}
        rg = {c: rg_ref[rows(c)] for c in chains}
        sig = {c: _sigmoid(rg[c]) for c in chains}
        dy = {c: dy_ref[rows(c)] for c in chains}
        for c in chains:
            drg_ref[rows(c)] = (dy[c] * on[c] * (sig[c] * (1.0 + rg[c] * (1.0 - sig[c])))).astype(BF16)
        don = {c: dy[c] * (rg[c] * sig[c]) for c in chains}
        do = {c: rstd[c] * (don[c] - _rowmean(don[c]) - on[c] * _rowmean(don[c] * on[c])) for c in chains}
        dob = {c: do[c].astype(BF16) for c in chains}
        dsc = {c: (_dot_nt(dob[c], vb[c]) * dms[c]).astype(BF16) for c in chains}
        st_b = {c: st_in[c[0], c[1]].astype(BF16) for c in chains}
        dst = {c: _dot_tn((q[c] * qdec[c]).astype(BF16), dob[c]) for c in chains}
        dsn = {(h, RET_RB): ds_ref[h] for h in range(4)}
        for b in reversed(range(RET_RB)):
            for h in range(4):
                dsn[(h, b)] = dsn[(h, b + 1)] * block_dec[(h, b)] + dst[(h, b)]
        for h in range(4):
            ds_ref[h] = dsn[(h, 0)]
        dsn_b = {c: dsn[(c[0], c[1] + 1)].astype(BF16) for c in chains}
        dq = {c: _dot(dsc[c], kb[c]) + _dot_nt(dob[c], st_b[c]) * qdec[c] for c in chains}
        dk = {c: (_dot_tn(dsc[c], qb[c]) + _dot_nt(vb[c], dsn_b[c]) * kdec[c]) * RK_SCALE for c in chains}
        dv = {c: _dot_tn(sc[c].astype(BF16), dob[c]) + _dot((k[c] * kdec[c]).astype(BF16), dsn_b[c])
              for c in chains}
        for c in chains:
            drq_ref[rows(c)] = (dq[c] * cosf[c] + _swap_halves(dq[c] * sinf[c])).astype(BF16)
            drk_ref[rows(c)] = (dk[c] * cosf[c] + _swap_halves(dk[c] * sinf[c])).astype(BF16)
            drv_ref[rows(c)] = dv[c].astype(BF16)

    return pl.pallas_call(
        body, name="ret_bwd", grid=(nsteps,),
        in_specs=[pl.BlockSpec(memory_space=pltpu.SMEM),
                  group(0), group(1), group(2), group(3), row_tab, row_tab,
                  group(0), pl.BlockSpec((4, RET_RB, LANES, LANES), lambda s: (0, rev(s), 0, 0)),
                  group(0)],
        out_specs=[group(0)] * 4,
        out_shape=[jax.ShapeDtypeStruct((S, GROUP_W), BF16)] * 4,
        scratch_shapes=[pltpu.VMEM((4, LANES, LANES), F32), pltpu.VMEM((4, RET_T, RET_T), F32)],
        compiler_params=_cp("arbitrary"),
    )(lgam, proj, proj, proj, proj, cosf, sinf, o, states, dycat)


def outproj_fwd(x, vecs, y_ret, y_sb, w_out, head=None, tm=1024):
    S, D = x.shape
    tm = min(tm, S)
    last = list(head or ())

    def body(x_ref, v_ref, yr_ref, ys_ref, w_ref, *refs):
        y = _dot(yr_ref[...], w_ref[0:GROUP_W, :]) + _dot(ys_ref[...], w_ref[GROUP_W:, :])
        xv = x_ref[...] + v_ref[2:3, :] * y
        if not last:
            y_ref, xo_ref = refs
            y_ref[...] = y.astype(BF16)
            xo_ref[...] = xv
            return
        g_ref, t_ref, y_ref, dx_ref, st_ref = refs
        y_ref[...] = y.astype(BF16)

        @pl.when(pl.program_id(0) == 0)
        def _():
            st_ref[...] = jnp.zeros_like(st_ref)

        g = g_ref[0:1, :]
        r = lax.rsqrt(_rowmean(xv * xv) + EPS)
        xn = xv * r
        err = xn * g - t_ref[...]
        dy = err * (1.0 / D)
        dxn = dy * g
        dx_ref[...] = r * (dxn - xn * _rowmean(dxn * xn))
        st_ref[0:1, :] += jnp.sum(dy * xn, axis=0, keepdims=True)
        st_ref[1:2, :] += jnp.sum(err * err, axis=0, keepdims=True)

    row = lambda w: pl.BlockSpec((tm, w), lambda i: (i, 0))
    fixed = pl.BlockSpec((8, D), lambda i: (0, 0))
    return pl.pallas_call(
        body, name="outproj_fwd", grid=(S // tm,),
        in_specs=[row(D), fixed, row(GROUP_W), row(GROUP_W), pl.BlockSpec((D, D), lambda i: (0, 0))]
        + ([fixed, row(D)] if last else []),
        out_specs=[row(D), row(D)] + ([fixed] if last else []),
        out_shape=[jax.ShapeDtypeStruct((S, D), BF16), jax.ShapeDtypeStruct((S, D), F32)]
        + ([jax.ShapeDtypeStruct((8, D), F32)] if last else []),
        compiler_params=_cp("arbitrary"),
    )(x, vecs, y_ret, y_sb, w_out, *last)


def outproj_bwd(dx, y, vecs, y_ret, y_sb, w_out, tm=1024):
    S, D = dx.shape
    tm = min(tm, S)
    n = S // tm

    def body(dx_ref, y_ref, v_ref, yr_ref, ys_ref, w_ref, dyc_ref, dw_ref, st_ref, acc):
        i = pl.program_id(0)

        @pl.when(i == 0)
        def _():
            st_ref[...] = jnp.zeros_like(st_ref)
            acc[...] = jnp.zeros_like(acc)

        dxv = dx_ref[...]
        st_ref[0:1, :] += jnp.sum(dxv * y_ref[...].astype(F32), axis=0, keepdims=True)
        dyy = (dxv * v_ref[2:3, :]).astype(BF16)
        dyc_ref[...] = _dot_nt(dyy, w_ref[...])
        acc[0:GROUP_W, :] += _dot_tn(yr_ref[...], dyy)
        acc[GROUP_W:, :] += _dot_tn(ys_ref[...], dyy)

        @pl.when(i == n - 1)
        def _():
            dw_ref[...] = acc[...].astype(BF16)

    row = lambda w: pl.BlockSpec((tm, w), lambda i: (i, 0))
    fixed = lambda r: pl.BlockSpec((r, D), lambda i: (0, 0))
    return pl.pallas_call(
        body, name="outproj_bwd", grid=(n,),
        in_specs=[row(D), row(D), fixed(8), row(GROUP_W), row(GROUP_W), fixed(D)],
        out_specs=[row(D), fixed(D), fixed(8)],
        out_shape=[jax.ShapeDtypeStruct((S, D), F32), jax.ShapeDtypeStruct((D, D), BF16),
                   jax.ShapeDtypeStruct((8, D), F32)],
        scratch_shapes=[pltpu.VMEM((D, D), F32)],
        compiler_params=_cp("arbitrary"),
    )(dx, y, vecs, y_ret, y_sb, w_out)


def inproj_bwd_x(pieces, w3, x, vecs, dx_res, ship=None, tm=512):
    S, D = x.shape
    n = S // tm
    ex = _Exchange(ship)

    def body(*refs):
        p_refs, (w_ref, x_ref, v_ref, dr_ref), refs = refs[:8], refs[8:12], refs[12:]
        ship_refs, (dx_ref, st_ref), refs = refs[:ex.n_in], refs[ex.n_in:ex.n_in + 2], refs[ex.n_in + 2:]
        start, finish = ex.ops(ship_refs, refs)

        @pl.when(pl.program_id(0) == 0)
        def _():
            st_ref[...] = jnp.zeros_like(st_ref)
            start()

        dh = jnp.zeros((tm, D), F32)
        for k, p_ref in enumerate(p_refs):
            c0 = (k % 2) * GROUP_W
            dh = dh + _dot_nt(p_ref[...], w_ref[k // 2, :, c0:c0 + GROUP_W])
        xv = x_ref[...]
        r = lax.rsqrt(_rowmean(xv * xv) + EPS)
        xn = xv * r
        g, scale1 = v_ref[3:4, :], 1.0 + v_ref[1:2, :]
        st_ref[0:1, :] += jnp.sum(dh, axis=0, keepdims=True)
        dh_xn = dh * xn
        st_ref[1:2, :] += jnp.sum(dh_xn, axis=0, keepdims=True) * g
        st_ref[2:3, :] += jnp.sum(dh_xn, axis=0, keepdims=True) * scale1
        dxn = dh * (g * scale1)
        dx_ref[...] = r * (dxn - xn * _rowmean(dxn * xn)) + dr_ref[...]
        pl.when(pl.program_id(0) == n - 1)(finish)

    row = lambda w: pl.BlockSpec((tm, w), lambda i: (i, 0))
    return pl.pallas_call(
        body, name="inproj_bwd_x", grid=(n,),
        in_specs=[row(GROUP_W)] * 8 + [pl.BlockSpec((N_SHARD, D, SHARD_W), lambda i: (0, 0, 0)),
                                       row(D), pl.BlockSpec((8, D), lambda i: (0, 0)), row(D)] + ex.in_specs,
        out_specs=[row(D), pl.BlockSpec((8, D), lambda i: (0, 0))] + ex.out_specs,
        out_shape=[jax.ShapeDtypeStruct((S, D), F32), jax.ShapeDtypeStruct((8, D), F32)] + ex.out_shape,
        scratch_shapes=ex.scratch,
        compiler_params=_cp("arbitrary"),
    )(*pieces, w3, x, vecs, dx_res, *ex.ship)


def inproj_bwd_w(h, pieces, tm=1024):
    S, D = h.shape
    tm = min(tm, S)
    n = S // tm

    def body(*refs):
        h_ref, p_refs, dw_ref, acc = refs[0], refs[1:9], refs[9], refs[10]
        i = pl.program_id(0)

        @pl.when(i == 0)
        def _():
            acc[...] = jnp.zeros_like(acc)

        hv = h_ref[...]
        for k, p_ref in enumerate(p_refs):
            c0 = (k % 2) * GROUP_W
            acc[k // 2, :, c0:c0 + GROUP_W] += _dot_tn(hv, p_ref[...])

        @pl.when(i == n - 1)
        def _():
            dw_ref[...] = acc[...].astype(BF16)

    row = lambda w: pl.BlockSpec((tm, w), lambda i: (i, 0))
    return pl.pallas_call(
        body, name="inproj_bwd_w", grid=(n,),
        in_specs=[row(D)] + [row(GROUP_W)] * 8,
        out_specs=pl.BlockSpec((N_SHARD, D, SHARD_W), lambda i: (0, 0, 0), pipeline_mode=pl.Buffered(1)),
        out_shape=jax.ShapeDtypeStruct((N_SHARD, D, SHARD_W), BF16),
        scratch_shapes=[pltpu.VMEM((N_SHARD, D, SHARD_W), F32)],
        compiler_params=_cp("arbitrary"),
    )(h, *pieces)


def layer_fwd(x, vecs, w3, w_out, tabs, gather=None, head=None):
    cosf, sinf, lgam = tabs
    ret, sg, h, sb = inproj_fwd(x, vecs, w3)
    y_ret, o_ret, states = ret_fwd(ret, cosf, sinf, lgam)
    y_sb, o_sb, sb_end, *gathered = sb_fwd(sb, sg, gather)
    if callable(w_out):
        w_out = w_out(gathered)
    y, *x_next = outproj_fwd(x, vecs, y_ret, y_sb, w_out, head)
    saved = (x, ret, sg, h, sb, y_ret, o_ret, states, y_sb, o_sb, sb_end, y)
    return (x_next[0] if head is None else x_next), saved, gathered


def _by_shard(dw_out):
    return dw_out.reshape(N_SHARD, D_MODEL // N_SHARD, D_MODEL)


def layer_bwd(dx, saved, vecs, w3, w_out, tabs, later_grads=None):
    cosf, sinf, lgam = tabs
    x, ret, sg, h, sb, y_ret, o_ret, states, y_sb, o_sb, sb_end, y = saved
    dycat, dw_out, st_o = outproj_bwd(dx, y, vecs, y_ret, y_sb, w_out)
    dw_out = _by_shard(dw_out)
    ship = None if later_grads is None else (later_grads[0], dw_out, later_grads[1])
    *d_sb, = sb_bwd(sb, sg, o_sb, sb_end, dycat, ship)
    d_ret = ret_bwd(ret, cosf, sinf, lgam, o_ret, states, dycat)
    pieces = list(d_ret) + d_sb[:4]
    dw_in = inproj_bwd_w(h, pieces)
    dx, st_i, *recv_in = inproj_bwd_x(pieces, w3, x, vecs, dx, None if later_grads is None else (dw_in,))
    dmod = jnp.concatenate([st_i[0:2], st_o[0:1]], axis=0)
    grads = (dw_in, dw_out) if later_grads is None else (recv_in[0], d_sb[4])
    return dx, dmod, st_i[2:3], grads


def _place():
    return lax.axis_index("x"), lax.axis_index("y"), lax.axis_index("c")


def _other_chips(mx, my):
    return [(1 - mx, my), (mx, 1 - my), (1 - mx, 1 - my)]


_ANY = pl.BlockSpec(memory_space=pl.ANY)


_GATHER_SCRATCH = [pltpu.SemaphoreType.DMA((7,)), pltpu.SemaphoreType.DMA((7,)), pltpu.SemaphoreType.DMA(())]


def _gather_ops(x_ref, out_ref, send_sems, recv_sems, local_sem):
    mx, my, mc = _place()
    me, sibling = (mx, my, mc), (mx, my, 1 - mc)
    chips = _other_chips(mx, my)

    def slot(px, py, pc):
        return out_ref.at[4 * px + 2 * py + pc]

    def copy(k, block, to, src=None):
        return pltpu.make_async_remote_copy(
            src_ref=slot(*block) if src is None else src, dst_ref=slot(*block),
            send_sem=send_sems.at[k], recv_sem=recv_sems.at[k], device_id=to, device_id_type=MESH)

    mine = pltpu.make_async_copy(x_ref, slot(*me), local_sem)
    first = [copy(0, me, sibling, src=x_ref)]
    first += [copy(1 + j, me, (*chip, mc), src=x_ref) for j, chip in enumerate(chips)]
    passed = [copy(4 + j, (*chip, mc), sibling) for j, chip in enumerate(chips)]

    def start():
        mine.start()
        for cp in first:
            cp.start()

    def forward():
        for j, chip in enumerate(chips):
            copy(1 + j, (*chip, mc), me).wait_recv()
            passed[j].start()

    def finish():
        copy(0, sibling, me).wait_recv()
        for j, chip in enumerate(chips):
            copy(4 + j, (*chip, 1 - mc), me).wait_recv()
        for cp in first + passed:
            cp.wait_send()
        mine.wait()

    return start, forward, finish


class _Exchange:
    def __init__(self, ship):
        self.ship = list(ship or ())
        self.n_in = len(self.ship)
        self.n_out = 1 if self.ship else 0
        self.rows = [a.shape[1] for a in self.ship]
        self.in_specs = [_ANY] * self.n_in
        self.out_specs = [_ANY] * self.n_out
        self.out_shape = [jax.ShapeDtypeStruct((N_SHARD, sum(self.rows), SHARD_W), BF16)] * self.n_out
        sem = pltpu.SemaphoreType.DMA
        self.scratch = [sem((3,)), sem((3,)), sem(())] * self.n_out

    def ops(self, ship_refs, tail):
        if not self.ship:
            return (lambda: None), (lambda: None)
        recv, send_sems, recv_sems, local_sem = tail
        mx, my, mc = _place()
        my_chip = 2 * mx + my
        chips = _other_chips(mx, my)

        def pieces(s):
            firsts = np.cumsum([0] + self.rows[:-1])
            return [(ref.at[s], int(r0), n) for ref, r0, n in zip(ship_refs, firsts, self.rows)]

        def start():
            for src, r0, n in pieces(my_chip):
                pltpu.make_async_copy(src, recv.at[my_chip, pl.ds(r0, n)], local_sem).start()
            for j, (px, py) in enumerate(chips):
                for src, r0, n in pieces(2 * px + py):
                    pltpu.make_async_remote_copy(
                        src_ref=src, dst_ref=recv.at[my_chip, pl.ds(r0, n)],
                        send_sem=send_sems.at[j], recv_sem=recv_sems.at[j],
                        device_id=(px, py, mc), device_id_type=MESH).start()

        def finish():
            for j, (px, py) in enumerate(chips):
                whole = recv.at[2 * px + py]
                both = pltpu.make_async_remote_copy(
                    src_ref=whole, dst_ref=whole, send_sem=send_sems.at[j], recv_sem=recv_sems.at[j],
                    device_id=(px, py, mc), device_id_type=MESH)
                both.wait_recv()
                both.wait_send()
            pltpu.make_async_copy(recv.at[my_chip], recv.at[my_chip], local_sem).wait()

        return start, finish


def sum_and_swap(recv_a, recv_b, stats, tr=256):
    n, rows_a, cols = recv_a.shape
    na, nb = rows_a // tr, recv_b.shape[1] // tr
    nt = na + nb

    def body(a_ref, b_ref, st_ref, own_ref, sib_ref, stall_ref, slots, send_sems, recv_sem, *gather_sems):
        i = pl.program_id(0)
        mx, my, mc = _place()
        slot = i % 2
        g_start, g_forward, g_finish = _gather_ops(st_ref, stall_ref, *gather_sems)
        pl.when(i == 0)(g_start)
        pl.when(i == nt // 2)(g_forward)

        def push(k, tile):
            return pltpu.make_async_remote_copy(
                src_ref=slots.at[k], dst_ref=sib_ref.at[pl.ds(pl.multiple_of(tile * tr, tr), tr)],
                send_sem=send_sems.at[k], recv_sem=recv_sem, device_id=(mx, my, 1 - mc), device_id_type=MESH)

        pl.when(i >= 2)(lambda: push(slot, i - 2).wait_send())

        def total(r_ref):
            acc = r_ref[0].astype(F32)
            for k in range(1, n):
                acc = acc + r_ref[k].astype(F32)
            own_ref[...] = acc
            slots[slot] = acc

        pl.when(i < na)(lambda: total(a_ref))
        pl.when(i >= na)(lambda: total(b_ref))
        push(slot, i).start()

        @pl.when(i == nt - 1)
        def _():
            push(1 - slot, i - 1).wait_send()
            push(slot, i).wait_send()
            pltpu.make_async_remote_copy(src_ref=sib_ref, dst_ref=sib_ref, send_sem=send_sems.at[0], recv_sem=recv_sem,
                                         device_id=(mx, my, 1 - mc), device_id_type=MESH).wait_recv()
            g_finish()

    return pl.pallas_call(
        body, name="sum_and_swap", grid=(nt,),
        in_specs=[pl.BlockSpec((n, tr, cols), lambda i: (0, jnp.minimum(i, na - 1), 0)),
                  pl.BlockSpec((n, tr, cols), lambda i: (0, jnp.maximum(i - na, 0), 0)), _ANY],
        out_specs=[pl.BlockSpec((tr, cols), lambda i: (i, 0)), _ANY, _ANY],
        out_shape=[jax.ShapeDtypeStruct((nt * tr, cols), F32)] * 2
        + [jax.ShapeDtypeStruct((8,) + stats.shape, stats.dtype)],
        scratch_shapes=[pltpu.VMEM((2, tr, cols), F32), pltpu.SemaphoreType.DMA((2,)), pltpu.SemaphoreType.DMA(())]
        + _GATHER_SCRATCH,
        compiler_params=_cp("arbitrary"),
    )(recv_a, recv_b, stats)


def _adamw(w, g, m, v):
    m = ADAM_B1 * m + (1.0 - ADAM_B1) * g
    v = ADAM_B2 * v + (1.0 - ADAM_B2) * (g * g)
    m_hat = m / (1.0 - ADAM_B1 ** ADAM_STEP)
    v_hat = v / (1.0 - ADAM_B2 ** ADAM_STEP)
    delta = -ADAM_LR * (m_hat / (jnp.sqrt(v_hat) + ADAM_EPS) + ADAM_WD * w)
    return delta, m, v


def adam_slab(p_own, p_sib, w, m, v, row0, name, tr=256):
    L, R, C = w.shape
    nr = R // tr

    def body(a_ref, b_ref, w_ref, m_ref, v_ref, g_out, d_out, m_out, v_out):
        g = a_ref[...] + b_ref[...]
        d, m2, v2 = _adamw(w_ref[0], g, m_ref[0], v_ref[0])
        g_out[0], d_out[0], m_out[0], v_out[0] = g, d, m2, v2

    slab = pl.BlockSpec((tr, C), lambda l, i: (row0 // tr + l * nr + i, 0))
    blk = pl.BlockSpec((1, tr, C), lambda l, i: (l, i, 0))
    return pl.pallas_call(
        body, name=name, grid=(L, nr),
        in_specs=[slab, slab, blk, blk, blk], out_specs=[blk] * 4,
        out_shape=[jax.ShapeDtypeStruct(w.shape, F32)] * 4,
        compiler_params=_cp("arbitrary", "arbitrary"),
    )(p_own, p_sib, w, m, v)


def prologue(c8, w_ada, b_ada, norm_g, win_first):
    L, D, W = w_ada.shape

    def body(c_ref, w_ref, b_ref, g_ref, win_ref, vecs_ref, call_ref, wall_ref, mod_ref, mall_ref, *sems):
        w_start, w_forward, w_finish = _gather_ops(win_ref, wall_ref, *sems[0:3])
        w_start()
        for step in _gather_ops(c_ref, call_ref, *sems[3:6]):
            step()
        cv = call_ref[:, 0, :]
        ca = cv * _sigmoid(cv)
        for l in range(L):
            mod_ref[l * 8:(l + 1) * 8, :] = jnp.dot(ca, w_ref[l], precision=lax.Precision.HIGHEST,
                                                    preferred_element_type=F32)
        for step in _gather_ops(mod_ref, mall_ref, *sems[6:9]):
            step()
        mx, my, mc = _place()
        me = 4 * mx + 2 * my + mc
        rowid = lax.broadcasted_iota(jnp.int32, (L * 8, 1), 0)
        vecs_ref[...] = jnp.zeros_like(vecs_ref)
        for l in range(L):
            parts = [jnp.sum(jnp.where(rowid == l * 8 + me, mall_ref[2 * s + mc], 0.0), axis=0, keepdims=True)
                     for s in range(N_SHARD)]
            mod = jnp.concatenate(parts, axis=1) + b_ref[l:l + 1, :]
            for t in range(3):
                vecs_ref[l, t:t + 1, :] = mod[:, t * D:(t + 1) * D]
            vecs_ref[l, 3:4, :] = g_ref[l:l + 1, :]
        w_forward()
        w_finish()

    vmem = pl.BlockSpec(memory_space=pltpu.VMEM)
    return pl.pallas_call(
        body, name="prologue",
        in_specs=[vmem, vmem, vmem, vmem, _ANY], out_specs=[vmem, vmem, _ANY],
        out_shape=[jax.ShapeDtypeStruct((L, 8, D), F32), jax.ShapeDtypeStruct((8, 8, D), F32),
                   jax.ShapeDtypeStruct((8,) + win_first.shape, win_first.dtype)],
        scratch_shapes=[pltpu.VMEM((L * 8, W), F32), pltpu.VMEM((8, L * 8, W), F32)] + _GATHER_SCRATCH * 3,
        compiler_params=pltpu.CompilerParams(vmem_limit_bytes=VMEM_LIMIT_BYTES),
    )(c8, w_ada, b_ada, norm_g, win_first)


def ada_update(dmods, c_t, w, m, v, tr=256):
    L, D, W = w.shape

    def body(dm_ref, c_ref, w_ref, m_ref, v_ref, g_out, d_out, m_out, v_out):
        mx, my, _ = _place()
        shard = 2 * mx + my
        dm = jnp.zeros((8, W), F32)
        for s in range(N_SHARD):
            dm = dm + jnp.where(shard == s, dm_ref[0, :, s * W:(s + 1) * W], 0.0)
        cv = c_ref[...]
        ca = cv * _sigmoid(cv)
        g = jnp.zeros((tr, W), F32)
        for b in range(8):
            g = g + ca[:, b:b + 1] * dm[b:b + 1, :]
        d, m2, v2 = _adamw(w_ref[0], g, m_ref[0], v_ref[0])
        g_out[0], d_out[0], m_out[0], v_out[0] = g, d, m2, v2

    blk = pl.BlockSpec((1, tr, W), lambda l, i: (l, i, 0))
    return pl.pallas_call(
        body, name="ada_update", grid=(L, D // tr),
        in_specs=[pl.BlockSpec((1, 8, 3 * D), lambda l, i: (l, 0, 0)), pl.BlockSpec((tr, 8), lambda l, i: (i, 0)),
                  blk, blk, blk],
        out_specs=[blk] * 4, out_shape=[jax.ShapeDtypeStruct(w.shape, F32)] * 4,
        compiler_params=_cp("arbitrary", "arbitrary"),
    )(dmods, c_t, w, m, v)


STAT_ROWS = 16


def small_update(stats_all, norm, b_ada, final):
    def body(s_ref, *refs):
        ins, outs = refs[:9], refs[9:]
        tot = s_ref[0]
        for k in range(1, 8):
            tot = tot + s_ref[k]
        g_norm = tot[0:2, :]
        g_final = tot[2:3, :]
        g_b = jnp.concatenate(
            [jnp.concatenate([tot[3 + 3 * l + t:4 + 3 * l + t, :] for t in range(3)], axis=1) for l in range(DEPTH)],
            axis=0)
        for p, g in enumerate((g_norm, g_b, g_final)):
            w_ref, m_ref, v_ref = ins[3 * p:3 * p + 3]
            d, m2, v2 = _adamw(w_ref[...], g, m_ref[...], v_ref[...])
            for o_ref, val in zip(outs[4 * p:4 * p + 4], (g, d, m2, v2)):
                o_ref[...] = val
        loss = (0.5 / D_MODEL) * jnp.sum(tot[9:10, :], axis=1, keepdims=True)
        outs[12][...] = jnp.broadcast_to(loss, (8, LANES))

    shapes = []
    for w, _, _ in (norm, b_ada, final):
        shapes += [jax.ShapeDtypeStruct(w.shape, F32)] * 4
    shapes.append(jax.ShapeDtypeStruct((8, LANES), F32))
    return pl.pallas_call(body, name="small_update", out_shape=shapes)(stats_all, *norm, *b_ada, *final)


def kernel(x, c, norm_g, w_ada, b_ada, w_in, w_out, final_g, loss_target, m_norm_g, m_w_ada, m_b_ada, m_w_in, m_w_out, m_final_g, v_norm_g, v_w_ada, v_b_ada, v_w_in, v_w_out, v_final_g):
    S, D = x.shape[1], x.shape[2]
    mc = lax.axis_index("c")
    out_rows = D // N_SHARD

    def my_half(a, rows):
        return lax.dynamic_slice_in_dim(a, mc * rows, rows, axis=0)

    assert DEPTH == 2
    win = [my_half(w_in[l], D // 2).astype(BF16) for l in range(DEPTH)]
    wout = [my_half(w_out[l], out_rows // 2).astype(BF16) for l in range(DEPTH)]
    rest = [jnp.concatenate(wout, axis=0), win[1]]

    def unpack(gathered):
        outs, w3_second = gathered
        outs = outs.reshape(N_SHARD, 2, DEPTH, out_rows // 2, SHARD_W)
        return outs[:, :, 0].reshape(D, D), (w3_second.reshape(N_SHARD, D, SHARD_W), outs[:, :, 1].reshape(D, D))

    vecs, c_all, w3_first = prologue(jnp.broadcast_to(c, (8, D)), w_ada, b_ada, norm_g, win[0])
    c_all, w3_first = c_all[:, 0, :], w3_first.reshape(N_SHARD, D, SHARD_W)

    tabs = (*rope_tables(S), ret_log_gamma())
    saved = [None] * DEPTH
    h, saved[0], wall = layer_fwd(x[0], vecs[0], w3_first, lambda g: unpack(g)[0], tabs, rest)
    weights = [(w3_first, unpack(wall)[0]), unpack(wall)[1]]
    head = (jnp.broadcast_to(final_g[None, :], (8, D)), loss_target[0])
    (dx, st_loss), saved[1], _ = layer_fwd(h, vecs[1], *weights[1], tabs, head=head)

    dmod, dnorm, grads = [None] * DEPTH, [None] * DEPTH, None
    for l in reversed(range(DEPTH)):
        dx, dmod[l], dnorm[l], grads = layer_bwd(dx, saved[l], vecs[l], *weights[l], tabs, grads)

    stats = jnp.concatenate(dnorm + [st_loss[0:1]] + dmod + [st_loss[1:2], jnp.zeros((STAT_ROWS - 10, D), F32)], axis=0)
    p_own, p_sib, stats_all = sum_and_swap(*grads, stats)
    res_in = adam_slab(p_own, p_sib, w_in, m_w_in, v_w_in, 0, "adam_w_in")
    res_out = adam_slab(p_own, p_sib, w_out, m_w_out, v_w_out, DEPTH * D, "adam_w_out", tr=128)

    dmods = stats_all[:, 3:9, :].reshape(8, DEPTH, 3 * D).transpose(1, 0, 2)
    res_ada = ada_update(dmods, c_all.T, w_ada, m_w_ada, v_w_ada)
    small = small_update(stats_all, (norm_g, m_norm_g, v_norm_g), (b_ada, m_b_ada, v_b_ada),
                         (final_g[None, :], m_final_g[None, :], v_final_g[None, :]))
    res_norm, res_b, res_final = small[0:4], small[4:8], [a[0] for a in small[8:12]]
    loss = small[12][0, 0]

    by_kind = [res_norm, res_ada, res_b, res_in, res_out, res_final]
    outs = [loss, dx[None]]
    for kind in range(4):
        outs += [r[kind] for r in by_kind]
    return tuple(outs)
```

```python
import numpy as np
import jax
import jax.numpy as jnp
from jax import lax
from jax.experimental import pallas as pl
from jax.experimental.pallas import tpu as pltpu

F32, BF16 = jnp.float32, jnp.bfloat16
MESH = pl.DeviceIdType.MESH

D_MODEL = 1024
DEPTH = 2
SHARD_W = 1024
N_SHARD = 4
GROUP_W = 512
LANES = 128
SB_HEAD_DIM = 64
RET_HEAD_DIM = 128
CHUNK = 64
ROPE_BASE = 10000.0
EPS = 1e-6
SQ_SCALE = SB_HEAD_DIM ** -0.5
RK_SCALE = RET_HEAD_DIM ** -0.5
SB_T = 1024
SB_CHAINS = 16
SB_NB = 4
RET_T = 256
EXP_ZERO = -104.0
VMEM_LIMIT_BYTES = 56 * 2 ** 20

ADAM_LR, ADAM_B1, ADAM_B2, ADAM_EPS, ADAM_WD, ADAM_STEP = 0.001, 0.9, 0.999, 1e-08, 0.01, 10


def _cp(*sem):
    return pltpu.CompilerParams(dimension_semantics=sem, vmem_limit_bytes=VMEM_LIMIT_BYTES)


def _dot(a, b):
    return lax.dot_general(a, b, (((1,), (0,)), ((), ())), preferred_element_type=F32)


def _dot_nt(a, b):
    return lax.dot_general(a, b, (((1,), (1,)), ((), ())), preferred_element_type=F32)


def _dot_tn(a, b):
    return lax.dot_general(a, b, (((0,), (0,)), ((), ())), preferred_element_type=F32)


def _running_sum(a, tri):
    return _dot(a.astype(BF16), tri)


def _sigmoid(x):
    return 1.0 / (1.0 + jnp.exp(-x))


def _rowsum(a):
    return jnp.sum(a, axis=1, keepdims=True)


def _rowmean(a):
    return jnp.mean(a, axis=1, keepdims=True)


def inproj_fwd(x, vecs, w3, tm=512):
    S, D = x.shape

    def body(x_ref, v_ref, w_ref, ret_ref, sg_ref, h_ref, sb_ref):
        xv = x_ref[...]
        r = lax.rsqrt(_rowmean(xv * xv) + EPS)
        h = xv * r * v_ref[3:4, :] * (1.0 + v_ref[1:2, :]) + v_ref[0:1, :]
        hb = h.astype(BF16)
        h_ref[...] = hb
        for s in range(N_SHARD):
            p = _dot(hb, w_ref[s])
            if s < 2:
                ret_ref[:, s * SHARD_W:(s + 1) * SHARD_W] = p
            if s == 2:
                sb_ref[:, 0:GROUP_W] = (p[:, 0:GROUP_W] * SQ_SCALE).astype(BF16)
                sb_ref[:, GROUP_W:SHARD_W] = p[:, GROUP_W:].astype(BF16)
            if s == 3:
                sb_ref[:, SHARD_W:SHARD_W + GROUP_W] = p[:, 0:GROUP_W].astype(BF16)
                sg_ref[...] = p[:, GROUP_W:]

    row = lambda w: pl.BlockSpec((tm, w), lambda i: (i, 0))
    return pl.pallas_call(
        body, name="inproj_fwd", grid=(S // tm,),
        in_specs=[row(D), pl.BlockSpec((8, D), lambda i: (0, 0)),
                  pl.BlockSpec((N_SHARD, D, SHARD_W), lambda i: (0, 0, 0))],
        out_specs=[row(2 * SHARD_W), row(GROUP_W), row(D), row(3 * GROUP_W)],
        out_shape=[jax.ShapeDtypeStruct((S, 2 * SHARD_W), F32), jax.ShapeDtypeStruct((S, GROUP_W), F32),
                   jax.ShapeDtypeStruct((S, D), BF16), jax.ShapeDtypeStruct((S, 3 * GROUP_W), BF16)],
        compiler_params=_cp("arbitrary"),
    )(x, vecs, w3)


def _sb_logits(qh, k2, keep):
    z = _dot_nt(qh, k2)
    sp = jnp.log(1.0 + jnp.exp(-jnp.abs(z)))
    lb = jnp.minimum(z, 0.0) - sp
    lk = lb - z
    if keep is not None:
        lk = jnp.where(keep, lk, 0.0)
    return lb, lk


class _sb_chains:
    def __init__(self, i, q2, do_b=None):
        t = self.t = SB_T // SB_CHAINS
        self.C = range(SB_CHAINS)
        r = lax.broadcasted_iota(jnp.int32, (SB_NB * t, SB_NB * t), 0)
        c = lax.broadcasted_iota(jnp.int32, (SB_NB * t, SB_NB * t), 1)
        self.later_all = jnp.where(r > c, 1.0, 0.0).astype(BF16)
        self.earlier_all = jnp.where(r < c, 1.0, 0.0).astype(BF16)
        self.later, self.earlier = self.later_all[:t, :t], self.earlier_all[:t, :t]
        self.head0 = lax.broadcasted_iota(jnp.int32, (1, LANES), 1) < SB_HEAD_DIM
        row = lax.broadcasted_iota(jnp.int32, (2 * t, SB_NB * t), 0) & (t - 1)
        col = lax.broadcasted_iota(jnp.int32, (2 * t, SB_NB * t), 1)
        qt = [SB_CHAINS * i + cc for cc in self.C]
        self.first = [jnp.maximum(qt[cc] - (SB_NB - 1), 0) for cc in self.C]
        self.keep = [self.first[cc] * t + col < qt[cc] * t + row for cc in self.C]
        self.qs = [self._stack(q2[cc * t:(cc + 1) * t]) for cc in self.C]
        if do_b is not None:
            self.dos = [self._stack(do_b[cc * t:(cc + 1) * t]) for cc in self.C]

    def _stack(self, a):
        zero = jnp.zeros_like(a)
        return jnp.concatenate([jnp.where(self.head0, a, zero), jnp.where(self.head0, zero, a)], axis=0)

    def rows(self, ref, j, n):
        return ref[pl.ds(pl.multiple_of(j * self.t, self.t), n * self.t), :]

    def suffix(self, lk):
        return _running_sum(lk, self.later_all), _rowsum(lk)

    def prefix(self, g, G0):
        return _running_sum(g, self.earlier_all) + G0


def sb_fwd(sb, sg, gather=None):
    S = sb.shape[0]
    T = SB_T
    nq = S // T
    carried = list(gather or ())
    ng = len(carried)

    def body(*refs):
        (q_ref, k_ref, v_ref, sg_ref), refs = refs[:4], refs[4:]
        x_refs, (y_ref, o_ref, end_ref), out_refs, sems = refs[:ng], refs[ng:ng + 3], refs[ng + 3:2 * ng + 3], refs[2 * ng + 3:]
        p, i = pl.program_id(0), pl.program_id(1)
        gathers = [_gather_ops(x_refs[g], out_refs[g], *sems[3 * g:3 * g + 3]) for g in range(ng)]
        for start, forward, _ in gathers:
            pl.when(jnp.logical_and(p == 0, i == 0))(start)
            pl.when(jnp.logical_and(p == 3, i == 0))(forward)
        ch = _sb_chains(i, q_ref[...])
        later, head0 = ch.later, ch.head0
        lbk = [_sb_logits(ch.qs[c], ch.rows(k_ref, ch.first[c], SB_NB), ch.keep[c]) for c in ch.C]
        suffix, R = zip(*[ch.suffix(lbk[c][1]) for c in ch.C])
        aa = [jnp.where(ch.keep[c], jnp.exp(lbk[c][0] + suffix[c]), 0.0) for c in ch.C]
        acc = [_dot(aa[c].astype(BF16), ch.rows(v_ref, ch.first[c], SB_NB)) for c in ch.C]

        nc = len(ch.C)

        def alive(n, Rs):
            m = None
            for c in ch.C:
                rc = jnp.where(ch.first[c] - n > 0, Rs[c], EXP_ZERO)
                m = rc if m is None else jnp.maximum(m, rc)
            return jnp.max(m)

        def cond(st):
            return st[-1] > EXP_ZERO

        def step(st):
            n, accs, Rs = st[0], list(st[1:1 + nc]), list(st[1 + nc:1 + 2 * nc])
            for c in ch.C:
                j = ch.first[c] - 1 - n
                jc = jnp.maximum(j, 0)
                lb, lk = _sb_logits(ch.qs[c], ch.rows(k_ref, jc, 1), None)
                a = jnp.exp(lb + _running_sum(lk, later) + Rs[c])
                cx = _dot(a.astype(BF16), ch.rows(v_ref, jc, 1))
                accs[c] = jnp.where(j >= 0, accs[c] + cx, accs[c])
                Rs[c] = jnp.where(j >= 0, Rs[c] + _rowsum(lk), Rs[c])
            return (n + 1, *accs, *Rs, alive(n + 1, Rs))

        st = lax.while_loop(cond, step, (jnp.int32(0), *acc, *R, alive(0, R)))
        n_end, acc, R = st[0], st[1:1 + nc], st[1 + nc:1 + 2 * nc]
        outs = []
        for c in ch.C:
            base = c * (2 * ch.t + 8)
            end_ref[0, 0, base:base + 2 * ch.t, :] = jnp.broadcast_to(R[c], (2 * ch.t, 8))
            end_ref[0, 0, base + 2 * ch.t:base + 2 * ch.t + 8, :] = jnp.full((8, 8), n_end.astype(F32))
            outs.append(jnp.where(head0, acc[c][:ch.t], acc[c][ch.t:]))
        o = jnp.concatenate(outs, axis=0)
        o_ref[...] = o
        sg = sg_ref[...]
        y_ref[...] = (o * (sg * _sigmoid(sg))).astype(BF16)
        for _, _, finish in gathers:
            pl.when(jnp.logical_and(p == 3, i == nq - 1))(finish)

    return pl.pallas_call(
        body, name="sb_fwd", grid=(4, nq),
        in_specs=[pl.BlockSpec((T, LANES), lambda p, i: (i, p)),
                  pl.BlockSpec((S, LANES), lambda p, i: (0, 4 + p)),
                  pl.BlockSpec((S, LANES), lambda p, i: (0, 8 + p)),
                  pl.BlockSpec((T, LANES), lambda p, i: (i, p))] + [_ANY for _ in carried],
        out_specs=[pl.BlockSpec((T, LANES), lambda p, i: (i, p)),
                   pl.BlockSpec((T, LANES), lambda p, i: (i, p)),
                   pl.BlockSpec((1, 1, SB_CHAINS * (2 * T // SB_CHAINS + 8), 8), lambda p, i: (p, i, 0, 0))] + [_ANY for _ in carried],
        out_shape=[jax.ShapeDtypeStruct((S, GROUP_W), BF16),
                   jax.ShapeDtypeStruct((S, GROUP_W), F32),
                   jax.ShapeDtypeStruct((4, nq, SB_CHAINS * (2 * T // SB_CHAINS + 8), 8), F32)]
        + [jax.ShapeDtypeStruct((8,) + a.shape, a.dtype) for a in carried],
        scratch_shapes=_GATHER_SCRATCH * ng,
        compiler_params=_cp("arbitrary", "arbitrary"),
    )(sb, sb, sb, sg, *carried)


def sb_bwd(sb, sg, o, sb_end, dycat, ship=None):
    S = sb.shape[0]
    T = SB_T
    nq = S // T
    ex = _Exchange(ship)

    def body(*refs):
        (q_ref, k_ref, v_ref, sg_ref, o_ref, dy_ref, end_ref), refs = refs[:7], refs[7:]
        ship_refs, (dq_ref, dk_ref, dv_ref, dsg_ref), refs = refs[:ex.n_in], refs[ex.n_in:ex.n_in + 4], refs[ex.n_in + 4:]
        recv, (dk_acc, dv_acc), sems = refs[:ex.n_out], refs[ex.n_out:ex.n_out + 2], refs[ex.n_out + 2:]
        start, finish = ex.ops(ship_refs, recv + sems)
        p, i = pl.program_id(0), pl.program_id(1)
        pl.when(jnp.logical_and(p == 0, i == 0))(start)

        @pl.when(i == 0)
        def _():
            dk_acc[...] = jnp.zeros_like(dk_acc)
            dv_acc[...] = jnp.zeros_like(dv_acc)

        sg = sg_ref[...]
        sig = _sigmoid(sg)
        dy = dy_ref[...]
        dsg_ref[...] = (dy * o_ref[...] * (sig * (1.0 + sg * (1.0 - sig)))).astype(BF16)
        do_b = (dy * (sg * sig)).astype(BF16)
        ch = _sb_chains(i, q_ref[...], do_b)
        later, earlier, head0, t = ch.later, ch.earlier, ch.head0, ch.t
        end = end_ref[0, 0]

        def grads(c, j, n, a, lb, g, G, keep):
            dz = g - jnp.exp(lb) * (g + G)
            if keep is not None:
                dz = jnp.where(keep, dz, 0.0)
            dzb = dz.astype(BF16)
            rows = pl.ds(pl.multiple_of(j * t, t), n * t)
            dk_acc[rows, :] += _dot_tn(dzb, ch.qs[c])
            dv_acc[rows, :] += _dot_tn(a.astype(BF16), ch.dos[c])
            return _dot(dzb, ch.rows(k_ref, j, n))

        nc = len(ch.C)
        n_end = jnp.max(end[2 * t:2 * t + 8, :]).astype(jnp.int32)

        def sweep(m, st):
            dqs, G0s, lefts = list(st[:nc]), list(st[nc:2 * nc]), list(st[2 * nc:])
            for c in ch.C:
                j = ch.first[c] - n_end + m
                jc = jnp.maximum(j, 0)
                lb, lk = _sb_logits(ch.qs[c], ch.rows(k_ref, jc, 1), None)
                stick = lefts[c] - _rowsum(lk)
                a = jnp.where(j >= 0, jnp.exp(lb + _running_sum(lk, later) + stick), 0.0)
                g = a * _dot_nt(ch.dos[c], ch.rows(v_ref, jc, 1))
                G = _running_sum(g, earlier) + G0s[c]
                dqs[c] = dqs[c] + grads(c, jc, 1, a, lb, jnp.where(j >= 0, g, 0.0), jnp.where(j >= 0, G, 0.0), None)
                G0s[c] = G0s[c] + _rowsum(g)
                lefts[c] = jnp.where(j >= 0, stick, lefts[c])
            return (*dqs, *G0s, *lefts)

        lefts = [end[c * (2 * t + 8):c * (2 * t + 8) + 2 * t, 0:1] for c in ch.C]
        st = lax.fori_loop(0, n_end, sweep, (*[jnp.zeros((2 * t, LANES), F32)] * nc,
                                             *[jnp.zeros((2 * t, 1), F32)] * nc, *lefts))
        dq, G0 = st[:nc], st[nc:2 * nc]

        lbk = [_sb_logits(ch.qs[c], ch.rows(k_ref, ch.first[c], SB_NB), ch.keep[c]) for c in ch.C]
        suffix = [ch.suffix(lbk[c][1])[0] for c in ch.C]
        aa = [jnp.where(ch.keep[c], jnp.exp(lbk[c][0] + suffix[c]), 0.0) for c in ch.C]
        g = [aa[c] * _dot_nt(ch.dos[c], ch.rows(v_ref, ch.first[c], SB_NB)) for c in ch.C]
        G = [ch.prefix(g[c], G0[c]) for c in ch.C]
        for c in ch.C:
            dqc = dq[c] + grads(c, ch.first[c], SB_NB, aa[c], lbk[c][0], g[c], G[c], ch.keep[c])
            dq_ref[c * t:(c + 1) * t, :] = (jnp.where(head0, dqc[:t], dqc[t:]) * SQ_SCALE).astype(BF16)

        @pl.when(i == nq - 1)
        def _():
            dk_ref[...] = dk_acc[...].astype(BF16)
            dv_ref[...] = dv_acc[...].astype(BF16)

        pl.when(jnp.logical_and(p == 3, i == nq - 1))(finish)

    tile_spec = lambda c0: pl.BlockSpec((T, LANES), lambda p, i: (i, c0 + p))
    head_spec = lambda c0: pl.BlockSpec((S, LANES), lambda p, i: (0, c0 + p))
    return pl.pallas_call(
        body, name="sb_bwd", grid=(4, nq),
        in_specs=[tile_spec(0), head_spec(4), head_spec(8), tile_spec(0), tile_spec(0), tile_spec(4),
                  pl.BlockSpec((1, 1, SB_CHAINS * (2 * T // SB_CHAINS + 8), 8), lambda p, i: (p, i, 0, 0))] + ex.in_specs,
        out_specs=[tile_spec(0), head_spec(0), head_spec(0), tile_spec(0)] + ex.out_specs,
        out_shape=[jax.ShapeDtypeStruct((S, GROUP_W), BF16)] * 4 + ex.out_shape,
        scratch_shapes=[pltpu.VMEM((S, LANES), F32), pltpu.VMEM((S, LANES), F32)] + ex.scratch,
        compiler_params=_cp("arbitrary", "arbitrary"),
    )(sb, sb, sb, sg, o, dycat, sb_end, *ex.ship)


def rope_tables(S):
    half = RET_HEAD_DIM // 2
    lane = jnp.arange(RET_HEAD_DIM)
    inv = ROPE_BASE ** (-(lane % half).astype(F32) / half)
    ang = jnp.arange(S, dtype=F32)[:, None] * inv[None, :]
    return jnp.cos(ang), jnp.where(lane < half, -1.0, 1.0)[None, :] * jnp.sin(ang)


def ret_log_gamma():
    return jnp.log1p(-(2.0 ** (-5.0 - jnp.arange(4, dtype=F32))))


def _swap_halves(a):
    return pltpu.roll(a, RET_HEAD_DIM // 2, axis=1)


def _ret_decay_mask(lg):
    n = lax.broadcasted_iota(jnp.int32, (RET_T, RET_T), 0)
    m = lax.broadcasted_iota(jnp.int32, (RET_T, RET_T), 1)
    dist = jnp.abs(n - m).astype(F32)
    return jnp.where((m // CHUNK) <= (n // CHUNK), jnp.exp(lg * dist), 0.0)


def _ret_block(lg, rq, rk, rv, cosf, sinf, dm):
    q = rq * cosf + _swap_halves(rq) * sinf
    k = (rk * cosf + _swap_halves(rk) * sinf) * RK_SCALE
    qb, kb, vb = q.astype(BF16), k.astype(BF16), rv.astype(BF16)
    sc = _dot_nt(qb, kb) * dm
    nloc = lax.broadcasted_iota(jnp.int32, (RET_T, 1), 0).astype(F32)
    qdec = jnp.exp(lg * (nloc + 1.0))
    kdec = jnp.exp(lg * (RET_T - 1.0 - nloc))
    block_dec = jnp.exp(jnp.full((1, LANES), lg * RET_T, F32))
    return q, k, qb, kb, vb, sc, qdec, kdec, block_dec


RET_RB = 2


def _ret_specs(S, rb):
    group = lambda c0: pl.BlockSpec((RET_RB * RET_T, GROUP_W), lambda s: (rb(s), c0))
    return group, pl.BlockSpec((RET_RB * RET_T, LANES), lambda s: (rb(s), 0))


def _ret_chains():
    chains = [(h, b) for b in range(RET_RB) for h in range(4)]
    rows = lambda c: (slice(c[1] * RET_T, (c[1] + 1) * RET_T), slice(c[0] * LANES, (c[0] + 1) * LANES))
    tab = lambda ref, c: ref[c[1] * RET_T:(c[1] + 1) * RET_T, :]
    return chains, rows, tab


def _ret_blocks(chains, rows, lg_ref, rq_ref, rk_ref, rv_ref, cosf, sinf, dm_ref):
    blk = {c: _ret_block(lg_ref[c[0]], rq_ref[rows(c)], rk_ref[rows(c)], rv_ref[rows(c)],
                         cosf[c], sinf[c], dm_ref[c[0]]) for c in chains}
    return ({c: blk[c][n] for c in chains} for n in range(9))


def ret_fwd(proj, cosf, sinf, lgam):
    S = proj.shape[0]
    nb = S // RET_T
    group, row_tab = _ret_specs(S, lambda s: s)

    def body(lg_ref, rq_ref, rk_ref, rv_ref, rg_ref, cos_ref, sin_ref, y_ref, o_ref, st_out, st_ref, dm_ref):
        @pl.when(pl.program_id(0) == 0)
        def _():
            st_ref[...] = jnp.zeros_like(st_ref)
            for h in range(4):
                dm_ref[h] = _ret_decay_mask(lg_ref[h])

        chains, rows, tab = _ret_chains()
        cosf, sinf = {c: tab(cos_ref, c) for c in chains}, {c: tab(sin_ref, c) for c in chains}
        q, k, qb, kb, vb, sc, qdec, kdec, block_dec = _ret_blocks(
            chains, rows, lg_ref, rq_ref, rk_ref, rv_ref, cosf, sinf, dm_ref)
        kv = {c: _dot_tn((k[c] * kdec[c]).astype(BF16), vb[c]) for c in chains}
        st = {(h, 0): st_ref[h] for h in range(4)}
        for b in range(RET_RB):
            for h in range(4):
                st[(h, b + 1)] = st[(h, b)] * block_dec[(h, b)] + kv[(h, b)]
        for h, b in chains:
            st_out[h, b] = st[(h, b)]
        for h in range(4):
            st_ref[h] = st[(h, RET_RB)]
        o = {c: _dot(sc[c].astype(BF16), vb[c]) + _dot(qb[c], st[c].astype(BF16)) * qdec[c] for c in chains}
        for c in chains:
            o_ref[rows(c)] = o[c]
        cen = {c: o[c] - _rowmean(o[c]) for c in chains}
        on = {c: cen[c] * lax.rsqrt(_rowmean(cen[c] * cen[c]) + EPS) for c in chains}
        rg = {c: rg_ref[rows(c)] for c in chains}
        for c in chains:
            y_ref[rows(c)] = (on[c] * (rg[c] * _sigmoid(rg[c]))).astype(BF16)

    return pl.pallas_call(
        body, name="ret_fwd", grid=(nb // RET_RB,),
        in_specs=[pl.BlockSpec(memory_space=pltpu.SMEM),
                  group(0), group(1), group(2), group(3), row_tab, row_tab],
        out_specs=[group(0), group(0),
                   pl.BlockSpec((4, RET_RB, LANES, LANES), lambda s: (0, s, 0, 0))],
        out_shape=[jax.ShapeDtypeStruct((S, GROUP_W), BF16),
                   jax.ShapeDtypeStruct((S, GROUP_W), F32),
                   jax.ShapeDtypeStruct((4, nb, LANES, LANES), F32)],
        scratch_shapes=[pltpu.VMEM((4, LANES, LANES), F32), pltpu.VMEM((4, RET_T, RET_T), F32)],
        compiler_params=_cp("arbitrary"),
    )(lgam, proj, proj, proj, proj, cosf, sinf)


def ret_bwd(proj, cosf, sinf, lgam, o, states, dycat):
    S = proj.shape[0]
    nsteps = S // RET_T // RET_RB
    rev = lambda s: nsteps - 1 - s
    group, row_tab = _ret_specs(S, rev)

    def body(lg_ref, rq_ref, rk_ref, rv_ref, rg_ref, cos_ref, sin_ref, o_ref, st_in, dy_ref,
             drq_ref, drk_ref, drv_ref, drg_ref, ds_ref, dm_ref):
        @pl.when(pl.program_id(0) == 0)
        def _():
            ds_ref[...] = jnp.zeros_like(ds_ref)
            for h in range(4):
                dm_ref[h] = _ret_decay_mask(lg_ref[h])

        chains, rows, tab = _ret_chains()
        cosf, sinf = {c: tab(cos_ref, c) for c in chains}, {c: tab(sin_ref, c) for c in chains}
        dms = {c: dm_ref[c[0]] for c in chains}
        q, k, qb, kb, vb, sc, qdec, kdec, block_dec = _ret_blocks(
            chains, rows, lg_ref, rq_ref, rk_ref, rv_ref, cosf, sinf, dm_ref)
        o_v = {c: o_ref[rows(c)] for c in chains}
        cen = {c: o_v[c] - _rowmean(o_v[c]) for c in chains}
        rstd = {c: lax.rsqrt(_rowmean(cen[c] * cen[c]) + EPS) for c in chains}
        on = {c: cen[c] * rstd[c] for c in chains}
        rg = {c: rg_ref[rows(c)] for c in chains}
        sig = {c: _sigmoid(rg[c]) for c in chains}
        dy = {c: dy_ref[rows(c)] for c in chains}
        for c in chains:
            drg_ref[rows(c)] = (dy[c] * on[c] * (sig[c] * (1.0 + rg[c] * (1.0 - sig[c])))).astype(BF16)
        don = {c: dy[c] * (rg[c] * sig[c]) for c in chains}
        do = {c: rstd[c] * (don[c] - _rowmean(don[c]) - on[c] * _rowmean(don[c] * on[c])) for c in chains}
        dob = {c: do[c].astype(BF16) for c in chains}
        dsc = {c: (_dot_nt(dob[c], vb[c]) * dms[c]).astype(BF16) for c in chains}
        st_b = {c: st_in[c[0], c[1]].astype(BF16) for c in chains}
        dst = {c: _dot_tn((q[c] * qdec[c]).astype(BF16), dob[c]) for c in chains}
        dsn = {(h, RET_RB): ds_ref[h] for h in range(4)}
        for b in reversed(range(RET_RB)):
            for h in range(4):
                dsn[(h, b)] = dsn[(h, b + 1)] * block_dec[(h, b)] + dst[(h, b)]
        for h in range(4):
            ds_ref[h] = dsn[(h, 0)]
        dsn_b = {c: dsn[(c[0], c[1] + 1)].astype(BF16) for c in chains}
        dq = {c: _dot(dsc[c], kb[c]) + _dot_nt(dob[c], st_b[c]) * qdec[c] for c in chains}
        dk = {c: (_dot_tn(dsc[c], qb[c]) + _dot_nt(vb[c], dsn_b[c]) * kdec[c]) * RK_SCALE for c in chains}
        dv = {c: _dot_tn(sc[c].astype(BF16), dob[c]) + _dot((k[c] * kdec[c]).astype(BF16), dsn_b[c])
              for c in chains}
        for c in chains:
            drq_ref[rows(c)] = (dq[c] * cosf[c] + _swap_halves(dq[c] * sinf[c])).astype(BF16)
            drk_ref[rows(c)] = (dk[c] * cosf[c] + _swap_halves(dk[c] * sinf[c])).astype(BF16)
            drv_ref[rows(c)] = dv[c].astype(BF16)

    return pl.pallas_call(
        body, name="ret_bwd", grid=(nsteps,),
        in_specs=[pl.BlockSpec(memory_space=pltpu.SMEM),
                  group(0), group(1), group(2), group(3), row_tab, row_tab,
                  group(0), pl.BlockSpec((4, RET_RB, LANES, LANES), lambda s: (0, rev(s), 0, 0)),
                  group(0)],
        out_specs=[group(0)] * 4,
        out_shape=[jax.ShapeDtypeStruct((S, GROUP_W), BF16)] * 4,
        scratch_shapes=[pltpu.VMEM((4, LANES, LANES), F32), pltpu.VMEM((4, RET_T, RET_T), F32)],
        compiler_params=_cp("arbitrary"),
    )(lgam, proj, proj, proj, proj, cosf, sinf, o, states, dycat)


def outproj_fwd(x, vecs, y_ret, y_sb, w_out, head=None, tm=1024):
    S, D = x.shape
    tm = min(tm, S)
    last = list(head or ())

    def body(x_ref, v_ref, yr_ref, ys_ref, w_ref, *refs):
        y = _dot(yr_ref[...], w_ref[0:GROUP_W, :]) + _dot(ys_ref[...], w_ref[GROUP_W:, :])
        xv = x_ref[...] + v_ref[2:3, :] * y
        if not last:
            y_ref, xo_ref = refs
            y_ref[...] = y.astype(BF16)
            xo_ref[...] = xv
            return
        g_ref, t_ref, y_ref, dx_ref, st_ref = refs
        y_ref[...] = y.astype(BF16)

        @pl.when(pl.program_id(0) == 0)
        def _():
            st_ref[...] = jnp.zeros_like(st_ref)

        g = g_ref[0:1, :]
        r = lax.rsqrt(_rowmean(xv * xv) + EPS)
        xn = xv * r
        err = xn * g - t_ref[...]
        dy = err * (1.0 / D)
        dxn = dy * g
        dx_ref[...] = r * (dxn - xn * _rowmean(dxn * xn))
        st_ref[0:1, :] += jnp.sum(dy * xn, axis=0, keepdims=True)
        st_ref[1:2, :] += jnp.sum(err * err, axis=0, keepdims=True)

    row = lambda w: pl.BlockSpec((tm, w), lambda i: (i, 0))
    fixed = pl.BlockSpec((8, D), lambda i: (0, 0))
    return pl.pallas_call(
        body, name="outproj_fwd", grid=(S // tm,),
        in_specs=[row(D), fixed, row(GROUP_W), row(GROUP_W), pl.BlockSpec((D, D), lambda i: (0, 0))]
        + ([fixed, row(D)] if last else []),
        out_specs=[row(D), row(D)] + ([fixed] if last else []),
        out_shape=[jax.ShapeDtypeStruct((S, D), BF16), jax.ShapeDtypeStruct((S, D), F32)]
        + ([jax.ShapeDtypeStruct((8, D), F32)] if last else []),
        compiler_params=_cp("arbitrary"),
    )(x, vecs, y_ret, y_sb, w_out, *last)


def outproj_bwd(dx, y, vecs, y_ret, y_sb, w_out, tm=1024):
    S, D = dx.shape
    tm = min(tm, S)
    n = S // tm

    def body(dx_ref, y_ref, v_ref, yr_ref, ys_ref, w_ref, dyc_ref, dw_ref, st_ref, acc):
        i = pl.program_id(0)

        @pl.when(i == 0)
        def _():
            st_ref[...] = jnp.zeros_like(st_ref)
            acc[...] = jnp.zeros_like(acc)

        dxv = dx_ref[...]
        st_ref[0:1, :] += jnp.sum(dxv * y_ref[...].astype(F32), axis=0, keepdims=True)
        dyy = (dxv * v_ref[2:3, :]).astype(BF16)
        dyc_ref[...] = _dot_nt(dyy, w_ref[...])
        acc[0:GROUP_W, :] += _dot_tn(yr_ref[...], dyy)
        acc[GROUP_W:, :] += _dot_tn(ys_ref[...], dyy)

        @pl.when(i == n - 1)
        def _():
            dw_ref[...] = acc[...].astype(BF16)

    row = lambda w: pl.BlockSpec((tm, w), lambda i: (i, 0))
    fixed = lambda r: pl.BlockSpec((r, D), lambda i: (0, 0))
    return pl.pallas_call(
        body, name="outproj_bwd", grid=(n,),
        in_specs=[row(D), row(D), fixed(8), row(GROUP_W), row(GROUP_W), fixed(D)],
        out_specs=[row(D), fixed(D), fixed(8)],
        out_shape=[jax.ShapeDtypeStruct((S, D), F32), jax.ShapeDtypeStruct((D, D), BF16),
                   jax.ShapeDtypeStruct((8, D), F32)],
        scratch_shapes=[pltpu.VMEM((D, D), F32)],
        compiler_params=_cp("arbitrary"),
    )(dx, y, vecs, y_ret, y_sb, w_out)


def inproj_bwd_x(pieces, w3, x, vecs, dx_res, ship=None, tm=512):
    S, D = x.shape
    n = S // tm
    ex = _Exchange(ship)

    def body(*refs):
        p_refs, (w_ref, x_ref, v_ref, dr_ref), refs = refs[:8], refs[8:12], refs[12:]
        ship_refs, (dx_ref, st_ref), refs = refs[:ex.n_in], refs[ex.n_in:ex.n_in + 2], refs[ex.n_in + 2:]
        start, finish = ex.ops(ship_refs, refs)

        @pl.when(pl.program_id(0) == 0)
        def _():
            st_ref[...] = jnp.zeros_like(st_ref)
            start()

        dh = jnp.zeros((tm, D), F32)
        for k, p_ref in enumerate(p_refs):
            c0 = (k % 2) * GROUP_W
            dh = dh + _dot_nt(p_ref[...], w_ref[k // 2, :, c0:c0 + GROUP_W])
        xv = x_ref[...]
        r = lax.rsqrt(_rowmean(xv * xv) + EPS)
        xn = xv * r
        g, scale1 = v_ref[3:4, :], 1.0 + v_ref[1:2, :]
        st_ref[0:1, :] += jnp.sum(dh, axis=0, keepdims=True)
        dh_xn = dh * xn
        st_ref[1:2, :] += jnp.sum(dh_xn, axis=0, keepdims=True) * g
        st_ref[2:3, :] += jnp.sum(dh_xn, axis=0, keepdims=True) * scale1
        dxn = dh * (g * scale1)
        dx_ref[...] = r * (dxn - xn * _rowmean(dxn * xn)) + dr_ref[...]
        pl.when(pl.program_id(0) == n - 1)(finish)

    row = lambda w: pl.BlockSpec((tm, w), lambda i: (i, 0))
    return pl.pallas_call(
        body, name="inproj_bwd_x", grid=(n,),
        in_specs=[row(GROUP_W)] * 8 + [pl.BlockSpec((N_SHARD, D, SHARD_W), lambda i: (0, 0, 0)),
                                       row(D), pl.BlockSpec((8, D), lambda i: (0, 0)), row(D)] + ex.in_specs,
        out_specs=[row(D), pl.BlockSpec((8, D), lambda i: (0, 0))] + ex.out_specs,
        out_shape=[jax.ShapeDtypeStruct((S, D), F32), jax.ShapeDtypeStruct((8, D), F32)] + ex.out_shape,
        scratch_shapes=ex.scratch,
        compiler_params=_cp("arbitrary"),
    )(*pieces, w3, x, vecs, dx_res, *ex.ship)


def inproj_bwd_w(h, pieces, tm=1024):
    S, D = h.shape
    tm = min(tm, S)
    n = S // tm

    def body(*refs):
        h_ref, p_refs, dw_ref, acc = refs[0], refs[1:9], refs[9], refs[10]
        i = pl.program_id(0)

        @pl.when(i == 0)
        def _():
            acc[...] = jnp.zeros_like(acc)

        hv = h_ref[...]
        for k, p_ref in enumerate(p_refs):
            c0 = (k % 2) * GROUP_W
            acc[k // 2, :, c0:c0 + GROUP_W] += _dot_tn(hv, p_ref[...])

        @pl.when(i == n - 1)
        def _():
            dw_ref[...] = acc[...].astype(BF16)

    row = lambda w: pl.BlockSpec((tm, w), lambda i: (i, 0))
    return pl.pallas_call(
        body, name="inproj_bwd_w", grid=(n,),
        in_specs=[row(D)] + [row(GROUP_W)] * 8,
        out_specs=pl.BlockSpec((N_SHARD, D, SHARD_W), lambda i: (0, 0, 0), pipeline_mode=pl.Buffered(1)),
        out_shape=jax.ShapeDtypeStruct((N_SHARD, D, SHARD_W), BF16),
        scratch_shapes=[pltpu.VMEM((N_SHARD, D, SHARD_W), F32)],
        compiler_params=_cp("arbitrary"),
    )(h, *pieces)


def layer_fwd(x, vecs, w3, w_out, tabs, gather=None, head=None):
    cosf, sinf, lgam = tabs
    ret, sg, h, sb = inproj_fwd(x, vecs, w3)
    y_ret, o_ret, states = ret_fwd(ret, cosf, sinf, lgam)
    y_sb, o_sb, sb_end, *gathered = sb_fwd(sb, sg, gather)
    if callable(w_out):
        w_out = w_out(gathered)
    y, *x_next = outproj_fwd(x, vecs, y_ret, y_sb, w_out, head)
    saved = (x, ret, sg, h, sb, y_ret, o_ret, states, y_sb, o_sb, sb_end, y)
    return (x_next[0] if head is None else x_next), saved, gathered


def _by_shard(dw_out):
    return dw_out.reshape(N_SHARD, D_MODEL // N_SHARD, D_MODEL)


def layer_bwd(dx, saved, vecs, w3, w_out, tabs, later_grads=None):
    cosf, sinf, lgam = tabs
    x, ret, sg, h, sb, y_ret, o_ret, states, y_sb, o_sb, sb_end, y = saved
    dycat, dw_out, st_o = outproj_bwd(dx, y, vecs, y_ret, y_sb, w_out)
    dw_out = _by_shard(dw_out)
    ship = None if later_grads is None else (later_grads[0], dw_out, later_grads[1])
    *d_sb, = sb_bwd(sb, sg, o_sb, sb_end, dycat, ship)
    d_ret = ret_bwd(ret, cosf, sinf, lgam, o_ret, states, dycat)
    pieces = list(d_ret) + d_sb[:4]
    dw_in = inproj_bwd_w(h, pieces)
    dx, st_i, *recv_in = inproj_bwd_x(pieces, w3, x, vecs, dx, None if later_grads is None else (dw_in,))
    dmod = jnp.concatenate([st_i[0:2], st_o[0:1]], axis=0)
    grads = (dw_in, dw_out) if later_grads is None else (recv_in[0], d_sb[4])
    return dx, dmod, st_i[2:3], grads


def _place():
    return lax.axis_index("x"), lax.axis_index("y"), lax.axis_index("c")


def _other_chips(mx, my):
    return [(1 - mx, my), (mx, 1 - my), (1 - mx, 1 - my)]


_ANY = pl.BlockSpec(memory_space=pl.ANY)


_GATHER_SCRATCH = [pltpu.SemaphoreType.DMA((7,)), pltpu.SemaphoreType.DMA((7,)), pltpu.SemaphoreType.DMA(())]


def _gather_ops(x_ref, out_ref, send_sems, recv_sems, local_sem):
    mx, my, mc = _place()
    me, sibling = (mx, my, mc), (mx, my, 1 - mc)
    chips = _other_chips(mx, my)

    def slot(px, py, pc):
        return out_ref.at[4 * px + 2 * py + pc]

    def copy(k, block, to, src=None):
        return pltpu.make_async_remote_copy(
            src_ref=slot(*block) if src is None else src, dst_ref=slot(*block),
            send_sem=send_sems.at[k], recv_sem=recv_sems.at[k], device_id=to, device_id_type=MESH)

    mine = pltpu.make_async_copy(x_ref, slot(*me), local_sem)
    first = [copy(0, me, sibling, src=x_ref)]
    first += [copy(1 + j, me, (*chip, mc), src=x_ref) for j, chip in enumerate(chips)]
    passed = [copy(4 + j, (*chip, mc), sibling) for j, chip in enumerate(chips)]

    def start():
        mine.start()
        for cp in first:
            cp.start()

    def forward():
        for j, chip in enumerate(chips):
            copy(1 + j, (*chip, mc), me).wait_recv()
            passed[j].start()

    def finish():
        copy(0, sibling, me).wait_recv()
        for j, chip in enumerate(chips):
            copy(4 + j, (*chip, 1 - mc), me).wait_recv()
        for cp in first + passed:
            cp.wait_send()
        mine.wait()

    return start, forward, finish


class _Exchange:
    def __init__(self, ship):
        self.ship = list(ship or ())
        self.n_in = len(self.ship)
        self.n_out = 1 if self.ship else 0
        self.rows = [a.shape[1] for a in self.ship]
        self.in_specs = [_ANY] * self.n_in
        self.out_specs = [_ANY] * self.n_out
        self.out_shape = [jax.ShapeDtypeStruct((N_SHARD, sum(self.rows), SHARD_W), BF16)] * self.n_out
        sem = pltpu.SemaphoreType.DMA
        self.scratch = [sem((3,)), sem((3,)), sem(())] * self.n_out

    def ops(self, ship_refs, tail):
        if not self.ship:
            return (lambda: None), (lambda: None)
        recv, send_sems, recv_sems, local_sem = tail
        mx, my, mc = _place()
        my_chip = 2 * mx + my
        chips = _other_chips(mx, my)

        def pieces(s):
            firsts = np.cumsum([0] + self.rows[:-1])
            return [(ref.at[s], int(r0), n) for ref, r0, n in zip(ship_refs, firsts, self.rows)]

        def start():
            for src, r0, n in pieces(my_chip):
                pltpu.make_async_copy(src, recv.at[my_chip, pl.ds(r0, n)], local_sem).start()
            for j, (px, py) in enumerate(chips):
                for src, r0, n in pieces(2 * px + py):
                    pltpu.make_async_remote_copy(
                        src_ref=src, dst_ref=recv.at[my_chip, pl.ds(r0, n)],
                        send_sem=send_sems.at[j], recv_sem=recv_sems.at[j],
                        device_id=(px, py, mc), device_id_type=MESH).start()

        def finish():
            for j, (px, py) in enumerate(chips):
                whole = recv.at[2 * px + py]
                both = pltpu.make_async_remote_copy(
                    src_ref=whole, dst_ref=whole, send_sem=send_sems.at[j], recv_sem=recv_sems.at[j],
                    device_id=(px, py, mc), device_id_type=MESH)
                both.wait_recv()
                both.wait_send()
            pltpu.make_async_copy(recv.at[my_chip], recv.at[my_chip], local_sem).wait()

        return start, finish


def sum_and_swap(recv_a, recv_b, stats, tr=256):
    n, rows_a, cols = recv_a.shape
    na, nb = rows_a // tr, recv_b.shape[1] // tr
    nt = na + nb

    def body(a_ref, b_ref, st_ref, own_ref, sib_ref, stall_ref, slots, send_sems, recv_sem, *gather_sems):
        i = pl.program_id(0)
        mx, my, mc = _place()
        slot = i % 2
        g_start, g_forward, g_finish = _gather_ops(st_ref, stall_ref, *gather_sems)
        pl.when(i == 0)(g_start)
        pl.when(i == nt // 2)(g_forward)

        def push(k, tile):
            return pltpu.make_async_remote_copy(
                src_ref=slots.at[k], dst_ref=sib_ref.at[pl.ds(pl.multiple_of(tile * tr, tr), tr)],
                send_sem=send_sems.at[k], recv_sem=recv_sem, device_id=(mx, my, 1 - mc), device_id_type=MESH)

        pl.when(i >= 2)(lambda: push(slot, i - 2).wait_send())

        def total(r_ref):
            acc = r_ref[0].astype(F32)
            for k in range(1, n):
                acc = acc + r_ref[k].astype(F32)
            own_ref[...] = acc
            slots[slot] = acc

        pl.when(i < na)(lambda: total(a_ref))
        pl.when(i >= na)(lambda: total(b_ref))
        push(slot, i).start()

        @pl.when(i == nt - 1)
        def _():
            push(1 - slot, i - 1).wait_send()
            push(slot, i).wait_send()
            pltpu.make_async_remote_copy(src_ref=sib_ref, dst_ref=sib_ref, send_sem=send_sems.at[0], recv_sem=recv_sem,
                                         device_id=(mx, my, 1 - mc), device_id_type=MESH).wait_recv()
            g_finish()

    return pl.pallas_call(
        body, name="sum_and_swap", grid=(nt,),
        in_specs=[pl.BlockSpec((n, tr, cols), lambda i: (0, jnp.minimum(i, na - 1), 0)),
                  pl.BlockSpec((n, tr, cols), lambda i: (0, jnp.maximum(i - na, 0), 0)), _ANY],
        out_specs=[pl.BlockSpec((tr, cols), lambda i: (i, 0)), _ANY, _ANY],
        out_shape=[jax.ShapeDtypeStruct((nt * tr, cols), F32)] * 2
        + [jax.ShapeDtypeStruct((8,) + stats.shape, stats.dtype)],
        scratch_shapes=[pltpu.VMEM((2, tr, cols), F32), pltpu.SemaphoreType.DMA((2,)), pltpu.SemaphoreType.DMA(())]
        + _GATHER_SCRATCH,
        compiler_params=_cp("arbitrary"),
    )(recv_a, recv_b, stats)


def _adamw(w, g, m, v):
    m = ADAM_B1 * m + (1.0 - ADAM_B1) * g
    v = ADAM_B2 * v + (1.0 - ADAM_B2) * (g * g)
    m_hat = m / (1.0 - ADAM_B1 ** ADAM_STEP)
    v_hat = v / (1.0 - ADAM_B2 ** ADAM_STEP)
    delta = -ADAM_LR * (m_hat / (jnp.sqrt(v_hat) + ADAM_EPS) + ADAM_WD * w)
    return delta, m, v


def adam_slab(p_own, p_sib, w, m, v, row0, name, tr=256):
    L, R, C = w.shape
    nr = R // tr

    def body(a_ref, b_ref, w_ref, m_ref, v_ref, g_out, d_out, m_out, v_out):
        g = a_ref[...] + b_ref[...]
        d, m2, v2 = _adamw(w_ref[0], g, m_ref[0], v_ref[0])
        g_out[0], d_out[0], m_out[0], v_out[0] = g, d, m2, v2

    slab = pl.BlockSpec((tr, C), lambda l, i: (row0 // tr + l * nr + i, 0))
    blk = pl.BlockSpec((1, tr, C), lambda l, i: (l, i, 0))
    return pl.pallas_call(
        body, name=name, grid=(L, nr),
        in_specs=[slab, slab, blk, blk, blk], out_specs=[blk] * 4,
        out_shape=[jax.ShapeDtypeStruct(w.shape, F32)] * 4,
        compiler_params=_cp("arbitrary", "arbitrary"),
    )(p_own, p_sib, w, m, v)


def prologue(c8, w_ada, b_ada, norm_g, win_first):
    L, D, W = w_ada.shape

    def body(c_ref, w_ref, b_ref, g_ref, win_ref, vecs_ref, call_ref, wall_ref, mod_ref, mall_ref, *sems):
        w_start, w_forward, w_finish = _gather_ops(win_ref, wall_ref, *sems[0:3])
        w_start()
        for step in _gather_ops(c_ref, call_ref, *sems[3:6]):
            step()
        cv = call_ref[:, 0, :]
        ca = cv * _sigmoid(cv)
        for l in range(L):
            mod_ref[l * 8:(l + 1) * 8, :] = jnp.dot(ca, w_ref[l], precision=lax.Precision.HIGHEST,
                                                    preferred_element_type=F32)
        for step in _gather_ops(mod_ref, mall_ref, *sems[6:9]):
            step()
        mx, my, mc = _place()
        me = 4 * mx + 2 * my + mc
        rowid = lax.broadcasted_iota(jnp.int32, (L * 8, 1), 0)
        vecs_ref[...] = jnp.zeros_like(vecs_ref)
        for l in range(L):
            parts = [jnp.sum(jnp.where(rowid == l * 8 + me, mall_ref[2 * s + mc], 0.0), axis=0, keepdims=True)
                     for s in range(N_SHARD)]
            mod = jnp.concatenate(parts, axis=1) + b_ref[l:l + 1, :]
            for t in range(3):
                vecs_ref[l, t:t + 1, :] = mod[:, t * D:(t + 1) * D]
            vecs_ref[l, 3:4, :] = g_ref[l:l + 1, :]
        w_forward()
        w_finish()

    vmem = pl.BlockSpec(memory_space=pltpu.VMEM)
    return pl.pallas_call(
        body, name="prologue",
        in_specs=[vmem, vmem, vmem, vmem, _ANY], out_specs=[vmem, vmem, _ANY],
        out_shape=[jax.ShapeDtypeStruct((L, 8, D), F32), jax.ShapeDtypeStruct((8, 8, D), F32),
                   jax.ShapeDtypeStruct((8,) + win_first.shape, win_first.dtype)],
        scratch_shapes=[pltpu.VMEM((L * 8, W), F32), pltpu.VMEM((8, L * 8, W), F32)] + _GATHER_SCRATCH * 3,
        compiler_params=pltpu.CompilerParams(vmem_limit_bytes=VMEM_LIMIT_BYTES),
    )(c8, w_ada, b_ada, norm_g, win_first)


def ada_update(dmods, c_t, w, m, v, tr=256):
    L, D, W = w.shape

    def body(dm_ref, c_ref, w_ref, m_ref, v_ref, g_out, d_out, m_out, v_out):
        mx, my, _ = _place()
        shard = 2 * mx + my
        dm = jnp.zeros((8, W), F32)
        for s in range(N_SHARD):
            dm = dm + jnp.where(shard == s, dm_ref[0, :, s * W:(s + 1) * W], 0.0)
        cv = c_ref[...]
        ca = cv * _sigmoid(cv)
        g = jnp.zeros((tr, W), F32)
        for b in range(8):
            g = g + ca[:, b:b + 1] * dm[b:b + 1, :]
        d, m2, v2 = _adamw(w_ref[0], g, m_ref[0], v_ref[0])
        g_out[0], d_out[0], m_out[0], v_out[0] = g, d, m2, v2

    blk = pl.BlockSpec((1, tr, W), lambda l, i: (l, i, 0))
    return pl.pallas_call(
        body, name="ada_update", grid=(L, D // tr),
        in_specs=[pl.BlockSpec((1, 8, 3 * D), lambda l, i: (l, 0, 0)), pl.BlockSpec((tr, 8), lambda l, i: (i, 0)),
                  blk, blk, blk],
        out_specs=[blk] * 4, out_shape=[jax.ShapeDtypeStruct(w.shape, F32)] * 4,
        compiler_params=_cp("arbitrary", "arbitrary"),
    )(dmods, c_t, w, m, v)


STAT_ROWS = 16


def small_update(stats_all, norm, b_ada, final):
    def body(s_ref, *refs):
        ins, outs = refs[:9], refs[9:]
        tot = s_ref[0]
        for k in range(1, 8):
            tot = tot + s_ref[k]
        g_norm = tot[0:2, :]
        g_final = tot[2:3, :]
        g_b = jnp.concatenate(
            [jnp.concatenate([tot[3 + 3 * l + t:4 + 3 * l + t, :] for t in range(3)], axis=1) for l in range(DEPTH)],
            axis=0)
        for p, g in enumerate((g_norm, g_b, g_final)):
            w_ref, m_ref, v_ref = ins[3 * p:3 * p + 3]
            d, m2, v2 = _adamw(w_ref[...], g, m_ref[...], v_ref[...])
            for o_ref, val in zip(outs[4 * p:4 * p + 4], (g, d, m2, v2)):
                o_ref[...] = val
        loss = (0.5 / D_MODEL) * jnp.sum(tot[9:10, :], axis=1, keepdims=True)
        outs[12][...] = jnp.broadcast_to(loss, (8, LANES))

    shapes = []
    for w, _, _ in (norm, b_ada, final):
        shapes += [jax.ShapeDtypeStruct(w.shape, F32)] * 4
    shapes.append(jax.ShapeDtypeStruct((8, LANES), F32))
    return pl.pallas_call(body, name="small_update", out_shape=shapes)(stats_all, *norm, *b_ada, *final)


def kernel(x, c, norm_g, w_ada, b_ada, w_in, w_out, final_g, loss_target, m_norm_g, m_w_ada, m_b_ada, m_w_in, m_w_out, m_final_g, v_norm_g, v_w_ada, v_b_ada, v_w_in, v_w_out, v_final_g):
    S, D = x.shape[1], x.shape[2]
    mc = lax.axis_index("c")
    out_rows = D // N_SHARD

    def my_half(a, rows):
        return lax.dynamic_slice_in_dim(a, mc * rows, rows, axis=0)

    assert DEPTH == 2
    win = [my_half(w_in[l], D // 2).astype(BF16) for l in range(DEPTH)]
    wout = [my_half(w_out[l], out_rows // 2).astype(BF16) for l in range(DEPTH)]
    rest = [jnp.concatenate(wout, axis=0), win[1]]

    def unpack(gathered):
        outs, w3_second = gathered
        outs = outs.reshape(N_SHARD, 2, DEPTH, out_rows // 2, SHARD_W)
        return outs[:, :, 0].reshape(D, D), (w3_second.reshape(N_SHARD, D, SHARD_W), outs[:, :, 1].reshape(D, D))

    vecs, c_all, w3_first = prologue(jnp.broadcast_to(c, (8, D)), w_ada, b_ada, norm_g, win[0])
    c_all, w3_first = c_all[:, 0, :], w3_first.reshape(N_SHARD, D, SHARD_W)

    tabs = (*rope_tables(S), ret_log_gamma())
    saved = [None] * DEPTH
    h, saved[0], wall = layer_fwd(x[0], vecs[0], w3_first, lambda g: unpack(g)[0], tabs, rest)
    weights = [(w3_first, unpack(wall)[0]), unpack(wall)[1]]
    head = (jnp.broadcast_to(final_g[None, :], (8, D)), loss_target[0])
    (dx, st_loss), saved[1], _ = layer_fwd(h, vecs[1], *weights[1], tabs, head=head)

    dmod, dnorm, grads = [None] * DEPTH, [None] * DEPTH, None
    for l in reversed(range(DEPTH)):
        dx, dmod[l], dnorm[l], grads = layer_bwd(dx, saved[l], vecs[l], *weights[l], tabs, grads)

    stats = jnp.concatenate(dnorm + [st_loss[0:1]] + dmod + [st_loss[1:2], jnp.zeros((STAT_ROWS - 10, D), F32)], axis=0)
    p_own, p_sib, stats_all = sum_and_swap(*grads, stats)
    res_in = adam_slab(p_own, p_sib, w_in, m_w_in, v_w_in, 0, "adam_w_in")
    res_out = adam_slab(p_own, p_sib, w_out, m_w_out, v_w_out, DEPTH * D, "adam_w_out", tr=128)

    dmods = stats_all[:, 3:9, :].reshape(8, DEPTH, 3 * D).transpose(1, 0, 2)
    res_ada = ada_update(dmods, c_all.T, w_ada, m_w_ada, v_w_ada)
    small = small_update(stats_all, (norm_g, m_norm_g, v_norm_g), (b_ada, m_b_ada, v_b_ada),
                         (final_g[None, :], m_final_g[None, :], v_final_g[None, :]))
    res_norm, res_b, res_final = small[0:4], small[4:8], [a[0] for a in small[8:12]]
    loss = small[12][0, 0]

    by_kind = [res_norm, res_ada, res_b, res_in, res_out, res_final]
    outs = [loss, dx[None]]
    for kind in range(4):
        outs += [r[kind] for r in by_kind]
    return tuple(outs)
```

```python
import numpy as np
import jax
import jax.numpy as jnp
from jax import lax
from jax.experimental import pallas as pl
from jax.experimental.pallas import tpu as pltpu

F32, BF16 = jnp.float32, jnp.bfloat16
MESH = pl.DeviceIdType.MESH

D_MODEL = 1024
DEPTH = 2
SHARD_W = 1024
N_SHARD = 4
GROUP_W = 512
LANES = 128
SB_HEAD_DIM = 64
RET_HEAD_DIM = 128
CHUNK = 64
ROPE_BASE = 10000.0
EPS = 1e-6
SQ_SCALE = SB_HEAD_DIM ** -0.5
RK_SCALE = RET_HEAD_DIM ** -0.5
SB_T = 1024
SB_CHAINS = 16
SB_NB = 4
RET_T = 256
EXP_ZERO = -104.0
VMEM_LIMIT_BYTES = 56 * 2 ** 20

ADAM_LR, ADAM_B1, ADAM_B2, ADAM_EPS, ADAM_WD, ADAM_STEP = 0.001, 0.9, 0.999, 1e-08, 0.01, 10


def _cp(*sem):
    return pltpu.CompilerParams(dimension_semantics=sem, vmem_limit_bytes=VMEM_LIMIT_BYTES)


def _dot(a, b):
    return lax.dot_general(a, b, (((1,), (0,)), ((), ())), preferred_element_type=F32)


def _dot_nt(a, b):
    return lax.dot_general(a, b, (((1,), (1,)), ((), ())), preferred_element_type=F32)


def _dot_tn(a, b):
    return lax.dot_general(a, b, (((0,), (0,)), ((), ())), preferred_element_type=F32)


def _running_sum(a, tri):
    return _dot(a.astype(BF16), tri)


def _sigmoid(x):
    return 1.0 / (1.0 + jnp.exp(-x))


def _rowsum(a):
    return jnp.sum(a, axis=1, keepdims=True)


def _rowmean(a):
    return jnp.mean(a, axis=1, keepdims=True)


def inproj_fwd(x, vecs, w3, tm=512):
    S, D = x.shape

    def body(x_ref, v_ref, w_ref, ret_ref, sg_ref, h_ref, sb_ref):
        xv = x_ref[...]
        r = lax.rsqrt(_rowmean(xv * xv) + EPS)
        h = xv * r * v_ref[3:4, :] * (1.0 + v_ref[1:2, :]) + v_ref[0:1, :]
        hb = h.astype(BF16)
        h_ref[...] = hb
        for s in range(N_SHARD):
            p = _dot(hb, w_ref[s])
            if s < 2:
                ret_ref[:, s * SHARD_W:(s + 1) * SHARD_W] = p
            if s == 2:
                sb_ref[:, 0:GROUP_W] = (p[:, 0:GROUP_W] * SQ_SCALE).astype(BF16)
                sb_ref[:, GROUP_W:SHARD_W] = p[:, GROUP_W:].astype(BF16)
            if s == 3:
                sb_ref[:, SHARD_W:SHARD_W + GROUP_W] = p[:, 0:GROUP_W].astype(BF16)
                sg_ref[...] = p[:, GROUP_W:]

    row = lambda w: pl.BlockSpec((tm, w), lambda i: (i, 0))
    return pl.pallas_call(
        body, name="inproj_fwd", grid=(S // tm,),
        in_specs=[row(D), pl.BlockSpec((8, D), lambda i: (0, 0)),
                  pl.BlockSpec((N_SHARD, D, SHARD_W), lambda i: (0, 0, 0))],
        out_specs=[row(2 * SHARD_W), row(GROUP_W), row(D), row(3 * GROUP_W)],
        out_shape=[jax.ShapeDtypeStruct((S, 2 * SHARD_W), F32), jax.ShapeDtypeStruct((S, GROUP_W), F32),
                   jax.ShapeDtypeStruct((S, D), BF16), jax.ShapeDtypeStruct((S, 3 * GROUP_W), BF16)],
        compiler_params=_cp("arbitrary"),
    )(x, vecs, w3)


def _sb_logits(qh, k2, keep):
    z = _dot_nt(qh, k2)
    sp = jnp.log(1.0 + jnp.exp(-jnp.abs(z)))
    lb = jnp.minimum(z, 0.0) - sp
    lk = lb - z
    if keep is not None:
        lk = jnp.where(keep, lk, 0.0)
    return lb, lk


class _sb_chains:
    def __init__(self, i, q2, do_b=None):
        t = self.t = SB_T // SB_CHAINS
        self.C = range(SB_CHAINS)
        r = lax.broadcasted_iota(jnp.int32, (SB_NB * t, SB_NB * t), 0)
        c = lax.broadcasted_iota(jnp.int32, (SB_NB * t, SB_NB * t), 1)
        self.later_all = jnp.where(r > c, 1.0, 0.0).astype(BF16)
        self.earlier_all = jnp.where(r < c, 1.0, 0.0).astype(BF16)
        self.later, self.earlier = self.later_all[:t, :t], self.earlier_all[:t, :t]
        self.head0 = lax.broadcasted_iota(jnp.int32, (1, LANES), 1) < SB_HEAD_DIM
        row = lax.broadcasted_iota(jnp.int32, (2 * t, SB_NB * t), 0) & (t - 1)
        col = lax.broadcasted_iota(jnp.int32, (2 * t, SB_NB * t), 1)
        qt = [SB_CHAINS * i + cc for cc in self.C]
        self.first = [jnp.maximum(qt[cc] - (SB_NB - 1), 0) for cc in self.C]
        self.keep = [self.first[cc] * t + col < qt[cc] * t + row for cc in self.C]
        self.qs = [self._stack(q2[cc * t:(cc + 1) * t]) for cc in self.C]
        if do_b is not None:
            self.dos = [self._stack(do_b[cc * t:(cc + 1) * t]) for cc in self.C]

    def _stack(self, a):
        zero = jnp.zeros_like(a)
        return jnp.concatenate([jnp.where(self.head0, a, zero), jnp.where(self.head0, zero, a)], axis=0)

    def rows(self, ref, j, n):
        return ref[pl.ds(pl.multiple_of(j * self.t, self.t), n * self.t), :]

    def suffix(self, lk):
        return _running_sum(lk, self.later_all), _rowsum(lk)

    def prefix(self, g, G0):
        return _running_sum(g, self.earlier_all) + G0


def sb_fwd(sb, sg, gather=None):
    S = sb.shape[0]
    T = SB_T
    nq = S // T
    carried = list(gather or ())
    ng = len(carried)

    def body(*refs):
        (q_ref, k_ref, v_ref, sg_ref), refs = refs[:4], refs[4:]
        x_refs, (y_ref, o_ref, end_ref), out_refs, sems = refs[:ng], refs[ng:ng + 3], refs[ng + 3:2 * ng + 3], refs[2 * ng + 3:]
        p, i = pl.program_id(0), pl.program_id(1)
        gathers = [_gather_ops(x_refs[g], out_refs[g], *sems[3 * g:3 * g + 3]) for g in range(ng)]
        for start, forward, _ in gathers:
            pl.when(jnp.logical_and(p == 0, i == 0))(start)
            pl.when(jnp.logical_and(p == 3, i == 0))(forward)
        ch = _sb_chains(i, q_ref[...])
        later, head0 = ch.later, ch.head0
        lbk = [_sb_logits(ch.qs[c], ch.rows(k_ref, ch.first[c], SB_NB), ch.keep[c]) for c in ch.C]
        suffix, R = zip(*[ch.suffix(lbk[c][1]) for c in ch.C])
        aa = [jnp.where(ch.keep[c], jnp.exp(lbk[c][0] + suffix[c]), 0.0) for c in ch.C]
        acc = [_dot(aa[c].astype(BF16), ch.rows(v_ref, ch.first[c], SB_NB)) for c in ch.C]

        nc = len(ch.C)

        def alive(n, Rs):
            m = None
            for c in ch.C:
                rc = jnp.where(ch.first[c] - n > 0, Rs[c], EXP_ZERO)
                m = rc if m is None else jnp.maximum(m, rc)
            return jnp.max(m)

        def cond(st):
            return st[-1] > EXP_ZERO

        def step(st):
            n, accs, Rs = st[0], list(st[1:1 + nc]), list(st[1 + nc:1 + 2 * nc])
            for c in ch.C:
                j = ch.first[c] - 1 - n
                jc = jnp.maximum(j, 0)
                lb, lk = _sb_logits(ch.qs[c], ch.rows(k_ref, jc, 1), None)
                a = jnp.exp(lb + _running_sum(lk, later) + Rs[c])
                cx = _dot(a.astype(BF16), ch.rows(v_ref, jc, 1))
                accs[c] = jnp.where(j >= 0, accs[c] + cx, accs[c])
                Rs[c] = jnp.where(j >= 0, Rs[c] + _rowsum(lk), Rs[c])
            return (n + 1, *accs, *Rs, alive(n + 1, Rs))

        st = lax.while_loop(cond, step, (jnp.int32(0), *acc, *R, alive(0, R)))
        n_end, acc, R = st[0], st[1:1 + nc], st[1 + nc:1 + 2 * nc]
        outs = []
        for c in ch.C:
            base = c * (2 * ch.t + 8)
            end_ref[0, 0, base:base + 2 * ch.t, :] = jnp.broadcast_to(R[c], (2 * ch.t, 8))
            end_ref[0, 0, base + 2 * ch.t:base + 2 * ch.t + 8, :] = jnp.full((8, 8), n_end.astype(F32))
            outs.append(jnp.where(head0, acc[c][:ch.t], acc[c][ch.t:]))
        o = jnp.concatenate(outs, axis=0)
        o_ref[...] = o
        sg = sg_ref[...]
        y_ref[...] = (o * (sg * _sigmoid(sg))).astype(BF16)
        for _, _, finish in gathers:
            pl.when(jnp.logical_and(p == 3, i == nq - 1))(finish)

    return pl.pallas_call(
        body, name="sb_fwd", grid=(4, nq),
        in_specs=[pl.BlockSpec((T, LANES), lambda p, i: (i, p)),
                  pl.BlockSpec((S, LANES), lambda p, i: (0, 4 + p)),
                  pl.BlockSpec((S, LANES), lambda p, i: (0, 8 + p)),
                  pl.BlockSpec((T, LANES), lambda p, i: (i, p))] + [_ANY for _ in carried],
        out_specs=[pl.BlockSpec((T, LANES), lambda p, i: (i, p)),
                   pl.BlockSpec((T, LANES), lambda p, i: (i, p)),
                   pl.BlockSpec((1, 1, SB_CHAINS * (2 * T // SB_CHAINS + 8), 8), lambda p, i: (p, i, 0, 0))] + [_ANY for _ in carried],
        out_shape=[jax.ShapeDtypeStruct((S, GROUP_W), BF16),
                   jax.ShapeDtypeStruct((S, GROUP_W), F32),
                   jax.ShapeDtypeStruct((4, nq, SB_CHAINS * (2 * T // SB_CHAINS + 8), 8), F32)]
        + [jax.ShapeDtypeStruct((8,) + a.shape, a.dtype) for a in carried],
        scratch_shapes=_GATHER_SCRATCH * ng,
        compiler_params=_cp("arbitrary", "arbitrary"),
    )(sb, sb, sb, sg, *carried)


def sb_bwd(sb, sg, o, sb_end, dycat, ship=None):
    S = sb.shape[0]
    T = SB_T
    nq = S // T
    ex = _Exchange(ship)

    def body(*refs):
        (q_ref, k_ref, v_ref, sg_ref, o_ref, dy_ref, end_ref), refs = refs[:7], refs[7:]
        ship_refs, (dq_ref, dk_ref, dv_ref, dsg_ref), refs = refs[:ex.n_in], refs[ex.n_in:ex.n_in + 4], refs[ex.n_in + 4:]
        recv, (dk_acc, dv_acc), sems = refs[:ex.n_out], refs[ex.n_out:ex.n_out + 2], refs[ex.n_out + 2:]
        start, finish = ex.ops(ship_refs, recv + sems)
        p, i = pl.program_id(0), pl.program_id(1)
        pl.when(jnp.logical_and(p == 0, i == 0))(start)

        @pl.when(i == 0)
        def _():
            dk_acc[...] = jnp.zeros_like(dk_acc)
            dv_acc[...] = jnp.zeros_like(dv_acc)

        sg = sg_ref[...]
        sig = _sigmoid(sg)
        dy = dy_ref[...]
        dsg_ref[...] = (dy * o_ref[...] * (sig * (1.0 + sg * (1.0 - sig)))).astype(BF16)
        do_b = (dy * (sg * sig)).astype(BF16)
        ch = _sb_chains(i, q_ref[...], do_b)
        later, earlier, head0, t = ch.later, ch.earlier, ch.head0, ch.t
        end = end_ref[0, 0]

        def grads(c, j, n, a, lb, g, G, keep):
            dz = g - jnp.exp(lb) * (g + G)
            if keep is not None:
                dz = jnp.where(keep, dz, 0.0)
            dzb = dz.astype(BF16)
            rows = pl.ds(pl.multiple_of(j * t, t), n * t)
            dk_acc[rows, :] += _dot_tn(dzb, ch.qs[c])
            dv_acc[rows, :] += _dot_tn(a.astype(BF16), ch.dos[c])
            return _dot(dzb, ch.rows(k_ref, j, n))

        nc = len(ch.C)
        n_end = jnp.max(end[2 * t:2 * t + 8, :]).astype(jnp.int32)

        def sweep(m, st):
            dqs, G0s, lefts = list(st[:nc]), list(st[nc:2 * nc]), list(st[2 * nc:])
            for c in ch.C:
                j = ch.first[c] - n_end + m
                jc = jnp.maximum(j, 0)
                lb, lk = _sb_logits(ch.qs[c], ch.rows(k_ref, jc, 1), None)
                stick = lefts[c] - _rowsum(lk)
                a = jnp.where(j >= 0, jnp.exp(lb + _running_sum(lk, later) + stick), 0.0)
                g = a * _dot_nt(ch.dos[c], ch.rows(v_ref, jc, 1))
                G = _running_sum(g, earlier) + G0s[c]
                dqs[c] = dqs[c] + grads(c, jc, 1, a, lb, jnp.where(j >= 0, g, 0.0), jnp.where(j >= 0, G, 0.0), None)
                G0s[c] = G0s[c] + _rowsum(g)
                lefts[c] = jnp.where(j >= 0, stick, lefts[c])
            return (*dqs, *G0s, *lefts)

        lefts = [end[c * (2 * t + 8):c * (2 * t + 8) + 2 * t, 0:1] for c in ch.C]
        st = lax.fori_loop(0, n_end, sweep, (*[jnp.zeros((2 * t, LANES), F32)] * nc,
                                             *[jnp.zeros((2 * t, 1), F32)] * nc, *lefts))
        dq, G0 = st[:nc], st[nc:2 * nc]

        lbk = [_sb_logits(ch.qs[c], ch.rows(k_ref, ch.first[c], SB_NB), ch.keep[c]) for c in ch.C]
        suffix = [ch.suffix(lbk[c][1])[0] for c in ch.C]
        aa = [jnp.where(ch.keep[c], jnp.exp(lbk[c][0] + suffix[c]), 0.0) for c in ch.C]
        g = [aa[c] * _dot_nt(ch.dos[c], ch.rows(v_ref, ch.first[c], SB_NB)) for c in ch.C]
        G = [ch.prefix(g[c], G0[c]) for c in ch.C]
        for c in ch.C:
            dqc = dq[c] + grads(c, ch.first[c], SB_NB, aa[c], lbk[c][0], g[c], G[c], ch.keep[c])
            dq_ref[c * t:(c + 1) * t, :] = (jnp.where(head0, dqc[:t], dqc[t:]) * SQ_SCALE).astype(BF16)

        @pl.when(i == nq - 1)
        def _():
            dk_ref[...] = dk_acc[...].astype(BF16)
            dv_ref[...] = dv_acc[...].astype(BF16)

        pl.when(jnp.logical_and(p == 3, i == nq - 1))(finish)

    tile_spec = lambda c0: pl.BlockSpec((T, LANES), lambda p, i: (i, c0 + p))
    head_spec = lambda c0: pl.BlockSpec((S, LANES), lambda p, i: (0, c0 + p))
    return pl.pallas_call(
        body, name="sb_bwd", grid=(4, nq),
        in_specs=[tile_spec(0), head_spec(4), head_spec(8), tile_spec(0), tile_spec(0), tile_spec(4),
                  pl.BlockSpec((1, 1, SB_CHAINS * (2 * T // SB_CHAINS + 8), 8), lambda p, i: (p, i, 0, 0))] + ex.in_specs,
        out_specs=[tile_spec(0), head_spec(0), head_spec(0), tile_spec(0)] + ex.out_specs,
        out_shape=[jax.ShapeDtypeStruct((S, GROUP_W), BF16)] * 4 + ex.out_shape,
        scratch_shapes=[pltpu.VMEM((S, LANES), F32), pltpu.VMEM((S, LANES), F32)] + ex.scratch,
        compiler_params=_cp("arbitrary", "arbitrary"),
    )(sb, sb, sb, sg, o, dycat, sb_end, *ex.ship)


def rope_tables(S):
    half = RET_HEAD_DIM // 2
    lane = jnp.arange(RET_HEAD_DIM)
    inv = ROPE_BASE ** (-(lane % half).astype(F32) / half)
    ang = jnp.arange(S, dtype=F32)[:, None] * inv[None, :]
    return jnp.cos(ang), jnp.where(lane < half, -1.0, 1.0)[None, :] * jnp.sin(ang)


def ret_log_gamma():
    return jnp.log1p(-(2.0 ** (-5.0 - jnp.arange(4, dtype=F32))))


def _swap_halves(a):
    return pltpu.roll(a, RET_HEAD_DIM // 2, axis=1)


def _ret_decay_mask(lg):
    n = lax.broadcasted_iota(jnp.int32, (RET_T, RET_T), 0)
    m = lax.broadcasted_iota(jnp.int32, (RET_T, RET_T), 1)
    dist = jnp.abs(n - m).astype(F32)
    return jnp.where((m // CHUNK) <= (n // CHUNK), jnp.exp(lg * dist), 0.0)


def _ret_block(lg, rq, rk, rv, cosf, sinf, dm):
    q = rq * cosf + _swap_halves(rq) * sinf
    k = (rk * cosf + _swap_halves(rk) * sinf) * RK_SCALE
    qb, kb, vb = q.astype(BF16), k.astype(BF16), rv.astype(BF16)
    sc = _dot_nt(qb, kb) * dm
    nloc = lax.broadcasted_iota(jnp.int32, (RET_T, 1), 0).astype(F32)
    qdec = jnp.exp(lg * (nloc + 1.0))
    kdec = jnp.exp(lg * (RET_T - 1.0 - nloc))
    block_dec = jnp.exp(jnp.full((1, LANES), lg * RET_T, F32))
    return q, k, qb, kb, vb, sc, qdec, kdec, block_dec


RET_RB = 2


def _ret_specs(S, rb):
    group = lambda c0: pl.BlockSpec((RET_RB * RET_T, GROUP_W), lambda s: (rb(s), c0))
    return group, pl.BlockSpec((RET_RB * RET_T, LANES), lambda s: (rb(s), 0))


def _ret_chains():
    chains = [(h, b) for b in range(RET_RB) for h in range(4)]
    rows = lambda c: (slice(c[1] * RET_T, (c[1] + 1) * RET_T), slice(c[0] * LANES, (c[0] + 1) * LANES))
    tab = lambda ref, c: ref[c[1] * RET_T:(c[1] + 1) * RET_T, :]
    return chains, rows, tab


def _ret_blocks(chains, rows, lg_ref, rq_ref, rk_ref, rv_ref, cosf, sinf, dm_ref):
    blk = {c: _ret_block(lg_ref[c[0]], rq_ref[rows(c)], rk_ref[rows(c)], rv_ref[rows(c)],
                         cosf[c], sinf[c], dm_ref[c[0]]) for c in chains}
    return ({c: blk[c][n] for c in chains} for n in range(9))


def ret_fwd(proj, cosf, sinf, lgam):
    S = proj.shape[0]
    nb = S // RET_T
    group, row_tab = _ret_specs(S, lambda s: s)

    def body(lg_ref, rq_ref, rk_ref, rv_ref, rg_ref, cos_ref, sin_ref, y_ref, o_ref, st_out, st_ref, dm_ref):
        @pl.when(pl.program_id(0) == 0)
        def _():
            st_ref[...] = jnp.zeros_like(st_ref)
            for h in range(4):
                dm_ref[h] = _ret_decay_mask(lg_ref[h])

        chains, rows, tab = _ret_chains()
        cosf, sinf = {c: tab(cos_ref, c) for c in chains}, {c: tab(sin_ref, c) for c in chains}
        q, k, qb, kb, vb, sc, qdec, kdec, block_dec = _ret_blocks(
            chains, rows, lg_ref, rq_ref, rk_ref, rv_ref, cosf, sinf, dm_ref)
        kv = {c: _dot_tn((k[c] * kdec[c]).astype(BF16), vb[c]) for c in chains}
        st = {(h, 0): st_ref[h] for h in range(4)}
        for b in range(RET_RB):
            for h in range(4):
                st[(h, b + 1)] = st[(h, b)] * block_dec[(h, b)] + kv[(h, b)]
        for h, b in chains:
            st_out[h, b] = st[(h, b)]
        for h in range(4):
            st_ref[h] = st[(h, RET_RB)]
        o = {c: _dot(sc[c].astype(BF16), vb[c]) + _dot(qb[c], st[c].astype(BF16)) * qdec[c] for c in chains}
        for c in chains:
            o_ref[rows(c)] = o[c]
        cen = {c: o[c] - _rowmean(o[c]) for c in chains}
        on = {c: cen[c] * lax.rsqrt(_rowmean(cen[c] * cen[c]) + EPS) for c in chains}
        rg = {c: rg_ref[rows(c)] for c in chains}
        for c in chains:
            y_ref[rows(c)] = (on[c] * (rg[c] * _sigmoid(rg[c]))).astype(BF16)

    return pl.pallas_call(
        body, name="ret_fwd", grid=(nb // RET_RB,),
        in_specs=[pl.BlockSpec(memory_space=pltpu.SMEM),
                  group(0), group(1), group(2), group(3), row_tab, row_tab],
        out_specs=[group(0), group(0),
                   pl.BlockSpec((4, RET_RB, LANES, LANES), lambda s: (0, s, 0, 0))],
        out_shape=[jax.ShapeDtypeStruct((S, GROUP_W), BF16),
                   jax.ShapeDtypeStruct((S, GROUP_W), F32),
                   jax.ShapeDtypeStruct((4, nb, LANES, LANES), F32)],
        scratch_shapes=[pltpu.VMEM((4, LANES, LANES), F32), pltpu.VMEM((4, RET_T, RET_T), F32)],
        compiler_params=_cp("arbitrary"),
    )(lgam, proj, proj, proj, proj, cosf, sinf)


def ret_bwd(proj, cosf, sinf, lgam, o, states, dycat):
    S = proj.shape[0]
    nsteps = S // RET_T // RET_RB
    rev = lambda s: nsteps - 1 - s
    group, row_tab = _ret_specs(S, rev)

    def body(lg_ref, rq_ref, rk_ref, rv_ref, rg_ref, cos_ref, sin_ref, o_ref, st_in, dy_ref,
             drq_ref, drk_ref, drv_ref, drg_ref, ds_ref, dm_ref):
        @pl.when(pl.program_id(0) == 0)
        def _():
            ds_ref[...] = jnp.zeros_like(ds_ref)
            for h in range(4):
                dm_ref[h] = _ret_decay_mask(lg_ref[h])

        chains, rows, tab = _ret_chains()
        cosf, sinf = {c: tab(cos_ref, c) for c in chains}, {c: tab(sin_ref, c) for c in chains}
        dms = {c: dm_ref[c[0]] for c in chains}
        q, k, qb, kb, vb, sc, qdec, kdec, block_dec = _ret_blocks(
            chains, rows, lg_ref, rq_ref, rk_ref, rv_ref, cosf, sinf, dm_ref)
        o_v = {c: o_ref[rows(c)] for c in chains}
        cen = {c: o_v[c] - _rowmean(o_v[c]) for c in chains}
        rstd = {c: lax.rsqrt(_rowmean(cen[c] * cen[c]) + EPS) for c in chains}
        on = {c: cen[c] * rstd[c] for c in chains}
        rg = {c: rg_ref[rows(c)] for c in chains}
        sig = {c: _sigmoid(rg[c]) for c in chains}
        dy = {c: dy_ref[rows(c)] for c in chains}
        for c in chains:
            drg_ref[rows(c)] = (dy[c] * on[c] * (sig[c] * (1.0 + rg[c] * (1.0 - sig[c])))).astype(BF16)
        don = {c: dy[c] * (rg[c] * sig[c]) for c in chains}
        do = {c: rstd[c] * (don[c] - _rowmean(don[c]) - on[c] * _rowmean(don[c] * on[c])) for c in chains}
        dob = {c: do[c].astype(BF16) for c in chains}
        dsc = {c: (_dot_nt(dob[c], vb[c]) * dms[c]).astype(BF16) for c in chains}
        st_b = {c: st_in[c[0], c[1]].astype(BF16) for c in chains}
        dst = {c: _dot_tn((q[c] * qdec[c]).astype(BF16), dob[c]) for c in chains}
        dsn = {(h, RET_RB): ds_ref[h] for h in range(4)}
        for b in reversed(range(RET_RB)):
            for h in range(4):
                dsn[(h, b)] = dsn[(h, b + 1)] * block_dec[(h, b)] + dst[(h, b)]
        for h in range(4):
            ds_ref[h] = dsn[(h, 0)]
        dsn_b = {c: dsn[(c[0], c[1] + 1)].astype(BF16) for c in chains}
        dq = {c: _dot(dsc[c], kb[c]) + _dot_nt(dob[c], st_b[c]) * qdec[c] for c in chains}
        dk = {c: (_dot_tn(dsc[c], qb[c]) + _dot_nt(vb[c], dsn_b[c]) * kdec[c]) * RK_SCALE for c in chains}
        dv = {c: _dot_tn(sc[c].astype(BF16), dob[c]) + _dot((k[c] * kdec[c]).astype(BF16), dsn_b[c])
              for c in chains}
        for c in chains:
            drq_ref[rows(c)] = (dq[c] * cosf[c] + _swap_halves(dq[c] * sinf[c])).astype(BF16)
            drk_ref[rows(c)] = (dk[c] * cosf[c] + _swap_halves(dk[c] * sinf[c])).astype(BF16)
            drv_ref[rows(c)] = dv[c].astype(BF16)

    return pl.pallas_call(
        body, name="ret_bwd", grid=(nsteps,),
        in_specs=[pl.BlockSpec(memory_space=pltpu.SMEM),
                  group(0), group(1), group(2), group(3), row_tab, row_tab,
                  group(0), pl.BlockSpec((4, RET_RB, LANES, LANES), lambda s: (0, rev(s), 0, 0)),
                  group(0)],
        out_specs=[group(0)] * 4,
        out_shape=[jax.ShapeDtypeStruct((S, GROUP_W), BF16)] * 4,
        scratch_shapes=[pltpu.VMEM((4, LANES, LANES), F32), pltpu.VMEM((4, RET_T, RET_T), F32)],
        compiler_params=_cp("arbitrary"),
    )(lgam, proj, proj, proj, proj, cosf, sinf, o, states, dycat)


def outproj_fwd(x, vecs, y_ret, y_sb, w_out, head=None, tm=1024):
    S, D = x.shape
    tm = min(tm, S)
    last = list(head or ())

    def body(x_ref, v_ref, yr_ref, ys_ref, w_ref, *refs):
        y = _dot(yr_ref[...], w_ref[0:GROUP_W, :]) + _dot(ys_ref[...], w_ref[GROUP_W:, :])
        xv = x_ref[...] + v_ref[2:3, :] * y
        if not last:
            y_ref, xo_ref = refs
            y_ref[...] = y.astype(BF16)
            xo_ref[...] = xv
            return
        g_ref, t_ref, y_ref, dx_ref, st_ref = refs
        y_ref[...] = y.astype(BF16)

        @pl.when(pl.program_id(0) == 0)
        def _():
            st_ref[...] = jnp.zeros_like(st_ref)

        g = g_ref[0:1, :]
        r = lax.rsqrt(_rowmean(xv * xv) + EPS)
        xn = xv * r
        err = xn * g - t_ref[...]
        dy = err * (1.0 / D)
        dxn = dy * g
        dx_ref[...] = r * (dxn - xn * _rowmean(dxn * xn))
        st_ref[0:1, :] += jnp.sum(dy * xn, axis=0, keepdims=True)
        st_ref[1:2, :] += jnp.sum(err * err, axis=0, keepdims=True)

    row = lambda w: pl.BlockSpec((tm, w), lambda i: (i, 0))
    fixed = pl.BlockSpec((8, D), lambda i: (0, 0))
    return pl.pallas_call(
        body, name="outproj_fwd", grid=(S // tm,),
        in_specs=[row(D), fixed, row(GROUP_W), row(GROUP_W), pl.BlockSpec((D, D), lambda i: (0, 0))]
        + ([fixed, row(D)] if last else []),
        out_specs=[row(D), row(D)] + ([fixed] if last else []),
        out_shape=[jax.ShapeDtypeStruct((S, D), BF16), jax.ShapeDtypeStruct((S, D), F32)]
        + ([jax.ShapeDtypeStruct((8, D), F32)] if last else []),
        compiler_params=_cp("arbitrary"),
    )(x, vecs, y_ret, y_sb, w_out, *last)


def outproj_bwd(dx, y, vecs, y_ret, y_sb, w_out, tm=1024):
    S, D = dx.shape
    tm = min(tm, S)
    n = S // tm

    def body(dx_ref, y_ref, v_ref, yr_ref, ys_ref, w_ref, dyc_ref, dw_ref, st_ref, acc):
        i = pl.program_id(0)

        @pl.when(i == 0)
        def _():
            st_ref[...] = jnp.zeros_like(st_ref)
            acc[...] = jnp.zeros_like(acc)

        dxv = dx_ref[...]
        st_ref[0:1, :] += jnp.sum(dxv * y_ref[...].astype(F32), axis=0, keepdims=True)
        dyy = (dxv * v_ref[2:3, :]).astype(BF16)
        dyc_ref[...] = _dot_nt(dyy, w_ref[...])
        acc[0:GROUP_W, :] += _dot_tn(yr_ref[...], dyy)
        acc[GROUP_W:, :] += _dot_tn(ys_ref[...], dyy)

        @pl.when(i == n - 1)
        def _():
            dw_ref[...] = acc[...].astype(BF16)

    row = lambda w: pl.BlockSpec((tm, w), lambda i: (i, 0))
    fixed = lambda r: pl.BlockSpec((r, D), lambda i: (0, 0))
    return pl.pallas_call(
        body, name="outproj_bwd", grid=(n,),
        in_specs=[row(D), row(D), fixed(8), row(GROUP_W), row(GROUP_W), fixed(D)],
        out_specs=[row(D), fixed(D), fixed(8)],
        out_shape=[jax.ShapeDtypeStruct((S, D), F32), jax.ShapeDtypeStruct((D, D), BF16),
                   jax.ShapeDtypeStruct((8, D), F32)],
        scratch_shapes=[pltpu.VMEM((D, D), F32)],
        compiler_params=_cp("arbitrary"),
    )(dx, y, vecs, y_ret, y_sb, w_out)


def inproj_bwd_x(pieces, w3, x, vecs, dx_res, ship=None, tm=512):
    S, D = x.shape
    n = S // tm
    ex = _Exchange(ship)

    def body(*refs):
        p_refs, (w_ref, x_ref, v_ref, dr_ref), refs = refs[:8], refs[8:12], refs[12:]
        ship_refs, (dx_ref, st_ref), refs = refs[:ex.n_in], refs[ex.n_in:ex.n_in + 2], refs[ex.n_in + 2:]
        start, finish = ex.ops(ship_refs, refs)

        @pl.when(pl.program_id(0) == 0)
        def _():
            st_ref[...] = jnp.zeros_like(st_ref)
            start()

        dh = jnp.zeros((tm, D), F32)
        for k, p_ref in enumerate(p_refs):
            c0 = (k % 2) * GROUP_W
            dh = dh + _dot_nt(p_ref[...], w_ref[k // 2, :, c0:c0 + GROUP_W])
        xv = x_ref[...]
        r = lax.rsqrt(_rowmean(xv * xv) + EPS)
        xn = xv * r
        g, scale1 = v_ref[3:4, :], 1.0 + v_ref[1:2, :]
        st_ref[0:1, :] += jnp.sum(dh, axis=0, keepdims=True)
        dh_xn = dh * xn
        st_ref[1:2, :] += jnp.sum(dh_xn, axis=0, keepdims=True) * g
        st_ref[2:3, :] += jnp.sum(dh_xn, axis=0, keepdims=True) * scale1
        dxn = dh * (g * scale1)
        dx_ref[...] = r * (dxn - xn * _rowmean(dxn * xn)) + dr_ref[...]
        pl.when(pl.program_id(0) == n - 1)(finish)

    row = lambda w: pl.BlockSpec((tm, w), lambda i: (i, 0))
    return pl.pallas_call(
        body, name="inproj_bwd_x", grid=(n,),
        in_specs=[row(GROUP_W)] * 8 + [pl.BlockSpec((N_SHARD, D, SHARD_W), lambda i: (0, 0, 0)),
                                       row(D), pl.BlockSpec((8, D), lambda i: (0, 0)), row(D)] + ex.in_specs,
        out_specs=[row(D), pl.BlockSpec((8, D), lambda i: (0, 0))] + ex.out_specs,
        out_shape=[jax.ShapeDtypeStruct((S, D), F32), jax.ShapeDtypeStruct((8, D), F32)] + ex.out_shape,
        scratch_shapes=ex.scratch,
        compiler_params=_cp("arbitrary"),
    )(*pieces, w3, x, vecs, dx_res, *ex.ship)


def inproj_bwd_w(h, pieces, tm=1024):
    S, D = h.shape
    tm = min(tm, S)
    n = S // tm

    def body(*refs):
        h_ref, p_refs, dw_ref, acc = refs[0], refs[1:9], refs[9], refs[10]
        i = pl.program_id(0)

        @pl.when(i == 0)
        def _():
            acc[...] = jnp.zeros_like(acc)

        hv = h_ref[...]
        for k, p_ref in enumerate(p_refs):
            c0 = (k % 2) * GROUP_W
            acc[k // 2, :, c0:c0 + GROUP_W] += _dot_tn(hv, p_ref[...])

        @pl.when(i == n - 1)
        def _():
            dw_ref[...] = acc[...].astype(BF16)

    row = lambda w: pl.BlockSpec((tm, w), lambda i: (i, 0))
    return pl.pallas_call(
        body, name="inproj_bwd_w", grid=(n,),
        in_specs=[row(D)] + [row(GROUP_W)] * 8,
        out_specs=pl.BlockSpec((N_SHARD, D, SHARD_W), lambda i: (0, 0, 0), pipeline_mode=pl.Buffered(1)),
        out_shape=jax.ShapeDtypeStruct((N_SHARD, D, SHARD_W), BF16),
        scratch_shapes=[pltpu.VMEM((N_SHARD, D, SHARD_W), F32)],
        compiler_params=_cp("arbitrary"),
    )(h, *pieces)


def layer_fwd(x, vecs, w3, w_out, tabs, gather=None, head=None):
    cosf, sinf, lgam = tabs
    ret, sg, h, sb = inproj_fwd(x, vecs, w3)
    y_ret, o_ret, states = ret_fwd(ret, cosf, sinf, lgam)
    y_sb, o_sb, sb_end, *gathered = sb_fwd(sb, sg, gather)
    if callable(w_out):
        w_out = w_out(gathered)
    y, *x_next = outproj_fwd(x, vecs, y_ret, y_sb, w_out, head)
    saved = (x, ret, sg, h, sb, y_ret, o_ret, states, y_sb, o_sb, sb_end, y)
    return (x_next[0] if head is None else x_next), saved, gathered


def _by_shard(dw_out):
    return dw_out.reshape(N_SHARD, D_MODEL // N_SHARD, D_MODEL)


def layer_bwd(dx, saved, vecs, w3, w_out, tabs, later_grads=None):
    cosf, sinf, lgam = tabs
    x, ret, sg, h, sb, y_ret, o_ret, states, y_sb, o_sb, sb_end, y = saved
    dycat, dw_out, st_o = outproj_bwd(dx, y, vecs, y_ret, y_sb, w_out)
    dw_out = _by_shard(dw_out)
    ship = None if later_grads is None else (later_grads[0], dw_out, later_grads[1])
    *d_sb, = sb_bwd(sb, sg, o_sb, sb_end, dycat, ship)
    d_ret = ret_bwd(ret, cosf, sinf, lgam, o_ret, states, dycat)
    pieces = list(d_ret) + d_sb[:4]
    dw_in = inproj_bwd_w(h, pieces)
    dx, st_i, *recv_in = inproj_bwd_x(pieces, w3, x, vecs, dx, None if later_grads is None else (dw_in,))
    dmod = jnp.concatenate([st_i[0:2], st_o[0:1]], axis=0)
    grads = (dw_in, dw_out) if later_grads is None else (recv_in[0], d_sb[4])
    return dx, dmod, st_i[2:3], grads


def _place():
    return lax.axis_index("x"), lax.axis_index("y"), lax.axis_index("c")


def _other_chips(mx, my):
    return [(1 - mx, my), (mx, 1 - my), (1 - mx, 1 - my)]


_ANY = pl.BlockSpec(memory_space=pl.ANY)


_GATHER_SCRATCH = [pltpu.SemaphoreType.DMA((7,)), pltpu.SemaphoreType.DMA((7,)), pltpu.SemaphoreType.DMA(())]


def _gather_ops(x_ref, out_ref, send_sems, recv_sems, local_sem):
    mx, my, mc = _place()
    me, sibling = (mx, my, mc), (mx, my, 1 - mc)
    chips = _other_chips(mx, my)

    def slot(px, py, pc):
        return out_ref.at[4 * px + 2 * py + pc]

    def copy(k, block, to, src=None):
        return pltpu.make_async_remote_copy(
            src_ref=slot(*block) if src is None else src, dst_ref=slot(*block),
            send_sem=send_sems.at[k], recv_sem=recv_sems.at[k], device_id=to, device_id_type=MESH)

    mine = pltpu.make_async_copy(x_ref, slot(*me), local_sem)
    first = [copy(0, me, sibling, src=x_ref)]
    first += [copy(1 + j, me, (*chip, mc), src=x_ref) for j, chip in enumerate(chips)]
    passed = [copy(4 + j, (*chip, mc), sibling) for j, chip in enumerate(chips)]

    def start():
        mine.start()
        for cp in first:
            cp.start()

    def forward():
        for j, chip in enumerate(chips):
            copy(1 + j, (*chip, mc), me).wait_recv()
            passed[j].start()

    def finish():
        copy(0, sibling, me).wait_recv()
        for j, chip in enumerate(chips):
            copy(4 + j, (*chip, 1 - mc), me).wait_recv()
        for cp in first + passed:
            cp.wait_send()
        mine.wait()

    return start, forward, finish


class _Exchange:
    def __init__(self, ship):
        self.ship = list(ship or ())
        self.n_in = len(self.ship)
        self.n_out = 1 if self.ship else 0
        self.rows = [a.shape[1] for a in self.ship]
        self.in_specs = [_ANY] * self.n_in
        self.out_specs = [_ANY] * self.n_out
        self.out_shape = [jax.ShapeDtypeStruct((N_SHARD, sum(self.rows), SHARD_W), BF16)] * self.n_out
        sem = pltpu.SemaphoreType.DMA
        self.scratch = [sem((3,)), sem((3,)), sem(())] * self.n_out

    def ops(self, ship_refs, tail):
        if not self.ship:
            return (lambda: None), (lambda: None)
        recv, send_sems, recv_sems, local_sem = tail
        mx, my, mc = _place()
        my_chip = 2 * mx + my
        chips = _other_chips(mx, my)

        def pieces(s):
            firsts = np.cumsum([0] + self.rows[:-1])
            return [(ref.at[s], int(r0), n) for ref, r0, n in zip(ship_refs, firsts, self.rows)]

        def start():
            for src, r0, n in pieces(my_chip):
                pltpu.make_async_copy(src, recv.at[my_chip, pl.ds(r0, n)], local_sem).start()
            for j, (px, py) in enumerate(chips):
                for src, r0, n in pieces(2 * px + py):
                    pltpu.make_async_remote_copy(
                        src_ref=src, dst_ref=recv.at[my_chip, pl.ds(r0, n)],
                        send_sem=send_sems.at[j], recv_sem=recv_sems.at[j],
                        device_id=(px, py, mc), device_id_type=MESH).start()

        def finish():
            for j, (px, py) in enumerate(chips):
                whole = recv.at[2 * px + py]
                both = pltpu.make_async_remote_copy(
                    src_ref=whole, dst_ref=whole, send_sem=send_sems.at[j], recv_sem=recv_sems.at[j],
                    device_id=(px, py, mc), device_id_type=MESH)
                both.wait_recv()
                both.wait_send()
            pltpu.make_async_copy(recv.at[my_chip], recv.at[my_chip], local_sem).wait()

        return start, finish


def sum_and_swap(recv_a, recv_b, stats, tr=256):
    n, rows_a, cols = recv_a.shape
    na, nb = rows_a // tr, recv_b.shape[1] // tr
    nt = na + nb

    def body(a_ref, b_ref, st_ref, own_ref, sib_ref, stall_ref, slots, send_sems, recv_sem, *gather_sems):
        i = pl.program_id(0)
        mx, my, mc = _place()
        slot = i % 2
        g_start, g_forward, g_finish = _gather_ops(st_ref, stall_ref, *gather_sems)
        pl.when(i == 0)(g_start)
        pl.when(i == nt // 2)(g_forward)

        def push(k, tile):
            return pltpu.make_async_remote_copy(
                src_ref=slots.at[k], dst_ref=sib_ref.at[pl.ds(pl.multiple_of(tile * tr, tr), tr)],
                send_sem=send_sems.at[k], recv_sem=recv_sem, device_id=(mx, my, 1 - mc), device_id_type=MESH)

        pl.when(i >= 2)(lambda: push(slot, i - 2).wait_send())

        def total(r_ref):
            acc = r_ref[0].astype(F32)
            for k in range(1, n):
                acc = acc + r_ref[k].astype(F32)
            own_ref[...] = acc
            slots[slot] = acc

        pl.when(i < na)(lambda: total(a_ref))
        pl.when(i >= na)(lambda: total(b_ref))
        push(slot, i).start()

        @pl.when(i == nt - 1)
        def _():
            push(1 - slot, i - 1).wait_send()
            push(slot, i).wait_send()
            pltpu.make_async_remote_copy(src_ref=sib_ref, dst_ref=sib_ref, send_sem=send_sems.at[0], recv_sem=recv_sem,
                                         device_id=(mx, my, 1 - mc), device_id_type=MESH).wait_recv()
            g_finish()

    return pl.pallas_call(
        body, name="sum_and_swap", grid=(nt,),
        in_specs=[pl.BlockSpec((n, tr, cols), lambda i: (0, jnp.minimum(i, na - 1), 0)),
                  pl.BlockSpec((n, tr, cols), lambda i: (0, jnp.maximum(i - na, 0), 0)), _ANY],
        out_specs=[pl.BlockSpec((tr, cols), lambda i: (i, 0)), _ANY, _ANY],
        out_shape=[jax.ShapeDtypeStruct((nt * tr, cols), F32)] * 2
        + [jax.ShapeDtypeStruct((8,) + stats.shape, stats.dtype)],
        scratch_shapes=[pltpu.VMEM((2, tr, cols), F32), pltpu.SemaphoreType.DMA((2,)), pltpu.SemaphoreType.DMA(())]
        + _GATHER_SCRATCH,
        compiler_params=_cp("arbitrary"),
    )(recv_a, recv_b, stats)


def _adamw(w, g, m, v):
    m = ADAM_B1 * m + (1.0 - ADAM_B1) * g
    v = ADAM_B2 * v + (1.0 - ADAM_B2) * (g * g)
    m_hat = m / (1.0 - ADAM_B1 ** ADAM_STEP)
    v_hat = v / (1.0 - ADAM_B2 ** ADAM_STEP)
    delta = -ADAM_LR * (m_hat / (jnp.sqrt(v_hat) + ADAM_EPS) + ADAM_WD * w)
    return delta, m, v


def adam_slab(p_own, p_sib, w, m, v, row0, name, tr=256):
    L, R, C = w.shape
    nr = R // tr

    def body(a_ref, b_ref, w_ref, m_ref, v_ref, g_out, d_out, m_out, v_out):
        g = a_ref[...] + b_ref[...]
        d, m2, v2 = _adamw(w_ref[0], g, m_ref[0], v_ref[0])
        g_out[0], d_out[0], m_out[0], v_out[0] = g, d, m2, v2

    slab = pl.BlockSpec((tr, C), lambda l, i: (row0 // tr + l * nr + i, 0))
    blk = pl.BlockSpec((1, tr, C), lambda l, i: (l, i, 0))
    return pl.pallas_call(
        body, name=name, grid=(L, nr),
        in_specs=[slab, slab, blk, blk, blk], out_specs=[blk] * 4,
        out_shape=[jax.ShapeDtypeStruct(w.shape, F32)] * 4,
        compiler_params=_cp("arbitrary", "arbitrary"),
    )(p_own, p_sib, w, m, v)


def prologue(c8, w_ada, b_ada, norm_g, win_first):
    L, D, W = w_ada.shape

    def body(c_ref, w_ref, b_ref, g_ref, win_ref, vecs_ref, call_ref, wall_ref, mod_ref, mall_ref, *sems):
        w_start, w_forward, w_finish = _gather_ops(win_ref, wall_ref, *sems[0:3])
        for step in _gather_ops(c_ref, call_ref, *sems[3:6]):
            step()
        w_start()
        cv = call_ref[:, 0, :]
        ca = cv * _sigmoid(cv)
        for l in range(L):
            mod_ref[l * 8:(l + 1) * 8, :] = jnp.dot(ca, w_ref[l], precision=lax.Precision.HIGHEST,
                                                    preferred_element_type=F32)
        for step in _gather_ops(mod_ref, mall_ref, *sems[6:9]):
            step()
        mx, my, mc = _place()
        me = 4 * mx + 2 * my + mc
        rowid = lax.broadcasted_iota(jnp.int32, (L * 8, 1), 0)
        vecs_ref[...] = jnp.zeros_like(vecs_ref)
        for l in range(L):
            parts = [jnp.sum(jnp.where(rowid == l * 8 + me, mall_ref[2 * s + mc], 0.0), axis=0, keepdims=True)
                     for s in range(N_SHARD)]
            mod = jnp.concatenate(parts, axis=1) + b_ref[l:l + 1, :]
            for t in range(3):
                vecs_ref[l, t:t + 1, :] = mod[:, t * D:(t + 1) * D]
            vecs_ref[l, 3:4, :] = g_ref[l:l + 1, :]
        w_forward()
        w_finish()

    vmem = pl.BlockSpec(memory_space=pltpu.VMEM)
    return pl.pallas_call(
        body, name="prologue",
        in_specs=[vmem, vmem, vmem, vmem, _ANY], out_specs=[vmem, vmem, _ANY],
        out_shape=[jax.ShapeDtypeStruct((L, 8, D), F32), jax.ShapeDtypeStruct((8, 8, D), F32),
                   jax.ShapeDtypeStruct((8,) + win_first.shape, win_first.dtype)],
        scratch_shapes=[pltpu.VMEM((L * 8, W), F32), pltpu.VMEM((8, L * 8, W), F32)] + _GATHER_SCRATCH * 3,
        compiler_params=pltpu.CompilerParams(vmem_limit_bytes=VMEM_LIMIT_BYTES),
    )(c8, w_ada, b_ada, norm_g, win_first)


def ada_update(dmods, c_t, w, m, v, tr=256):
    L, D, W = w.shape

    def body(dm_ref, c_ref, w_ref, m_ref, v_ref, g_out, d_out, m_out, v_out):
        mx, my, _ = _place()
        shard = 2 * mx + my
        dm = jnp.zeros((8, W), F32)
        for s in range(N_SHARD):
            dm = dm + jnp.where(shard == s, dm_ref[0, :, s * W:(s + 1) * W], 0.0)
        cv = c_ref[...]
        ca = cv * _sigmoid(cv)
        g = jnp.zeros((tr, W), F32)
        for b in range(8):
            g = g + ca[:, b:b + 1] * dm[b:b + 1, :]
        d, m2, v2 = _adamw(w_ref[0], g, m_ref[0], v_ref[0])
        g_out[0], d_out[0], m_out[0], v_out[0] = g, d, m2, v2

    blk = pl.BlockSpec((1, tr, W), lambda l, i: (l, i, 0))
    return pl.pallas_call(
        body, name="ada_update", grid=(L, D // tr),
        in_specs=[pl.BlockSpec((1, 8, 3 * D), lambda l, i: (l, 0, 0)), pl.BlockSpec((tr, 8), lambda l, i: (i, 0)),
                  blk, blk, blk],
        out_specs=[blk] * 4, out_shape=[jax.ShapeDtypeStruct(w.shape, F32)] * 4,
        compiler_params=_cp("arbitrary", "arbitrary"),
    )(dmods, c_t, w, m, v)


STAT_ROWS = 16


def small_update(stats_all, norm, b_ada, final):
    def body(s_ref, *refs):
        ins, outs = refs[:9], refs[9:]
        tot = s_ref[0]
        for k in range(1, 8):
            tot = tot + s_ref[k]
        g_norm = tot[0:2, :]
        g_final = tot[2:3, :]
        g_b = jnp.concatenate(
            [jnp.concatenate([tot[3 + 3 * l + t:4 + 3 * l + t, :] for t in range(3)], axis=1) for l in range(DEPTH)],
            axis=0)
        for p, g in enumerate((g_norm, g_b, g_final)):
            w_ref, m_ref, v_ref = ins[3 * p:3 * p + 3]
            d, m2, v2 = _adamw(w_ref[...], g, m_ref[...], v_ref[...])
            for o_ref, val in zip(outs[4 * p:4 * p + 4], (g, d, m2, v2)):
                o_ref[...] = val
        loss = (0.5 / D_MODEL) * jnp.sum(tot[9:10, :], axis=1, keepdims=True)
        outs[12][...] = jnp.broadcast_to(loss, (8, LANES))

    shapes = []
    for w, _, _ in (norm, b_ada, final):
        shapes += [jax.ShapeDtypeStruct(w.shape, F32)] * 4
    shapes.append(jax.ShapeDtypeStruct((8, LANES), F32))
    return pl.pallas_call(body, name="small_update", out_shape=shapes)(stats_all, *norm, *b_ada, *final)


def kernel(x, c, norm_g, w_ada, b_ada, w_in, w_out, final_g, loss_target, m_norm_g, m_w_ada, m_b_ada, m_w_in, m_w_out, m_final_g, v_norm_g, v_w_ada, v_b_ada, v_w_in, v_w_out, v_final_g):
    S, D = x.shape[1], x.shape[2]
    mc = lax.axis_index("c")
    out_rows = D // N_SHARD

    def my_half(a, rows):
        return lax.dynamic_slice_in_dim(a, mc * rows, rows, axis=0)

    assert DEPTH == 2
    win = [my_half(w_in[l], D // 2).astype(BF16) for l in range(DEPTH)]
    wout = [my_half(w_out[l], out_rows // 2).astype(BF16) for l in range(DEPTH)]
    rest = [jnp.concatenate(wout, axis=0), win[1]]

    def unpack(gathered):
        outs, w3_second = gathered
        outs = outs.reshape(N_SHARD, 2, DEPTH, out_rows // 2, SHARD_W)
        return outs[:, :, 0].reshape(D, D), (w3_second.reshape(N_SHARD, D, SHARD_W), outs[:, :, 1].reshape(D, D))

    vecs, c_all, w3_first = prologue(jnp.broadcast_to(c, (8, D)), w_ada, b_ada, norm_g, win[0])
    c_all, w3_first = c_all[:, 0, :], w3_first.reshape(N_SHARD, D, SHARD_W)

    tabs = (*rope_tables(S), ret_log_gamma())
    saved = [None] * DEPTH
    h, saved[0], wall = layer_fwd(x[0], vecs[0], w3_first, lambda g: unpack(g)[0], tabs, rest)
    weights = [(w3_first, unpack(wall)[0]), unpack(wall)[1]]
    head = (jnp.broadcast_to(final_g[None, :], (8, D)), loss_target[0])
    (dx, st_loss), saved[1], _ = layer_fwd(h, vecs[1], *weights[1], tabs, head=head)

    dmod, dnorm, grads = [None] * DEPTH, [None] * DEPTH, None
    for l in reversed(range(DEPTH)):
        dx, dmod[l], dnorm[l], grads = layer_bwd(dx, saved[l], vecs[l], *weights[l], tabs, grads)

    stats = jnp.concatenate(dnorm + [st_loss[0:1]] + dmod + [st_loss[1:2], jnp.zeros((STAT_ROWS - 10, D), F32)], axis=0)
    p_own, p_sib, stats_all = sum_and_swap(*grads, stats)
    res_in = adam_slab(p_own, p_sib, w_in, m_w_in, v_w_in, 0, "adam_w_in")
    res_out = adam_slab(p_own, p_sib, w_out, m_w_out, v_w_out, DEPTH * D, "adam_w_out", tr=128)

    dmods = stats_all[:, 3:9, :].reshape(8, DEPTH, 3 * D).transpose(1, 0, 2)
    res_ada = ada_update(dmods, c_all.T, w_ada, m_w_ada, v_w_ada)
    small = small_update(stats_all, (norm_g, m_norm_g, v_norm_g), (b_ada, m_b_ada, v_b_ada),
                         (final_g[None, :], m_final_g[None, :], v_final_g[None, :]))
    res_norm, res_b, res_final = small[0:4], small[4:8], [a[0] for a in small[8:12]]
    loss = small[12][0, 0]

    by_kind = [res_norm, res_ada, res_b, res_in, res_out, res_final]
    outs = [loss, dx[None]]
    for kind in range(4):
        outs += [r[kind] for r in by_kind]
    return tuple(outs)
```

```python
import numpy as np
import jax
import jax.numpy as jnp
from jax import lax
from jax.experimental import pallas as pl
from jax.experimental.pallas import tpu as pltpu

F32, BF16 = jnp.float32, jnp.bfloat16
MESH = pl.DeviceIdType.MESH

D_MODEL = 1024
DEPTH = 2
SHARD_W = 1024
N_SHARD = 4
GROUP_W = 512
LANES = 128
SB_HEAD_DIM = 64
RET_HEAD_DIM = 128
CHUNK = 64
ROPE_BASE = 10000.0
EPS = 1e-6
SQ_SCALE = SB_HEAD_DIM ** -0.5
RK_SCALE = RET_HEAD_DIM ** -0.5
SB_T = 1024
SB_CHAINS = 16
SB_NB = 4
RET_T = 256
EXP_ZERO = -104.0
VMEM_LIMIT_BYTES = 56 * 2 ** 20

ADAM_LR, ADAM_B1, ADAM_B2, ADAM_EPS, ADAM_WD, ADAM_STEP = 0.001, 0.9, 0.999, 1e-08, 0.01, 10


def _cp(*sem):
    return pltpu.CompilerParams(dimension_semantics=sem, vmem_limit_bytes=VMEM_LIMIT_BYTES)


def _dot(a, b):
    return lax.dot_general(a, b, (((1,), (0,)), ((), ())), preferred_element_type=F32)


def _dot_nt(a, b):
    return lax.dot_general(a, b, (((1,), (1,)), ((), ())), preferred_element_type=F32)


def _dot_tn(a, b):
    return lax.dot_general(a, b, (((0,), (0,)), ((), ())), preferred_element_type=F32)


def _running_sum(a, tri):
    return _dot(a.astype(BF16), tri)


def _sigmoid(x):
    return 1.0 / (1.0 + jnp.exp(-x))


def _rowsum(a):
    return jnp.sum(a, axis=1, keepdims=True)


def _rowmean(a):
    return jnp.mean(a, axis=1, keepdims=True)


def inproj_fwd(x, vecs, w3, tm=512):
    S, D = x.shape

    def body(x_ref, v_ref, w_ref, ret_ref, sg_ref, h_ref, sb_ref):
        xv = x_ref[...]
        r = lax.rsqrt(_rowmean(xv * xv) + EPS)
        h = xv * r * v_ref[3:4, :] * (1.0 + v_ref[1:2, :]) + v_ref[0:1, :]
        hb = h.astype(BF16)
        h_ref[...] = hb
        for s in range(N_SHARD):
            p = _dot(hb, w_ref[s])
            if s < 2:
                ret_ref[:, s * SHARD_W:(s + 1) * SHARD_W] = p
            if s == 2:
                sb_ref[:, 0:GROUP_W] = (p[:, 0:GROUP_W] * SQ_SCALE).astype(BF16)
                sb_ref[:, GROUP_W:SHARD_W] = p[:, GROUP_W:].astype(BF16)
            if s == 3:
                sb_ref[:, SHARD_W:SHARD_W + GROUP_W] = p[:, 0:GROUP_W].astype(BF16)
                sg_ref[...] = p[:, GROUP_W:]

    row = lambda w: pl.BlockSpec((tm, w), lambda i: (i, 0))
    return pl.pallas_call(
        body, name="inproj_fwd", grid=(S // tm,),
        in_specs=[row(D), pl.BlockSpec((8, D), lambda i: (0, 0)),
                  pl.BlockSpec((N_SHARD, D, SHARD_W), lambda i: (0, 0, 0))],
        out_specs=[row(2 * SHARD_W), row(GROUP_W), row(D), row(3 * GROUP_W)],
        out_shape=[jax.ShapeDtypeStruct((S, 2 * SHARD_W), F32), jax.ShapeDtypeStruct((S, GROUP_W), F32),
                   jax.ShapeDtypeStruct((S, D), BF16), jax.ShapeDtypeStruct((S, 3 * GROUP_W), BF16)],
        compiler_params=_cp("arbitrary"),
    )(x, vecs, w3)


def _sb_logits(qh, k2, keep):
    z = _dot_nt(qh, k2)
    sp = jnp.log(1.0 + jnp.exp(-jnp.abs(z)))
    lb = jnp.minimum(z, 0.0) - sp
    lk = lb - z
    if keep is not None:
        lk = jnp.where(keep, lk, 0.0)
    return lb, lk


class _sb_chains:
    def __init__(self, i, q2, do_b=None):
        t = self.t = SB_T // SB_CHAINS
        self.C = range(SB_CHAINS)
        r = lax.broadcasted_iota(jnp.int32, (SB_NB * t, SB_NB * t), 0)
        c = lax.broadcasted_iota(jnp.int32, (SB_NB * t, SB_NB * t), 1)
        self.later_all = jnp.where(r > c, 1.0, 0.0).astype(BF16)
        self.earlier_all = jnp.where(r < c, 1.0, 0.0).astype(BF16)
        self.later, self.earlier = self.later_all[:t, :t], self.earlier_all[:t, :t]
        self.head0 = lax.broadcasted_iota(jnp.int32, (1, LANES), 1) < SB_HEAD_DIM
        row = lax.broadcasted_iota(jnp.int32, (2 * t, SB_NB * t), 0) & (t - 1)
        col = lax.broadcasted_iota(jnp.int32, (2 * t, SB_NB * t), 1)
        qt = [SB_CHAINS * i + cc for cc in self.C]
        self.first = [jnp.maximum(qt[cc] - (SB_NB - 1), 0) for cc in self.C]
        self.keep = [self.first[cc] * t + col < qt[cc] * t + row for cc in self.C]
        self.qs = [self._stack(q2[cc * t:(cc + 1) * t]) for cc in self.C]
        if do_b is not None:
            self.dos = [self._stack(do_b[cc * t:(cc + 1) * t]) for cc in self.C]

    def _stack(self, a):
        zero = jnp.zeros_like(a)
        return jnp.concatenate([jnp.where(self.head0, a, zero), jnp.where(self.head0, zero, a)], axis=0)

    def rows(self, ref, j, n):
        return ref[pl.ds(pl.multiple_of(j * self.t, self.t), n * self.t), :]

    def suffix(self, lk):
        return _running_sum(lk, self.later_all), _rowsum(lk)

    def prefix(self, g, G0):
        return _running_sum(g, self.earlier_all) + G0


def sb_fwd(sb, sg, gather=None):
    S = sb.shape[0]
    T = SB_T
    nq = S // T
    carried = list(gather or ())
    ng = len(carried)

    def body(*refs):
        (q_ref, k_ref, v_ref, sg_ref), refs = refs[:4], refs[4:]
        x_refs, (y_ref, o_ref, end_ref), out_refs, sems = refs[:ng], refs[ng:ng + 3], refs[ng + 3:2 * ng + 3], refs[2 * ng + 3:]
        p, i = pl.program_id(0), pl.program_id(1)
        gathers = [_gather_ops(x_refs[g], out_refs[g], *sems[3 * g:3 * g + 3]) for g in range(ng)]
        for start, forward, _ in gathers:
            pl.when(jnp.logical_and(p == 0, i == 0))(start)
            pl.when(jnp.logical_and(p == 3, i == 0))(forward)
        ch = _sb_chains(i, q_ref[...])
        later, head0 = ch.later, ch.head0
        lbk = [_sb_logits(ch.qs[c], ch.rows(k_ref, ch.first[c], SB_NB), ch.keep[c]) for c in ch.C]
        suffix, R = zip(*[ch.suffix(lbk[c][1]) for c in ch.C])
        aa = [jnp.where(ch.keep[c], jnp.exp(lbk[c][0] + suffix[c]), 0.0) for c in ch.C]
        acc = [_dot(aa[c].astype(BF16), ch.rows(v_ref, ch.first[c], SB_NB)) for c in ch.C]

        nc = len(ch.C)

        def alive(n, Rs):
            m = None
            for c in ch.C:
                rc = jnp.where(ch.first[c] - n > 0, Rs[c], EXP_ZERO)
                m = rc if m is None else jnp.maximum(m, rc)
            return jnp.max(m)

        def cond(st):
            return st[-1] > EXP_ZERO

        def step(st):
            n, accs, Rs = st[0], list(st[1:1 + nc]), list(st[1 + nc:1 + 2 * nc])
            for c in ch.C:
                j = ch.first[c] - 1 - n
                jc = jnp.maximum(j, 0)
                lb, lk = _sb_logits(ch.qs[c], ch.rows(k_ref, jc, 1), None)
                a = jnp.exp(lb + _running_sum(lk, later) + Rs[c])
                cx = _dot(a.astype(BF16), ch.rows(v_ref, jc, 1))
                accs[c] = jnp.where(j >= 0, accs[c] + cx, accs[c])
                Rs[c] = jnp.where(j >= 0, Rs[c] + _rowsum(lk), Rs[c])
            return (n + 1, *accs, *Rs, alive(n + 1, Rs))

        st = lax.while_loop(cond, step, (jnp.int32(0), *acc, *R, alive(0, R)))
        n_end, acc, R = st[0], st[1:1 + nc], st[1 + nc:1 + 2 * nc]
        outs = []
        for c in ch.C:
            base = c * (2 * ch.t + 8)
            end_ref[0, 0, base:base + 2 * ch.t, :] = jnp.broadcast_to(R[c], (2 * ch.t, 8))
            end_ref[0, 0, base + 2 * ch.t:base + 2 * ch.t + 8, :] = jnp.full((8, 8), n_end.astype(F32))
            outs.append(jnp.where(head0, acc[c][:ch.t], acc[c][ch.t:]))
        o = jnp.concatenate(outs, axis=0)
        o_ref[...] = o
        sg = sg_ref[...]
        y_ref[...] = (o * (sg * _sigmoid(sg))).astype(BF16)
        for _, _, finish in gathers:
            pl.when(jnp.logical_and(p == 3, i == nq - 1))(finish)

    return pl.pallas_call(
        body, name="sb_fwd", grid=(4, nq),
        in_specs=[pl.BlockSpec((T, LANES), lambda p, i: (i, p)),
                  pl.BlockSpec((S, LANES), lambda p, i: (0, 4 + p)),
                  pl.BlockSpec((S, LANES), lambda p, i: (0, 8 + p)),
                  pl.BlockSpec((T, LANES), lambda p, i: (i, p))] + [_ANY for _ in carried],
        out_specs=[pl.BlockSpec((T, LANES), lambda p, i: (i, p)),
                   pl.BlockSpec((T, LANES), lambda p, i: (i, p)),
                   pl.BlockSpec((1, 1, SB_CHAINS * (2 * T // SB_CHAINS + 8), 8), lambda p, i: (p, i, 0, 0))] + [_ANY for _ in carried],
        out_shape=[jax.ShapeDtypeStruct((S, GROUP_W), BF16),
                   jax.ShapeDtypeStruct((S, GROUP_W), F32),
                   jax.ShapeDtypeStruct((4, nq, SB_CHAINS * (2 * T // SB_CHAINS + 8), 8), F32)]
        + [jax.ShapeDtypeStruct((8,) + a.shape, a.dtype) for a in carried],
        scratch_shapes=_GATHER_SCRATCH * ng,
        compiler_params=_cp("arbitrary", "arbitrary"),
    )(sb, sb, sb, sg, *carried)


def sb_bwd(sb, sg, o, sb_end, dycat, ship=None):
    S = sb.shape[0]
    T = SB_T
    nq = S // T
    ex = _Exchange(ship)

    def body(*refs):
        (q_ref, k_ref, v_ref, sg_ref, o_ref, dy_ref, end_ref), refs = refs[:7], refs[7:]
        ship_refs, (dq_ref, dk_ref, dv_ref, dsg_ref), refs = refs[:ex.n_in], refs[ex.n_in:ex.n_in + 4], refs[ex.n_in + 4:]
        recv, (dk_acc, dv_acc), sems = refs[:ex.n_out], refs[ex.n_out:ex.n_out + 2], refs[ex.n_out + 2:]
        start, finish = ex.ops(ship_refs, recv + sems)
        p, i = pl.program_id(0), pl.program_id(1)
        pl.when(jnp.logical_and(p == 0, i == 0))(start)

        @pl.when(i == 0)
        def _():
            dk_acc[...] = jnp.zeros_like(dk_acc)
            dv_acc[...] = jnp.zeros_like(dv_acc)

        sg = sg_ref[...]
        sig = _sigmoid(sg)
        dy = dy_ref[...]
        dsg_ref[...] = (dy * o_ref[...] * (sig * (1.0 + sg * (1.0 - sig)))).astype(BF16)
        do_b = (dy * (sg * sig)).astype(BF16)
        ch = _sb_chains(i, q_ref[...], do_b)
        later, earlier, head0, t = ch.later, ch.earlier, ch.head0, ch.t
        end = end_ref[0, 0]

        def grads(c, j, n, a, lb, g, G, keep):
            dz = g - jnp.exp(lb) * (g + G)
            if keep is not None:
                dz = jnp.where(keep, dz, 0.0)
            dzb = dz.astype(BF16)
            rows = pl.ds(pl.multiple_of(j * t, t), n * t)
            dk_acc[rows, :] += _dot_tn(dzb, ch.qs[c])
            dv_acc[rows, :] += _dot_tn(a.astype(BF16), ch.dos[c])
            return _dot(dzb, ch.rows(k_ref, j, n))

        nc = len(ch.C)
        n_end = jnp.max(end[2 * t:2 * t + 8, :]).astype(jnp.int32)

        def sweep(m, st):
            dqs, G0s, lefts = list(st[:nc]), list(st[nc:2 * nc]), list(st[2 * nc:])
            for c in ch.C:
                j = ch.first[c] - n_end + m
                jc = jnp.maximum(j, 0)
                lb, lk = _sb_logits(ch.qs[c], ch.rows(k_ref, jc, 1), None)
                stick = lefts[c] - _rowsum(lk)
                a = jnp.where(j >= 0, jnp.exp(lb + _running_sum(lk, later) + stick), 0.0)
                g = a * _dot_nt(ch.dos[c], ch.rows(v_ref, jc, 1))
                G = _running_sum(g, earlier) + G0s[c]
                dqs[c] = dqs[c] + grads(c, jc, 1, a, lb, jnp.where(j >= 0, g, 0.0), jnp.where(j >= 0, G, 0.0), None)
                G0s[c] = G0s[c] + _rowsum(g)
                lefts[c] = jnp.where(j >= 0, stick, lefts[c])
            return (*dqs, *G0s, *lefts)

        lefts = [end[c * (2 * t + 8):c * (2 * t + 8) + 2 * t, 0:1] for c in ch.C]
        st = lax.fori_loop(0, n_end, sweep, (*[jnp.zeros((2 * t, LANES), F32)] * nc,
                                             *[jnp.zeros((2 * t, 1), F32)] * nc, *lefts))
        dq, G0 = st[:nc], st[nc:2 * nc]

        lbk = [_sb_logits(ch.qs[c], ch.rows(k_ref, ch.first[c], SB_NB), ch.keep[c]) for c in ch.C]
        suffix = [ch.suffix(lbk[c][1])[0] for c in ch.C]
        aa = [jnp.where(ch.keep[c], jnp.exp(lbk[c][0] + suffix[c]), 0.0) for c in ch.C]
        g = [aa[c] * _dot_nt(ch.dos[c], ch.rows(v_ref, ch.first[c], SB_NB)) for c in ch.C]
        G = [ch.prefix(g[c], G0[c]) for c in ch.C]
        for c in ch.C:
            dqc = dq[c] + grads(c, ch.first[c], SB_NB, aa[c], lbk[c][0], g[c], G[c], ch.keep[c])
            dq_ref[c * t:(c + 1) * t, :] = (jnp.where(head0, dqc[:t], dqc[t:]) * SQ_SCALE).astype(BF16)

        @pl.when(i == nq - 1)
        def _():
            dk_ref[...] = dk_acc[...].astype(BF16)
            dv_ref[...] = dv_acc[...].astype(BF16)

        pl.when(jnp.logical_and(p == 3, i == nq - 1))(finish)

    tile_spec = lambda c0: pl.BlockSpec((T, LANES), lambda p, i: (i, c0 + p))
    head_spec = lambda c0: pl.BlockSpec((S, LANES), lambda p, i: (0, c0 + p))
    return pl.pallas_call(
        body, name="sb_bwd", grid=(4, nq),
        in_specs=[tile_spec(0), head_spec(4), head_spec(8), tile_spec(0), tile_spec(0), tile_spec(4),
                  pl.BlockSpec((1, 1, SB_CHAINS * (2 * T // SB_CHAINS + 8), 8), lambda p, i: (p, i, 0, 0))] + ex.in_specs,
        out_specs=[tile_spec(0), head_spec(0), head_spec(0), tile_spec(0)] + ex.out_specs,
        out_shape=[jax.ShapeDtypeStruct((S, GROUP_W), BF16)] * 4 + ex.out_shape,
        scratch_shapes=[pltpu.VMEM((S, LANES), F32), pltpu.VMEM((S, LANES), F32)] + ex.scratch,
        compiler_params=_cp("arbitrary", "arbitrary"),
    )(sb, sb, sb, sg, o, dycat, sb_end, *ex.ship)


def rope_tables(S):
    half = RET_HEAD_DIM // 2
    lane = jnp.arange(RET_HEAD_DIM)
    inv = ROPE_BASE ** (-(lane % half).astype(F32) / half)
    ang = jnp.arange(S, dtype=F32)[:, None] * inv[None, :]
    return jnp.cos(ang), jnp.where(lane < half, -1.0, 1.0)[None, :] * jnp.sin(ang)


def ret_log_gamma():
    return jnp.log1p(-(2.0 ** (-5.0 - jnp.arange(4, dtype=F32))))


def _swap_halves(a):
    return pltpu.roll(a, RET_HEAD_DIM // 2, axis=1)


def _ret_decay_mask(lg):
    n = lax.broadcasted_iota(jnp.int32, (RET_T, RET_T), 0)
    m = lax.broadcasted_iota(jnp.int32, (RET_T, RET_T), 1)
    dist = jnp.abs(n - m).astype(F32)
    return jnp.where((m // CHUNK) <= (n // CHUNK), jnp.exp(lg * dist), 0.0)


def _ret_block(lg, rq, rk, rv, cosf, sinf, dm):
    q = rq * cosf + _swap_halves(rq) * sinf
    k = (rk * cosf + _swap_halves(rk) * sinf) * RK_SCALE
    qb, kb, vb = q.astype(BF16), k.astype(BF16), rv.astype(BF16)
    sc = _dot_nt(qb, kb) * dm
    nloc = lax.broadcasted_iota(jnp.int32, (RET_T, 1), 0).astype(F32)
    qdec = jnp.exp(lg * (nloc + 1.0))
    kdec = jnp.exp(lg * (RET_T - 1.0 - nloc))
    block_dec = jnp.exp(jnp.full((1, LANES), lg * RET_T, F32))
    return q, k, qb, kb, vb, sc, qdec, kdec, block_dec


RET_RB = 2


def _ret_specs(S, rb):
    group = lambda c0: pl.BlockSpec((RET_RB * RET_T, GROUP_W), lambda s: (rb(s), c0))
    return group, pl.BlockSpec((RET_RB * RET_T, LANES), lambda s: (rb(s), 0))


def _ret_chains():
    chains = [(h, b) for b in range(RET_RB) for h in range(4)]
    rows = lambda c: (slice(c[1] * RET_T, (c[1] + 1) * RET_T), slice(c[0] * LANES, (c[0] + 1) * LANES))
    tab = lambda ref, c: ref[c[1] * RET_T:(c[1] + 1) * RET_T, :]
    return chains, rows, tab


def _ret_blocks(chains, rows, lg_ref, rq_ref, rk_ref, rv_ref, cosf, sinf, dm_ref):
    blk = {c: _ret_block(lg_ref[c[0]], rq_ref[rows(c)], rk_ref[rows(c)], rv_ref[rows(c)],
                         cosf[c], sinf[c], dm_ref[c[0]]) for c in chains}
    return ({c: blk[c][n] for c in chains} for n in range(9))


def ret_fwd(proj, cosf, sinf, lgam):
    S = proj.shape[0]
    nb = S // RET_T
    group, row_tab = _ret_specs(S, lambda s: s)

    def body(lg_ref, rq_ref, rk_ref, rv_ref, rg_ref, cos_ref, sin_ref, y_ref, o_ref, st_out, st_ref, dm_ref):
        @pl.when(pl.program_id(0) == 0)
        def _():
            st_ref[...] = jnp.zeros_like(st_ref)
            for h in range(4):
                dm_ref[h] = _ret_decay_mask(lg_ref[h])

        chains, rows, tab = _ret_chains()
        cosf, sinf = {c: tab(cos_ref, c) for c in chains}, {c: tab(sin_ref, c) for c in chains}
        q, k, qb, kb, vb, sc, qdec, kdec, block_dec = _ret_blocks(
            chains, rows, lg_ref, rq_ref, rk_ref, rv_ref, cosf, sinf, dm_ref)
        kv = {c: _dot_tn((k[c] * kdec[c]).astype(BF16), vb[c]) for c in chains}
        st = {(h, 0): st_ref[h] for h in range(4)}
        for b in range(RET_RB):
            for h in range(4):
                st[(h, b + 1)] = st[(h, b)] * block_dec[(h, b)] + kv[(h, b)]
        for h, b in chains:
            st_out[h, b] = st[(h, b)]
        for h in range(4):
            st_ref[h] = st[(h, RET_RB)]
        o = {c: _dot(sc[c].astype(BF16), vb[c]) + _dot(qb[c], st[c].astype(BF16)) * qdec[c] for c in chains}
        for c in chains:
            o_ref[rows(c)] = o[c]
        cen = {c: o[c] - _rowmean(o[c]) for c in chains}
        on = {c: cen[c] * lax.rsqrt(_rowmean(cen[c] * cen[c]) + EPS) for c in chains}
        rg = {c: rg_ref[rows(c)] for c in chains}
        for c in chains:
            y_ref[rows(c)] = (on[c] * (rg[c] * _sigmoid(rg[c]))).astype(BF16)

    return pl.pallas_call(
        body, name="ret_fwd", grid=(nb // RET_RB,),
        in_specs=[pl.BlockSpec(memory_space=pltpu.SMEM),
                  group(0), group(1), group(2), group(3), row_tab, row_tab],
        out_specs=[group(0), group(0),
                   pl.BlockSpec((4, RET_RB, LANES, LANES), lambda s: (0, s, 0, 0))],
        out_shape=[jax.ShapeDtypeStruct((S, GROUP_W), BF16),
                   jax.ShapeDtypeStruct((S, GROUP_W), F32),
                   jax.ShapeDtypeStruct((4, nb, LANES, LANES), F32)],
        scratch_shapes=[pltpu.VMEM((4, LANES, LANES), F32), pltpu.VMEM((4, RET_T, RET_T), F32)],
        compiler_params=_cp("arbitrary"),
    )(lgam, proj, proj, proj, proj, cosf, sinf)


def ret_bwd(proj, cosf, sinf, lgam, o, states, dycat):
    S = proj.shape[0]
    nsteps = S // RET_T // RET_RB
    rev = lambda s: nsteps - 1 - s
    group, row_tab = _ret_specs(S, rev)

    def body(lg_ref, rq_ref, rk_ref, rv_ref, rg_ref, cos_ref, sin_ref, o_ref, st_in, dy_ref,
             drq_ref, drk_ref, drv_ref, drg_ref, ds_ref, dm_ref):
        @pl.when(pl.program_id(0) == 0)
        def _():
            ds_ref[...] = jnp.zeros_like(ds_ref)
            for h in range(4):
                dm_ref[h] = _ret_decay_mask(lg_ref[h])

        chains, rows, tab = _ret_chains()
        cosf, sinf = {c: tab(cos_ref, c) for c in chains}, {c: tab(sin_ref, c) for c in chains}
        dms = {c: dm_ref[c[0]] for c in chains}
        q, k, qb, kb, vb, sc, qdec, kdec, block_dec = _ret_blocks(
            chains, rows, lg_ref, rq_ref, rk_ref, rv_ref, cosf, sinf, dm_ref)
        o_v = {c: o_ref[rows(c)] for c in chains}
        cen = {c: o_v[c] - _rowmean(o_v[c]) for c in chains}
        rstd = {c: lax.rsqrt(_rowmean(cen[c] * cen[c]) + EPS) for c in chains}
        on = {c: cen[c] * rstd[c] for c in chains}
        rg = {c: rg_ref[rows(c)] for c in chains}
        sig = {c: _sigmoid(rg[c]) for c in chains}
        dy = {c: dy_ref[rows(c)] for c in chains}
        for c in chains:
            drg_ref[rows(c)] = (dy[c] * on[c] * (sig[c] * (1.0 + rg[c] * (1.0 - sig[c])))).astype(BF16)
        don = {c: dy[c] * (rg[c] * sig[c]) for c in chains}
        do = {c: rstd[c] * (don[c] - _rowmean(don[c]) - on[c] * _rowmean(don[c] * on[c])) for c in chains}
        dob = {c: do[c].astype(BF16) for c in chains}
        dsc = {c: (_dot_nt(dob[c], vb[c]) * dms[c]).astype(BF16) for c in chains}
        st_b = {c: st_in[c[0], c[1]].astype(BF16) for c in chains}
        dst = {c: _dot_tn((q[c] * qdec[c]).astype(BF16), dob[c]) for c in chains}
        dsn = {(h, RET_RB): ds_ref[h] for h in range(4)}
        for b in reversed(range(RET_RB)):
            for h in range(4):
                dsn[(h, b)] = dsn[(h, b + 1)] * block_dec[(h, b)] + dst[(h, b)]
        for h in range(4):
            ds_ref[h] = dsn[(h, 0)]
        dsn_b = {c: dsn[(c[0], c[1] + 1)].astype(BF16) for c in chains}
        dq = {c: _dot(dsc[c], kb[c]) + _dot_nt(dob[c], st_b[c]) * qdec[c] for c in chains}
        dk = {c: (_dot_tn(dsc[c], qb[c]) + _dot_nt(vb[c], dsn_b[c]) * kdec[c]) * RK_SCALE for c in chains}
        dv = {c: _dot_tn(sc[c].astype(BF16), dob[c]) + _dot((k[c] * kdec[c]).astype(BF16), dsn_b[c])
              for c in chains}
        for c in chains:
            drq_ref[rows(c)] = (dq[c] * cosf[c] + _swap_halves(dq[c] * sinf[c])).astype(BF16)
            drk_ref[rows(c)] = (dk[c] * cosf[c] + _swap_halves(dk[c] * sinf[c])).astype(BF16)
            drv_ref[rows(c)] = dv[c].astype(BF16)

    return pl.pallas_call(
        body, name="ret_bwd", grid=(nsteps,),
        in_specs=[pl.BlockSpec(memory_space=pltpu.SMEM),
                  group(0), group(1), group(2), group(3), row_tab, row_tab,
                  group(0), pl.BlockSpec((4, RET_RB, LANES, LANES), lambda s: (0, rev(s), 0, 0)),
                  group(0)],
        out_specs=[group(0)] * 4,
        out_shape=[jax.ShapeDtypeStruct((S, GROUP_W), BF16)] * 4,
        scratch_shapes=[pltpu.VMEM((4, LANES, LANES), F32), pltpu.VMEM((4, RET_T, RET_T), F32)],
        compiler_params=_cp("arbitrary"),
    )(lgam, proj, proj, proj, proj, cosf, sinf, o, states, dycat)


def outproj_fwd(x, vecs, y_ret, y_sb, w_out, head=None, tm=1024):
    S, D = x.shape
    tm = min(tm, S)
    last = list(head or ())

    def body(x_ref, v_ref, yr_ref, ys_ref, w_ref, *refs):
        y = _dot(yr_ref[...], w_ref[0:GROUP_W, :]) + _dot(ys_ref[...], w_ref[GROUP_W:, :])
        xv = x_ref[...] + v_ref[2:3, :] * y
        if not last:
            y_ref, xo_ref = refs
            y_ref[...] = y.astype(BF16)
            xo_ref[...] = xv
            return
        g_ref, t_ref, y_ref, dx_ref, st_ref = refs
        y_ref[...] = y.astype(BF16)

        @pl.when(pl.program_id(0) == 0)
        def _():
            st_ref[...] = jnp.zeros_like(st_ref)

        g = g_ref[0:1, :]
        r = lax.rsqrt(_rowmean(xv * xv) + EPS)
        xn = xv * r
        err = xn * g - t_ref[...]
        dy = err * (1.0 / D)
        dxn = dy * g
        dx_ref[...] = r * (dxn - xn * _rowmean(dxn * xn))
        st_ref[0:1, :] += jnp.sum(dy * xn, axis=0, keepdims=True)
        st_ref[1:2, :] += jnp.sum(err * err, axis=0, keepdims=True)

    row = lambda w: pl.BlockSpec((tm, w), lambda i: (i, 0))
    fixed = pl.BlockSpec((8, D), lambda i: (0, 0))
    return pl.pallas_call(
        body, name="outproj_fwd", grid=(S // tm,),
        in_specs=[row(D), fixed, row(GROUP_W), row(GROUP_W), pl.BlockSpec((D, D), lambda i: (0, 0))]
        + ([fixed, row(D)] if last else []),
        out_specs=[row(D), row(D)] + ([fixed] if last else []),
        out_shape=[jax.ShapeDtypeStruct((S, D), BF16), jax.ShapeDtypeStruct((S, D), F32)]
        + ([jax.ShapeDtypeStruct((8, D), F32)] if last else []),
        compiler_params=_cp("arbitrary"),
    )(x, vecs, y_ret, y_sb, w_out, *last)


def outproj_bwd(dx, y, vecs, y_ret, y_sb, w_out, tm=1024):
    S, D = dx.shape
    tm = min(tm, S)
    n = S // tm

    def body(dx_ref, y_ref, v_ref, yr_ref, ys_ref, w_ref, dyc_ref, dw_ref, st_ref, acc):
        i = pl.program_id(0)

        @pl.when(i == 0)
        def _():
            st_ref[...] = jnp.zeros_like(st_ref)
            acc[...] = jnp.zeros_like(acc)

        dxv = dx_ref[...]
        st_ref[0:1, :] += jnp.sum(dxv * y_ref[...].astype(F32), axis=0, keepdims=True)
        dyy = (dxv * v_ref[2:3, :]).astype(BF16)
        dyc_ref[...] = _dot_nt(dyy, w_ref[...])
        acc[0:GROUP_W, :] += _dot_tn(yr_ref[...], dyy)
        acc[GROUP_W:, :] += _dot_tn(ys_ref[...], dyy)

        @pl.when(i == n - 1)
        def _():
            dw_ref[...] = acc[...].astype(BF16)

    row = lambda w: pl.BlockSpec((tm, w), lambda i: (i, 0))
    fixed = lambda r: pl.BlockSpec((r, D), lambda i: (0, 0))
    return pl.pallas_call(
        body, name="outproj_bwd", grid=(n,),
        in_specs=[row(D), row(D), fixed(8), row(GROUP_W), row(GROUP_W), fixed(D)],
        out_specs=[row(D), fixed(D), fixed(8)],
        out_shape=[jax.ShapeDtypeStruct((S, D), F32), jax.ShapeDtypeStruct((D, D), BF16),
                   jax.ShapeDtypeStruct((8, D), F32)],
        scratch_shapes=[pltpu.VMEM((D, D), F32)],
        compiler_params=_cp("arbitrary"),
    )(dx, y, vecs, y_ret, y_sb, w_out)


def inproj_bwd_x(pieces, w3, x, vecs, dx_res, ship=None, tm=512):
    S, D = x.shape
    n = S // tm
    ex = _Exchange(ship)

    def body(*refs):
        p_refs, (w_ref, x_ref, v_ref, dr_ref), refs = refs[:8], refs[8:12], refs[12:]
        ship_refs, (dx_ref, st_ref), refs = refs[:ex.n_in], refs[ex.n_in:ex.n_in + 2], refs[ex.n_in + 2:]
        start, finish = ex.ops(ship_refs, refs)

        @pl.when(pl.program_id(0) == 0)
        def _():
            st_ref[...] = jnp.zeros_like(st_ref)
            start()

        dh = jnp.zeros((tm, D), F32)
        for k, p_ref in enumerate(p_refs):
            c0 = (k % 2) * GROUP_W
            dh = dh + _dot_nt(p_ref[...], w_ref[k // 2, :, c0:c0 + GROUP_W])
        xv = x_ref[...]
        r = lax.rsqrt(_rowmean(xv * xv) + EPS)
        xn = xv * r
        g, scale1 = v_ref[3:4, :], 1.0 + v_ref[1:2, :]
        st_ref[0:1, :] += jnp.sum(dh, axis=0, keepdims=True)
        dh_xn = dh * xn
        st_ref[1:2, :] += jnp.sum(dh_xn, axis=0, keepdims=True) * g
        st_ref[2:3, :] += jnp.sum(dh_xn, axis=0, keepdims=True) * scale1
        dxn = dh * (g * scale1)
        dx_ref[...] = r * (dxn - xn * _rowmean(dxn * xn)) + dr_ref[...]
        pl.when(pl.program_id(0) == n - 1)(finish)

    row = lambda w: pl.BlockSpec((tm, w), lambda i: (i, 0))
    return pl.pallas_call(
        body, name="inproj_bwd_x", grid=(n,),
        in_specs=[row(GROUP_W)] * 8 + [pl.BlockSpec((N_SHARD, D, SHARD_W), lambda i: (0, 0, 0)),
                                       row(D), pl.BlockSpec((8, D), lambda i: (0, 0)), row(D)] + ex.in_specs,
        out_specs=[row(D), pl.BlockSpec((8, D), lambda i: (0, 0))] + ex.out_specs,
        out_shape=[jax.ShapeDtypeStruct((S, D), F32), jax.ShapeDtypeStruct((8, D), F32)] + ex.out_shape,
        scratch_shapes=ex.scratch,
        compiler_params=_cp("arbitrary"),
    )(*pieces, w3, x, vecs, dx_res, *ex.ship)


def inproj_bwd_w(h, pieces, tm=1024):
    S, D = h.shape
    tm = min(tm, S)
    n = S // tm

    def body(*refs):
        h_ref, p_refs, dw_ref, acc = refs[0], refs[1:9], refs[9], refs[10]
        i = pl.program_id(0)

        @pl.when(i == 0)
        def _():
            acc[...] = jnp.zeros_like(acc)

        hv = h_ref[...]
        for k, p_ref in enumerate(p_refs):
            c0 = (k % 2) * GROUP_W
            acc[k // 2, :, c0:c0 + GROUP_W] += _dot_tn(hv, p_ref[...])

        @pl.when(i == n - 1)
        def _():
            dw_ref[...] = acc[...].astype(BF16)

    row = lambda w: pl.BlockSpec((tm, w), lambda i: (i, 0))
    return pl.pallas_call(
        body, name="inproj_bwd_w", grid=(n,),
        in_specs=[row(D)] + [row(GROUP_W)] * 8,
        out_specs=pl.BlockSpec((N_SHARD, D, SHARD_W), lambda i: (0, 0, 0), pipeline_mode=pl.Buffered(1)),
        out_shape=jax.ShapeDtypeStruct((N_SHARD, D, SHARD_W), BF16),
        scratch_shapes=[pltpu.VMEM((N_SHARD, D, SHARD_W), F32)],
        compiler_params=_cp("arbitrary"),
    )(h, *pieces)


def layer_fwd(x, vecs, w3, w_out, tabs, gather=None, head=None):
    cosf, sinf, lgam = tabs
    ret, sg, h, sb = inproj_fwd(x, vecs, w3)
    y_ret, o_ret, states = ret_fwd(ret, cosf, sinf, lgam)
    y_sb, o_sb, sb_end, *gathered = sb_fwd(sb, sg, gather)
    if callable(w_out):
        w_out = w_out(gathered)
    y, *x_next = outproj_fwd(x, vecs, y_ret, y_sb, w_out, head)
    saved = (x, ret, sg, h, sb, y_ret, o_ret, states, y_sb, o_sb, sb_end, y)
    return (x_next[0] if head is None else x_next), saved, gathered


def _by_shard(dw_out):
    return dw_out.reshape(N_SHARD, D_MODEL // N_SHARD, D_MODEL)


def layer_bwd(dx, saved, vecs, w3, w_out, tabs, later_grads=None):
    cosf, sinf, lgam = tabs
    x, ret, sg, h, sb, y_ret, o_ret, states, y_sb, o_sb, sb_end, y = saved
    dycat, dw_out, st_o = outproj_bwd(dx, y, vecs, y_ret, y_sb, w_out)
    dw_out = _by_shard(dw_out)
    ship = None if later_grads is None else (later_grads[0], dw_out, later_grads[1])
    *d_sb, = sb_bwd(sb, sg, o_sb, sb_end, dycat, ship)
    d_ret = ret_bwd(ret, cosf, sinf, lgam, o_ret, states, dycat)
    pieces = list(d_ret) + d_sb[:4]
    dw_in = inproj_bwd_w(h, pieces)
    dx, st_i, *recv_in = inproj_bwd_x(pieces, w3, x, vecs, dx, None if later_grads is None else (dw_in,))
    dmod = jnp.concatenate([st_i[0:2], st_o[0:1]], axis=0)
    grads = (dw_in, dw_out) if later_grads is None else (recv_in[0], d_sb[4])
    return dx, dmod, st_i[2:3], grads


def _place():
    return lax.axis_index("x"), lax.axis_index("y"), lax.axis_index("c")


def _other_chips(mx, my):
    return [(1 - mx, my), (mx, 1 - my), (1 - mx, 1 - my)]


_ANY = pl.BlockSpec(memory_space=pl.ANY)


_GATHER_SCRATCH = [pltpu.SemaphoreType.DMA((7,)), pltpu.SemaphoreType.DMA((7,)), pltpu.SemaphoreType.DMA(())]


def _gather_ops(x_ref, out_ref, send_sems, recv_sems, local_sem):
    mx, my, mc = _place()
    me, sibling = (mx, my, mc), (mx, my, 1 - mc)
    chips = _other_chips(mx, my)

    def slot(px, py, pc):
        return out_ref.at[4 * px + 2 * py + pc]

    def copy(k, block, to, src=None):
        return pltpu.make_async_remote_copy(
            src_ref=slot(*block) if src is None else src, dst_ref=slot(*block),
            send_sem=send_sems.at[k], recv_sem=recv_sems.at[k], device_id=to, device_id_type=MESH)

    mine = pltpu.make_async_copy(x_ref, slot(*me), local_sem)
    first = [copy(0, me, sibling, src=x_ref)]
    first += [copy(1 + j, me, (*chip, mc), src=x_ref) for j, chip in enumerate(chips)]
    passed = [copy(4 + j, (*chip, mc), sibling) for j, chip in enumerate(chips)]

    def start():
        mine.start()
        for cp in first:
            cp.start()

    def forward():
        for j, chip in enumerate(chips):
            copy(1 + j, (*chip, mc), me).wait_recv()
            passed[j].start()

    def finish():
        copy(0, sibling, me).wait_recv()
        for j, chip in enumerate(chips):
            copy(4 + j, (*chip, 1 - mc), me).wait_recv()
        for cp in first + passed:
            cp.wait_send()
        mine.wait()

    return start, forward, finish


class _Exchange:
    def __init__(self, ship):
        self.ship = list(ship or ())
        self.n_in = len(self.ship)
        self.n_out = 1 if self.ship else 0
        self.rows = [a.shape[1] for a in self.ship]
        self.in_specs = [_ANY] * self.n_in
        self.out_specs = [_ANY] * self.n_out
        self.out_shape = [jax.ShapeDtypeStruct((N_SHARD, sum(self.rows), SHARD_W), BF16)] * self.n_out
        sem = pltpu.SemaphoreType.DMA
        self.scratch = [sem((3,)), sem((3,)), sem(())] * self.n_out

    def ops(self, ship_refs, tail):
        if not self.ship:
            return (lambda: None), (lambda: None)
        recv, send_sems, recv_sems, local_sem = tail
        mx, my, mc = _place()
        my_chip = 2 * mx + my
        chips = _other_chips(mx, my)

        def pieces(s):
            firsts = np.cumsum([0] + self.rows[:-1])
            return [(ref.at[s], int(r0), n) for ref, r0, n in zip(ship_refs, firsts, self.rows)]

        def start():
            for src, r0, n in pieces(my_chip):
                pltpu.make_async_copy(src, recv.at[my_chip, pl.ds(r0, n)], local_sem).start()
            for j, (px, py) in enumerate(chips):
                for src, r0, n in pieces(2 * px + py):
                    pltpu.make_async_remote_copy(
                        src_ref=src, dst_ref=recv.at[my_chip, pl.ds(r0, n)],
                        send_sem=send_sems.at[j], recv_sem=recv_sems.at[j],
                        device_id=(px, py, mc), device_id_type=MESH).start()

        def finish():
            for j, (px, py) in enumerate(chips):
                whole = recv.at[2 * px + py]
                both = pltpu.make_async_remote_copy(
                    src_ref=whole, dst_ref=whole, send_sem=send_sems.at[j], recv_sem=recv_sems.at[j],
                    device_id=(px, py, mc), device_id_type=MESH)
                both.wait_recv()
                both.wait_send()
            pltpu.make_async_copy(recv.at[my_chip], recv.at[my_chip], local_sem).wait()

        return start, finish


def sum_and_swap(recv_a, recv_b, stats, tr=256):
    n, rows_a, cols = recv_a.shape
    na, nb = rows_a // tr, recv_b.shape[1] // tr
    nt = na + nb

    def body(a_ref, b_ref, st_ref, own_ref, sib_ref, stall_ref, slots, send_sems, recv_sem, *gather_sems):
        i = pl.program_id(0)
        mx, my, mc = _place()
        slot = i % 2
        g_start, g_forward, g_finish = _gather_ops(st_ref, stall_ref, *gather_sems)
        pl.when(i == 0)(g_start)
        pl.when(i == nt // 2)(g_forward)

        def push(k, tile):
            return pltpu.make_async_remote_copy(
                src_ref=slots.at[k], dst_ref=sib_ref.at[pl.ds(pl.multiple_of(tile * tr, tr), tr)],
                send_sem=send_sems.at[k], recv_sem=recv_sem, device_id=(mx, my, 1 - mc), device_id_type=MESH)

        pl.when(i >= 2)(lambda: push(slot, i - 2).wait_send())

        def total(r_ref):
            acc = r_ref[0].astype(F32)
            for k in range(1, n):
                acc = acc + r_ref[k].astype(F32)
            own_ref[...] = acc
            slots[slot] = acc

        pl.when(i < na)(lambda: total(a_ref))
        pl.when(i >= na)(lambda: total(b_ref))
        push(slot, i).start()

        @pl.when(i == nt - 1)
        def _():
            push(1 - slot, i - 1).wait_send()
            push(slot, i).wait_send()
            pltpu.make_async_remote_copy(src_ref=sib_ref, dst_ref=sib_ref, send_sem=send_sems.at[0], recv_sem=recv_sem,
                                         device_id=(mx, my, 1 - mc), device_id_type=MESH).wait_recv()
            g_finish()

    return pl.pallas_call(
        body, name="sum_and_swap", grid=(nt,),
        in_specs=[pl.BlockSpec((n, tr, cols), lambda i: (0, jnp.minimum(i, na - 1), 0)),
                  pl.BlockSpec((n, tr, cols), lambda i: (0, jnp.maximum(i - na, 0), 0)), _ANY],
        out_specs=[pl.BlockSpec((tr, cols), lambda i: (i, 0)), _ANY, _ANY],
        out_shape=[jax.ShapeDtypeStruct((nt * tr, cols), F32)] * 2
        + [jax.ShapeDtypeStruct((8,) + stats.shape, stats.dtype)],
        scratch_shapes=[pltpu.VMEM((2, tr, cols), F32), pltpu.SemaphoreType.DMA((2,)), pltpu.SemaphoreType.DMA(())]
        + _GATHER_SCRATCH,
        compiler_params=_cp("arbitrary"),
    )(recv_a, recv_b, stats)


def _adamw(w, g, m, v):
    m = ADAM_B1 * m + (1.0 - ADAM_B1) * g
    v = ADAM_B2 * v + (1.0 - ADAM_B2) * (g * g)
    m_hat = m / (1.0 - ADAM_B1 ** ADAM_STEP)
    v_hat = v / (1.0 - ADAM_B2 ** ADAM_STEP)
    delta = -ADAM_LR * (m_hat / (jnp.sqrt(v_hat) + ADAM_EPS) + ADAM_WD * w)
    return delta, m, v


def adam_slab(p_own, p_sib, w, m, v, row0, name, tr=512):
    L, R, C = w.shape
    nr = R // tr

    def body(a_ref, b_ref, w_ref, m_ref, v_ref, g_out, d_out, m_out, v_out):
        g = a_ref[...] + b_ref[...]
        d, m2, v2 = _adamw(w_ref[0], g, m_ref[0], v_ref[0])
        g_out[0], d_out[0], m_out[0], v_out[0] = g, d, m2, v2

    slab = pl.BlockSpec((tr, C), lambda l, i: (row0 // tr + l * nr + i, 0))
    blk = pl.BlockSpec((1, tr, C), lambda l, i: (l, i, 0))
    return pl.pallas_call(
        body, name=name, grid=(L, nr),
        in_specs=[slab, slab, blk, blk, blk], out_specs=[blk] * 4,
        out_shape=[jax.ShapeDtypeStruct(w.shape, F32)] * 4,
        compiler_params=_cp("arbitrary", "arbitrary"),
    )(p_own, p_sib, w, m, v)


def prologue(c8, w_ada, b_ada, norm_g, win_first):
    L, D, W = w_ada.shape

    def body(c_ref, w_ref, b_ref, g_ref, win_ref, vecs_ref, call_ref, wall_ref, mod_ref, mall_ref, *sems):
        w_start, w_forward, w_finish = _gather_ops(win_ref, wall_ref, *sems[0:3])
        for step in _gather_ops(c_ref, call_ref, *sems[3:6]):
            step()
        w_start()
        cv = call_ref[:, 0, :]
        ca = cv * _sigmoid(cv)
        for l in range(L):
            mod_ref[l * 8:(l + 1) * 8, :] = jnp.dot(ca, w_ref[l], precision=lax.Precision.HIGHEST,
                                                    preferred_element_type=F32)
        for step in _gather_ops(mod_ref, mall_ref, *sems[6:9]):
            step()
        mx, my, mc = _place()
        me = 4 * mx + 2 * my + mc
        rowid = lax.broadcasted_iota(jnp.int32, (L * 8, 1), 0)
        vecs_ref[...] = jnp.zeros_like(vecs_ref)
        for l in range(L):
            parts = [jnp.sum(jnp.where(rowid == l * 8 + me, mall_ref[2 * s + mc], 0.0), axis=0, keepdims=True)
                     for s in range(N_SHARD)]
            mod = jnp.concatenate(parts, axis=1) + b_ref[l:l + 1, :]
            for t in range(3):
                vecs_ref[l, t:t + 1, :] = mod[:, t * D:(t + 1) * D]
            vecs_ref[l, 3:4, :] = g_ref[l:l + 1, :]
        w_forward()
        w_finish()

    vmem = pl.BlockSpec(memory_space=pltpu.VMEM)
    return pl.pallas_call(
        body, name="prologue",
        in_specs=[vmem, vmem, vmem, vmem, _ANY], out_specs=[vmem, vmem, _ANY],
        out_shape=[jax.ShapeDtypeStruct((L, 8, D), F32), jax.ShapeDtypeStruct((8, 8, D), F32),
                   jax.ShapeDtypeStruct((8,) + win_first.shape, win_first.dtype)],
        scratch_shapes=[pltpu.VMEM((L * 8, W), F32), pltpu.VMEM((8, L * 8, W), F32)] + _GATHER_SCRATCH * 3,
        compiler_params=pltpu.CompilerParams(vmem_limit_bytes=VMEM_LIMIT_BYTES),
    )(c8, w_ada, b_ada, norm_g, win_first)


def ada_update(dmods, c_t, w, m, v, tr=512):
    L, D, W = w.shape

    def body(dm_ref, c_ref, w_ref, m_ref, v_ref, g_out, d_out, m_out, v_out):
        mx, my, _ = _place()
        shard = 2 * mx + my
        dm = jnp.zeros((8, W), F32)
        for s in range(N_SHARD):
            dm = dm + jnp.where(shard == s, dm_ref[0, :, s * W:(s + 1) * W], 0.0)
        cv = c_ref[...]
        ca = cv * _sigmoid(cv)
        g = jnp.zeros((tr, W), F32)
        for b in range(8):
            g = g + ca[:, b:b + 1] * dm[b:b + 1, :]
        d, m2, v2 = _adamw(w_ref[0], g, m_ref[0], v_ref[0])
        g_out[0], d_out[0], m_out[0], v_out[0] = g, d, m2, v2

    blk = pl.BlockSpec((1, tr, W), lambda l, i: (l, i, 0))
    return pl.pallas_call(
        body, name="ada_update", grid=(L, D // tr),
        in_specs=[pl.BlockSpec((1, 8, 3 * D), lambda l, i: (l, 0, 0)), pl.BlockSpec((tr, 8), lambda l, i: (i, 0)),
                  blk, blk, blk],
        out_specs=[blk] * 4, out_shape=[jax.ShapeDtypeStruct(w.shape, F32)] * 4,
        compiler_params=_cp("arbitrary", "arbitrary"),
    )(dmods, c_t, w, m, v)


STAT_ROWS = 16


def small_update(stats_all, norm, b_ada, final):
    def body(s_ref, *refs):
        ins, outs = refs[:9], refs[9:]
        tot = s_ref[0]
        for k in range(1, 8):
            tot = tot + s_ref[k]
        g_norm = tot[0:2, :]
        g_final = tot[2:3, :]
        g_b = jnp.concatenate(
            [jnp.concatenate([tot[3 + 3 * l + t:4 + 3 * l + t, :] for t in range(3)], axis=1) for l in range(DEPTH)],
            axis=0)
        for p, g in enumerate((g_norm, g_b, g_final)):
            w_ref, m_ref, v_ref = ins[3 * p:3 * p + 3]
            d, m2, v2 = _adamw(w_ref[...], g, m_ref[...], v_ref[...])
            for o_ref, val in zip(outs[4 * p:4 * p + 4], (g, d, m2, v2)):
                o_ref[...] = val
        loss = (0.5 / D_MODEL) * jnp.sum(tot[9:10, :], axis=1, keepdims=True)
        outs[12][...] = jnp.broadcast_to(loss, (8, LANES))

    shapes = []
    for w, _, _ in (norm, b_ada, final):
        shapes += [jax.ShapeDtypeStruct(w.shape, F32)] * 4
    shapes.append(jax.ShapeDtypeStruct((8, LANES), F32))
    return pl.pallas_call(body, name="small_update", out_shape=shapes)(stats_all, *norm, *b_ada, *final)


def kernel(x, c, norm_g, w_ada, b_ada, w_in, w_out, final_g, loss_target, m_norm_g, m_w_ada, m_b_ada, m_w_in, m_w_out, m_final_g, v_norm_g, v_w_ada, v_b_ada, v_w_in, v_w_out, v_final_g):
    S, D = x.shape[1], x.shape[2]
    mc = lax.axis_index("c")
    out_rows = D // N_SHARD

    def my_half(a, rows):
        return lax.dynamic_slice_in_dim(a, mc * rows, rows, axis=0)

    assert DEPTH == 2
    win = [my_half(w_in[l], D // 2).astype(BF16) for l in range(DEPTH)]
    wout = [my_half(w_out[l], out_rows // 2).astype(BF16) for l in range(DEPTH)]
    rest = [jnp.concatenate(wout, axis=0), win[1]]

    def unpack(gathered):
        outs, w3_second = gathered
        outs = outs.reshape(N_SHARD, 2, DEPTH, out_rows // 2, SHARD_W)
        return outs[:, :, 0].reshape(D, D), (w3_second.reshape(N_SHARD, D, SHARD_W), outs[:, :, 1].reshape(D, D))

    vecs, c_all, w3_first = prologue(jnp.broadcast_to(c, (8, D)), w_ada, b_ada, norm_g, win[0])
    c_all, w3_first = c_all[:, 0, :], w3_first.reshape(N_SHARD, D, SHARD_W)

    tabs = (*rope_tables(S), ret_log_gamma())
    saved = [None] * DEPTH
    h, saved[0], wall = layer_fwd(x[0], vecs[0], w3_first, lambda g: unpack(g)[0], tabs, rest)
    weights = [(w3_first, unpack(wall)[0]), unpack(wall)[1]]
    head = (jnp.broadcast_to(final_g[None, :], (8, D)), loss_target[0])
    (dx, st_loss), saved[1], _ = layer_fwd(h, vecs[1], *weights[1], tabs, head=head)

    dmod, dnorm, grads = [None] * DEPTH, [None] * DEPTH, None
    for l in reversed(range(DEPTH)):
        dx, dmod[l], dnorm[l], grads = layer_bwd(dx, saved[l], vecs[l], *weights[l], tabs, grads)

    stats = jnp.concatenate(dnorm + [st_loss[0:1]] + dmod + [st_loss[1:2], jnp.zeros((STAT_ROWS - 10, D), F32)], axis=0)
    p_own, p_sib, stats_all = sum_and_swap(*grads, stats)
    res_in = adam_slab(p_own, p_sib, w_in, m_w_in, v_w_in, 0, "adam_w_in")
    res_out = adam_slab(p_own, p_sib, w_out, m_w_out, v_w_out, DEPTH * D, "adam_w_out", tr=256)

    dmods = stats_all[:, 3:9, :].reshape(8, DEPTH, 3 * D).transpose(1, 0, 2)
    res_ada = ada_update(dmods, c_all.T, w_ada, m_w_ada, v_w_ada)
    small = small_update(stats_all, (norm_g, m_norm_g, v_norm_g), (b_ada, m_b_ada, v_b_ada),
                         (final_g[None, :], m_final_g[None, :], v_final_g[None, :]))
    res_norm, res_b, res_final = small[0:4], small[4:8], [a[0] for a in small[8:12]]
    loss = small[12][0, 0]

    by_kind = [res_norm, res_ada, res_b, res_in, res_out, res_final]
    outs = [loss, dx[None]]
    for kind in range(4):
        outs += [r[kind] for r in by_kind]
    return tuple(outs)
```

```python
import numpy as np
import jax
import jax.numpy as jnp
from jax import lax
from jax.experimental import pallas as pl
from jax.experimental.pallas import tpu as pltpu

F32, BF16 = jnp.float32, jnp.bfloat16
MESH = pl.DeviceIdType.MESH

D_MODEL = 1024
DEPTH = 2
SHARD_W = 1024
N_SHARD = 4
GROUP_W = 512
LANES = 128
SB_HEAD_DIM = 64
RET_HEAD_DIM = 128
CHUNK = 64
ROPE_BASE = 10000.0
EPS = 1e-6
SQ_SCALE = SB_HEAD_DIM ** -0.5
RK_SCALE = RET_HEAD_DIM ** -0.5
SB_T = 1024
SB_CHAINS = 16
SB_NB = 4
RET_T = 256
EXP_ZERO = -104.0
VMEM_LIMIT_BYTES = 56 * 2 ** 20

ADAM_LR, ADAM_B1, ADAM_B2, ADAM_EPS, ADAM_WD, ADAM_STEP = 0.001, 0.9, 0.999, 1e-08, 0.01, 10


def _cp(*sem):
    return pltpu.CompilerParams(dimension_semantics=sem, vmem_limit_bytes=VMEM_LIMIT_BYTES)


def _dot(a, b):
    return lax.dot_general(a, b, (((1,), (0,)), ((), ())), preferred_element_type=F32)


def _dot_nt(a, b):
    return lax.dot_general(a, b, (((1,), (1,)), ((), ())), preferred_element_type=F32)


def _dot_tn(a, b):
    return lax.dot_general(a, b, (((0,), (0,)), ((), ())), preferred_element_type=F32)


def _running_sum(a, tri):
    return _dot(a.astype(BF16), tri)


def _sigmoid(x):
    return 1.0 / (1.0 + jnp.exp(-x))


def _rowsum(a):
    return jnp.sum(a, axis=1, keepdims=True)


def _rowmean(a):
    return jnp.mean(a, axis=1, keepdims=True)


def inproj_fwd(x, vecs, w3, tm=512):
    S, D = x.shape

    def body(x_ref, v_ref, w_ref, ret_ref, sg_ref, h_ref, sb_ref):
        xv = x_ref[...]
        r = lax.rsqrt(_rowmean(xv * xv) + EPS)
        h = xv * r * v_ref[3:4, :] * (1.0 + v_ref[1:2, :]) + v_ref[0:1, :]
        hb = h.astype(BF16)
        h_ref[...] = hb
        for s in range(N_SHARD):
            p = _dot(hb, w_ref[s])
            if s < 2:
                ret_ref[:, s * SHARD_W:(s + 1) * SHARD_W] = p
            if s == 2:
                sb_ref[:, 0:GROUP_W] = (p[:, 0:GROUP_W] * SQ_SCALE).astype(BF16)
                sb_ref[:, GROUP_W:SHARD_W] = p[:, GROUP_W:].astype(BF16)
            if s == 3:
                sb_ref[:, SHARD_W:SHARD_W + GROUP_W] = p[:, 0:GROUP_W].astype(BF16)
                sg_ref[...] = p[:, GROUP_W:]

    row = lambda w: pl.BlockSpec((tm, w), lambda i: (i, 0))
    return pl.pallas_call(
        body, name="inproj_fwd", grid=(S // tm,),
        in_specs=[row(D), pl.BlockSpec((8, D), lambda i: (0, 0)),
                  pl.BlockSpec((N_SHARD, D, SHARD_W), lambda i: (0, 0, 0))],
        out_specs=[row(2 * SHARD_W), row(GROUP_W), row(D), row(3 * GROUP_W)],
        out_shape=[jax.ShapeDtypeStruct((S, 2 * SHARD_W), F32), jax.ShapeDtypeStruct((S, GROUP_W), F32),
                   jax.ShapeDtypeStruct((S, D), BF16), jax.ShapeDtypeStruct((S, 3 * GROUP_W), BF16)],
        compiler_params=_cp("arbitrary"),
    )(x, vecs, w3)


def _sb_logits(qh, k2, keep):
    z = _dot_nt(qh, k2)
    sp = jnp.log(1.0 + jnp.exp(-jnp.abs(z)))
    lb = jnp.minimum(z, 0.0) - sp
    lk = lb - z
    if keep is not None:
        lk = jnp.where(keep, lk, 0.0)
    return lb, lk


class _sb_chains:
    def __init__(self, i, q2, do_b=None):
        t = self.t = SB_T // SB_CHAINS
        self.C = range(SB_CHAINS)
        r = lax.broadcasted_iota(jnp.int32, (SB_NB * t, SB_NB * t), 0)
        c = lax.broadcasted_iota(jnp.int32, (SB_NB * t, SB_NB * t), 1)
        self.later_all = jnp.where(r > c, 1.0, 0.0).astype(BF16)
        self.earlier_all = jnp.where(r < c, 1.0, 0.0).astype(BF16)
        self.later, self.earlier = self.later_all[:t, :t], self.earlier_all[:t, :t]
        self.head0 = lax.broadcasted_iota(jnp.int32, (1, LANES), 1) < SB_HEAD_DIM
        row = lax.broadcasted_iota(jnp.int32, (2 * t, SB_NB * t), 0) & (t - 1)
        col = lax.broadcasted_iota(jnp.int32, (2 * t, SB_NB * t), 1)
        qt = [SB_CHAINS * i + cc for cc in self.C]
        self.first = [jnp.maximum(qt[cc] - (SB_NB - 1), 0) for cc in self.C]
        self.keep = [self.first[cc] * t + col < qt[cc] * t + row for cc in self.C]
        self.qs = [self._stack(q2[cc * t:(cc + 1) * t]) for cc in self.C]
        if do_b is not None:
            self.dos = [self._stack(do_b[cc * t:(cc + 1) * t]) for cc in self.C]

    def _stack(self, a):
        zero = jnp.zeros_like(a)
        return jnp.concatenate([jnp.where(self.head0, a, zero), jnp.where(self.head0, zero, a)], axis=0)

    def rows(self, ref, j, n):
        return ref[pl.ds(pl.multiple_of(j * self.t, self.t), n * self.t), :]

    def suffix(self, lk):
        return _running_sum(lk, self.later_all), _rowsum(lk)

    def prefix(self, g, G0):
        return _running_sum(g, self.earlier_all) + G0


def sb_fwd(sb, sg, gather=None):
    S = sb.shape[0]
    T = SB_T
    nq = S // T
    carried = list(gather or ())
    ng = len(carried)

    def body(*refs):
        (q_ref, k_ref, v_ref, sg_ref), refs = refs[:4], refs[4:]
        x_refs, (y_ref, o_ref, end_ref), out_refs, sems = refs[:ng], refs[ng:ng + 3], refs[ng + 3:2 * ng + 3], refs[2 * ng + 3:]
        p, i = pl.program_id(0), pl.program_id(1)
        gathers = [_gather_ops(x_refs[g], out_refs[g], *sems[3 * g:3 * g + 3]) for g in range(ng)]
        for start, forward, _ in gathers:
            pl.when(jnp.logical_and(p == 0, i == 0))(start)
            pl.when(jnp.logical_and(p == 3, i == 0))(forward)
        ch = _sb_chains(i, q_ref[...])
        later, head0 = ch.later, ch.head0
        lbk = [_sb_logits(ch.qs[c], ch.rows(k_ref, ch.first[c], SB_NB), ch.keep[c]) for c in ch.C]
        suffix, R = zip(*[ch.suffix(lbk[c][1]) for c in ch.C])
        aa = [jnp.where(ch.keep[c], jnp.exp(lbk[c][0] + suffix[c]), 0.0) for c in ch.C]
        acc = [_dot(aa[c].astype(BF16), ch.rows(v_ref, ch.first[c], SB_NB)) for c in ch.C]

        nc = len(ch.C)

        def alive(n, Rs):
            m = None
            for c in ch.C:
                rc = jnp.where(ch.first[c] - n > 0, Rs[c], EXP_ZERO)
                m = rc if m is None else jnp.maximum(m, rc)
            return jnp.max(m)

        def cond(st):
            return st[-1] > EXP_ZERO

        def step(st):
            n, accs, Rs = st[0], list(st[1:1 + nc]), list(st[1 + nc:1 + 2 * nc])
            for c in ch.C:
                j = ch.first[c] - 1 - n
                jc = jnp.maximum(j, 0)
                lb, lk = _sb_logits(ch.qs[c], ch.rows(k_ref, jc, 1), None)
                a = jnp.exp(lb + _running_sum(lk, later) + Rs[c])
                cx = _dot(a.astype(BF16), ch.rows(v_ref, jc, 1))
                accs[c] = jnp.where(j >= 0, accs[c] + cx, accs[c])
                Rs[c] = jnp.where(j >= 0, Rs[c] + _rowsum(lk), Rs[c])
            return (n + 1, *accs, *Rs, alive(n + 1, Rs))

        st = lax.while_loop(cond, step, (jnp.int32(0), *acc, *R, alive(0, R)))
        n_end, acc, R = st[0], st[1:1 + nc], st[1 + nc:1 + 2 * nc]
        outs = []
        for c in ch.C:
            base = c * (2 * ch.t + 8)
            end_ref[0, 0, base:base + 2 * ch.t, :] = jnp.broadcast_to(R[c], (2 * ch.t, 8))
            end_ref[0, 0, base + 2 * ch.t:base + 2 * ch.t + 8, :] = jnp.full((8, 8), n_end.astype(F32))
            outs.append(jnp.where(head0, acc[c][:ch.t], acc[c][ch.t:]))
        o = jnp.concatenate(outs, axis=0)
        o_ref[...] = o
        sg = sg_ref[...]
        y_ref[...] = (o * (sg * _sigmoid(sg))).astype(BF16)
        for _, _, finish in gathers:
            pl.when(jnp.logical_and(p == 3, i == nq - 1))(finish)

    return pl.pallas_call(
        body, name="sb_fwd", grid=(4, nq),
        in_specs=[pl.BlockSpec((T, LANES), lambda p, i: (i, p)),
                  pl.BlockSpec((S, LANES), lambda p, i: (0, 4 + p)),
                  pl.BlockSpec((S, LANES), lambda p, i: (0, 8 + p)),
                  pl.BlockSpec((T, LANES), lambda p, i: (i, p))] + [_ANY for _ in carried],
        out_specs=[pl.BlockSpec((T, LANES), lambda p, i: (i, p)),
                   pl.BlockSpec((T, LANES), lambda p, i: (i, p)),
                   pl.BlockSpec((1, 1, SB_CHAINS * (2 * T // SB_CHAINS + 8), 8), lambda p, i: (p, i, 0, 0))] + [_ANY for _ in carried],
        out_shape=[jax.ShapeDtypeStruct((S, GROUP_W), BF16),
                   jax.ShapeDtypeStruct((S, GROUP_W), F32),
                   jax.ShapeDtypeStruct((4, nq, SB_CHAINS * (2 * T // SB_CHAINS + 8), 8), F32)]
        + [jax.ShapeDtypeStruct((8,) + a.shape, a.dtype) for a in carried],
        scratch_shapes=_GATHER_SCRATCH * ng,
        compiler_params=_cp("arbitrary", "arbitrary"),
    )(sb, sb, sb, sg, *carried)


def sb_bwd(sb, sg, o, sb_end, dycat, ship=None):
    S = sb.shape[0]
    T = SB_T
    nq = S // T
    ex = _Exchange(ship)

    def body(*refs):
        (q_ref, k_ref, v_ref, sg_ref, o_ref, dy_ref, end_ref), refs = refs[:7], refs[7:]
        ship_refs, (dq_ref, dk_ref, dv_ref, dsg_ref), refs = refs[:ex.n_in], refs[ex.n_in:ex.n_in + 4], refs[ex.n_in + 4:]
        recv, (dk_acc, dv_acc), sems = refs[:ex.n_out], refs[ex.n_out:ex.n_out + 2], refs[ex.n_out + 2:]
        start, finish = ex.ops(ship_refs, recv + sems)
        p, i = pl.program_id(0), pl.program_id(1)
        pl.when(jnp.logical_and(p == 0, i == 0))(start)

        @pl.when(i == 0)
        def _():
            dk_acc[...] = jnp.zeros_like(dk_acc)
            dv_acc[...] = jnp.zeros_like(dv_acc)

        sg = sg_ref[...]
        sig = _sigmoid(sg)
        dy = dy_ref[...]
        dsg_ref[...] = (dy * o_ref[...] * (sig * (1.0 + sg * (1.0 - sig)))).astype(BF16)
        do_b = (dy * (sg * sig)).astype(BF16)
        ch = _sb_chains(i, q_ref[...], do_b)
        later, earlier, head0, t = ch.later, ch.earlier, ch.head0, ch.t
        end = end_ref[0, 0]

        def grads(c, j, n, a, lb, g, G, keep):
            dz = g - jnp.exp(lb) * (g + G)
            if keep is not None:
                dz = jnp.where(keep, dz, 0.0)
            dzb = dz.astype(BF16)
            rows = pl.ds(pl.multiple_of(j * t, t), n * t)
            dk_acc[rows, :] += _dot_tn(dzb, ch.qs[c])
            dv_acc[rows, :] += _dot_tn(a.astype(BF16), ch.dos[c])
            return _dot(dzb, ch.rows(k_ref, j, n))

        nc = len(ch.C)
        n_end = jnp.max(end[2 * t:2 * t + 8, :]).astype(jnp.int32)

        def sweep(m, st):
            dqs, G0s, lefts = list(st[:nc]), list(st[nc:2 * nc]), list(st[2 * nc:])
            for c in ch.C:
                j = ch.first[c] - n_end + m
                jc = jnp.maximum(j, 0)
                lb, lk = _sb_logits(ch.qs[c], ch.rows(k_ref, jc, 1), None)
                stick = lefts[c] - _rowsum(lk)
                a = jnp.where(j >= 0, jnp.exp(lb + _running_sum(lk, later) + stick), 0.0)
                g = a * _dot_nt(ch.dos[c], ch.rows(v_ref, jc, 1))
                G = _running_sum(g, earlier) + G0s[c]
                dqs[c] = dqs[c] + grads(c, jc, 1, a, lb, jnp.where(j >= 0, g, 0.0), jnp.where(j >= 0, G, 0.0), None)
                G0s[c] = G0s[c] + _rowsum(g)
                lefts[c] = jnp.where(j >= 0, stick, lefts[c])
            return (*dqs, *G0s, *lefts)

        lefts = [end[c * (2 * t + 8):c * (2 * t + 8) + 2 * t, 0:1] for c in ch.C]
        st = lax.fori_loop(0, n_end, sweep, (*[jnp.zeros((2 * t, LANES), F32)] * nc,
                                             *[jnp.zeros((2 * t, 1), F32)] * nc, *lefts))
        dq, G0 = st[:nc], st[nc:2 * nc]

        lbk = [_sb_logits(ch.qs[c], ch.rows(k_ref, ch.first[c], SB_NB), ch.keep[c]) for c in ch.C]
        suffix = [ch.suffix(lbk[c][1])[0] for c in ch.C]
        aa = [jnp.where(ch.keep[c], jnp.exp(lbk[c][0] + suffix[c]), 0.0) for c in ch.C]
        g = [aa[c] * _dot_nt(ch.dos[c], ch.rows(v_ref, ch.first[c], SB_NB)) for c in ch.C]
        G = [ch.prefix(g[c], G0[c]) for c in ch.C]
        for c in ch.C:
            dqc = dq[c] + grads(c, ch.first[c], SB_NB, aa[c], lbk[c][0], g[c], G[c], ch.keep[c])
            dq_ref[c * t:(c + 1) * t, :] = (jnp.where(head0, dqc[:t], dqc[t:]) * SQ_SCALE).astype(BF16)

        @pl.when(i == nq - 1)
        def _():
            dk_ref[...] = dk_acc[...].astype(BF16)
            dv_ref[...] = dv_acc[...].astype(BF16)

        pl.when(jnp.logical_and(p == 3, i == nq - 1))(finish)

    tile_spec = lambda c0: pl.BlockSpec((T, LANES), lambda p, i: (i, c0 + p))
    head_spec = lambda c0: pl.BlockSpec((S, LANES), lambda p, i: (0, c0 + p))
    return pl.pallas_call(
        body, name="sb_bwd", grid=(4, nq),
        in_specs=[tile_spec(0), head_spec(4), head_spec(8), tile_spec(0), tile_spec(0), tile_spec(4),
                  pl.BlockSpec((1, 1, SB_CHAINS * (2 * T // SB_CHAINS + 8), 8), lambda p, i: (p, i, 0, 0))] + ex.in_specs,
        out_specs=[tile_spec(0), head_spec(0), head_spec(0), tile_spec(0)] + ex.out_specs,
        out_shape=[jax.ShapeDtypeStruct((S, GROUP_W), BF16)] * 4 + ex.out_shape,
        scratch_shapes=[pltpu.VMEM((S, LANES), F32), pltpu.VMEM((S, LANES), F32)] + ex.scratch,
        compiler_params=_cp("arbitrary", "arbitrary"),
    )(sb, sb, sb, sg, o, dycat, sb_end, *ex.ship)


def rope_tables(S):
    half = RET_HEAD_DIM // 2
    lane = jnp.arange(RET_HEAD_DIM)
    inv = ROPE_BASE ** (-(lane % half).astype(F32) / half)
    ang = jnp.arange(S, dtype=F32)[:, None] * inv[None, :]
    return jnp.cos(ang), jnp.where(lane < half, -1.0, 1.0)[None, :] * jnp.sin(ang)


def ret_log_gamma():
    return jnp.log1p(-(2.0 ** (-5.0 - jnp.arange(4, dtype=F32))))


def _swap_halves(a):
    return pltpu.roll(a, RET_HEAD_DIM // 2, axis=1)


def _ret_decay_mask(lg):
    n = lax.broadcasted_iota(jnp.int32, (RET_T, RET_T), 0)
    m = lax.broadcasted_iota(jnp.int32, (RET_T, RET_T), 1)
    dist = jnp.abs(n - m).astype(F32)
    return jnp.where((m // CHUNK) <= (n // CHUNK), jnp.exp(lg * dist), 0.0)


def _ret_block(lg, rq, rk, rv, cosf, sinf, dm):
    q = rq * cosf + _swap_halves(rq) * sinf
    k = (rk * cosf + _swap_halves(rk) * sinf) * RK_SCALE
    qb, kb, vb = q.astype(BF16), k.astype(BF16), rv.astype(BF16)
    sc = _dot_nt(qb, kb) * dm
    nloc = lax.broadcasted_iota(jnp.int32, (RET_T, 1), 0).astype(F32)
    qdec = jnp.exp(lg * (nloc + 1.0))
    kdec = jnp.exp(lg * (RET_T - 1.0 - nloc))
    block_dec = jnp.exp(jnp.full((1, LANES), lg * RET_T, F32))
    return q, k, qb, kb, vb, sc, qdec, kdec, block_dec


RET_RB = 2


def _ret_specs(S, rb):
    group = lambda c0: pl.BlockSpec((RET_RB * RET_T, GROUP_W), lambda s: (rb(s), c0))
    return group, pl.BlockSpec((RET_RB * RET_T, LANES), lambda s: (rb(s), 0))


def _ret_chains():
    chains = [(h, b) for b in range(RET_RB) for h in range(4)]
    rows = lambda c: (slice(c[1] * RET_T, (c[1] + 1) * RET_T), slice(c[0] * LANES, (c[0] + 1) * LANES))
    tab = lambda ref, c: ref[c[1] * RET_T:(c[1] + 1) * RET_T, :]
    return chains, rows, tab


def _ret_blocks(chains, rows, lg_ref, rq_ref, rk_ref, rv_ref, cosf, sinf, dm_ref):
    blk = {c: _ret_block(lg_ref[c[0]], rq_ref[rows(c)], rk_ref[rows(c)], rv_ref[rows(c)],
                         cosf[c], sinf[c], dm_ref[c[0]]) for c in chains}
    return ({c: blk[c][n] for c in chains} for n in range(9))


def ret_fwd(proj, cosf, sinf, lgam):
    S = proj.shape[0]
    nb = S // RET_T
    group, row_tab = _ret_specs(S, lambda s: s)

    def body(lg_ref, rq_ref, rk_ref, rv_ref, rg_ref, cos_ref, sin_ref, y_ref, o_ref, st_out, st_ref, dm_ref):
        @pl.when(pl.program_id(0) == 0)
        def _():
            st_ref[...] = jnp.zeros_like(st_ref)
            for h in range(4):
                dm_ref[h] = _ret_decay_mask(lg_ref[h])

        chains, rows, tab = _ret_chains()
        cosf, sinf = {c: tab(cos_ref, c) for c in chains}, {c: tab(sin_ref, c) for c in chains}
        q, k, qb, kb, vb, sc, qdec, kdec, block_dec = _ret_blocks(
            chains, rows, lg_ref, rq_ref, rk_ref, rv_ref, cosf, sinf, dm_ref)
        kv = {c: _dot_tn((k[c] * kdec[c]).astype(BF16), vb[c]) for c in chains}
        st = {(h, 0): st_ref[h] for h in range(4)}
        for b in range(RET_RB):
            for h in range(4):
                st[(h, b + 1)] = st[(h, b)] * block_dec[(h, b)] + kv[(h, b)]
        for h, b in chains:
            st_out[h, b] = st[(h, b)]
        for h in range(4):
            st_ref[h] = st[(h, RET_RB)]
        o = {c: _dot(sc[c].astype(BF16), vb[c]) + _dot(qb[c], st[c].astype(BF16)) * qdec[c] for c in chains}
        for c in chains:
            o_ref[rows(c)] = o[c]
        cen = {c: o[c] - _rowmean(o[c]) for c in chains}
        on = {c: cen[c] * lax.rsqrt(_rowmean(cen[c] * cen[c]) + EPS) for c in chains}
        rg = {c: rg_ref[rows(c)] for c in chains}
        for c in chains:
            y_ref[rows(c)] = (on[c] * (rg[c] * _sigmoid(rg[c]))).astype(BF16)

    return pl.pallas_call(
        body, name="ret_fwd", grid=(nb // RET_RB,),
        in_specs=[pl.BlockSpec(memory_space=pltpu.SMEM),
                  group(0), group(1), group(2), group(3), row_tab, row_tab],
        out_specs=[group(0), group(0),
                   pl.BlockSpec((4, RET_RB, LANES, LANES), lambda s: (0, s, 0, 0))],
        out_shape=[jax.ShapeDtypeStruct((S, GROUP_W), BF16),
                   jax.ShapeDtypeStruct((S, GROUP_W), F32),
                   jax.ShapeDtypeStruct((4, nb, LANES, LANES), F32)],
        scratch_shapes=[pltpu.VMEM((4, LANES, LANES), F32), pltpu.VMEM((4, RET_T, RET_T), F32)],
        compiler_params=_cp("arbitrary"),
    )(lgam, proj, proj, proj, proj, cosf, sinf)


def ret_bwd(proj, cosf, sinf, lgam, o, states, dycat):
    S = proj.shape[0]
    nsteps = S // RET_T // RET_RB
    rev = lambda s: nsteps - 1 - s
    group, row_tab = _ret_specs(S, rev)

    def body(lg_ref, rq_ref, rk_ref, rv_ref, rg_ref, cos_ref, sin_ref, o_ref, st_in, dy_ref,
             drq_ref, drk_ref, drv_ref, drg_ref, ds_ref, dm_ref):
        @pl.when(pl.program_id(0) == 0)
        def _():
            ds_ref[...] = jnp.zeros_like(ds_ref)
            for h in range(4):
                dm_ref[h] = _ret_decay_mask(lg_ref[h])

        chains, rows, tab = _ret_chains()
        cosf, sinf = {c: tab(cos_ref, c) for c in chains}, {c: tab(sin_ref, c) for c in chains}
        dms = {c: dm_ref[c[0]] for c in chains}
        q, k, qb, kb, vb, sc, qdec, kdec, block_dec = _ret_blocks(
            chains, rows, lg_ref, rq_ref, rk_ref, rv_ref, cosf, sinf, dm_ref)
        o_v = {c: o_ref[rows(c)] for c in chains}
        cen = {c: o_v[c] - _rowmean(o_v[c]) for c in chains}
        rstd = {c: lax.rsqrt(_rowmean(cen[c] * cen[c]) + EPS) for c in chains}
        on = {c: cen[c] * rstd[c] for c in chains}
        rg = {c: rg_ref[rows(c)] for c in chains}
        sig = {c: _sigmoid(rg[c]) for c in chains}
        dy = {c: dy_ref[rows(c)] for c in chains}
        for c in chains:
            drg_ref[rows(c)] = (dy[c] * on[c] * (sig[c] * (1.0 + rg[c] * (1.0 - sig[c])))).astype(BF16)
        don = {c: dy[c] * (rg[c] * sig[c]) for c in chains}
        do = {c: rstd[c] * (don[c] - _rowmean(don[c]) - on[c] * _rowmean(don[c] * on[c])) for c in chains}
        dob = {c: do[c].astype(BF16) for c in chains}
        dsc = {c: (_dot_nt(dob[c], vb[c]) * dms[c]).astype(BF16) for c in chains}
        st_b = {c: st_in[c[0], c[1]].astype(BF16) for c in chains}
        dst = {c: _dot_tn((q[c] * qdec[c]).astype(BF16), dob[c]) for c in chains}
        dsn = {(h, RET_RB): ds_ref[h] for h in range(4)}
        for b in reversed(range(RET_RB)):
            for h in range(4):
                dsn[(h, b)] = dsn[(h, b + 1)] * block_dec[(h, b)] + dst[(h, b)]
        for h in range(4):
            ds_ref[h] = dsn[(h, 0)]
        dsn_b = {c: dsn[(c[0], c[1] + 1)].astype(BF16) for c in chains}
        dq = {c: _dot(dsc[c], kb[c]) + _dot_nt(dob[c], st_b[c]) * qdec[c] for c in chains}
        dk = {c: (_dot_tn(dsc[c], qb[c]) + _dot_nt(vb[c], dsn_b[c]) * kdec[c]) * RK_SCALE for c in chains}
        dv = {c: _dot_tn(sc[c].astype(BF16), dob[c]) + _dot((k[c] * kdec[c]).astype(BF16), dsn_b[c])
              for c in chains}
        for c in chains:
            drq_ref[rows(c)] = (dq[c] * cosf[c] + _swap_halves(dq[c] * sinf[c])).astype(BF16)
            drk_ref[rows(c)] = (dk[c] * cosf[c] + _swap_halves(dk[c] * sinf[c])).astype(BF16)
            drv_ref[rows(c)] = dv[c].astype(BF16)

    return pl.pallas_call(
        body, name="ret_bwd", grid=(nsteps,),
        in_specs=[pl.BlockSpec(memory_space=pltpu.SMEM),
                  group(0), group(1), group(2), group(3), row_tab, row_tab,
                  group(0), pl.BlockSpec((4, RET_RB, LANES, LANES), lambda s: (0, rev(s), 0, 0)),
                  group(0)],
        out_specs=[group(0)] * 4,
        out_shape=[jax.ShapeDtypeStruct((S, GROUP_W), BF16)] * 4,
        scratch_shapes=[pltpu.VMEM((4, LANES, LANES), F32), pltpu.VMEM((4, RET_T, RET_T), F32)],
        compiler_params=_cp("arbitrary"),
    )(lgam, proj, proj, proj, proj, cosf, sinf, o, states, dycat)


def outproj_fwd(x, vecs, y_ret, y_sb, w_out, head=None, tm=1024):
    S, D = x.shape
    tm = min(tm, S)
    last = list(head or ())

    def body(x_ref, v_ref, yr_ref, ys_ref, w_ref, *refs):
        y = _dot(yr_ref[...], w_ref[0:GROUP_W, :]) + _dot(ys_ref[...], w_ref[GROUP_W:, :])
        xv = x_ref[...] + v_ref[2:3, :] * y
        if not last:
            y_ref, xo_ref = refs
            y_ref[...] = y.astype(BF16)
            xo_ref[...] = xv
            return
        g_ref, t_ref, y_ref, dx_ref, st_ref = refs
        y_ref[...] = y.astype(BF16)

        @pl.when(pl.program_id(0) == 0)
        def _():
            st_ref[...] = jnp.zeros_like(st_ref)

        g = g_ref[0:1, :]
        for hh in range(2):
            rows = slice(hh * (tm // 2), (hh + 1) * (tm // 2))
            xh = xv[rows, :]
            r = lax.rsqrt(_rowmean(xh * xh) + EPS)
            xn = xh * r
            err = xn * g - t_ref[rows, :]
            dy = err * (1.0 / D)
            dxn = dy * g
            dx_ref[rows, :] = r * (dxn - xn * _rowmean(dxn * xn))
            st_ref[0:1, :] += jnp.sum(dy * xn, axis=0, keepdims=True)
            st_ref[1:2, :] += jnp.sum(err * err, axis=0, keepdims=True)

    row = lambda w: pl.BlockSpec((tm, w), lambda i: (i, 0))
    fixed = pl.BlockSpec((8, D), lambda i: (0, 0))
    return pl.pallas_call(
        body, name="outproj_fwd", grid=(S // tm,),
        in_specs=[row(D), fixed, row(GROUP_W), row(GROUP_W), pl.BlockSpec((D, D), lambda i: (0, 0))]
        + ([fixed, row(D)] if last else []),
        out_specs=[row(D), row(D)] + ([fixed] if last else []),
        out_shape=[jax.ShapeDtypeStruct((S, D), BF16), jax.ShapeDtypeStruct((S, D), F32)]
        + ([jax.ShapeDtypeStruct((8, D), F32)] if last else []),
        compiler_params=_cp("arbitrary"),
    )(x, vecs, y_ret, y_sb, w_out, *last)


def outproj_bwd(dx, y, vecs, y_ret, y_sb, w_out, tm=1024):
    S, D = dx.shape
    tm = min(tm, S)
    n = S // tm

    def body(dx_ref, y_ref, v_ref, yr_ref, ys_ref, w_ref, dyc_ref, dw_ref, st_ref, acc):
        i = pl.program_id(0)

        @pl.when(i == 0)
        def _():
            st_ref[...] = jnp.zeros_like(st_ref)
            acc[...] = jnp.zeros_like(acc)

        dxv = dx_ref[...]
        st_ref[0:1, :] += jnp.sum(dxv * y_ref[...].astype(F32), axis=0, keepdims=True)
        dyy = (dxv * v_ref[2:3, :]).astype(BF16)
        dyc_ref[...] = _dot_nt(dyy, w_ref[...])
        acc[0:GROUP_W, :] += _dot_tn(yr_ref[...], dyy)
        acc[GROUP_W:, :] += _dot_tn(ys_ref[...], dyy)

        @pl.when(i == n - 1)
        def _():
            dw_ref[...] = acc[...].astype(BF16)

    row = lambda w: pl.BlockSpec((tm, w), lambda i: (i, 0))
    fixed = lambda r: pl.BlockSpec((r, D), lambda i: (0, 0))
    return pl.pallas_call(
        body, name="outproj_bwd", grid=(n,),
        in_specs=[row(D), row(D), fixed(8), row(GROUP_W), row(GROUP_W), fixed(D)],
        out_specs=[row(D), fixed(D), fixed(8)],
        out_shape=[jax.ShapeDtypeStruct((S, D), F32), jax.ShapeDtypeStruct((D, D), BF16),
                   jax.ShapeDtypeStruct((8, D), F32)],
        scratch_shapes=[pltpu.VMEM((D, D), F32)],
        compiler_params=_cp("arbitrary"),
    )(dx, y, vecs, y_ret, y_sb, w_out)


def inproj_bwd_x(pieces, w3, x, vecs, dx_res, ship=None, tm=512):
    S, D = x.shape
    n = S // tm
    ex = _Exchange(ship)

    def body(*refs):
        p_refs, (w_ref, x_ref, v_ref, dr_ref), refs = refs[:8], refs[8:12], refs[12:]
        ship_refs, (dx_ref, st_ref), refs = refs[:ex.n_in], refs[ex.n_in:ex.n_in + 2], refs[ex.n_in + 2:]
        start, finish = ex.ops(ship_refs, refs)

        @pl.when(pl.program_id(0) == 0)
        def _():
            st_ref[...] = jnp.zeros_like(st_ref)
            start()

        g, scale1 = v_ref[3:4, :], 1.0 + v_ref[1:2, :]
        halves = [slice(hh * (tm // 2), (hh + 1) * (tm // 2)) for hh in range(2)]
        dhs = []
        for rows in halves:
            dh = jnp.zeros((tm // 2, D), F32)
            for k, p_ref in enumerate(p_refs):
                c0 = (k % 2) * GROUP_W
                dh = dh + _dot_nt(p_ref[rows, :], w_ref[k // 2, :, c0:c0 + GROUP_W])
            dhs.append(dh)
        for rows, dh in zip(halves, dhs):
            xv = x_ref[rows, :]
            r = lax.rsqrt(_rowmean(xv * xv) + EPS)
            xn = xv * r
            st_ref[0:1, :] += jnp.sum(dh, axis=0, keepdims=True)
            dh_xn = dh * xn
            st_ref[1:2, :] += jnp.sum(dh_xn, axis=0, keepdims=True) * g
            st_ref[2:3, :] += jnp.sum(dh_xn, axis=0, keepdims=True) * scale1
            dxn = dh * (g * scale1)
            dx_ref[rows, :] = r * (dxn - xn * _rowmean(dxn * xn)) + dr_ref[rows, :]
        pl.when(pl.program_id(0) == n - 1)(finish)

    row = lambda w: pl.BlockSpec((tm, w), lambda i: (i, 0))
    return pl.pallas_call(
        body, name="inproj_bwd_x", grid=(n,),
        in_specs=[row(GROUP_W)] * 8 + [pl.BlockSpec((N_SHARD, D, SHARD_W), lambda i: (0, 0, 0)),
                                       row(D), pl.BlockSpec((8, D), lambda i: (0, 0)), row(D)] + ex.in_specs,
        out_specs=[row(D), pl.BlockSpec((8, D), lambda i: (0, 0))] + ex.out_specs,
        out_shape=[jax.ShapeDtypeStruct((S, D), F32), jax.ShapeDtypeStruct((8, D), F32)] + ex.out_shape,
        scratch_shapes=ex.scratch,
        compiler_params=_cp("arbitrary"),
    )(*pieces, w3, x, vecs, dx_res, *ex.ship)


def inproj_bwd_w(h, pieces, tm=1024):
    S, D = h.shape
    tm = min(tm, S)
    n = S // tm

    def body(*refs):
        h_ref, p_refs, dw_ref, acc = refs[0], refs[1:9], refs[9], refs[10]
        i = pl.program_id(0)

        @pl.when(i == 0)
        def _():
            acc[...] = jnp.zeros_like(acc)

        hv = h_ref[...]
        for k, p_ref in enumerate(p_refs):
            c0 = (k % 2) * GROUP_W
            acc[k // 2, :, c0:c0 + GROUP_W] += _dot_tn(hv, p_ref[...])

        @pl.when(i == n - 1)
        def _():
            dw_ref[...] = acc[...].astype(BF16)

    row = lambda w: pl.BlockSpec((tm, w), lambda i: (i, 0))
    return pl.pallas_call(
        body, name="inproj_bwd_w", grid=(n,),
        in_specs=[row(D)] + [row(GROUP_W)] * 8,
        out_specs=pl.BlockSpec((N_SHARD, D, SHARD_W), lambda i: (0, 0, 0), pipeline_mode=pl.Buffered(1)),
        out_shape=jax.ShapeDtypeStruct((N_SHARD, D, SHARD_W), BF16),
        scratch_shapes=[pltpu.VMEM((N_SHARD, D, SHARD_W), F32)],
        compiler_params=_cp("arbitrary"),
    )(h, *pieces)


def layer_fwd(x, vecs, w3, w_out, tabs, gather=None, head=None):
    cosf, sinf, lgam = tabs
    ret, sg, h, sb = inproj_fwd(x, vecs, w3)
    y_ret, o_ret, states = ret_fwd(ret, cosf, sinf, lgam)
    y_sb, o_sb, sb_end, *gathered = sb_fwd(sb, sg, gather)
    if callable(w_out):
        w_out = w_out(gathered)
    y, *x_next = outproj_fwd(x, vecs, y_ret, y_sb, w_out, head)
    saved = (x, ret, sg, h, sb, y_ret, o_ret, states, y_sb, o_sb, sb_end, y)
    return (x_next[0] if head is None else x_next), saved, gathered


def _by_shard(dw_out):
    return dw_out.reshape(N_SHARD, D_MODEL // N_SHARD, D_MODEL)


def layer_bwd(dx, saved, vecs, w3, w_out, tabs, later_grads=None):
    cosf, sinf, lgam = tabs
    x, ret, sg, h, sb, y_ret, o_ret, states, y_sb, o_sb, sb_end, y = saved
    dycat, dw_out, st_o = outproj_bwd(dx, y, vecs, y_ret, y_sb, w_out)
    dw_out = _by_shard(dw_out)
    ship = None if later_grads is None else (later_grads[0], dw_out, later_grads[1])
    *d_sb, = sb_bwd(sb, sg, o_sb, sb_end, dycat, ship)
    d_ret = ret_bwd(ret, cosf, sinf, lgam, o_ret, states, dycat)
    pieces = list(d_ret) + d_sb[:4]
    dw_in = inproj_bwd_w(h, pieces)
    dx, st_i, *recv_in = inproj_bwd_x(pieces, w3, x, vecs, dx, None if later_grads is None else (dw_in,))
    dmod = jnp.concatenate([st_i[0:2], st_o[0:1]], axis=0)
    grads = (dw_in, dw_out) if later_grads is None else (recv_in[0], d_sb[4])
    return dx, dmod, st_i[2:3], grads


def _place():
    return lax.axis_index("x"), lax.axis_index("y"), lax.axis_index("c")


def _other_chips(mx, my):
    return [(1 - mx, my), (mx, 1 - my), (1 - mx, 1 - my)]


_ANY = pl.BlockSpec(memory_space=pl.ANY)


_GATHER_SCRATCH = [pltpu.SemaphoreType.DMA((7,)), pltpu.SemaphoreType.DMA((7,)), pltpu.SemaphoreType.DMA(())]


def _gather_ops(x_ref, out_ref, send_sems, recv_sems, local_sem):
    mx, my, mc = _place()
    me, sibling = (mx, my, mc), (mx, my, 1 - mc)
    chips = _other_chips(mx, my)

    def slot(px, py, pc):
        return out_ref.at[4 * px + 2 * py + pc]

    def copy(k, block, to, src=None):
        return pltpu.make_async_remote_copy(
            src_ref=slot(*block) if src is None else src, dst_ref=slot(*block),
            send_sem=send_sems.at[k], recv_sem=recv_sems.at[k], device_id=to, device_id_type=MESH)

    mine = pltpu.make_async_copy(x_ref, slot(*me), local_sem)
    first = [copy(0, me, sibling, src=x_ref)]
    first += [copy(1 + j, me, (*chip, mc), src=x_ref) for j, chip in enumerate(chips)]
    passed = [copy(4 + j, (*chip, mc), sibling) for j, chip in enumerate(chips)]

    def start():
        mine.start()
        for cp in first:
            cp.start()

    def forward():
        for j, chip in enumerate(chips):
            copy(1 + j, (*chip, mc), me).wait_recv()
            passed[j].start()

    def finish():
        copy(0, sibling, me).wait_recv()
        for j, chip in enumerate(chips):
            copy(4 + j, (*chip, 1 - mc), me).wait_recv()
        for cp in first + passed:
            cp.wait_send()
        mine.wait()

    return start, forward, finish


class _Exchange:
    def __init__(self, ship):
        self.ship = list(ship or ())
        self.n_in = len(self.ship)
        self.n_out = 1 if self.ship else 0
        self.rows = [a.shape[1] for a in self.ship]
        self.in_specs = [_ANY] * self.n_in
        self.out_specs = [_ANY] * self.n_out
        self.out_shape = [jax.ShapeDtypeStruct((N_SHARD, sum(self.rows), SHARD_W), BF16)] * self.n_out
        sem = pltpu.SemaphoreType.DMA
        self.scratch = [sem((3,)), sem((3,)), sem(())] * self.n_out

    def ops(self, ship_refs, tail):
        if not self.ship:
            return (lambda: None), (lambda: None)
        recv, send_sems, recv_sems, local_sem = tail
        mx, my, mc = _place()
        my_chip = 2 * mx + my
        chips = _other_chips(mx, my)

        def pieces(s):
            firsts = np.cumsum([0] + self.rows[:-1])
            return [(ref.at[s], int(r0), n) for ref, r0, n in zip(ship_refs, firsts, self.rows)]

        def start():
            for src, r0, n in pieces(my_chip):
                pltpu.make_async_copy(src, recv.at[my_chip, pl.ds(r0, n)], local_sem).start()
            for j, (px, py) in enumerate(chips):
                for src, r0, n in pieces(2 * px + py):
                    pltpu.make_async_remote_copy(
                        src_ref=src, dst_ref=recv.at[my_chip, pl.ds(r0, n)],
                        send_sem=send_sems.at[j], recv_sem=recv_sems.at[j],
                        device_id=(px, py, mc), device_id_type=MESH).start()

        def finish():
            for j, (px, py) in enumerate(chips):
                whole = recv.at[2 * px + py]
                both = pltpu.make_async_remote_copy(
                    src_ref=whole, dst_ref=whole, send_sem=send_sems.at[j], recv_sem=recv_sems.at[j],
                    device_id=(px, py, mc), device_id_type=MESH)
                both.wait_recv()
                both.wait_send()
            pltpu.make_async_copy(recv.at[my_chip], recv.at[my_chip], local_sem).wait()

        return start, finish


def sum_and_swap(recv_a, recv_b, stats, tr=256):
    n, rows_a, cols = recv_a.shape
    na, nb = rows_a // tr, recv_b.shape[1] // tr
    nt = na + nb

    def body(a_ref, b_ref, st_ref, own_ref, sib_ref, stall_ref, slots, send_sems, recv_sem, *gather_sems):
        i = pl.program_id(0)
        mx, my, mc = _place()
        slot = i % 2
        g_start, g_forward, g_finish = _gather_ops(st_ref, stall_ref, *gather_sems)
        pl.when(i == 0)(g_start)
        pl.when(i == nt // 2)(g_forward)

        def push(k, tile):
            return pltpu.make_async_remote_copy(
                src_ref=slots.at[k], dst_ref=sib_ref.at[pl.ds(pl.multiple_of(tile * tr, tr), tr)],
                send_sem=send_sems.at[k], recv_sem=recv_sem, device_id=(mx, my, 1 - mc), device_id_type=MESH)

        pl.when(i >= 2)(lambda: push(slot, i - 2).wait_send())

        def total(r_ref):
            acc = r_ref[0].astype(F32)
            for k in range(1, n):
                acc = acc + r_ref[k].astype(F32)
            own_ref[...] = acc
            slots[slot] = acc

        pl.when(i < na)(lambda: total(a_ref))
        pl.when(i >= na)(lambda: total(b_ref))
        push(slot, i).start()

        @pl.when(i == nt - 1)
        def _():
            push(1 - slot, i - 1).wait_send()
            push(slot, i).wait_send()
            pltpu.make_async_remote_copy(src_ref=sib_ref, dst_ref=sib_ref, send_sem=send_sems.at[0], recv_sem=recv_sem,
                                         device_id=(mx, my, 1 - mc), device_id_type=MESH).wait_recv()
            g_finish()

    return pl.pallas_call(
        body, name="sum_and_swap", grid=(nt,),
        in_specs=[pl.BlockSpec((n, tr, cols), lambda i: (0, jnp.minimum(i, na - 1), 0)),
                  pl.BlockSpec((n, tr, cols), lambda i: (0, jnp.maximum(i - na, 0), 0)), _ANY],
        out_specs=[pl.BlockSpec((tr, cols), lambda i: (i, 0)), _ANY, _ANY],
        out_shape=[jax.ShapeDtypeStruct((nt * tr, cols), F32)] * 2
        + [jax.ShapeDtypeStruct((8,) + stats.shape, stats.dtype)],
        scratch_shapes=[pltpu.VMEM((2, tr, cols), F32), pltpu.SemaphoreType.DMA((2,)), pltpu.SemaphoreType.DMA(())]
        + _GATHER_SCRATCH,
        compiler_params=_cp("arbitrary"),
    )(recv_a, recv_b, stats)


def _adamw(w, g, m, v):
    m = ADAM_B1 * m + (1.0 - ADAM_B1) * g
    v = ADAM_B2 * v + (1.0 - ADAM_B2) * (g * g)
    m_hat = m / (1.0 - ADAM_B1 ** ADAM_STEP)
    v_hat = v / (1.0 - ADAM_B2 ** ADAM_STEP)
    delta = -ADAM_LR * (m_hat / (jnp.sqrt(v_hat) + ADAM_EPS) + ADAM_WD * w)
    return delta, m, v


def adam_slab(p_own, p_sib, w, m, v, row0, name, tr=512):
    L, R, C = w.shape
    nr = R // tr

    def body(a_ref, b_ref, w_ref, m_ref, v_ref, g_out, d_out, m_out, v_out):
        g = a_ref[...] + b_ref[...]
        d, m2, v2 = _adamw(w_ref[0], g, m_ref[0], v_ref[0])
        g_out[0], d_out[0], m_out[0], v_out[0] = g, d, m2, v2

    slab = pl.BlockSpec((tr, C), lambda l, i: (row0 // tr + l * nr + i, 0))
    blk = pl.BlockSpec((1, tr, C), lambda l, i: (l, i, 0))
    return pl.pallas_call(
        body, name=name, grid=(L, nr),
        in_specs=[slab, slab, blk, blk, blk], out_specs=[blk] * 4,
        out_shape=[jax.ShapeDtypeStruct(w.shape, F32)] * 4,
        compiler_params=_cp("arbitrary", "arbitrary"),
    )(p_own, p_sib, w, m, v)


def prologue(c8, w_ada, b_ada, norm_g, win_first):
    L, D, W = w_ada.shape

    def body(c_ref, w_ref, b_ref, g_ref, win_ref, vecs_ref, call_ref, wall_ref, mod_ref, mall_ref, *sems):
        w_start, w_forward, w_finish = _gather_ops(win_ref, wall_ref, *sems[0:3])
        for step in _gather_ops(c_ref, call_ref, *sems[3:6]):
            step()
        w_start()
        cv = call_ref[:, 0, :]
        ca = cv * _sigmoid(cv)
        for l in range(L):
            mod_ref[l * 8:(l + 1) * 8, :] = jnp.dot(ca, w_ref[l], precision=lax.Precision.HIGHEST,
                                                    preferred_element_type=F32)
        for step in _gather_ops(mod_ref, mall_ref, *sems[6:9]):
            step()
        mx, my, mc = _place()
        me = 4 * mx + 2 * my + mc
        rowid = lax.broadcasted_iota(jnp.int32, (L * 8, 1), 0)
        vecs_ref[...] = jnp.zeros_like(vecs_ref)
        for l in range(L):
            parts = [jnp.sum(jnp.where(rowid == l * 8 + me, mall_ref[2 * s + mc], 0.0), axis=0, keepdims=True)
                     for s in range(N_SHARD)]
            mod = jnp.concatenate(parts, axis=1) + b_ref[l:l + 1, :]
            for t in range(3):
                vecs_ref[l, t:t + 1, :] = mod[:, t * D:(t + 1) * D]
            vecs_ref[l, 3:4, :] = g_ref[l:l + 1, :]
        w_forward()
        w_finish()

    vmem = pl.BlockSpec(memory_space=pltpu.VMEM)
    return pl.pallas_call(
        body, name="prologue",
        in_specs=[vmem, vmem, vmem, vmem, _ANY], out_specs=[vmem, vmem, _ANY],
        out_shape=[jax.ShapeDtypeStruct((L, 8, D), F32), jax.ShapeDtypeStruct((8, 8, D), F32),
                   jax.ShapeDtypeStruct((8,) + win_first.shape, win_first.dtype)],
        scratch_shapes=[pltpu.VMEM((L * 8, W), F32), pltpu.VMEM((8, L * 8, W), F32)] + _GATHER_SCRATCH * 3,
        compiler_params=pltpu.CompilerParams(vmem_limit_bytes=VMEM_LIMIT_BYTES),
    )(c8, w_ada, b_ada, norm_g, win_first)


def ada_update(dmods, c_t, w, m, v, tr=512):
    L, D, W = w.shape

    def body(dm_ref, c_ref, w_ref, m_ref, v_ref, g_out, d_out, m_out, v_out):
        mx, my, _ = _place()
        shard = 2 * mx + my
        dm = jnp.zeros((8, W), F32)
        for s in range(N_SHARD):
            dm = dm + jnp.where(shard == s, dm_ref[0, :, s * W:(s + 1) * W], 0.0)
        cv = c_ref[...]
        ca = cv * _sigmoid(cv)
        g = jnp.zeros((tr, W), F32)
        for b in range(8):
            g = g + ca[:, b:b + 1] * dm[b:b + 1, :]
        d, m2, v2 = _adamw(w_ref[0], g, m_ref[0], v_ref[0])
        g_out[0], d_out[0], m_out[0], v_out[0] = g, d, m2, v2

    blk = pl.BlockSpec((1, tr, W), lambda l, i: (l, i, 0))
    return pl.pallas_call(
        body, name="ada_update", grid=(L, D // tr),
        in_specs=[pl.BlockSpec((1, 8, 3 * D), lambda l, i: (l, 0, 0)), pl.BlockSpec((tr, 8), lambda l, i: (i, 0)),
                  blk, blk, blk],
        out_specs=[blk] * 4, out_shape=[jax.ShapeDtypeStruct(w.shape, F32)] * 4,
        compiler_params=_cp("arbitrary", "arbitrary"),
    )(dmods, c_t, w, m, v)


STAT_ROWS = 16


def small_update(stats_all, norm, b_ada, final):
    def body(s_ref, *refs):
        ins, outs = refs[:9], refs[9:]
        tot = s_ref[0]
        for k in range(1, 8):
            tot = tot + s_ref[k]
        g_norm = tot[0:2, :]
        g_final = tot[2:3, :]
        g_b = jnp.concatenate(
            [jnp.concatenate([tot[3 + 3 * l + t:4 + 3 * l + t, :] for t in range(3)], axis=1) for l in range(DEPTH)],
            axis=0)
        for p, g in enumerate((g_norm, g_b, g_final)):
            w_ref, m_ref, v_ref = ins[3 * p:3 * p + 3]
            d, m2, v2 = _adamw(w_ref[...], g, m_ref[...], v_ref[...])
            for o_ref, val in zip(outs[4 * p:4 * p + 4], (g, d, m2, v2)):
                o_ref[...] = val
        loss = (0.5 / D_MODEL) * jnp.sum(tot[9:10, :], axis=1, keepdims=True)
        outs[12][...] = jnp.broadcast_to(loss, (8, LANES))

    shapes = []
    for w, _, _ in (norm, b_ada, final):
        shapes += [jax.ShapeDtypeStruct(w.shape, F32)] * 4
    shapes.append(jax.ShapeDtypeStruct((8, LANES), F32))
    return pl.pallas_call(body, name="small_update", out_shape=shapes)(stats_all, *norm, *b_ada, *final)


def kernel(x, c, norm_g, w_ada, b_ada, w_in, w_out, final_g, loss_target, m_norm_g, m_w_ada, m_b_ada, m_w_in, m_w_out, m_final_g, v_norm_g, v_w_ada, v_b_ada, v_w_in, v_w_out, v_final_g):
    S, D = x.shape[1], x.shape[2]
    mc = lax.axis_index("c")
    out_rows = D // N_SHARD

    def my_half(a, rows):
        return lax.dynamic_slice_in_dim(a, mc * rows, rows, axis=0)

    assert DEPTH == 2
    win = [my_half(w_in[l], D // 2).astype(BF16) for l in range(DEPTH)]
    wout = [my_half(w_out[l], out_rows // 2).astype(BF16) for l in range(DEPTH)]
    rest = [jnp.concatenate(wout, axis=0), win[1]]

    def unpack(gathered):
        outs, w3_second = gathered
        outs = outs.reshape(N_SHARD, 2, DEPTH, out_rows // 2, SHARD_W)
        return outs[:, :, 0].reshape(D, D), (w3_second.reshape(N_SHARD, D, SHARD_W), outs[:, :, 1].reshape(D, D))

    vecs, c_all, w3_first = prologue(jnp.broadcast_to(c, (8, D)), w_ada, b_ada, norm_g, win[0])
    c_all, w3_first = c_all[:, 0, :], w3_first.reshape(N_SHARD, D, SHARD_W)

    tabs = (*rope_tables(S), ret_log_gamma())
    saved = [None] * DEPTH
    h, saved[0], wall = layer_fwd(x[0], vecs[0], w3_first, lambda g: unpack(g)[0], tabs, rest)
    weights = [(w3_first, unpack(wall)[0]), unpack(wall)[1]]
    head = (jnp.broadcast_to(final_g[None, :], (8, D)), loss_target[0])
    (dx, st_loss), saved[1], _ = layer_fwd(h, vecs[1], *weights[1], tabs, head=head)

    dmod, dnorm, grads = [None] * DEPTH, [None] * DEPTH, None
    for l in reversed(range(DEPTH)):
        dx, dmod[l], dnorm[l], grads = layer_bwd(dx, saved[l], vecs[l], *weights[l], tabs, grads)

    stats = jnp.concatenate(dnorm + [st_loss[0:1]] + dmod + [st_loss[1:2], jnp.zeros((STAT_ROWS - 10, D), F32)], axis=0)
    p_own, p_sib, stats_all = sum_and_swap(*grads, stats)
    res_in = adam_slab(p_own, p_sib, w_in, m_w_in, v_w_in, 0, "adam_w_in")
    res_out = adam_slab(p_own, p_sib, w_out, m_w_out, v_w_out, DEPTH * D, "adam_w_out", tr=256)

    dmods = stats_all[:, 3:9, :].reshape(8, DEPTH, 3 * D).transpose(1, 0, 2)
    res_ada = ada_update(dmods, c_all.T, w_ada, m_w_ada, v_w_ada)
    small = small_update(stats_all, (norm_g, m_norm_g, v_norm_g), (b_ada, m_b_ada, v_b_ada),
                         (final_g[None, :], m_final_g[None, :], v_final_g[None, :]))
    res_norm, res_b, res_final = small[0:4], small[4:8], [a[0] for a in small[8:12]]
    loss = small[12][0, 0]

    by_kind = [res_norm, res_ada, res_b, res_in, res_out, res_final]
    outs = [loss, dx[None]]
    for kind in range(4):
        outs += [r[kind] for r in by_kind]
    return tuple(outs)
```

```python
import numpy as np
import jax
import jax.numpy as jnp
from jax import lax
from jax.experimental import pallas as pl
from jax.experimental.pallas import tpu as pltpu

F32, BF16 = jnp.float32, jnp.bfloat16
MESH = pl.DeviceIdType.MESH

D_MODEL = 1024
DEPTH = 2
SHARD_W = 1024
N_SHARD = 4
GROUP_W = 512
LANES = 128
SB_HEAD_DIM = 64
RET_HEAD_DIM = 128
CHUNK = 64
ROPE_BASE = 10000.0
EPS = 1e-6
SQ_SCALE = SB_HEAD_DIM ** -0.5
RK_SCALE = RET_HEAD_DIM ** -0.5
SB_T = 1024
SB_CHAINS = 16
SB_NB = 4
RET_T = 256
EXP_ZERO = -104.0
VMEM_LIMIT_BYTES = 56 * 2 ** 20

ADAM_LR, ADAM_B1, ADAM_B2, ADAM_EPS, ADAM_WD, ADAM_STEP = 0.001, 0.9, 0.999, 1e-08, 0.01, 10


def _cp(*sem):
    return pltpu.CompilerParams(dimension_semantics=sem, vmem_limit_bytes=VMEM_LIMIT_BYTES)


def _dot(a, b):
    return lax.dot_general(a, b, (((1,), (0,)), ((), ())), preferred_element_type=F32)


def _dot_nt(a, b):
    return lax.dot_general(a, b, (((1,), (1,)), ((), ())), preferred_element_type=F32)


def _dot_tn(a, b):
    return lax.dot_general(a, b, (((0,), (0,)), ((), ())), preferred_element_type=F32)


def _running_sum(a, tri):
    return _dot(a.astype(BF16), tri)


def _sigmoid(x):
    return 1.0 / (1.0 + jnp.exp(-x))


def _rowsum(a):
    return jnp.sum(a, axis=1, keepdims=True)


def _rowmean(a):
    return jnp.mean(a, axis=1, keepdims=True)


def inproj_fwd(x, vecs, w3, tm=512):
    S, D = x.shape

    def body(x_ref, v_ref, w_ref, ret_ref, sg_ref, h_ref, sb_ref):
        xv = x_ref[...]
        r = lax.rsqrt(_rowmean(xv * xv) + EPS)
        h = xv * r * v_ref[3:4, :] * (1.0 + v_ref[1:2, :]) + v_ref[0:1, :]
        hb = h.astype(BF16)
        h_ref[...] = hb
        for s in range(N_SHARD):
            p = _dot(hb, w_ref[s])
            if s < 2:
                ret_ref[:, s * SHARD_W:(s + 1) * SHARD_W] = p
            if s == 2:
                sb_ref[:, 0:GROUP_W] = (p[:, 0:GROUP_W] * SQ_SCALE).astype(BF16)
                sb_ref[:, GROUP_W:SHARD_W] = p[:, GROUP_W:].astype(BF16)
            if s == 3:
                sb_ref[:, SHARD_W:SHARD_W + GROUP_W] = p[:, 0:GROUP_W].astype(BF16)
                sg_ref[...] = p[:, GROUP_W:]

    row = lambda w: pl.BlockSpec((tm, w), lambda i: (i, 0))
    return pl.pallas_call(
        body, name="inproj_fwd", grid=(S // tm,),
        in_specs=[row(D), pl.BlockSpec((8, D), lambda i: (0, 0)),
                  pl.BlockSpec((N_SHARD, D, SHARD_W), lambda i: (0, 0, 0))],
        out_specs=[row(2 * SHARD_W), row(GROUP_W), row(D), row(3 * GROUP_W)],
        out_shape=[jax.ShapeDtypeStruct((S, 2 * SHARD_W), F32), jax.ShapeDtypeStruct((S, GROUP_W), F32),
                   jax.ShapeDtypeStruct((S, D), BF16), jax.ShapeDtypeStruct((S, 3 * GROUP_W), BF16)],
        compiler_params=_cp("arbitrary"),
    )(x, vecs, w3)


def _sb_logits(qh, k2, keep):
    z = _dot_nt(qh, k2)
    sp = jnp.log(1.0 + jnp.exp(-jnp.abs(z)))
    lb = jnp.minimum(z, 0.0) - sp
    lk = lb - z
    if keep is not None:
        lk = jnp.where(keep, lk, 0.0)
    return lb, lk


class _sb_chains:
    def __init__(self, i, q2, do_b=None):
        t = self.t = SB_T // SB_CHAINS
        self.C = range(SB_CHAINS)
        r = lax.broadcasted_iota(jnp.int32, (SB_NB * t, SB_NB * t), 0)
        c = lax.broadcasted_iota(jnp.int32, (SB_NB * t, SB_NB * t), 1)
        self.later_all = jnp.where(r > c, 1.0, 0.0).astype(BF16)
        self.earlier_all = jnp.where(r < c, 1.0, 0.0).astype(BF16)
        self.later, self.earlier = self.later_all[:t, :t], self.earlier_all[:t, :t]
        self.head0 = lax.broadcasted_iota(jnp.int32, (1, LANES), 1) < SB_HEAD_DIM
        row = lax.broadcasted_iota(jnp.int32, (2 * t, SB_NB * t), 0) & (t - 1)
        col = lax.broadcasted_iota(jnp.int32, (2 * t, SB_NB * t), 1)
        qt = [SB_CHAINS * i + cc for cc in self.C]
        self.first = [jnp.maximum(qt[cc] - (SB_NB - 1), 0) for cc in self.C]
        self._ahead, self._qt = col - row, qt
        self.qs = [self._stack(q2[cc * t:(cc + 1) * t]) for cc in self.C]
        if do_b is not None:
            self.dos = [self._stack(do_b[cc * t:(cc + 1) * t]) for cc in self.C]

    def keep_of(self, cc):
        return self._ahead < (self._qt[cc] - self.first[cc]) * self.t

    def _stack(self, a):
        zero = jnp.zeros_like(a)
        return jnp.concatenate([jnp.where(self.head0, a, zero), jnp.where(self.head0, zero, a)], axis=0)

    def rows(self, ref, j, n):
        return ref[pl.ds(pl.multiple_of(j * self.t, self.t), n * self.t), :]

    def suffix(self, lk):
        return _running_sum(lk, self.later_all), _rowsum(lk)

    def prefix(self, g, G0):
        return _running_sum(g, self.earlier_all) + G0


def sb_fwd(sb, sg, gather=None):
    S = sb.shape[0]
    T = SB_T
    nq = S // T
    carried = list(gather or ())
    ng = len(carried)

    def body(*refs):
        (q_ref, k_ref, v_ref, sg_ref), refs = refs[:4], refs[4:]
        x_refs, (y_ref, o_ref, end_ref), out_refs, sems = refs[:ng], refs[ng:ng + 3], refs[ng + 3:2 * ng + 3], refs[2 * ng + 3:]
        p, i = pl.program_id(0), pl.program_id(1)
        gathers = [_gather_ops(x_refs[g], out_refs[g], *sems[3 * g:3 * g + 3]) for g in range(ng)]
        for start, forward, _ in gathers:
            pl.when(jnp.logical_and(p == 0, i == 0))(start)
            pl.when(jnp.logical_and(p == 3, i == 0))(forward)
        ch = _sb_chains(i, q_ref[...])
        later, head0 = ch.later, ch.head0
        lbk = [_sb_logits(ch.qs[c], ch.rows(k_ref, ch.first[c], SB_NB), ch.keep_of(c)) for c in ch.C]
        suffix, R = zip(*[ch.suffix(lbk[c][1]) for c in ch.C])
        aa = [jnp.where(ch.keep_of(c), jnp.exp(lbk[c][0] + suffix[c]), 0.0) for c in ch.C]
        acc = [_dot(aa[c].astype(BF16), ch.rows(v_ref, ch.first[c], SB_NB)) for c in ch.C]

        nc = len(ch.C)

        def alive(n, Rs):
            m = None
            for c in ch.C:
                rc = jnp.where(ch.first[c] - n > 0, Rs[c], EXP_ZERO)
                m = rc if m is None else jnp.maximum(m, rc)
            return jnp.max(m)

        def cond(st):
            return st[-1] > EXP_ZERO

        def step(st):
            n, accs, Rs = st[0], list(st[1:1 + nc]), list(st[1 + nc:1 + 2 * nc])
            for c in ch.C:
                j = ch.first[c] - 1 - n
                jc = jnp.maximum(j, 0)
                lb, lk = _sb_logits(ch.qs[c], ch.rows(k_ref, jc, 1), None)
                a = jnp.exp(lb + _running_sum(lk, later) + Rs[c])
                cx = _dot(a.astype(BF16), ch.rows(v_ref, jc, 1))
                accs[c] = jnp.where(j >= 0, accs[c] + cx, accs[c])
                Rs[c] = jnp.where(j >= 0, Rs[c] + _rowsum(lk), Rs[c])
            return (n + 1, *accs, *Rs, alive(n + 1, Rs))

        st = lax.while_loop(cond, step, (jnp.int32(0), *acc, *R, alive(0, R)))
        n_end, acc, R = st[0], st[1:1 + nc], st[1 + nc:1 + 2 * nc]
        outs = []
        for c in ch.C:
            base = c * (2 * ch.t + 8)
            end_ref[0, 0, base:base + 2 * ch.t, :] = jnp.broadcast_to(R[c], (2 * ch.t, 8))
            end_ref[0, 0, base + 2 * ch.t:base + 2 * ch.t + 8, :] = jnp.full((8, 8), n_end.astype(F32))
            outs.append(jnp.where(head0, acc[c][:ch.t], acc[c][ch.t:]))
        o = jnp.concatenate(outs, axis=0)
        o_ref[...] = o
        sg = sg_ref[...]
        y_ref[...] = (o * (sg * _sigmoid(sg))).astype(BF16)
        for _, _, finish in gathers:
            pl.when(jnp.logical_and(p == 3, i == nq - 1))(finish)

    return pl.pallas_call(
        body, name="sb_fwd", grid=(4, nq),
        in_specs=[pl.BlockSpec((T, LANES), lambda p, i: (i, p)),
                  pl.BlockSpec((S, LANES), lambda p, i: (0, 4 + p)),
                  pl.BlockSpec((S, LANES), lambda p, i: (0, 8 + p)),
                  pl.BlockSpec((T, LANES), lambda p, i: (i, p))] + [_ANY for _ in carried],
        out_specs=[pl.BlockSpec((T, LANES), lambda p, i: (i, p)),
                   pl.BlockSpec((T, LANES), lambda p, i: (i, p)),
                   pl.BlockSpec((1, 1, SB_CHAINS * (2 * T // SB_CHAINS + 8), 8), lambda p, i: (p, i, 0, 0))] + [_ANY for _ in carried],
        out_shape=[jax.ShapeDtypeStruct((S, GROUP_W), BF16),
                   jax.ShapeDtypeStruct((S, GROUP_W), F32),
                   jax.ShapeDtypeStruct((4, nq, SB_CHAINS * (2 * T // SB_CHAINS + 8), 8), F32)]
        + [jax.ShapeDtypeStruct((8,) + a.shape, a.dtype) for a in carried],
        scratch_shapes=_GATHER_SCRATCH * ng,
        compiler_params=_cp("arbitrary", "arbitrary"),
    )(sb, sb, sb, sg, *carried)


def sb_bwd(sb, sg, o, sb_end, dycat, ship=None):
    S = sb.shape[0]
    T = SB_T
    nq = S // T
    ex = _Exchange(ship)

    def body(*refs):
        (q_ref, k_ref, v_ref, sg_ref, o_ref, dy_ref, end_ref), refs = refs[:7], refs[7:]
        ship_refs, (dq_ref, dk_ref, dv_ref, dsg_ref), refs = refs[:ex.n_in], refs[ex.n_in:ex.n_in + 4], refs[ex.n_in + 4:]
        recv, (dk_acc, dv_acc), sems = refs[:ex.n_out], refs[ex.n_out:ex.n_out + 2], refs[ex.n_out + 2:]
        start, finish = ex.ops(ship_refs, recv + sems)
        p, i = pl.program_id(0), pl.program_id(1)
        pl.when(jnp.logical_and(p == 0, i == 0))(start)

        @pl.when(i == 0)
        def _():
            dk_acc[...] = jnp.zeros_like(dk_acc)
            dv_acc[...] = jnp.zeros_like(dv_acc)

        sg = sg_ref[...]
        sig = _sigmoid(sg)
        dy = dy_ref[...]
        dsg_ref[...] = (dy * o_ref[...] * (sig * (1.0 + sg * (1.0 - sig)))).astype(BF16)
        do_b = (dy * (sg * sig)).astype(BF16)
        ch = _sb_chains(i, q_ref[...], do_b)
        later, earlier, head0, t = ch.later, ch.earlier, ch.head0, ch.t
        end = end_ref[0, 0]

        def grads(c, j, n, a, lb, g, G, keep):
            dz = g - jnp.exp(lb) * (g + G)
            if keep is not None:
                dz = jnp.where(keep, dz, 0.0)
            dzb = dz.astype(BF16)
            rows = pl.ds(pl.multiple_of(j * t, t), n * t)
            dk_acc[rows, :] += _dot_tn(dzb, ch.qs[c])
            dv_acc[rows, :] += _dot_tn(a.astype(BF16), ch.dos[c])
            return _dot(dzb, ch.rows(k_ref, j, n))

        nc = len(ch.C)
        n_end = jnp.max(end[2 * t:2 * t + 8, :]).astype(jnp.int32)

        def sweep(m, st):
            dqs, G0s, lefts = list(st[:nc]), list(st[nc:2 * nc]), list(st[2 * nc:])
            for c in ch.C:
                j = ch.first[c] - n_end + m
                jc = jnp.maximum(j, 0)
                lb, lk = _sb_logits(ch.qs[c], ch.rows(k_ref, jc, 1), None)
                stick = lefts[c] - _rowsum(lk)
                a = jnp.where(j >= 0, jnp.exp(lb + _running_sum(lk, later) + stick), 0.0)
                g = a * _dot_nt(ch.dos[c], ch.rows(v_ref, jc, 1))
                G = _running_sum(g, earlier) + G0s[c]
                dqs[c] = dqs[c] + grads(c, jc, 1, a, lb, jnp.where(j >= 0, g, 0.0), jnp.where(j >= 0, G, 0.0), None)
                G0s[c] = G0s[c] + _rowsum(g)
                lefts[c] = jnp.where(j >= 0, stick, lefts[c])
            return (*dqs, *G0s, *lefts)

        lefts = [end[c * (2 * t + 8):c * (2 * t + 8) + 2 * t, 0:1] for c in ch.C]
        st = lax.fori_loop(0, n_end, sweep, (*[jnp.zeros((2 * t, LANES), F32)] * nc,
                                             *[jnp.zeros((2 * t, 1), F32)] * nc, *lefts))
        dq, G0 = st[:nc], st[nc:2 * nc]

        lbk = [_sb_logits(ch.qs[c], ch.rows(k_ref, ch.first[c], SB_NB), ch.keep_of(c)) for c in ch.C]
        suffix = [ch.suffix(lbk[c][1])[0] for c in ch.C]
        aa = [jnp.where(ch.keep_of(c), jnp.exp(lbk[c][0] + suffix[c]), 0.0) for c in ch.C]
        g = [aa[c] * _dot_nt(ch.dos[c], ch.rows(v_ref, ch.first[c], SB_NB)) for c in ch.C]
        G = [ch.prefix(g[c], G0[c]) for c in ch.C]
        for c in ch.C:
            dqc = dq[c] + grads(c, ch.first[c], SB_NB, aa[c], lbk[c][0], g[c], G[c], ch.keep_of(c))
            dq_ref[c * t:(c + 1) * t, :] = (jnp.where(head0, dqc[:t], dqc[t:]) * SQ_SCALE).astype(BF16)

        @pl.when(i == nq - 1)
        def _():
            dk_ref[...] = dk_acc[...].astype(BF16)
            dv_ref[...] = dv_acc[...].astype(BF16)

        pl.when(jnp.logical_and(p == 3, i == nq - 1))(finish)

    tile_spec = lambda c0: pl.BlockSpec((T, LANES), lambda p, i: (i, c0 + p))
    head_spec = lambda c0: pl.BlockSpec((S, LANES), lambda p, i: (0, c0 + p))
    return pl.pallas_call(
        body, name="sb_bwd", grid=(4, nq),
        in_specs=[tile_spec(0), head_spec(4), head_spec(8), tile_spec(0), tile_spec(0), tile_spec(4),
                  pl.BlockSpec((1, 1, SB_CHAINS * (2 * T // SB_CHAINS + 8), 8), lambda p, i: (p, i, 0, 0))] + ex.in_specs,
        out_specs=[tile_spec(0), head_spec(0), head_spec(0), tile_spec(0)] + ex.out_specs,
        out_shape=[jax.ShapeDtypeStruct((S, GROUP_W), BF16)] * 4 + ex.out_shape,
        scratch_shapes=[pltpu.VMEM((S, LANES), F32), pltpu.VMEM((S, LANES), F32)] + ex.scratch,
        compiler_params=_cp("arbitrary", "arbitrary"),
    )(sb, sb, sb, sg, o, dycat, sb_end, *ex.ship)


def rope_tables(S):
    half = RET_HEAD_DIM // 2
    lane = jnp.arange(RET_HEAD_DIM)
    inv = ROPE_BASE ** (-(lane % half).astype(F32) / half)
    ang = jnp.arange(S, dtype=F32)[:, None] * inv[None, :]
    return jnp.cos(ang), jnp.where(lane < half, -1.0, 1.0)[None, :] * jnp.sin(ang)


def ret_log_gamma():
    return jnp.log1p(-(2.0 ** (-5.0 - jnp.arange(4, dtype=F32))))


def _swap_halves(a):
    return pltpu.roll(a, RET_HEAD_DIM // 2, axis=1)


def _ret_decay_mask(lg):
    n = lax.broadcasted_iota(jnp.int32, (RET_T, RET_T), 0)
    m = lax.broadcasted_iota(jnp.int32, (RET_T, RET_T), 1)
    dist = jnp.abs(n - m).astype(F32)
    return jnp.where((m // CHUNK) <= (n // CHUNK), jnp.exp(lg * dist), 0.0)


def _ret_block(lg, rq, rk, rv, cosf, sinf, dm):
    q = rq * cosf + _swap_halves(rq) * sinf
    k = (rk * cosf + _swap_halves(rk) * sinf) * RK_SCALE
    qb, kb, vb = q.astype(BF16), k.astype(BF16), rv.astype(BF16)
    sc = _dot_nt(qb, kb) * dm
    nloc = lax.broadcasted_iota(jnp.int32, (RET_T, 1), 0).astype(F32)
    qdec = jnp.exp(lg * (nloc + 1.0))
    kdec = jnp.exp(lg * (RET_T - 1.0 - nloc))
    block_dec = jnp.exp(jnp.full((1, LANES), lg * RET_T, F32))
    return q, k, qb, kb, vb, sc, qdec, kdec, block_dec


RET_RB = 2


def _ret_specs(S, rb):
    group = lambda c0: pl.BlockSpec((RET_RB * RET_T, GROUP_W), lambda s: (rb(s), c0))
    return group, pl.BlockSpec((RET_RB * RET_T, LANES), lambda s: (rb(s), 0))


def _ret_chains():
    chains = [(h, b) for b in range(RET_RB) for h in range(4)]
    rows = lambda c: (slice(c[1] * RET_T, (c[1] + 1) * RET_T), slice(c[0] * LANES, (c[0] + 1) * LANES))
    tab = lambda ref, c: ref[c[1] * RET_T:(c[1] + 1) * RET_T, :]
    return chains, rows, tab


def _ret_blocks(chains, rows, lg_ref, rq_ref, rk_ref, rv_ref, cosf, sinf, dm_ref):
    blk = {c: _ret_block(lg_ref[c[0]], rq_ref[rows(c)], rk_ref[rows(c)], rv_ref[rows(c)],
                         cosf[c], sinf[c], dm_ref[c[0]]) for c in chains}
    return ({c: blk[c][n] for c in chains} for n in range(9))


def ret_fwd(proj, cosf, sinf, lgam):
    S = proj.shape[0]
    nb = S // RET_T
    group, row_tab = _ret_specs(S, lambda s: s)

    def body(lg_ref, rq_ref, rk_ref, rv_ref, rg_ref, cos_ref, sin_ref, y_ref, o_ref, st_out, st_ref, dm_ref):
        @pl.when(pl.program_id(0) == 0)
        def _():
            st_ref[...] = jnp.zeros_like(st_ref)
            for h in range(4):
                dm_ref[h] = _ret_decay_mask(lg_ref[h])

        chains, rows, tab = _ret_chains()
        cosf, sinf = {c: tab(cos_ref, c) for c in chains}, {c: tab(sin_ref, c) for c in chains}
        q, k, qb, kb, vb, sc, qdec, kdec, block_dec = _ret_blocks(
            chains, rows, lg_ref, rq_ref, rk_ref, rv_ref, cosf, sinf, dm_ref)
        kv = {c: _dot_tn((k[c] * kdec[c]).astype(BF16), vb[c]) for c in chains}
        st = {(h, 0): st_ref[h] for h in range(4)}
        for b in range(RET_RB):
            for h in range(4):
                st[(h, b + 1)] = st[(h, b)] * block_dec[(h, b)] + kv[(h, b)]
        for h, b in chains:
            st_out[h, b] = st[(h, b)]
        for h in range(4):
            st_ref[h] = st[(h, RET_RB)]
        o = {c: _dot(sc[c].astype(BF16), vb[c]) + _dot(qb[c], st[c].astype(BF16)) * qdec[c] for c in chains}
        for c in chains:
            o_ref[rows(c)] = o[c]
        cen = {c: o[c] - _rowmean(o[c]) for c in chains}
        on = {c: cen[c] * lax.rsqrt(_rowmean(cen[c] * cen[c]) + EPS) for c in chains}
        rg = {c: rg_ref[rows(c)] for c in chains}
        for c in chains:
            y_ref[rows(c)] = (on[c] * (rg[c] * _sigmoid(rg[c]))).astype(BF16)

    return pl.pallas_call(
        body, name="ret_fwd", grid=(nb // RET_RB,),
        in_specs=[pl.BlockSpec(memory_space=pltpu.SMEM),
                  group(0), group(1), group(2), group(3), row_tab, row_tab],
        out_specs=[group(0), group(0),
                   pl.BlockSpec((4, RET_RB, LANES, LANES), lambda s: (0, s, 0, 0))],
        out_shape=[jax.ShapeDtypeStruct((S, GROUP_W), BF16),
                   jax.ShapeDtypeStruct((S, GROUP_W), F32),
                   jax.ShapeDtypeStruct((4, nb, LANES, LANES), F32)],
        scratch_shapes=[pltpu.VMEM((4, LANES, LANES), F32), pltpu.VMEM((4, RET_T, RET_T), F32)],
        compiler_params=_cp("arbitrary"),
    )(lgam, proj, proj, proj, proj, cosf, sinf)


def ret_bwd(proj, cosf, sinf, lgam, o, states, dycat):
    S = proj.shape[0]
    nsteps = S // RET_T // RET_RB
    rev = lambda s: nsteps - 1 - s
    group, row_tab = _ret_specs(S, rev)

    def body(lg_ref, rq_ref, rk_ref, rv_ref, rg_ref, cos_ref, sin_ref, o_ref, st_in, dy_ref,
             drq_ref, drk_ref, drv_ref, drg_ref, ds_ref, dm_ref):
        @pl.when(pl.program_id(0) == 0)
        def _():
            ds_ref[...] = jnp.zeros_like(ds_ref)
            for h in range(4):
                dm_ref[h] = _ret_decay_mask(lg_ref[h])

        chains, rows, tab = _ret_chains()
        cosf, sinf = {c: tab(cos_ref, c) for c in chains}, {c: tab(sin_ref, c) for c in chains}
        dms = {c: dm_ref[c[0]] for c in chains}
        q, k, qb, kb, vb, sc, qdec, kdec, block_dec = _ret_blocks(
            chains, rows, lg_ref, rq_ref, rk_ref, rv_ref, cosf, sinf, dm_ref)
        o_v = {c: o_ref[rows(c)] for c in chains}
        cen = {c: o_v[c] - _rowmean(o_v[c]) for c in chains}
        rstd = {c: lax.rsqrt(_rowmean(cen[c] * cen[c]) + EPS) for c in chains}
        on = {c: cen[c] * rstd[c] for c in chains}
        rg = {c: rg_ref[rows(c)] for c in chains}
        sig = {c: _sigmoid(rg[c]) for c in chains}
        dy = {c: dy_ref[rows(c)] for c in chains}
        for c in chains:
            drg_ref[rows(c)] = (dy[c] * on[c] * (sig[c] * (1.0 + rg[c] * (1.0 - sig[c])))).astype(BF16)
        don = {c: dy[c] * (rg[c] * sig[c]) for c in chains}
        do = {c: rstd[c] * (don[c] - _rowmean(don[c]) - on[c] * _rowmean(don[c] * on[c])) for c in chains}
        dob = {c: do[c].astype(BF16) for c in chains}
        dsc = {c: (_dot_nt(dob[c], vb[c]) * dms[c]).astype(BF16) for c in chains}
        st_b = {c: st_in[c[0], c[1]].astype(BF16) for c in chains}
        dst = {c: _dot_tn((q[c] * qdec[c]).astype(BF16), dob[c]) for c in chains}
        dsn = {(h, RET_RB): ds_ref[h] for h in range(4)}
        for b in reversed(range(RET_RB)):
            for h in range(4):
                dsn[(h, b)] = dsn[(h, b + 1)] * block_dec[(h, b)] + dst[(h, b)]
        for h in range(4):
            ds_ref[h] = dsn[(h, 0)]
        dsn_b = {c: dsn[(c[0], c[1] + 1)].astype(BF16) for c in chains}
        dq = {c: _dot(dsc[c], kb[c]) + _dot_nt(dob[c], st_b[c]) * qdec[c] for c in chains}
        dk = {c: (_dot_tn(dsc[c], qb[c]) + _dot_nt(vb[c], dsn_b[c]) * kdec[c]) * RK_SCALE for c in chains}
        dv = {c: _dot_tn(sc[c].astype(BF16), dob[c]) + _dot((k[c] * kdec[c]).astype(BF16), dsn_b[c])
              for c in chains}
        for c in chains:
            drq_ref[rows(c)] = (dq[c] * cosf[c] + _swap_halves(dq[c] * sinf[c])).astype(BF16)
            drk_ref[rows(c)] = (dk[c] * cosf[c] + _swap_halves(dk[c] * sinf[c])).astype(BF16)
            drv_ref[rows(c)] = dv[c].astype(BF16)

    return pl.pallas_call(
        body, name="ret_bwd", grid=(nsteps,),
        in_specs=[pl.BlockSpec(memory_space=pltpu.SMEM),
                  group(0), group(1), group(2), group(3), row_tab, row_tab,
                  group(0), pl.BlockSpec((4, RET_RB, LANES, LANES), lambda s: (0, rev(s), 0, 0)),
                  group(0)],
        out_specs=[group(0)] * 4,
        out_shape=[jax.ShapeDtypeStruct((S, GROUP_W), BF16)] * 4,
        scratch_shapes=[pltpu.VMEM((4, LANES, LANES), F32), pltpu.VMEM((4, RET_T, RET_T), F32)],
        compiler_params=_cp("arbitrary"),
    )(lgam, proj, proj, proj, proj, cosf, sinf, o, states, dycat)


def outproj_fwd(x, vecs, y_ret, y_sb, w_out, head=None, tm=1024):
    S, D = x.shape
    tm = min(tm, S)
    last = list(head or ())

    def body(x_ref, v_ref, yr_ref, ys_ref, w_ref, *refs):
        y = _dot(yr_ref[...], w_ref[0:GROUP_W, :]) + _dot(ys_ref[...], w_ref[GROUP_W:, :])
        xv = x_ref[...] + v_ref[2:3, :] * y
        if not last:
            y_ref, xo_ref = refs
            y_ref[...] = y.astype(BF16)
            xo_ref[...] = xv
            return
        g_ref, t_ref, y_ref, dx_ref, st_ref = refs
        y_ref[...] = y.astype(BF16)

        @pl.when(pl.program_id(0) == 0)
        def _():
            st_ref[...] = jnp.zeros_like(st_ref)

        g = g_ref[0:1, :]
        for hh in range(2):
            rows = slice(hh * (tm // 2), (hh + 1) * (tm // 2))
            xh = xv[rows, :]
            r = lax.rsqrt(_rowmean(xh * xh) + EPS)
            xn = xh * r
            err = xn * g - t_ref[rows, :]
            dy = err * (1.0 / D)
            dxn = dy * g
            dx_ref[rows, :] = r * (dxn - xn * _rowmean(dxn * xn))
            st_ref[0:1, :] += jnp.sum(dy * xn, axis=0, keepdims=True)
            st_ref[1:2, :] += jnp.sum(err * err, axis=0, keepdims=True)

    row = lambda w: pl.BlockSpec((tm, w), lambda i: (i, 0))
    fixed = pl.BlockSpec((8, D), lambda i: (0, 0))
    return pl.pallas_call(
        body, name="outproj_fwd", grid=(S // tm,),
        in_specs=[row(D), fixed, row(GROUP_W), row(GROUP_W), pl.BlockSpec((D, D), lambda i: (0, 0))]
        + ([fixed, row(D)] if last else []),
        out_specs=[row(D), row(D)] + ([fixed] if last else []),
        out_shape=[jax.ShapeDtypeStruct((S, D), BF16), jax.ShapeDtypeStruct((S, D), F32)]
        + ([jax.ShapeDtypeStruct((8, D), F32)] if last else []),
        compiler_params=_cp("arbitrary"),
    )(x, vecs, y_ret, y_sb, w_out, *last)


def outproj_bwd(dx, y, vecs, y_ret, y_sb, w_out, tm=1024):
    S, D = dx.shape
    tm = min(tm, S)
    n = S // tm

    def body(dx_ref, y_ref, v_ref, yr_ref, ys_ref, w_ref, dyc_ref, dw_ref, st_ref, acc):
        i = pl.program_id(0)

        @pl.when(i == 0)
        def _():
            st_ref[...] = jnp.zeros_like(st_ref)
            acc[...] = jnp.zeros_like(acc)

        dxv = dx_ref[...]
        st_ref[0:1, :] += jnp.sum(dxv * y_ref[...].astype(F32), axis=0, keepdims=True)
        dyy = (dxv * v_ref[2:3, :]).astype(BF16)
        dyc_ref[...] = _dot_nt(dyy, w_ref[...])
        acc[0:GROUP_W, :] += _dot_tn(yr_ref[...], dyy)
        acc[GROUP_W:, :] += _dot_tn(ys_ref[...], dyy)

        @pl.when(i == n - 1)
        def _():
            dw_ref[...] = acc[...].astype(BF16)

    row = lambda w: pl.BlockSpec((tm, w), lambda i: (i, 0))
    fixed = lambda r: pl.BlockSpec((r, D), lambda i: (0, 0))
    return pl.pallas_call(
        body, name="outproj_bwd", grid=(n,),
        in_specs=[row(D), row(D), fixed(8), row(GROUP_W), row(GROUP_W), fixed(D)],
        out_specs=[row(D), fixed(D), fixed(8)],
        out_shape=[jax.ShapeDtypeStruct((S, D), F32), jax.ShapeDtypeStruct((D, D), BF16),
                   jax.ShapeDtypeStruct((8, D), F32)],
        scratch_shapes=[pltpu.VMEM((D, D), F32)],
        compiler_params=_cp("arbitrary"),
    )(dx, y, vecs, y_ret, y_sb, w_out)


def inproj_bwd_x(pieces, w3, x, vecs, dx_res, ship=None, tm=512):
    S, D = x.shape
    n = S // tm
    ex = _Exchange(ship)

    def body(*refs):
        p_refs, (w_ref, x_ref, v_ref, dr_ref), refs = refs[:8], refs[8:12], refs[12:]
        ship_refs, (dx_ref, st_ref), refs = refs[:ex.n_in], refs[ex.n_in:ex.n_in + 2], refs[ex.n_in + 2:]
        start, finish = ex.ops(ship_refs, refs)

        @pl.when(pl.program_id(0) == 0)
        def _():
            st_ref[...] = jnp.zeros_like(st_ref)
            start()

        g, scale1 = v_ref[3:4, :], 1.0 + v_ref[1:2, :]
        halves = [slice(hh * (tm // 2), (hh + 1) * (tm // 2)) for hh in range(2)]
        dhs = []
        for rows in halves:
            dh = jnp.zeros((tm // 2, D), F32)
            for k, p_ref in enumerate(p_refs):
                c0 = (k % 2) * GROUP_W
                dh = dh + _dot_nt(p_ref[rows, :], w_ref[k // 2, :, c0:c0 + GROUP_W])
            dhs.append(dh)
        for rows, dh in zip(halves, dhs):
            xv = x_ref[rows, :]
            r = lax.rsqrt(_rowmean(xv * xv) + EPS)
            xn = xv * r
            st_ref[0:1, :] += jnp.sum(dh, axis=0, keepdims=True)
            dh_xn = dh * xn
            st_ref[1:2, :] += jnp.sum(dh_xn, axis=0, keepdims=True) * g
            st_ref[2:3, :] += jnp.sum(dh_xn, axis=0, keepdims=True) * scale1
            dxn = dh * (g * scale1)
            dx_ref[rows, :] = r * (dxn - xn * _rowmean(dxn * xn)) + dr_ref[rows, :]
        pl.when(pl.program_id(0) == n - 1)(finish)

    row = lambda w: pl.BlockSpec((tm, w), lambda i: (i, 0))
    return pl.pallas_call(
        body, name="inproj_bwd_x", grid=(n,),
        in_specs=[row(GROUP_W)] * 8 + [pl.BlockSpec((N_SHARD, D, SHARD_W), lambda i: (0, 0, 0)),
                                       row(D), pl.BlockSpec((8, D), lambda i: (0, 0)), row(D)] + ex.in_specs,
        out_specs=[row(D), pl.BlockSpec((8, D), lambda i: (0, 0))] + ex.out_specs,
        out_shape=[jax.ShapeDtypeStruct((S, D), F32), jax.ShapeDtypeStruct((8, D), F32)] + ex.out_shape,
        scratch_shapes=ex.scratch,
        compiler_params=_cp("arbitrary"),
    )(*pieces, w3, x, vecs, dx_res, *ex.ship)


def inproj_bwd_w(h, pieces, tm=1024):
    S, D = h.shape
    tm = min(tm, S)
    n = S // tm

    def body(*refs):
        h_ref, p_refs, dw_ref, acc = refs[0], refs[1:9], refs[9], refs[10]
        i = pl.program_id(0)

        @pl.when(i == 0)
        def _():
            acc[...] = jnp.zeros_like(acc)

        hv = h_ref[...]
        for k, p_ref in enumerate(p_refs):
            c0 = (k % 2) * GROUP_W
            acc[k // 2, :, c0:c0 + GROUP_W] += _dot_tn(hv, p_ref[...])

        @pl.when(i == n - 1)
        def _():
            dw_ref[...] = acc[...].astype(BF16)

    row = lambda w: pl.BlockSpec((tm, w), lambda i: (i, 0))
    return pl.pallas_call(
        body, name="inproj_bwd_w", grid=(n,),
        in_specs=[row(D)] + [row(GROUP_W)] * 8,
        out_specs=pl.BlockSpec((N_SHARD, D, SHARD_W), lambda i: (0, 0, 0), pipeline_mode=pl.Buffered(1)),
        out_shape=jax.ShapeDtypeStruct((N_SHARD, D, SHARD_W), BF16),
        scratch_shapes=[pltpu.VMEM((N_SHARD, D, SHARD_W), F32)],
        compiler_params=_cp("arbitrary"),
    )(h, *pieces)


def layer_fwd(x, vecs, w3, w_out, tabs, gather=None, head=None):
    cosf, sinf, lgam = tabs
    ret, sg, h, sb = inproj_fwd(x, vecs, w3)
    y_ret, o_ret, states = ret_fwd(ret, cosf, sinf, lgam)
    y_sb, o_sb, sb_end, *gathered = sb_fwd(sb, sg, gather)
    if callable(w_out):
        w_out = w_out(gathered)
    y, *x_next = outproj_fwd(x, vecs, y_ret, y_sb, w_out, head)
    saved = (x, ret, sg, h, sb, y_ret, o_ret, states, y_sb, o_sb, sb_end, y)
    return (x_next[0] if head is None else x_next), saved, gathered


def _by_shard(dw_out):
    return dw_out.reshape(N_SHARD, D_MODEL // N_SHARD, D_MODEL)


def layer_bwd(dx, saved, vecs, w3, w_out, tabs, later_grads=None):
    cosf, sinf, lgam = tabs
    x, ret, sg, h, sb, y_ret, o_ret, states, y_sb, o_sb, sb_end, y = saved
    dycat, dw_out, st_o = outproj_bwd(dx, y, vecs, y_ret, y_sb, w_out)
    dw_out = _by_shard(dw_out)
    ship = None if later_grads is None else (later_grads[0], dw_out, later_grads[1])
    *d_sb, = sb_bwd(sb, sg, o_sb, sb_end, dycat, ship)
    d_ret = ret_bwd(ret, cosf, sinf, lgam, o_ret, states, dycat)
    pieces = list(d_ret) + d_sb[:4]
    dw_in = inproj_bwd_w(h, pieces)
    dx, st_i, *recv_in = inproj_bwd_x(pieces, w3, x, vecs, dx, None if later_grads is None else (dw_in,))
    dmod = jnp.concatenate([st_i[0:2], st_o[0:1]], axis=0)
    grads = (dw_in, dw_out) if later_grads is None else (recv_in[0], d_sb[4])
    return dx, dmod, st_i[2:3], grads


def _place():
    return lax.axis_index("x"), lax.axis_index("y"), lax.axis_index("c")


def _other_chips(mx, my):
    return [(1 - mx, my), (mx, 1 - my), (1 - mx, 1 - my)]


_ANY = pl.BlockSpec(memory_space=pl.ANY)


_GATHER_SCRATCH = [pltpu.SemaphoreType.DMA((7,)), pltpu.SemaphoreType.DMA((7,)), pltpu.SemaphoreType.DMA(())]


def _gather_ops(x_ref, out_ref, send_sems, recv_sems, local_sem):
    mx, my, mc = _place()
    me, sibling = (mx, my, mc), (mx, my, 1 - mc)
    chips = _other_chips(mx, my)

    def slot(px, py, pc):
        return out_ref.at[4 * px + 2 * py + pc]

    def copy(k, block, to, src=None):
        return pltpu.make_async_remote_copy(
            src_ref=slot(*block) if src is None else src, dst_ref=slot(*block),
            send_sem=send_sems.at[k], recv_sem=recv_sems.at[k], device_id=to, device_id_type=MESH)

    mine = pltpu.make_async_copy(x_ref, slot(*me), local_sem)
    first = [copy(0, me, sibling, src=x_ref)]
    first += [copy(1 + j, me, (*chip, mc), src=x_ref) for j, chip in enumerate(chips)]
    passed = [copy(4 + j, (*chip, mc), sibling) for j, chip in enumerate(chips)]

    def start():
        mine.start()
        for cp in first:
            cp.start()

    def forward():
        for j, chip in enumerate(chips):
            copy(1 + j, (*chip, mc), me).wait_recv()
            passed[j].start()

    def finish():
        copy(0, sibling, me).wait_recv()
        for j, chip in enumerate(chips):
            copy(4 + j, (*chip, 1 - mc), me).wait_recv()
        for cp in first + passed:
            cp.wait_send()
        mine.wait()

    return start, forward, finish


class _Exchange:
    def __init__(self, ship):
        self.ship = list(ship or ())
        self.n_in = len(self.ship)
        self.n_out = 1 if self.ship else 0
        self.rows = [a.shape[1] for a in self.ship]
        self.in_specs = [_ANY] * self.n_in
        self.out_specs = [_ANY] * self.n_out
        self.out_shape = [jax.ShapeDtypeStruct((N_SHARD, sum(self.rows), SHARD_W), BF16)] * self.n_out
        sem = pltpu.SemaphoreType.DMA
        self.scratch = [sem((3,)), sem((3,)), sem(())] * self.n_out

    def ops(self, ship_refs, tail):
        if not self.ship:
            return (lambda: None), (lambda: None)
        recv, send_sems, recv_sems, local_sem = tail
        mx, my, mc = _place()
        my_chip = 2 * mx + my
        chips = _other_chips(mx, my)

        def pieces(s):
            firsts = np.cumsum([0] + self.rows[:-1])
            return [(ref.at[s], int(r0), n) for ref, r0, n in zip(ship_refs, firsts, self.rows)]

        def start():
            for src, r0, n in pieces(my_chip):
                pltpu.make_async_copy(src, recv.at[my_chip, pl.ds(r0, n)], local_sem).start()
            for j, (px, py) in enumerate(chips):
                for src, r0, n in pieces(2 * px + py):
                    pltpu.make_async_remote_copy(
                        src_ref=src, dst_ref=recv.at[my_chip, pl.ds(r0, n)],
                        send_sem=send_sems.at[j], recv_sem=recv_sems.at[j],
                        device_id=(px, py, mc), device_id_type=MESH).start()

        def finish():
            for j, (px, py) in enumerate(chips):
                whole = recv.at[2 * px + py]
                both = pltpu.make_async_remote_copy(
                    src_ref=whole, dst_ref=whole, send_sem=send_sems.at[j], recv_sem=recv_sems.at[j],
                    device_id=(px, py, mc), device_id_type=MESH)
                both.wait_recv()
                both.wait_send()
            pltpu.make_async_copy(recv.at[my_chip], recv.at[my_chip], local_sem).wait()

        return start, finish


def sum_and_swap(recv_a, recv_b, stats, tr=256):
    n, rows_a, cols = recv_a.shape
    na, nb = rows_a // tr, recv_b.shape[1] // tr
    nt = na + nb

    def body(a_ref, b_ref, st_ref, own_ref, sib_ref, stall_ref, slots, send_sems, recv_sem, *gather_sems):
        i = pl.program_id(0)
        mx, my, mc = _place()
        slot = i % 2
        g_start, g_forward, g_finish = _gather_ops(st_ref, stall_ref, *gather_sems)
        pl.when(i == 0)(g_start)
        pl.when(i == nt // 2)(g_forward)

        def push(k, tile):
            return pltpu.make_async_remote_copy(
                src_ref=slots.at[k], dst_ref=sib_ref.at[pl.ds(pl.multiple_of(tile * tr, tr), tr)],
                send_sem=send_sems.at[k], recv_sem=recv_sem, device_id=(mx, my, 1 - mc), device_id_type=MESH)

        pl.when(i >= 2)(lambda: push(slot, i - 2).wait_send())

        def total(r_ref):
            acc = r_ref[0].astype(F32)
            for k in range(1, n):
                acc = acc + r_ref[k].astype(F32)
            own_ref[...] = acc
            slots[slot] = acc

        pl.when(i < na)(lambda: total(a_ref))
        pl.when(i >= na)(lambda: total(b_ref))
        push(slot, i).start()

        @pl.when(i == nt - 1)
        def _():
            push(1 - slot, i - 1).wait_send()
            push(slot, i).wait_send()
            pltpu.make_async_remote_copy(src_ref=sib_ref, dst_ref=sib_ref, send_sem=send_sems.at[0], recv_sem=recv_sem,
                                         device_id=(mx, my, 1 - mc), device_id_type=MESH).wait_recv()
            g_finish()

    return pl.pallas_call(
        body, name="sum_and_swap", grid=(nt,),
        in_specs=[pl.BlockSpec((n, tr, cols), lambda i: (0, jnp.minimum(i, na - 1), 0)),
                  pl.BlockSpec((n, tr, cols), lambda i: (0, jnp.maximum(i - na, 0), 0)), _ANY],
        out_specs=[pl.BlockSpec((tr, cols), lambda i: (i, 0)), _ANY, _ANY],
        out_shape=[jax.ShapeDtypeStruct((nt * tr, cols), F32)] * 2
        + [jax.ShapeDtypeStruct((8,) + stats.shape, stats.dtype)],
        scratch_shapes=[pltpu.VMEM((2, tr, cols), F32), pltpu.SemaphoreType.DMA((2,)), pltpu.SemaphoreType.DMA(())]
        + _GATHER_SCRATCH,
        compiler_params=_cp("arbitrary"),
    )(recv_a, recv_b, stats)


def _adamw(w, g, m, v):
    m = ADAM_B1 * m + (1.0 - ADAM_B1) * g
    v = ADAM_B2 * v + (1.0 - ADAM_B2) * (g * g)
    m_hat = m / (1.0 - ADAM_B1 ** ADAM_STEP)
    v_hat = v / (1.0 - ADAM_B2 ** ADAM_STEP)
    delta = -ADAM_LR * (m_hat / (jnp.sqrt(v_hat) + ADAM_EPS) + ADAM_WD * w)
    return delta, m, v


def adam_slab(p_own, p_sib, w, m, v, row0, name, tr=512):
    L, R, C = w.shape
    nr = R // tr

    def body(a_ref, b_ref, w_ref, m_ref, v_ref, g_out, d_out, m_out, v_out):
        g = a_ref[...] + b_ref[...]
        d, m2, v2 = _adamw(w_ref[0], g, m_ref[0], v_ref[0])
        g_out[0], d_out[0], m_out[0], v_out[0] = g, d, m2, v2

    slab = pl.BlockSpec((tr, C), lambda l, i: (row0 // tr + l * nr + i, 0))
    blk = pl.BlockSpec((1, tr, C), lambda l, i: (l, i, 0))
    return pl.pallas_call(
        body, name=name, grid=(L, nr),
        in_specs=[slab, slab, blk, blk, blk], out_specs=[blk] * 4,
        out_shape=[jax.ShapeDtypeStruct(w.shape, F32)] * 4,
        compiler_params=_cp("arbitrary", "arbitrary"),
    )(p_own, p_sib, w, m, v)


def prologue(c8, w_ada, b_ada, norm_g, win_first):
    L, D, W = w_ada.shape

    def body(c_ref, w_ref, b_ref, g_ref, win_ref, vecs_ref, call_ref, wall_ref, mod_ref, mall_ref, *sems):
        w_start, w_forward, w_finish = _gather_ops(win_ref, wall_ref, *sems[0:3])
        for step in _gather_ops(c_ref, call_ref, *sems[3:6]):
            step()
        w_start()
        cv = call_ref[:, 0, :]
        ca = cv * _sigmoid(cv)
        for l in range(L):
            mod_ref[l * 8:(l + 1) * 8, :] = jnp.dot(ca, w_ref[l], precision=lax.Precision.HIGHEST,
                                                    preferred_element_type=F32)
        for step in _gather_ops(mod_ref, mall_ref, *sems[6:9]):
            step()
        mx, my, mc = _place()
        me = 4 * mx + 2 * my + mc
        rowid = lax.broadcasted_iota(jnp.int32, (L * 8, 1), 0)
        vecs_ref[...] = jnp.zeros_like(vecs_ref)
        for l in range(L):
            parts = [jnp.sum(jnp.where(rowid == l * 8 + me, mall_ref[2 * s + mc], 0.0), axis=0, keepdims=True)
                     for s in range(N_SHARD)]
            mod = jnp.concatenate(parts, axis=1) + b_ref[l:l + 1, :]
            for t in range(3):
                vecs_ref[l, t:t + 1, :] = mod[:, t * D:(t + 1) * D]
            vecs_ref[l, 3:4, :] = g_ref[l:l + 1, :]
        w_forward()
        w_finish()

    vmem = pl.BlockSpec(memory_space=pltpu.VMEM)
    return pl.pallas_call(
        body, name="prologue",
        in_specs=[vmem, vmem, vmem, vmem, _ANY], out_specs=[vmem, vmem, _ANY],
        out_shape=[jax.ShapeDtypeStruct((L, 8, D), F32), jax.ShapeDtypeStruct((8, 8, D), F32),
                   jax.ShapeDtypeStruct((8,) + win_first.shape, win_first.dtype)],
        scratch_shapes=[pltpu.VMEM((L * 8, W), F32), pltpu.VMEM((8, L * 8, W), F32)] + _GATHER_SCRATCH * 3,
        compiler_params=pltpu.CompilerParams(vmem_limit_bytes=VMEM_LIMIT_BYTES),
    )(c8, w_ada, b_ada, norm_g, win_first)


def ada_update(dmods, c_t, w, m, v, tr=512):
    L, D, W = w.shape

    def body(dm_ref, c_ref, w_ref, m_ref, v_ref, g_out, d_out, m_out, v_out):
        mx, my, _ = _place()
        shard = 2 * mx + my
        dm = jnp.zeros((8, W), F32)
        for s in range(N_SHARD):
            dm = dm + jnp.where(shard == s, dm_ref[0, :, s * W:(s + 1) * W], 0.0)
        cv = c_ref[...]
        ca = cv * _sigmoid(cv)
        g = jnp.zeros((tr, W), F32)
        for b in range(8):
            g = g + ca[:, b:b + 1] * dm[b:b + 1, :]
        d, m2, v2 = _adamw(w_ref[0], g, m_ref[0], v_ref[0])
        g_out[0], d_out[0], m_out[0], v_out[0] = g, d, m2, v2

    blk = pl.BlockSpec((1, tr, W), lambda l, i: (l, i, 0))
    return pl.pallas_call(
        body, name="ada_update", grid=(L, D // tr),
        in_specs=[pl.BlockSpec((1, 8, 3 * D), lambda l, i: (l, 0, 0)), pl.BlockSpec((tr, 8), lambda l, i: (i, 0)),
                  blk, blk, blk],
        out_specs=[blk] * 4, out_shape=[jax.ShapeDtypeStruct(w.shape, F32)] * 4,
        compiler_params=_cp("arbitrary", "arbitrary"),
    )(dmods, c_t, w, m, v)


STAT_ROWS = 16


def small_update(stats_all, norm, b_ada, final):
    def body(s_ref, *refs):
        ins, outs = refs[:9], refs[9:]
        tot = s_ref[0]
        for k in range(1, 8):
            tot = tot + s_ref[k]
        g_norm = tot[0:2, :]
        g_final = tot[2:3, :]
        g_b = jnp.concatenate(
            [jnp.concatenate([tot[3 + 3 * l + t:4 + 3 * l + t, :] for t in range(3)], axis=1) for l in range(DEPTH)],
            axis=0)
        for p, g in enumerate((g_norm, g_b, g_final)):
            w_ref, m_ref, v_ref = ins[3 * p:3 * p + 3]
            d, m2, v2 = _adamw(w_ref[...], g, m_ref[...], v_ref[...])
            for o_ref, val in zip(outs[4 * p:4 * p + 4], (g, d, m2, v2)):
                o_ref[...] = val
        loss = (0.5 / D_MODEL) * jnp.sum(tot[9:10, :], axis=1, keepdims=True)
        outs[12][...] = jnp.broadcast_to(loss, (8, LANES))

    shapes = []
    for w, _, _ in (norm, b_ada, final):
        shapes += [jax.ShapeDtypeStruct(w.shape, F32)] * 4
    shapes.append(jax.ShapeDtypeStruct((8, LANES), F32))
    return pl.pallas_call(body, name="small_update", out_shape=shapes)(stats_all, *norm, *b_ada, *final)


def kernel(x, c, norm_g, w_ada, b_ada, w_in, w_out, final_g, loss_target, m_norm_g, m_w_ada, m_b_ada, m_w_in, m_w_out, m_final_g, v_norm_g, v_w_ada, v_b_ada, v_w_in, v_w_out, v_final_g):
    S, D = x.shape[1], x.shape[2]
    mc = lax.axis_index("c")
    out_rows = D // N_SHARD

    def my_half(a, rows):
        return lax.dynamic_slice_in_dim(a, mc * rows, rows, axis=0)

    assert DEPTH == 2
    win = [my_half(w_in[l], D // 2).astype(BF16) for l in range(DEPTH)]
    wout = [my_half(w_out[l], out_rows // 2).astype(BF16) for l in range(DEPTH)]
    rest = [jnp.concatenate(wout, axis=0), win[1]]

    def unpack(gathered):
        outs, w3_second = gathered
        outs = outs.reshape(N_SHARD, 2, DEPTH, out_rows // 2, SHARD_W)
        return outs[:, :, 0].reshape(D, D), (w3_second.reshape(N_SHARD, D, SHARD_W), outs[:, :, 1].reshape(D, D))

    vecs, c_all, w3_first = prologue(jnp.broadcast_to(c, (8, D)), w_ada, b_ada, norm_g, win[0])
    c_all, w3_first = c_all[:, 0, :], w3_first.reshape(N_SHARD, D, SHARD_W)

    tabs = (*rope_tables(S), ret_log_gamma())
    saved = [None] * DEPTH
    h, saved[0], wall = layer_fwd(x[0], vecs[0], w3_first, lambda g: unpack(g)[0], tabs, rest)
    weights = [(w3_first, unpack(wall)[0]), unpack(wall)[1]]
    head = (jnp.broadcast_to(final_g[None, :], (8, D)), loss_target[0])
    (dx, st_loss), saved[1], _ = layer_fwd(h, vecs[1], *weights[1], tabs, head=head)

    dmod, dnorm, grads = [None] * DEPTH, [None] * DEPTH, None
    for l in reversed(range(DEPTH)):
        dx, dmod[l], dnorm[l], grads = layer_bwd(dx, saved[l], vecs[l], *weights[l], tabs, grads)

    stats = jnp.concatenate(dnorm + [st_loss[0:1]] + dmod + [st_loss[1:2], jnp.zeros((STAT_ROWS - 10, D), F32)], axis=0)
    p_own, p_sib, stats_all = sum_and_swap(*grads, stats)
    res_in = adam_slab(p_own, p_sib, w_in, m_w_in, v_w_in, 0, "adam_w_in")
    res_out = adam_slab(p_own, p_sib, w_out, m_w_out, v_w_out, DEPTH * D, "adam_w_out", tr=256)

    dmods = stats_all[:, 3:9, :].reshape(8, DEPTH, 3 * D).transpose(1, 0, 2)
    res_ada = ada_update(dmods, c_all.T, w_ada, m_w_ada, v_w_ada)
    small = small_update(stats_all, (norm_g, m_norm_g, v_norm_g), (b_ada, m_b_ada, v_b_ada),
                         (final_g[None, :], m_final_g[None, :], v_final_g[None, :]))
    res_norm, res_b, res_final = small[0:4], small[4:8], [a[0] for a in small[8:12]]
    loss = small[12][0, 0]

    by_kind = [res_norm, res_ada, res_b, res_in, res_out, res_final]
    outs = [loss, dx[None]]
    for kind in range(4):
        outs += [r[kind] for r in by_kind]
    return tuple(outs)
```

```python
import numpy as np
import jax
import jax.numpy as jnp
from jax import lax
from jax.experimental import pallas as pl
from jax.experimental.pallas import tpu as pltpu

F32, BF16 = jnp.float32, jnp.bfloat16
MESH = pl.DeviceIdType.MESH

D_MODEL = 1024
DEPTH = 2
SHARD_W = 1024
N_SHARD = 4
GROUP_W = 512
LANES = 128
SB_HEAD_DIM = 64
RET_HEAD_DIM = 128
CHUNK = 64
ROPE_BASE = 10000.0
EPS = 1e-6
SQ_SCALE = SB_HEAD_DIM ** -0.5
RK_SCALE = RET_HEAD_DIM ** -0.5
SB_T = 1024
SB_CHAINS = 16
SB_NB = 4
RET_T = 256
EXP_ZERO = -104.0
VMEM_LIMIT_BYTES = 56 * 2 ** 20

ADAM_LR, ADAM_B1, ADAM_B2, ADAM_EPS, ADAM_WD, ADAM_STEP = 0.001, 0.9, 0.999, 1e-08, 0.01, 10


def _cp(*sem):
    return pltpu.CompilerParams(dimension_semantics=sem, vmem_limit_bytes=VMEM_LIMIT_BYTES)


def _dot(a, b):
    return lax.dot_general(a, b, (((1,), (0,)), ((), ())), preferred_element_type=F32)


def _dot_nt(a, b):
    return lax.dot_general(a, b, (((1,), (1,)), ((), ())), preferred_element_type=F32)


def _dot_tn(a, b):
    return lax.dot_general(a, b, (((0,), (0,)), ((), ())), preferred_element_type=F32)


def _running_sum(a, tri):
    return _dot(a.astype(BF16), tri)


def _sigmoid(x):
    return 1.0 / (1.0 + jnp.exp(-x))


def _rowsum(a):
    return jnp.sum(a, axis=1, keepdims=True)


def _rowmean(a):
    return jnp.mean(a, axis=1, keepdims=True)


def inproj_fwd(x, vecs, w3, tm=512):
    S, D = x.shape

    def body(x_ref, v_ref, w_ref, ret_ref, sg_ref, h_ref, sb_ref):
        xv = x_ref[...]
        r = lax.rsqrt(_rowmean(xv * xv) + EPS)
        h = xv * r * v_ref[3:4, :] * (1.0 + v_ref[1:2, :]) + v_ref[0:1, :]
        hb = h.astype(BF16)
        h_ref[...] = hb
        for s in range(N_SHARD):
            p = _dot(hb, w_ref[s])
            if s < 2:
                ret_ref[:, s * SHARD_W:(s + 1) * SHARD_W] = p
            if s == 2:
                sb_ref[:, 0:GROUP_W] = (p[:, 0:GROUP_W] * SQ_SCALE).astype(BF16)
                sb_ref[:, GROUP_W:SHARD_W] = p[:, GROUP_W:].astype(BF16)
            if s == 3:
                sb_ref[:, SHARD_W:SHARD_W + GROUP_W] = p[:, 0:GROUP_W].astype(BF16)
                sg_ref[...] = p[:, GROUP_W:]

    row = lambda w: pl.BlockSpec((tm, w), lambda i: (i, 0))
    return pl.pallas_call(
        body, name="inproj_fwd", grid=(S // tm,),
        in_specs=[row(D), pl.BlockSpec((8, D), lambda i: (0, 0)),
                  pl.BlockSpec((N_SHARD, D, SHARD_W), lambda i: (0, 0, 0))],
        out_specs=[row(2 * SHARD_W), row(GROUP_W), row(D), row(3 * GROUP_W)],
        out_shape=[jax.ShapeDtypeStruct((S, 2 * SHARD_W), F32), jax.ShapeDtypeStruct((S, GROUP_W), F32),
                   jax.ShapeDtypeStruct((S, D), BF16), jax.ShapeDtypeStruct((S, 3 * GROUP_W), BF16)],
        compiler_params=_cp("arbitrary"),
    )(x, vecs, w3)


def _sb_logits(qh, k2, keep):
    z = _dot_nt(qh, k2)
    sp = jnp.log(1.0 + jnp.exp(-jnp.abs(z)))
    lb = jnp.minimum(z, 0.0) - sp
    lk = lb - z
    if keep is not None:
        lk = jnp.where(keep, lk, 0.0)
    return lb, lk


class _sb_chains:
    def __init__(self, i, q2, do_b=None):
        t = self.t = SB_T // SB_CHAINS
        self.C = range(SB_CHAINS)
        r = lax.broadcasted_iota(jnp.int32, (SB_NB * t, SB_NB * t), 0)
        c = lax.broadcasted_iota(jnp.int32, (SB_NB * t, SB_NB * t), 1)
        self.later_all = jnp.where(r > c, 1.0, 0.0).astype(BF16)
        self.earlier_all = jnp.where(r < c, 1.0, 0.0).astype(BF16)
        self.later, self.earlier = self.later_all[:t, :t], self.earlier_all[:t, :t]
        self.head0 = lax.broadcasted_iota(jnp.int32, (1, LANES), 1) < SB_HEAD_DIM
        row = lax.broadcasted_iota(jnp.int32, (2 * t, SB_NB * t), 0) & (t - 1)
        col = lax.broadcasted_iota(jnp.int32, (2 * t, SB_NB * t), 1)
        qt = [SB_CHAINS * i + cc for cc in self.C]
        self.first = [jnp.maximum(qt[cc] - (SB_NB - 1), 0) for cc in self.C]
        self._ahead, self._qt = col - row, qt
        self.qs = [self._stack(q2[cc * t:(cc + 1) * t]) for cc in self.C]
        if do_b is not None:
            self.dos = [self._stack(do_b[cc * t:(cc + 1) * t]) for cc in self.C]

    def keep_of(self, cc):
        return self._ahead < (self._qt[cc] - self.first[cc]) * self.t

    def _stack(self, a):
        zero = jnp.zeros_like(a)
        return jnp.concatenate([jnp.where(self.head0, a, zero), jnp.where(self.head0, zero, a)], axis=0)

    def rows(self, ref, j, n):
        return ref[pl.ds(pl.multiple_of(j * self.t, self.t), n * self.t), :]

    def suffix(self, lk):
        return _running_sum(lk, self.later_all), _rowsum(lk)

    def prefix(self, g, G0):
        return _running_sum(g, self.earlier_all) + G0


def sb_fwd(sb, sg, gather=None):
    S = sb.shape[0]
    T = SB_T
    nq = S // T
    carried = list(gather or ())
    ng = len(carried)

    def body(*refs):
        (q_ref, k_ref, v_ref, sg_ref), refs = refs[:4], refs[4:]
        x_refs, (y_ref, o_ref, end_ref), out_refs, sems = refs[:ng], refs[ng:ng + 3], refs[ng + 3:2 * ng + 3], refs[2 * ng + 3:]
        p, i = pl.program_id(0), pl.program_id(1)
        gathers = [_gather_ops(x_refs[g], out_refs[g], *sems[3 * g:3 * g + 3]) for g in range(ng)]
        for start, forward, _ in gathers:
            pl.when(jnp.logical_and(p == 0, i == 0))(start)
            pl.when(jnp.logical_and(p == 3, i == 0))(forward)
        ch = _sb_chains(i, q_ref[...])
        later, head0 = ch.later, ch.head0
        lbk = [_sb_logits(ch.qs[c], ch.rows(k_ref, ch.first[c], SB_NB), ch.keep_of(c)) for c in ch.C]
        suffix, R = zip(*[ch.suffix(lbk[c][1]) for c in ch.C])
        aa = [jnp.where(ch.keep_of(c), jnp.exp(lbk[c][0] + suffix[c]), 0.0) for c in ch.C]
        acc = [_dot(aa[c].astype(BF16), ch.rows(v_ref, ch.first[c], SB_NB)) for c in ch.C]

        nc = len(ch.C)

        def alive(n, Rs):
            m = None
            for c in ch.C:
                rc = jnp.where(ch.first[c] - n > 0, Rs[c], EXP_ZERO)
                m = rc if m is None else jnp.maximum(m, rc)
            return jnp.max(m)

        def cond(st):
            return st[-1] > EXP_ZERO

        def step(st):
            n, accs, Rs = st[0], list(st[1:1 + nc]), list(st[1 + nc:1 + 2 * nc])
            for c in ch.C:
                j = ch.first[c] - 1 - n
                jc = jnp.maximum(j, 0)
                lb, lk = _sb_logits(ch.qs[c], ch.rows(k_ref, jc, 1), None)
                a = jnp.exp(lb + _running_sum(lk, later) + Rs[c])
                cx = _dot(a.astype(BF16), ch.rows(v_ref, jc, 1))
                accs[c] = jnp.where(j >= 0, accs[c] + cx, accs[c])
                Rs[c] = jnp.where(j >= 0, Rs[c] + _rowsum(lk), Rs[c])
            return (n + 1, *accs, *Rs, alive(n + 1, Rs))

        st = lax.while_loop(cond, step, (jnp.int32(0), *acc, *R, alive(0, R)))
        n_end, acc, R = st[0], st[1:1 + nc], st[1 + nc:1 + 2 * nc]
        outs = []
        for c in ch.C:
            base = c * (2 * ch.t + 8)
            end_ref[0, 0, base:base + 2 * ch.t, :] = jnp.broadcast_to(R[c], (2 * ch.t, 8))
            end_ref[0, 0, base + 2 * ch.t:base + 2 * ch.t + 8, :] = jnp.full((8, 8), n_end.astype(F32))
            outs.append(jnp.where(head0, acc[c][:ch.t], acc[c][ch.t:]))
        o = jnp.concatenate(outs, axis=0)
        o_ref[...] = o
        sg = sg_ref[...]
        y_ref[...] = (o * (sg * _sigmoid(sg))).astype(BF16)
        for _, _, finish in gathers:
            pl.when(jnp.logical_and(p == 3, i == nq - 1))(finish)

    return pl.pallas_call(
        body, name="sb_fwd", grid=(4, nq),
        in_specs=[pl.BlockSpec((T, LANES), lambda p, i: (i, p)),
                  pl.BlockSpec((S, LANES), lambda p, i: (0, 4 + p)),
                  pl.BlockSpec((S, LANES), lambda p, i: (0, 8 + p)),
                  pl.BlockSpec((T, LANES), lambda p, i: (i, p))] + [_ANY for _ in carried],
        out_specs=[pl.BlockSpec((T, LANES), lambda p, i: (i, p)),
                   pl.BlockSpec((T, LANES), lambda p, i: (i, p)),
                   pl.BlockSpec((1, 1, SB_CHAINS * (2 * T // SB_CHAINS + 8), 8), lambda p, i: (p, i, 0, 0))] + [_ANY for _ in carried],
        out_shape=[jax.ShapeDtypeStruct((S, GROUP_W), BF16),
                   jax.ShapeDtypeStruct((S, GROUP_W), F32),
                   jax.ShapeDtypeStruct((4, nq, SB_CHAINS * (2 * T // SB_CHAINS + 8), 8), F32)]
        + [jax.ShapeDtypeStruct((8,) + a.shape, a.dtype) for a in carried],
        scratch_shapes=_GATHER_SCRATCH * ng,
        compiler_params=_cp("arbitrary", "arbitrary"),
    )(sb, sb, sb, sg, *carried)


def sb_bwd(sb, sg, o, sb_end, dycat, ship=None):
    S = sb.shape[0]
    T = SB_T
    nq = S // T
    ex = _Exchange(ship)

    def body(*refs):
        (q_ref, k_ref, v_ref, sg_ref, o_ref, dy_ref, end_ref), refs = refs[:7], refs[7:]
        ship_refs, (dq_ref, dk_ref, dv_ref, dsg_ref), refs = refs[:ex.n_in], refs[ex.n_in:ex.n_in + 4], refs[ex.n_in + 4:]
        recv, (dk_acc, dv_acc), sems = refs[:ex.n_out], refs[ex.n_out:ex.n_out + 2], refs[ex.n_out + 2:]
        start, finish = ex.ops(ship_refs, recv + sems)
        p, i = pl.program_id(0), pl.program_id(1)
        pl.when(jnp.logical_and(p == 0, i == 0))(start)

        @pl.when(i == 0)
        def _():
            dk_acc[...] = jnp.zeros_like(dk_acc)
            dv_acc[...] = jnp.zeros_like(dv_acc)

        sg = sg_ref[...]
        sig = _sigmoid(sg)
        dy = dy_ref[...]
        dsg_ref[...] = (dy * o_ref[...] * (sig * (1.0 + sg * (1.0 - sig)))).astype(BF16)
        do_b = (dy * (sg * sig)).astype(BF16)
        ch = _sb_chains(i, q_ref[...], do_b)
        later, earlier, head0, t = ch.later, ch.earlier, ch.head0, ch.t
        end = end_ref[0, 0]

        def grads(c, j, n, a, lb, g, G, keep):
            dz = g - jnp.exp(lb) * (g + G)
            if keep is not None:
                dz = jnp.where(keep, dz, 0.0)
            dzb = dz.astype(BF16)
            rows = pl.ds(pl.multiple_of(j * t, t), n * t)
            dk_acc[rows, :] += _dot_tn(dzb, ch.qs[c])
            dv_acc[rows, :] += _dot_tn(a.astype(BF16), ch.dos[c])
            return _dot(dzb, ch.rows(k_ref, j, n))

        nc = len(ch.C)
        n_end = jnp.max(end[2 * t:2 * t + 8, :]).astype(jnp.int32)

        def sweep(m, st):
            dqs, G0s, lefts = list(st[:nc]), list(st[nc:2 * nc]), list(st[2 * nc:])
            for c in ch.C:
                j = ch.first[c] - n_end + m
                jc = jnp.maximum(j, 0)
                lb, lk = _sb_logits(ch.qs[c], ch.rows(k_ref, jc, 1), None)
                stick = lefts[c] - _rowsum(lk)
                a = jnp.where(j >= 0, jnp.exp(lb + _running_sum(lk, later) + stick), 0.0)
                g = a * _dot_nt(ch.dos[c], ch.rows(v_ref, jc, 1))
                G = _running_sum(g, earlier) + G0s[c]
                dqs[c] = dqs[c] + grads(c, jc, 1, a, lb, jnp.where(j >= 0, g, 0.0), jnp.where(j >= 0, G, 0.0), None)
                G0s[c] = G0s[c] + _rowsum(g)
                lefts[c] = jnp.where(j >= 0, stick, lefts[c])
            return (*dqs, *G0s, *lefts)

        lefts = [end[c * (2 * t + 8):c * (2 * t + 8) + 2 * t, 0:1] for c in ch.C]
        st = lax.fori_loop(0, n_end, sweep, (*[jnp.zeros((2 * t, LANES), F32)] * nc,
                                             *[jnp.zeros((2 * t, 1), F32)] * nc, *lefts))
        dq, G0 = st[:nc], st[nc:2 * nc]

        lbk = [_sb_logits(ch.qs[c], ch.rows(k_ref, ch.first[c], SB_NB), ch.keep_of(c)) for c in ch.C]
        suffix = [ch.suffix(lbk[c][1])[0] for c in ch.C]
        aa = [jnp.where(ch.keep_of(c), jnp.exp(lbk[c][0] + suffix[c]), 0.0) for c in ch.C]
        g = [aa[c] * _dot_nt(ch.dos[c], ch.rows(v_ref, ch.first[c], SB_NB)) for c in ch.C]
        G = [ch.prefix(g[c], G0[c]) for c in ch.C]
        for c in ch.C:
            dqc = dq[c] + grads(c, ch.first[c], SB_NB, aa[c], lbk[c][0], g[c], G[c], ch.keep_of(c))
            dq_ref[c * t:(c + 1) * t, :] = (jnp.where(head0, dqc[:t], dqc[t:]) * SQ_SCALE).astype(BF16)

        @pl.when(i == nq - 1)
        def _():
            dk_ref[...] = dk_acc[...].astype(BF16)
            dv_ref[...] = dv_acc[...].astype(BF16)

        pl.when(jnp.logical_and(p == 3, i == nq - 1))(finish)

    tile_spec = lambda c0: pl.BlockSpec((T, LANES), lambda p, i: (i, c0 + p))
    head_spec = lambda c0: pl.BlockSpec((S, LANES), lambda p, i: (0, c0 + p))
    return pl.pallas_call(
        body, name="sb_bwd", grid=(4, nq),
        in_specs=[tile_spec(0), head_spec(4), head_spec(8), tile_spec(0), tile_spec(0), tile_spec(4),
                  pl.BlockSpec((1, 1, SB_CHAINS * (2 * T // SB_CHAINS + 8), 8), lambda p, i: (p, i, 0, 0))] + ex.in_specs,
        out_specs=[tile_spec(0), head_spec(0), head_spec(0), tile_spec(0)] + ex.out_specs,
        out_shape=[jax.ShapeDtypeStruct((S, GROUP_W), BF16)] * 4 + ex.out_shape,
        scratch_shapes=[pltpu.VMEM((S, LANES), F32), pltpu.VMEM((S, LANES), F32)] + ex.scratch,
        compiler_params=_cp("arbitrary", "arbitrary"),
    )(sb, sb, sb, sg, o, dycat, sb_end, *ex.ship)


def rope_tables(S):
    half = RET_HEAD_DIM // 2
    lane = jnp.arange(RET_HEAD_DIM)
    inv = ROPE_BASE ** (-(lane % half).astype(F32) / half)
    ang = jnp.arange(S, dtype=F32)[:, None] * inv[None, :]
    return jnp.cos(ang), jnp.where(lane < half, -1.0, 1.0)[None, :] * jnp.sin(ang)


def ret_log_gamma():
    return jnp.log1p(-(2.0 ** (-5.0 - jnp.arange(4, dtype=F32))))


def _swap_halves(a):
    return pltpu.roll(a, RET_HEAD_DIM // 2, axis=1)


def _ret_decay_mask(lg):
    n = lax.broadcasted_iota(jnp.int32, (RET_T, RET_T), 0)
    m = lax.broadcasted_iota(jnp.int32, (RET_T, RET_T), 1)
    dist = jnp.abs(n - m).astype(F32)
    return jnp.where((m // CHUNK) <= (n // CHUNK), jnp.exp(lg * dist), 0.0)


def _ret_block(lg, rq, rk, rv, cosf, sinf, dm):
    q = rq * cosf + _swap_halves(rq) * sinf
    k = (rk * cosf + _swap_halves(rk) * sinf) * RK_SCALE
    qb, kb, vb = q.astype(BF16), k.astype(BF16), rv.astype(BF16)
    sc = _dot_nt(qb, kb) * dm
    nloc = lax.broadcasted_iota(jnp.int32, (RET_T, 1), 0).astype(F32)
    qdec = jnp.exp(lg * (nloc + 1.0))
    kdec = jnp.exp(lg * (RET_T - 1.0 - nloc))
    block_dec = jnp.exp(jnp.full((1, LANES), lg * RET_T, F32))
    return q, k, qb, kb, vb, sc, qdec, kdec, block_dec


RET_RB = 4


def _ret_specs(S, rb):
    group = lambda c0: pl.BlockSpec((RET_RB * RET_T, GROUP_W), lambda s: (rb(s), c0))
    return group, pl.BlockSpec((RET_RB * RET_T, LANES), lambda s: (rb(s), 0))


def _ret_chains():
    chains = [(h, b) for b in range(RET_RB) for h in range(4)]
    rows = lambda c: (slice(c[1] * RET_T, (c[1] + 1) * RET_T), slice(c[0] * LANES, (c[0] + 1) * LANES))
    tab = lambda ref, c: ref[c[1] * RET_T:(c[1] + 1) * RET_T, :]
    return chains, rows, tab


def _ret_blocks(chains, rows, lg_ref, rq_ref, rk_ref, rv_ref, cosf, sinf, dm_ref):
    blk = {c: _ret_block(lg_ref[c[0]], rq_ref[rows(c)], rk_ref[rows(c)], rv_ref[rows(c)],
                         cosf[c], sinf[c], dm_ref[c[0]]) for c in chains}
    return ({c: blk[c][n] for c in chains} for n in range(9))


def ret_fwd(proj, cosf, sinf, lgam):
    S = proj.shape[0]
    nb = S // RET_T
    group, row_tab = _ret_specs(S, lambda s: s)

    def body(lg_ref, rq_ref, rk_ref, rv_ref, rg_ref, cos_ref, sin_ref, y_ref, o_ref, st_out, st_ref, dm_ref):
        @pl.when(pl.program_id(0) == 0)
        def _():
            st_ref[...] = jnp.zeros_like(st_ref)
            for h in range(4):
                dm_ref[h] = _ret_decay_mask(lg_ref[h])

        chains, rows, tab = _ret_chains()
        cosf, sinf = {c: tab(cos_ref, c) for c in chains}, {c: tab(sin_ref, c) for c in chains}
        q, k, qb, kb, vb, sc, qdec, kdec, block_dec = _ret_blocks(
            chains, rows, lg_ref, rq_ref, rk_ref, rv_ref, cosf, sinf, dm_ref)
        kv = {c: _dot_tn((k[c] * kdec[c]).astype(BF16), vb[c]) for c in chains}
        st = {(h, 0): st_ref[h] for h in range(4)}
        for b in range(RET_RB):
            for h in range(4):
                st[(h, b + 1)] = st[(h, b)] * block_dec[(h, b)] + kv[(h, b)]
        for h, b in chains:
            st_out[h, b] = st[(h, b)]
        for h in range(4):
            st_ref[h] = st[(h, RET_RB)]
        o = {c: _dot(sc[c].astype(BF16), vb[c]) + _dot(qb[c], st[c].astype(BF16)) * qdec[c] for c in chains}
        for c in chains:
            o_ref[rows(c)] = o[c]
        cen = {c: o[c] - _rowmean(o[c]) for c in chains}
        on = {c: cen[c] * lax.rsqrt(_rowmean(cen[c] * cen[c]) + EPS) for c in chains}
        rg = {c: rg_ref[rows(c)] for c in chains}
        for c in chains:
            y_ref[rows(c)] = (on[c] * (rg[c] * _sigmoid(rg[c]))).astype(BF16)

    return pl.pallas_call(
        body, name="ret_fwd", grid=(nb // RET_RB,),
        in_specs=[pl.BlockSpec(memory_space=pltpu.SMEM),
                  group(0), group(1), group(2), group(3), row_tab, row_tab],
        out_specs=[group(0), group(0),
                   pl.BlockSpec((4, RET_RB, LANES, LANES), lambda s: (0, s, 0, 0))],
        out_shape=[jax.ShapeDtypeStruct((S, GROUP_W), BF16),
                   jax.ShapeDtypeStruct((S, GROUP_W), F32),
                   jax.ShapeDtypeStruct((4, nb, LANES, LANES), F32)],
        scratch_shapes=[pltpu.VMEM((4, LANES, LANES), F32), pltpu.VMEM((4, RET_T, RET_T), F32)],
        compiler_params=_cp("arbitrary"),
    )(lgam, proj, proj, proj, proj, cosf, sinf)


def ret_bwd(proj, cosf, sinf, lgam, o, states, dycat):
    S = proj.shape[0]
    nsteps = S // RET_T // RET_RB
    rev = lambda s: nsteps - 1 - s
    group, row_tab = _ret_specs(S, rev)

    def body(lg_ref, rq_ref, rk_ref, rv_ref, rg_ref, cos_ref, sin_ref, o_ref, st_in, dy_ref,
             drq_ref, drk_ref, drv_ref, drg_ref, ds_ref, dm_ref):
        @pl.when(pl.program_id(0) == 0)
        def _():
            ds_ref[...] = jnp.zeros_like(ds_ref)
            for h in range(4):
                dm_ref[h] = _ret_decay_mask(lg_ref[h])

        chains, rows, tab = _ret_chains()
        cosf, sinf = {c: tab(cos_ref, c) for c in chains}, {c: tab(sin_ref, c) for c in chains}
        dms = {c: dm_ref[c[0]] for c in chains}
        q, k, qb, kb, vb, sc, qdec, kdec, block_dec = _ret_blocks(
            chains, rows, lg_ref, rq_ref, rk_ref, rv_ref, cosf, sinf, dm_ref)
        o_v = {c: o_ref[rows(c)] for c in chains}
        cen = {c: o_v[c] - _rowmean(o_v[c]) for c in chains}
        rstd = {c: lax.rsqrt(_rowmean(cen[c] * cen[c]) + EPS) for c in chains}
        on = {c: cen[c] * rstd[c] for c in chains}
        rg = {c: rg_ref[rows(c)] for c in chains}
        sig = {c: _sigmoid(rg[c]) for c in chains}
        dy = {c: dy_ref[rows(c)] for c in chains}
        for c in chains:
            drg_ref[rows(c)] = (dy[c] * on[c] * (sig[c] * (1.0 + rg[c] * (1.0 - sig[c])))).astype(BF16)
        don = {c: dy[c] * (rg[c] * sig[c]) for c in chains}
        do = {c: rstd[c] * (don[c] - _rowmean(don[c]) - on[c] * _rowmean(don[c] * on[c])) for c in chains}
        dob = {c: do[c].astype(BF16) for c in chains}
        dsc = {c: (_dot_nt(dob[c], vb[c]) * dms[c]).astype(BF16) for c in chains}
        st_b = {c: st_in[c[0], c[1]].astype(BF16) for c in chains}
        dst = {c: _dot_tn((q[c] * qdec[c]).astype(BF16), dob[c]) for c in chains}
        dsn = {(h, RET_RB): ds_ref[h] for h in range(4)}
        for b in reversed(range(RET_RB)):
            for h in range(4):
                dsn[(h, b)] = dsn[(h, b + 1)] * block_dec[(h, b)] + dst[(h, b)]
        for h in range(4):
            ds_ref[h] = dsn[(h, 0)]
        dsn_b = {c: dsn[(c[0], c[1] + 1)].astype(BF16) for c in chains}
        dq = {c: _dot(dsc[c], kb[c]) + _dot_nt(dob[c], st_b[c]) * qdec[c] for c in chains}
        dk = {c: (_dot_tn(dsc[c], qb[c]) + _dot_nt(vb[c], dsn_b[c]) * kdec[c]) * RK_SCALE for c in chains}
        dv = {c: _dot_tn(sc[c].astype(BF16), dob[c]) + _dot((k[c] * kdec[c]).astype(BF16), dsn_b[c])
              for c in chains}
        for c in chains:
            drq_ref[rows(c)] = (dq[c] * cosf[c] + _swap_halves(dq[c] * sinf[c])).astype(BF16)
            drk_ref[rows(c)] = (dk[c] * cosf[c] + _swap_halves(dk[c] * sinf[c])).astype(BF16)
            drv_ref[rows(c)] = dv[c].astype(BF16)

    return pl.pallas_call(
        body, name="ret_bwd", grid=(nsteps,),
        in_specs=[pl.BlockSpec(memory_space=pltpu.SMEM),
                  group(0), group(1), group(2), group(3), row_tab, row_tab,
                  group(0), pl.BlockSpec((4, RET_RB, LANES, LANES), lambda s: (0, rev(s), 0, 0)),
                  group(0)],
        out_specs=[group(0)] * 4,
        out_shape=[jax.ShapeDtypeStruct((S, GROUP_W), BF16)] * 4,
        scratch_shapes=[pltpu.VMEM((4, LANES, LANES), F32), pltpu.VMEM((4, RET_T, RET_T), F32)],
        compiler_params=_cp("arbitrary"),
    )(lgam, proj, proj, proj, proj, cosf, sinf, o, states, dycat)


def outproj_fwd(x, vecs, y_ret, y_sb, w_out, head=None, tm=1024):
    S, D = x.shape
    tm = min(tm, S)
    last = list(head or ())

    def body(x_ref, v_ref, yr_ref, ys_ref, w_ref, *refs):
        y = _dot(yr_ref[...], w_ref[0:GROUP_W, :]) + _dot(ys_ref[...], w_ref[GROUP_W:, :])
        xv = x_ref[...] + v_ref[2:3, :] * y
        if not last:
            y_ref, xo_ref = refs
            y_ref[...] = y.astype(BF16)
            xo_ref[...] = xv
            return
        g_ref, t_ref, y_ref, dx_ref, st_ref = refs
        y_ref[...] = y.astype(BF16)

        @pl.when(pl.program_id(0) == 0)
        def _():
            st_ref[...] = jnp.zeros_like(st_ref)

        g = g_ref[0:1, :]
        for hh in range(2):
            rows = slice(hh * (tm // 2), (hh + 1) * (tm // 2))
            xh = xv[rows, :]
            r = lax.rsqrt(_rowmean(xh * xh) + EPS)
            xn = xh * r
            err = xn * g - t_ref[rows, :]
            dy = err * (1.0 / D)
            dxn = dy * g
            dx_ref[rows, :] = r * (dxn - xn * _rowmean(dxn * xn))
            st_ref[0:1, :] += jnp.sum(dy * xn, axis=0, keepdims=True)
            st_ref[1:2, :] += jnp.sum(err * err, axis=0, keepdims=True)

    row = lambda w: pl.BlockSpec((tm, w), lambda i: (i, 0))
    fixed = pl.BlockSpec((8, D), lambda i: (0, 0))
    return pl.pallas_call(
        body, name="outproj_fwd", grid=(S // tm,),
        in_specs=[row(D), fixed, row(GROUP_W), row(GROUP_W), pl.BlockSpec((D, D), lambda i: (0, 0))]
        + ([fixed, row(D)] if last else []),
        out_specs=[row(D), row(D)] + ([fixed] if last else []),
        out_shape=[jax.ShapeDtypeStruct((S, D), BF16), jax.ShapeDtypeStruct((S, D), F32)]
        + ([jax.ShapeDtypeStruct((8, D), F32)] if last else []),
        compiler_params=_cp("arbitrary"),
    )(x, vecs, y_ret, y_sb, w_out, *last)


def outproj_bwd(dx, y, vecs, y_ret, y_sb, w_out, tm=1024):
    S, D = dx.shape
    tm = min(tm, S)
    n = S // tm

    def body(dx_ref, y_ref, v_ref, yr_ref, ys_ref, w_ref, dyc_ref, dw_ref, st_ref, acc):
        i = pl.program_id(0)

        @pl.when(i == 0)
        def _():
            st_ref[...] = jnp.zeros_like(st_ref)
            acc[...] = jnp.zeros_like(acc)

        dxv = dx_ref[...]
        st_ref[0:1, :] += jnp.sum(dxv * y_ref[...].astype(F32), axis=0, keepdims=True)
        dyy = (dxv * v_ref[2:3, :]).astype(BF16)
        dyc_ref[...] = _dot_nt(dyy, w_ref[...])
        acc[0:GROUP_W, :] += _dot_tn(yr_ref[...], dyy)
        acc[GROUP_W:, :] += _dot_tn(ys_ref[...], dyy)

        @pl.when(i == n - 1)
        def _():
            dw_ref[...] = acc[...].astype(BF16)

    row = lambda w: pl.BlockSpec((tm, w), lambda i: (i, 0))
    fixed = lambda r: pl.BlockSpec((r, D), lambda i: (0, 0))
    return pl.pallas_call(
        body, name="outproj_bwd", grid=(n,),
        in_specs=[row(D), row(D), fixed(8), row(GROUP_W), row(GROUP_W), fixed(D)],
        out_specs=[row(D), fixed(D), fixed(8)],
        out_shape=[jax.ShapeDtypeStruct((S, D), F32), jax.ShapeDtypeStruct((D, D), BF16),
                   jax.ShapeDtypeStruct((8, D), F32)],
        scratch_shapes=[pltpu.VMEM((D, D), F32)],
        compiler_params=_cp("arbitrary"),
    )(dx, y, vecs, y_ret, y_sb, w_out)


def inproj_bwd_x(pieces, w3, x, vecs, dx_res, ship=None, tm=512):
    S, D = x.shape
    n = S // tm
    ex = _Exchange(ship)

    def body(*refs):
        p_refs, (w_ref, x_ref, v_ref, dr_ref), refs = refs[:8], refs[8:12], refs[12:]
        ship_refs, (dx_ref, st_ref), refs = refs[:ex.n_in], refs[ex.n_in:ex.n_in + 2], refs[ex.n_in + 2:]
        start, finish = ex.ops(ship_refs, refs)

        @pl.when(pl.program_id(0) == 0)
        def _():
            st_ref[...] = jnp.zeros_like(st_ref)
            start()

        g, scale1 = v_ref[3:4, :], 1.0 + v_ref[1:2, :]
        halves = [slice(hh * (tm // 2), (hh + 1) * (tm // 2)) for hh in range(2)]
        dhs = []
        for rows in halves:
            dh = jnp.zeros((tm // 2, D), F32)
            for k, p_ref in enumerate(p_refs):
                c0 = (k % 2) * GROUP_W
                dh = dh + _dot_nt(p_ref[rows, :], w_ref[k // 2, :, c0:c0 + GROUP_W])
            dhs.append(dh)
        for rows, dh in zip(halves, dhs):
            xv = x_ref[rows, :]
            r = lax.rsqrt(_rowmean(xv * xv) + EPS)
            xn = xv * r
            st_ref[0:1, :] += jnp.sum(dh, axis=0, keepdims=True)
            dh_xn = dh * xn
            st_ref[1:2, :] += jnp.sum(dh_xn, axis=0, keepdims=True) * g
            st_ref[2:3, :] += jnp.sum(dh_xn, axis=0, keepdims=True) * scale1
            dxn = dh * (g * scale1)
            dx_ref[rows, :] = r * (dxn - xn * _rowmean(dxn * xn)) + dr_ref[rows, :]
        pl.when(pl.program_id(0) == n - 1)(finish)

    row = lambda w: pl.BlockSpec((tm, w), lambda i: (i, 0))
    return pl.pallas_call(
        body, name="inproj_bwd_x", grid=(n,),
        in_specs=[row(GROUP_W)] * 8 + [pl.BlockSpec((N_SHARD, D, SHARD_W), lambda i: (0, 0, 0)),
                                       row(D), pl.BlockSpec((8, D), lambda i: (0, 0)), row(D)] + ex.in_specs,
        out_specs=[row(D), pl.BlockSpec((8, D), lambda i: (0, 0))] + ex.out_specs,
        out_shape=[jax.ShapeDtypeStruct((S, D), F32), jax.ShapeDtypeStruct((8, D), F32)] + ex.out_shape,
        scratch_shapes=ex.scratch,
        compiler_params=_cp("arbitrary"),
    )(*pieces, w3, x, vecs, dx_res, *ex.ship)


def inproj_bwd_w(h, pieces, tm=1024):
    S, D = h.shape
    tm = min(tm, S)
    n = S // tm

    def body(*refs):
        h_ref, p_refs, dw_ref, acc = refs[0], refs[1:9], refs[9], refs[10]
        i = pl.program_id(0)

        @pl.when(i == 0)
        def _():
            acc[...] = jnp.zeros_like(acc)

        hv = h_ref[...]
        for k, p_ref in enumerate(p_refs):
            c0 = (k % 2) * GROUP_W
            acc[k // 2, :, c0:c0 + GROUP_W] += _dot_tn(hv, p_ref[...])

        @pl.when(i == n - 1)
        def _():
            dw_ref[...] = acc[...].astype(BF16)

    row = lambda w: pl.BlockSpec((tm, w), lambda i: (i, 0))
    return pl.pallas_call(
        body, name="inproj_bwd_w", grid=(n,),
        in_specs=[row(D)] + [row(GROUP_W)] * 8,
        out_specs=pl.BlockSpec((N_SHARD, D, SHARD_W), lambda i: (0, 0, 0), pipeline_mode=pl.Buffered(1)),
        out_shape=jax.ShapeDtypeStruct((N_SHARD, D, SHARD_W), BF16),
        scratch_shapes=[pltpu.VMEM((N_SHARD, D, SHARD_W), F32)],
        compiler_params=_cp("arbitrary"),
    )(h, *pieces)


def layer_fwd(x, vecs, w3, w_out, tabs, gather=None, head=None):
    cosf, sinf, lgam = tabs
    ret, sg, h, sb = inproj_fwd(x, vecs, w3)
    y_ret, o_ret, states = ret_fwd(ret, cosf, sinf, lgam)
    y_sb, o_sb, sb_end, *gathered = sb_fwd(sb, sg, gather)
    if callable(w_out):
        w_out = w_out(gathered)
    y, *x_next = outproj_fwd(x, vecs, y_ret, y_sb, w_out, head)
    saved = (x, ret, sg, h, sb, y_ret, o_ret, states, y_sb, o_sb, sb_end, y)
    return (x_next[0] if head is None else x_next), saved, gathered


def _by_shard(dw_out):
    return dw_out.reshape(N_SHARD, D_MODEL // N_SHARD, D_MODEL)


def layer_bwd(dx, saved, vecs, w3, w_out, tabs, later_grads=None):
    cosf, sinf, lgam = tabs
    x, ret, sg, h, sb, y_ret, o_ret, states, y_sb, o_sb, sb_end, y = saved
    dycat, dw_out, st_o = outproj_bwd(dx, y, vecs, y_ret, y_sb, w_out)
    dw_out = _by_shard(dw_out)
    ship = None if later_grads is None else (later_grads[0], dw_out, later_grads[1])
    *d_sb, = sb_bwd(sb, sg, o_sb, sb_end, dycat, ship)
    d_ret = ret_bwd(ret, cosf, sinf, lgam, o_ret, states, dycat)
    pieces = list(d_ret) + d_sb[:4]
    dw_in = inproj_bwd_w(h, pieces)
    dx, st_i, *recv_in = inproj_bwd_x(pieces, w3, x, vecs, dx, None if later_grads is None else (dw_in,))
    dmod = jnp.concatenate([st_i[0:2], st_o[0:1]], axis=0)
    grads = (dw_in, dw_out) if later_grads is None else (recv_in[0], d_sb[4])
    return dx, dmod, st_i[2:3], grads


def _place():
    return lax.axis_index("x"), lax.axis_index("y"), lax.axis_index("c")


def _other_chips(mx, my):
    return [(1 - mx, my), (mx, 1 - my), (1 - mx, 1 - my)]


_ANY = pl.BlockSpec(memory_space=pl.ANY)


_GATHER_SCRATCH = [pltpu.SemaphoreType.DMA((7,)), pltpu.SemaphoreType.DMA((7,)), pltpu.SemaphoreType.DMA(())]


def _gather_ops(x_ref, out_ref, send_sems, recv_sems, local_sem):
    mx, my, mc = _place()
    me, sibling = (mx, my, mc), (mx, my, 1 - mc)
    chips = _other_chips(mx, my)

    def slot(px, py, pc):
        return out_ref.at[4 * px + 2 * py + pc]

    def copy(k, block, to, src=None):
        return pltpu.make_async_remote_copy(
            src_ref=slot(*block) if src is None else src, dst_ref=slot(*block),
            send_sem=send_sems.at[k], recv_sem=recv_sems.at[k], device_id=to, device_id_type=MESH)

    mine = pltpu.make_async_copy(x_ref, slot(*me), local_sem)
    first = [copy(0, me, sibling, src=x_ref)]
    first += [copy(1 + j, me, (*chip, mc), src=x_ref) for j, chip in enumerate(chips)]
    passed = [copy(4 + j, (*chip, mc), sibling) for j, chip in enumerate(chips)]

    def start():
        mine.start()
        for cp in first:
            cp.start()

    def forward():
        for j, chip in enumerate(chips):
            copy(1 + j, (*chip, mc), me).wait_recv()
            passed[j].start()

    def finish():
        copy(0, sibling, me).wait_recv()
        for j, chip in enumerate(chips):
            copy(4 + j, (*chip, 1 - mc), me).wait_recv()
        for cp in first + passed:
            cp.wait_send()
        mine.wait()

    return start, forward, finish


class _Exchange:
    def __init__(self, ship):
        self.ship = list(ship or ())
        self.n_in = len(self.ship)
        self.n_out = 1 if self.ship else 0
        self.rows = [a.shape[1] for a in self.ship]
        self.in_specs = [_ANY] * self.n_in
        self.out_specs = [_ANY] * self.n_out
        self.out_shape = [jax.ShapeDtypeStruct((N_SHARD, sum(self.rows), SHARD_W), BF16)] * self.n_out
        sem = pltpu.SemaphoreType.DMA
        self.scratch = [sem((3,)), sem((3,)), sem(())] * self.n_out

    def ops(self, ship_refs, tail):
        if not self.ship:
            return (lambda: None), (lambda: None)
        recv, send_sems, recv_sems, local_sem = tail
        mx, my, mc = _place()
        my_chip = 2 * mx + my
        chips = _other_chips(mx, my)

        def pieces(s):
            firsts = np.cumsum([0] + self.rows[:-1])
            return [(ref.at[s], int(r0), n) for ref, r0, n in zip(ship_refs, firsts, self.rows)]

        def start():
            for src, r0, n in pieces(my_chip):
                pltpu.make_async_copy(src, recv.at[my_chip, pl.ds(r0, n)], local_sem).start()
            for j, (px, py) in enumerate(chips):
                for src, r0, n in pieces(2 * px + py):
                    pltpu.make_async_remote_copy(
                        src_ref=src, dst_ref=recv.at[my_chip, pl.ds(r0, n)],
                        send_sem=send_sems.at[j], recv_sem=recv_sems.at[j],
                        device_id=(px, py, mc), device_id_type=MESH).start()

        def finish():
            for j, (px, py) in enumerate(chips):
                whole = recv.at[2 * px + py]
                both = pltpu.make_async_remote_copy(
                    src_ref=whole, dst_ref=whole, send_sem=send_sems.at[j], recv_sem=recv_sems.at[j],
                    device_id=(px, py, mc), device_id_type=MESH)
                both.wait_recv()
                both.wait_send()
            pltpu.make_async_copy(recv.at[my_chip], recv.at[my_chip], local_sem).wait()

        return start, finish


def sum_and_swap(recv_a, recv_b, stats, tr=256):
    n, rows_a, cols = recv_a.shape
    na, nb = rows_a // tr, recv_b.shape[1] // tr
    nt = na + nb

    def body(a_ref, b_ref, st_ref, own_ref, sib_ref, stall_ref, slots, send_sems, recv_sem, *gather_sems):
        i = pl.program_id(0)
        mx, my, mc = _place()
        slot = i % 2
        g_start, g_forward, g_finish = _gather_ops(st_ref, stall_ref, *gather_sems)
        pl.when(i == 0)(g_start)
        pl.when(i == nt // 2)(g_forward)

        def push(k, tile):
            return pltpu.make_async_remote_copy(
                src_ref=slots.at[k], dst_ref=sib_ref.at[pl.ds(pl.multiple_of(tile * tr, tr), tr)],
                send_sem=send_sems.at[k], recv_sem=recv_sem, device_id=(mx, my, 1 - mc), device_id_type=MESH)

        pl.when(i >= 2)(lambda: push(slot, i - 2).wait_send())

        def total(r_ref):
            acc = r_ref[0].astype(F32)
            for k in range(1, n):
                acc = acc + r_ref[k].astype(F32)
            own_ref[...] = acc
            slots[slot] = acc

        pl.when(i < na)(lambda: total(a_ref))
        pl.when(i >= na)(lambda: total(b_ref))
        push(slot, i).start()

        @pl.when(i == nt - 1)
        def _():
            push(1 - slot, i - 1).wait_send()
            push(slot, i).wait_send()
            pltpu.make_async_remote_copy(src_ref=sib_ref, dst_ref=sib_ref, send_sem=send_sems.at[0], recv_sem=recv_sem,
                                         device_id=(mx, my, 1 - mc), device_id_type=MESH).wait_recv()
            g_finish()

    return pl.pallas_call(
        body, name="sum_and_swap", grid=(nt,),
        in_specs=[pl.BlockSpec((n, tr, cols), lambda i: (0, jnp.minimum(i, na - 1), 0)),
                  pl.BlockSpec((n, tr, cols), lambda i: (0, jnp.maximum(i - na, 0), 0)), _ANY],
        out_specs=[pl.BlockSpec((tr, cols), lambda i: (i, 0)), _ANY, _ANY],
        out_shape=[jax.ShapeDtypeStruct((nt * tr, cols), F32)] * 2
        + [jax.ShapeDtypeStruct((8,) + stats.shape, stats.dtype)],
        scratch_shapes=[pltpu.VMEM((2, tr, cols), F32), pltpu.SemaphoreType.DMA((2,)), pltpu.SemaphoreType.DMA(())]
        + _GATHER_SCRATCH,
        compiler_params=_cp("arbitrary"),
    )(recv_a, recv_b, stats)


def _adamw(w, g, m, v):
    m = ADAM_B1 * m + (1.0 - ADAM_B1) * g
    v = ADAM_B2 * v + (1.0 - ADAM_B2) * (g * g)
    m_hat = m / (1.0 - ADAM_B1 ** ADAM_STEP)
    v_hat = v / (1.0 - ADAM_B2 ** ADAM_STEP)
    delta = -ADAM_LR * (m_hat / (jnp.sqrt(v_hat) + ADAM_EPS) + ADAM_WD * w)
    return delta, m, v


def adam_slab(p_own, p_sib, w, m, v, row0, name, tr=512):
    L, R, C = w.shape
    nr = R // tr

    def body(a_ref, b_ref, w_ref, m_ref, v_ref, g_out, d_out, m_out, v_out):
        g = a_ref[...] + b_ref[...]
        d, m2, v2 = _adamw(w_ref[0], g, m_ref[0], v_ref[0])
        g_out[0], d_out[0], m_out[0], v_out[0] = g, d, m2, v2

    slab = pl.BlockSpec((tr, C), lambda l, i: (row0 // tr + l * nr + i, 0))
    blk = pl.BlockSpec((1, tr, C), lambda l, i: (l, i, 0))
    return pl.pallas_call(
        body, name=name, grid=(L, nr),
        in_specs=[slab, slab, blk, blk, blk], out_specs=[blk] * 4,
        out_shape=[jax.ShapeDtypeStruct(w.shape, F32)] * 4,
        compiler_params=_cp("arbitrary", "arbitrary"),
    )(p_own, p_sib, w, m, v)


def prologue(c8, w_ada, b_ada, norm_g, win_first):
    L, D, W = w_ada.shape

    def body(c_ref, w_ref, b_ref, g_ref, win_ref, vecs_ref, call_ref, wall_ref, mod_ref, mall_ref, *sems):
        w_start, w_forward, w_finish = _gather_ops(win_ref, wall_ref, *sems[0:3])
        for step in _gather_ops(c_ref, call_ref, *sems[3:6]):
            step()
        w_start()
        cv = call_ref[:, 0, :]
        ca = cv * _sigmoid(cv)
        for l in range(L):
            mod_ref[l * 8:(l + 1) * 8, :] = jnp.dot(ca, w_ref[l], precision=lax.Precision.HIGHEST,
                                                    preferred_element_type=F32)
        for step in _gather_ops(mod_ref, mall_ref, *sems[6:9]):
            step()
        mx, my, mc = _place()
        me = 4 * mx + 2 * my + mc
        rowid = lax.broadcasted_iota(jnp.int32, (L * 8, 1), 0)
        vecs_ref[...] = jnp.zeros_like(vecs_ref)
        for l in range(L):
            parts = [jnp.sum(jnp.where(rowid == l * 8 + me, mall_ref[2 * s + mc], 0.0), axis=0, keepdims=True)
                     for s in range(N_SHARD)]
            mod = jnp.concatenate(parts, axis=1) + b_ref[l:l + 1, :]
            for t in range(3):
                vecs_ref[l, t:t + 1, :] = mod[:, t * D:(t + 1) * D]
            vecs_ref[l, 3:4, :] = g_ref[l:l + 1, :]
        w_forward()
        w_finish()

    vmem = pl.BlockSpec(memory_space=pltpu.VMEM)
    return pl.pallas_call(
        body, name="prologue",
        in_specs=[vmem, vmem, vmem, vmem, _ANY], out_specs=[vmem, vmem, _ANY],
        out_shape=[jax.ShapeDtypeStruct((L, 8, D), F32), jax.ShapeDtypeStruct((8, 8, D), F32),
                   jax.ShapeDtypeStruct((8,) + win_first.shape, win_first.dtype)],
        scratch_shapes=[pltpu.VMEM((L * 8, W), F32), pltpu.VMEM((8, L * 8, W), F32)] + _GATHER_SCRATCH * 3,
        compiler_params=pltpu.CompilerParams(vmem_limit_bytes=VMEM_LIMIT_BYTES),
    )(c8, w_ada, b_ada, norm_g, win_first)


def ada_update(dmods, c_t, w, m, v, tr=512):
    L, D, W = w.shape

    def body(dm_ref, c_ref, w_ref, m_ref, v_ref, g_out, d_out, m_out, v_out):
        mx, my, _ = _place()
        shard = 2 * mx + my
        dm = jnp.zeros((8, W), F32)
        for s in range(N_SHARD):
            dm = dm + jnp.where(shard == s, dm_ref[0, :, s * W:(s + 1) * W], 0.0)
        cv = c_ref[...]
        ca = cv * _sigmoid(cv)
        g = jnp.zeros((tr, W), F32)
        for b in range(8):
            g = g + ca[:, b:b + 1] * dm[b:b + 1, :]
        d, m2, v2 = _adamw(w_ref[0], g, m_ref[0], v_ref[0])
        g_out[0], d_out[0], m_out[0], v_out[0] = g, d, m2, v2

    blk = pl.BlockSpec((1, tr, W), lambda l, i: (l, i, 0))
    return pl.pallas_call(
        body, name="ada_update", grid=(L, D // tr),
        in_specs=[pl.BlockSpec((1, 8, 3 * D), lambda l, i: (l, 0, 0)), pl.BlockSpec((tr, 8), lambda l, i: (i, 0)),
                  blk, blk, blk],
        out_specs=[blk] * 4, out_shape=[jax.ShapeDtypeStruct(w.shape, F32)] * 4,
        compiler_params=_cp("arbitrary", "arbitrary"),
    )(dmods, c_t, w, m, v)


STAT_ROWS = 16


def small_update(stats_all, norm, b_ada, final):
    def body(s_ref, *refs):
        ins, outs = refs[:9], refs[9:]
        tot = s_ref[0]
        for k in range(1, 8):
            tot = tot + s_ref[k]
        g_norm = tot[0:2, :]
        g_final = tot[2:3, :]
        g_b = jnp.concatenate(
            [jnp.concatenate([tot[3 + 3 * l + t:4 + 3 * l + t, :] for t in range(3)], axis=1) for l in range(DEPTH)],
            axis=0)
        for p, g in enumerate((g_norm, g_b, g_final)):
            w_ref, m_ref, v_ref = ins[3 * p:3 * p + 3]
            d, m2, v2 = _adamw(w_ref[...], g, m_ref[...], v_ref[...])
            for o_ref, val in zip(outs[4 * p:4 * p + 4], (g, d, m2, v2)):
                o_ref[...] = val
        loss = (0.5 / D_MODEL) * jnp.sum(tot[9:10, :], axis=1, keepdims=True)
        outs[12][...] = jnp.broadcast_to(loss, (8, LANES))

    shapes = []
    for w, _, _ in (norm, b_ada, final):
        shapes += [jax.ShapeDtypeStruct(w.shape, F32)] * 4
    shapes.append(jax.ShapeDtypeStruct((8, LANES), F32))
    return pl.pallas_call(body, name="small_update", out_shape=shapes)(stats_all, *norm, *b_ada, *final)


def kernel(x, c, norm_g, w_ada, b_ada, w_in, w_out, final_g, loss_target, m_norm_g, m_w_ada, m_b_ada, m_w_in, m_w_out, m_final_g, v_norm_g, v_w_ada, v_b_ada, v_w_in, v_w_out, v_final_g):
    S, D = x.shape[1], x.shape[2]
    mc = lax.axis_index("c")
    out_rows = D // N_SHARD

    def my_half(a, rows):
        return lax.dynamic_slice_in_dim(a, mc * rows, rows, axis=0)

    assert DEPTH == 2
    win = [my_half(w_in[l], D // 2).astype(BF16) for l in range(DEPTH)]
    wout = [my_half(w_out[l], out_rows // 2).astype(BF16) for l in range(DEPTH)]
    rest = [jnp.concatenate(wout, axis=0), win[1]]

    def unpack(gathered):
        outs, w3_second = gathered
        outs = outs.reshape(N_SHARD, 2, DEPTH, out_rows // 2, SHARD_W)
        return outs[:, :, 0].reshape(D, D), (w3_second.reshape(N_SHARD, D, SHARD_W), outs[:, :, 1].reshape(D, D))

    vecs, c_all, w3_first = prologue(jnp.broadcast_to(c, (8, D)), w_ada, b_ada, norm_g, win[0])
    c_all, w3_first = c_all[:, 0, :], w3_first.reshape(N_SHARD, D, SHARD_W)

    tabs = (*rope_tables(S), ret_log_gamma())
    saved = [None] * DEPTH
    h, saved[0], wall = layer_fwd(x[0], vecs[0], w3_first, lambda g: unpack(g)[0], tabs, rest)
    weights = [(w3_first, unpack(wall)[0]), unpack(wall)[1]]
    head = (jnp.broadcast_to(final_g[None, :], (8, D)), loss_target[0])
    (dx, st_loss), saved[1], _ = layer_fwd(h, vecs[1], *weights[1], tabs, head=head)

    dmod, dnorm, grads = [None] * DEPTH, [None] * DEPTH, None
    for l in reversed(range(DEPTH)):
        dx, dmod[l], dnorm[l], grads = layer_bwd(dx, saved[l], vecs[l], *weights[l], tabs, grads)

    stats = jnp.concatenate(dnorm + [st_loss[0:1]] + dmod + [st_loss[1:2], jnp.zeros((STAT_ROWS - 10, D), F32)], axis=0)
    p_own, p_sib, stats_all = sum_and_swap(*grads, stats)
    res_in = adam_slab(p_own, p_sib, w_in, m_w_in, v_w_in, 0, "adam_w_in")
    res_out = adam_slab(p_own, p_sib, w_out, m_w_out, v_w_out, DEPTH * D, "adam_w_out", tr=256)

    dmods = stats_all[:, 3:9, :].reshape(8, DEPTH, 3 * D).transpose(1, 0, 2)
    res_ada = ada_update(dmods, c_all.T, w_ada, m_w_ada, v_w_ada)
    small = small_update(stats_all, (norm_g, m_norm_g, v_norm_g), (b_ada, m_b_ada, v_b_ada),
                         (final_g[None, :], m_final_g[None, :], v_final_g[None, :]))
    res_norm, res_b, res_final = small[0:4], small[4:8], [a[0] for a in small[8:12]]
    loss = small[12][0, 0]

    by_kind = [res_norm, res_ada, res_b, res_in, res_out, res_final]
    outs = [loss, dx[None]]
    for kind in range(4):
        outs += [r[kind] for r in by_kind]
    return tuple(outs)
```

```python
import numpy as np
import jax
import jax.numpy as jnp
from jax import lax
from jax.experimental import pallas as pl
from jax.experimental.pallas import tpu as pltpu

F32, BF16 = jnp.float32, jnp.bfloat16
MESH = pl.DeviceIdType.MESH

D_MODEL = 1024
DEPTH = 2
SHARD_W = 1024
N_SHARD = 4
GROUP_W = 512
LANES = 128
SB_HEAD_DIM = 64
RET_HEAD_DIM = 128
CHUNK = 64
ROPE_BASE = 10000.0
EPS = 1e-6
SQ_SCALE = SB_HEAD_DIM ** -0.5
RK_SCALE = RET_HEAD_DIM ** -0.5
SB_T = 1024
SB_CHAINS = 16
SB_NB = 4
RET_T = 256
EXP_ZERO = -104.0
MASKED_SCORE = -1e4
VMEM_LIMIT_BYTES = 56 * 2 ** 20

ADAM_LR, ADAM_B1, ADAM_B2, ADAM_EPS, ADAM_WD, ADAM_STEP = 0.001, 0.9, 0.999, 1e-08, 0.01, 10


def _cp(*sem):
    return pltpu.CompilerParams(dimension_semantics=sem, vmem_limit_bytes=VMEM_LIMIT_BYTES)


def _dot(a, b):
    return lax.dot_general(a, b, (((1,), (0,)), ((), ())), preferred_element_type=F32)


def _dot_nt(a, b):
    return lax.dot_general(a, b, (((1,), (1,)), ((), ())), preferred_element_type=F32)


def _dot_tn(a, b):
    return lax.dot_general(a, b, (((0,), (0,)), ((), ())), preferred_element_type=F32)


def _running_sum(a, tri):
    return _dot(a.astype(BF16), tri)


def _sigmoid(x):
    return 1.0 / (1.0 + jnp.exp(-x))


def _rowsum(a):
    return jnp.sum(a, axis=1, keepdims=True)


def _rowmean(a):
    return jnp.mean(a, axis=1, keepdims=True)


def inproj_fwd(x, vecs, w3, tm=512):
    S, D = x.shape

    def body(x_ref, v_ref, w_ref, ret_ref, sg_ref, h_ref, sb_ref):
        xv = x_ref[...]
        r = lax.rsqrt(_rowmean(xv * xv) + EPS)
        h = xv * r * v_ref[3:4, :] * (1.0 + v_ref[1:2, :]) + v_ref[0:1, :]
        hb = h.astype(BF16)
        h_ref[...] = hb
        for s in range(N_SHARD):
            p = _dot(hb, w_ref[s])
            if s < 2:
                ret_ref[:, s * SHARD_W:(s + 1) * SHARD_W] = p
            if s == 2:
                sb_ref[:, 0:GROUP_W] = (p[:, 0:GROUP_W] * SQ_SCALE).astype(BF16)
                sb_ref[:, GROUP_W:SHARD_W] = p[:, GROUP_W:].astype(BF16)
            if s == 3:
                sb_ref[:, SHARD_W:SHARD_W + GROUP_W] = p[:, 0:GROUP_W].astype(BF16)
                sg_ref[...] = p[:, GROUP_W:]

    row = lambda w: pl.BlockSpec((tm, w), lambda i: (i, 0))
    return pl.pallas_call(
        body, name="inproj_fwd", grid=(S // tm,),
        in_specs=[row(D), pl.BlockSpec((8, D), lambda i: (0, 0)),
                  pl.BlockSpec((N_SHARD, D, SHARD_W), lambda i: (0, 0, 0))],
        out_specs=[row(2 * SHARD_W), row(GROUP_W), row(D), row(3 * GROUP_W)],
        out_shape=[jax.ShapeDtypeStruct((S, 2 * SHARD_W), F32), jax.ShapeDtypeStruct((S, GROUP_W), F32),
                   jax.ShapeDtypeStruct((S, D), BF16), jax.ShapeDtypeStruct((S, 3 * GROUP_W), BF16)],
        compiler_params=_cp("arbitrary"),
    )(x, vecs, w3)


def _sb_logits(qh, k2, keep):
    z = _dot_nt(qh, k2)
    if keep is not None:
        z = jnp.where(keep, z, MASKED_SCORE)
    sp = jnp.log(1.0 + jnp.exp(-jnp.abs(z)))
    lb = jnp.minimum(z, 0.0) - sp
    return lb, lb - z


class _sb_chains:
    def __init__(self, i, q2, do_b=None):
        t = self.t = SB_T // SB_CHAINS
        self.C = range(SB_CHAINS)
        r = lax.broadcasted_iota(jnp.int32, (SB_NB * t, SB_NB * t), 0)
        c = lax.broadcasted_iota(jnp.int32, (SB_NB * t, SB_NB * t), 1)
        self.later_all = jnp.where(r > c, 1.0, 0.0).astype(BF16)
        self.earlier_all = jnp.where(r < c, 1.0, 0.0).astype(BF16)
        self.later, self.earlier = self.later_all[:t, :t], self.earlier_all[:t, :t]
        self.head0 = lax.broadcasted_iota(jnp.int32, (1, LANES), 1) < SB_HEAD_DIM
        row = lax.broadcasted_iota(jnp.int32, (2 * t, SB_NB * t), 0) & (t - 1)
        col = lax.broadcasted_iota(jnp.int32, (2 * t, SB_NB * t), 1)
        qt = [SB_CHAINS * i + cc for cc in self.C]
        self.first = [jnp.maximum(qt[cc] - (SB_NB - 1), 0) for cc in self.C]
        self._ahead, self._qt = col - row, qt
        self.qs = [self._stack(q2[cc * t:(cc + 1) * t]) for cc in self.C]
        if do_b is not None:
            self.dos = [self._stack(do_b[cc * t:(cc + 1) * t]) for cc in self.C]

    def keep_of(self, cc):
        return self._ahead < (self._qt[cc] - self.first[cc]) * self.t

    def _stack(self, a):
        zero = jnp.zeros_like(a)
        return jnp.concatenate([jnp.where(self.head0, a, zero), jnp.where(self.head0, zero, a)], axis=0)

    def rows(self, ref, j, n):
        return ref[pl.ds(pl.multiple_of(j * self.t, self.t), n * self.t), :]

    def suffix(self, lk):
        return _running_sum(lk, self.later_all), _rowsum(lk)

    def prefix(self, g, G0):
        return _running_sum(g, self.earlier_all) + G0


def sb_fwd(sb, sg, gather=None):
    S = sb.shape[0]
    T = SB_T
    nq = S // T
    carried = list(gather or ())
    ng = len(carried)

    def body(*refs):
        (q_ref, k_ref, v_ref, sg_ref), refs = refs[:4], refs[4:]
        x_refs, (y_ref, o_ref, end_ref), out_refs, sems = refs[:ng], refs[ng:ng + 3], refs[ng + 3:2 * ng + 3], refs[2 * ng + 3:]
        p, i = pl.program_id(0), pl.program_id(1)
        gathers = [_gather_ops(x_refs[g], out_refs[g], *sems[3 * g:3 * g + 3]) for g in range(ng)]
        for start, forward, _ in gathers:
            pl.when(jnp.logical_and(p == 0, i == 0))(start)
            pl.when(jnp.logical_and(p == 3, i == 0))(forward)
        ch = _sb_chains(i, q_ref[...])
        later, head0 = ch.later, ch.head0
        lbk = [_sb_logits(ch.qs[c], ch.rows(k_ref, ch.first[c], SB_NB), ch.keep_of(c)) for c in ch.C]
        suffix, R = zip(*[ch.suffix(lbk[c][1]) for c in ch.C])
        aa = [jnp.exp(lbk[c][0] + suffix[c]) for c in ch.C]
        acc = [_dot(aa[c].astype(BF16), ch.rows(v_ref, ch.first[c], SB_NB)) for c in ch.C]

        nc = len(ch.C)

        def alive(n, Rs):
            m = None
            for c in ch.C:
                rc = jnp.where(ch.first[c] - n > 0, Rs[c], EXP_ZERO)
                m = rc if m is None else jnp.maximum(m, rc)
            return jnp.max(m)

        def cond(st):
            return st[-1] > EXP_ZERO

        def step(st):
            n, accs, Rs = st[0], list(st[1:1 + nc]), list(st[1 + nc:1 + 2 * nc])
            for c in ch.C:
                j = ch.first[c] - 1 - n
                jc = jnp.maximum(j, 0)
                lb, lk = _sb_logits(ch.qs[c], ch.rows(k_ref, jc, 1), None)
                a = jnp.exp(lb + _running_sum(lk, later) + Rs[c])
                cx = _dot(a.astype(BF16), ch.rows(v_ref, jc, 1))
                accs[c] = jnp.where(j >= 0, accs[c] + cx, accs[c])
                Rs[c] = jnp.where(j >= 0, Rs[c] + _rowsum(lk), Rs[c])
            return (n + 1, *accs, *Rs, alive(n + 1, Rs))

        st = lax.while_loop(cond, step, (jnp.int32(0), *acc, *R, alive(0, R)))
        n_end, acc, R = st[0], st[1:1 + nc], st[1 + nc:1 + 2 * nc]
        outs = []
        for c in ch.C:
            base = c * (2 * ch.t + 8)
            end_ref[0, 0, base:base + 2 * ch.t, :] = jnp.broadcast_to(R[c], (2 * ch.t, 8))
            end_ref[0, 0, base + 2 * ch.t:base + 2 * ch.t + 8, :] = jnp.full((8, 8), n_end.astype(F32))
            outs.append(jnp.where(head0, acc[c][:ch.t], acc[c][ch.t:]))
        o = jnp.concatenate(outs, axis=0)
        o_ref[...] = o
        sg = sg_ref[...]
        y_ref[...] = (o * (sg * _sigmoid(sg))).astype(BF16)
        for _, _, finish in gathers:
            pl.when(jnp.logical_and(p == 3, i == nq - 1))(finish)

    return pl.pallas_call(
        body, name="sb_fwd", grid=(4, nq),
        in_specs=[pl.BlockSpec((T, LANES), lambda p, i: (i, p)),
                  pl.BlockSpec((S, LANES), lambda p, i: (0, 4 + p)),
                  pl.BlockSpec((S, LANES), lambda p, i: (0, 8 + p)),
                  pl.BlockSpec((T, LANES), lambda p, i: (i, p))] + [_ANY for _ in carried],
        out_specs=[pl.BlockSpec((T, LANES), lambda p, i: (i, p)),
                   pl.BlockSpec((T, LANES), lambda p, i: (i, p)),
                   pl.BlockSpec((1, 1, SB_CHAINS * (2 * T // SB_CHAINS + 8), 8), lambda p, i: (p, i, 0, 0))] + [_ANY for _ in carried],
        out_shape=[jax.ShapeDtypeStruct((S, GROUP_W), BF16),
                   jax.ShapeDtypeStruct((S, GROUP_W), F32),
                   jax.ShapeDtypeStruct((4, nq, SB_CHAINS * (2 * T // SB_CHAINS + 8), 8), F32)]
        + [jax.ShapeDtypeStruct((8,) + a.shape, a.dtype) for a in carried],
        scratch_shapes=_GATHER_SCRATCH * ng,
        compiler_params=_cp("arbitrary", "arbitrary"),
    )(sb, sb, sb, sg, *carried)


def sb_bwd(sb, sg, o, sb_end, dycat, ship=None):
    S = sb.shape[0]
    T = SB_T
    nq = S // T
    ex = _Exchange(ship)

    def body(*refs):
        (q_ref, k_ref, v_ref, sg_ref, o_ref, dy_ref, end_ref), refs = refs[:7], refs[7:]
        ship_refs, (dq_ref, dk_ref, dv_ref, dsg_ref), refs = refs[:ex.n_in], refs[ex.n_in:ex.n_in + 4], refs[ex.n_in + 4:]
        recv, (dk_acc, dv_acc), sems = refs[:ex.n_out], refs[ex.n_out:ex.n_out + 2], refs[ex.n_out + 2:]
        start, finish = ex.ops(ship_refs, recv + sems)
        p, i = pl.program_id(0), pl.program_id(1)
        pl.when(jnp.logical_and(p == 0, i == 0))(start)

        @pl.when(i == 0)
        def _():
            dk_acc[...] = jnp.zeros_like(dk_acc)
            dv_acc[...] = jnp.zeros_like(dv_acc)

        sg = sg_ref[...]
        sig = _sigmoid(sg)
        dy = dy_ref[...]
        dsg_ref[...] = (dy * o_ref[...] * (sig * (1.0 + sg * (1.0 - sig)))).astype(BF16)
        do_b = (dy * (sg * sig)).astype(BF16)
        ch = _sb_chains(i, q_ref[...], do_b)
        later, earlier, head0, t = ch.later, ch.earlier, ch.head0, ch.t
        end = end_ref[0, 0]

        def grads(c, j, n, a, lb, g, G, keep):
            dz = g - jnp.exp(lb) * (g + G)
            if keep is not None:
                dz = jnp.where(keep, dz, 0.0)
            dzb = dz.astype(BF16)
            rows = pl.ds(pl.multiple_of(j * t, t), n * t)
            dk_acc[rows, :] += _dot_tn(dzb, ch.qs[c])
            dv_acc[rows, :] += _dot_tn(a.astype(BF16), ch.dos[c])
            return _dot(dzb, ch.rows(k_ref, j, n))

        nc = len(ch.C)
        n_end = jnp.max(end[2 * t:2 * t + 8, :]).astype(jnp.int32)

        def sweep(m, st):
            dqs, G0s, lefts = list(st[:nc]), list(st[nc:2 * nc]), list(st[2 * nc:])
            for c in ch.C:
                j = ch.first[c] - n_end + m
                jc = jnp.maximum(j, 0)
                lb, lk = _sb_logits(ch.qs[c], ch.rows(k_ref, jc, 1), None)
                stick = lefts[c] - _rowsum(lk)
                a = jnp.where(j >= 0, jnp.exp(lb + _running_sum(lk, later) + stick), 0.0)
                g = a * _dot_nt(ch.dos[c], ch.rows(v_ref, jc, 1))
                G = _running_sum(g, earlier) + G0s[c]
                dqs[c] = dqs[c] + grads(c, jc, 1, a, lb, jnp.where(j >= 0, g, 0.0), jnp.where(j >= 0, G, 0.0), None)
                G0s[c] = G0s[c] + _rowsum(g)
                lefts[c] = jnp.where(j >= 0, stick, lefts[c])
            return (*dqs, *G0s, *lefts)

        lefts = [end[c * (2 * t + 8):c * (2 * t + 8) + 2 * t, 0:1] for c in ch.C]
        st = lax.fori_loop(0, n_end, sweep, (*[jnp.zeros((2 * t, LANES), F32)] * nc,
                                             *[jnp.zeros((2 * t, 1), F32)] * nc, *lefts))
        dq, G0 = st[:nc], st[nc:2 * nc]

        lbk = [_sb_logits(ch.qs[c], ch.rows(k_ref, ch.first[c], SB_NB), ch.keep_of(c)) for c in ch.C]
        suffix = [ch.suffix(lbk[c][1])[0] for c in ch.C]
        aa = [jnp.exp(lbk[c][0] + suffix[c]) for c in ch.C]
        g = [aa[c] * _dot_nt(ch.dos[c], ch.rows(v_ref, ch.first[c], SB_NB)) for c in ch.C]
        G = [ch.prefix(g[c], G0[c]) for c in ch.C]
        for c in ch.C:
            dqc = dq[c] + grads(c, ch.first[c], SB_NB, aa[c], lbk[c][0], g[c], G[c], None)
            dq_ref[c * t:(c + 1) * t, :] = (jnp.where(head0, dqc[:t], dqc[t:]) * SQ_SCALE).astype(BF16)

        @pl.when(i == nq - 1)
        def _():
            dk_ref[...] = dk_acc[...].astype(BF16)
            dv_ref[...] = dv_acc[...].astype(BF16)

        pl.when(jnp.logical_and(p == 3, i == nq - 1))(finish)

    tile_spec = lambda c0: pl.BlockSpec((T, LANES), lambda p, i: (i, c0 + p))
    head_spec = lambda c0: pl.BlockSpec((S, LANES), lambda p, i: (0, c0 + p))
    return pl.pallas_call(
        body, name="sb_bwd", grid=(4, nq),
        in_specs=[tile_spec(0), head_spec(4), head_spec(8), tile_spec(0), tile_spec(0), tile_spec(4),
                  pl.BlockSpec((1, 1, SB_CHAINS * (2 * T // SB_CHAINS + 8), 8), lambda p, i: (p, i, 0, 0))] + ex.in_specs,
        out_specs=[tile_spec(0), head_spec(0), head_spec(0), tile_spec(0)] + ex.out_specs,
        out_shape=[jax.ShapeDtypeStruct((S, GROUP_W), BF16)] * 4 + ex.out_shape,
        scratch_shapes=[pltpu.VMEM((S, LANES), F32), pltpu.VMEM((S, LANES), F32)] + ex.scratch,
        compiler_params=_cp("arbitrary", "arbitrary"),
    )(sb, sb, sb, sg, o, dycat, sb_end, *ex.ship)


def rope_tables(S):
    half = RET_HEAD_DIM // 2
    lane = jnp.arange(RET_HEAD_DIM)
    inv = ROPE_BASE ** (-(lane % half).astype(F32) / half)
    ang = jnp.arange(S, dtype=F32)[:, None] * inv[None, :]
    return jnp.cos(ang), jnp.where(lane < half, -1.0, 1.0)[None, :] * jnp.sin(ang)


def ret_log_gamma():
    return jnp.log1p(-(2.0 ** (-5.0 - jnp.arange(4, dtype=F32))))


def _swap_halves(a):
    return pltpu.roll(a, RET_HEAD_DIM // 2, axis=1)


def _ret_decay_mask(lg):
    n = lax.broadcasted_iota(jnp.int32, (RET_T, RET_T), 0)
    m = lax.broadcasted_iota(jnp.int32, (RET_T, RET_T), 1)
    dist = jnp.abs(n - m).astype(F32)
    return jnp.where((m // CHUNK) <= (n // CHUNK), jnp.exp(lg * dist), 0.0)


def _ret_block(lg, rq, rk, rv, cosf, sinf, dm):
    q = rq * cosf + _swap_halves(rq) * sinf
    k = (rk * cosf + _swap_halves(rk) * sinf) * RK_SCALE
    qb, kb, vb = q.astype(BF16), k.astype(BF16), rv.astype(BF16)
    sc = _dot_nt(qb, kb) * dm
    nloc = lax.broadcasted_iota(jnp.int32, (RET_T, 1), 0).astype(F32)
    qdec = jnp.exp(lg * (nloc + 1.0))
    kdec = jnp.exp(lg * (RET_T - 1.0 - nloc))
    block_dec = jnp.exp(jnp.full((1, LANES), lg * RET_T, F32))
    return q, k, qb, kb, vb, sc, qdec, kdec, block_dec


RET_RB = 4


def _ret_specs(S, rb):
    group = lambda c0: pl.BlockSpec((RET_RB * RET_T, GROUP_W), lambda s: (rb(s), c0))
    return group, pl.BlockSpec((RET_RB * RET_T, LANES), lambda s: (rb(s), 0))


def _ret_chains():
    chains = [(h, b) for b in range(RET_RB) for h in range(4)]
    rows = lambda c: (slice(c[1] * RET_T, (c[1] + 1) * RET_T), slice(c[0] * LANES, (c[0] + 1) * LANES))
    tab = lambda ref, c: ref[c[1] * RET_T:(c[1] + 1) * RET_T, :]
    return chains, rows, tab


def _ret_blocks(chains, rows, lg_ref, rq_ref, rk_ref, rv_ref, cosf, sinf, dm_ref):
    blk = {c: _ret_block(lg_ref[c[0]], rq_ref[rows(c)], rk_ref[rows(c)], rv_ref[rows(c)],
                         cosf[c], sinf[c], dm_ref[c[0]]) for c in chains}
    return ({c: blk[c][n] for c in chains} for n in range(9))


def ret_fwd(proj, cosf, sinf, lgam):
    S = proj.shape[0]
    nb = S // RET_T
    group, row_tab = _ret_specs(S, lambda s: s)

    def body(lg_ref, rq_ref, rk_ref, rv_ref, rg_ref, cos_ref, sin_ref, y_ref, o_ref, st_out, st_ref, dm_ref):
        @pl.when(pl.program_id(0) == 0)
        def _():
            st_ref[...] = jnp.zeros_like(st_ref)
            for h in range(4):
                dm_ref[h] = _ret_decay_mask(lg_ref[h])

        chains, rows, tab = _ret_chains()
        cosf, sinf = {c: tab(cos_ref, c) for c in chains}, {c: tab(sin_ref, c) for c in chains}
        q, k, qb, kb, vb, sc, qdec, kdec, block_dec = _ret_blocks(
            chains, rows, lg_ref, rq_ref, rk_ref, rv_ref, cosf, sinf, dm_ref)
        kv = {c: _dot_tn((k[c] * kdec[c]).astype(BF16), vb[c]) for c in chains}
        st = {(h, 0): st_ref[h] for h in range(4)}
        for b in range(RET_RB):
            for h in range(4):
                st[(h, b + 1)] = st[(h, b)] * block_dec[(h, b)] + kv[(h, b)]
        for h, b in chains:
            st_out[h, b] = st[(h, b)]
        for h in range(4):
            st_ref[h] = st[(h, RET_RB)]
        o = {c: _dot(sc[c].astype(BF16), vb[c]) + _dot(qb[c], st[c].astype(BF16)) * qdec[c] for c in chains}
        for c in chains:
            o_ref[rows(c)] = o[c]
        cen = {c: o[c] - _rowmean(o[c]) for c in chains}
        on = {c: cen[c] * lax.rsqrt(_rowmean(cen[c] * cen[c]) + EPS) for c in chains}
        rg = {c: rg_ref[rows(c)] for c in chains}
        for c in chains:
            y_ref[rows(c)] = (on[c] * (rg[c] * _sigmoid(rg[c]))).astype(BF16)

    return pl.pallas_call(
        body, name="ret_fwd", grid=(nb // RET_RB,),
        in_specs=[pl.BlockSpec(memory_space=pltpu.SMEM),
                  group(0), group(1), group(2), group(3), row_tab, row_tab],
        out_specs=[group(0), group(0),
                   pl.BlockSpec((4, RET_RB, LANES, LANES), lambda s: (0, s, 0, 0))],
        out_shape=[jax.ShapeDtypeStruct((S, GROUP_W), BF16),
                   jax.ShapeDtypeStruct((S, GROUP_W), F32),
                   jax.ShapeDtypeStruct((4, nb, LANES, LANES), F32)],
        scratch_shapes=[pltpu.VMEM((4, LANES, LANES), F32), pltpu.VMEM((4, RET_T, RET_T), F32)],
        compiler_params=_cp("arbitrary"),
    )(lgam, proj, proj, proj, proj, cosf, sinf)


def ret_bwd(proj, cosf, sinf, lgam, o, states, dycat):
    S = proj.shape[0]
    nsteps = S // RET_T // RET_RB
    rev = lambda s: nsteps - 1 - s
    group, row_tab = _ret_specs(S, rev)

    def body(lg_ref, rq_ref, rk_ref, rv_ref, rg_ref, cos_ref, sin_ref, o_ref, st_in, dy_ref,
             drq_ref, drk_ref, drv_ref, drg_ref, ds_ref, dm_ref):
        @pl.when(pl.program_id(0) == 0)
        def _():
            ds_ref[...] = jnp.zeros_like(ds_ref)
            for h in range(4):
                dm_ref[h] = _ret_decay_mask(lg_ref[h])

        chains, rows, tab = _ret_chains()
        cosf, sinf = {c: tab(cos_ref, c) for c in chains}, {c: tab(sin_ref, c) for c in chains}
        dms = {c: dm_ref[c[0]] for c in chains}
        q, k, qb, kb, vb, sc, qdec, kdec, block_dec = _ret_blocks(
            chains, rows, lg_ref, rq_ref, rk_ref, rv_ref, cosf, sinf, dm_ref)
        o_v = {c: o_ref[rows(c)] for c in chains}
        cen = {c: o_v[c] - _rowmean(o_v[c]) for c in chains}
        rstd = {c: lax.rsqrt(_rowmean(cen[c] * cen[c]) + EPS) for c in chains}
        on = {c: cen[c] * rstd[c] for c in chains}
        rg = {c: rg_ref[rows(c)] for c in chains}
        sig = {c: _sigmoid(rg[c]) for c in chains}
        dy = {c: dy_ref[rows(c)] for c in chains}
        for c in chains:
            drg_ref[rows(c)] = (dy[c] * on[c] * (sig[c] * (1.0 + rg[c] * (1.0 - sig[c])))).astype(BF16)
        don = {c: dy[c] * (rg[c] * sig[c]) for c in chains}
        do = {c: rstd[c] * (don[c] - _rowmean(don[c]) - on[c] * _rowmean(don[c] * on[c])) for c in chains}
        dob = {c: do[c].astype(BF16) for c in chains}
        dsc = {c: (_dot_nt(dob[c], vb[c]) * dms[c]).astype(BF16) for c in chains}
        st_b = {c: st_in[c[0], c[1]].astype(BF16) for c in chains}
        dst = {c: _dot_tn((q[c] * qdec[c]).astype(BF16), dob[c]) for c in chains}
        dsn = {(h, RET_RB): ds_ref[h] for h in range(4)}
        for b in reversed(range(RET_RB)):
            for h in range(4):
                dsn[(h, b)] = dsn[(h, b + 1)] * block_dec[(h, b)] + dst[(h, b)]
        for h in range(4):
            ds_ref[h] = dsn[(h, 0)]
        dsn_b = {c: dsn[(c[0], c[1] + 1)].astype(BF16) for c in chains}
        dq = {c: _dot(dsc[c], kb[c]) + _dot_nt(dob[c], st_b[c]) * qdec[c] for c in chains}
        dk = {c: (_dot_tn(dsc[c], qb[c]) + _dot_nt(vb[c], dsn_b[c]) * kdec[c]) * RK_SCALE for c in chains}
        dv = {c: _dot_tn(sc[c].astype(BF16), dob[c]) + _dot((k[c] * kdec[c]).astype(BF16), dsn_b[c])
              for c in chains}
        for c in chains:
            drq_ref[rows(c)] = (dq[c] * cosf[c] + _swap_halves(dq[c] * sinf[c])).astype(BF16)
            drk_ref[rows(c)] = (dk[c] * cosf[c] + _swap_halves(dk[c] * sinf[c])).astype(BF16)
            drv_ref[rows(c)] = dv[c].astype(BF16)

    return pl.pallas_call(
        body, name="ret_bwd", grid=(nsteps,),
        in_specs=[pl.BlockSpec(memory_space=pltpu.SMEM),
                  group(0), group(1), group(2), group(3), row_tab, row_tab,
                  group(0), pl.BlockSpec((4, RET_RB, LANES, LANES), lambda s: (0, rev(s), 0, 0)),
                  group(0)],
        out_specs=[group(0)] * 4,
        out_shape=[jax.ShapeDtypeStruct((S, GROUP_W), BF16)] * 4,
        scratch_shapes=[pltpu.VMEM((4, LANES, LANES), F32), pltpu.VMEM((4, RET_T, RET_T), F32)],
        compiler_params=_cp("arbitrary"),
    )(lgam, proj, proj, proj, proj, cosf, sinf, o, states, dycat)


def outproj_fwd(x, vecs, y_ret, y_sb, w_out, head=None, tm=1024):
    S, D = x.shape
    tm = min(tm, S)
    last = list(head or ())

    def body(x_ref, v_ref, yr_ref, ys_ref, w_ref, *refs):
        y = _dot(yr_ref[...], w_ref[0:GROUP_W, :]) + _dot(ys_ref[...], w_ref[GROUP_W:, :])
        xv = x_ref[...] + v_ref[2:3, :] * y
        if not last:
            y_ref, xo_ref = refs
            y_ref[...] = y.astype(BF16)
            xo_ref[...] = xv
            return
        g_ref, t_ref, y_ref, dx_ref, st_ref = refs
        y_ref[...] = y.astype(BF16)

        @pl.when(pl.program_id(0) == 0)
        def _():
            st_ref[...] = jnp.zeros_like(st_ref)

        g = g_ref[0:1, :]
        for hh in range(2):
            rows = slice(hh * (tm // 2), (hh + 1) * (tm // 2))
            xh = xv[rows, :]
            r = lax.rsqrt(_rowmean(xh * xh) + EPS)
            xn = xh * r
            err = xn * g - t_ref[rows, :]
            dy = err * (1.0 / D)
            dxn = dy * g
            dx_ref[rows, :] = r * (dxn - xn * _rowmean(dxn * xn))
            st_ref[0:1, :] += jnp.sum(dy * xn, axis=0, keepdims=True)
            st_ref[1:2, :] += jnp.sum(err * err, axis=0, keepdims=True)

    row = lambda w: pl.BlockSpec((tm, w), lambda i: (i, 0))
    fixed = pl.BlockSpec((8, D), lambda i: (0, 0))
    return pl.pallas_call(
        body, name="outproj_fwd", grid=(S // tm,),
        in_specs=[row(D), fixed, row(GROUP_W), row(GROUP_W), pl.BlockSpec((D, D), lambda i: (0, 0))]
        + ([fixed, row(D)] if last else []),
        out_specs=[row(D), row(D)] + ([fixed] if last else []),
        out_shape=[jax.ShapeDtypeStruct((S, D), BF16), jax.ShapeDtypeStruct((S, D), F32)]
        + ([jax.ShapeDtypeStruct((8, D), F32)] if last else []),
        compiler_params=_cp("arbitrary"),
    )(x, vecs, y_ret, y_sb, w_out, *last)


def outproj_bwd(dx, y, vecs, y_ret, y_sb, w_out, tm=1024):
    S, D = dx.shape
    tm = min(tm, S)
    n = S // tm

    def body(dx_ref, y_ref, v_ref, yr_ref, ys_ref, w_ref, dyc_ref, dw_ref, st_ref, acc):
        i = pl.program_id(0)

        @pl.when(i == 0)
        def _():
            st_ref[...] = jnp.zeros_like(st_ref)
            acc[...] = jnp.zeros_like(acc)

        dxv = dx_ref[...]
        st_ref[0:1, :] += jnp.sum(dxv * y_ref[...].astype(F32), axis=0, keepdims=True)
        dyy = (dxv * v_ref[2:3, :]).astype(BF16)
        dyc_ref[...] = _dot_nt(dyy, w_ref[...])
        acc[0:GROUP_W, :] += _dot_tn(yr_ref[...], dyy)
        acc[GROUP_W:, :] += _dot_tn(ys_ref[...], dyy)

        @pl.when(i == n - 1)
        def _():
            dw_ref[...] = acc[...].astype(BF16)

    row = lambda w: pl.BlockSpec((tm, w), lambda i: (i, 0))
    fixed = lambda r: pl.BlockSpec((r, D), lambda i: (0, 0))
    return pl.pallas_call(
        body, name="outproj_bwd", grid=(n,),
        in_specs=[row(D), row(D), fixed(8), row(GROUP_W), row(GROUP_W), fixed(D)],
        out_specs=[row(D), fixed(D), fixed(8)],
        out_shape=[jax.ShapeDtypeStruct((S, D), F32), jax.ShapeDtypeStruct((D, D), BF16),
                   jax.ShapeDtypeStruct((8, D), F32)],
        scratch_shapes=[pltpu.VMEM((D, D), F32)],
        compiler_params=_cp("arbitrary"),
    )(dx, y, vecs, y_ret, y_sb, w_out)


def inproj_bwd_x(pieces, w3, x, vecs, dx_res, ship=None, tm=512):
    S, D = x.shape
    n = S // tm
    ex = _Exchange(ship)

    def body(*refs):
        p_refs, (w_ref, x_ref, v_ref, dr_ref), refs = refs[:8], refs[8:12], refs[12:]
        ship_refs, (dx_ref, st_ref), refs = refs[:ex.n_in], refs[ex.n_in:ex.n_in + 2], refs[ex.n_in + 2:]
        start, finish = ex.ops(ship_refs, refs)

        @pl.when(pl.program_id(0) == 0)
        def _():
            st_ref[...] = jnp.zeros_like(st_ref)
            start()

        g, scale1 = v_ref[3:4, :], 1.0 + v_ref[1:2, :]
        halves = [slice(hh * (tm // 2), (hh + 1) * (tm // 2)) for hh in range(2)]
        dhs = []
        for rows in halves:
            dh = jnp.zeros((tm // 2, D), F32)
            for k, p_ref in enumerate(p_refs):
                c0 = (k % 2) * GROUP_W
                dh = dh + _dot_nt(p_ref[rows, :], w_ref[k // 2, :, c0:c0 + GROUP_W])
            dhs.append(dh)
        for rows, dh in zip(halves, dhs):
            xv = x_ref[rows, :]
            r = lax.rsqrt(_rowmean(xv * xv) + EPS)
            xn = xv * r
            st_ref[0:1, :] += jnp.sum(dh, axis=0, keepdims=True)
            dh_xn = dh * xn
            st_ref[1:2, :] += jnp.sum(dh_xn, axis=0, keepdims=True) * g
            st_ref[2:3, :] += jnp.sum(dh_xn, axis=0, keepdims=True) * scale1
            dxn = dh * (g * scale1)
            dx_ref[rows, :] = r * (dxn - xn * _rowmean(dxn * xn)) + dr_ref[rows, :]
        pl.when(pl.program_id(0) == n - 1)(finish)

    row = lambda w: pl.BlockSpec((tm, w), lambda i: (i, 0))
    return pl.pallas_call(
        body, name="inproj_bwd_x", grid=(n,),
        in_specs=[row(GROUP_W)] * 8 + [pl.BlockSpec((N_SHARD, D, SHARD_W), lambda i: (0, 0, 0)),
                                       row(D), pl.BlockSpec((8, D), lambda i: (0, 0)), row(D)] + ex.in_specs,
        out_specs=[row(D), pl.BlockSpec((8, D), lambda i: (0, 0))] + ex.out_specs,
        out_shape=[jax.ShapeDtypeStruct((S, D), F32), jax.ShapeDtypeStruct((8, D), F32)] + ex.out_shape,
        scratch_shapes=ex.scratch,
        compiler_params=_cp("arbitrary"),
    )(*pieces, w3, x, vecs, dx_res, *ex.ship)


def inproj_bwd_w(h, pieces, tm=1024):
    S, D = h.shape
    tm = min(tm, S)
    n = S // tm

    def body(*refs):
        h_ref, p_refs, dw_ref, acc = refs[0], refs[1:9], refs[9], refs[10]
        i = pl.program_id(0)

        @pl.when(i == 0)
        def _():
            acc[...] = jnp.zeros_like(acc)

        hv = h_ref[...]
        for k, p_ref in enumerate(p_refs):
            c0 = (k % 2) * GROUP_W
            acc[k // 2, :, c0:c0 + GROUP_W] += _dot_tn(hv, p_ref[...])

        @pl.when(i == n - 1)
        def _():
            dw_ref[...] = acc[...].astype(BF16)

    row = lambda w: pl.BlockSpec((tm, w), lambda i: (i, 0))
    return pl.pallas_call(
        body, name="inproj_bwd_w", grid=(n,),
        in_specs=[row(D)] + [row(GROUP_W)] * 8,
        out_specs=pl.BlockSpec((N_SHARD, D, SHARD_W), lambda i: (0, 0, 0), pipeline_mode=pl.Buffered(1)),
        out_shape=jax.ShapeDtypeStruct((N_SHARD, D, SHARD_W), BF16),
        scratch_shapes=[pltpu.VMEM((N_SHARD, D, SHARD_W), F32)],
        compiler_params=_cp("arbitrary"),
    )(h, *pieces)


def layer_fwd(x, vecs, w3, w_out, tabs, gather=None, head=None):
    cosf, sinf, lgam = tabs
    ret, sg, h, sb = inproj_fwd(x, vecs, w3)
    y_ret, o_ret, states = ret_fwd(ret, cosf, sinf, lgam)
    y_sb, o_sb, sb_end, *gathered = sb_fwd(sb, sg, gather)
    if callable(w_out):
        w_out = w_out(gathered)
    y, *x_next = outproj_fwd(x, vecs, y_ret, y_sb, w_out, head)
    saved = (x, ret, sg, h, sb, y_ret, o_ret, states, y_sb, o_sb, sb_end, y)
    return (x_next[0] if head is None else x_next), saved, gathered


def _by_shard(dw_out):
    return dw_out.reshape(N_SHARD, D_MODEL // N_SHARD, D_MODEL)


def layer_bwd(dx, saved, vecs, w3, w_out, tabs, later_grads=None):
    cosf, sinf, lgam = tabs
    x, ret, sg, h, sb, y_ret, o_ret, states, y_sb, o_sb, sb_end, y = saved
    dycat, dw_out, st_o = outproj_bwd(dx, y, vecs, y_ret, y_sb, w_out)
    dw_out = _by_shard(dw_out)
    ship = None if later_grads is None else (later_grads[0], dw_out, later_grads[1])
    *d_sb, = sb_bwd(sb, sg, o_sb, sb_end, dycat, ship)
    d_ret = ret_bwd(ret, cosf, sinf, lgam, o_ret, states, dycat)
    pieces = list(d_ret) + d_sb[:4]
    dw_in = inproj_bwd_w(h, pieces)
    dx, st_i, *recv_in = inproj_bwd_x(pieces, w3, x, vecs, dx, None if later_grads is None else (dw_in,))
    dmod = jnp.concatenate([st_i[0:2], st_o[0:1]], axis=0)
    grads = (dw_in, dw_out) if later_grads is None else (recv_in[0], d_sb[4])
    return dx, dmod, st_i[2:3], grads


def _place():
    return lax.axis_index("x"), lax.axis_index("y"), lax.axis_index("c")


def _other_chips(mx, my):
    return [(1 - mx, my), (mx, 1 - my), (1 - mx, 1 - my)]


_ANY = pl.BlockSpec(memory_space=pl.ANY)


_GATHER_SCRATCH = [pltpu.SemaphoreType.DMA((7,)), pltpu.SemaphoreType.DMA((7,)), pltpu.SemaphoreType.DMA(())]


def _gather_ops(x_ref, out_ref, send_sems, recv_sems, local_sem):
    mx, my, mc = _place()
    me, sibling = (mx, my, mc), (mx, my, 1 - mc)
    chips = _other_chips(mx, my)

    def slot(px, py, pc):
        return out_ref.at[4 * px + 2 * py + pc]

    def copy(k, block, to, src=None):
        return pltpu.make_async_remote_copy(
            src_ref=slot(*block) if src is None else src, dst_ref=slot(*block),
            send_sem=send_sems.at[k], recv_sem=recv_sems.at[k], device_id=to, device_id_type=MESH)

    mine = pltpu.make_async_copy(x_ref, slot(*me), local_sem)
    first = [copy(0, me, sibling, src=x_ref)]
    first += [copy(1 + j, me, (*chip, mc), src=x_ref) for j, chip in enumerate(chips)]
    passed = [copy(4 + j, (*chip, mc), sibling) for j, chip in enumerate(chips)]

    def start():
        mine.start()
        for cp in first:
            cp.start()

    def forward():
        for j, chip in enumerate(chips):
            copy(1 + j, (*chip, mc), me).wait_recv()
            passed[j].start()

    def finish():
        copy(0, sibling, me).wait_recv()
        for j, chip in enumerate(chips):
            copy(4 + j, (*chip, 1 - mc), me).wait_recv()
        for cp in first + passed:
            cp.wait_send()
        mine.wait()

    return start, forward, finish


class _Exchange:
    def __init__(self, ship):
        self.ship = list(ship or ())
        self.n_in = len(self.ship)
        self.n_out = 1 if self.ship else 0
        self.rows = [a.shape[1] for a in self.ship]
        self.in_specs = [_ANY] * self.n_in
        self.out_specs = [_ANY] * self.n_out
        self.out_shape = [jax.ShapeDtypeStruct((N_SHARD, sum(self.rows), SHARD_W), BF16)] * self.n_out
        sem = pltpu.SemaphoreType.DMA
        self.scratch = [sem((3,)), sem((3,)), sem(())] * self.n_out

    def ops(self, ship_refs, tail):
        if not self.ship:
            return (lambda: None), (lambda: None)
        recv, send_sems, recv_sems, local_sem = tail
        mx, my, mc = _place()
        my_chip = 2 * mx + my
        chips = _other_chips(mx, my)

        def pieces(s):
            firsts = np.cumsum([0] + self.rows[:-1])
            return [(ref.at[s], int(r0), n) for ref, r0, n in zip(ship_refs, firsts, self.rows)]

        def start():
            for src, r0, n in pieces(my_chip):
                pltpu.make_async_copy(src, recv.at[my_chip, pl.ds(r0, n)], local_sem).start()
            for j, (px, py) in enumerate(chips):
                for src, r0, n in pieces(2 * px + py):
                    pltpu.make_async_remote_copy(
                        src_ref=src, dst_ref=recv.at[my_chip, pl.ds(r0, n)],
                        send_sem=send_sems.at[j], recv_sem=recv_sems.at[j],
                        device_id=(px, py, mc), device_id_type=MESH).start()

        def finish():
            for j, (px, py) in enumerate(chips):
                whole = recv.at[2 * px + py]
                both = pltpu.make_async_remote_copy(
                    src_ref=whole, dst_ref=whole, send_sem=send_sems.at[j], recv_sem=recv_sems.at[j],
                    device_id=(px, py, mc), device_id_type=MESH)
                both.wait_recv()
                both.wait_send()
            pltpu.make_async_copy(recv.at[my_chip], recv.at[my_chip], local_sem).wait()

        return start, finish


def sum_and_swap(recv_a, recv_b, stats, tr=256):
    n, rows_a, cols = recv_a.shape
    na, nb = rows_a // tr, recv_b.shape[1] // tr
    nt = na + nb

    def body(a_ref, b_ref, st_ref, own_ref, sib_ref, stall_ref, slots, send_sems, recv_sem, *gather_sems):
        i = pl.program_id(0)
        mx, my, mc = _place()
        slot = i % 2
        g_start, g_forward, g_finish = _gather_ops(st_ref, stall_ref, *gather_sems)
        pl.when(i == 0)(g_start)
        pl.when(i == nt // 2)(g_forward)

        def push(k, tile):
            return pltpu.make_async_remote_copy(
                src_ref=slots.at[k], dst_ref=sib_ref.at[pl.ds(pl.multiple_of(tile * tr, tr), tr)],
                send_sem=send_sems.at[k], recv_sem=recv_sem, device_id=(mx, my, 1 - mc), device_id_type=MESH)

        pl.when(i >= 2)(lambda: push(slot, i - 2).wait_send())

        def total(r_ref):
            acc = r_ref[0].astype(F32)
            for k in range(1, n):
                acc = acc + r_ref[k].astype(F32)
            own_ref[...] = acc
            slots[slot] = acc

        pl.when(i < na)(lambda: total(a_ref))
        pl.when(i >= na)(lambda: total(b_ref))
        push(slot, i).start()

        @pl.when(i == nt - 1)
        def _():
            push(1 - slot, i - 1).wait_send()
            push(slot, i).wait_send()
            pltpu.make_async_remote_copy(src_ref=sib_ref, dst_ref=sib_ref, send_sem=send_sems.at[0], recv_sem=recv_sem,
                                         device_id=(mx, my, 1 - mc), device_id_type=MESH).wait_recv()
            g_finish()

    return pl.pallas_call(
        body, name="sum_and_swap", grid=(nt,),
        in_specs=[pl.BlockSpec((n, tr, cols), lambda i: (0, jnp.minimum(i, na - 1), 0)),
                  pl.BlockSpec((n, tr, cols), lambda i: (0, jnp.maximum(i - na, 0), 0)), _ANY],
        out_specs=[pl.BlockSpec((tr, cols), lambda i: (i, 0)), _ANY, _ANY],
        out_shape=[jax.ShapeDtypeStruct((nt * tr, cols), F32)] * 2
        + [jax.ShapeDtypeStruct((8,) + stats.shape, stats.dtype)],
        scratch_shapes=[pltpu.VMEM((2, tr, cols), F32), pltpu.SemaphoreType.DMA((2,)), pltpu.SemaphoreType.DMA(())]
        + _GATHER_SCRATCH,
        compiler_params=_cp("arbitrary"),
    )(recv_a, recv_b, stats)


def _adamw(w, g, m, v):
    m = ADAM_B1 * m + (1.0 - ADAM_B1) * g
    v = ADAM_B2 * v + (1.0 - ADAM_B2) * (g * g)
    m_hat = m / (1.0 - ADAM_B1 ** ADAM_STEP)
    v_hat = v / (1.0 - ADAM_B2 ** ADAM_STEP)
    delta = -ADAM_LR * (m_hat / (jnp.sqrt(v_hat) + ADAM_EPS) + ADAM_WD * w)
    return delta, m, v


def adam_slab(p_own, p_sib, w, m, v, row0, name, tr=512):
    L, R, C = w.shape
    nr = R // tr

    def body(a_ref, b_ref, w_ref, m_ref, v_ref, g_out, d_out, m_out, v_out):
        g = a_ref[...] + b_ref[...]
        d, m2, v2 = _adamw(w_ref[0], g, m_ref[0], v_ref[0])
        g_out[0], d_out[0], m_out[0], v_out[0] = g, d, m2, v2

    slab = pl.BlockSpec((tr, C), lambda l, i: (row0 // tr + l * nr + i, 0))
    blk = pl.BlockSpec((1, tr, C), lambda l, i: (l, i, 0))
    return pl.pallas_call(
        body, name=name, grid=(L, nr),
        in_specs=[slab, slab, blk, blk, blk], out_specs=[blk] * 4,
        out_shape=[jax.ShapeDtypeStruct(w.shape, F32)] * 4,
        compiler_params=_cp("arbitrary", "arbitrary"),
    )(p_own, p_sib, w, m, v)


def prologue(c8, w_ada, b_ada, norm_g, win_first):
    L, D, W = w_ada.shape

    def body(c_ref, w_ref, b_ref, g_ref, win_ref, vecs_ref, call_ref, wall_ref, mod_ref, mall_ref, *sems):
        w_start, w_forward, w_finish = _gather_ops(win_ref, wall_ref, *sems[0:3])
        for step in _gather_ops(c_ref, call_ref, *sems[3:6]):
            step()
        w_start()
        cv = call_ref[:, 0, :]
        ca = cv * _sigmoid(cv)
        for l in range(L):
            mod_ref[l * 8:(l + 1) * 8, :] = jnp.dot(ca, w_ref[l], precision=lax.Precision.HIGHEST,
                                                    preferred_element_type=F32)
        for step in _gather_ops(mod_ref, mall_ref, *sems[6:9]):
            step()
        mx, my, mc = _place()
        me = 4 * mx + 2 * my + mc
        rowid = lax.broadcasted_iota(jnp.int32, (L * 8, 1), 0)
        vecs_ref[...] = jnp.zeros_like(vecs_ref)
        for l in range(L):
            parts = [jnp.sum(jnp.where(rowid == l * 8 + me, mall_ref[2 * s + mc], 0.0), axis=0, keepdims=True)
                     for s in range(N_SHARD)]
            mod = jnp.concatenate(parts, axis=1) + b_ref[l:l + 1, :]
            for t in range(3):
                vecs_ref[l, t:t + 1, :] = mod[:, t * D:(t + 1) * D]
            vecs_ref[l, 3:4, :] = g_ref[l:l + 1, :]
        w_forward()
        w_finish()

    vmem = pl.BlockSpec(memory_space=pltpu.VMEM)
    return pl.pallas_call(
        body, name="prologue",
        in_specs=[vmem, vmem, vmem, vmem, _ANY], out_specs=[vmem, vmem, _ANY],
        out_shape=[jax.ShapeDtypeStruct((L, 8, D), F32), jax.ShapeDtypeStruct((8, 8, D), F32),
                   jax.ShapeDtypeStruct((8,) + win_first.shape, win_first.dtype)],
        scratch_shapes=[pltpu.VMEM((L * 8, W), F32), pltpu.VMEM((8, L * 8, W), F32)] + _GATHER_SCRATCH * 3,
        compiler_params=pltpu.CompilerParams(vmem_limit_bytes=VMEM_LIMIT_BYTES),
    )(c8, w_ada, b_ada, norm_g, win_first)


def ada_update(dmods, c_t, w, m, v, tr=512):
    L, D, W = w.shape

    def body(dm_ref, c_ref, w_ref, m_ref, v_ref, g_out, d_out, m_out, v_out):
        mx, my, _ = _place()
        shard = 2 * mx + my
        dm = jnp.zeros((8, W), F32)
        for s in range(N_SHARD):
            dm = dm + jnp.where(shard == s, dm_ref[0, :, s * W:(s + 1) * W], 0.0)
        cv = c_ref[...]
        ca = cv * _sigmoid(cv)
        g = jnp.zeros((tr, W), F32)
        for b in range(8):
            g = g + ca[:, b:b + 1] * dm[b:b + 1, :]
        d, m2, v2 = _adamw(w_ref[0], g, m_ref[0], v_ref[0])
        g_out[0], d_out[0], m_out[0], v_out[0] = g, d, m2, v2

    blk = pl.BlockSpec((1, tr, W), lambda l, i: (l, i, 0))
    return pl.pallas_call(
        body, name="ada_update", grid=(L, D // tr),
        in_specs=[pl.BlockSpec((1, 8, 3 * D), lambda l, i: (l, 0, 0)), pl.BlockSpec((tr, 8), lambda l, i: (i, 0)),
                  blk, blk, blk],
        out_specs=[blk] * 4, out_shape=[jax.ShapeDtypeStruct(w.shape, F32)] * 4,
        compiler_params=_cp("arbitrary", "arbitrary"),
    )(dmods, c_t, w, m, v)


STAT_ROWS = 16


def small_update(stats_all, norm, b_ada, final):
    def body(s_ref, *refs):
        ins, outs = refs[:9], refs[9:]
        tot = s_ref[0]
        for k in range(1, 8):
            tot = tot + s_ref[k]
        g_norm = tot[0:2, :]
        g_final = tot[2:3, :]
        g_b = jnp.concatenate(
            [jnp.concatenate([tot[3 + 3 * l + t:4 + 3 * l + t, :] for t in range(3)], axis=1) for l in range(DEPTH)],
            axis=0)
        for p, g in enumerate((g_norm, g_b, g_final)):
            w_ref, m_ref, v_ref = ins[3 * p:3 * p + 3]
            d, m2, v2 = _adamw(w_ref[...], g, m_ref[...], v_ref[...])
            for o_ref, val in zip(outs[4 * p:4 * p + 4], (g, d, m2, v2)):
                o_ref[...] = val
        loss = (0.5 / D_MODEL) * jnp.sum(tot[9:10, :], axis=1, keepdims=True)
        outs[12][...] = jnp.broadcast_to(loss, (8, LANES))

    shapes = []
    for w, _, _ in (norm, b_ada, final):
        shapes += [jax.ShapeDtypeStruct(w.shape, F32)] * 4
    shapes.append(jax.ShapeDtypeStruct((8, LANES), F32))
    return pl.pallas_call(body, name="small_update", out_shape=shapes)(stats_all, *norm, *b_ada, *final)


def kernel(x, c, norm_g, w_ada, b_ada, w_in, w_out, final_g, loss_target, m_norm_g, m_w_ada, m_b_ada, m_w_in, m_w_out, m_final_g, v_norm_g, v_w_ada, v_b_ada, v_w_in, v_w_out, v_final_g):
    S, D = x.shape[1], x.shape[2]
    mc = lax.axis_index("c")
    out_rows = D // N_SHARD

    def my_half(a, rows):
        return lax.dynamic_slice_in_dim(a, mc * rows, rows, axis=0)

    assert DEPTH == 2
    win = [my_half(w_in[l], D // 2).astype(BF16) for l in range(DEPTH)]
    wout = [my_half(w_out[l], out_rows // 2).astype(BF16) for l in range(DEPTH)]
    rest = [jnp.concatenate(wout, axis=0), win[1]]

    def unpack(gathered):
        outs, w3_second = gathered
        outs = outs.reshape(N_SHARD, 2, DEPTH, out_rows // 2, SHARD_W)
        return outs[:, :, 0].reshape(D, D), (w3_second.reshape(N_SHARD, D, SHARD_W), outs[:, :, 1].reshape(D, D))

    vecs, c_all, w3_first = prologue(jnp.broadcast_to(c, (8, D)), w_ada, b_ada, norm_g, win[0])
    c_all, w3_first = c_all[:, 0, :], w3_first.reshape(N_SHARD, D, SHARD_W)

    tabs = (*rope_tables(S), ret_log_gamma())
    saved = [None] * DEPTH
    h, saved[0], wall = layer_fwd(x[0], vecs[0], w3_first, lambda g: unpack(g)[0], tabs, rest)
    weights = [(w3_first, unpack(wall)[0]), unpack(wall)[1]]
    head = (jnp.broadcast_to(final_g[None, :], (8, D)), loss_target[0])
    (dx, st_loss), saved[1], _ = layer_fwd(h, vecs[1], *weights[1], tabs, head=head)

    dmod, dnorm, grads = [None] * DEPTH, [None] * DEPTH, None
    for l in reversed(range(DEPTH)):
        dx, dmod[l], dnorm[l], grads = layer_bwd(dx, saved[l], vecs[l], *weights[l], tabs, grads)

    stats = jnp.concatenate(dnorm + [st_loss[0:1]] + dmod + [st_loss[1:2], jnp.zeros((STAT_ROWS - 10, D), F32)], axis=0)
    p_own, p_sib, stats_all = sum_and_swap(*grads, stats)
    res_in = adam_slab(p_own, p_sib, w_in, m_w_in, v_w_in, 0, "adam_w_in")
    res_out = adam_slab(p_own, p_sib, w_out, m_w_out, v_w_out, DEPTH * D, "adam_w_out", tr=256)

    dmods = stats_all[:, 3:9, :].reshape(8, DEPTH, 3 * D).transpose(1, 0, 2)
    res_ada = ada_update(dmods, c_all.T, w_ada, m_w_ada, v_w_ada)
    small = small_update(stats_all, (norm_g, m_norm_g, v_norm_g), (b_ada, m_b_ada, v_b_ada),
                         (final_g[None, :], m_final_g[None, :], v_final_g[None, :]))
    res_norm, res_b, res_final = small[0:4], small[4:8], [a[0] for a in small[8:12]]
    loss = small[12][0, 0]

    by_kind = [res_norm, res_ada, res_b, res_in, res_out, res_final]
    outs = [loss, dx[None]]
    for kind in range(4):
        outs += [r[kind] for r in by_kind]
    return tuple(outs)
```

```python
import numpy as np
import jax
import jax.numpy as jnp
from jax import lax
from jax.experimental import pallas as pl
from jax.experimental.pallas import tpu as pltpu

F32, BF16 = jnp.float32, jnp.bfloat16
MESH = pl.DeviceIdType.MESH

D_MODEL = 1024
DEPTH = 2
SHARD_W = 1024
N_SHARD = 4
GROUP_W = 512
LANES = 128
SB_HEAD_DIM = 64
RET_HEAD_DIM = 128
CHUNK = 64
ROPE_BASE = 10000.0
EPS = 1e-6
SQ_SCALE = SB_HEAD_DIM ** -0.5
RK_SCALE = RET_HEAD_DIM ** -0.5
SB_T = 1024
SB_CHAINS = 16
SB_NB = 4
RET_T = 256
EXP_ZERO = -104.0
MASKED_SCORE = -1e4
VMEM_LIMIT_BYTES = 56 * 2 ** 20

ADAM_LR, ADAM_B1, ADAM_B2, ADAM_EPS, ADAM_WD, ADAM_STEP = 0.001, 0.9, 0.999, 1e-08, 0.01, 10


def _cp(*sem):
    return pltpu.CompilerParams(dimension_semantics=sem, vmem_limit_bytes=VMEM_LIMIT_BYTES)


def _dot(a, b):
    return lax.dot_general(a, b, (((1,), (0,)), ((), ())), preferred_element_type=F32)


def _dot_nt(a, b):
    return lax.dot_general(a, b, (((1,), (1,)), ((), ())), preferred_element_type=F32)


def _dot_tn(a, b):
    return lax.dot_general(a, b, (((0,), (0,)), ((), ())), preferred_element_type=F32)


def _running_sum(a, tri):
    return _dot(a.astype(BF16), tri)


def _sigmoid(x):
    return 1.0 / (1.0 + jnp.exp(-x))


def _rowsum(a):
    return jnp.sum(a, axis=1, keepdims=True)


def _rowmean(a):
    return jnp.mean(a, axis=1, keepdims=True)


def inproj_fwd(x, vecs, w3, tm=512):
    S, D = x.shape

    def body(x_ref, v_ref, w_ref, ret_ref, sg_ref, h_ref, sb_ref):
        xv = x_ref[...]
        r = lax.rsqrt(_rowmean(xv * xv) + EPS)
        h = xv * r * v_ref[3:4, :] * (1.0 + v_ref[1:2, :]) + v_ref[0:1, :]
        hb = h.astype(BF16)
        h_ref[...] = hb
        for s in range(N_SHARD):
            p = _dot(hb, w_ref[s])
            if s < 2:
                ret_ref[:, s * SHARD_W:(s + 1) * SHARD_W] = p
            if s == 2:
                sb_ref[:, 0:GROUP_W] = (p[:, 0:GROUP_W] * SQ_SCALE).astype(BF16)
                sb_ref[:, GROUP_W:SHARD_W] = p[:, GROUP_W:].astype(BF16)
            if s == 3:
                sb_ref[:, SHARD_W:SHARD_W + GROUP_W] = p[:, 0:GROUP_W].astype(BF16)
                sg_ref[...] = p[:, GROUP_W:]

    row = lambda w: pl.BlockSpec((tm, w), lambda i: (i, 0))
    return pl.pallas_call(
        body, name="inproj_fwd", grid=(S // tm,),
        in_specs=[row(D), pl.BlockSpec((8, D), lambda i: (0, 0)),
                  pl.BlockSpec((N_SHARD, D, SHARD_W), lambda i: (0, 0, 0))],
        out_specs=[row(2 * SHARD_W), row(GROUP_W), row(D), row(3 * GROUP_W)],
        out_shape=[jax.ShapeDtypeStruct((S, 2 * SHARD_W), F32), jax.ShapeDtypeStruct((S, GROUP_W), F32),
                   jax.ShapeDtypeStruct((S, D), BF16), jax.ShapeDtypeStruct((S, 3 * GROUP_W), BF16)],
        compiler_params=_cp("arbitrary"),
    )(x, vecs, w3)


def _sb_logits(qh, k2, keep):
    z = _dot_nt(qh, k2)
    if keep is not None:
        z = jnp.where(keep, z, MASKED_SCORE)
    sp = jnp.log(1.0 + jnp.exp(-jnp.abs(z)))
    lb = jnp.minimum(z, 0.0) - sp
    return lb, lb - z


class _sb_chains:
    def __init__(self, i, q2, do_b=None):
        t = self.t = SB_T // SB_CHAINS
        self.C = range(SB_CHAINS)
        r = lax.broadcasted_iota(jnp.int32, (SB_NB * t, SB_NB * t), 0)
        c = lax.broadcasted_iota(jnp.int32, (SB_NB * t, SB_NB * t), 1)
        self.later_all = jnp.where(r > c, 1.0, 0.0).astype(BF16)
        self.earlier_all = jnp.where(r < c, 1.0, 0.0).astype(BF16)
        self.later, self.earlier = self.later_all[:t, :t], self.earlier_all[:t, :t]
        self.head0 = lax.broadcasted_iota(jnp.int32, (1, LANES), 1) < SB_HEAD_DIM
        row = lax.broadcasted_iota(jnp.int32, (2 * t, SB_NB * t), 0) & (t - 1)
        col = lax.broadcasted_iota(jnp.int32, (2 * t, SB_NB * t), 1)
        qt = [SB_CHAINS * i + cc for cc in self.C]
        self.first = [jnp.maximum(qt[cc] - (SB_NB - 1), 0) for cc in self.C]
        self._ahead, self._qt = col - row, qt
        self.qs = [self._stack(q2[cc * t:(cc + 1) * t]) for cc in self.C]
        if do_b is not None:
            self.dos = [self._stack(do_b[cc * t:(cc + 1) * t]) for cc in self.C]

    def keep_of(self, cc):
        return self._ahead < (self._qt[cc] - self.first[cc]) * self.t

    def _stack(self, a):
        zero = jnp.zeros_like(a)
        return jnp.concatenate([jnp.where(self.head0, a, zero), jnp.where(self.head0, zero, a)], axis=0)

    def rows(self, ref, j, n):
        return ref[pl.ds(pl.multiple_of(j * self.t, self.t), n * self.t), :]

    def suffix(self, lk):
        return _running_sum(lk, self.later_all), _rowsum(lk)

    def prefix(self, g, G0):
        return _running_sum(g, self.earlier_all) + G0


def sb_fwd(sb, sg, gather=None):
    S = sb.shape[0]
    T = SB_T
    nq = S // T
    carried = list(gather or ())
    ng = len(carried)

    def body(*refs):
        (q_ref, k_ref, v_ref, sg_ref), refs = refs[:4], refs[4:]
        x_refs, (y_ref, o_ref, end_ref), out_refs, sems = refs[:ng], refs[ng:ng + 3], refs[ng + 3:2 * ng + 3], refs[2 * ng + 3:]
        p, i = pl.program_id(0), pl.program_id(1)
        gathers = [_gather_ops(x_refs[g], out_refs[g], *sems[3 * g:3 * g + 3]) for g in range(ng)]
        for start, forward, _ in gathers:
            pl.when(jnp.logical_and(p == 0, i == 0))(start)
            pl.when(jnp.logical_and(p == 3, i == 0))(forward)
        ch = _sb_chains(i, q_ref[...])
        later, head0 = ch.later, ch.head0
        lbk = [_sb_logits(ch.qs[c], ch.rows(k_ref, ch.first[c], SB_NB), ch.keep_of(c)) for c in ch.C]
        suffix, R = zip(*[ch.suffix(lbk[c][1]) for c in ch.C])
        aa = [jnp.exp(lbk[c][0] + suffix[c]) for c in ch.C]
        acc = [_dot(aa[c].astype(BF16), ch.rows(v_ref, ch.first[c], SB_NB)) for c in ch.C]

        nc = len(ch.C)

        def alive(n, Rs):
            m = None
            for c in ch.C:
                rc = jnp.where(ch.first[c] - n > 0, Rs[c], EXP_ZERO)
                m = rc if m is None else jnp.maximum(m, rc)
            return jnp.max(m)

        def cond(st):
            return st[-1] > EXP_ZERO

        def step(st):
            n, accs, Rs = st[0], list(st[1:1 + nc]), list(st[1 + nc:1 + 2 * nc])
            for c in ch.C:
                j = ch.first[c] - 1 - n
                jc = jnp.maximum(j, 0)
                lb, lk = _sb_logits(ch.qs[c], ch.rows(k_ref, jc, 1), None)
                a = jnp.exp(lb + _running_sum(lk, later) + Rs[c])
                cx = _dot(a.astype(BF16), ch.rows(v_ref, jc, 1))
                accs[c] = jnp.where(j >= 0, accs[c] + cx, accs[c])
                Rs[c] = jnp.where(j >= 0, Rs[c] + _rowsum(lk), Rs[c])
            return (n + 1, *accs, *Rs, alive(n + 1, Rs))

        st = lax.while_loop(cond, step, (jnp.int32(0), *acc, *R, alive(0, R)))
        n_end, acc, R = st[0], st[1:1 + nc], st[1 + nc:1 + 2 * nc]
        outs = []
        for c in ch.C:
            base = c * (2 * ch.t + 8)
            end_ref[0, 0, base:base + 2 * ch.t, :] = jnp.broadcast_to(R[c], (2 * ch.t, 8))
            end_ref[0, 0, base + 2 * ch.t:base + 2 * ch.t + 8, :] = jnp.full((8, 8), n_end.astype(F32))
            outs.append(jnp.where(head0, acc[c][:ch.t], acc[c][ch.t:]))
        o = jnp.concatenate(outs, axis=0)
        o_ref[...] = o
        sg = sg_ref[...]
        y_ref[...] = (o * (sg * _sigmoid(sg))).astype(BF16)
        for _, _, finish in gathers:
            pl.when(jnp.logical_and(p == 3, i == nq - 1))(finish)

    return pl.pallas_call(
        body, name="sb_fwd", grid=(4, nq),
        in_specs=[pl.BlockSpec((T, LANES), lambda p, i: (i, p)),
                  pl.BlockSpec((S, LANES), lambda p, i: (0, 4 + p)),
                  pl.BlockSpec((S, LANES), lambda p, i: (0, 8 + p)),
                  pl.BlockSpec((T, LANES), lambda p, i: (i, p))] + [_ANY for _ in carried],
        out_specs=[pl.BlockSpec((T, LANES), lambda p, i: (i, p)),
                   pl.BlockSpec((T, LANES), lambda p, i: (i, p)),
                   pl.BlockSpec((1, 1, SB_CHAINS * (2 * T // SB_CHAINS + 8), 8), lambda p, i: (p, i, 0, 0))] + [_ANY for _ in carried],
        out_shape=[jax.ShapeDtypeStruct((S, GROUP_W), BF16),
                   jax.ShapeDtypeStruct((S, GROUP_W), F32),
                   jax.ShapeDtypeStruct((4, nq, SB_CHAINS * (2 * T // SB_CHAINS + 8), 8), F32)]
        + [jax.ShapeDtypeStruct((8,) + a.shape, a.dtype) for a in carried],
        scratch_shapes=_GATHER_SCRATCH * ng,
        compiler_params=_cp("arbitrary", "arbitrary"),
    )(sb, sb, sb, sg, *carried)


def sb_bwd(sb, sg, o, sb_end, dycat, ship=None):
    S = sb.shape[0]
    T = SB_T
    nq = S // T
    ex = _Exchange(ship)

    def body(*refs):
        (q_ref, k_ref, v_ref, sg_ref, o_ref, dy_ref, end_ref), refs = refs[:7], refs[7:]
        ship_refs, (dq_ref, dk_ref, dv_ref, dsg_ref), refs = refs[:ex.n_in], refs[ex.n_in:ex.n_in + 4], refs[ex.n_in + 4:]
        recv, (dk_acc, dv_acc), sems = refs[:ex.n_out], refs[ex.n_out:ex.n_out + 2], refs[ex.n_out + 2:]
        start, finish = ex.ops(ship_refs, recv + sems)
        p, i = pl.program_id(0), pl.program_id(1)
        pl.when(jnp.logical_and(p == 0, i == 0))(start)

        @pl.when(i == 0)
        def _():
            dk_acc[...] = jnp.zeros_like(dk_acc)
            dv_acc[...] = jnp.zeros_like(dv_acc)

        sg = sg_ref[...]
        sig = _sigmoid(sg)
        dy = dy_ref[...]
        dsg_ref[...] = (dy * o_ref[...] * (sig * (1.0 + sg * (1.0 - sig)))).astype(BF16)
        do_b = (dy * (sg * sig)).astype(BF16)
        ch = _sb_chains(i, q_ref[...], do_b)
        later, earlier, head0, t = ch.later, ch.earlier, ch.head0, ch.t
        end = end_ref[0, 0]

        def grads(c, j, n, a, lb, g, G, keep):
            dz = g - jnp.exp(lb) * (g + G)
            if keep is not None:
                dz = jnp.where(keep, dz, 0.0)
            dzb = dz.astype(BF16)
            rows = pl.ds(pl.multiple_of(j * t, t), n * t)
            dk_acc[rows, :] += _dot_tn(dzb, ch.qs[c])
            dv_acc[rows, :] += _dot_tn(a.astype(BF16), ch.dos[c])
            return _dot(dzb, ch.rows(k_ref, j, n))

        nc = len(ch.C)
        n_end = jnp.max(end[2 * t:2 * t + 8, :]).astype(jnp.int32)

        def sweep(m, st):
            dqs, G0s, lefts = list(st[:nc]), list(st[nc:2 * nc]), list(st[2 * nc:])
            for c in ch.C:
                j = ch.first[c] - n_end + m
                jc = jnp.maximum(j, 0)
                lb, lk = _sb_logits(ch.qs[c], ch.rows(k_ref, jc, 1), None)
                stick = lefts[c] - _rowsum(lk)
                a = jnp.where(j >= 0, jnp.exp(lb + _running_sum(lk, later) + stick), 0.0)
                g = a * _dot_nt(ch.dos[c], ch.rows(v_ref, jc, 1))
                G = _running_sum(g, earlier) + G0s[c]
                dqs[c] = dqs[c] + grads(c, jc, 1, a, lb, jnp.where(j >= 0, g, 0.0), jnp.where(j >= 0, G, 0.0), None)
                G0s[c] = G0s[c] + _rowsum(g)
                lefts[c] = jnp.where(j >= 0, stick, lefts[c])
            return (*dqs, *G0s, *lefts)

        lefts = [end[c * (2 * t + 8):c * (2 * t + 8) + 2 * t, 0:1] for c in ch.C]
        st = lax.fori_loop(0, n_end, sweep, (*[jnp.zeros((2 * t, LANES), F32)] * nc,
                                             *[jnp.zeros((2 * t, 1), F32)] * nc, *lefts))
        dq, G0 = st[:nc], st[nc:2 * nc]

        lbk = [_sb_logits(ch.qs[c], ch.rows(k_ref, ch.first[c], SB_NB), ch.keep_of(c)) for c in ch.C]
        suffix = [ch.suffix(lbk[c][1])[0] for c in ch.C]
        aa = [jnp.exp(lbk[c][0] + suffix[c]) for c in ch.C]
        g = [aa[c] * _dot_nt(ch.dos[c], ch.rows(v_ref, ch.first[c], SB_NB)) for c in ch.C]
        G = [ch.prefix(g[c], G0[c]) for c in ch.C]
        for c in ch.C:
            dqc = dq[c] + grads(c, ch.first[c], SB_NB, aa[c], lbk[c][0], g[c], G[c], None)
            dq_ref[c * t:(c + 1) * t, :] = (jnp.where(head0, dqc[:t], dqc[t:]) * SQ_SCALE).astype(BF16)

        @pl.when(i == nq - 1)
        def _():
            dk_ref[...] = dk_acc[...].astype(BF16)
            dv_ref[...] = dv_acc[...].astype(BF16)

        pl.when(jnp.logical_and(p == 3, i == nq - 1))(finish)

    tile_spec = lambda c0: pl.BlockSpec((T, LANES), lambda p, i: (i, c0 + p))
    head_spec = lambda c0: pl.BlockSpec((S, LANES), lambda p, i: (0, c0 + p))
    return pl.pallas_call(
        body, name="sb_bwd", grid=(4, nq),
        in_specs=[tile_spec(0), head_spec(4), head_spec(8), tile_spec(0), tile_spec(0), tile_spec(4),
                  pl.BlockSpec((1, 1, SB_CHAINS * (2 * T // SB_CHAINS + 8), 8), lambda p, i: (p, i, 0, 0))] + ex.in_specs,
        out_specs=[tile_spec(0), head_spec(0), head_spec(0), tile_spec(0)] + ex.out_specs,
        out_shape=[jax.ShapeDtypeStruct((S, GROUP_W), BF16)] * 4 + ex.out_shape,
        scratch_shapes=[pltpu.VMEM((S, LANES), F32), pltpu.VMEM((S, LANES), F32)] + ex.scratch,
        compiler_params=_cp("arbitrary", "arbitrary"),
    )(sb, sb, sb, sg, o, dycat, sb_end, *ex.ship)


def rope_tables(S):
    half = RET_HEAD_DIM // 2
    lane = jnp.arange(RET_HEAD_DIM)
    inv = ROPE_BASE ** (-(lane % half).astype(F32) / half)
    ang = jnp.arange(S, dtype=F32)[:, None] * inv[None, :]
    return jnp.cos(ang), jnp.where(lane < half, -1.0, 1.0)[None, :] * jnp.sin(ang)


def ret_log_gamma():
    return jnp.log1p(-(2.0 ** (-5.0 - jnp.arange(4, dtype=F32))))


def _swap_halves(a):
    return pltpu.roll(a, RET_HEAD_DIM // 2, axis=1)


def _ret_decay_mask(lg):
    n = lax.broadcasted_iota(jnp.int32, (RET_T, RET_T), 0)
    m = lax.broadcasted_iota(jnp.int32, (RET_T, RET_T), 1)
    dist = jnp.abs(n - m).astype(F32)
    return jnp.where((m // CHUNK) <= (n // CHUNK), jnp.exp(lg * dist), 0.0)


def _ret_block(lg, rq, rk, rv, cosf, sinf, dm):
    q = rq * cosf + _swap_halves(rq) * sinf
    k = (rk * cosf + _swap_halves(rk) * sinf) * RK_SCALE
    qb, kb, vb = q.astype(BF16), k.astype(BF16), rv.astype(BF16)
    sc = _dot_nt(qb, kb) * dm
    nloc = lax.broadcasted_iota(jnp.int32, (RET_T, 1), 0).astype(F32)
    qdec = jnp.exp(lg * (nloc + 1.0))
    kdec = jnp.exp(lg * (RET_T - 1.0 - nloc))
    block_dec = jnp.exp(jnp.full((1, LANES), lg * RET_T, F32))
    return q, k, qb, kb, vb, sc, qdec, kdec, block_dec


RET_RB = 4


def _ret_specs(S, rb):
    group = lambda c0: pl.BlockSpec((RET_RB * RET_T, GROUP_W), lambda s: (rb(s), c0))
    return group, pl.BlockSpec((RET_RB * RET_T, LANES), lambda s: (rb(s), 0))


def _ret_chains():
    chains = [(h, b) for b in range(RET_RB) for h in range(4)]
    rows = lambda c: (slice(c[1] * RET_T, (c[1] + 1) * RET_T), slice(c[0] * LANES, (c[0] + 1) * LANES))
    tab = lambda ref, c: ref[c[1] * RET_T:(c[1] + 1) * RET_T, :]
    return chains, rows, tab


def _ret_blocks(chains, rows, lg_ref, rq_ref, rk_ref, rv_ref, cosf, sinf, dm_ref):
    blk = {c: _ret_block(lg_ref[c[0]], rq_ref[rows(c)], rk_ref[rows(c)], rv_ref[rows(c)],
                         cosf[c], sinf[c], dm_ref[c[0]]) for c in chains}
    return ({c: blk[c][n] for c in chains} for n in range(9))


def ret_fwd(proj, cosf, sinf, lgam):
    S = proj.shape[0]
    nb = S // RET_T
    group, row_tab = _ret_specs(S, lambda s: s)

    def body(lg_ref, rq_ref, rk_ref, rv_ref, rg_ref, cos_ref, sin_ref, y_ref, o_ref, st_out, st_ref, dm_ref):
        @pl.when(pl.program_id(0) == 0)
        def _():
            st_ref[...] = jnp.zeros_like(st_ref)
            for h in range(4):
                dm_ref[h] = _ret_decay_mask(lg_ref[h])

        chains, rows, tab = _ret_chains()
        cosf, sinf = {c: tab(cos_ref, c) for c in chains}, {c: tab(sin_ref, c) for c in chains}
        q, k, qb, kb, vb, sc, qdec, kdec, block_dec = _ret_blocks(
            chains, rows, lg_ref, rq_ref, rk_ref, rv_ref, cosf, sinf, dm_ref)
        kv = {c: _dot_tn((k[c] * kdec[c]).astype(BF16), vb[c]) for c in chains}
        st = {(h, 0): st_ref[h] for h in range(4)}
        for b in range(RET_RB):
            for h in range(4):
                st[(h, b + 1)] = st[(h, b)] * block_dec[(h, b)] + kv[(h, b)]
        for h, b in chains:
            st_out[h, b] = st[(h, b)]
        for h in range(4):
            st_ref[h] = st[(h, RET_RB)]
        o = {c: _dot(sc[c].astype(BF16), vb[c]) + _dot(qb[c], st[c].astype(BF16)) * qdec[c] for c in chains}
        for c in chains:
            o_ref[rows(c)] = o[c]
        cen = {c: o[c] - _rowmean(o[c]) for c in chains}
        on = {c: cen[c] * lax.rsqrt(_rowmean(cen[c] * cen[c]) + EPS) for c in chains}
        rg = {c: rg_ref[rows(c)] for c in chains}
        for c in chains:
            y_ref[rows(c)] = (on[c] * (rg[c] * _sigmoid(rg[c]))).astype(BF16)

    return pl.pallas_call(
        body, name="ret_fwd", grid=(nb // RET_RB,),
        in_specs=[pl.BlockSpec(memory_space=pltpu.SMEM),
                  group(0), group(1), group(2), group(3), row_tab, row_tab],
        out_specs=[group(0), group(0),
                   pl.BlockSpec((4, RET_RB, LANES, LANES), lambda s: (0, s, 0, 0))],
        out_shape=[jax.ShapeDtypeStruct((S, GROUP_W), BF16),
                   jax.ShapeDtypeStruct((S, GROUP_W), F32),
                   jax.ShapeDtypeStruct((4, nb, LANES, LANES), F32)],
        scratch_shapes=[pltpu.VMEM((4, LANES, LANES), F32), pltpu.VMEM((4, RET_T, RET_T), F32)],
        compiler_params=_cp("arbitrary"),
    )(lgam, proj, proj, proj, proj, cosf, sinf)


def ret_bwd(proj, cosf, sinf, lgam, o, states, dycat):
    S = proj.shape[0]
    nsteps = S // RET_T // RET_RB
    rev = lambda s: nsteps - 1 - s
    group, row_tab = _ret_specs(S, rev)

    def body(lg_ref, rq_ref, rk_ref, rv_ref, rg_ref, cos_ref, sin_ref, o_ref, st_in, dy_ref,
             drq_ref, drk_ref, drv_ref, drg_ref, ds_ref, dm_ref):
        @pl.when(pl.program_id(0) == 0)
        def _():
            ds_ref[...] = jnp.zeros_like(ds_ref)
            for h in range(4):
                dm_ref[h] = _ret_decay_mask(lg_ref[h])

        chains, rows, tab = _ret_chains()
        cosf, sinf = {c: tab(cos_ref, c) for c in chains}, {c: tab(sin_ref, c) for c in chains}
        dms = {c: dm_ref[c[0]] for c in chains}
        q, k, qb, kb, vb, sc, qdec, kdec, block_dec = _ret_blocks(
            chains, rows, lg_ref, rq_ref, rk_ref, rv_ref, cosf, sinf, dm_ref)
        o_v = {c: o_ref[rows(c)] for c in chains}
        cen = {c: o_v[c] - _rowmean(o_v[c]) for c in chains}
        rstd = {c: lax.rsqrt(_rowmean(cen[c] * cen[c]) + EPS) for c in chains}
        on = {c: cen[c] * rstd[c] for c in chains}
        rg = {c: rg_ref[rows(c)] for c in chains}
        sig = {c: _sigmoid(rg[c]) for c in chains}
        dy = {c: dy_ref[rows(c)] for c in chains}
        for c in chains:
            drg_ref[rows(c)] = (dy[c] * on[c] * (sig[c] * (1.0 + rg[c] * (1.0 - sig[c])))).astype(BF16)
        don = {c: dy[c] * (rg[c] * sig[c]) for c in chains}
        do = {c: rstd[c] * (don[c] - _rowmean(don[c]) - on[c] * _rowmean(don[c] * on[c])) for c in chains}
        dob = {c: do[c].astype(BF16) for c in chains}
        dsc = {c: (_dot_nt(dob[c], vb[c]) * dms[c]).astype(BF16) for c in chains}
        st_b = {c: st_in[c[0], c[1]].astype(BF16) for c in chains}
        dst = {c: _dot_tn((q[c] * qdec[c]).astype(BF16), dob[c]) for c in chains}
        dsn = {(h, RET_RB): ds_ref[h] for h in range(4)}
        for b in reversed(range(RET_RB)):
            for h in range(4):
                dsn[(h, b)] = dsn[(h, b + 1)] * block_dec[(h, b)] + dst[(h, b)]
        for h in range(4):
            ds_ref[h] = dsn[(h, 0)]
        dsn_b = {c: dsn[(c[0], c[1] + 1)].astype(BF16) for c in chains}
        dq = {c: _dot(dsc[c], kb[c]) + _dot_nt(dob[c], st_b[c]) * qdec[c] for c in chains}
        dk = {c: (_dot_tn(dsc[c], qb[c]) + _dot_nt(vb[c], dsn_b[c]) * kdec[c]) * RK_SCALE for c in chains}
        dv = {c: _dot_tn(sc[c].astype(BF16), dob[c]) + _dot((k[c] * kdec[c]).astype(BF16), dsn_b[c])
              for c in chains}
        for c in chains:
            drq_ref[rows(c)] = (dq[c] * cosf[c] + _swap_halves(dq[c] * sinf[c])).astype(BF16)
            drk_ref[rows(c)] = (dk[c] * cosf[c] + _swap_halves(dk[c] * sinf[c])).astype(BF16)
            drv_ref[rows(c)] = dv[c].astype(BF16)

    return pl.pallas_call(
        body, name="ret_bwd", grid=(nsteps,),
        in_specs=[pl.BlockSpec(memory_space=pltpu.SMEM),
                  group(0), group(1), group(2), group(3), row_tab, row_tab,
                  group(0), pl.BlockSpec((4, RET_RB, LANES, LANES), lambda s: (0, rev(s), 0, 0)),
                  group(0)],
        out_specs=[group(0)] * 4,
        out_shape=[jax.ShapeDtypeStruct((S, GROUP_W), BF16)] * 4,
        scratch_shapes=[pltpu.VMEM((4, LANES, LANES), F32), pltpu.VMEM((4, RET_T, RET_T), F32)],
        compiler_params=_cp("arbitrary"),
    )(lgam, proj, proj, proj, proj, cosf, sinf, o, states, dycat)


def outproj_fwd(x, vecs, y_ret, y_sb, w_out, head=None, tm=1024):
    S, D = x.shape
    tm = min(tm, S)
    last = list(head or ())

    n = S // tm
    ring = min(3, n)

    def body(x_hbm, v_ref, yr_ref, ys_ref, w_ref, *refs):
        refs, (x_buf, x_sems) = refs[:-2], refs[-2:]
        i = pl.program_id(0)

        def fetch(step):
            slot = step % ring
            return pltpu.make_async_copy(x_hbm.at[pl.ds(pl.multiple_of(step * tm, tm), tm)], x_buf.at[slot],
                                         x_sems.at[slot])

        @pl.when(i == 0)
        def _():
            for s in range(ring - 1):
                fetch(s).start()

        pl.when(i + ring - 1 < n)(lambda: fetch(i + ring - 1).start())
        y = _dot(yr_ref[...], w_ref[0:GROUP_W, :]) + _dot(ys_ref[...], w_ref[GROUP_W:, :])
        fetch(i).wait()
        xv = x_buf[i % ring] + v_ref[2:3, :] * y
        if not last:
            y_ref, xo_ref = refs
            y_ref[...] = y.astype(BF16)
            xo_ref[...] = xv
            return
        g_ref, t_ref, y_ref, dx_ref, st_ref = refs
        y_ref[...] = y.astype(BF16)

        @pl.when(pl.program_id(0) == 0)
        def _():
            st_ref[...] = jnp.zeros_like(st_ref)

        g = g_ref[0:1, :]
        for hh in range(2):
            rows = slice(hh * (tm // 2), (hh + 1) * (tm // 2))
            xh = xv[rows, :]
            r = lax.rsqrt(_rowmean(xh * xh) + EPS)
            xn = xh * r
            err = xn * g - t_ref[rows, :]
            dy = err * (1.0 / D)
            dxn = dy * g
            dx_ref[rows, :] = r * (dxn - xn * _rowmean(dxn * xn))
            st_ref[0:1, :] += jnp.sum(dy * xn, axis=0, keepdims=True)
            st_ref[1:2, :] += jnp.sum(err * err, axis=0, keepdims=True)

    row = lambda w: pl.BlockSpec((tm, w), lambda i: (i, 0))
    fixed = pl.BlockSpec((8, D), lambda i: (0, 0))
    return pl.pallas_call(
        body, name="outproj_fwd", grid=(n,),
        in_specs=[_ANY, fixed, row(GROUP_W), row(GROUP_W), pl.BlockSpec((D, D), lambda i: (0, 0))]
        + ([fixed, row(D)] if last else []),
        out_specs=[row(D), row(D)] + ([fixed] if last else []),
        out_shape=[jax.ShapeDtypeStruct((S, D), BF16), jax.ShapeDtypeStruct((S, D), F32)]
        + ([jax.ShapeDtypeStruct((8, D), F32)] if last else []),
        scratch_shapes=[pltpu.VMEM((ring, tm, D), F32), pltpu.SemaphoreType.DMA((ring,))],
        compiler_params=_cp("arbitrary"),
    )(x, vecs, y_ret, y_sb, w_out, *last)


def outproj_bwd(dx, y, vecs, y_ret, y_sb, w_out, tm=1024):
    S, D = dx.shape
    tm = min(tm, S)
    n = S // tm

    def body(dx_ref, y_ref, v_ref, yr_ref, ys_ref, w_ref, dyc_ref, dw_ref, st_ref, acc):
        i = pl.program_id(0)

        @pl.when(i == 0)
        def _():
            st_ref[...] = jnp.zeros_like(st_ref)
            acc[...] = jnp.zeros_like(acc)

        dxv = dx_ref[...]
        st_ref[0:1, :] += jnp.sum(dxv * y_ref[...].astype(F32), axis=0, keepdims=True)
        dyy = (dxv * v_ref[2:3, :]).astype(BF16)
        dyc_ref[...] = _dot_nt(dyy, w_ref[...])
        acc[0:GROUP_W, :] += _dot_tn(yr_ref[...], dyy)
        acc[GROUP_W:, :] += _dot_tn(ys_ref[...], dyy)

        @pl.when(i == n - 1)
        def _():
            dw_ref[...] = acc[...].astype(BF16)

    row = lambda w: pl.BlockSpec((tm, w), lambda i: (i, 0))
    fixed = lambda r: pl.BlockSpec((r, D), lambda i: (0, 0))
    return pl.pallas_call(
        body, name="outproj_bwd", grid=(n,),
        in_specs=[row(D), row(D), fixed(8), row(GROUP_W), row(GROUP_W), fixed(D)],
        out_specs=[row(D), fixed(D), fixed(8)],
        out_shape=[jax.ShapeDtypeStruct((S, D), F32), jax.ShapeDtypeStruct((D, D), BF16),
                   jax.ShapeDtypeStruct((8, D), F32)],
        scratch_shapes=[pltpu.VMEM((D, D), F32)],
        compiler_params=_cp("arbitrary"),
    )(dx, y, vecs, y_ret, y_sb, w_out)


def inproj_bwd_x(pieces, w3, x, vecs, dx_res, ship=None, tm=512):
    S, D = x.shape
    n = S // tm
    ex = _Exchange(ship)

    def body(*refs):
        p_refs, (w_ref, x_ref, v_ref, dr_ref), refs = refs[:8], refs[8:12], refs[12:]
        ship_refs, (dx_ref, st_ref), refs = refs[:ex.n_in], refs[ex.n_in:ex.n_in + 2], refs[ex.n_in + 2:]
        start, finish = ex.ops(ship_refs, refs)

        @pl.when(pl.program_id(0) == 0)
        def _():
            st_ref[...] = jnp.zeros_like(st_ref)
            start()

        g, scale1 = v_ref[3:4, :], 1.0 + v_ref[1:2, :]
        halves = [slice(hh * (tm // 2), (hh + 1) * (tm // 2)) for hh in range(2)]
        dhs = []
        for rows in halves:
            dh = jnp.zeros((tm // 2, D), F32)
            for k, p_ref in enumerate(p_refs):
                c0 = (k % 2) * GROUP_W
                dh = dh + _dot_nt(p_ref[rows, :], w_ref[k // 2, :, c0:c0 + GROUP_W])
            dhs.append(dh)
        for rows, dh in zip(halves, dhs):
            xv = x_ref[rows, :]
            r = lax.rsqrt(_rowmean(xv * xv) + EPS)
            xn = xv * r
            st_ref[0:1, :] += jnp.sum(dh, axis=0, keepdims=True)
            dh_xn = dh * xn
            st_ref[1:2, :] += jnp.sum(dh_xn, axis=0, keepdims=True) * g
            st_ref[2:3, :] += jnp.sum(dh_xn, axis=0, keepdims=True) * scale1
            dxn = dh * (g * scale1)
            dx_ref[rows, :] = r * (dxn - xn * _rowmean(dxn * xn)) + dr_ref[rows, :]
        pl.when(pl.program_id(0) == n - 1)(finish)

    row = lambda w: pl.BlockSpec((tm, w), lambda i: (i, 0))
    return pl.pallas_call(
        body, name="inproj_bwd_x", grid=(n,),
        in_specs=[row(GROUP_W)] * 8 + [pl.BlockSpec((N_SHARD, D, SHARD_W), lambda i: (0, 0, 0)),
                                       row(D), pl.BlockSpec((8, D), lambda i: (0, 0)), row(D)] + ex.in_specs,
        out_specs=[row(D), pl.BlockSpec((8, D), lambda i: (0, 0))] + ex.out_specs,
        out_shape=[jax.ShapeDtypeStruct((S, D), F32), jax.ShapeDtypeStruct((8, D), F32)] + ex.out_shape,
        scratch_shapes=ex.scratch,
        compiler_params=_cp("arbitrary"),
    )(*pieces, w3, x, vecs, dx_res, *ex.ship)


def inproj_bwd_w(h, pieces, tm=1024):
    S, D = h.shape
    tm = min(tm, S)
    n = S // tm

    def body(*refs):
        h_ref, p_refs, dw_ref, acc = refs[0], refs[1:9], refs[9], refs[10]
        i = pl.program_id(0)

        @pl.when(i == 0)
        def _():
            acc[...] = jnp.zeros_like(acc)

        hv = h_ref[...]
        for k, p_ref in enumerate(p_refs):
            c0 = (k % 2) * GROUP_W
            acc[k // 2, :, c0:c0 + GROUP_W] += _dot_tn(hv, p_ref[...])

        @pl.when(i == n - 1)
        def _():
            dw_ref[...] = acc[...].astype(BF16)

    row = lambda w: pl.BlockSpec((tm, w), lambda i: (i, 0))
    return pl.pallas_call(
        body, name="inproj_bwd_w", grid=(n,),
        in_specs=[row(D)] + [row(GROUP_W)] * 8,
        out_specs=pl.BlockSpec((N_SHARD, D, SHARD_W), lambda i: (0, 0, 0), pipeline_mode=pl.Buffered(1)),
        out_shape=jax.ShapeDtypeStruct((N_SHARD, D, SHARD_W), BF16),
        scratch_shapes=[pltpu.VMEM((N_SHARD, D, SHARD_W), F32)],
        compiler_params=_cp("arbitrary"),
    )(h, *pieces)


def layer_fwd(x, vecs, w3, w_out, tabs, gather=None, head=None):
    cosf, sinf, lgam = tabs
    ret, sg, h, sb = inproj_fwd(x, vecs, w3)
    y_ret, o_ret, states = ret_fwd(ret, cosf, sinf, lgam)
    y_sb, o_sb, sb_end, *gathered = sb_fwd(sb, sg, gather)
    if callable(w_out):
        w_out = w_out(gathered)
    y, *x_next = outproj_fwd(x, vecs, y_ret, y_sb, w_out, head)
    saved = (x, ret, sg, h, sb, y_ret, o_ret, states, y_sb, o_sb, sb_end, y)
    return (x_next[0] if head is None else x_next), saved, gathered


def _by_shard(dw_out):
    return dw_out.reshape(N_SHARD, D_MODEL // N_SHARD, D_MODEL)


def layer_bwd(dx, saved, vecs, w3, w_out, tabs, later_grads=None):
    cosf, sinf, lgam = tabs
    x, ret, sg, h, sb, y_ret, o_ret, states, y_sb, o_sb, sb_end, y = saved
    dycat, dw_out, st_o = outproj_bwd(dx, y, vecs, y_ret, y_sb, w_out)
    dw_out = _by_shard(dw_out)
    ship = None if later_grads is None else (later_grads[0], dw_out, later_grads[1])
    *d_sb, = sb_bwd(sb, sg, o_sb, sb_end, dycat, ship)
    d_ret = ret_bwd(ret, cosf, sinf, lgam, o_ret, states, dycat)
    pieces = list(d_ret) + d_sb[:4]
    dw_in = inproj_bwd_w(h, pieces)
    dx, st_i, *recv_in = inproj_bwd_x(pieces, w3, x, vecs, dx, None if later_grads is None else (dw_in,))
    dmod = jnp.concatenate([st_i[0:2], st_o[0:1]], axis=0)
    grads = (dw_in, dw_out) if later_grads is None else (recv_in[0], d_sb[4])
    return dx, dmod, st_i[2:3], grads


def _place():
    return lax.axis_index("x"), lax.axis_index("y"), lax.axis_index("c")


def _other_chips(mx, my):
    return [(1 - mx, my), (mx, 1 - my), (1 - mx, 1 - my)]


_ANY = pl.BlockSpec(memory_space=pl.ANY)


_GATHER_SCRATCH = [pltpu.SemaphoreType.DMA((7,)), pltpu.SemaphoreType.DMA((7,)), pltpu.SemaphoreType.DMA(())]


def _gather_ops(x_ref, out_ref, send_sems, recv_sems, local_sem):
    mx, my, mc = _place()
    me, sibling = (mx, my, mc), (mx, my, 1 - mc)
    chips = _other_chips(mx, my)

    def slot(px, py, pc):
        return out_ref.at[4 * px + 2 * py + pc]

    def copy(k, block, to, src=None):
        return pltpu.make_async_remote_copy(
            src_ref=slot(*block) if src is None else src, dst_ref=slot(*block),
            send_sem=send_sems.at[k], recv_sem=recv_sems.at[k], device_id=to, device_id_type=MESH)

    mine = pltpu.make_async_copy(x_ref, slot(*me), local_sem)
    first = [copy(0, me, sibling, src=x_ref)]
    first += [copy(1 + j, me, (*chip, mc), src=x_ref) for j, chip in enumerate(chips)]
    passed = [copy(4 + j, (*chip, mc), sibling) for j, chip in enumerate(chips)]

    def start():
        mine.start()
        for cp in first:
            cp.start()

    def forward():
        for j, chip in enumerate(chips):
            copy(1 + j, (*chip, mc), me).wait_recv()
            passed[j].start()

    def finish():
        copy(0, sibling, me).wait_recv()
        for j, chip in enumerate(chips):
            copy(4 + j, (*chip, 1 - mc), me).wait_recv()
        for cp in first + passed:
            cp.wait_send()
        mine.wait()

    return start, forward, finish


class _Exchange:
    def __init__(self, ship):
        self.ship = list(ship or ())
        self.n_in = len(self.ship)
        self.n_out = 1 if self.ship else 0
        self.rows = [a.shape[1] for a in self.ship]
        self.in_specs = [_ANY] * self.n_in
        self.out_specs = [_ANY] * self.n_out
        self.out_shape = [jax.ShapeDtypeStruct((N_SHARD, sum(self.rows), SHARD_W), BF16)] * self.n_out
        sem = pltpu.SemaphoreType.DMA
        self.scratch = [sem((3,)), sem((3,)), sem(())] * self.n_out

    def ops(self, ship_refs, tail):
        if not self.ship:
            return (lambda: None), (lambda: None)
        recv, send_sems, recv_sems, local_sem = tail
        mx, my, mc = _place()
        my_chip = 2 * mx + my
        chips = _other_chips(mx, my)

        def pieces(s):
            firsts = np.cumsum([0] + self.rows[:-1])
            return [(ref.at[s], int(r0), n) for ref, r0, n in zip(ship_refs, firsts, self.rows)]

        def start():
            for src, r0, n in pieces(my_chip):
                pltpu.make_async_copy(src, recv.at[my_chip, pl.ds(r0, n)], local_sem).start()
            for j, (px, py) in enumerate(chips):
                for src, r0, n in pieces(2 * px + py):
                    pltpu.make_async_remote_copy(
                        src_ref=src, dst_ref=recv.at[my_chip, pl.ds(r0, n)],
                        send_sem=send_sems.at[j], recv_sem=recv_sems.at[j],
                        device_id=(px, py, mc), device_id_type=MESH).start()

        def finish():
            for j, (px, py) in enumerate(chips):
                whole = recv.at[2 * px + py]
                both = pltpu.make_async_remote_copy(
                    src_ref=whole, dst_ref=whole, send_sem=send_sems.at[j], recv_sem=recv_sems.at[j],
                    device_id=(px, py, mc), device_id_type=MESH)
                both.wait_recv()
                both.wait_send()
            pltpu.make_async_copy(recv.at[my_chip], recv.at[my_chip], local_sem).wait()

        return start, finish


def sum_and_swap(recv_a, recv_b, stats, tr=256):
    n, rows_a, cols = recv_a.shape
    na, nb = rows_a // tr, recv_b.shape[1] // tr
    nt = na + nb

    def body(a_ref, b_ref, st_ref, own_ref, sib_ref, stall_ref, slots, send_sems, recv_sem, *gather_sems):
        i = pl.program_id(0)
        mx, my, mc = _place()
        slot = i % 2
        g_start, g_forward, g_finish = _gather_ops(st_ref, stall_ref, *gather_sems)
        pl.when(i == 0)(g_start)
        pl.when(i == nt // 2)(g_forward)

        def push(k, tile):
            return pltpu.make_async_remote_copy(
                src_ref=slots.at[k], dst_ref=sib_ref.at[pl.ds(pl.multiple_of(tile * tr, tr), tr)],
                send_sem=send_sems.at[k], recv_sem=recv_sem, device_id=(mx, my, 1 - mc), device_id_type=MESH)

        pl.when(i >= 2)(lambda: push(slot, i - 2).wait_send())

        def total(r_ref):
            acc = r_ref[0].astype(F32)
            for k in range(1, n):
                acc = acc + r_ref[k].astype(F32)
            own_ref[...] = acc
            slots[slot] = acc

        pl.when(i < na)(lambda: total(a_ref))
        pl.when(i >= na)(lambda: total(b_ref))
        push(slot, i).start()

        @pl.when(i == nt - 1)
        def _():
            push(1 - slot, i - 1).wait_send()
            push(slot, i).wait_send()
            pltpu.make_async_remote_copy(src_ref=sib_ref, dst_ref=sib_ref, send_sem=send_sems.at[0], recv_sem=recv_sem,
                                         device_id=(mx, my, 1 - mc), device_id_type=MESH).wait_recv()
            g_finish()

    return pl.pallas_call(
        body, name="sum_and_swap", grid=(nt,),
        in_specs=[pl.BlockSpec((n, tr, cols), lambda i: (0, jnp.minimum(i, na - 1), 0)),
                  pl.BlockSpec((n, tr, cols), lambda i: (0, jnp.maximum(i - na, 0), 0)), _ANY],
        out_specs=[pl.BlockSpec((tr, cols), lambda i: (i, 0)), _ANY, _ANY],
        out_shape=[jax.ShapeDtypeStruct((nt * tr, cols), F32)] * 2
        + [jax.ShapeDtypeStruct((8,) + stats.shape, stats.dtype)],
        scratch_shapes=[pltpu.VMEM((2, tr, cols), F32), pltpu.SemaphoreType.DMA((2,)), pltpu.SemaphoreType.DMA(())]
        + _GATHER_SCRATCH,
        compiler_params=_cp("arbitrary"),
    )(recv_a, recv_b, stats)


def _adamw(w, g, m, v):
    m = ADAM_B1 * m + (1.0 - ADAM_B1) * g
    v = ADAM_B2 * v + (1.0 - ADAM_B2) * (g * g)
    m_hat = m / (1.0 - ADAM_B1 ** ADAM_STEP)
    v_hat = v / (1.0 - ADAM_B2 ** ADAM_STEP)
    delta = -ADAM_LR * (m_hat / (jnp.sqrt(v_hat) + ADAM_EPS) + ADAM_WD * w)
    return delta, m, v


def adam_slab(p_own, p_sib, w, m, v, row0, name, tr=512):
    L, R, C = w.shape
    nr = R // tr

    def body(a_ref, b_ref, w_ref, m_ref, v_ref, g_out, d_out, m_out, v_out):
        g = a_ref[...] + b_ref[...]
        d, m2, v2 = _adamw(w_ref[0], g, m_ref[0], v_ref[0])
        g_out[0], d_out[0], m_out[0], v_out[0] = g, d, m2, v2

    slab = pl.BlockSpec((tr, C), lambda l, i: (row0 // tr + l * nr + i, 0))
    blk = pl.BlockSpec((1, tr, C), lambda l, i: (l, i, 0))
    return pl.pallas_call(
        body, name=name, grid=(L, nr),
        in_specs=[slab, slab, blk, blk, blk], out_specs=[blk] * 4,
        out_shape=[jax.ShapeDtypeStruct(w.shape, F32)] * 4,
        compiler_params=_cp("arbitrary", "arbitrary"),
    )(p_own, p_sib, w, m, v)


def prologue(c8, w_ada, b_ada, norm_g, win_first):
    L, D, W = w_ada.shape

    def body(c_ref, w_ref, b_ref, g_ref, win_ref, vecs_ref, call_ref, wall_ref, mod_ref, mall_ref, *sems):
        w_start, w_forward, w_finish = _gather_ops(win_ref, wall_ref, *sems[0:3])
        for step in _gather_ops(c_ref, call_ref, *sems[3:6]):
            step()
        w_start()
        cv = call_ref[:, 0, :]
        ca = cv * _sigmoid(cv)
        for l in range(L):
            mod_ref[l * 8:(l + 1) * 8, :] = jnp.dot(ca, w_ref[l], precision=lax.Precision.HIGHEST,
                                                    preferred_element_type=F32)
        for step in _gather_ops(mod_ref, mall_ref, *sems[6:9]):
            step()
        mx, my, mc = _place()
        me = 4 * mx + 2 * my + mc
        rowid = lax.broadcasted_iota(jnp.int32, (L * 8, 1), 0)
        vecs_ref[...] = jnp.zeros_like(vecs_ref)
        for l in range(L):
            parts = [jnp.sum(jnp.where(rowid == l * 8 + me, mall_ref[2 * s + mc], 0.0), axis=0, keepdims=True)
                     for s in range(N_SHARD)]
            mod = jnp.concatenate(parts, axis=1) + b_ref[l:l + 1, :]
            for t in range(3):
                vecs_ref[l, t:t + 1, :] = mod[:, t * D:(t + 1) * D]
            vecs_ref[l, 3:4, :] = g_ref[l:l + 1, :]
        w_forward()
        w_finish()

    vmem = pl.BlockSpec(memory_space=pltpu.VMEM)
    return pl.pallas_call(
        body, name="prologue",
        in_specs=[vmem, vmem, vmem, vmem, _ANY], out_specs=[vmem, vmem, _ANY],
        out_shape=[jax.ShapeDtypeStruct((L, 8, D), F32), jax.ShapeDtypeStruct((8, 8, D), F32),
                   jax.ShapeDtypeStruct((8,) + win_first.shape, win_first.dtype)],
        scratch_shapes=[pltpu.VMEM((L * 8, W), F32), pltpu.VMEM((8, L * 8, W), F32)] + _GATHER_SCRATCH * 3,
        compiler_params=pltpu.CompilerParams(vmem_limit_bytes=VMEM_LIMIT_BYTES),
    )(c8, w_ada, b_ada, norm_g, win_first)


def ada_update(dmods, c_t, w, m, v, tr=512):
    L, D, W = w.shape

    def body(dm_ref, c_ref, w_ref, m_ref, v_ref, g_out, d_out, m_out, v_out):
        mx, my, _ = _place()
        shard = 2 * mx + my
        dm = jnp.zeros((8, W), F32)
        for s in range(N_SHARD):
            dm = dm + jnp.where(shard == s, dm_ref[0, :, s * W:(s + 1) * W], 0.0)
        cv = c_ref[...]
        ca = cv * _sigmoid(cv)
        g = jnp.zeros((tr, W), F32)
        for b in range(8):
            g = g + ca[:, b:b + 1] * dm[b:b + 1, :]
        d, m2, v2 = _adamw(w_ref[0], g, m_ref[0], v_ref[0])
        g_out[0], d_out[0], m_out[0], v_out[0] = g, d, m2, v2

    blk = pl.BlockSpec((1, tr, W), lambda l, i: (l, i, 0))
    return pl.pallas_call(
        body, name="ada_update", grid=(L, D // tr),
        in_specs=[pl.BlockSpec((1, 8, 3 * D), lambda l, i: (l, 0, 0)), pl.BlockSpec((tr, 8), lambda l, i: (i, 0)),
                  blk, blk, blk],
        out_specs=[blk] * 4, out_shape=[jax.ShapeDtypeStruct(w.shape, F32)] * 4,
        compiler_params=_cp("arbitrary", "arbitrary"),
    )(dmods, c_t, w, m, v)


STAT_ROWS = 16


def small_update(stats_all, norm, b_ada, final):
    def body(s_ref, *refs):
        ins, outs = refs[:9], refs[9:]
        tot = s_ref[0]
        for k in range(1, 8):
            tot = tot + s_ref[k]
        g_norm = tot[0:2, :]
        g_final = tot[2:3, :]
        g_b = jnp.concatenate(
            [jnp.concatenate([tot[3 + 3 * l + t:4 + 3 * l + t, :] for t in range(3)], axis=1) for l in range(DEPTH)],
            axis=0)
        for p, g in enumerate((g_norm, g_b, g_final)):
            w_ref, m_ref, v_ref = ins[3 * p:3 * p + 3]
            d, m2, v2 = _adamw(w_ref[...], g, m_ref[...], v_ref[...])
            for o_ref, val in zip(outs[4 * p:4 * p + 4], (g, d, m2, v2)):
                o_ref[...] = val
        loss = (0.5 / D_MODEL) * jnp.sum(tot[9:10, :], axis=1, keepdims=True)
        outs[12][...] = jnp.broadcast_to(loss, (8, LANES))

    shapes = []
    for w, _, _ in (norm, b_ada, final):
        shapes += [jax.ShapeDtypeStruct(w.shape, F32)] * 4
    shapes.append(jax.ShapeDtypeStruct((8, LANES), F32))
    return pl.pallas_call(body, name="small_update", out_shape=shapes)(stats_all, *norm, *b_ada, *final)


def kernel(x, c, norm_g, w_ada, b_ada, w_in, w_out, final_g, loss_target, m_norm_g, m_w_ada, m_b_ada, m_w_in, m_w_out, m_final_g, v_norm_g, v_w_ada, v_b_ada, v_w_in, v_w_out, v_final_g):
    S, D = x.shape[1], x.shape[2]
    mc = lax.axis_index("c")
    out_rows = D // N_SHARD

    def my_half(a, rows):
        return lax.dynamic_slice_in_dim(a, mc * rows, rows, axis=0)

    assert DEPTH == 2
    win = [my_half(w_in[l], D // 2).astype(BF16) for l in range(DEPTH)]
    wout = [my_half(w_out[l], out_rows // 2).astype(BF16) for l in range(DEPTH)]
    rest = [jnp.concatenate(wout, axis=0), win[1]]

    def unpack(gathered):
        outs, w3_second = gathered
        outs = outs.reshape(N_SHARD, 2, DEPTH, out_rows // 2, SHARD_W)
        return outs[:, :, 0].reshape(D, D), (w3_second.reshape(N_SHARD, D, SHARD_W), outs[:, :, 1].reshape(D, D))

    vecs, c_all, w3_first = prologue(jnp.broadcast_to(c, (8, D)), w_ada, b_ada, norm_g, win[0])
    c_all, w3_first = c_all[:, 0, :], w3_first.reshape(N_SHARD, D, SHARD_W)

    tabs = (*rope_tables(S), ret_log_gamma())
    saved = [None] * DEPTH
    h, saved[0], wall = layer_fwd(x[0], vecs[0], w3_first, lambda g: unpack(g)[0], tabs, rest)
    weights = [(w3_first, unpack(wall)[0]), unpack(wall)[1]]
    head = (jnp.broadcast_to(final_g[None, :], (8, D)), loss_target[0])
    (dx, st_loss), saved[1], _ = layer_fwd(h, vecs[1], *weights[1], tabs, head=head)

    dmod, dnorm, grads = [None] * DEPTH, [None] * DEPTH, None
    for l in reversed(range(DEPTH)):
        dx, dmod[l], dnorm[l], grads = layer_bwd(dx, saved[l], vecs[l], *weights[l], tabs, grads)

    stats = jnp.concatenate(dnorm + [st_loss[0:1]] + dmod + [st_loss[1:2], jnp.zeros((STAT_ROWS - 10, D), F32)], axis=0)
    p_own, p_sib, stats_all = sum_and_swap(*grads, stats)
    res_in = adam_slab(p_own, p_sib, w_in, m_w_in, v_w_in, 0, "adam_w_in")
    res_out = adam_slab(p_own, p_sib, w_out, m_w_out, v_w_out, DEPTH * D, "adam_w_out", tr=256)

    dmods = stats_all[:, 3:9, :].reshape(8, DEPTH, 3 * D).transpose(1, 0, 2)
    res_ada = ada_update(dmods, c_all.T, w_ada, m_w_ada, v_w_ada)
    small = small_update(stats_all, (norm_g, m_norm_g, v_norm_g), (b_ada, m_b_ada, v_b_ada),
                         (final_g[None, :], m_final_g[None, :], v_final_g[None, :]))
    res_norm, res_b, res_final = small[0:4], small[4:8], [a[0] for a in small[8:12]]
    loss = small[12][0, 0]

    by_kind = [res_norm, res_ada, res_b, res_in, res_out, res_final]
    outs = [loss, dx[None]]
    for kind in range(4):
        outs += [r[kind] for r in by_kind]
    return tuple(outs)
```

```python
import numpy as np
import jax
import jax.numpy as jnp
from jax import lax
from jax.experimental import pallas as pl
from jax.experimental.pallas import tpu as pltpu

F32, BF16 = jnp.float32, jnp.bfloat16
MESH = pl.DeviceIdType.MESH

D_MODEL = 1024
DEPTH = 2
SHARD_W = 1024
N_SHARD = 4
GROUP_W = 512
LANES = 128
SB_HEAD_DIM = 64
RET_HEAD_DIM = 128
CHUNK = 64
ROPE_BASE = 10000.0
EPS = 1e-6
SQ_SCALE = SB_HEAD_DIM ** -0.5
RK_SCALE = RET_HEAD_DIM ** -0.5
SB_T = 1024
SB_CHAINS = 16
SB_NB = 4
RET_T = 256
EXP_ZERO = -104.0
MASKED_SCORE = -1e4
VMEM_LIMIT_BYTES = 56 * 2 ** 20

ADAM_LR, ADAM_B1, ADAM_B2, ADAM_EPS, ADAM_WD, ADAM_STEP = 0.001, 0.9, 0.999, 1e-08, 0.01, 10


def _cp(*sem):
    return pltpu.CompilerParams(dimension_semantics=sem, vmem_limit_bytes=VMEM_LIMIT_BYTES)


def _dot(a, b):
    return lax.dot_general(a, b, (((1,), (0,)), ((), ())), preferred_element_type=F32)


def _dot_nt(a, b):
    return lax.dot_general(a, b, (((1,), (1,)), ((), ())), preferred_element_type=F32)


def _dot_tn(a, b):
    return lax.dot_general(a, b, (((0,), (0,)), ((), ())), preferred_element_type=F32)


def _running_sum(a, tri):
    return _dot(a.astype(BF16), tri)


def _sigmoid(x):
    return 1.0 / (1.0 + jnp.exp(-x))


def _rowsum(a):
    return jnp.sum(a, axis=1, keepdims=True)


def _rowmean(a):
    return jnp.mean(a, axis=1, keepdims=True)


def inproj_fwd(x, vecs, w3, tm=512):
    S, D = x.shape

    def body(x_ref, v_ref, w_ref, ret_ref, sg_ref, h_ref, sb_ref):
        xv = x_ref[...]
        r = lax.rsqrt(_rowmean(xv * xv) + EPS)
        h = xv * r * v_ref[3:4, :] * (1.0 + v_ref[1:2, :]) + v_ref[0:1, :]
        hb = h.astype(BF16)
        h_ref[...] = hb
        for s in range(N_SHARD):
            p = _dot(hb, w_ref[s])
            if s < 2:
                ret_ref[:, s * SHARD_W:(s + 1) * SHARD_W] = p
            if s == 2:
                sb_ref[:, 0:GROUP_W] = (p[:, 0:GROUP_W] * SQ_SCALE).astype(BF16)
                sb_ref[:, GROUP_W:SHARD_W] = p[:, GROUP_W:].astype(BF16)
            if s == 3:
                sb_ref[:, SHARD_W:SHARD_W + GROUP_W] = p[:, 0:GROUP_W].astype(BF16)
                sg_ref[...] = p[:, GROUP_W:]

    row = lambda w: pl.BlockSpec((tm, w), lambda i: (i, 0))
    return pl.pallas_call(
        body, name="inproj_fwd", grid=(S // tm,),
        in_specs=[row(D), pl.BlockSpec((8, D), lambda i: (0, 0)),
                  pl.BlockSpec((N_SHARD, D, SHARD_W), lambda i: (0, 0, 0))],
        out_specs=[row(2 * SHARD_W), row(GROUP_W), row(D), row(3 * GROUP_W)],
        out_shape=[jax.ShapeDtypeStruct((S, 2 * SHARD_W), F32), jax.ShapeDtypeStruct((S, GROUP_W), F32),
                   jax.ShapeDtypeStruct((S, D), BF16), jax.ShapeDtypeStruct((S, 3 * GROUP_W), BF16)],
        compiler_params=_cp("arbitrary"),
    )(x, vecs, w3)


def _sb_logits(qh, k2, keep):
    z = _dot_nt(qh, k2)
    if keep is not None:
        z = jnp.where(keep, z, MASKED_SCORE)
    sp = jnp.log(1.0 + jnp.exp(-jnp.abs(z)))
    lb = jnp.minimum(z, 0.0) - sp
    return lb, lb - z


class _sb_chains:
    def __init__(self, i, q2, do_b=None):
        t = self.t = SB_T // SB_CHAINS
        self.C = range(SB_CHAINS)
        r = lax.broadcasted_iota(jnp.int32, (SB_NB * t, SB_NB * t), 0)
        c = lax.broadcasted_iota(jnp.int32, (SB_NB * t, SB_NB * t), 1)
        self.later_all = jnp.where(r > c, 1.0, 0.0).astype(BF16)
        self.earlier_all = jnp.where(r < c, 1.0, 0.0).astype(BF16)
        self.later, self.earlier = self.later_all[:t, :t], self.earlier_all[:t, :t]
        self.head0 = lax.broadcasted_iota(jnp.int32, (1, LANES), 1) < SB_HEAD_DIM
        row = lax.broadcasted_iota(jnp.int32, (2 * t, SB_NB * t), 0) & (t - 1)
        col = lax.broadcasted_iota(jnp.int32, (2 * t, SB_NB * t), 1)
        qt = [SB_CHAINS * i + cc for cc in self.C]
        self.first = [jnp.maximum(qt[cc] - (SB_NB - 1), 0) for cc in self.C]
        self._ahead, self._qt = col - row, qt
        self.qs = [self._stack(q2[cc * t:(cc + 1) * t]) for cc in self.C]
        if do_b is not None:
            self.dos = [self._stack(do_b[cc * t:(cc + 1) * t]) for cc in self.C]

    def keep_of(self, cc):
        return self._ahead < (self._qt[cc] - self.first[cc]) * self.t

    def _stack(self, a):
        zero = jnp.zeros_like(a)
        return jnp.concatenate([jnp.where(self.head0, a, zero), jnp.where(self.head0, zero, a)], axis=0)

    def rows(self, ref, j, n):
        return ref[pl.ds(pl.multiple_of(j * self.t, self.t), n * self.t), :]

    def suffix(self, lk):
        return _running_sum(lk, self.later_all), _rowsum(lk)

    def prefix(self, g, G0):
        return _running_sum(g, self.earlier_all) + G0


def sb_fwd(sb, sg, gather=None):
    S = sb.shape[0]
    T = SB_T
    nq = S // T
    carried = list(gather or ())
    ng = len(carried)

    def body(*refs):
        (q_ref, k_ref, v_ref, sg_ref), refs = refs[:4], refs[4:]
        x_refs, (y_ref, o_ref, end_ref), out_refs, sems = refs[:ng], refs[ng:ng + 3], refs[ng + 3:2 * ng + 3], refs[2 * ng + 3:]
        p, i = pl.program_id(0), pl.program_id(1)
        gathers = [_gather_ops(x_refs[g], out_refs[g], *sems[3 * g:3 * g + 3]) for g in range(ng)]
        for start, forward, _ in gathers:
            pl.when(jnp.logical_and(p == 0, i == 0))(start)
            pl.when(jnp.logical_and(p == 3, i == 0))(forward)
        ch = _sb_chains(i, q_ref[...])
        later, head0 = ch.later, ch.head0
        lbk = [_sb_logits(ch.qs[c], ch.rows(k_ref, ch.first[c], SB_NB), ch.keep_of(c)) for c in ch.C]
        suffix, R = zip(*[ch.suffix(lbk[c][1]) for c in ch.C])
        aa = [jnp.exp(lbk[c][0] + suffix[c]) for c in ch.C]
        acc = [_dot(aa[c].astype(BF16), ch.rows(v_ref, ch.first[c], SB_NB)) for c in ch.C]

        nc = len(ch.C)

        def alive(n, Rs):
            m = None
            for c in ch.C:
                rc = jnp.where(ch.first[c] - n > 0, Rs[c], EXP_ZERO)
                m = rc if m is None else jnp.maximum(m, rc)
            return jnp.max(m)

        def cond(st):
            return st[-1] > EXP_ZERO

        def step(st):
            n, accs, Rs = st[0], list(st[1:1 + nc]), list(st[1 + nc:1 + 2 * nc])
            for c in ch.C:
                j = ch.first[c] - 1 - n
                jc = jnp.maximum(j, 0)
                lb, lk = _sb_logits(ch.qs[c], ch.rows(k_ref, jc, 1), None)
                a = jnp.exp(lb + _running_sum(lk, later) + Rs[c])
                cx = _dot(a.astype(BF16), ch.rows(v_ref, jc, 1))
                accs[c] = jnp.where(j >= 0, accs[c] + cx, accs[c])
                Rs[c] = jnp.where(j >= 0, Rs[c] + _rowsum(lk), Rs[c])
            return (n + 1, *accs, *Rs, alive(n + 1, Rs))

        st = lax.while_loop(cond, step, (jnp.int32(0), *acc, *R, alive(0, R)))
        n_end, acc, R = st[0], st[1:1 + nc], st[1 + nc:1 + 2 * nc]
        outs = []
        for c in ch.C:
            base = c * (2 * ch.t + 8)
            end_ref[0, 0, base:base + 2 * ch.t, :] = jnp.broadcast_to(R[c], (2 * ch.t, 8))
            end_ref[0, 0, base + 2 * ch.t:base + 2 * ch.t + 8, :] = jnp.full((8, 8), n_end.astype(F32))
            outs.append(jnp.where(head0, acc[c][:ch.t], acc[c][ch.t:]))
        o = jnp.concatenate(outs, axis=0)
        o_ref[...] = o
        sg = sg_ref[...]
        y_ref[...] = (o * (sg * _sigmoid(sg))).astype(BF16)
        for _, _, finish in gathers:
            pl.when(jnp.logical_and(p == 3, i == nq - 1))(finish)

    return pl.pallas_call(
        body, name="sb_fwd", grid=(4, nq),
        in_specs=[pl.BlockSpec((T, LANES), lambda p, i: (i, p)),
                  pl.BlockSpec((S, LANES), lambda p, i: (0, 4 + p)),
                  pl.BlockSpec((S, LANES), lambda p, i: (0, 8 + p)),
                  pl.BlockSpec((T, LANES), lambda p, i: (i, p))] + [_ANY for _ in carried],
        out_specs=[pl.BlockSpec((T, LANES), lambda p, i: (i, p)),
                   pl.BlockSpec((T, LANES), lambda p, i: (i, p)),
                   pl.BlockSpec((1, 1, SB_CHAINS * (2 * T // SB_CHAINS + 8), 8), lambda p, i: (p, i, 0, 0))] + [_ANY for _ in carried],
        out_shape=[jax.ShapeDtypeStruct((S, GROUP_W), BF16),
                   jax.ShapeDtypeStruct((S, GROUP_W), F32),
                   jax.ShapeDtypeStruct((4, nq, SB_CHAINS * (2 * T // SB_CHAINS + 8), 8), F32)]
        + [jax.ShapeDtypeStruct((8,) + a.shape, a.dtype) for a in carried],
        scratch_shapes=_GATHER_SCRATCH * ng,
        compiler_params=_cp("arbitrary", "arbitrary"),
    )(sb, sb, sb, sg, *carried)


def sb_bwd(sb, sg, o, sb_end, dycat, ship=None):
    S = sb.shape[0]
    T = SB_T
    nq = S // T
    ex = _Exchange(ship)

    def body(*refs):
        (q_ref, k_ref, v_ref, sg_ref, o_ref, dy_ref, end_ref), refs = refs[:7], refs[7:]
        ship_refs, (dq_ref, dk_ref, dv_ref, dsg_ref), refs = refs[:ex.n_in], refs[ex.n_in:ex.n_in + 4], refs[ex.n_in + 4:]
        recv, (dk_acc, dv_acc), sems = refs[:ex.n_out], refs[ex.n_out:ex.n_out + 2], refs[ex.n_out + 2:]
        start, finish = ex.ops(ship_refs, recv + sems)
        p, i = pl.program_id(0), pl.program_id(1)
        pl.when(jnp.logical_and(p == 0, i == 0))(start)

        @pl.when(i == 0)
        def _():
            dk_acc[...] = jnp.zeros_like(dk_acc)
            dv_acc[...] = jnp.zeros_like(dv_acc)

        sg = sg_ref[...]
        sig = _sigmoid(sg)
        dy = dy_ref[...]
        dsg_ref[...] = (dy * o_ref[...] * (sig * (1.0 + sg * (1.0 - sig)))).astype(BF16)
        do_b = (dy * (sg * sig)).astype(BF16)
        ch = _sb_chains(i, q_ref[...], do_b)
        later, earlier, head0, t = ch.later, ch.earlier, ch.head0, ch.t
        end = end_ref[0, 0]

        def grads(c, j, n, a, lb, g, G, keep):
            dz = g - jnp.exp(lb) * (g + G)
            if keep is not None:
                dz = jnp.where(keep, dz, 0.0)
            dzb = dz.astype(BF16)
            rows = pl.ds(pl.multiple_of(j * t, t), n * t)
            dk_acc[rows, :] += _dot_tn(dzb, ch.qs[c])
            dv_acc[rows, :] += _dot_tn(a.astype(BF16), ch.dos[c])
            return _dot(dzb, ch.rows(k_ref, j, n))

        nc = len(ch.C)
        n_end = jnp.max(end[2 * t:2 * t + 8, :]).astype(jnp.int32)

        def sweep(m, st):
            dqs, G0s, lefts = list(st[:nc]), list(st[nc:2 * nc]), list(st[2 * nc:])
            for c in ch.C:
                j = ch.first[c] - n_end + m
                jc = jnp.maximum(j, 0)
                lb, lk = _sb_logits(ch.qs[c], ch.rows(k_ref, jc, 1), None)
                stick = lefts[c] - _rowsum(lk)
                a = jnp.where(j >= 0, jnp.exp(lb + _running_sum(lk, later) + stick), 0.0)
                g = a * _dot_nt(ch.dos[c], ch.rows(v_ref, jc, 1))
                G = _running_sum(g, earlier) + G0s[c]
                dqs[c] = dqs[c] + grads(c, jc, 1, a, lb, jnp.where(j >= 0, g, 0.0), jnp.where(j >= 0, G, 0.0), None)
                G0s[c] = G0s[c] + _rowsum(g)
                lefts[c] = jnp.where(j >= 0, stick, lefts[c])
            return (*dqs, *G0s, *lefts)

        lefts = [end[c * (2 * t + 8):c * (2 * t + 8) + 2 * t, 0:1] for c in ch.C]
        st = lax.fori_loop(0, n_end, sweep, (*[jnp.zeros((2 * t, LANES), F32)] * nc,
                                             *[jnp.zeros((2 * t, 1), F32)] * nc, *lefts))
        dq, G0 = st[:nc], st[nc:2 * nc]

        lbk = [_sb_logits(ch.qs[c], ch.rows(k_ref, ch.first[c], SB_NB), ch.keep_of(c)) for c in ch.C]
        suffix = [ch.suffix(lbk[c][1])[0] for c in ch.C]
        aa = [jnp.exp(lbk[c][0] + suffix[c]) for c in ch.C]
        g = [aa[c] * _dot_nt(ch.dos[c], ch.rows(v_ref, ch.first[c], SB_NB)) for c in ch.C]
        G = [ch.prefix(g[c], G0[c]) for c in ch.C]
        for c in ch.C:
            dqc = dq[c] + grads(c, ch.first[c], SB_NB, aa[c], lbk[c][0], g[c], G[c], None)
            dq_ref[c * t:(c + 1) * t, :] = (jnp.where(head0, dqc[:t], dqc[t:]) * SQ_SCALE).astype(BF16)

        @pl.when(i == nq - 1)
        def _():
            dk_ref[...] = dk_acc[...].astype(BF16)
            dv_ref[...] = dv_acc[...].astype(BF16)

        pl.when(jnp.logical_and(p == 3, i == nq - 1))(finish)

    tile_spec = lambda c0: pl.BlockSpec((T, LANES), lambda p, i: (i, c0 + p))
    head_spec = lambda c0: pl.BlockSpec((S, LANES), lambda p, i: (0, c0 + p))
    return pl.pallas_call(
        body, name="sb_bwd", grid=(4, nq),
        in_specs=[tile_spec(0), head_spec(4), head_spec(8), tile_spec(0), tile_spec(0), tile_spec(4),
                  pl.BlockSpec((1, 1, SB_CHAINS * (2 * T // SB_CHAINS + 8), 8), lambda p, i: (p, i, 0, 0))] + ex.in_specs,
        out_specs=[tile_spec(0), head_spec(0), head_spec(0), tile_spec(0)] + ex.out_specs,
        out_shape=[jax.ShapeDtypeStruct((S, GROUP_W), BF16)] * 4 + ex.out_shape,
        scratch_shapes=[pltpu.VMEM((S, LANES), F32), pltpu.VMEM((S, LANES), F32)] + ex.scratch,
        compiler_params=_cp("arbitrary", "arbitrary"),
    )(sb, sb, sb, sg, o, dycat, sb_end, *ex.ship)


def rope_tables(S):
    half = RET_HEAD_DIM // 2
    lane = jnp.arange(RET_HEAD_DIM)
    inv = ROPE_BASE ** (-(lane % half).astype(F32) / half)
    ang = jnp.arange(S, dtype=F32)[:, None] * inv[None, :]
    return jnp.cos(ang), jnp.where(lane < half, -1.0, 1.0)[None, :] * jnp.sin(ang)


def ret_log_gamma():
    return jnp.log1p(-(2.0 ** (-5.0 - jnp.arange(4, dtype=F32))))


def _swap_halves(a):
    return pltpu.roll(a, RET_HEAD_DIM // 2, axis=1)


def _ret_decay_mask(lg):
    n = lax.broadcasted_iota(jnp.int32, (RET_T, RET_T), 0)
    m = lax.broadcasted_iota(jnp.int32, (RET_T, RET_T), 1)
    dist = jnp.abs(n - m).astype(F32)
    return jnp.where((m // CHUNK) <= (n // CHUNK), jnp.exp(lg * dist), 0.0)


def _ret_block(lg, rq, rk, rv, cosf, sinf, dm):
    q = rq * cosf + _swap_halves(rq) * sinf
    k = (rk * cosf + _swap_halves(rk) * sinf) * RK_SCALE
    qb, kb, vb = q.astype(BF16), k.astype(BF16), rv.astype(BF16)
    sc = _dot_nt(qb, kb) * dm
    nloc = lax.broadcasted_iota(jnp.int32, (RET_T, 1), 0).astype(F32)
    qdec = jnp.exp(lg * (nloc + 1.0))
    kdec = jnp.exp(lg * (RET_T - 1.0 - nloc))
    block_dec = jnp.exp(jnp.full((1, LANES), lg * RET_T, F32))
    return q, k, qb, kb, vb, sc, qdec, kdec, block_dec


RET_RB = 4


def _ret_specs(S, rb):
    group = lambda c0: pl.BlockSpec((RET_RB * RET_T, GROUP_W), lambda s: (rb(s), c0))
    return group, pl.BlockSpec((RET_RB * RET_T, LANES), lambda s: (rb(s), 0))


def _ret_chains():
    chains = [(h, b) for b in range(RET_RB) for h in range(4)]
    rows = lambda c: (slice(c[1] * RET_T, (c[1] + 1) * RET_T), slice(c[0] * LANES, (c[0] + 1) * LANES))
    tab = lambda ref, c: ref[c[1] * RET_T:(c[1] + 1) * RET_T, :]
    return chains, rows, tab


def _ret_blocks(chains, rows, lg_ref, rq_ref, rk_ref, rv_ref, cosf, sinf, dm_ref):
    blk = {c: _ret_block(lg_ref[c[0]], rq_ref[rows(c)], rk_ref[rows(c)], rv_ref[rows(c)],
                         cosf[c], sinf[c], dm_ref[c[0]]) for c in chains}
    return ({c: blk[c][n] for c in chains} for n in range(9))


def ret_fwd(proj, cosf, sinf, lgam):
    S = proj.shape[0]
    nb = S // RET_T
    group, row_tab = _ret_specs(S, lambda s: s)

    def body(lg_ref, rq_ref, rk_ref, rv_ref, rg_ref, cos_ref, sin_ref, y_ref, o_ref, st_out, st_ref, dm_ref):
        @pl.when(pl.program_id(0) == 0)
        def _():
            st_ref[...] = jnp.zeros_like(st_ref)
            for h in range(4):
                dm_ref[h] = _ret_decay_mask(lg_ref[h])

        chains, rows, tab = _ret_chains()
        cosf, sinf = {c: tab(cos_ref, c) for c in chains}, {c: tab(sin_ref, c) for c in chains}
        q, k, qb, kb, vb, sc, qdec, kdec, block_dec = _ret_blocks(
            chains, rows, lg_ref, rq_ref, rk_ref, rv_ref, cosf, sinf, dm_ref)
        kv = {c: _dot_tn((k[c] * kdec[c]).astype(BF16), vb[c]) for c in chains}
        st = {(h, 0): st_ref[h] for h in range(4)}
        for b in range(RET_RB):
            for h in range(4):
                st[(h, b + 1)] = st[(h, b)] * block_dec[(h, b)] + kv[(h, b)]
        for h, b in chains:
            st_out[h, b] = st[(h, b)]
        for h in range(4):
            st_ref[h] = st[(h, RET_RB)]
        o = {c: _dot(sc[c].astype(BF16), vb[c]) + _dot(qb[c], st[c].astype(BF16)) * qdec[c] for c in chains}
        for c in chains:
            o_ref[rows(c)] = o[c]
        cen = {c: o[c] - _rowmean(o[c]) for c in chains}
        on = {c: cen[c] * lax.rsqrt(_rowmean(cen[c] * cen[c]) + EPS) for c in chains}
        rg = {c: rg_ref[rows(c)] for c in chains}
        for c in chains:
            y_ref[rows(c)] = (on[c] * (rg[c] * _sigmoid(rg[c]))).astype(BF16)

    return pl.pallas_call(
        body, name="ret_fwd", grid=(nb // RET_RB,),
        in_specs=[pl.BlockSpec(memory_space=pltpu.SMEM),
                  group(0), group(1), group(2), group(3), row_tab, row_tab],
        out_specs=[group(0), group(0),
                   pl.BlockSpec((4, RET_RB, LANES, LANES), lambda s: (0, s, 0, 0))],
        out_shape=[jax.ShapeDtypeStruct((S, GROUP_W), BF16),
                   jax.ShapeDtypeStruct((S, GROUP_W), F32),
                   jax.ShapeDtypeStruct((4, nb, LANES, LANES), F32)],
        scratch_shapes=[pltpu.VMEM((4, LANES, LANES), F32), pltpu.VMEM((4, RET_T, RET_T), F32)],
        compiler_params=_cp("arbitrary"),
    )(lgam, proj, proj, proj, proj, cosf, sinf)


def ret_bwd(proj, cosf, sinf, lgam, o, states, dycat):
    S = proj.shape[0]
    nsteps = S // RET_T // RET_RB
    rev = lambda s: nsteps - 1 - s
    group, row_tab = _ret_specs(S, rev)

    def body(lg_ref, rq_ref, rk_ref, rv_ref, rg_ref, cos_ref, sin_ref, o_ref, st_in, dy_ref,
             drq_ref, drk_ref, drv_ref, drg_ref, ds_ref, dm_ref):
        @pl.when(pl.program_id(0) == 0)
        def _():
            ds_ref[...] = jnp.zeros_like(ds_ref)
            for h in range(4):
                dm_ref[h] = _ret_decay_mask(lg_ref[h])

        chains, rows, tab = _ret_chains()
        cosf, sinf = {c: tab(cos_ref, c) for c in chains}, {c: tab(sin_ref, c) for c in chains}
        dms = {c: dm_ref[c[0]] for c in chains}
        q, k, qb, kb, vb, sc, qdec, kdec, block_dec = _ret_blocks(
            chains, rows, lg_ref, rq_ref, rk_ref, rv_ref, cosf, sinf, dm_ref)
        o_v = {c: o_ref[rows(c)] for c in chains}
        cen = {c: o_v[c] - _rowmean(o_v[c]) for c in chains}
        rstd = {c: lax.rsqrt(_rowmean(cen[c] * cen[c]) + EPS) for c in chains}
        on = {c: cen[c] * rstd[c] for c in chains}
        rg = {c: rg_ref[rows(c)] for c in chains}
        sig = {c: _sigmoid(rg[c]) for c in chains}
        dy = {c: dy_ref[rows(c)] for c in chains}
        for c in chains:
            drg_ref[rows(c)] = (dy[c] * on[c] * (sig[c] * (1.0 + rg[c] * (1.0 - sig[c])))).astype(BF16)
        don = {c: dy[c] * (rg[c] * sig[c]) for c in chains}
        do = {c: rstd[c] * (don[c] - _rowmean(don[c]) - on[c] * _rowmean(don[c] * on[c])) for c in chains}
        dob = {c: do[c].astype(BF16) for c in chains}
        dsc = {c: (_dot_nt(dob[c], vb[c]) * dms[c]).astype(BF16) for c in chains}
        st_b = {c: st_in[c[0], c[1]].astype(BF16) for c in chains}
        dst = {c: _dot_tn((q[c] * qdec[c]).astype(BF16), dob[c]) for c in chains}
        dsn = {(h, RET_RB): ds_ref[h] for h in range(4)}
        for b in reversed(range(RET_RB)):
            for h in range(4):
                dsn[(h, b)] = dsn[(h, b + 1)] * block_dec[(h, b)] + dst[(h, b)]
        for h in range(4):
            ds_ref[h] = dsn[(h, 0)]
        dsn_b = {c: dsn[(c[0], c[1] + 1)].astype(BF16) for c in chains}
        dq = {c: _dot(dsc[c], kb[c]) + _dot_nt(dob[c], st_b[c]) * qdec[c] for c in chains}
        dk = {c: (_dot_tn(dsc[c], qb[c]) + _dot_nt(vb[c], dsn_b[c]) * kdec[c]) * RK_SCALE for c in chains}
        dv = {c: _dot_tn(sc[c].astype(BF16), dob[c]) + _dot((k[c] * kdec[c]).astype(BF16), dsn_b[c])
              for c in chains}
        for c in chains:
            drq_ref[rows(c)] = (dq[c] * cosf[c] + _swap_halves(dq[c] * sinf[c])).astype(BF16)
            drk_ref[rows(c)] = (dk[c] * cosf[c] + _swap_halves(dk[c] * sinf[c])).astype(BF16)
            drv_ref[rows(c)] = dv[c].astype(BF16)

    return pl.pallas_call(
        body, name="ret_bwd", grid=(nsteps,),
        in_specs=[pl.BlockSpec(memory_space=pltpu.SMEM),
                  group(0), group(1), group(2), group(3), row_tab, row_tab,
                  group(0), pl.BlockSpec((4, RET_RB, LANES, LANES), lambda s: (0, rev(s), 0, 0)),
                  group(0)],
        out_specs=[group(0)] * 4,
        out_shape=[jax.ShapeDtypeStruct((S, GROUP_W), BF16)] * 4,
        scratch_shapes=[pltpu.VMEM((4, LANES, LANES), F32), pltpu.VMEM((4, RET_T, RET_T), F32)],
        compiler_params=_cp("arbitrary"),
    )(lgam, proj, proj, proj, proj, cosf, sinf, o, states, dycat)


def outproj_fwd(x, vecs, y_ret, y_sb, w_out, head=None, tm=1024):
    S, D = x.shape
    tm = min(tm, S)
    last = list(head or ())

    n = S // tm
    ring = min(3, n)

    def body(x_hbm, v_ref, yr_ref, ys_ref, w_ref, *refs):
        refs, (x_buf, x_sems) = refs[:-2], refs[-2:]
        i = pl.program_id(0)

        def fetch(step):
            slot = step % ring
            return pltpu.make_async_copy(x_hbm.at[pl.ds(pl.multiple_of(step * tm, tm), tm)], x_buf.at[slot],
                                         x_sems.at[slot])

        @pl.when(i == 0)
        def _():
            for s in range(ring - 1):
                fetch(s).start()

        pl.when(i + ring - 1 < n)(lambda: fetch(i + ring - 1).start())
        y = _dot(yr_ref[...], w_ref[0:GROUP_W, :]) + _dot(ys_ref[...], w_ref[GROUP_W:, :])
        fetch(i).wait()
        xv = x_buf[i % ring] + v_ref[2:3, :] * y
        if not last:
            y_ref, xo_ref = refs
            y_ref[...] = y.astype(BF16)
            xo_ref[...] = xv
            return
        g_ref, t_ref, y_ref, dx_ref, st_ref = refs
        y_ref[...] = y.astype(BF16)

        @pl.when(pl.program_id(0) == 0)
        def _():
            st_ref[...] = jnp.zeros_like(st_ref)

        g = g_ref[0:1, :]
        for hh in range(2):
            rows = slice(hh * (tm // 2), (hh + 1) * (tm // 2))
            xh = xv[rows, :]
            r = lax.rsqrt(_rowmean(xh * xh) + EPS)
            xn = xh * r
            err = xn * g - t_ref[rows, :]
            dy = err * (1.0 / D)
            dxn = dy * g
            dx_ref[rows, :] = r * (dxn - xn * _rowmean(dxn * xn))
            st_ref[0:1, :] += jnp.sum(dy * xn, axis=0, keepdims=True)
            st_ref[1:2, :] += jnp.sum(err * err, axis=0, keepdims=True)

    row = lambda w: pl.BlockSpec((tm, w), lambda i: (i, 0))
    fixed = pl.BlockSpec((8, D), lambda i: (0, 0))
    return pl.pallas_call(
        body, name="outproj_fwd", grid=(n,),
        in_specs=[_ANY, fixed, row(GROUP_W), row(GROUP_W), pl.BlockSpec((D, D), lambda i: (0, 0))]
        + ([fixed, row(D)] if last else []),
        out_specs=[row(D), row(D)] + ([fixed] if last else []),
        out_shape=[jax.ShapeDtypeStruct((S, D), BF16), jax.ShapeDtypeStruct((S, D), F32)]
        + ([jax.ShapeDtypeStruct((8, D), F32)] if last else []),
        scratch_shapes=[pltpu.VMEM((ring, tm, D), F32), pltpu.SemaphoreType.DMA((ring,))],
        compiler_params=_cp("arbitrary"),
    )(x, vecs, y_ret, y_sb, w_out, *last)


def outproj_bwd(dx, y, vecs, y_ret, y_sb, w_out, tm=1024):
    S, D = dx.shape
    tm = min(tm, S)
    n = S // tm

    ring = min(3, n)

    def body(dx_hbm, y_ref, v_ref, yr_ref, ys_ref, w_ref, dyc_ref, dw_ref, st_ref, acc, dx_buf, dx_sems):
        i = pl.program_id(0)

        def fetch(step):
            slot = step % ring
            return pltpu.make_async_copy(dx_hbm.at[pl.ds(pl.multiple_of(step * tm, tm), tm)], dx_buf.at[slot],
                                         dx_sems.at[slot])

        @pl.when(i == 0)
        def _():
            st_ref[...] = jnp.zeros_like(st_ref)
            acc[...] = jnp.zeros_like(acc)
            for s in range(ring - 1):
                fetch(s).start()

        pl.when(i + ring - 1 < n)(lambda: fetch(i + ring - 1).start())
        fetch(i).wait()
        dxv = dx_buf[i % ring]
        st_ref[0:1, :] += jnp.sum(dxv * y_ref[...].astype(F32), axis=0, keepdims=True)
        dyy = (dxv * v_ref[2:3, :]).astype(BF16)
        dyc_ref[...] = _dot_nt(dyy, w_ref[...])
        acc[0:GROUP_W, :] += _dot_tn(yr_ref[...], dyy)
        acc[GROUP_W:, :] += _dot_tn(ys_ref[...], dyy)

        @pl.when(i == n - 1)
        def _():
            dw_ref[...] = acc[...].astype(BF16)

    row = lambda w: pl.BlockSpec((tm, w), lambda i: (i, 0))
    fixed = lambda r: pl.BlockSpec((r, D), lambda i: (0, 0))
    return pl.pallas_call(
        body, name="outproj_bwd", grid=(n,),
        in_specs=[_ANY, row(D), fixed(8), row(GROUP_W), row(GROUP_W), fixed(D)],
        out_specs=[row(D), fixed(D), fixed(8)],
        out_shape=[jax.ShapeDtypeStruct((S, D), F32), jax.ShapeDtypeStruct((D, D), BF16),
                   jax.ShapeDtypeStruct((8, D), F32)],
        scratch_shapes=[pltpu.VMEM((D, D), F32), pltpu.VMEM((ring, tm, D), F32), pltpu.SemaphoreType.DMA((ring,))],
        compiler_params=_cp("arbitrary"),
    )(dx, y, vecs, y_ret, y_sb, w_out)


def inproj_bwd_x(pieces, w3, x, vecs, dx_res, ship=None, tm=512):
    S, D = x.shape
    n = S // tm
    ex = _Exchange(ship)

    def body(*refs):
        p_refs, (w_ref, x_ref, v_ref, dr_ref), refs = refs[:8], refs[8:12], refs[12:]
        ship_refs, (dx_ref, st_ref), refs = refs[:ex.n_in], refs[ex.n_in:ex.n_in + 2], refs[ex.n_in + 2:]
        start, finish = ex.ops(ship_refs, refs)

        @pl.when(pl.program_id(0) == 0)
        def _():
            st_ref[...] = jnp.zeros_like(st_ref)
            start()

        g, scale1 = v_ref[3:4, :], 1.0 + v_ref[1:2, :]
        halves = [slice(hh * (tm // 2), (hh + 1) * (tm // 2)) for hh in range(2)]
        dhs = []
        for rows in halves:
            dh = jnp.zeros((tm // 2, D), F32)
            for k, p_ref in enumerate(p_refs):
                c0 = (k % 2) * GROUP_W
                dh = dh + _dot_nt(p_ref[rows, :], w_ref[k // 2, :, c0:c0 + GROUP_W])
            dhs.append(dh)
        for rows, dh in zip(halves, dhs):
            xv = x_ref[rows, :]
            r = lax.rsqrt(_rowmean(xv * xv) + EPS)
            xn = xv * r
            st_ref[0:1, :] += jnp.sum(dh, axis=0, keepdims=True)
            dh_xn = dh * xn
            st_ref[1:2, :] += jnp.sum(dh_xn, axis=0, keepdims=True) * g
            st_ref[2:3, :] += jnp.sum(dh_xn, axis=0, keepdims=True) * scale1
            dxn = dh * (g * scale1)
            dx_ref[rows, :] = r * (dxn - xn * _rowmean(dxn * xn)) + dr_ref[rows, :]
        pl.when(pl.program_id(0) == n - 1)(finish)

    row = lambda w: pl.BlockSpec((tm, w), lambda i: (i, 0))
    return pl.pallas_call(
        body, name="inproj_bwd_x", grid=(n,),
        in_specs=[row(GROUP_W)] * 8 + [pl.BlockSpec((N_SHARD, D, SHARD_W), lambda i: (0, 0, 0)),
                                       row(D), pl.BlockSpec((8, D), lambda i: (0, 0)), row(D)] + ex.in_specs,
        out_specs=[row(D), pl.BlockSpec((8, D), lambda i: (0, 0))] + ex.out_specs,
        out_shape=[jax.ShapeDtypeStruct((S, D), F32), jax.ShapeDtypeStruct((8, D), F32)] + ex.out_shape,
        scratch_shapes=ex.scratch,
        compiler_params=_cp("arbitrary"),
    )(*pieces, w3, x, vecs, dx_res, *ex.ship)


def inproj_bwd_w(h, pieces, tm=1024):
    S, D = h.shape
    tm = min(tm, S)
    n = S // tm

    def body(*refs):
        h_ref, p_refs, dw_ref, acc = refs[0], refs[1:9], refs[9], refs[10]
        i = pl.program_id(0)

        @pl.when(i == 0)
        def _():
            acc[...] = jnp.zeros_like(acc)

        hv = h_ref[...]
        for k, p_ref in enumerate(p_refs):
            c0 = (k % 2) * GROUP_W
            acc[k // 2, :, c0:c0 + GROUP_W] += _dot_tn(hv, p_ref[...])

        @pl.when(i == n - 1)
        def _():
            dw_ref[...] = acc[...].astype(BF16)

    row = lambda w: pl.BlockSpec((tm, w), lambda i: (i, 0))
    return pl.pallas_call(
        body, name="inproj_bwd_w", grid=(n,),
        in_specs=[row(D)] + [row(GROUP_W)] * 8,
        out_specs=pl.BlockSpec((N_SHARD, D, SHARD_W), lambda i: (0, 0, 0), pipeline_mode=pl.Buffered(1)),
        out_shape=jax.ShapeDtypeStruct((N_SHARD, D, SHARD_W), BF16),
        scratch_shapes=[pltpu.VMEM((N_SHARD, D, SHARD_W), F32)],
        compiler_params=_cp("arbitrary"),
    )(h, *pieces)


def layer_fwd(x, vecs, w3, w_out, tabs, gather=None, head=None):
    cosf, sinf, lgam = tabs
    ret, sg, h, sb = inproj_fwd(x, vecs, w3)
    y_ret, o_ret, states = ret_fwd(ret, cosf, sinf, lgam)
    y_sb, o_sb, sb_end, *gathered = sb_fwd(sb, sg, gather)
    if callable(w_out):
        w_out = w_out(gathered)
    y, *x_next = outproj_fwd(x, vecs, y_ret, y_sb, w_out, head)
    saved = (x, ret, sg, h, sb, y_ret, o_ret, states, y_sb, o_sb, sb_end, y)
    return (x_next[0] if head is None else x_next), saved, gathered


def _by_shard(dw_out):
    return dw_out.reshape(N_SHARD, D_MODEL // N_SHARD, D_MODEL)


def layer_bwd(dx, saved, vecs, w3, w_out, tabs, later_grads=None):
    cosf, sinf, lgam = tabs
    x, ret, sg, h, sb, y_ret, o_ret, states, y_sb, o_sb, sb_end, y = saved
    dycat, dw_out, st_o = outproj_bwd(dx, y, vecs, y_ret, y_sb, w_out)
    dw_out = _by_shard(dw_out)
    ship = None if later_grads is None else (later_grads[0], dw_out, later_grads[1])
    *d_sb, = sb_bwd(sb, sg, o_sb, sb_end, dycat, ship)
    d_ret = ret_bwd(ret, cosf, sinf, lgam, o_ret, states, dycat)
    pieces = list(d_ret) + d_sb[:4]
    dw_in = inproj_bwd_w(h, pieces)
    dx, st_i, *recv_in = inproj_bwd_x(pieces, w3, x, vecs, dx, None if later_grads is None else (dw_in,))
    dmod = jnp.concatenate([st_i[0:2], st_o[0:1]], axis=0)
    grads = (dw_in, dw_out) if later_grads is None else (recv_in[0], d_sb[4])
    return dx, dmod, st_i[2:3], grads


def _place():
    return lax.axis_index("x"), lax.axis_index("y"), lax.axis_index("c")


def _other_chips(mx, my):
    return [(1 - mx, my), (mx, 1 - my), (1 - mx, 1 - my)]


_ANY = pl.BlockSpec(memory_space=pl.ANY)


_GATHER_SCRATCH = [pltpu.SemaphoreType.DMA((7,)), pltpu.SemaphoreType.DMA((7,)), pltpu.SemaphoreType.DMA(())]


def _gather_ops(x_ref, out_ref, send_sems, recv_sems, local_sem):
    mx, my, mc = _place()
    me, sibling = (mx, my, mc), (mx, my, 1 - mc)
    chips = _other_chips(mx, my)

    def slot(px, py, pc):
        return out_ref.at[4 * px + 2 * py + pc]

    def copy(k, block, to, src=None):
        return pltpu.make_async_remote_copy(
            src_ref=slot(*block) if src is None else src, dst_ref=slot(*block),
            send_sem=send_sems.at[k], recv_sem=recv_sems.at[k], device_id=to, device_id_type=MESH)

    mine = pltpu.make_async_copy(x_ref, slot(*me), local_sem)
    first = [copy(0, me, sibling, src=x_ref)]
    first += [copy(1 + j, me, (*chip, mc), src=x_ref) for j, chip in enumerate(chips)]
    passed = [copy(4 + j, (*chip, mc), sibling) for j, chip in enumerate(chips)]

    def start():
        mine.start()
        for cp in first:
            cp.start()

    def forward():
        for j, chip in enumerate(chips):
            copy(1 + j, (*chip, mc), me).wait_recv()
            passed[j].start()

    def finish():
        copy(0, sibling, me).wait_recv()
        for j, chip in enumerate(chips):
            copy(4 + j, (*chip, 1 - mc), me).wait_recv()
        for cp in first + passed:
            cp.wait_send()
        mine.wait()

    return start, forward, finish


class _Exchange:
    def __init__(self, ship):
        self.ship = list(ship or ())
        self.n_in = len(self.ship)
        self.n_out = 1 if self.ship else 0
        self.rows = [a.shape[1] for a in self.ship]
        self.in_specs = [_ANY] * self.n_in
        self.out_specs = [_ANY] * self.n_out
        self.out_shape = [jax.ShapeDtypeStruct((N_SHARD, sum(self.rows), SHARD_W), BF16)] * self.n_out
        sem = pltpu.SemaphoreType.DMA
        self.scratch = [sem((3,)), sem((3,)), sem(())] * self.n_out

    def ops(self, ship_refs, tail):
        if not self.ship:
            return (lambda: None), (lambda: None)
        recv, send_sems, recv_sems, local_sem = tail
        mx, my, mc = _place()
        my_chip = 2 * mx + my
        chips = _other_chips(mx, my)

        def pieces(s):
            firsts = np.cumsum([0] + self.rows[:-1])
            return [(ref.at[s], int(r0), n) for ref, r0, n in zip(ship_refs, firsts, self.rows)]

        def start():
            for src, r0, n in pieces(my_chip):
                pltpu.make_async_copy(src, recv.at[my_chip, pl.ds(r0, n)], local_sem).start()
            for j, (px, py) in enumerate(chips):
                for src, r0, n in pieces(2 * px + py):
                    pltpu.make_async_remote_copy(
                        src_ref=src, dst_ref=recv.at[my_chip, pl.ds(r0, n)],
                        send_sem=send_sems.at[j], recv_sem=recv_sems.at[j],
                        device_id=(px, py, mc), device_id_type=MESH).start()

        def finish():
            for j, (px, py) in enumerate(chips):
                whole = recv.at[2 * px + py]
                both = pltpu.make_async_remote_copy(
                    src_ref=whole, dst_ref=whole, send_sem=send_sems.at[j], recv_sem=recv_sems.at[j],
                    device_id=(px, py, mc), device_id_type=MESH)
                both.wait_recv()
                both.wait_send()
            pltpu.make_async_copy(recv.at[my_chip], recv.at[my_chip], local_sem).wait()

        return start, finish


def sum_and_swap(recv_a, recv_b, stats, tr=256):
    n, rows_a, cols = recv_a.shape
    na, nb = rows_a // tr, recv_b.shape[1] // tr
    nt = na + nb

    def body(a_ref, b_ref, st_ref, own_ref, sib_ref, stall_ref, slots, send_sems, recv_sem, *gather_sems):
        i = pl.program_id(0)
        mx, my, mc = _place()
        slot = i % 2
        g_start, g_forward, g_finish = _gather_ops(st_ref, stall_ref, *gather_sems)
        pl.when(i == 0)(g_start)
        pl.when(i == nt // 2)(g_forward)

        def push(k, tile):
            return pltpu.make_async_remote_copy(
                src_ref=slots.at[k], dst_ref=sib_ref.at[pl.ds(pl.multiple_of(tile * tr, tr), tr)],
                send_sem=send_sems.at[k], recv_sem=recv_sem, device_id=(mx, my, 1 - mc), device_id_type=MESH)

        pl.when(i >= 2)(lambda: push(slot, i - 2).wait_send())

        def total(r_ref):
            acc = r_ref[0].astype(F32)
            for k in range(1, n):
                acc = acc + r_ref[k].astype(F32)
            own_ref[...] = acc
            slots[slot] = acc

        pl.when(i < na)(lambda: total(a_ref))
        pl.when(i >= na)(lambda: total(b_ref))
        push(slot, i).start()

        @pl.when(i == nt - 1)
        def _():
            push(1 - slot, i - 1).wait_send()
            push(slot, i).wait_send()
            pltpu.make_async_remote_copy(src_ref=sib_ref, dst_ref=sib_ref, send_sem=send_sems.at[0], recv_sem=recv_sem,
                                         device_id=(mx, my, 1 - mc), device_id_type=MESH).wait_recv()
            g_finish()

    return pl.pallas_call(
        body, name="sum_and_swap", grid=(nt,),
        in_specs=[pl.BlockSpec((n, tr, cols), lambda i: (0, jnp.minimum(i, na - 1), 0)),
                  pl.BlockSpec((n, tr, cols), lambda i: (0, jnp.maximum(i - na, 0), 0)), _ANY],
        out_specs=[pl.BlockSpec((tr, cols), lambda i: (i, 0)), _ANY, _ANY],
        out_shape=[jax.ShapeDtypeStruct((nt * tr, cols), F32)] * 2
        + [jax.ShapeDtypeStruct((8,) + stats.shape, stats.dtype)],
        scratch_shapes=[pltpu.VMEM((2, tr, cols), F32), pltpu.SemaphoreType.DMA((2,)), pltpu.SemaphoreType.DMA(())]
        + _GATHER_SCRATCH,
        compiler_params=_cp("arbitrary"),
    )(recv_a, recv_b, stats)


def _adamw(w, g, m, v):
    m = ADAM_B1 * m + (1.0 - ADAM_B1) * g
    v = ADAM_B2 * v + (1.0 - ADAM_B2) * (g * g)
    m_hat = m / (1.0 - ADAM_B1 ** ADAM_STEP)
    v_hat = v / (1.0 - ADAM_B2 ** ADAM_STEP)
    delta = -ADAM_LR * (m_hat / (jnp.sqrt(v_hat) + ADAM_EPS) + ADAM_WD * w)
    return delta, m, v


def adam_slab(p_own, p_sib, w, m, v, row0, name, tr=512):
    L, R, C = w.shape
    nr = R // tr

    def body(a_ref, b_ref, w_ref, m_ref, v_ref, g_out, d_out, m_out, v_out):
        g = a_ref[...] + b_ref[...]
        d, m2, v2 = _adamw(w_ref[0], g, m_ref[0], v_ref[0])
        g_out[0], d_out[0], m_out[0], v_out[0] = g, d, m2, v2

    slab = pl.BlockSpec((tr, C), lambda l, i: (row0 // tr + l * nr + i, 0))
    blk = pl.BlockSpec((1, tr, C), lambda l, i: (l, i, 0))
    return pl.pallas_call(
        body, name=name, grid=(L, nr),
        in_specs=[slab, slab, blk, blk, blk], out_specs=[blk] * 4,
        out_shape=[jax.ShapeDtypeStruct(w.shape, F32)] * 4,
        compiler_params=_cp("arbitrary", "arbitrary"),
    )(p_own, p_sib, w, m, v)


def prologue(c8, w_ada, b_ada, norm_g, win_first):
    L, D, W = w_ada.shape

    def body(c_ref, w_ref, b_ref, g_ref, win_ref, vecs_ref, call_ref, wall_ref, mod_ref, mall_ref, *sems):
        w_start, w_forward, w_finish = _gather_ops(win_ref, wall_ref, *sems[0:3])
        for step in _gather_ops(c_ref, call_ref, *sems[3:6]):
            step()
        w_start()
        cv = call_ref[:, 0, :]
        ca = cv * _sigmoid(cv)
        for l in range(L):
            mod_ref[l * 8:(l + 1) * 8, :] = jnp.dot(ca, w_ref[l], precision=lax.Precision.HIGHEST,
                                                    preferred_element_type=F32)
        for step in _gather_ops(mod_ref, mall_ref, *sems[6:9]):
            step()
        mx, my, mc = _place()
        me = 4 * mx + 2 * my + mc
        rowid = lax.broadcasted_iota(jnp.int32, (L * 8, 1), 0)
        vecs_ref[...] = jnp.zeros_like(vecs_ref)
        for l in range(L):
            parts = [jnp.sum(jnp.where(rowid == l * 8 + me, mall_ref[2 * s + mc], 0.0), axis=0, keepdims=True)
                     for s in range(N_SHARD)]
            mod = jnp.concatenate(parts, axis=1) + b_ref[l:l + 1, :]
            for t in range(3):
                vecs_ref[l, t:t + 1, :] = mod[:, t * D:(t + 1) * D]
            vecs_ref[l, 3:4, :] = g_ref[l:l + 1, :]
        w_forward()
        w_finish()

    vmem = pl.BlockSpec(memory_space=pltpu.VMEM)
    return pl.pallas_call(
        body, name="prologue",
        in_specs=[vmem, vmem, vmem, vmem, _ANY], out_specs=[vmem, vmem, _ANY],
        out_shape=[jax.ShapeDtypeStruct((L, 8, D), F32), jax.ShapeDtypeStruct((8, 8, D), F32),
                   jax.ShapeDtypeStruct((8,) + win_first.shape, win_first.dtype)],
        scratch_shapes=[pltpu.VMEM((L * 8, W), F32), pltpu.VMEM((8, L * 8, W), F32)] + _GATHER_SCRATCH * 3,
        compiler_params=pltpu.CompilerParams(vmem_limit_bytes=VMEM_LIMIT_BYTES),
    )(c8, w_ada, b_ada, norm_g, win_first)


def ada_update(dmods, c_t, w, m, v, tr=512):
    L, D, W = w.shape

    def body(dm_ref, c_ref, w_ref, m_ref, v_ref, g_out, d_out, m_out, v_out):
        mx, my, _ = _place()
        shard = 2 * mx + my
        dm = jnp.zeros((8, W), F32)
        for s in range(N_SHARD):
            dm = dm + jnp.where(shard == s, dm_ref[0, :, s * W:(s + 1) * W], 0.0)
        cv = c_ref[...]
        ca = cv * _sigmoid(cv)
        g = jnp.zeros((tr, W), F32)
        for b in range(8):
            g = g + ca[:, b:b + 1] * dm[b:b + 1, :]
        d, m2, v2 = _adamw(w_ref[0], g, m_ref[0], v_ref[0])
        g_out[0], d_out[0], m_out[0], v_out[0] = g, d, m2, v2

    blk = pl.BlockSpec((1, tr, W), lambda l, i: (l, i, 0))
    return pl.pallas_call(
        body, name="ada_update", grid=(L, D // tr),
        in_specs=[pl.BlockSpec((1, 8, 3 * D), lambda l, i: (l, 0, 0)), pl.BlockSpec((tr, 8), lambda l, i: (i, 0)),
                  blk, blk, blk],
        out_specs=[blk] * 4, out_shape=[jax.ShapeDtypeStruct(w.shape, F32)] * 4,
        compiler_params=_cp("arbitrary", "arbitrary"),
    )(dmods, c_t, w, m, v)


STAT_ROWS = 16


def small_update(stats_all, norm, b_ada, final):
    def body(s_ref, *refs):
        ins, outs = refs[:9], refs[9:]
        tot = s_ref[0]
        for k in range(1, 8):
            tot = tot + s_ref[k]
        g_norm = tot[0:2, :]
        g_final = tot[2:3, :]
        g_b = jnp.concatenate(
            [jnp.concatenate([tot[3 + 3 * l + t:4 + 3 * l + t, :] for t in range(3)], axis=1) for l in range(DEPTH)],
            axis=0)
        for p, g in enumerate((g_norm, g_b, g_final)):
            w_ref, m_ref, v_ref = ins[3 * p:3 * p + 3]
            d, m2, v2 = _adamw(w_ref[...], g, m_ref[...], v_ref[...])
            for o_ref, val in zip(outs[4 * p:4 * p + 4], (g, d, m2, v2)):
                o_ref[...] = val
        loss = (0.5 / D_MODEL) * jnp.sum(tot[9:10, :], axis=1, keepdims=True)
        outs[12][...] = jnp.broadcast_to(loss, (8, LANES))

    shapes = []
    for w, _, _ in (norm, b_ada, final):
        shapes += [jax.ShapeDtypeStruct(w.shape, F32)] * 4
    shapes.append(jax.ShapeDtypeStruct((8, LANES), F32))
    return pl.pallas_call(body, name="small_update", out_shape=shapes)(stats_all, *norm, *b_ada, *final)


def kernel(x, c, norm_g, w_ada, b_ada, w_in, w_out, final_g, loss_target, m_norm_g, m_w_ada, m_b_ada, m_w_in, m_w_out, m_final_g, v_norm_g, v_w_ada, v_b_ada, v_w_in, v_w_out, v_final_g):
    S, D = x.shape[1], x.shape[2]
    mc = lax.axis_index("c")
    out_rows = D // N_SHARD

    def my_half(a, rows):
        return lax.dynamic_slice_in_dim(a, mc * rows, rows, axis=0)

    assert DEPTH == 2
    win = [my_half(w_in[l], D // 2).astype(BF16) for l in range(DEPTH)]
    wout = [my_half(w_out[l], out_rows // 2).astype(BF16) for l in range(DEPTH)]
    rest = [jnp.concatenate(wout, axis=0), win[1]]

    def unpack(gathered):
        outs, w3_second = gathered
        outs = outs.reshape(N_SHARD, 2, DEPTH, out_rows // 2, SHARD_W)
        return outs[:, :, 0].reshape(D, D), (w3_second.reshape(N_SHARD, D, SHARD_W), outs[:, :, 1].reshape(D, D))

    vecs, c_all, w3_first = prologue(jnp.broadcast_to(c, (8, D)), w_ada, b_ada, norm_g, win[0])
    c_all, w3_first = c_all[:, 0, :], w3_first.reshape(N_SHARD, D, SHARD_W)

    tabs = (*rope_tables(S), ret_log_gamma())
    saved = [None] * DEPTH
    h, saved[0], wall = layer_fwd(x[0], vecs[0], w3_first, lambda g: unpack(g)[0], tabs, rest)
    weights = [(w3_first, unpack(wall)[0]), unpack(wall)[1]]
    head = (jnp.broadcast_to(final_g[None, :], (8, D)), loss_target[0])
    (dx, st_loss), saved[1], _ = layer_fwd(h, vecs[1], *weights[1], tabs, head=head)

    dmod, dnorm, grads = [None] * DEPTH, [None] * DEPTH, None
    for l in reversed(range(DEPTH)):
        dx, dmod[l], dnorm[l], grads = layer_bwd(dx, saved[l], vecs[l], *weights[l], tabs, grads)

    stats = jnp.concatenate(dnorm + [st_loss[0:1]] + dmod + [st_loss[1:2], jnp.zeros((STAT_ROWS - 10, D), F32)], axis=0)
    p_own, p_sib, stats_all = sum_and_swap(*grads, stats)
    res_in = adam_slab(p_own, p_sib, w_in, m_w_in, v_w_in, 0, "adam_w_in")
    res_out = adam_slab(p_own, p_sib, w_out, m_w_out, v_w_out, DEPTH * D, "adam_w_out", tr=256)

    dmods = stats_all[:, 3:9, :].reshape(8, DEPTH, 3 * D).transpose(1, 0, 2)
    res_ada = ada_update(dmods, c_all.T, w_ada, m_w_ada, v_w_ada)
    small = small_update(stats_all, (norm_g, m_norm_g, v_norm_g), (b_ada, m_b_ada, v_b_ada),
                         (final_g[None, :], m_final_g[None, :], v_final_g[None, :]))
    res_norm, res_b, res_final = small[0:4], small[4:8], [a[0] for a in small[8:12]]
    loss = small[12][0, 0]

    by_kind = [res_norm, res_ada, res_b, res_in, res_out, res_final]
    outs = [loss, dx[None]]
    for kind in range(4):
        outs += [r[kind] for r in by_kind]
    return tuple(outs)
```
